```python
import math
import numpy as np
import jax
import jax.numpy as jnp
from jax import lax

D_MODEL = 1024
BATCH = 8
SEQ = 2048
DEPTH = 2

HEAD_DIM = 64
N_MIX_HEADS = 4
GROUP_W = N_MIX_HEADS * HEAD_DIM
MIX_W = 4 * GROUP_W
ROPE_THETA = 10000.0
EPS = 1e-6
Q_BLOCK = 128
NEG = -1e30
BIG = 1e9

DIL_PATTERNS = ((128, 1), (512, 4), (2048, 16))

MLA_Q_RANK = 256
MLA_KV_RANK = 128
MLA_NOPE = 64
MLA_ROPE = 32
MLA_V = 64

NSA_KV_DIM = 64
NSA_CMP_LEN = 32
NSA_CMP_STRIDE = 16
NSA_CMP_HID = 256
NSA_SEL_LEN = 64
NSA_N_SEL = 16
NSA_WINDOW = 512
SEL_Q_BLOCK = 64

IN_WIDTHS = (
    GROUP_W, GROUP_W, GROUP_W, GROUP_W,
    MLA_Q_RANK, MLA_KV_RANK, MLA_ROPE, GROUP_W,
    GROUP_W, NSA_KV_DIM, NSA_KV_DIM, NSA_KV_DIM, NSA_KV_DIM,
    NSA_KV_DIM, NSA_KV_DIM, 3 * N_MIX_HEADS, GROUP_W,
    GROUP_W, GROUP_W, GROUP_W, GROUP_W,
)
D_IN = sum(IN_WIDTHS)

kernel_name = "hybrid_parallel_heads_dilated_mla_nsa_stickbreak"


def rms_norm(x, g):
    xf = x.astype(jnp.float32)
    y = xf * lax.rsqrt(jnp.mean(xf * xf, axis=-1, keepdims=True) + EPS)
    return (y * g.astype(jnp.float32)).astype(x.dtype)


def rope(x, pos):
    half = x.shape[-1] // 2
    inv = ROPE_THETA ** (-jnp.arange(half, dtype=jnp.float32) / half)
    ang = pos.astype(jnp.float32)[:, None, :, None] * inv
    cos, sin = jnp.cos(ang), jnp.sin(ang)
    xf = x.astype(jnp.float32)
    x1, x2 = xf[..., :half], xf[..., half:]
    return jnp.concatenate([x1 * cos - x2 * sin, x1 * sin + x2 * cos], -1).astype(x.dtype)


def to_heads(t, n):
    b, s, _ = t.shape
    return t.reshape(b, s, n, -1).transpose(0, 2, 1, 3)


def from_heads(t):
    b, h, s, d = t.shape
    return t.transpose(0, 2, 1, 3).reshape(b, s, h * d)


def banded_attention(q, k, v, max_dist, block, scale):
    L = q.shape[-2]
    lead = q.shape[:-2]
    nb = -(-L // block)
    lp = nb * block
    n_prev = -(-max_dist // block)
    nz = [(0, 0)] * len(lead)
    qb = jnp.pad(q, nz + [(0, lp - L), (0, 0)]).reshape(*lead, nb, block, q.shape[-1])
    kr = jnp.pad(k, nz + [(n_prev * block, lp - L), (0, 0)]).reshape(*lead, nb + n_prev, block, k.shape[-1])
    vr = jnp.pad(v, nz + [(n_prev * block, lp - L), (0, 0)]).reshape(*lead, nb + n_prev, block, v.shape[-1])
    kb = jnp.concatenate([kr[..., o:o + nb, :, :] for o in range(n_prev + 1)], axis=-2)
    vb = jnp.concatenate([vr[..., o:o + nb, :, :] for o in range(n_prev + 1)], axis=-2)
    s = jnp.einsum('...nqd,...nkd->...nqk', qb, kb).astype(jnp.float32) * scale
    qpos = (jnp.arange(nb)[:, None] * block + jnp.arange(block)[None, :])[:, :, None]
    kpos = ((jnp.arange(nb)[:, None] - n_prev) * block
            + jnp.arange((n_prev + 1) * block)[None, :])[:, None, :]
    dist = qpos - kpos
    ok = (dist >= 0) & (dist <= max_dist) & (kpos >= 0)
    s = jnp.where(ok, s, NEG)
    m = jnp.max(s, axis=-1)
    p = jnp.where(ok, jnp.exp(s - m[..., None]), 0.0)
    l = jnp.sum(p, axis=-1)
    o = jnp.einsum('...nqk,...nkd->...nqd', p, vb.astype(jnp.float32)) / l[..., None]
    o = o.reshape(*lead, lp, v.shape[-1])[..., :L, :]
    return o, m.reshape(*lead, lp)[..., :L], l.reshape(*lead, lp)[..., :L]


def dilated_mixture_attention(q, k, v):
    b, h, s, d = q.shape
    outs, ms, ls = [], [], []
    for window, dil in DIL_PATTERNS:
        def regroup(t):
            return t.reshape(b, h, s // dil, dil, d).swapaxes(2, 3)
        o, m, l = banded_attention(regroup(q), regroup(k), regroup(v), window // dil, Q_BLOCK, d ** -0.5)
        outs.append(o.swapaxes(2, 3).reshape(b, h, s, d))
        ms.append(m.swapaxes(2, 3).reshape(b, h, s))
        ls.append(l.swapaxes(2, 3).reshape(b, h, s))
    m_all = jnp.stack(ms)
    wts = jnp.stack(ls) * jnp.exp(m_all - jnp.max(m_all, axis=0, keepdims=True))
    o = jnp.sum(wts[..., None] * jnp.stack(outs), axis=0) / jnp.sum(wts, axis=0)[..., None]
    return o.astype(q.dtype)


def blocked_causal_attention(q, k, v, scale):
    b, h, s, dq = q.shape
    nb = s // Q_BLOCK
    qb = q.reshape(b, h, nb, Q_BLOCK, dq).transpose(2, 0, 1, 3, 4)
    kpos = jnp.arange(s)

    def one(args):
        qi, blk = args
        sc = jnp.einsum('bhqd,bhkd->bhqk', qi, k).astype(jnp.float32) * scale
        qpos = blk * Q_BLOCK + jnp.arange(Q_BLOCK)
        sc = jnp.where(kpos[None, :] <= qpos[:, None], sc, NEG)
        p = jax.nn.softmax(sc, axis=-1)
        return jnp.einsum('bhqk,bhkd->bhqd', p, v.astype(jnp.float32))

    o = lax.map(one, (qb, jnp.arange(nb)))
    return o.transpose(1, 2, 0, 3, 4).reshape(b, h, s, v.shape[-1]).astype(q.dtype)


def mla_attention(c_q, c_kv, k_rope_in, pos, g_q, g_kv, w_uq, w_ukv):
    b, s, _ = c_q.shape
    q = to_heads(rms_norm(c_q, g_q) @ w_uq, N_MIX_HEADS)
    q = jnp.concatenate([q[..., :MLA_NOPE], rope(q[..., MLA_NOPE:], pos)], -1)
    kv = to_heads(rms_norm(c_kv, g_kv) @ w_ukv, N_MIX_HEADS)
    k_r = rope(k_rope_in[:, None], pos)
    k = jnp.concatenate([kv[..., :MLA_NOPE], jnp.broadcast_to(k_r, (b, N_MIX_HEADS, s, MLA_ROPE))], -1)
    v = kv[..., MLA_NOPE:]
    return blocked_causal_attention(q, k, v, (MLA_NOPE + MLA_ROPE) ** -0.5)


def compress_blocks(x, pos_emb, w1, w2):
    b, s, d = x.shape
    n = (s - NSA_CMP_LEN) // NSA_CMP_STRIDE + 1
    idx = jnp.arange(n)[:, None] * NSA_CMP_STRIDE + jnp.arange(NSA_CMP_LEN)[None, :]
    blocks = x[:, idx, :] + pos_emb
    return jax.nn.silu(blocks.reshape(b, n, NSA_CMP_LEN * d) @ w1) @ w2


def selected_block_attention(q, k, v, sel_idx, scale):
    b, h, s, d = q.shape
    n_top = sel_idx.shape[-1]
    kb = k.reshape(b, s // NSA_SEL_LEN, NSA_SEL_LEN, d)
    vb = v.reshape(b, s // NSA_SEL_LEN, NSA_SEL_LEN, d)
    nq = s // SEL_Q_BLOCK
    qb = q.reshape(b, h, nq, SEL_Q_BLOCK, d).transpose(2, 0, 1, 3, 4)
    ib = sel_idx.reshape(b, nq, SEL_Q_BLOCK, n_top).transpose(1, 0, 2, 3)
    gather = jax.vmap(lambda blocks, ix: blocks[ix])

    def one(args):
        qi, idx, blk = args
        kg = gather(kb, idx).reshape(b, SEL_Q_BLOCK, n_top * NSA_SEL_LEN, d)
        vg = gather(vb, idx).reshape(b, SEL_Q_BLOCK, n_top * NSA_SEL_LEN, d)
        kpos = (idx[..., None] * NSA_SEL_LEN + jnp.arange(NSA_SEL_LEN)).reshape(b, SEL_Q_BLOCK, -1)
        qpos = blk * SEL_Q_BLOCK + jnp.arange(SEL_Q_BLOCK)
        ok = kpos <= qpos[None, :, None]
        sc = jnp.einsum('bhqd,bqkd->bhqk', qi, kg).astype(jnp.float32) * scale
        p = jax.nn.softmax(jnp.where(ok[:, None], sc, NEG), axis=-1)
        return jnp.einsum('bhqk,bqkd->bhqd', p, vg.astype(jnp.float32))

    o = lax.map(one, (qb, ib, jnp.arange(nq)))
    return o.transpose(1, 2, 0, 3, 4).reshape(b, h, s, d)


def nsa_attention(q, k_cmp, v_cmp, k_slc, v_slc, k_win, v_win, gate_logits,
                  pos_k, pos_v, kw1, kw2, vw1, vw2):
    b, h, s, d = q.shape
    scale = d ** -0.5
    t = jnp.arange(s)
    kc = compress_blocks(k_cmp, pos_k, kw1, kw2)
    vc = compress_blocks(v_cmp, pos_v, vw1, vw2)
    n_cmp = kc.shape[1]
    cmp_ok = (jnp.arange(n_cmp) * NSA_CMP_STRIDE + NSA_CMP_LEN - 1)[None, :] <= t[:, None]
    sc = jnp.where(cmp_ok, jnp.einsum('bhtd,bnd->bhtn', q, kc).astype(jnp.float32) * scale, NEG)
    e = jnp.where(cmp_ok, jnp.exp(sc - jnp.max(sc, axis=-1, keepdims=True)), 0.0)
    den = jnp.sum(e, axis=-1, keepdims=True)
    p_cmp = e / jnp.maximum(den, 1e-30)
    o_cmp = jnp.einsum('bhtn,bnd->bhtd', p_cmp, vc.astype(jnp.float32))
    n_slc = s // NSA_SEL_LEN
    ci = jnp.arange(n_cmp)[:, None] * NSA_CMP_STRIDE
    sj = jnp.arange(n_slc)[None, :] * NSA_SEL_LEN
    overlap = ((ci < sj + NSA_SEL_LEN) & (ci + NSA_CMP_LEN > sj)).astype(jnp.float32)
    imp = jnp.einsum('bhtn,nj->btj', p_cmp, overlap)
    jj = jnp.arange(n_slc)[None, :]
    bt = (t // NSA_SEL_LEN)[:, None]
    forced = (jj == 0) | (jj == bt) | (jj == bt - 1)
    imp = jnp.where(forced, BIG, jnp.where(jj > bt, -BIG, imp))
    _, sel_idx = lax.top_k(imp, min(NSA_N_SEL, n_slc))
    o_slc = selected_block_attention(q, k_slc, v_slc, sel_idx, scale)
    kw = jnp.broadcast_to(k_win[:, None], (b, h, s, d))
    vw = jnp.broadcast_to(v_win[:, None], (b, h, s, d))
    o_win, _, _ = banded_attention(q, kw, vw, NSA_WINDOW - 1, Q_BLOCK, scale)
    g = jax.nn.sigmoid(gate_logits.astype(jnp.float32)).reshape(b, s, h, 3).transpose(0, 2, 1, 3)
    return (g[..., 0:1] * o_cmp + g[..., 1:2] * o_slc + g[..., 2:3] * o_win).astype(q.dtype)


def stick_breaking_attention(q, k, v):
    b, h, s, d = q.shape
    nb = s // Q_BLOCK
    qb = q.reshape(b, h, nb, Q_BLOCK, d).transpose(2, 0, 1, 3, 4)
    kpos = jnp.arange(s)

    def one(args):
        qi, blk = args
        z = jnp.einsum('bhqd,bhkd->bhqk', qi, k).astype(jnp.float32) * d ** -0.5
        qpos = blk * Q_BLOCK + jnp.arange(Q_BLOCK)
        strict = kpos[None, :] < qpos[:, None]
        log_keep = jnp.where(strict, jax.nn.log_sigmoid(-z), 0.0)
        rev = lax.cumsum(log_keep, axis=3, reverse=True)
        after = jnp.concatenate([rev[..., 1:], jnp.zeros_like(rev[..., :1])], axis=-1)
        a = jnp.where(strict, jnp.exp(jax.nn.log_sigmoid(z) + after), 0.0)
        return jnp.einsum('bhqk,bhkd->bhqd', a, v.astype(jnp.float32))

    o = lax.map(one, (qb, jnp.arange(nb)))
    return o.transpose(1, 2, 0, 3, 4).reshape(b, h, s, d).astype(q.dtype)


def hybrid_layer(x, pos, w_in, w_out, g_pre, g_post, mla_g_q, mla_g_kv, mla_w_uq, mla_w_ukv,
                 nsa_pos_k, nsa_pos_v, nsa_k_w1, nsa_k_w2, nsa_v_w1, nsa_v_w2):
    h = rms_norm(x, g_pre)
    proj = h @ w_in
    points = np.cumsum(IN_WIDTHS)[:-1].tolist()
    (a_q, a_k, a_v, a_g,
     b_cq, b_ckv, b_kr, b_g,
     c_q, c_kc, c_vc, c_ks, c_vs, c_kw, c_vw, c_gl, c_g,
     d_q, d_k, d_v, d_g) = jnp.split(proj, points, axis=-1)
    o_a = dilated_mixture_attention(rope(to_heads(a_q, N_MIX_HEADS), pos),
                                    rope(to_heads(a_k, N_MIX_HEADS), pos),
                                    to_heads(a_v, N_MIX_HEADS))
    o_b = mla_attention(b_cq, b_ckv, b_kr, pos, mla_g_q, mla_g_kv, mla_w_uq, mla_w_ukv)
    rope_k = lambda t: rope(t[:, None], pos)[:, 0]
    o_c = nsa_attention(rope(to_heads(c_q, N_MIX_HEADS), pos), rope_k(c_kc), c_vc, rope_k(c_ks), c_vs,
                        rope_k(c_kw), c_vw, c_gl, nsa_pos_k, nsa_pos_v,
                        nsa_k_w1, nsa_k_w2, nsa_v_w1, nsa_v_w2)
    o_d = stick_breaking_attention(to_heads(d_q, N_MIX_HEADS), to_heads(d_k, N_MIX_HEADS),
                                   to_heads(d_v, N_MIX_HEADS))
    mixed = jnp.concatenate([from_heads(o_a) * jax.nn.silu(a_g),
                             from_heads(o_b) * jax.nn.silu(b_g),
                             from_heads(o_c) * jax.nn.silu(c_g),
                             from_heads(o_d) * jax.nn.silu(d_g)], axis=-1).astype(x.dtype)
    return x + rms_norm(mixed @ w_out, g_post)


def setup_inputs(seed: int = 0) -> dict:
    key = jax.random.key(seed)
    ks = jax.random.split(key, 16)
    nrm = lambda k, shape, fan: jax.random.normal(k, shape, jnp.float32) * fan ** -0.5
    gain = lambda k, n: 1.0 + 0.05 * jax.random.normal(k, (DEPTH, n), jnp.float32)
    return {
        "x": jax.random.normal(ks[0], (BATCH, SEQ, D_MODEL), jnp.float32),
        "positions": jnp.broadcast_to(jnp.arange(SEQ, dtype=jnp.int32), (BATCH, SEQ)),
        "w_in": nrm(ks[1], (DEPTH, D_MODEL, D_IN), D_MODEL),
        "w_out": nrm(ks[2], (DEPTH, MIX_W, D_MODEL), MIX_W),
        "g_pre": gain(ks[3], D_MODEL),
        "g_post": gain(ks[4], D_MODEL),
        "mla_g_q": gain(ks[5], MLA_Q_RANK),
        "mla_g_kv": gain(ks[6], MLA_KV_RANK),
        "mla_w_uq": nrm(ks[7], (DEPTH, MLA_Q_RANK, N_MIX_HEADS * (MLA_NOPE + MLA_ROPE)), MLA_Q_RANK),
        "mla_w_ukv": nrm(ks[8], (DEPTH, MLA_KV_RANK, N_MIX_HEADS * (MLA_NOPE + MLA_V)), MLA_KV_RANK),
        "nsa_pos_k": 0.5 * jax.random.normal(ks[9], (DEPTH, NSA_CMP_LEN, NSA_KV_DIM), jnp.float32),
        "nsa_pos_v": 0.5 * jax.random.normal(ks[10], (DEPTH, NSA_CMP_LEN, NSA_KV_DIM), jnp.float32),
        "nsa_k_w1": nrm(ks[11], (DEPTH, NSA_CMP_LEN * NSA_KV_DIM, NSA_CMP_HID), NSA_CMP_LEN * NSA_KV_DIM),
        "nsa_k_w2": nrm(ks[12], (DEPTH, NSA_CMP_HID, NSA_KV_DIM), NSA_CMP_HID),
        "nsa_v_w1": nrm(ks[13], (DEPTH, NSA_CMP_LEN * NSA_KV_DIM, NSA_CMP_HID), NSA_CMP_LEN * NSA_KV_DIM),
        "nsa_v_w2": nrm(ks[14], (DEPTH, NSA_CMP_HID, NSA_KV_DIM), NSA_CMP_HID),
    }


def reference(x, positions, w_in, w_out, g_pre, g_post, mla_g_q, mla_g_kv, mla_w_uq, mla_w_ukv,
              nsa_pos_k, nsa_pos_v, nsa_k_w1, nsa_k_w2, nsa_v_w1, nsa_v_w2):
    for l in range(DEPTH):
        x = hybrid_layer(x, positions, w_in[l], w_out[l], g_pre[l], g_post[l],
                         mla_g_q[l], mla_g_kv[l], mla_w_uq[l], mla_w_ukv[l],
                         nsa_pos_k[l], nsa_pos_v[l], nsa_k_w1[l], nsa_k_w2[l],
                         nsa_v_w1[l], nsa_v_w2[l])
    return x
```

```python
import functools
import math

import numpy as np
import jax
import jax.numpy as jnp
from jax import lax
from jax.experimental import pallas as pl
from jax.experimental.pallas import tpu as pltpu

D_MODEL = 1024
HEAD_DIM = 64
N_HEADS = 4
GROUP_W = N_HEADS * HEAD_DIM
ROPE_THETA = 10000.0
EPS = 1e-6
NEG = -1e30
BIG = 1e9

MLA_Q_RANK = 256
MLA_KV_RANK = 128
MLA_NOPE = 64
MLA_ROPE = 32
MLA_V = 64

NSA_CMP_LEN = 32
NSA_CMP_STRIDE = 16
NSA_CMP_HID = 256
NSA_SEL_LEN = 64
NSA_N_SEL = 16
NSA_WINDOW = 512

LANES = 128
BLK = 256
VMEM_LIMIT = 48 * 1024 * 1024

F32 = jnp.float32
BF16 = jnp.bfloat16

(PB_AG, PB_BCQ, PB_BKV, PB_BG, PB_CQ, PB_C0, PB_C1, PB_C2, PB_GC, PB_GS, PB_GW,
 PB_CG, PB_DQ, PB_DK, PB_DV, PB_DG) = range(16)
N_PBLK = 16
ROPE_NONE, ROPE_64, ROPE_32, ROPE_64_LO = 0, 1, 2, 3


def _dot_nt(a, b):
    return lax.dot_general(a, b, (((1,), (1,)), ((), ())), preferred_element_type=F32)


def _dot(a, b):
    return jnp.dot(a, b, preferred_element_type=F32)


def _rope_half(y, cos, sin_signed, group):
    half = group // 2
    lane = lax.broadcasted_iota(jnp.int32, y.shape, 1)
    first = (lane & (group - 1)) < half
    partner = jnp.where(first, pltpu.roll(y, LANES - half, 1), pltpu.roll(y, half, 1))
    return y * cos + partner * sin_signed


def _rope_table_kernel(pos_ref, const_ref, c64_ref, s64_ref, c32_ref, s32_ref):
    pos = pos_ref[...].astype(F32)
    a64 = pos * const_ref[0:1, :]
    a32 = pos * const_ref[2:3, :]
    c64_ref[...] = jnp.cos(a64)
    s64_ref[...] = jnp.sin(a64) * const_ref[1:2, :]
    c32_ref[...] = jnp.cos(a32)
    s32_ref[...] = jnp.sin(a32) * const_ref[3:4, :]


def _rope_tables(positions):
    rows = positions.size
    tm = 1024
    lane = np.arange(LANES)
    inv64 = ROPE_THETA ** (-(lane % 32).astype(np.float64) / 32.0)
    sgn64 = np.where((lane % 64) < 32, -1.0, 1.0)
    inv32 = ROPE_THETA ** (-(lane % 16).astype(np.float64) / 16.0)
    sgn32 = np.where((lane % 32) < 16, -1.0, 1.0)
    const = np.zeros((8, LANES), np.float32)
    const[0], const[1], const[2], const[3] = inv64, sgn64, inv32, sgn32
    out = jax.ShapeDtypeStruct((rows, LANES), F32)
    tab_spec = pl.BlockSpec((tm, LANES), lambda i: (i, 0))
    return pl.pallas_call(
        _rope_table_kernel,
        grid=(rows // tm,),
        in_specs=[pl.BlockSpec((tm, 1), lambda i: (i, 0)),
                  pl.BlockSpec((8, LANES), lambda i: (0, 0))],
        out_specs=[tab_spec] * 4,
        out_shape=[out] * 4,
        name="rope_tables",
    )(positions.reshape(rows, 1), jnp.asarray(const))


def _inproj_kernel(x_ref, g_ref, w_ref, c64_ref, s64_ref, c32_ref, s32_ref,
                   aqkv_ref, proj_ref, *, plan_a, plan_p):
    x = x_ref[...]
    ms = jnp.mean(x * x, axis=-1, keepdims=True)
    h = ((x * lax.rsqrt(ms + EPS)) * g_ref[...]).astype(BF16)
    lane = lax.broadcasted_iota(jnp.int32, (x.shape[0], LANES), 1)

    def epilogue(y, ops, scale):
        halves = []
        for hf, op in enumerate(ops):
            yh = y[:, hf * LANES:(hf + 1) * LANES]
            if op == ROPE_64:
                yh = _rope_half(yh, c64_ref[...], s64_ref[...], 64)
            elif op == ROPE_32:
                yh = _rope_half(yh, c32_ref[...], s32_ref[...], 32)
            elif op == ROPE_64_LO:
                yh = jnp.where(lane < 64, _rope_half(yh, c64_ref[...], s64_ref[...], 64), yh)
            if scale != 1.0:
                yh = yh * scale
            halves.append(yh)
        return jnp.concatenate(halves, axis=1).astype(BF16)

    col = 0
    for out_ref, plan in ((aqkv_ref, plan_a), (proj_ref, plan_p)):
        for b, (ops, scale) in enumerate(plan):
            y = _dot(h, w_ref[:, col:col + BLK])
            out_ref[:, b * BLK:(b + 1) * BLK] = epilogue(y, ops, scale)
            col += BLK


_PLAN_A = (((ROPE_64, ROPE_64), HEAD_DIM ** -0.5),
           ((ROPE_64, ROPE_64), 1.0),
           ((ROPE_NONE, ROPE_NONE), 1.0))
_PLAN_P = (
    ((ROPE_NONE, ROPE_NONE), 1.0),
    ((ROPE_NONE, ROPE_NONE), 1.0),
    ((ROPE_NONE, ROPE_32), 1.0),
    ((ROPE_NONE, ROPE_NONE), 1.0),
    ((ROPE_64, ROPE_64), HEAD_DIM ** -0.5),
    ((ROPE_64_LO, ROPE_64), 1.0),
    ((ROPE_64, ROPE_NONE), 1.0),
    ((ROPE_NONE, ROPE_NONE), 1.0),
    ((ROPE_NONE, ROPE_NONE), 1.0),
    ((ROPE_NONE, ROPE_NONE), 1.0),
    ((ROPE_NONE, ROPE_NONE), 1.0),
    ((ROPE_NONE, ROPE_NONE), 1.0),
    ((ROPE_NONE, ROPE_NONE), HEAD_DIM ** -0.5),
    ((ROPE_NONE, ROPE_NONE), 1.0),
    ((ROPE_NONE, ROPE_NONE), 1.0),
    ((ROPE_NONE, ROPE_NONE), 1.0),
)
assert len(_PLAN_P) == N_PBLK


def _pack_w_in(w):
    o = 0
    seg = {}
    for name, width in (("aq", 256), ("ak", 256), ("av", 256), ("ag", 256),
                        ("bcq", 256), ("bckv", 128), ("bkr", 32), ("bg", 256),
                        ("cq", 256), ("ckc", 64), ("cvc", 64), ("cks", 64), ("cvs", 64),
                        ("ckw", 64), ("cvw", 64), ("cgl", 12), ("cg", 256),
                        ("dq", 256), ("dk", 256), ("dv", 256), ("dg", 256)):
        seg[name] = w[:, o:o + width]
        o += width
    gl = seg["cgl"].reshape(D_MODEL, N_HEADS, 3)
    gate = lambda r: jnp.repeat(gl[:, :, r], HEAD_DIM, axis=1)
    zeros = jnp.zeros((D_MODEL, LANES), w.dtype)
    cols = [seg["aq"], seg["ak"], seg["av"],
            seg["ag"], seg["bcq"],
            seg["bckv"], jnp.tile(seg["bkr"], (1, 4)),
            seg["bg"], seg["cq"],
            seg["ckc"], seg["cvc"], seg["cks"], seg["cks"],
            seg["ckw"], seg["ckw"], seg["cvs"], seg["cvs"],
            seg["cvw"], seg["cvw"], zeros,
            gate(0), gate(1), gate(2),
            seg["cg"], seg["dq"], seg["dk"], seg["dv"], seg["dg"]]
    return jnp.concatenate(cols, axis=1).astype(BF16)


def _in_projection(x2, g_pre, w_packed, tabs):
    rows = x2.shape[0]
    tm = 512
    ncol = w_packed.shape[1]
    tab_spec = pl.BlockSpec((tm, LANES), lambda i: (i, 0))
    return pl.pallas_call(
        functools.partial(_inproj_kernel, plan_a=_PLAN_A, plan_p=_PLAN_P),
        grid=(rows // tm,),
        in_specs=[pl.BlockSpec((tm, D_MODEL), lambda i: (i, 0)),
                  pl.BlockSpec((1, D_MODEL), lambda i: (0, 0)),
                  pl.BlockSpec((D_MODEL, ncol), lambda i: (0, 0))] + [tab_spec] * 4,
        out_specs=[pl.BlockSpec((tm, 3 * BLK), lambda i: (i, 0)),
                   pl.BlockSpec((tm, N_PBLK * BLK), lambda i: (i, 0))],
        out_shape=[jax.ShapeDtypeStruct((rows, 3 * BLK), BF16),
                   jax.ShapeDtypeStruct((rows, N_PBLK * BLK), BF16)],
        compiler_params=pltpu.CompilerParams(dimension_semantics=("arbitrary",),
                                             vmem_limit_bytes=VMEM_LIMIT),
        name="in_projection",
    )(x2, g_pre.reshape(1, D_MODEL), w_packed, *tabs)


def _head_masks(shape):
    lane = lax.broadcasted_iota(jnp.int32, shape, 1)
    return lane < HEAD_DIM


def _masked_heads(q_ref, lo_mask):
    out = []
    for h in range(N_HEADS):
        pair, e = divmod(h, 2)
        qp = q_ref[:, pair * LANES:(pair + 1) * LANES]
        keep = lo_mask if e == 0 else jnp.logical_not(lo_mask)
        out.append(jnp.where(keep, qp, jnp.zeros_like(qp)))
    return out


def _flash_kernel(*refs, T, window, kshared, mla, use_sel, want_lse):
    it = iter(refs)
    q_ref = next(it)
    qr_ref = next(it) if mla else None
    k_ref = next(it)
    kr_ref = next(it) if mla else None
    v_ref = next(it)
    sel_ref = next(it) if use_sel else None
    et_ref = next(it) if use_sel else None
    o_ref = next(it)
    lse_ref = next(it) if want_lse else None

    i = pl.program_id(2)
    lo = _head_masks((T, LANES))
    qm = _masked_heads(q_ref, lo)
    if mla:
        lane = lax.broadcasted_iota(jnp.int32, (T, LANES), 1)
        qr = qr_ref[...]
        qrm = [jnp.where((lane >> 5) == h, qr, jnp.zeros_like(qr)) for h in range(N_HEADS)]
    if use_sel:
        selm = sel_ref[...]
    row = lax.broadcasted_iota(jnp.int32, (T, T), 0)
    col = lax.broadcasted_iota(jnp.int32, (T, T), 1)

    def step(states, j, mask):
        rows = pl.ds(pl.multiple_of(j * T, T), T)
        if use_sel:
            chosen = _dot_nt(selm, et_ref[rows, :]) > 0.5
            mask = chosen if mask is None else jnp.logical_and(mask, chosen)
        if mla:
            krb = kr_ref[rows, :]
        new = []
        for h in range(N_HEADS):
            m, l, acc = states[h]
            pair = h // 2
            cols = slice(0, LANES) if kshared else slice(pair * LANES, (pair + 1) * LANES)
            s = _dot_nt(qm[h], k_ref[rows, cols])
            if mla:
                s = s + _dot_nt(qrm[h], krb)
            if mask is not None:
                s = jnp.where(mask, s, NEG)
            m_new = jnp.maximum(m, jnp.max(s, axis=-1, keepdims=True))
            alpha = jnp.exp(m - m_new)
            p = jnp.exp(s - m_new)
            if mask is not None:
                p = jnp.where(mask, p, 0.0)
            l_new = alpha * l + jnp.sum(p, axis=-1, keepdims=True)
            acc_new = alpha * acc + _dot(p.astype(BF16), v_ref[rows, cols])
            new.append((m_new, l_new, acc_new))
        return tuple(new)

    init = tuple((jnp.full((T, 1), NEG, F32), jnp.zeros((T, 1), F32), jnp.zeros((T, LANES), F32))
                 for _ in range(N_HEADS))
    states = init
    if window is None:
        lo_full = 0
    else:
        o_full = (window - T + 1) // T
        o_none = (window + T - 1) // T + 1
        for o in range(o_none - 1, o_full, -1):
            j = i - o
            limit = jnp.where(j >= 0, window - o * T, -2 * T)
            states = step(states, jnp.maximum(j, 0), (row - col) <= limit)
        lo_full = jnp.maximum(i - o_full, 0)
    states = lax.fori_loop(lo_full, i, lambda j, st: step(st, j, None), states)
    states = step(states, i, col <= row)

    for pair in range(2):
        outs, lses = [], []
        for e in range(2):
            m, l, acc = states[2 * pair + e]
            outs.append(acc * (1.0 / l))
            lses.append(m + jnp.log(l))
        o_ref[:, pair * LANES:(pair + 1) * LANES] = jnp.where(lo, outs[0], outs[1]).astype(BF16)
        if want_lse:
            lse_ref[:, pair * LANES:(pair + 1) * LANES] = jnp.where(
                lo, jnp.broadcast_to(lses[0], (T, LANES)), jnp.broadcast_to(lses[1], (T, LANES)))


def _flash(q, k, v, *, B, L, ncls, T, qcol, kcol, vcol, kw, stride, window=None,
           mla=None, sel=None, want_lse=False, name="flash"):
    del stride
    nq = L // T
    kshared = kw == LANES
    in_specs = [pl.BlockSpec((None, T, BLK), lambda b, c, i: (b, i, qcol(c)))]
    args = [q]
    if mla is not None:
        qr, qrcol, kr, krcol = mla
        in_specs.append(pl.BlockSpec((None, T, LANES), lambda b, c, i: (b, i, qrcol)))
        args.append(qr)
    in_specs.append(pl.BlockSpec((None, L, kw), lambda b, c, i: (b, 0, kcol(c))))
    args.append(k)
    if mla is not None:
        in_specs.append(pl.BlockSpec((None, L, LANES), lambda b, c, i: (b, 0, krcol)))
        args.append(kr)
    in_specs.append(pl.BlockSpec((None, L, kw), lambda b, c, i: (b, 0, vcol(c))))
    args.append(v)
    if sel is not None:
        selarr, et = sel
        in_specs.append(pl.BlockSpec((None, T, LANES), lambda b, c, i: (b, i, 0)))
        in_specs.append(pl.BlockSpec((L, LANES), lambda b, c, i: (0, 0)))
        args += [selarr, et]
    out_specs = [pl.BlockSpec((None, T, BLK), lambda b, c, i: (b, i, c))]
    out_shape = [jax.ShapeDtypeStruct((B, L, ncls * BLK), BF16)]
    if want_lse:
        out_specs.append(pl.BlockSpec((None, T, BLK), lambda b, c, i: (b, i, c)))
        out_shape.append(jax.ShapeDtypeStruct((B, L, ncls * BLK), F32))
    return pl.pallas_call(
        functools.partial(_flash_kernel, T=T, window=window, kshared=kshared,
                          mla=mla is not None, use_sel=sel is not None, want_lse=want_lse),
        grid=(B, ncls, nq),
        in_specs=in_specs,
        out_specs=out_specs,
        out_shape=out_shape,
        compiler_params=pltpu.CompilerParams(
            dimension_semantics=("arbitrary", "arbitrary", "arbitrary"),
            vmem_limit_bytes=VMEM_LIMIT),
        name=name,
    )(*args)


def _mla_up_kernel(cq_ref, ckv_ref, gq_ref, gkv_ref, wq_ref, wkv_ref, c32_ref, s32_ref,
                   qn_ref, qr_ref, kn_ref, v_ref):
    def norm(t, g):
        t = t.astype(F32)
        ms = jnp.mean(t * t, axis=-1, keepdims=True)
        return ((t * lax.rsqrt(ms + EPS)) * g).astype(BF16)

    scale = (MLA_NOPE + MLA_ROPE) ** -0.5
    q = _dot(norm(cq_ref[...], gq_ref[...]), wq_ref[...])
    qn_ref[...] = (q[:, :BLK] * scale).astype(BF16)
    qr_ref[...] = (_rope_half(q[:, BLK:], c32_ref[...], s32_ref[...], MLA_ROPE) * scale).astype(BF16)
    kv = _dot(norm(ckv_ref[...], gkv_ref[...]), wkv_ref[...])
    kn_ref[...] = kv[:, :BLK].astype(BF16)
    v_ref[...] = kv[:, BLK:].astype(BF16)


def _mla_up(proj, g_q, g_kv, w_uq, w_ukv, c32, s32):
    rows = proj.shape[0]
    tm = 1024
    wq = w_uq.reshape(MLA_Q_RANK, N_HEADS, MLA_NOPE + MLA_ROPE)
    wq = jnp.concatenate([wq[:, :, :MLA_NOPE].reshape(MLA_Q_RANK, -1),
                          wq[:, :, MLA_NOPE:].reshape(MLA_Q_RANK, -1)], axis=1).astype(BF16)
    wkv = w_ukv.reshape(MLA_KV_RANK, N_HEADS, MLA_NOPE + MLA_V)
    wkv = jnp.concatenate([wkv[:, :, :MLA_NOPE].reshape(MLA_KV_RANK, -1),
                           wkv[:, :, MLA_NOPE:].reshape(MLA_KV_RANK, -1)], axis=1).astype(BF16)
    row_spec = lambda w, c: pl.BlockSpec((tm, w), lambda i: (i, c))
    full = lambda a: pl.BlockSpec(a.shape, lambda i: (0, 0))
    gq = g_q.reshape(1, -1)
    gkv = g_kv.reshape(1, -1)
    outs = [jax.ShapeDtypeStruct((rows, BLK), BF16), jax.ShapeDtypeStruct((rows, LANES), BF16),
            jax.ShapeDtypeStruct((rows, BLK), BF16), jax.ShapeDtypeStruct((rows, BLK), BF16)]
    return pl.pallas_call(
        _mla_up_kernel,
        grid=(rows // tm,),
        in_specs=[row_spec(BLK, PB_BCQ), row_spec(LANES, 2 * PB_BKV), full(gq), full(gkv),
                  full(wq), full(wkv), row_spec(LANES, 0), row_spec(LANES, 0)],
        out_specs=[row_spec(BLK, 0), row_spec(LANES, 0), row_spec(BLK, 0), row_spec(BLK, 0)],
        out_shape=outs,
        compiler_params=pltpu.CompilerParams(dimension_semantics=("arbitrary",),
                                             vmem_limit_bytes=VMEM_LIMIT),
        name="mla_up",
    )(proj, proj, gq, gkv, wq, wkv, c32, s32)


def _compress_kernel(x_ref, pos_ref, w1_ref, w2_ref, kc_ref, vc_ref):
    x = x_ref[...].astype(F32)
    half = x.shape[1]
    top = (x + pos_ref[0:1, :]).astype(BF16)
    bot = (x + pos_ref[1:2, :]).astype(BF16)
    y_top = _dot(top, w1_ref[0:half, :])
    y_bot = _dot(bot, w1_ref[half:2 * half, :])
    pre = y_top + pltpu.roll(y_bot, y_bot.shape[0] - 1, 0)
    hid = (pre * (1.0 / (1.0 + jnp.exp(-pre)))).astype(BF16)
    kc_ref[...] = _dot(hid[:, :NSA_CMP_HID], w2_ref[0]).astype(BF16)
    vc_ref[...] = _dot(hid[:, NSA_CMP_HID:], w2_ref[1]).astype(BF16)


def _compress(kcvc, pos_k, pos_v, kw1, kw2, vw1, vw2, B, S):
    n16 = S // NSA_CMP_STRIDE
    x16 = kcvc.reshape(B, n16, NSA_CMP_STRIDE * LANES)
    pos = jnp.concatenate([pos_k, pos_v], axis=1).reshape(2, NSA_CMP_STRIDE * LANES)
    w1 = jnp.zeros((NSA_CMP_LEN, LANES, 2 * NSA_CMP_HID), F32)
    w1 = w1.at[:, :HEAD_DIM, :NSA_CMP_HID].set(kw1.reshape(NSA_CMP_LEN, HEAD_DIM, NSA_CMP_HID))
    w1 = w1.at[:, HEAD_DIM:, NSA_CMP_HID:].set(vw1.reshape(NSA_CMP_LEN, HEAD_DIM, NSA_CMP_HID))
    w1 = w1.reshape(NSA_CMP_LEN * LANES, 2 * NSA_CMP_HID).astype(BF16)
    w2 = jnp.stack([jnp.tile(kw2, (1, 2)), jnp.tile(vw2, (1, 2))]).astype(BF16)
    out = jax.ShapeDtypeStruct((B, n16, LANES), BF16)
    return pl.pallas_call(
        _compress_kernel,
        grid=(B,),
        in_specs=[pl.BlockSpec((None, n16, NSA_CMP_STRIDE * LANES), lambda b: (b, 0, 0)),
                  pl.BlockSpec(pos.shape, lambda b: (0, 0)),
                  pl.BlockSpec(w1.shape, lambda b: (0, 0)),
                  pl.BlockSpec(w2.shape, lambda b: (0, 0, 0))],
        out_specs=[pl.BlockSpec((None, n16, LANES), lambda b: (b, 0, 0))] * 2,
        out_shape=[out, out],
        compiler_params=pltpu.CompilerParams(dimension_semantics=("arbitrary",),
                                             vmem_limit_bytes=VMEM_LIMIT),
        name="nsa_compress",
    )(x16, pos, w1, w2)


def _cmp_topk_kernel(q_ref, kc_ref, vc_ref, ovt_ref, o_ref, sel_ref, *, T, n_cmp, n_slc):
    i = pl.program_id(1)
    t0 = i * T
    lane = lax.broadcasted_iota(jnp.int32, (T, LANES), 1)
    tok = t0 + lax.broadcasted_iota(jnp.int32, (T, LANES), 0)
    ok = jnp.logical_and(lane * NSA_CMP_STRIDE + (NSA_CMP_LEN - 1) <= tok, lane < n_cmp)
    lo = lane < HEAD_DIM
    qm = _masked_heads(q_ref, lo)
    kc = kc_ref[...]
    vc = vc_ref[...]
    psum = jnp.zeros((T, LANES), F32)
    outs = []
    for h in range(N_HEADS):
        s = jnp.where(ok, _dot_nt(qm[h], kc), NEG)
        mx = jnp.max(s, axis=-1, keepdims=True)
        e = jnp.where(ok, jnp.exp(s - mx), 0.0)
        den = jnp.sum(e, axis=-1, keepdims=True)
        p = e * (1.0 / jnp.maximum(den, 1e-30))
        psum = psum + p
        outs.append(_dot(p.astype(BF16), vc))
    for pair in range(2):
        o_ref[:, pair * LANES:(pair + 1) * LANES] = jnp.where(
            lo, outs[2 * pair], outs[2 * pair + 1]).astype(BF16)

    p_hi = psum.astype(BF16)
    p_lo = (psum - p_hi.astype(F32)).astype(BF16)
    ovt = ovt_ref[...]
    imp = _dot_nt(ovt, p_hi) + _dot_nt(ovt, p_lo)
    jj = lax.broadcasted_iota(jnp.int32, (n_slc, T), 0)
    bt = (t0 + lax.broadcasted_iota(jnp.int32, (n_slc, T), 1)) >> 6
    forced = jnp.logical_or(jj == 0, jnp.logical_or(jj == bt, jj == bt - 1))
    val = jnp.where(forced, BIG, jnp.where(jj > bt, -BIG, imp))
    sel_rows = []
    for j in range(n_slc):
        vj = val[j:j + 1, :]
        beats = jnp.logical_or(val > vj, jnp.logical_and(val == vj, jj < j))
        rank = jnp.sum(beats.astype(F32), axis=0, keepdims=True)
        sel_rows.append((rank < NSA_N_SEL).astype(F32))
    sel_rows.append(jnp.zeros((LANES - n_slc, T), F32))
    sel_t = jnp.concatenate(sel_rows, axis=0)
    sel_ref[...] = sel_t.T.astype(BF16)


def _cmp_topk(proj3, kc2, vc2, B, S):
    T = 256
    n_cmp = (S - NSA_CMP_LEN) // NSA_CMP_STRIDE + 1
    n_slc = S // NSA_SEL_LEN
    ci = np.arange(LANES)[None, :] * NSA_CMP_STRIDE
    sj = np.arange(n_slc)[:, None] * NSA_SEL_LEN
    ovt = ((ci < sj + NSA_SEL_LEN) & (ci + NSA_CMP_LEN > sj) & (np.arange(LANES)[None, :] < n_cmp))
    ovt = jnp.asarray(ovt.astype(np.float32), BF16)
    return pl.pallas_call(
        functools.partial(_cmp_topk_kernel, T=T, n_cmp=n_cmp, n_slc=n_slc),
        grid=(B, S // T),
        in_specs=[pl.BlockSpec((None, T, BLK), lambda b, i: (b, i, PB_CQ)),
                  pl.BlockSpec((None, LANES, LANES), lambda b, i: (b, 0, 0)),
                  pl.BlockSpec((None, LANES, LANES), lambda b, i: (b, 0, 0)),
                  pl.BlockSpec(ovt.shape, lambda b, i: (0, 0))],
        out_specs=[pl.BlockSpec((None, T, BLK), lambda b, i: (b, i, 0)),
                   pl.BlockSpec((None, T, LANES), lambda b, i: (b, i, 0))],
        out_shape=[jax.ShapeDtypeStruct((B, S, BLK), BF16),
                   jax.ShapeDtypeStruct((B, S, LANES), BF16)],
        compiler_params=pltpu.CompilerParams(dimension_semantics=("arbitrary", "arbitrary"),
                                             vmem_limit_bytes=VMEM_LIMIT),
        name="nsa_cmp_topk",
    )(proj3, kc2, vc2, ovt)


def _stick_kernel(q_ref, k_ref, v_ref, u_ref, o_ref, *, T):
    i = pl.program_id(1)
    lo = _head_masks((T, LANES))
    qm = _masked_heads(q_ref, lo)
    row = lax.broadcasted_iota(jnp.int32, (T, T), 0)
    col = lax.broadcasted_iota(jnp.int32, (T, T), 1)
    strict = col < row
    u = u_ref[...]

    def step(states, j, diag):
        rows = pl.ds(pl.multiple_of(j * T, T), T)
        new = []
        for h in range(N_HEADS):
            carry, acc = states[h]
            pair = h // 2
            cols = slice(pair * LANES, (pair + 1) * LANES)
            z = _dot_nt(qm[h], k_ref[rows, cols])
            sp = jnp.maximum(z, 0.0) + jnp.log(1.0 + jnp.exp(-jnp.abs(z)))
            log_keep = -sp
            if diag:
                log_keep = jnp.where(strict, log_keep, 0.0)
            hi = log_keep.astype(BF16)
            lo_part = (log_keep - hi.astype(F32)).astype(BF16)
            after = _dot(hi, u) + _dot(lo_part, u) + carry
            a = jnp.exp((z - sp) + after)
            if diag:
                a = jnp.where(strict, a, 0.0)
            acc = acc + _dot(a.astype(BF16), v_ref[rows, cols])
            carry = carry + jnp.sum(log_keep, axis=-1, keepdims=True)
            new.append((carry, acc))
        return tuple(new)

    states = tuple((jnp.zeros((T, 1), F32), jnp.zeros((T, LANES), F32)) for _ in range(N_HEADS))
    states = step(states, i, True)
    states = lax.fori_loop(0, i, lambda jj, st: step(st, i - 1 - jj, False), states)
    for pair in range(2):
        o_ref[:, pair * LANES:(pair + 1) * LANES] = jnp.where(
            lo, states[2 * pair][1], states[2 * pair + 1][1]).astype(BF16)


def _stick_breaking(proj3, B, S):
    T = 256
    u = jnp.asarray(np.tril(np.ones((T, T), np.float32), -1), BF16)
    return pl.pallas_call(
        functools.partial(_stick_kernel, T=T),
        grid=(B, S // T),
        in_specs=[pl.BlockSpec((None, T, BLK), lambda b, i: (b, i, PB_DQ)),
                  pl.BlockSpec((None, S, BLK), lambda b, i: (b, 0, PB_DK)),
                  pl.BlockSpec((None, S, BLK), lambda b, i: (b, 0, PB_DV)),
                  pl.BlockSpec((T, T), lambda b, i: (0, 0))],
        out_specs=pl.BlockSpec((None, T, BLK), lambda b, i: (b, i, 0)),
        out_shape=jax.ShapeDtypeStruct((B, S, BLK), BF16),
        compiler_params=pltpu.CompilerParams(dimension_semantics=("arbitrary", "arbitrary"),
                                             vmem_limit_bytes=VMEM_LIMIT),
        name="stick_breaking",
    )(proj3, proj3, proj3, u)


def _outproj_kernel(x_ref, oa1, la1, oa2, la2, oa3, la3, ob, oc, os_, ow, od,
                    ga, gb, gcc, gcs, gcw, gc, gd, w_ref, g_ref, out_ref):
    f = lambda r: r[...].astype(F32)
    sig = lambda t: 1.0 / (1.0 + jnp.exp(-t))
    silu = lambda t: t * sig(t)
    l1, l2, l3 = la1[...], la2[...], la3[...]
    mx = jnp.maximum(l1, jnp.maximum(l2, l3))
    w1, w2, w3 = jnp.exp(l1 - mx), jnp.exp(l2 - mx), jnp.exp(l3 - mx)
    o_a = (w1 * f(oa1) + w2 * f(oa2) + w3 * f(oa3)) / (w1 + w2 + w3)
    o_c = sig(f(gcc)) * f(oc) + sig(f(gcs)) * f(os_) + sig(f(gcw)) * f(ow)
    mixed = [o_a * silu(f(ga)), f(ob) * silu(f(gb)), o_c * silu(f(gc)), f(od) * silu(f(gd))]
    y = _dot(mixed[0].astype(BF16), w_ref[0:BLK, :])
    for g in range(1, 4):
        y = y + _dot(mixed[g].astype(BF16), w_ref[g * BLK:(g + 1) * BLK, :])
    ms = jnp.mean(y * y, axis=-1, keepdims=True)
    out_ref[...] = x_ref[...] + (y * lax.rsqrt(ms + EPS)) * g_ref[...]


def _out_projection(x2, a_parts, o_b, c_parts, o_d, proj, w_out, g_post):
    rows = x2.shape[0]
    tm = 512
    blk = lambda c: pl.BlockSpec((tm, BLK), lambda i: (i, c))
    a_args, a_specs = [], []
    for o, lse in a_parts:
        a_args += [o, lse]
        a_specs += [blk(0), blk(0)]
    c_args = list(c_parts)
    gates = [PB_AG, PB_BG, PB_GC, PB_GS, PB_GW, PB_CG, PB_DG]
    return pl.pallas_call(
        _outproj_kernel,
        grid=(rows // tm,),
        in_specs=[pl.BlockSpec((tm, D_MODEL), lambda i: (i, 0))] + a_specs
                 + [blk(0)] * (1 + len(c_args) + 1) + [blk(c) for c in gates]
                 + [pl.BlockSpec((D_MODEL, D_MODEL), lambda i: (0, 0)),
                    pl.BlockSpec((1, D_MODEL), lambda i: (0, 0))],
        out_specs=pl.BlockSpec((tm, D_MODEL), lambda i: (i, 0)),
        out_shape=jax.ShapeDtypeStruct((rows, D_MODEL), F32),
        compiler_params=pltpu.CompilerParams(dimension_semantics=("arbitrary",),
                                             vmem_limit_bytes=VMEM_LIMIT),
        name="out_projection",
    )(x2, *a_args, o_b, *c_args, o_d, *([proj] * len(gates)), w_out.astype(BF16),
      g_post.reshape(1, D_MODEL))


def _layer(x2, tabs, B, S, w_in, w_out, g_pre, g_post, mla_g_q, mla_g_kv, mla_w_uq, mla_w_ukv,
           nsa_pos_k, nsa_pos_v, nsa_k_w1, nsa_k_w2, nsa_v_w1, nsa_v_w2):
    rows = B * S
    c64, s64, c32, s32 = tabs
    aqkv, proj = _in_projection(x2, g_pre, _pack_w_in(w_in), tabs)
    proj3 = proj.reshape(B, S, N_PBLK * BLK)

    a_parts = []
    for win, dil in ((128, 1), (512, 4), (2048, 16)):
        L = S // dil
        view = aqkv.reshape(B, L, dil * 3 * BLK)
        o, lse = _flash(view, view, view, B=B, L=L, ncls=dil, T=128,
                        qcol=lambda c: 3 * c, kcol=lambda c: 3 * c + 1, vcol=lambda c: 3 * c + 2,
                        kw=BLK, stride=dil, window=win // dil, want_lse=True, name=f"dilated_r{dil}")
        a_parts.append((o.reshape(rows, BLK), lse.reshape(rows, BLK)))

    qn, qr, kn, vb = _mla_up(proj, mla_g_q, mla_g_kv, mla_w_uq, mla_w_ukv, c32, s32)
    r3 = lambda t: t.reshape(B, S, t.shape[-1])
    (o_b,) = _flash(r3(qn), r3(kn), r3(vb), B=B, L=S, ncls=1, T=256,
                    qcol=lambda c: 0, kcol=lambda c: 0, vcol=lambda c: 0, kw=BLK, stride=1,
                    mla=(r3(qr), 0, proj3, 2 * PB_BKV + 1), name="mla_attention")

    kcvc = proj3[:, :, PB_C0 * BLK:PB_C0 * BLK + LANES]
    kc2, vc2 = _compress(kcvc, nsa_pos_k, nsa_pos_v, nsa_k_w1, nsa_k_w2, nsa_v_w1, nsa_v_w2, B, S)
    o_cmp, sel = _cmp_topk(proj3, kc2, vc2, B, S)
    et = np.zeros((S, LANES), np.float32)
    et[np.arange(S), np.arange(S) // NSA_SEL_LEN] = 1.0
    (o_slc,) = _flash(proj3, proj3, proj3, B=B, L=S, ncls=1, T=256,
                      qcol=lambda c: PB_CQ, kcol=lambda c: 2 * PB_C0 + 1, vcol=lambda c: 2 * PB_C1 + 1,
                      kw=LANES, stride=1, sel=(sel, jnp.asarray(et, BF16)), name="nsa_selected")
    (o_win,) = _flash(proj3, proj3, proj3, B=B, L=S, ncls=1, T=256,
                      qcol=lambda c: PB_CQ, kcol=lambda c: 2 * PB_C1, vcol=lambda c: 2 * PB_C2,
                      kw=LANES, stride=1, window=NSA_WINDOW - 1, name="nsa_window")

    o_d = _stick_breaking(proj3, B, S)

    flat = lambda t: t.reshape(rows, BLK)
    return _out_projection(x2, a_parts, flat(o_b), (flat(o_cmp), flat(o_slc), flat(o_win)),
                           flat(o_d), proj, w_out, g_post)


def kernel(x, positions, w_in, w_out, g_pre, g_post, mla_g_q, mla_g_kv, mla_w_uq, mla_w_ukv,
           nsa_pos_k, nsa_pos_v, nsa_k_w1, nsa_k_w2, nsa_v_w1, nsa_v_w2):
    B, S, D = x.shape
    assert D == D_MODEL and S % 256 == 0
    x2 = x.reshape(B * S, D)
    tabs = _rope_tables(positions)
    for l in range(w_in.shape[0]):
        x2 = _layer(x2, tabs, B, S, w_in[l], w_out[l], g_pre[l], g_post[l],
                    mla_g_q[l], mla_g_kv[l], mla_w_uq[l], mla_w_ukv[l],
                    nsa_pos_k[l], nsa_pos_v[l], nsa_k_w1[l], nsa_k_w2[l],
                    nsa_v_w1[l], nsa_v_w2[l])
    return x2.reshape(B, S, D)
```

```python
import functools
import math

import numpy as np
import jax
import jax.numpy as jnp
from jax import lax
from jax.experimental import pallas as pl
from jax.experimental.pallas import tpu as pltpu

D_MODEL = 1024
HEAD_DIM = 64
N_HEADS = 4
ROPE_THETA = 10000.0
EPS = 1e-6
NEG = -1e30
LOG2E = math.log2(math.e)
MASK_PENALTY = -(2.0 ** 100)
BIG = 1e9

MLA_Q_RANK = 256
MLA_KV_RANK = 128
MLA_NOPE = 64
MLA_ROPE = 32
MLA_V = 64

NSA_CMP_LEN = 32
NSA_CMP_STRIDE = 16
NSA_CMP_HID = 256
NSA_SEL_LEN = 64
NSA_N_SEL = 16
NSA_WINDOW = 512

LANES = 128
BLK = 256
TB = 256
ONES_ROWS = 16
VMEM_LIMIT = 48 * 1024 * 1024

F32 = jnp.float32
BF16 = jnp.bfloat16

(PB_AG, PB_BCQ, PB_BKV, PB_BG, PB_CQ, PB_C0, PB_C1, PB_GC, PB_GS, PB_GW,
 PB_CG, PB_DQ, PB_DK, PB_DG) = range(14)
N_PBLK = 14
ROPE_NONE, ROPE_64, ROPE_32, ROPE_64_LO = 0, 1, 2, 3


def _dot_nt(a, b):
    return lax.dot_general(a, b, (((1,), (1,)), ((), ())), preferred_element_type=F32)


def _dot(a, b):
    return jnp.dot(a, b, preferred_element_type=F32)


def _rope_half(y, cos, sin_signed, group):
    half = group // 2
    lane = lax.broadcasted_iota(jnp.int32, y.shape, 1)
    first = (lane & (group - 1)) < half
    partner = jnp.where(first, pltpu.roll(y, LANES - half, 1), pltpu.roll(y, half, 1))
    return y * cos + partner * sin_signed


def _rope_table_kernel(pos_ref, const_ref, c64_ref, s64_ref, c32_ref, s32_ref):
    pos = pos_ref[...].astype(F32)
    a64 = pos * const_ref[0:1, :]
    a32 = pos * const_ref[2:3, :]
    c64_ref[...] = jnp.cos(a64)
    s64_ref[...] = jnp.sin(a64) * const_ref[1:2, :]
    c32_ref[...] = jnp.cos(a32)
    s32_ref[...] = jnp.sin(a32) * const_ref[3:4, :]


def _rope_tables(positions):
    rows = positions.size
    tm = 1024
    lane = np.arange(LANES)
    inv64 = ROPE_THETA ** (-(lane % 32).astype(np.float64) / 32.0)
    sgn64 = np.where((lane % 64) < 32, -1.0, 1.0)
    inv32 = ROPE_THETA ** (-(lane % 16).astype(np.float64) / 16.0)
    sgn32 = np.where((lane % 32) < 16, -1.0, 1.0)
    const = np.zeros((8, LANES), np.float32)
    const[0], const[1], const[2], const[3] = inv64, sgn64, inv32, sgn32
    out = jax.ShapeDtypeStruct((rows, LANES), F32)
    tab_spec = pl.BlockSpec((tm, LANES), lambda i: (i, 0))
    return pl.pallas_call(
        _rope_table_kernel,
        grid=(rows // tm,),
        in_specs=[pl.BlockSpec((tm, 1), lambda i: (i, 0)),
                  pl.BlockSpec((8, LANES), lambda i: (0, 0))],
        out_specs=[tab_spec] * 4,
        out_shape=[out] * 4,
        name="rope_tables",
    )(positions.reshape(rows, 1), jnp.asarray(const))


def _inproj_kernel(x_ref, g_ref, w_ref, wt_ref, c64_ref, s64_ref, c32_ref, s32_ref,
                   aqkv_ref, proj_ref, vct_ref, vdt_ref, *, plan_a, plan_p):
    x = x_ref[...]
    tm = x.shape[0]
    ms = jnp.mean(x * x, axis=-1, keepdims=True)
    h = ((x * lax.rsqrt(ms + EPS)) * g_ref[...]).astype(BF16)
    lane = lax.broadcasted_iota(jnp.int32, (tm, LANES), 1)

    def epilogue(y, ops, scale):
        halves = []
        for hf, op in enumerate(ops):
            yh = y[:, hf * LANES:(hf + 1) * LANES]
            if op == ROPE_64:
                yh = _rope_half(yh, c64_ref[...], s64_ref[...], 64)
            elif op == ROPE_32:
                yh = _rope_half(yh, c32_ref[...], s32_ref[...], 32)
            elif op == ROPE_64_LO:
                yh = jnp.where(lane < 64, _rope_half(yh, c64_ref[...], s64_ref[...], 64), yh)
            if scale != 1.0:
                yh = yh * scale
            halves.append(yh)
        return jnp.concatenate(halves, axis=1).astype(BF16)

    col = 0
    for out_ref, plan in ((aqkv_ref, plan_a), (proj_ref, plan_p)):
        for b, (ops, scale) in enumerate(plan):
            y = _dot(h, w_ref[:, col:col + BLK])
            out_ref[:, b * BLK:(b + 1) * BLK] = epilogue(y, ops, scale)
            col += BLK

    vt = _dot_nt(wt_ref[...], h)
    for blk in range(tm // TB):
        cols = slice(blk * TB, (blk + 1) * TB)
        vct_ref[blk] = vt[0:LANES, cols].astype(BF16)
        vdt_ref[blk] = vt[LANES:LANES + BLK, cols].astype(BF16)


_PLAN_A = (((ROPE_64, ROPE_64), HEAD_DIM ** -0.5),
           ((ROPE_64, ROPE_64), 1.0),
           ((ROPE_NONE, ROPE_NONE), 1.0))
_PLAN_P = (
    ((ROPE_NONE, ROPE_NONE), 1.0),
    ((ROPE_NONE, ROPE_NONE), 1.0),
    ((ROPE_NONE, ROPE_32), 1.0),
    ((ROPE_NONE, ROPE_NONE), 1.0),
    ((ROPE_64, ROPE_64), LOG2E * HEAD_DIM ** -0.5),
    ((ROPE_64_LO, ROPE_64), 1.0),
    ((ROPE_64, ROPE_NONE), 1.0),
    ((ROPE_NONE, ROPE_NONE), 1.0),
    ((ROPE_NONE, ROPE_NONE), 1.0),
    ((ROPE_NONE, ROPE_NONE), 1.0),
    ((ROPE_NONE, ROPE_NONE), 1.0),
    ((ROPE_NONE, ROPE_NONE), HEAD_DIM ** -0.5),
    ((ROPE_NONE, ROPE_NONE), 1.0),
    ((ROPE_NONE, ROPE_NONE), 1.0),
)
assert len(_PLAN_P) == N_PBLK


def _pack_w_in(w):
    o = 0
    seg = {}
    for name, width in (("aq", 256), ("ak", 256), ("av", 256), ("ag", 256),
                        ("bcq", 256), ("bckv", 128), ("bkr", 32), ("bg", 256),
                        ("cq", 256), ("ckc", 64), ("cvc", 64), ("cks", 64), ("cvs", 64),
                        ("ckw", 64), ("cvw", 64), ("cgl", 12), ("cg", 256),
                        ("dq", 256), ("dk", 256), ("dv", 256), ("dg", 256)):
        seg[name] = w[:, o:o + width]
        o += width
    gl = seg["cgl"].reshape(D_MODEL, N_HEADS, 3)
    gate = lambda r: jnp.repeat(gl[:, :, r], HEAD_DIM, axis=1)
    zeros = jnp.zeros((D_MODEL, LANES), w.dtype)
    cols = [seg["aq"], seg["ak"], seg["av"],
            seg["ag"], seg["bcq"],
            seg["bckv"], jnp.tile(seg["bkr"], (1, 4)),
            seg["bg"], seg["cq"],
            seg["ckc"], seg["cvc"], seg["cks"], seg["cks"],
            seg["ckw"], seg["ckw"], zeros,
            gate(0), gate(1), gate(2),
            seg["cg"], seg["dq"], seg["dk"], seg["dg"]]
    w_rows = jnp.concatenate(cols, axis=1).astype(BF16)
    w_t = jnp.concatenate([seg["cvs"], seg["cvw"], seg["dv"]], axis=1).T.astype(BF16)
    return w_rows, w_t


def _in_projection(x2, g_pre, w_rows, w_t, tabs, B, S):
    rows = x2.shape[0]
    tm = 512
    per_b = S // tm
    nkb = tm // TB
    tab_spec = pl.BlockSpec((tm, LANES), lambda i: (i, 0))
    vt_spec = lambda r: pl.BlockSpec((None, nkb, r, TB), lambda i: (i // per_b, i % per_b, 0, 0))
    return pl.pallas_call(
        functools.partial(_inproj_kernel, plan_a=_PLAN_A, plan_p=_PLAN_P),
        grid=(rows // tm,),
        in_specs=[pl.BlockSpec((tm, D_MODEL), lambda i: (i, 0)),
                  pl.BlockSpec((1, D_MODEL), lambda i: (0, 0)),
                  pl.BlockSpec(w_rows.shape, lambda i: (0, 0)),
                  pl.BlockSpec(w_t.shape, lambda i: (0, 0))] + [tab_spec] * 4,
        out_specs=[pl.BlockSpec((tm, 3 * BLK), lambda i: (i, 0)),
                   pl.BlockSpec((tm, N_PBLK * BLK), lambda i: (i, 0)),
                   vt_spec(LANES), vt_spec(BLK)],
        out_shape=[jax.ShapeDtypeStruct((rows, 3 * BLK), BF16),
                   jax.ShapeDtypeStruct((rows, N_PBLK * BLK), BF16),
                   jax.ShapeDtypeStruct((B, S // TB, LANES, TB), BF16),
                   jax.ShapeDtypeStruct((B, S // TB, BLK, TB), BF16)],
        compiler_params=pltpu.CompilerParams(dimension_semantics=("arbitrary",),
                                             vmem_limit_bytes=VMEM_LIMIT),
        name="in_projection",
    )(x2, g_pre.reshape(1, D_MODEL), w_rows, w_t, *tabs)


def _head_masks(shape):
    lane = lax.broadcasted_iota(jnp.int32, shape, 1)
    return lane < HEAD_DIM


def _masked_heads(q_ref, lo_mask):
    out = []
    for h in range(N_HEADS):
        pair, e = divmod(h, 2)
        qp = q_ref[:, pair * LANES:(pair + 1) * LANES]
        keep = lo_mask if e == 0 else jnp.logical_not(lo_mask)
        out.append(jnp.where(keep, qp, jnp.zeros_like(qp)))
    return out


def _flash_kernel(q_ref, k_ref, v_ref, o_ref, lse_ref, *, T, window, n_blocks):
    i = pl.program_id(2)
    lo = _head_masks((T, LANES))
    qm = _masked_heads(q_ref, lo)
    row = lax.broadcasted_iota(jnp.int32, (T, T), 0)
    col = lax.broadcasted_iota(jnp.int32, (T, T), 1)

    def scores(j):
        rows = pl.ds(pl.multiple_of(j * T, T), T)
        return tuple(_dot_nt(qm[h], k_ref[rows, (h // 2) * LANES:(h // 2 + 1) * LANES])
                     for h in range(N_HEADS))

    def absorb(states, s_all, j, mask):
        rows = pl.ds(pl.multiple_of(j * T, T), T)
        probs = []
        for h in range(N_HEADS):
            m, l, _ = states[h]
            s = jnp.where(mask, s_all[h], NEG)
            m_new = jnp.maximum(m, jnp.max(s, axis=-1, keepdims=True))
            alpha = jnp.exp(m - m_new)
            p = jnp.where(mask, jnp.exp(s - m_new), 0.0)
            probs.append((m_new, alpha, alpha * l + jnp.sum(p, axis=-1, keepdims=True), p.astype(BF16)))
        new = []
        for h in range(N_HEADS):
            m_new, alpha, l_new, p = probs[h]
            cols = slice((h // 2) * LANES, (h // 2 + 1) * LANES)
            new.append((m_new, l_new, alpha * states[h][2] + _dot(p, v_ref[rows, cols])))
        return tuple(new)

    assert window == T
    states = tuple((jnp.full((T, 1), NEG, F32), jnp.zeros((T, 1), F32), jnp.zeros((T, LANES), F32))
                   for _ in range(N_HEADS))
    if n_blocks > 1:
        j = i - 1
        jc = jnp.maximum(j, 0)
        s_band = scores(jc)
        s_diag = scores(i)
        limit = jnp.where(j >= 0, window - T, -2 * T)
        states = absorb(states, s_band, jc, (row - col) <= limit)
    else:
        s_diag = scores(i)
    states = absorb(states, s_diag, i, col <= row)

    for pair in range(2):
        outs, lses = [], []
        for e in range(2):
            m, l, acc = states[2 * pair + e]
            outs.append(acc * (1.0 / l))
            lses.append(m + jnp.log(l))
        o_ref[:, pair * LANES:(pair + 1) * LANES] = jnp.where(lo, outs[0], outs[1]).astype(BF16)
        lse_ref[:, pair * LANES:(pair + 1) * LANES] = jnp.where(
            lo, jnp.broadcast_to(lses[0], (T, LANES)), jnp.broadcast_to(lses[1], (T, LANES)))


def _flash(view, *, B, L, ncls, T, window, name):
    spec = lambda rows, rowmap, part: pl.BlockSpec(
        (None, rows, BLK), lambda b, c, i: (b, rowmap(i), 3 * c + part))
    out_spec = pl.BlockSpec((None, T, BLK), lambda b, c, i: (b, i, c))
    return pl.pallas_call(
        functools.partial(_flash_kernel, T=T, window=window, n_blocks=L // T),
        grid=(B, ncls, L // T),
        in_specs=[spec(T, lambda i: i, 0), spec(L, lambda i: 0, 1), spec(L, lambda i: 0, 2)],
        out_specs=[out_spec, out_spec],
        out_shape=[jax.ShapeDtypeStruct((B, L, ncls * BLK), BF16),
                   jax.ShapeDtypeStruct((B, L, ncls * BLK), F32)],
        compiler_params=pltpu.CompilerParams(
            dimension_semantics=("arbitrary", "arbitrary", "arbitrary"),
            vmem_limit_bytes=VMEM_LIMIT),
        name=name,
    )(view, view, view)


def _flash_t_kernel(*refs, window, kshared, vrow0, aux):
    it = iter(refs)
    q_ref = next(it)
    qx_ref = next(it) if aux else None
    k_ref = next(it)
    kx_ref = next(it) if aux else None
    vt_ref = next(it)
    o_ref = next(it)
    T = TB

    i = pl.program_id(1)
    lo = _head_masks((T, LANES))
    qcat = _masked_heads(q_ref, lo)
    if aux == "per_head":
        lane = lax.broadcasted_iota(jnp.int32, (T, LANES), 1)
        qx = qx_ref[...]
        assert LANES // N_HEADS == 32
        qcat = [jnp.concatenate(
            [qcat[h], jnp.where((lane >> 5) == h, qx, jnp.zeros_like(qx))], axis=1)
            for h in range(N_HEADS)]
    elif aux == "shared":
        qx = qx_ref[...]
        qcat = [jnp.concatenate([qcat[h], qx], axis=1) for h in range(N_HEADS)]
    krow = lax.broadcasted_iota(jnp.int32, (T, T), 0)
    qcol = lax.broadcasted_iota(jnp.int32, (T, T), 1)
    ones = jnp.ones((ONES_ROWS, T), BF16)

    def scores(j):
        rows = pl.ds(pl.multiple_of(j * T, T), T)
        out = []
        for h in range(N_HEADS):
            pair = h // 2
            cols = slice(0, LANES) if kshared else slice(pair * LANES, (pair + 1) * LANES)
            kb = k_ref[rows, cols]
            if aux:
                kb = jnp.concatenate([kb, kx_ref[rows, :]], axis=1)
            out.append(_dot_nt(kb, qcat[h]))
        return tuple(out)

    def absorb(states, s_all, j, mask):
        vt_all = vt_ref[j]
        probs = []
        for h in range(N_HEADS):
            m, _ = states[h]
            s = s_all[h]
            if mask is not None:
                s = jnp.where(mask, s, NEG)
            m_new = jnp.maximum(m, jnp.max(s, axis=0, keepdims=True))
            alpha = jnp.exp2(m - m_new)
            p = jnp.exp2(s - m_new)
            if mask is not None:
                p = jnp.where(mask, p, 0.0)
            probs.append((m_new, alpha, p.astype(BF16)))
        new = []
        for h in range(N_HEADS):
            m_new, alpha, p = probs[h]
            r0 = vrow0 if kshared else h * HEAD_DIM
            lhs = jnp.concatenate([vt_all[r0:r0 + HEAD_DIM, :], ones], axis=0)
            acc_new = alpha * states[h][1] + _dot(lhs, p)
            new.append((m_new, acc_new))
        return tuple(new)

    states = tuple((jnp.full((1, T), NEG, F32), jnp.zeros((HEAD_DIM + ONES_ROWS, T), F32))
                   for _ in range(N_HEADS))
    if window is None:
        first = 0
        s_cur = scores(first)
    else:
        o_full = (window - T + 1) // T
        o_none = (window + T - 1) // T + 1
        assert o_none - o_full == 2
        o = o_full + 1
        j = i - o
        jc = jnp.maximum(j, 0)
        first = jnp.maximum(i - o_full, 0)
        s_band = scores(jc)
        s_cur = scores(first)
        limit = jnp.where(j >= 0, window - o * T, -2 * T)
        states = absorb(states, s_band, jc, (qcol - krow) <= limit)

    def body(j, carry):
        st, s_now = carry
        s_next = scores(j + 1)
        return absorb(st, s_now, j, None), s_next

    states, s_cur = lax.fori_loop(first, i, body, (states, s_cur))
    states = absorb(states, s_cur, i, krow <= qcol)

    for pair in range(2):
        outs = []
        for e in range(2):
            _, acc = states[2 * pair + e]
            outs.append(acc[0:HEAD_DIM, :] * (1.0 / acc[HEAD_DIM:HEAD_DIM + 1, :]))
        o_ref[:, pair * LANES:(pair + 1) * LANES] = jnp.concatenate(outs, axis=0).T.astype(BF16)


def _flash_t(q, qcol, k, kcol, kw, vt, *, B, S, window=None, vrow0=0, aux=None, name):
    kshared = kw == LANES
    in_specs = [pl.BlockSpec((None, TB, BLK), lambda b, i: (b, i, qcol))]
    args = [q]
    if aux is not None:
        kind, qx, qxcol, kx, kxcol = aux
        in_specs.append(pl.BlockSpec((None, TB, LANES), lambda b, i: (b, i, qxcol)))
        args.append(qx)
    in_specs.append(pl.BlockSpec((None, S, kw), lambda b, i: (b, 0, kcol)))
    args.append(k)
    if aux is not None:
        if kx.ndim == 3:
            in_specs.append(pl.BlockSpec((None, S, LANES), lambda b, i: (b, 0, kxcol)))
        else:
            in_specs.append(pl.BlockSpec((S, LANES), lambda b, i: (0, kxcol)))
        args.append(kx)
    in_specs.append(pl.BlockSpec((None,) + vt.shape[1:], lambda b, i: (b, 0, 0, 0)))
    args.append(vt)
    return pl.pallas_call(
        functools.partial(_flash_t_kernel, window=window, kshared=kshared, vrow0=vrow0,
                          aux=None if aux is None else aux[0]),
        grid=(B, S // TB),
        in_specs=in_specs,
        out_specs=pl.BlockSpec((None, TB, BLK), lambda b, i: (b, i, 0)),
        out_shape=jax.ShapeDtypeStruct((B, S, BLK), BF16),
        compiler_params=pltpu.CompilerParams(dimension_semantics=("arbitrary", "arbitrary"),
                                             vmem_limit_bytes=VMEM_LIMIT),
        name=name,
    )(*args)


def _mla_up_kernel(cq_ref, ckv_ref, gq_ref, gkv_ref, wq_ref, wk_ref, wvt_ref, c32_ref, s32_ref,
                   qn_ref, qr_ref, kn_ref, vt_ref):
    def norm(t, g):
        t = t.astype(F32)
        ms = jnp.mean(t * t, axis=-1, keepdims=True)
        return ((t * lax.rsqrt(ms + EPS)) * g).astype(BF16)

    scale = LOG2E * (MLA_NOPE + MLA_ROPE) ** -0.5
    q = _dot(norm(cq_ref[...], gq_ref[...]), wq_ref[...])
    qn_ref[...] = (q[:, :BLK] * scale).astype(BF16)
    qr_ref[...] = (_rope_half(q[:, BLK:], c32_ref[...], s32_ref[...], MLA_ROPE) * scale).astype(BF16)
    ckv = norm(ckv_ref[...], gkv_ref[...])
    kn_ref[...] = _dot(ckv, wk_ref[...]).astype(BF16)
    vt = _dot_nt(wvt_ref[...], ckv)
    for blk in range(vt.shape[1] // TB):
        vt_ref[blk] = vt[:, blk * TB:(blk + 1) * TB].astype(BF16)


def _mla_up(proj, g_q, g_kv, w_uq, w_ukv, c32, s32, B, S):
    rows = proj.shape[0]
    tm = 1024
    per_b = S // tm
    wq = w_uq.reshape(MLA_Q_RANK, N_HEADS, MLA_NOPE + MLA_ROPE)
    wq = jnp.concatenate([wq[:, :, :MLA_NOPE].reshape(MLA_Q_RANK, -1),
                          wq[:, :, MLA_NOPE:].reshape(MLA_Q_RANK, -1)], axis=1).astype(BF16)
    wkv = w_ukv.reshape(MLA_KV_RANK, N_HEADS, MLA_NOPE + MLA_V)
    wk = wkv[:, :, :MLA_NOPE].reshape(MLA_KV_RANK, -1).astype(BF16)
    wvt = wkv[:, :, MLA_NOPE:].reshape(MLA_KV_RANK, -1).T.astype(BF16)
    row_spec = lambda w, c: pl.BlockSpec((tm, w), lambda i: (i, c))
    full = lambda a: pl.BlockSpec(a.shape, lambda i: (0, 0))
    gq = g_q.reshape(1, -1)
    gkv = g_kv.reshape(1, -1)
    outs = [jax.ShapeDtypeStruct((rows, BLK), BF16), jax.ShapeDtypeStruct((rows, LANES), BF16),
            jax.ShapeDtypeStruct((rows, BLK), BF16),
            jax.ShapeDtypeStruct((B, S // TB, BLK, TB), BF16)]
    return pl.pallas_call(
        _mla_up_kernel,
        grid=(rows // tm,),
        in_specs=[row_spec(BLK, PB_BCQ), row_spec(LANES, 2 * PB_BKV), full(gq), full(gkv),
                  full(wq), full(wk), full(wvt), row_spec(LANES, 0), row_spec(LANES, 0)],
        out_specs=[row_spec(BLK, 0), row_spec(LANES, 0), row_spec(BLK, 0),
                   pl.BlockSpec((None, tm // TB, BLK, TB), lambda i: (i // per_b, i % per_b, 0, 0))],
        out_shape=outs,
        compiler_params=pltpu.CompilerParams(dimension_semantics=("arbitrary",),
                                             vmem_limit_bytes=VMEM_LIMIT),
        name="mla_up",
    )(proj, proj, gq, gkv, wq, wk, wvt, c32, s32)


def _compress_kernel(x_ref, pos_ref, w1_ref, w2_ref, kc_ref, vc_ref):
    x = x_ref[...].astype(F32)
    half = x.shape[1]
    top = (x + pos_ref[0:1, :]).astype(BF16)
    bot = (x + pos_ref[1:2, :]).astype(BF16)
    y_top = _dot(top, w1_ref[0:half, :])
    y_bot = _dot(bot, w1_ref[half:2 * half, :])
    pre = y_top + pltpu.roll(y_bot, y_bot.shape[0] - 1, 0)
    hid = (pre * (1.0 / (1.0 + jnp.exp(-pre)))).astype(BF16)
    kc_ref[...] = _dot(hid[:, :NSA_CMP_HID], w2_ref[0]).astype(BF16)
    vc_ref[...] = _dot(hid[:, NSA_CMP_HID:], w2_ref[1]).astype(BF16)


def _compress(kcvc, pos_k, pos_v, kw1, kw2, vw1, vw2, B, S):
    n16 = S // NSA_CMP_STRIDE
    x16 = kcvc.reshape(B, n16, NSA_CMP_STRIDE * LANES)
    pos = jnp.concatenate([pos_k, pos_v], axis=1).reshape(2, NSA_CMP_STRIDE * LANES)
    w1 = jnp.zeros((NSA_CMP_LEN, LANES, 2 * NSA_CMP_HID), F32)
    w1 = w1.at[:, :HEAD_DIM, :NSA_CMP_HID].set(kw1.reshape(NSA_CMP_LEN, HEAD_DIM, NSA_CMP_HID))
    w1 = w1.at[:, HEAD_DIM:, NSA_CMP_HID:].set(vw1.reshape(NSA_CMP_LEN, HEAD_DIM, NSA_CMP_HID))
    w1 = w1.reshape(NSA_CMP_LEN * LANES, 2 * NSA_CMP_HID).astype(BF16)
    w2 = jnp.stack([jnp.tile(kw2, (1, 2)), jnp.tile(vw2, (1, 2))]).astype(BF16)
    out = jax.ShapeDtypeStruct((B, n16, LANES), BF16)
    return pl.pallas_call(
        _compress_kernel,
        grid=(B,),
        in_specs=[pl.BlockSpec((None, n16, NSA_CMP_STRIDE * LANES), lambda b: (b, 0, 0)),
                  pl.BlockSpec(pos.shape, lambda b: (0, 0)),
                  pl.BlockSpec(w1.shape, lambda b: (0, 0)),
                  pl.BlockSpec(w2.shape, lambda b: (0, 0, 0))],
        out_specs=[pl.BlockSpec((None, n16, LANES), lambda b: (b, 0, 0))] * 2,
        out_shape=[out, out],
        compiler_params=pltpu.CompilerParams(dimension_semantics=("arbitrary",),
                                             vmem_limit_bytes=VMEM_LIMIT),
        name="nsa_compress",
    )(x16, pos, w1, w2)


def _cmp_topk_kernel(q_ref, kc_ref, vc_ref, ovt_ref, o_ref, nsel_ref, *, T, n_cmp, n_slc):
    i = pl.program_id(1)
    t0 = i * T
    lane = lax.broadcasted_iota(jnp.int32, (T, LANES), 1)
    tok = t0 + lax.broadcasted_iota(jnp.int32, (T, LANES), 0)
    ok = jnp.logical_and(lane * NSA_CMP_STRIDE + (NSA_CMP_LEN - 1) <= tok, lane < n_cmp)
    lo = lane < HEAD_DIM
    qm = _masked_heads(q_ref, lo)
    kc = kc_ref[...]
    vc = vc_ref[...]
    psum = jnp.zeros((T, LANES), F32)
    outs = []
    for h in range(N_HEADS):
        s = jnp.where(ok, _dot_nt(qm[h], kc), NEG)
        mx = jnp.max(s, axis=-1, keepdims=True)
        e = jnp.where(ok, jnp.exp2(s - mx), 0.0)
        den = jnp.sum(e, axis=-1, keepdims=True)
        p = e * (1.0 / jnp.maximum(den, 1e-30))
        psum = psum + p
        outs.append(_dot(p.astype(BF16), vc))
    for pair in range(2):
        o_ref[:, pair * LANES:(pair + 1) * LANES] = jnp.where(
            lo, outs[2 * pair], outs[2 * pair + 1]).astype(BF16)

    p_hi = psum.astype(BF16)
    p_lo = (psum - p_hi.astype(F32)).astype(BF16)
    ovt = ovt_ref[...]
    imp = _dot_nt(ovt, p_hi) + _dot_nt(ovt, p_lo)
    jj = lax.broadcasted_iota(jnp.int32, (n_slc, T), 0)
    bt = (t0 + lax.broadcasted_iota(jnp.int32, (n_slc, T), 1)) >> 6
    forced = jnp.logical_or(jj == 0, jnp.logical_or(jj == bt, jj == bt - 1))
    val = jnp.where(forced, BIG, jnp.where(jj > bt, -BIG, imp))
    sel_rows = []
    for j in range(n_slc):
        vj = val[j:j + 1, :]
        beats = jnp.logical_or(val > vj, jnp.logical_and(val == vj, jj < j))
        rank = jnp.sum(beats.astype(F32), axis=0, keepdims=True)
        sel_rows.append((rank >= NSA_N_SEL).astype(F32))
    sel_rows.append(jnp.zeros((LANES - n_slc, T), F32))
    nsel_ref[...] = jnp.concatenate(sel_rows, axis=0).T.astype(BF16)


def _cmp_topk(proj3, kc2, vc2, B, S):
    T = TB
    n_cmp = (S - NSA_CMP_LEN) // NSA_CMP_STRIDE + 1
    n_slc = S // NSA_SEL_LEN
    assert NSA_SEL_LEN == 64 and n_slc <= LANES and n_cmp <= LANES
    ci = np.arange(LANES)[None, :] * NSA_CMP_STRIDE
    sj = np.arange(n_slc)[:, None] * NSA_SEL_LEN
    ovt = ((ci < sj + NSA_SEL_LEN) & (ci + NSA_CMP_LEN > sj) & (np.arange(LANES)[None, :] < n_cmp))
    ovt = jnp.asarray(ovt.astype(np.float32), BF16)
    return pl.pallas_call(
        functools.partial(_cmp_topk_kernel, T=T, n_cmp=n_cmp, n_slc=n_slc),
        grid=(B, S // T),
        in_specs=[pl.BlockSpec((None, T, BLK), lambda b, i: (b, i, PB_CQ)),
                  pl.BlockSpec((None, LANES, LANES), lambda b, i: (b, 0, 0)),
                  pl.BlockSpec((None, LANES, LANES), lambda b, i: (b, 0, 0)),
                  pl.BlockSpec(ovt.shape, lambda b, i: (0, 0))],
        out_specs=[pl.BlockSpec((None, T, BLK), lambda b, i: (b, i, 0)),
                   pl.BlockSpec((None, T, LANES), lambda b, i: (b, i, 0))],
        out_shape=[jax.ShapeDtypeStruct((B, S, BLK), BF16),
                   jax.ShapeDtypeStruct((B, S, LANES), BF16)],
        compiler_params=pltpu.CompilerParams(dimension_semantics=("arbitrary", "arbitrary"),
                                             vmem_limit_bytes=VMEM_LIMIT),
        name="nsa_cmp_topk",
    )(proj3, kc2, vc2, ovt)


def _stick_kernel(q_ref, k_ref, vt_ref, u_ref, o_ref):
    T = TB
    i = pl.program_id(1)
    lo = _head_masks((T, LANES))
    qm = _masked_heads(q_ref, lo)
    krow = lax.broadcasted_iota(jnp.int32, (T, T), 0)
    qcol = lax.broadcasted_iota(jnp.int32, (T, T), 1)
    strict = krow < qcol
    u = u_ref[...]

    def scores(j):
        rows = pl.ds(pl.multiple_of(j * T, T), T)
        return tuple(_dot_nt(k_ref[rows, (h // 2) * LANES:(h // 2 + 1) * LANES], qm[h])
                     for h in range(N_HEADS))

    def absorb(states, z_all, j, diag):
        vt_all = vt_ref[j]
        split = []
        for h in range(N_HEADS):
            z = z_all[h]
            sp = jnp.maximum(z, 0.0) + jnp.log(1.0 + jnp.exp(-jnp.abs(z)))
            log_keep = -sp
            if diag:
                log_keep = jnp.where(strict, log_keep, 0.0)
            hi = log_keep.astype(BF16)
            split.append((z - sp, hi, (log_keep - hi.astype(F32)).astype(BF16)))
        sums = [_dot(u, hi) + _dot(u, lo_part) for _, hi, lo_part in split]
        weights = []
        for h in range(N_HEADS):
            a = jnp.exp(split[h][0] + (sums[h][0:T, :] + states[h][0]))
            if diag:
                a = jnp.where(strict, a, 0.0)
            weights.append(a.astype(BF16))
        new = []
        for h in range(N_HEADS):
            carry, acc = states[h]
            acc = acc + _dot(vt_all[h * HEAD_DIM:(h + 1) * HEAD_DIM, :], weights[h])
            new.append((carry + sums[h][T:T + 1, :], acc))
        return tuple(new)

    states = tuple((jnp.zeros((1, T), F32), jnp.zeros((HEAD_DIM, T), F32)) for _ in range(N_HEADS))
    z_diag = scores(i)
    z_cur = scores(jnp.maximum(i - 1, 0))
    states = absorb(states, z_diag, i, True)

    def body(jj, carry):
        st, z_now = carry
        j = i - 1 - jj
        z_next = scores(jnp.maximum(j - 1, 0))
        return absorb(st, z_now, j, False), z_next

    states, _ = lax.fori_loop(0, i, body, (states, z_cur))
    for pair in range(2):
        o_t = jnp.concatenate([states[2 * pair][1], states[2 * pair + 1][1]], axis=0)
        o_ref[:, pair * LANES:(pair + 1) * LANES] = o_t.T.astype(BF16)


def _stick_breaking(proj3, vdt, B, S):
    u = np.zeros((TB + ONES_ROWS, TB), np.float32)
    u[:TB] = np.triu(np.ones((TB, TB), np.float32), 1)
    u[TB:] = 1.0
    return pl.pallas_call(
        _stick_kernel,
        grid=(B, S // TB),
        in_specs=[pl.BlockSpec((None, TB, BLK), lambda b, i: (b, i, PB_DQ)),
                  pl.BlockSpec((None, S, BLK), lambda b, i: (b, 0, PB_DK)),
                  pl.BlockSpec((None,) + vdt.shape[1:], lambda b, i: (b, 0, 0, 0)),
                  pl.BlockSpec(u.shape, lambda b, i: (0, 0))],
        out_specs=pl.BlockSpec((None, TB, BLK), lambda b, i: (b, i, 0)),
        out_shape=jax.ShapeDtypeStruct((B, S, BLK), BF16),
        compiler_params=pltpu.CompilerParams(dimension_semantics=("arbitrary", "arbitrary"),
                                             vmem_limit_bytes=VMEM_LIMIT),
        name="stick_breaking",
    )(proj3, proj3, vdt, jnp.asarray(u, BF16))


def _outproj_kernel(x_ref, oa1, la1, oa2, la2, oa3, la3, ob, oc, os_, ow, od,
                    ga, gb, gcc, gcs, gcw, gc, gd, w_ref, g_ref, out_ref):
    f = lambda r: r[...].astype(F32)
    sig = lambda t: 1.0 / (1.0 + jnp.exp(-t))
    silu = lambda t: t * sig(t)
    l1, l2, l3 = la1[...], la2[...], la3[...]
    mx = jnp.maximum(l1, jnp.maximum(l2, l3))
    w1, w2, w3 = jnp.exp(l1 - mx), jnp.exp(l2 - mx), jnp.exp(l3 - mx)
    o_a = (w1 * f(oa1) + w2 * f(oa2) + w3 * f(oa3)) / (w1 + w2 + w3)
    o_c = sig(f(gcc)) * f(oc) + sig(f(gcs)) * f(os_) + sig(f(gcw)) * f(ow)
    mixed = [o_a * silu(f(ga)), f(ob) * silu(f(gb)), o_c * silu(f(gc)), f(od) * silu(f(gd))]
    y = _dot(mixed[0].astype(BF16), w_ref[0:BLK, :])
    for g in range(1, 4):
        y = y + _dot(mixed[g].astype(BF16), w_ref[g * BLK:(g + 1) * BLK, :])
    ms = jnp.mean(y * y, axis=-1, keepdims=True)
    out_ref[...] = x_ref[...] + (y * lax.rsqrt(ms + EPS)) * g_ref[...]


def _out_projection(x2, a_parts, o_b, c_parts, o_d, proj, w_out, g_post):
    rows = x2.shape[0]
    tm = 512
    blk = lambda c: pl.BlockSpec((tm, BLK), lambda i: (i, c))
    a_args, a_specs = [], []
    for o, lse in a_parts:
        a_args += [o, lse]
        a_specs += [blk(0), blk(0)]
    c_args = list(c_parts)
    gates = [PB_AG, PB_BG, PB_GC, PB_GS, PB_GW, PB_CG, PB_DG]
    return pl.pallas_call(
        _outproj_kernel,
        grid=(rows // tm,),
        in_specs=[pl.BlockSpec((tm, D_MODEL), lambda i: (i, 0))] + a_specs
                 + [blk(0)] * (1 + len(c_args) + 1) + [blk(c) for c in gates]
                 + [pl.BlockSpec((D_MODEL, D_MODEL), lambda i: (0, 0)),
                    pl.BlockSpec((1, D_MODEL), lambda i: (0, 0))],
        out_specs=pl.BlockSpec((tm, D_MODEL), lambda i: (i, 0)),
        out_shape=jax.ShapeDtypeStruct((rows, D_MODEL), F32),
        compiler_params=pltpu.CompilerParams(dimension_semantics=("arbitrary",),
                                             vmem_limit_bytes=VMEM_LIMIT),
        name="out_projection",
    )(x2, *a_args, o_b, *c_args, o_d, *([proj] * len(gates)), w_out.astype(BF16),
      g_post.reshape(1, D_MODEL))


def _layer(x2, tabs, B, S, w_in, w_out, g_pre, g_post, mla_g_q, mla_g_kv, mla_w_uq, mla_w_ukv,
           nsa_pos_k, nsa_pos_v, nsa_k_w1, nsa_k_w2, nsa_v_w1, nsa_v_w2):
    rows = B * S
    c64, s64, c32, s32 = tabs
    w_rows, w_t = _pack_w_in(w_in)
    aqkv, proj, vct, vdt = _in_projection(x2, g_pre, w_rows, w_t, tabs, B, S)
    proj3 = proj.reshape(B, S, N_PBLK * BLK)

    a_parts = []
    for win, dil in ((128, 1), (512, 4), (2048, 16)):
        L = S // dil
        o, lse = _flash(aqkv.reshape(B, L, dil * 3 * BLK), B=B, L=L, ncls=dil, T=128,
                        window=win // dil, name=f"dilated_r{dil}")
        a_parts.append((o.reshape(rows, BLK), lse.reshape(rows, BLK)))

    qn, qr, kn, vbt = _mla_up(proj, mla_g_q, mla_g_kv, mla_w_uq, mla_w_ukv, c32, s32, B, S)
    r3 = lambda t: t.reshape(B, S, t.shape[-1])
    o_b = _flash_t(r3(qn), 0, r3(kn), 0, BLK, vbt, B=B, S=S,
                   aux=("per_head", r3(qr), 0, proj3, 2 * PB_BKV + 1), name="mla_attention")

    kcvc = proj3[:, :, PB_C0 * BLK:PB_C0 * BLK + LANES]
    kc2, vc2 = _compress(kcvc, nsa_pos_k, nsa_pos_v, nsa_k_w1, nsa_k_w2, nsa_v_w1, nsa_v_w2, B, S)
    o_cmp, nsel = _cmp_topk(proj3, kc2, vc2, B, S)
    penalty = np.zeros((S, LANES), np.float32)
    penalty[np.arange(S), np.arange(S) // NSA_SEL_LEN] = MASK_PENALTY
    o_slc = _flash_t(proj3, PB_CQ, proj3, 2 * PB_C0 + 1, LANES, vct, B=B, S=S, vrow0=0,
                     aux=("shared", nsel, 0, jnp.asarray(penalty, BF16), 0), name="nsa_selected")
    o_win = _flash_t(proj3, PB_CQ, proj3, 2 * PB_C1, LANES, vct, B=B, S=S, vrow0=HEAD_DIM,
                     window=NSA_WINDOW - 1, name="nsa_window")

    o_d = _stick_breaking(proj3, vdt, B, S)

    flat = lambda t: t.reshape(rows, BLK)
    return _out_projection(x2, a_parts, flat(o_b), (flat(o_cmp), flat(o_slc), flat(o_win)),
                           flat(o_d), proj, w_out, g_post)


def kernel(x, positions, w_in, w_out, g_pre, g_post, mla_g_q, mla_g_kv, mla_w_uq, mla_w_ukv,
           nsa_pos_k, nsa_pos_v, nsa_k_w1, nsa_k_w2, nsa_v_w1, nsa_v_w2):
    B, S, D = x.shape
    assert D == D_MODEL and S % 1024 == 0
    x2 = x.reshape(B * S, D)
    tabs = _rope_tables(positions)
    for l in range(w_in.shape[0]):
        x2 = _layer(x2, tabs, B, S, w_in[l], w_out[l], g_pre[l], g_post[l],
                    mla_g_q[l], mla_g_kv[l], mla_w_uq[l], mla_w_ukv[l],
                    nsa_pos_k[l], nsa_pos_v[l], nsa_k_w1[l], nsa_k_w2[l],
                    nsa_v_w1[l], nsa_v_w2[l])
    return x2.reshape(B, S, D)
```

```python
import functools
import math

import numpy as np
import jax
import jax.numpy as jnp
from jax import lax
from jax.experimental import pallas as pl
from jax.experimental.pallas import tpu as pltpu

D_MODEL = 1024
HEAD_DIM = 64
N_HEADS = 4
ROPE_THETA = 10000.0
EPS = 1e-6
NEG = -1e30
LOG2E = math.log2(math.e)
MASK_PENALTY = -(2.0 ** 100)
BIG = 1e9

MLA_Q_RANK = 256
MLA_KV_RANK = 128
MLA_NOPE = 64
MLA_ROPE = 32
MLA_V = 64

NSA_CMP_LEN = 32
NSA_CMP_STRIDE = 16
NSA_CMP_HID = 256
NSA_SEL_LEN = 64
NSA_N_SEL = 16
NSA_WINDOW = 512

LANES = 128
BLK = 256
TILE = 512
DILATIONS = (1, 4, 16)
DIL_BLOCK = 128
LSE_PARTS = 3
TB = 256
ONES_ROWS = 16
VMEM_LIMIT = 48 * 1024 * 1024

F32 = jnp.float32
BF16 = jnp.bfloat16

(PB_AG, PB_BCQ, PB_BKV, PB_BG, PB_CQ, PB_C0, PB_C1, PB_CG, PB_DQ, PB_DK, PB_DG) = range(11)
N_PBLK = 11
N_BRANCH = 3
ROPE_NONE, ROPE_64, ROPE_32, ROPE_64_LO = 0, 1, 2, 3


def _dot_nt(a, b):
    return lax.dot_general(a, b, (((1,), (1,)), ((), ())), preferred_element_type=F32)


def _dot(a, b):
    return jnp.dot(a, b, preferred_element_type=F32)


def _rope_half(y, cos, sin_signed, group):
    half = group // 2
    lane = lax.broadcasted_iota(jnp.int32, y.shape, 1)
    first = (lane & (group - 1)) < half
    partner = jnp.where(first, pltpu.roll(y, LANES - half, 1), pltpu.roll(y, half, 1))
    return y * cos + partner * sin_signed


def _rope_table_kernel(pos_ref, const_ref, c64_ref, s64_ref, c32_ref, s32_ref):
    pos = pos_ref[...].astype(F32)
    a64 = pos * const_ref[0:1, :]
    a32 = pos * const_ref[2:3, :]
    c64_ref[...] = jnp.cos(a64)
    s64_ref[...] = jnp.sin(a64) * const_ref[1:2, :]
    c32_ref[...] = jnp.cos(a32)
    s32_ref[...] = jnp.sin(a32) * const_ref[3:4, :]


def _rope_tables(positions):
    rows = positions.size
    tm = 1024
    lane = np.arange(LANES)
    inv64 = ROPE_THETA ** (-(lane % 32).astype(np.float64) / 32.0)
    sgn64 = np.where((lane % 64) < 32, -1.0, 1.0)
    inv32 = ROPE_THETA ** (-(lane % 16).astype(np.float64) / 16.0)
    sgn32 = np.where((lane % 32) < 16, -1.0, 1.0)
    const = np.zeros((8, LANES), np.float32)
    const[0], const[1], const[2], const[3] = inv64, sgn64, inv32, sgn32
    out = jax.ShapeDtypeStruct((rows, LANES), F32)
    tab_spec = pl.BlockSpec((tm, LANES), lambda i: (i, 0))
    return pl.pallas_call(
        _rope_table_kernel,
        grid=(rows // tm,),
        in_specs=[pl.BlockSpec((tm, 1), lambda i: (i, 0)),
                  pl.BlockSpec((8, LANES), lambda i: (0, 0))],
        out_specs=[tab_spec] * 4,
        out_shape=[out] * 4,
        name="rope_tables",
    )(positions.reshape(rows, 1), jnp.asarray(const))


def _inproj_kernel(x_ref, g_ref, w_ref, wt_ref, perm_ref, c64_ref, s64_ref, c32_ref, s32_ref,
                   aqkv_ref, aqkv4_ref, aqkv16_ref, proj_ref, vct_ref, vdt_ref, *, plan_a, plan_p):
    x = x_ref[...]
    tm = x.shape[0]
    ms = jnp.mean(x * x, axis=-1, keepdims=True)
    h = ((x * lax.rsqrt(ms + EPS)) * g_ref[...]).astype(BF16)
    lane = lax.broadcasted_iota(jnp.int32, (tm, LANES), 1)

    def epilogue(y, ops, scale):
        halves = []
        for hf, op in enumerate(ops):
            yh = y[:, hf * LANES:(hf + 1) * LANES]
            if op == ROPE_64:
                yh = _rope_half(yh, c64_ref[...], s64_ref[...], 64)
            elif op == ROPE_32:
                yh = _rope_half(yh, c32_ref[...], s32_ref[...], 32)
            elif op == ROPE_64_LO:
                yh = jnp.where(lane < 64, _rope_half(yh, c64_ref[...], s64_ref[...], 64), yh)
            if scale != 1.0:
                yh = yh * scale
            halves.append(yh)
        return jnp.concatenate(halves, axis=1).astype(BF16)

    col = 0
    for out_ref, plan in ((aqkv_ref, plan_a), (proj_ref, plan_p)):
        for b, (ops, scale) in enumerate(plan):
            y = _dot(h, w_ref[:, col:col + BLK])
            out_ref[:, b * BLK:(b + 1) * BLK] = epilogue(y, ops, scale)
            col += BLK

    a_rows = aqkv_ref[...]
    aqkv4_ref[...] = _dot(perm_ref[0], a_rows).astype(BF16)
    aqkv16_ref[...] = _dot(perm_ref[1], a_rows).astype(BF16)

    vt = _dot_nt(wt_ref[...], h)
    for blk in range(tm // TB):
        cols = slice(blk * TB, (blk + 1) * TB)
        vct_ref[blk] = vt[0:LANES, cols].astype(BF16)
        vdt_ref[blk] = vt[LANES:LANES + BLK, cols].astype(BF16)


_PLAN_A = (((ROPE_64, ROPE_64), LOG2E * HEAD_DIM ** -0.5),
           ((ROPE_64, ROPE_64), 1.0),
           ((ROPE_NONE, ROPE_NONE), 1.0))
_PLAN_P = (
    ((ROPE_NONE, ROPE_NONE), 1.0),
    ((ROPE_NONE, ROPE_NONE), 1.0),
    ((ROPE_NONE, ROPE_32), 1.0),
    ((ROPE_NONE, ROPE_NONE), 1.0),
    ((ROPE_64, ROPE_64), LOG2E * HEAD_DIM ** -0.5),
    ((ROPE_64_LO, ROPE_64), 1.0),
    ((ROPE_64, ROPE_NONE), 1.0),
    ((ROPE_NONE, ROPE_NONE), 1.0),
    ((ROPE_NONE, ROPE_NONE), LOG2E * HEAD_DIM ** -0.5),
    ((ROPE_NONE, ROPE_NONE), 1.0),
    ((ROPE_NONE, ROPE_NONE), 1.0),
)
assert len(_PLAN_P) == N_PBLK


def _pack_w_in(w):
    o = 0
    seg = {}
    for name, width in (("aq", 256), ("ak", 256), ("av", 256), ("ag", 256),
                        ("bcq", 256), ("bckv", 128), ("bkr", 32), ("bg", 256),
                        ("cq", 256), ("ckc", 64), ("cvc", 64), ("cks", 64), ("cvs", 64),
                        ("ckw", 64), ("cvw", 64), ("cgl", 12), ("cg", 256),
                        ("dq", 256), ("dk", 256), ("dv", 256), ("dg", 256)):
        seg[name] = w[:, o:o + width]
        o += width
    gl_pad = jnp.zeros((D_MODEL, LANES - N_HEADS * N_BRANCH), w.dtype)
    cols = [seg["aq"], seg["ak"], seg["av"],
            seg["ag"], seg["bcq"],
            seg["bckv"], jnp.tile(seg["bkr"], (1, 4)),
            seg["bg"], seg["cq"],
            seg["ckc"], seg["cvc"], seg["cks"], seg["cks"],
            seg["ckw"], seg["ckw"], seg["cgl"], gl_pad,
            seg["cg"], seg["dq"], seg["dk"], seg["dg"]]
    w_rows = jnp.concatenate(cols, axis=1).astype(BF16)
    w_t = jnp.concatenate([seg["cvs"], seg["cvw"], seg["dv"]], axis=1).T.astype(BF16)
    return w_rows, w_t


def _class_major_perm(dil):
    n = TILE // dil
    p = np.zeros((TILE, TILE), np.float32)
    pos = np.arange(TILE)
    p[(pos % dil) * n + pos // dil, pos] = 1.0
    return p


def _in_projection(x2, g_pre, w_rows, w_t, tabs, B, S):
    rows = x2.shape[0]
    tm = TILE
    per_b = S // tm
    nkb = tm // TB
    perm = jnp.asarray(np.stack([_class_major_perm(d) for d in DILATIONS[1:]]), BF16)
    tab_spec = pl.BlockSpec((tm, LANES), lambda i: (i, 0))
    vt_spec = lambda r: pl.BlockSpec((None, nkb, r, TB), lambda i: (i // per_b, i % per_b, 0, 0))
    a_spec = pl.BlockSpec((tm, 3 * BLK), lambda i: (i, 0))
    a_shape = jax.ShapeDtypeStruct((rows, 3 * BLK), BF16)
    return pl.pallas_call(
        functools.partial(_inproj_kernel, plan_a=_PLAN_A, plan_p=_PLAN_P),
        grid=(rows // tm,),
        in_specs=[pl.BlockSpec((tm, D_MODEL), lambda i: (i, 0)),
                  pl.BlockSpec((1, D_MODEL), lambda i: (0, 0)),
                  pl.BlockSpec(w_rows.shape, lambda i: (0, 0)),
                  pl.BlockSpec(w_t.shape, lambda i: (0, 0)),
                  pl.BlockSpec(perm.shape, lambda i: (0, 0, 0))] + [tab_spec] * 4,
        out_specs=[a_spec, a_spec, a_spec,
                   pl.BlockSpec((tm, N_PBLK * BLK), lambda i: (i, 0)),
                   vt_spec(LANES), vt_spec(BLK)],
        out_shape=[a_shape, a_shape, a_shape,
                   jax.ShapeDtypeStruct((rows, N_PBLK * BLK), BF16),
                   jax.ShapeDtypeStruct((B, S // TB, LANES, TB), BF16),
                   jax.ShapeDtypeStruct((B, S // TB, BLK, TB), BF16)],
        compiler_params=pltpu.CompilerParams(dimension_semantics=("arbitrary",),
                                             vmem_limit_bytes=VMEM_LIMIT),
        name="in_projection",
    )(x2, g_pre.reshape(1, D_MODEL), w_rows, w_t, perm, *tabs)


def _head_masks(shape):
    lane = lax.broadcasted_iota(jnp.int32, shape, 1)
    return lane < HEAD_DIM


def _masked_heads(q_ref, lo_mask):
    out = []
    for h in range(N_HEADS):
        pair, e = divmod(h, 2)
        qp = q_ref[:, pair * LANES:(pair + 1) * LANES]
        keep = lo_mask if e == 0 else jnp.logical_not(lo_mask)
        out.append(jnp.where(keep, qp, jnp.zeros_like(qp)))
    return out


def _dilated_kernel(an_ref, a4_ref, a16_ref, unperm_ref, expand_ref, o_ref, res_ref):
    T = DIL_BLOCK
    S = an_ref.shape[0]
    n_tiles = S // TILE
    lo = _head_masks((T, LANES))
    lane = lax.broadcasted_iota(jnp.int32, (T, LANES), 1)
    dist2 = (lax.broadcasted_iota(jnp.int32, (T, 2 * T), 0)
             - lax.broadcasted_iota(jnp.int32, (T, 2 * T), 1))
    causal1 = (lax.broadcasted_iota(jnp.int32, (T, T), 1)
               <= lax.broadcasted_iota(jnp.int32, (T, T), 0))

    def scores(q, k):
        qm = _masked_heads(q, lo)
        return tuple(_dot_nt(qm[h], k[:, (h // 2) * LANES:(h // 2 + 1) * LANES])
                     for h in range(N_HEADS))

    def finish(s_all, v, mask):
        probs = []
        for h in range(N_HEADS):
            s = jnp.where(mask, s_all[h], NEG)
            m = jnp.max(s, axis=-1, keepdims=True)
            p = jnp.where(mask, jnp.exp2(s - m), 0.0)
            probs.append((m, jnp.sum(p, axis=-1, keepdims=True), p.astype(BF16)))
        outs = []
        parts = jnp.zeros((T, LANES), F32)
        for h in range(N_HEADS):
            m, l, p = probs[h]
            outs.append(_dot(p, v[:, (h // 2) * LANES:(h // 2 + 1) * LANES]) * (1.0 / l))
            rest = m + jnp.log2(l)
            for part in range(LSE_PARTS):
                piece = rest.astype(BF16).astype(F32)
                parts = jnp.where(lane == LSE_PARTS * h + part, piece, parts)
                rest = rest - piece
        return jnp.concatenate(
            [jnp.where(lo, outs[0], outs[1]), jnp.where(lo, outs[2], outs[3]), parts],
            axis=1).astype(BF16)

    def band_mask(delta):
        d = dist2 + delta
        return jnp.logical_and(d >= 0, d <= DIL_BLOCK)

    def run(n_units, load_qk, load_v, store, mask_of):
        def body(u, s_now):
            s_next = scores(*load_qk(jnp.minimum(u + 1, n_units - 1)))
            store(u, finish(s_now, load_v(u), mask_of(u)))
            return s_next
        lax.fori_loop(0, n_units, body, scores(*load_qk(0)))

    def rows(start, size):
        return pl.ds(pl.multiple_of(start, 32), size)

    def k0_1(u):
        return T * jnp.maximum(u - 1, 0)

    def store1(u, r):
        res_ref[0, rows(T * u, T), :] = r

    run(S // T,
        lambda u: (an_ref[rows(T * u, T), 0:BLK], an_ref[rows(k0_1(u), 2 * T), BLK:2 * BLK]),
        lambda u: an_ref[rows(k0_1(u), 2 * T), 2 * BLK:3 * BLK],
        store1,
        lambda u: band_mask(T * u - k0_1(u)))

    def piece4(u, blk):
        return rows(TILE * blk + T * (u // n_tiles), T)

    def kb0_4(u):
        return jnp.maximum(u % n_tiles - 1, 0)

    def load4(u, cols):
        return jnp.concatenate([a4_ref[piece4(u, kb0_4(u)), cols],
                                a4_ref[piece4(u, kb0_4(u) + 1), cols]], axis=0)

    def store4(u, r):
        res_ref[1, piece4(u, u % n_tiles), :] = r

    run(4 * n_tiles,
        lambda u: (a4_ref[piece4(u, u % n_tiles), 0:BLK], load4(u, slice(BLK, 2 * BLK))),
        lambda u: load4(u, slice(2 * BLK, 3 * BLK)),
        store4,
        lambda u: band_mask(T * (u % n_tiles - kb0_4(u))))

    per = TILE // 16

    def load16(c, cols):
        return jnp.concatenate([a16_ref[rows(TILE * t + per * c, per), cols]
                                for t in range(n_tiles)], axis=0)

    def store16(c, r):
        for t in range(n_tiles):
            res_ref[2, rows(TILE * t + per * c, per), :] = r[per * t:per * (t + 1), :]

    run(16,
        lambda c: (load16(c, slice(0, BLK)), load16(c, slice(BLK, 2 * BLK))),
        lambda c: load16(c, slice(2 * BLK, 3 * BLK)),
        store16,
        lambda c: causal1)

    expand = expand_ref[...]
    for t in range(n_tiles):
        tile = slice(TILE * t, TILE * (t + 1))
        nat = [res_ref[0, tile, :].astype(F32),
               _dot(unperm_ref[0], res_ref[1, tile, :]),
               _dot(unperm_ref[1], res_ref[2, tile, :])]
        lse = [_dot(r[:, BLK:].astype(BF16), expand) for r in nat]
        top = jnp.maximum(lse[0], jnp.maximum(lse[1], lse[2]))
        w = [jnp.exp2(l - top) for l in lse]
        mixed = (w[0] * nat[0][:, :BLK] + w[1] * nat[1][:, :BLK] + w[2] * nat[2][:, :BLK])
        o_ref[tile, :] = (mixed / (w[0] + w[1] + w[2])).astype(BF16)


def _dilated(aqkv, aqkv4, aqkv16, B, S):
    assert S == 16 * DIL_BLOCK and S % TILE == 0 and TILE == 4 * DIL_BLOCK
    r3 = lambda t: t.reshape(B, S, 3 * BLK)
    unperm = jnp.asarray(np.stack([_class_major_perm(d).T for d in DILATIONS[1:]]), BF16)
    expand = np.zeros((LANES, BLK), np.float32)
    for h in range(N_HEADS):
        expand[LSE_PARTS * h:LSE_PARTS * (h + 1), h * HEAD_DIM:(h + 1) * HEAD_DIM] = 1.0
    a_spec = pl.BlockSpec((None, S, 3 * BLK), lambda b: (b, 0, 0))
    return pl.pallas_call(
        _dilated_kernel,
        grid=(B,),
        in_specs=[a_spec, a_spec, a_spec,
                  pl.BlockSpec(unperm.shape, lambda b: (0, 0, 0)),
                  pl.BlockSpec(expand.shape, lambda b: (0, 0))],
        out_specs=pl.BlockSpec((None, S, BLK), lambda b: (b, 0, 0)),
        out_shape=jax.ShapeDtypeStruct((B, S, BLK), BF16),
        scratch_shapes=[pltpu.VMEM((3, S, BLK + LANES), BF16)],
        compiler_params=pltpu.CompilerParams(dimension_semantics=("arbitrary",),
                                             vmem_limit_bytes=VMEM_LIMIT),
        name="dilated_mixture",
    )(r3(aqkv), r3(aqkv4), r3(aqkv16), unperm, jnp.asarray(expand, BF16))


def _flash_t_kernel(*refs, window, kshared, vrow0, aux):
    it = iter(refs)
    q_ref = next(it)
    qx_ref = next(it) if aux else None
    k_ref = next(it)
    kx_ref = next(it) if aux else None
    vt_ref = next(it)
    o_ref = next(it)
    T = TB

    i = pl.program_id(1)
    lo = _head_masks((T, LANES))
    qcat = _masked_heads(q_ref, lo)
    if aux == "per_head":
        lane = lax.broadcasted_iota(jnp.int32, (T, LANES), 1)
        qx = qx_ref[...]
        assert LANES // N_HEADS == 32
        qcat = [jnp.concatenate(
            [qcat[h], jnp.where((lane >> 5) == h, qx, jnp.zeros_like(qx))], axis=1)
            for h in range(N_HEADS)]
    elif aux == "shared":
        qx = qx_ref[...]
        qcat = [jnp.concatenate([qcat[h], qx], axis=1) for h in range(N_HEADS)]
    krow = lax.broadcasted_iota(jnp.int32, (T, T), 0)
    qcol = lax.broadcasted_iota(jnp.int32, (T, T), 1)
    ones = jnp.ones((ONES_ROWS, T), BF16)

    def scores(j):
        rows = pl.ds(pl.multiple_of(j * T, T), T)
        out = []
        for h in range(N_HEADS):
            pair = h // 2
            cols = slice(0, LANES) if kshared else slice(pair * LANES, (pair + 1) * LANES)
            kb = k_ref[rows, cols]
            if aux:
                kb = jnp.concatenate([kb, kx_ref[rows, :]], axis=1)
            out.append(_dot_nt(kb, qcat[h]))
        return tuple(out)

    def absorb(states, s_all, j, mask):
        vt_all = vt_ref[j]
        probs = []
        for h in range(N_HEADS):
            m, _ = states[h]
            s = s_all[h]
            if mask is not None:
                s = jnp.where(mask, s, NEG)
            m_new = jnp.maximum(m, jnp.max(s, axis=0, keepdims=True))
            alpha = jnp.exp2(m - m_new)
            p = jnp.exp2(s - m_new)
            if mask is not None:
                p = jnp.where(mask, p, 0.0)
            probs.append((m_new, alpha, p.astype(BF16)))
        new = []
        for h in range(N_HEADS):
            m_new, alpha, p = probs[h]
            r0 = vrow0 if kshared else h * HEAD_DIM
            lhs = jnp.concatenate([vt_all[r0:r0 + HEAD_DIM, :], ones], axis=0)
            acc_new = alpha * states[h][1] + _dot(lhs, p)
            new.append((m_new, acc_new))
        return tuple(new)

    states = tuple((jnp.full((1, T), NEG, F32), jnp.zeros((HEAD_DIM + ONES_ROWS, T), F32))
                   for _ in range(N_HEADS))
    if window is None:
        first = 0
        s_cur = scores(first)
    else:
        o_full = (window - T + 1) // T
        o_none = (window + T - 1) // T + 1
        assert o_none - o_full == 2
        o = o_full + 1
        j = i - o
        jc = jnp.maximum(j, 0)
        first = jnp.maximum(i - o_full, 0)
        s_band = scores(jc)
        s_cur = scores(first)
        limit = jnp.where(j >= 0, window - o * T, -2 * T)
        states = absorb(states, s_band, jc, (qcol - krow) <= limit)

    def body(j, carry):
        st, s_now = carry
        s_next = scores(j + 1)
        return absorb(st, s_now, j, None), s_next

    states, s_cur = lax.fori_loop(first, i, body, (states, s_cur))
    states = absorb(states, s_cur, i, krow <= qcol)

    for pair in range(2):
        outs = []
        for e in range(2):
            _, acc = states[2 * pair + e]
            outs.append(acc[0:HEAD_DIM, :] * (1.0 / acc[HEAD_DIM:HEAD_DIM + 1, :]))
        o_ref[:, pair * LANES:(pair + 1) * LANES] = jnp.concatenate(outs, axis=0).T.astype(BF16)


def _flash_t(q, qcol, k, kcol, kw, vt, *, B, S, window=None, vrow0=0, aux=None, name):
    kshared = kw == LANES
    in_specs = [pl.BlockSpec((None, TB, BLK), lambda b, i: (b, i, qcol))]
    args = [q]
    if aux is not None:
        kind, qx, qxcol, kx, kxcol = aux
        in_specs.append(pl.BlockSpec((None, TB, LANES), lambda b, i: (b, i, qxcol)))
        args.append(qx)
    in_specs.append(pl.BlockSpec((None, S, kw), lambda b, i: (b, 0, kcol)))
    args.append(k)
    if aux is not None:
        if kx.ndim == 3:
            in_specs.append(pl.BlockSpec((None, S, LANES), lambda b, i: (b, 0, kxcol)))
        else:
            in_specs.append(pl.BlockSpec((S, LANES), lambda b, i: (0, kxcol)))
        args.append(kx)
    in_specs.append(pl.BlockSpec((None,) + vt.shape[1:], lambda b, i: (b, 0, 0, 0)))
    args.append(vt)
    return pl.pallas_call(
        functools.partial(_flash_t_kernel, window=window, kshared=kshared, vrow0=vrow0,
                          aux=None if aux is None else aux[0]),
        grid=(B, S // TB),
        in_specs=in_specs,
        out_specs=pl.BlockSpec((None, TB, BLK), lambda b, i: (b, i, 0)),
        out_shape=jax.ShapeDtypeStruct((B, S, BLK), BF16),
        compiler_params=pltpu.CompilerParams(dimension_semantics=("arbitrary", "arbitrary"),
                                             vmem_limit_bytes=VMEM_LIMIT),
        name=name,
    )(*args)


def _mla_up_kernel(cq_ref, ckv_ref, gq_ref, gkv_ref, wq_ref, wk_ref, wvt_ref, c32_ref, s32_ref,
                   qn_ref, qr_ref, kn_ref, vt_ref):
    def norm(t, g):
        t = t.astype(F32)
        ms = jnp.mean(t * t, axis=-1, keepdims=True)
        return ((t * lax.rsqrt(ms + EPS)) * g).astype(BF16)

    scale = LOG2E * (MLA_NOPE + MLA_ROPE) ** -0.5
    q = _dot(norm(cq_ref[...], gq_ref[...]), wq_ref[...])
    qn_ref[...] = (q[:, :BLK] * scale).astype(BF16)
    qr_ref[...] = (_rope_half(q[:, BLK:], c32_ref[...], s32_ref[...], MLA_ROPE) * scale).astype(BF16)
    ckv = norm(ckv_ref[...], gkv_ref[...])
    kn_ref[...] = _dot(ckv, wk_ref[...]).astype(BF16)
    vt = _dot_nt(wvt_ref[...], ckv)
    for blk in range(vt.shape[1] // TB):
        vt_ref[blk] = vt[:, blk * TB:(blk + 1) * TB].astype(BF16)


def _mla_up(proj, g_q, g_kv, w_uq, w_ukv, c32, s32, B, S):
    rows = proj.shape[0]
    tm = 1024
    per_b = S // tm
    wq = w_uq.reshape(MLA_Q_RANK, N_HEADS, MLA_NOPE + MLA_ROPE)
    wq = jnp.concatenate([wq[:, :, :MLA_NOPE].reshape(MLA_Q_RANK, -1),
                          wq[:, :, MLA_NOPE:].reshape(MLA_Q_RANK, -1)], axis=1).astype(BF16)
    wkv = w_ukv.reshape(MLA_KV_RANK, N_HEADS, MLA_NOPE + MLA_V)
    wk = wkv[:, :, :MLA_NOPE].reshape(MLA_KV_RANK, -1).astype(BF16)
    wvt = wkv[:, :, MLA_NOPE:].reshape(MLA_KV_RANK, -1).T.astype(BF16)
    row_spec = lambda w, c: pl.BlockSpec((tm, w), lambda i: (i, c))
    full = lambda a: pl.BlockSpec(a.shape, lambda i: (0, 0))
    gq = g_q.reshape(1, -1)
    gkv = g_kv.reshape(1, -1)
    outs = [jax.ShapeDtypeStruct((rows, BLK), BF16), jax.ShapeDtypeStruct((rows, LANES), BF16),
            jax.ShapeDtypeStruct((rows, BLK), BF16),
            jax.ShapeDtypeStruct((B, S // TB, BLK, TB), BF16)]
    return pl.pallas_call(
        _mla_up_kernel,
        grid=(rows // tm,),
        in_specs=[row_spec(BLK, PB_BCQ), row_spec(LANES, 2 * PB_BKV), full(gq), full(gkv),
                  full(wq), full(wk), full(wvt), row_spec(LANES, 0), row_spec(LANES, 0)],
        out_specs=[row_spec(BLK, 0), row_spec(LANES, 0), row_spec(BLK, 0),
                   pl.BlockSpec((None, tm // TB, BLK, TB), lambda i: (i // per_b, i % per_b, 0, 0))],
        out_shape=outs,
        compiler_params=pltpu.CompilerParams(dimension_semantics=("arbitrary",),
                                             vmem_limit_bytes=VMEM_LIMIT),
        name="mla_up",
    )(proj, proj, gq, gkv, wq, wk, wvt, c32, s32)


def _compress_kernel(x_ref, pos_ref, w1_ref, w2_ref, kc_ref, vc_ref):
    x = x_ref[...].astype(F32)
    half = x.shape[1]
    top = (x + pos_ref[0:1, :]).astype(BF16)
    bot = (x + pos_ref[1:2, :]).astype(BF16)
    y_top = _dot(top, w1_ref[0:half, :])
    y_bot = _dot(bot, w1_ref[half:2 * half, :])
    pre = y_top + pltpu.roll(y_bot, y_bot.shape[0] - 1, 0)
    hid = (pre * (1.0 / (1.0 + jnp.exp(-pre)))).astype(BF16)
    kc_ref[...] = _dot(hid[:, :NSA_CMP_HID], w2_ref[0]).astype(BF16)
    vc_ref[...] = _dot(hid[:, NSA_CMP_HID:], w2_ref[1]).astype(BF16)


def _compress(kcvc, pos_k, pos_v, kw1, kw2, vw1, vw2, B, S):
    n16 = S // NSA_CMP_STRIDE
    x16 = kcvc.reshape(B, n16, NSA_CMP_STRIDE * LANES)
    pos = jnp.concatenate([pos_k, pos_v], axis=1).reshape(2, NSA_CMP_STRIDE * LANES)
    w1 = jnp.zeros((NSA_CMP_LEN, LANES, 2 * NSA_CMP_HID), F32)
    w1 = w1.at[:, :HEAD_DIM, :NSA_CMP_HID].set(kw1.reshape(NSA_CMP_LEN, HEAD_DIM, NSA_CMP_HID))
    w1 = w1.at[:, HEAD_DIM:, NSA_CMP_HID:].set(vw1.reshape(NSA_CMP_LEN, HEAD_DIM, NSA_CMP_HID))
    w1 = w1.reshape(NSA_CMP_LEN * LANES, 2 * NSA_CMP_HID).astype(BF16)
    w2 = jnp.stack([jnp.tile(kw2, (1, 2)), jnp.tile(vw2, (1, 2))]).astype(BF16)
    out = jax.ShapeDtypeStruct((B, n16, LANES), BF16)
    return pl.pallas_call(
        _compress_kernel,
        grid=(B,),
        in_specs=[pl.BlockSpec((None, n16, NSA_CMP_STRIDE * LANES), lambda b: (b, 0, 0)),
                  pl.BlockSpec(pos.shape, lambda b: (0, 0)),
                  pl.BlockSpec(w1.shape, lambda b: (0, 0)),
                  pl.BlockSpec(w2.shape, lambda b: (0, 0, 0))],
        out_specs=[pl.BlockSpec((None, n16, LANES), lambda b: (b, 0, 0))] * 2,
        out_shape=[out, out],
        compiler_params=pltpu.CompilerParams(dimension_semantics=("arbitrary",),
                                             vmem_limit_bytes=VMEM_LIMIT),
        name="nsa_compress",
    )(x16, pos, w1, w2)


def _cmp_topk_kernel(q_ref, kc_ref, vc_ref, ovt_ref, o_ref, nsel_ref, *, T, n_cmp, n_slc):
    i = pl.program_id(1)
    t0 = i * T
    lane = lax.broadcasted_iota(jnp.int32, (T, LANES), 1)
    tok = t0 + lax.broadcasted_iota(jnp.int32, (T, LANES), 0)
    ok = jnp.logical_and(lane * NSA_CMP_STRIDE + (NSA_CMP_LEN - 1) <= tok, lane < n_cmp)
    lo = lane < HEAD_DIM
    qm = _masked_heads(q_ref, lo)
    kc = kc_ref[...]
    vc = vc_ref[...]
    psum = jnp.zeros((T, LANES), F32)
    outs = []
    for h in range(N_HEADS):
        s = jnp.where(ok, _dot_nt(qm[h], kc), NEG)
        mx = jnp.max(s, axis=-1, keepdims=True)
        e = jnp.where(ok, jnp.exp2(s - mx), 0.0)
        den = jnp.sum(e, axis=-1, keepdims=True)
        p = e * (1.0 / jnp.maximum(den, 1e-30))
        psum = psum + p
        outs.append(_dot(p.astype(BF16), vc))
    for pair in range(2):
        o_ref[:, pair * LANES:(pair + 1) * LANES] = jnp.where(
            lo, outs[2 * pair], outs[2 * pair + 1]).astype(BF16)

    p_hi = psum.astype(BF16)
    p_lo = (psum - p_hi.astype(F32)).astype(BF16)
    ovt = ovt_ref[...]
    imp = _dot_nt(ovt, p_hi) + _dot_nt(ovt, p_lo)
    jj = lax.broadcasted_iota(jnp.int32, (n_slc, T), 0)
    bt = (t0 + lax.broadcasted_iota(jnp.int32, (n_slc, T), 1)) >> 6
    forced = jnp.logical_or(jj == 0, jnp.logical_or(jj == bt, jj == bt - 1))
    val = jnp.where(forced, BIG, jnp.where(jj > bt, -BIG, imp))
    sel_rows = []
    for j in range(n_slc):
        vj = val[j:j + 1, :]
        beats = jnp.logical_or(val > vj, jnp.logical_and(val == vj, jj < j))
        rank = jnp.sum(beats.astype(F32), axis=0, keepdims=True)
        sel_rows.append((rank >= NSA_N_SEL).astype(F32))
    sel_rows.append(jnp.zeros((LANES - n_slc, T), F32))
    nsel_ref[...] = jnp.concatenate(sel_rows, axis=0).T.astype(BF16)


def _cmp_topk(proj3, kc2, vc2, B, S):
    T = TB
    n_cmp = (S - NSA_CMP_LEN) // NSA_CMP_STRIDE + 1
    n_slc = S // NSA_SEL_LEN
    assert NSA_SEL_LEN == 64 and n_slc <= LANES and n_cmp <= LANES
    ci = np.arange(LANES)[None, :] * NSA_CMP_STRIDE
    sj = np.arange(n_slc)[:, None] * NSA_SEL_LEN
    ovt = ((ci < sj + NSA_SEL_LEN) & (ci + NSA_CMP_LEN > sj) & (np.arange(LANES)[None, :] < n_cmp))
    ovt = jnp.asarray(ovt.astype(np.float32), BF16)
    return pl.pallas_call(
        functools.partial(_cmp_topk_kernel, T=T, n_cmp=n_cmp, n_slc=n_slc),
        grid=(B, S // T),
        in_specs=[pl.BlockSpec((None, T, BLK), lambda b, i: (b, i, PB_CQ)),
                  pl.BlockSpec((None, LANES, LANES), lambda b, i: (b, 0, 0)),
                  pl.BlockSpec((None, LANES, LANES), lambda b, i: (b, 0, 0)),
                  pl.BlockSpec(ovt.shape, lambda b, i: (0, 0))],
        out_specs=[pl.BlockSpec((None, T, BLK), lambda b, i: (b, i, 0)),
                   pl.BlockSpec((None, T, LANES), lambda b, i: (b, i, 0))],
        out_shape=[jax.ShapeDtypeStruct((B, S, BLK), BF16),
                   jax.ShapeDtypeStruct((B, S, LANES), BF16)],
        compiler_params=pltpu.CompilerParams(dimension_semantics=("arbitrary", "arbitrary"),
                                             vmem_limit_bytes=VMEM_LIMIT),
        name="nsa_cmp_topk",
    )(proj3, kc2, vc2, ovt)


def _stick_kernel(q_ref, k_ref, vt_ref, u_ref, o_ref):
    T = TB
    i = pl.program_id(1)
    lo = _head_masks((T, LANES))
    qm = _masked_heads(q_ref, lo)
    krow = lax.broadcasted_iota(jnp.int32, (T, T), 0)
    qcol = lax.broadcasted_iota(jnp.int32, (T, T), 1)
    strict = krow < qcol
    u = u_ref[...]

    def scores(j):
        rows = pl.ds(pl.multiple_of(j * T, T), T)
        return tuple(_dot_nt(k_ref[rows, (h // 2) * LANES:(h // 2 + 1) * LANES], qm[h])
                     for h in range(N_HEADS))

    def absorb(states, z_all, j, diag):
        vt_all = vt_ref[j]
        split = []
        for h in range(N_HEADS):
            z = z_all[h]
            sp = jnp.maximum(z, 0.0) + jnp.log2(1.0 + jnp.exp2(-jnp.abs(z)))
            if diag:
                sp = jnp.where(strict, sp, 0.0)
            hi = sp.astype(BF16)
            split.append((hi, (sp - hi.astype(F32)).astype(BF16)))
        sums = [_dot(u, hi) + _dot(u, lo_part) for hi, lo_part in split]
        weights = []
        for h in range(N_HEADS):
            a = jnp.exp2(z_all[h] - (sums[h] + states[h][0]))
            if diag:
                a = jnp.where(strict, a, 0.0)
            weights.append(a.astype(BF16))
        new = []
        for h in range(N_HEADS):
            carry, acc = states[h]
            acc = acc + _dot(vt_all[h * HEAD_DIM:(h + 1) * HEAD_DIM, :], weights[h])
            new.append((carry + sums[h][0:1, :], acc))
        return tuple(new)

    states = tuple((jnp.zeros((1, T), F32), jnp.zeros((HEAD_DIM, T), F32)) for _ in range(N_HEADS))
    z_diag = scores(i)
    z_cur = scores(jnp.maximum(i - 1, 0))
    states = absorb(states, z_diag, i, True)

    def body(jj, carry):
        st, z_now = carry
        j = i - 1 - jj
        z_next = scores(jnp.maximum(j - 1, 0))
        return absorb(st, z_now, j, False), z_next

    states, _ = lax.fori_loop(0, i, body, (states, z_cur))
    for pair in range(2):
        o_t = jnp.concatenate([states[2 * pair][1], states[2 * pair + 1][1]], axis=0)
        o_ref[:, pair * LANES:(pair + 1) * LANES] = o_t.T.astype(BF16)


def _stick_breaking(proj3, vdt, B, S):
    u = np.triu(np.ones((TB, TB), np.float32))
    return pl.pallas_call(
        _stick_kernel,
        grid=(B, S // TB),
        in_specs=[pl.BlockSpec((None, TB, BLK), lambda b, i: (b, i, PB_DQ)),
                  pl.BlockSpec((None, S, BLK), lambda b, i: (b, 0, PB_DK)),
                  pl.BlockSpec((None,) + vdt.shape[1:], lambda b, i: (b, 0, 0, 0)),
                  pl.BlockSpec(u.shape, lambda b, i: (0, 0))],
        out_specs=pl.BlockSpec((None, TB, BLK), lambda b, i: (b, i, 0)),
        out_shape=jax.ShapeDtypeStruct((B, S, BLK), BF16),
        compiler_params=pltpu.CompilerParams(dimension_semantics=("arbitrary", "arbitrary"),
                                             vmem_limit_bytes=VMEM_LIMIT),
        name="stick_breaking",
    )(proj3, proj3, vdt, jnp.asarray(u, BF16))


def _outproj_kernel(x_ref, oa, ob, oc, os_, ow, od,
                    ga, gb, gl, gc, gd, expand_ref, w_ref, g_ref, out_ref):
    f = lambda r: r[...].astype(F32)
    sig = lambda t: 1.0 / (1.0 + jnp.exp(-t))
    silu = lambda t: t * sig(t)
    gates = sig(_dot(gl[...], expand_ref[...]))
    o_c = (gates[:, 0:BLK] * f(oc) + gates[:, BLK:2 * BLK] * f(os_)
           + gates[:, 2 * BLK:3 * BLK] * f(ow))
    mixed = [f(oa) * silu(f(ga)), f(ob) * silu(f(gb)), o_c * silu(f(gc)), f(od) * silu(f(gd))]
    y = _dot(mixed[0].astype(BF16), w_ref[0:BLK, :])
    for g in range(1, 4):
        y = y + _dot(mixed[g].astype(BF16), w_ref[g * BLK:(g + 1) * BLK, :])
    ms = jnp.mean(y * y, axis=-1, keepdims=True)
    out_ref[...] = x_ref[...] + (y * lax.rsqrt(ms + EPS)) * g_ref[...]


def _out_projection(x2, mixer_outs, proj, w_out, g_post):
    rows = x2.shape[0]
    tm = TILE
    blk = lambda c: pl.BlockSpec((tm, BLK), lambda i: (i, c))
    expand = np.zeros((LANES, N_BRANCH * BLK), np.float32)
    for h in range(N_HEADS):
        for r in range(N_BRANCH):
            expand[N_BRANCH * h + r, r * BLK + h * HEAD_DIM:r * BLK + (h + 1) * HEAD_DIM] = 1.0
    gate_specs = [blk(PB_AG), blk(PB_BG),
                  pl.BlockSpec((tm, LANES), lambda i: (i, 2 * PB_C1 + 1)), blk(PB_CG), blk(PB_DG)]
    return pl.pallas_call(
        _outproj_kernel,
        grid=(rows // tm,),
        in_specs=[pl.BlockSpec((tm, D_MODEL), lambda i: (i, 0))]
                 + [blk(0)] * len(mixer_outs) + gate_specs
                 + [pl.BlockSpec(expand.shape, lambda i: (0, 0)),
                    pl.BlockSpec((D_MODEL, D_MODEL), lambda i: (0, 0)),
                    pl.BlockSpec((1, D_MODEL), lambda i: (0, 0))],
        out_specs=pl.BlockSpec((tm, D_MODEL), lambda i: (i, 0)),
        out_shape=jax.ShapeDtypeStruct((rows, D_MODEL), F32),
        compiler_params=pltpu.CompilerParams(dimension_semantics=("arbitrary",),
                                             vmem_limit_bytes=VMEM_LIMIT),
        name="out_projection",
    )(x2, *mixer_outs, *([proj] * len(gate_specs)), jnp.asarray(expand, BF16),
      w_out.astype(BF16), g_post.reshape(1, D_MODEL))


def _layer(x2, tabs, B, S, w_in, w_out, g_pre, g_post, mla_g_q, mla_g_kv, mla_w_uq, mla_w_ukv,
           nsa_pos_k, nsa_pos_v, nsa_k_w1, nsa_k_w2, nsa_v_w1, nsa_v_w2):
    rows = B * S
    c64, s64, c32, s32 = tabs
    w_rows, w_t = _pack_w_in(w_in)
    aqkv, aqkv4, aqkv16, proj, vct, vdt = _in_projection(x2, g_pre, w_rows, w_t, tabs, B, S)
    proj3 = proj.reshape(B, S, N_PBLK * BLK)

    o_a = _dilated(aqkv, aqkv4, aqkv16, B, S)

    qn, qr, kn, vbt = _mla_up(proj, mla_g_q, mla_g_kv, mla_w_uq, mla_w_ukv, c32, s32, B, S)
    r3 = lambda t: t.reshape(B, S, t.shape[-1])
    o_b = _flash_t(r3(qn), 0, r3(kn), 0, BLK, vbt, B=B, S=S,
                   aux=("per_head", r3(qr), 0, proj3, 2 * PB_BKV + 1), name="mla_attention")

    kcvc = proj3[:, :, PB_C0 * BLK:PB_C0 * BLK + LANES]
    kc2, vc2 = _compress(kcvc, nsa_pos_k, nsa_pos_v, nsa_k_w1, nsa_k_w2, nsa_v_w1, nsa_v_w2, B, S)
    o_cmp, nsel = _cmp_topk(proj3, kc2, vc2, B, S)
    penalty = np.zeros((S, LANES), np.float32)
    penalty[np.arange(S), np.arange(S) // NSA_SEL_LEN] = MASK_PENALTY
    o_slc = _flash_t(proj3, PB_CQ, proj3, 2 * PB_C0 + 1, LANES, vct, B=B, S=S, vrow0=0,
                     aux=("shared", nsel, 0, jnp.asarray(penalty, BF16), 0), name="nsa_selected")
    o_win = _flash_t(proj3, PB_CQ, proj3, 2 * PB_C1, LANES, vct, B=B, S=S, vrow0=HEAD_DIM,
                     window=NSA_WINDOW - 1, name="nsa_window")

    o_d = _stick_breaking(proj3, vdt, B, S)

    flat = lambda t: t.reshape(rows, BLK)
    outs = [flat(o) for o in (o_a, o_b, o_cmp, o_slc, o_win, o_d)]
    return _out_projection(x2, outs, proj, w_out, g_post)


def kernel(x, positions, w_in, w_out, g_pre, g_post, mla_g_q, mla_g_kv, mla_w_uq, mla_w_ukv,
           nsa_pos_k, nsa_pos_v, nsa_k_w1, nsa_k_w2, nsa_v_w1, nsa_v_w2):
    B, S, D = x.shape
    assert D == D_MODEL and S % 1024 == 0
    x2 = x.reshape(B * S, D)
    tabs = _rope_tables(positions)
    for l in range(w_in.shape[0]):
        x2 = _layer(x2, tabs, B, S, w_in[l], w_out[l], g_pre[l], g_post[l],
                    mla_g_q[l], mla_g_kv[l], mla_w_uq[l], mla_w_ukv[l],
                    nsa_pos_k[l], nsa_pos_v[l], nsa_k_w1[l], nsa_k_w2[l],
                    nsa_v_w1[l], nsa_v_w2[l])
    return x2.reshape(B, S, D)
```

```python
import functools
import math

import numpy as np
import jax
import jax.numpy as jnp
from jax import lax
from jax.experimental import pallas as pl
from jax.experimental.pallas import tpu as pltpu

D_MODEL = 1024
HEAD_DIM = 64
N_HEADS = 4
ROPE_THETA = 10000.0
EPS = 1e-6
NEG = -1e30
LOG2E = math.log2(math.e)
MASK_PENALTY = -(2.0 ** 100)
BIG = 1e9

MLA_Q_RANK = 256
MLA_KV_RANK = 128
MLA_NOPE = 64
MLA_ROPE = 32
MLA_V = 64

NSA_CMP_LEN = 32
NSA_CMP_STRIDE = 16
NSA_CMP_HID = 256
NSA_SEL_LEN = 64
NSA_N_SEL = 16
NSA_WINDOW = 512

LANES = 128
BLK = 256
TILE = 512
DILATIONS = (1, 4, 16)
DIL_BLOCK = 128
LSE_PARTS = 2
TB = 256
ONES_ROWS = 16
VMEM_LIMIT = 48 * 1024 * 1024

F32 = jnp.float32
BF16 = jnp.bfloat16

(PB_AG, PB_BCQ, PB_BKV, PB_BG, PB_CQ, PB_C0, PB_C1, PB_CG, PB_DQ, PB_DK, PB_DG) = range(11)
N_PBLK = 11
N_BRANCH = 3
ROPE_NONE, ROPE_64, ROPE_32, ROPE_64_LO = 0, 1, 2, 3


def _dot_nt(a, b):
    return lax.dot_general(a, b, (((1,), (1,)), ((), ())), preferred_element_type=F32)


def _dot(a, b):
    return jnp.dot(a, b, preferred_element_type=F32)


def _rope_half(y, cos, sin_signed, group):
    half = group // 2
    lane = lax.broadcasted_iota(jnp.int32, y.shape, 1)
    first = (lane & (group - 1)) < half
    partner = jnp.where(first, pltpu.roll(y, LANES - half, 1), pltpu.roll(y, half, 1))
    return y * cos + partner * sin_signed


def _rope_table_kernel(pos_ref, const_ref, c64_ref, s64_ref, c32_ref, s32_ref):
    pos = pos_ref[...].astype(F32)
    a64 = pos * const_ref[0:1, :]
    a32 = pos * const_ref[2:3, :]
    c64_ref[...] = jnp.cos(a64)
    s64_ref[...] = jnp.sin(a64) * const_ref[1:2, :]
    c32_ref[...] = jnp.cos(a32)
    s32_ref[...] = jnp.sin(a32) * const_ref[3:4, :]


def _rope_tables(positions):
    rows = positions.size
    tm = 1024
    lane = np.arange(LANES)
    inv64 = ROPE_THETA ** (-(lane % 32).astype(np.float64) / 32.0)
    sgn64 = np.where((lane % 64) < 32, -1.0, 1.0)
    inv32 = ROPE_THETA ** (-(lane % 16).astype(np.float64) / 16.0)
    sgn32 = np.where((lane % 32) < 16, -1.0, 1.0)
    const = np.zeros((8, LANES), np.float32)
    const[0], const[1], const[2], const[3] = inv64, sgn64, inv32, sgn32
    out = jax.ShapeDtypeStruct((rows, LANES), F32)
    tab_spec = pl.BlockSpec((tm, LANES), lambda i: (i, 0))
    return pl.pallas_call(
        _rope_table_kernel,
        grid=(rows // tm,),
        in_specs=[pl.BlockSpec((tm, 1), lambda i: (i, 0)),
                  pl.BlockSpec((8, LANES), lambda i: (0, 0))],
        out_specs=[tab_spec] * 4,
        out_shape=[out] * 4,
        name="rope_tables",
    )(positions.reshape(rows, 1), jnp.asarray(const))


def _inproj_kernel(x_ref, g_ref, w_ref, wt_ref, perm_ref, c64_ref, s64_ref, c32_ref, s32_ref,
                   aqkv_ref, aqkv4_ref, aqkv16_ref, proj_ref, vct_ref, vdt_ref, *, plan_a, plan_p):
    x = x_ref[...]
    tm = x.shape[0]
    ms = jnp.mean(x * x, axis=-1, keepdims=True)
    h = ((x * lax.rsqrt(ms + EPS)) * g_ref[...]).astype(BF16)
    lane = lax.broadcasted_iota(jnp.int32, (tm, LANES), 1)

    def epilogue(y, ops, scale):
        halves = []
        for hf, op in enumerate(ops):
            yh = y[:, hf * LANES:(hf + 1) * LANES]
            if op == ROPE_64:
                yh = _rope_half(yh, c64_ref[...], s64_ref[...], 64)
            elif op == ROPE_32:
                yh = _rope_half(yh, c32_ref[...], s32_ref[...], 32)
            elif op == ROPE_64_LO:
                yh = jnp.where(lane < 64, _rope_half(yh, c64_ref[...], s64_ref[...], 64), yh)
            if scale != 1.0:
                yh = yh * scale
            halves.append(yh)
        return jnp.concatenate(halves, axis=1).astype(BF16)

    col = 0
    for out_ref, plan in ((aqkv_ref, plan_a), (proj_ref, plan_p)):
        for b, (ops, scale) in enumerate(plan):
            y = _dot(h, w_ref[:, col:col + BLK])
            out_ref[:, b * BLK:(b + 1) * BLK] = epilogue(y, ops, scale)
            col += BLK

    a_rows = aqkv_ref[...]
    aqkv4_ref[...] = _dot(perm_ref[0], a_rows).astype(BF16)
    aqkv16_ref[...] = _dot(perm_ref[1], a_rows).astype(BF16)

    vt = _dot_nt(wt_ref[...], h)
    for blk in range(tm // TB):
        cols = slice(blk * TB, (blk + 1) * TB)
        vct_ref[blk] = vt[0:LANES, cols].astype(BF16)
        vdt_ref[blk] = vt[LANES:LANES + BLK, cols].astype(BF16)


_PLAN_A = (((ROPE_64, ROPE_64), LOG2E * HEAD_DIM ** -0.5),
           ((ROPE_64, ROPE_64), 1.0),
           ((ROPE_NONE, ROPE_NONE), 1.0))
_PLAN_P = (
    ((ROPE_NONE, ROPE_NONE), 1.0),
    ((ROPE_NONE, ROPE_NONE), 1.0),
    ((ROPE_NONE, ROPE_32), 1.0),
    ((ROPE_NONE, ROPE_NONE), 1.0),
    ((ROPE_64, ROPE_64), LOG2E * HEAD_DIM ** -0.5),
    ((ROPE_64_LO, ROPE_64), 1.0),
    ((ROPE_64, ROPE_NONE), 1.0),
    ((ROPE_NONE, ROPE_NONE), 1.0),
    ((ROPE_NONE, ROPE_NONE), LOG2E * HEAD_DIM ** -0.5),
    ((ROPE_NONE, ROPE_NONE), 1.0),
    ((ROPE_NONE, ROPE_NONE), 1.0),
)
assert len(_PLAN_P) == N_PBLK


def _pack_w_in(w):
    o = 0
    seg = {}
    for name, width in (("aq", 256), ("ak", 256), ("av", 256), ("ag", 256),
                        ("bcq", 256), ("bckv", 128), ("bkr", 32), ("bg", 256),
                        ("cq", 256), ("ckc", 64), ("cvc", 64), ("cks", 64), ("cvs", 64),
                        ("ckw", 64), ("cvw", 64), ("cgl", 12), ("cg", 256),
                        ("dq", 256), ("dk", 256), ("dv", 256), ("dg", 256)):
        seg[name] = w[:, o:o + width]
        o += width
    gl_pad = jnp.zeros((D_MODEL, LANES - N_HEADS * N_BRANCH), w.dtype)
    cols = [seg["aq"], seg["ak"], seg["av"],
            seg["ag"], seg["bcq"],
            seg["bckv"], jnp.tile(seg["bkr"], (1, 4)),
            seg["bg"], seg["cq"],
            seg["ckc"], seg["cvc"], seg["cks"], seg["cks"],
            seg["ckw"], seg["ckw"], seg["cgl"], gl_pad,
            seg["cg"], seg["dq"], seg["dk"], seg["dg"]]
    w_rows = jnp.concatenate(cols, axis=1).astype(BF16)
    w_t = jnp.concatenate([seg["cvs"], seg["cvw"], seg["dv"]], axis=1).T.astype(BF16)
    return w_rows, w_t


def _class_major_perm(dil):
    n = TILE // dil
    p = np.zeros((TILE, TILE), np.float32)
    pos = np.arange(TILE)
    p[(pos % dil) * n + pos // dil, pos] = 1.0
    return p


def _in_projection(x2, g_pre, w_rows, w_t, tabs, B, S):
    rows = x2.shape[0]
    tm = TILE
    per_b = S // tm
    nkb = tm // TB
    perm = jnp.asarray(np.stack([_class_major_perm(d) for d in DILATIONS[1:]]), BF16)
    tab_spec = pl.BlockSpec((tm, LANES), lambda i: (i, 0))
    vt_spec = lambda r: pl.BlockSpec((None, nkb, r, TB), lambda i: (i // per_b, i % per_b, 0, 0))
    a_spec = pl.BlockSpec((tm, 3 * BLK), lambda i: (i, 0))
    a_shape = jax.ShapeDtypeStruct((rows, 3 * BLK), BF16)
    return pl.pallas_call(
        functools.partial(_inproj_kernel, plan_a=_PLAN_A, plan_p=_PLAN_P),
        grid=(rows // tm,),
        in_specs=[pl.BlockSpec((tm, D_MODEL), lambda i: (i, 0)),
                  pl.BlockSpec((1, D_MODEL), lambda i: (0, 0)),
                  pl.BlockSpec(w_rows.shape, lambda i: (0, 0)),
                  pl.BlockSpec(w_t.shape, lambda i: (0, 0)),
                  pl.BlockSpec(perm.shape, lambda i: (0, 0, 0))] + [tab_spec] * 4,
        out_specs=[a_spec, a_spec, a_spec,
                   pl.BlockSpec((tm, N_PBLK * BLK), lambda i: (i, 0)),
                   vt_spec(LANES), vt_spec(BLK)],
        out_shape=[a_shape, a_shape, a_shape,
                   jax.ShapeDtypeStruct((rows, N_PBLK * BLK), BF16),
                   jax.ShapeDtypeStruct((B, S // TB, LANES, TB), BF16),
                   jax.ShapeDtypeStruct((B, S // TB, BLK, TB), BF16)],
        compiler_params=pltpu.CompilerParams(dimension_semantics=("arbitrary",),
                                             vmem_limit_bytes=VMEM_LIMIT),
        name="in_projection",
    )(x2, g_pre.reshape(1, D_MODEL), w_rows, w_t, perm, *tabs)


def _head_masks(shape):
    lane = lax.broadcasted_iota(jnp.int32, shape, 1)
    return lane < HEAD_DIM


def _masked_heads(q_ref, lo_mask):
    out = []
    for h in range(N_HEADS):
        pair, e = divmod(h, 2)
        qp = q_ref[:, pair * LANES:(pair + 1) * LANES]
        keep = lo_mask if e == 0 else jnp.logical_not(lo_mask)
        out.append(jnp.where(keep, qp, jnp.zeros_like(qp)))
    return out


def _dilated_kernel(an_ref, a4_ref, a16_ref, unperm_ref, expand_ref, o_ref, res_ref, s_ref):
    T = DIL_BLOCK
    S = an_ref.shape[0]
    n_tiles = S // TILE
    lo = _head_masks((T, LANES))
    lane = lax.broadcasted_iota(jnp.int32, (T, LANES), 1)
    dist2 = (lax.broadcasted_iota(jnp.int32, (T, 2 * T), 0)
             - lax.broadcasted_iota(jnp.int32, (T, 2 * T), 1))
    causal1 = (lax.broadcasted_iota(jnp.int32, (T, T), 1)
               <= lax.broadcasted_iota(jnp.int32, (T, T), 0))

    def scores(q, k):
        qm = _masked_heads(q, lo)
        return tuple(_dot_nt(qm[h], k[:, (h // 2) * LANES:(h // 2 + 1) * LANES])
                     for h in range(N_HEADS))

    def finish(n_keys, v, mask, stage_next):
        probs = []
        for h in range(N_HEADS):
            s = jnp.where(mask, s_ref[h, :, 0:n_keys], NEG)
            m = jnp.max(s, axis=-1, keepdims=True)
            p = jnp.exp2(s - m)
            probs.append((m, jnp.sum(p, axis=-1, keepdims=True), p.astype(BF16)))
        stage_next()
        outs = []
        parts = jnp.zeros((T, LANES), F32)
        for h in range(N_HEADS):
            m, l, p = probs[h]
            outs.append(_dot(p, v[:, (h // 2) * LANES:(h // 2 + 1) * LANES]) * (1.0 / l))
            rest = m + jnp.log2(l)
            for part in range(LSE_PARTS):
                piece = rest.astype(BF16).astype(F32)
                parts = jnp.where(lane == LSE_PARTS * h + part, piece, parts)
                rest = rest - piece
        return jnp.concatenate(
            [jnp.where(lo, outs[0], outs[1]), jnp.where(lo, outs[2], outs[3]), parts],
            axis=1).astype(BF16)

    def band_mask(delta):
        d = dist2 + delta
        return jnp.logical_and(d >= 0, d <= DIL_BLOCK)

    def run(n_units, n_keys, load_qk, load_v, store, mask_of):
        def stage(u):
            for h, s in enumerate(scores(*load_qk(u))):
                s_ref[h, :, 0:n_keys] = s

        def body(u, carry):
            nxt = jnp.minimum(u + 1, n_units - 1)
            store(u, finish(n_keys, load_v(u), mask_of(u), lambda: stage(nxt)))
            return carry

        stage(0)
        lax.fori_loop(0, n_units, body, 0)

    def rows(start, size):
        return pl.ds(pl.multiple_of(start, 32), size)

    def k0_1(u):
        return T * jnp.maximum(u - 1, 0)

    def store1(u, r):
        res_ref[0, rows(T * u, T), :] = r

    run(S // T, 2 * T,
        lambda u: (an_ref[rows(T * u, T), 0:BLK], an_ref[rows(k0_1(u), 2 * T), BLK:2 * BLK]),
        lambda u: an_ref[rows(k0_1(u), 2 * T), 2 * BLK:3 * BLK],
        store1,
        lambda u: band_mask(T * u - k0_1(u)))

    def piece4(u, blk):
        return rows(TILE * blk + T * (u // n_tiles), T)

    def kb0_4(u):
        return jnp.maximum(u % n_tiles - 1, 0)

    def load4(u, cols):
        return jnp.concatenate([a4_ref[piece4(u, kb0_4(u)), cols],
                                a4_ref[piece4(u, kb0_4(u) + 1), cols]], axis=0)

    def store4(u, r):
        res_ref[1, piece4(u, u % n_tiles), :] = r

    run(4 * n_tiles, 2 * T,
        lambda u: (a4_ref[piece4(u, u % n_tiles), 0:BLK], load4(u, slice(BLK, 2 * BLK))),
        lambda u: load4(u, slice(2 * BLK, 3 * BLK)),
        store4,
        lambda u: band_mask(T * (u % n_tiles - kb0_4(u))))

    per = TILE // 16

    def load16(c, cols):
        return jnp.concatenate([a16_ref[rows(TILE * t + per * c, per), cols]
                                for t in range(n_tiles)], axis=0)

    def store16(c, r):
        for t in range(n_tiles):
            res_ref[2, rows(TILE * t + per * c, per), :] = r[per * t:per * (t + 1), :]

    run(16, T,
        lambda c: (load16(c, slice(0, BLK)), load16(c, slice(BLK, 2 * BLK))),
        lambda c: load16(c, slice(2 * BLK, 3 * BLK)),
        store16,
        lambda c: causal1)

    expand = expand_ref[...]
    for t in range(n_tiles):
        tile = slice(TILE * t, TILE * (t + 1))
        nat = [res_ref[0, tile, :].astype(F32),
               _dot(unperm_ref[0], res_ref[1, tile, :]),
               _dot(unperm_ref[1], res_ref[2, tile, :])]
        lse = [_dot(r[:, BLK:].astype(BF16), expand) for r in nat]
        top = jnp.maximum(lse[0], jnp.maximum(lse[1], lse[2]))
        w = [jnp.exp2(l - top) for l in lse]
        mixed = (w[0] * nat[0][:, :BLK] + w[1] * nat[1][:, :BLK] + w[2] * nat[2][:, :BLK])
        o_ref[tile, :] = (mixed / (w[0] + w[1] + w[2])).astype(BF16)


def _dilated(aqkv, aqkv4, aqkv16, B, S):
    assert S == 16 * DIL_BLOCK and S % TILE == 0 and TILE == 4 * DIL_BLOCK
    r3 = lambda t: t.reshape(B, S, 3 * BLK)
    unperm = jnp.asarray(np.stack([_class_major_perm(d).T for d in DILATIONS[1:]]), BF16)
    expand = np.zeros((LANES, BLK), np.float32)
    for h in range(N_HEADS):
        expand[LSE_PARTS * h:LSE_PARTS * (h + 1), h * HEAD_DIM:(h + 1) * HEAD_DIM] = 1.0
    a_spec = pl.BlockSpec((None, S, 3 * BLK), lambda b: (b, 0, 0))
    return pl.pallas_call(
        _dilated_kernel,
        grid=(B,),
        in_specs=[a_spec, a_spec, a_spec,
                  pl.BlockSpec(unperm.shape, lambda b: (0, 0, 0)),
                  pl.BlockSpec(expand.shape, lambda b: (0, 0))],
        out_specs=pl.BlockSpec((None, S, BLK), lambda b: (b, 0, 0)),
        out_shape=jax.ShapeDtypeStruct((B, S, BLK), BF16),
        scratch_shapes=[pltpu.VMEM((3, S, BLK + LANES), BF16),
                        pltpu.VMEM((N_HEADS, DIL_BLOCK, 2 * DIL_BLOCK), F32)],
        compiler_params=pltpu.CompilerParams(dimension_semantics=("arbitrary",),
                                             vmem_limit_bytes=VMEM_LIMIT),
        name="dilated_mixture",
    )(r3(aqkv), r3(aqkv4), r3(aqkv16), unperm, jnp.asarray(expand, BF16))


def _flash_t_kernel(*refs, window, kshared, vrow0, aux):
    it = iter(refs)
    q_ref = next(it)
    qx_ref = next(it) if aux else None
    k_ref = next(it)
    kx_ref = next(it) if aux else None
    vt_ref = next(it)
    o_ref = next(it)
    s_ref, m_ref, acc_ref = next(it), next(it), next(it)
    T = TB

    i = pl.program_id(1)
    lo = _head_masks((T, LANES))
    qcat = _masked_heads(q_ref, lo)
    if aux == "per_head":
        lane = lax.broadcasted_iota(jnp.int32, (T, LANES), 1)
        qx = qx_ref[...]
        assert LANES // N_HEADS == 32
        qcat = [jnp.concatenate(
            [qcat[h], jnp.where((lane >> 5) == h, qx, jnp.zeros_like(qx))], axis=1)
            for h in range(N_HEADS)]
    elif aux == "shared":
        qx = qx_ref[...]
        qcat = [jnp.concatenate([qcat[h], qx], axis=1) for h in range(N_HEADS)]
    krow = lax.broadcasted_iota(jnp.int32, (T, T), 0)
    qcol = lax.broadcasted_iota(jnp.int32, (T, T), 1)
    ones = jnp.ones((ONES_ROWS, T), BF16)

    def scores(j):
        rows = pl.ds(pl.multiple_of(j * T, T), T)
        out = []
        for h in range(N_HEADS):
            pair = h // 2
            cols = slice(0, LANES) if kshared else slice(pair * LANES, (pair + 1) * LANES)
            kb = k_ref[rows, cols]
            if aux:
                kb = jnp.concatenate([kb, kx_ref[rows, :]], axis=1)
            out.append(_dot_nt(kb, qcat[h]))
        return out

    def stage(j):
        for h, s in enumerate(scores(j)):
            s_ref[h] = s

    def absorb(j, mask, staged_next=None, s_all=None):
        vt_all = vt_ref[j]
        probs = []
        for h in range(N_HEADS):
            s = s_ref[h] if s_all is None else s_all[h]
            if mask is not None:
                s = jnp.where(mask, s, NEG)
            m = m_ref[h]
            m_new = jnp.maximum(m, jnp.max(s, axis=0, keepdims=True))
            alpha = jnp.exp2(m - m_new)
            p = jnp.exp2(s - m_new)
            if mask is not None:
                p = jnp.where(mask, p, 0.0)
            probs.append((m_new, alpha, p.astype(BF16)))
        if staged_next is not None:
            stage(staged_next)
        for h in range(N_HEADS):
            m_new, alpha, p = probs[h]
            r0 = vrow0 if kshared else h * HEAD_DIM
            lhs = jnp.concatenate([vt_all[r0:r0 + HEAD_DIM, :], ones], axis=0)
            acc_ref[h] = alpha * acc_ref[h] + _dot(lhs, p)
            m_ref[h] = m_new

    m_ref[...] = jnp.full(m_ref.shape, NEG, F32)
    acc_ref[...] = jnp.zeros(acc_ref.shape, F32)
    if window is None:
        first = 0
        stage(first)
    else:
        o_full = (window - T + 1) // T
        o_none = (window + T - 1) // T + 1
        assert o_none - o_full == 2
        o = o_full + 1
        j = i - o
        jc = jnp.maximum(j, 0)
        first = jnp.maximum(i - o_full, 0)
        s_band = scores(jc)
        stage(first)
        limit = jnp.where(j >= 0, window - o * T, -2 * T)
        absorb(jc, (qcol - krow) <= limit, s_all=s_band)

    def body(j, carry):
        absorb(j, None, staged_next=j + 1)
        return carry

    lax.fori_loop(first, i, body, 0)
    absorb(i, krow <= qcol)

    for pair in range(2):
        outs = []
        for e in range(2):
            acc = acc_ref[2 * pair + e]
            outs.append(acc[0:HEAD_DIM, :] * (1.0 / acc[HEAD_DIM:HEAD_DIM + 1, :]))
        o_ref[:, pair * LANES:(pair + 1) * LANES] = jnp.concatenate(outs, axis=0).T.astype(BF16)


def _flash_t(q, qcol, k, kcol, kw, vt, *, B, S, window=None, vrow0=0, aux=None, name):
    kshared = kw == LANES
    in_specs = [pl.BlockSpec((None, TB, BLK), lambda b, i: (b, i, qcol))]
    args = [q]
    if aux is not None:
        kind, qx, qxcol, kx, kxcol = aux
        in_specs.append(pl.BlockSpec((None, TB, LANES), lambda b, i: (b, i, qxcol)))
        args.append(qx)
    in_specs.append(pl.BlockSpec((None, S, kw), lambda b, i: (b, 0, kcol)))
    args.append(k)
    if aux is not None:
        if kx.ndim == 3:
            in_specs.append(pl.BlockSpec((None, S, LANES), lambda b, i: (b, 0, kxcol)))
        else:
            in_specs.append(pl.BlockSpec((S, LANES), lambda b, i: (0, kxcol)))
        args.append(kx)
    in_specs.append(pl.BlockSpec((None,) + vt.shape[1:], lambda b, i: (b, 0, 0, 0)))
    args.append(vt)
    return pl.pallas_call(
        functools.partial(_flash_t_kernel, window=window, kshared=kshared, vrow0=vrow0,
                          aux=None if aux is None else aux[0]),
        grid=(B, S // TB),
        in_specs=in_specs,
        out_specs=pl.BlockSpec((None, TB, BLK), lambda b, i: (b, i, 0)),
        out_shape=jax.ShapeDtypeStruct((B, S, BLK), BF16),
        scratch_shapes=[pltpu.VMEM((N_HEADS, TB, TB), F32),
                        pltpu.VMEM((N_HEADS, 1, TB), F32),
                        pltpu.VMEM((N_HEADS, HEAD_DIM + ONES_ROWS, TB), F32)],
        compiler_params=pltpu.CompilerParams(dimension_semantics=("arbitrary", "arbitrary"),
                                             vmem_limit_bytes=VMEM_LIMIT),
        name=name,
    )(*args)


def _mla_up_kernel(cq_ref, ckv_ref, gq_ref, gkv_ref, wq_ref, wk_ref, wvt_ref, c32_ref, s32_ref,
                   qn_ref, qr_ref, kn_ref, vt_ref):
    def norm(t, g):
        t = t.astype(F32)
        ms = jnp.mean(t * t, axis=-1, keepdims=True)
        return ((t * lax.rsqrt(ms + EPS)) * g).astype(BF16)

    scale = LOG2E * (MLA_NOPE + MLA_ROPE) ** -0.5
    q = _dot(norm(cq_ref[...], gq_ref[...]), wq_ref[...])
    qn_ref[...] = (q[:, :BLK] * scale).astype(BF16)
    qr_ref[...] = (_rope_half(q[:, BLK:], c32_ref[...], s32_ref[...], MLA_ROPE) * scale).astype(BF16)
    ckv = norm(ckv_ref[...], gkv_ref[...])
    kn_ref[...] = _dot(ckv, wk_ref[...]).astype(BF16)
    vt = _dot_nt(wvt_ref[...], ckv)
    for blk in range(vt.shape[1] // TB):
        vt_ref[blk] = vt[:, blk * TB:(blk + 1) * TB].astype(BF16)


def _mla_up(proj, g_q, g_kv, w_uq, w_ukv, c32, s32, B, S):
    rows = proj.shape[0]
    tm = 1024
    per_b = S // tm
    wq = w_uq.reshape(MLA_Q_RANK, N_HEADS, MLA_NOPE + MLA_ROPE)
    wq = jnp.concatenate([wq[:, :, :MLA_NOPE].reshape(MLA_Q_RANK, -1),
                          wq[:, :, MLA_NOPE:].reshape(MLA_Q_RANK, -1)], axis=1).astype(BF16)
    wkv = w_ukv.reshape(MLA_KV_RANK, N_HEADS, MLA_NOPE + MLA_V)
    wk = wkv[:, :, :MLA_NOPE].reshape(MLA_KV_RANK, -1).astype(BF16)
    wvt = wkv[:, :, MLA_NOPE:].reshape(MLA_KV_RANK, -1).T.astype(BF16)
    row_spec = lambda w, c: pl.BlockSpec((tm, w), lambda i: (i, c))
    full = lambda a: pl.BlockSpec(a.shape, lambda i: (0, 0))
    gq = g_q.reshape(1, -1)
    gkv = g_kv.reshape(1, -1)
    outs = [jax.ShapeDtypeStruct((rows, BLK), BF16), jax.ShapeDtypeStruct((rows, LANES), BF16),
            jax.ShapeDtypeStruct((rows, BLK), BF16),
            jax.ShapeDtypeStruct((B, S // TB, BLK, TB), BF16)]
    return pl.pallas_call(
        _mla_up_kernel,
        grid=(rows // tm,),
        in_specs=[row_spec(BLK, PB_BCQ), row_spec(LANES, 2 * PB_BKV), full(gq), full(gkv),
                  full(wq), full(wk), full(wvt), row_spec(LANES, 0), row_spec(LANES, 0)],
        out_specs=[row_spec(BLK, 0), row_spec(LANES, 0), row_spec(BLK, 0),
                   pl.BlockSpec((None, tm // TB, BLK, TB), lambda i: (i // per_b, i % per_b, 0, 0))],
        out_shape=outs,
        compiler_params=pltpu.CompilerParams(dimension_semantics=("arbitrary",),
                                             vmem_limit_bytes=VMEM_LIMIT),
        name="mla_up",
    )(proj, proj, gq, gkv, wq, wk, wvt, c32, s32)


def _compress_kernel(x_ref, pos_ref, w1_ref, w2_ref, kc_ref, vc_ref):
    x = x_ref[...].astype(F32)
    half = x.shape[1]
    top = (x + pos_ref[0:1, :]).astype(BF16)
    bot = (x + pos_ref[1:2, :]).astype(BF16)
    y_top = _dot(top, w1_ref[0:half, :])
    y_bot = _dot(bot, w1_ref[half:2 * half, :])
    pre = y_top + pltpu.roll(y_bot, y_bot.shape[0] - 1, 0)
    hid = (pre * (1.0 / (1.0 + jnp.exp(-pre)))).astype(BF16)
    kc_ref[...] = _dot(hid[:, :NSA_CMP_HID], w2_ref[0]).astype(BF16)
    vc_ref[...] = _dot(hid[:, NSA_CMP_HID:], w2_ref[1]).astype(BF16)


def _compress(kcvc, pos_k, pos_v, kw1, kw2, vw1, vw2, B, S):
    n16 = S // NSA_CMP_STRIDE
    x16 = kcvc.reshape(B, n16, NSA_CMP_STRIDE * LANES)
    pos = jnp.concatenate([pos_k, pos_v], axis=1).reshape(2, NSA_CMP_STRIDE * LANES)
    w1 = jnp.zeros((NSA_CMP_LEN, LANES, 2 * NSA_CMP_HID), F32)
    w1 = w1.at[:, :HEAD_DIM, :NSA_CMP_HID].set(kw1.reshape(NSA_CMP_LEN, HEAD_DIM, NSA_CMP_HID))
    w1 = w1.at[:, HEAD_DIM:, NSA_CMP_HID:].set(vw1.reshape(NSA_CMP_LEN, HEAD_DIM, NSA_CMP_HID))
    w1 = w1.reshape(NSA_CMP_LEN * LANES, 2 * NSA_CMP_HID).astype(BF16)
    w2 = jnp.stack([jnp.tile(kw2, (1, 2)), jnp.tile(vw2, (1, 2))]).astype(BF16)
    out = jax.ShapeDtypeStruct((B, n16, LANES), BF16)
    return pl.pallas_call(
        _compress_kernel,
        grid=(B,),
        in_specs=[pl.BlockSpec((None, n16, NSA_CMP_STRIDE * LANES), lambda b: (b, 0, 0)),
                  pl.BlockSpec(pos.shape, lambda b: (0, 0)),
                  pl.BlockSpec(w1.shape, lambda b: (0, 0)),
                  pl.BlockSpec(w2.shape, lambda b: (0, 0, 0))],
        out_specs=[pl.BlockSpec((None, n16, LANES), lambda b: (b, 0, 0))] * 2,
        out_shape=[out, out],
        compiler_params=pltpu.CompilerParams(dimension_semantics=("arbitrary",),
                                             vmem_limit_bytes=VMEM_LIMIT),
        name="nsa_compress",
    )(x16, pos, w1, w2)


def _cmp_topk_kernel(q_ref, kc_ref, vc_ref, ovt_ref, o_ref, nsel_ref, *, T, n_cmp, n_slc):
    i = pl.program_id(1)
    t0 = i * T
    lane = lax.broadcasted_iota(jnp.int32, (T, LANES), 1)
    tok = t0 + lax.broadcasted_iota(jnp.int32, (T, LANES), 0)
    ok = jnp.logical_and(lane * NSA_CMP_STRIDE + (NSA_CMP_LEN - 1) <= tok, lane < n_cmp)
    lo = lane < HEAD_DIM
    qm = _masked_heads(q_ref, lo)
    kc = kc_ref[...]
    vc = vc_ref[...]
    psum = jnp.zeros((T, LANES), F32)
    outs = []
    for h in range(N_HEADS):
        s = jnp.where(ok, _dot_nt(qm[h], kc), NEG)
        mx = jnp.max(s, axis=-1, keepdims=True)
        e = jnp.where(ok, jnp.exp2(s - mx), 0.0)
        den = jnp.sum(e, axis=-1, keepdims=True)
        p = e * (1.0 / jnp.maximum(den, 1e-30))
        psum = psum + p
        outs.append(_dot(p.astype(BF16), vc))
    for pair in range(2):
        o_ref[:, pair * LANES:(pair + 1) * LANES] = jnp.where(
            lo, outs[2 * pair], outs[2 * pair + 1]).astype(BF16)

    p_hi = psum.astype(BF16)
    p_lo = (psum - p_hi.astype(F32)).astype(BF16)
    ovt = ovt_ref[...]
    imp = _dot_nt(ovt, p_hi) + _dot_nt(ovt, p_lo)
    jj = lax.broadcasted_iota(jnp.int32, (n_slc, T), 0)
    bt = (t0 + lax.broadcasted_iota(jnp.int32, (n_slc, T), 1)) >> 6
    forced = jnp.logical_or(jj == 0, jnp.logical_or(jj == bt, jj == bt - 1))
    val = jnp.where(forced, BIG, jnp.where(jj > bt, -BIG, imp))
    sel_rows = []
    for j in range(n_slc):
        vj = val[j:j + 1, :]
        beats = jnp.logical_or(val > vj, jnp.logical_and(val == vj, jj < j))
        rank = jnp.sum(beats.astype(F32), axis=0, keepdims=True)
        sel_rows.append((rank >= NSA_N_SEL).astype(F32))
    sel_rows.append(jnp.zeros((LANES - n_slc, T), F32))
    nsel_ref[...] = jnp.concatenate(sel_rows, axis=0).T.astype(BF16)


def _cmp_topk(proj3, kc2, vc2, B, S):
    T = TB
    n_cmp = (S - NSA_CMP_LEN) // NSA_CMP_STRIDE + 1
    n_slc = S // NSA_SEL_LEN
    assert NSA_SEL_LEN == 64 and n_slc <= LANES and n_cmp <= LANES
    ci = np.arange(LANES)[None, :] * NSA_CMP_STRIDE
    sj = np.arange(n_slc)[:, None] * NSA_SEL_LEN
    ovt = ((ci < sj + NSA_SEL_LEN) & (ci + NSA_CMP_LEN > sj) & (np.arange(LANES)[None, :] < n_cmp))
    ovt = jnp.asarray(ovt.astype(np.float32), BF16)
    return pl.pallas_call(
        functools.partial(_cmp_topk_kernel, T=T, n_cmp=n_cmp, n_slc=n_slc),
        grid=(B, S // T),
        in_specs=[pl.BlockSpec((None, T, BLK), lambda b, i: (b, i, PB_CQ)),
                  pl.BlockSpec((None, LANES, LANES), lambda b, i: (b, 0, 0)),
                  pl.BlockSpec((None, LANES, LANES), lambda b, i: (b, 0, 0)),
                  pl.BlockSpec(ovt.shape, lambda b, i: (0, 0))],
        out_specs=[pl.BlockSpec((None, T, BLK), lambda b, i: (b, i, 0)),
                   pl.BlockSpec((None, T, LANES), lambda b, i: (b, i, 0))],
        out_shape=[jax.ShapeDtypeStruct((B, S, BLK), BF16),
                   jax.ShapeDtypeStruct((B, S, LANES), BF16)],
        compiler_params=pltpu.CompilerParams(dimension_semantics=("arbitrary", "arbitrary"),
                                             vmem_limit_bytes=VMEM_LIMIT),
        name="nsa_cmp_topk",
    )(proj3, kc2, vc2, ovt)


def _stick_kernel(q_ref, k_ref, vt_ref, u_ref, o_ref, z_ref, carry_ref, acc_ref):
    T = TB
    i = pl.program_id(1)
    lo = _head_masks((T, LANES))
    qm = _masked_heads(q_ref, lo)
    krow = lax.broadcasted_iota(jnp.int32, (T, T), 0)
    qcol = lax.broadcasted_iota(jnp.int32, (T, T), 1)
    strict = krow < qcol
    u = u_ref[...]

    def stage(j):
        rows = pl.ds(pl.multiple_of(j * T, T), T)
        for h in range(N_HEADS):
            z_ref[h] = _dot_nt(k_ref[rows, (h // 2) * LANES:(h // 2 + 1) * LANES], qm[h])

    def absorb(j, diag, staged_next):
        vt_all = vt_ref[j]
        split = []
        for h in range(N_HEADS):
            z = z_ref[h]
            sp = jnp.maximum(z, 0.0) + jnp.log2(1.0 + jnp.exp2(-jnp.abs(z)))
            if diag:
                sp = jnp.where(strict, sp, 0.0)
            hi = sp.astype(BF16)
            split.append((hi, (sp - hi.astype(F32)).astype(BF16)))
        sums = [_dot(u, hi) + _dot(u, lo_part) for hi, lo_part in split]
        weights = []
        for h in range(N_HEADS):
            a = jnp.exp2(z_ref[h] - (sums[h] + carry_ref[h]))
            if diag:
                a = jnp.where(strict, a, 0.0)
            weights.append(a.astype(BF16))
        if staged_next is not None:
            stage(staged_next)
        for h in range(N_HEADS):
            acc_ref[h] = acc_ref[h] + _dot(vt_all[h * HEAD_DIM:(h + 1) * HEAD_DIM, :], weights[h])
            carry_ref[h] = carry_ref[h] + sums[h][0:1, :]

    carry_ref[...] = jnp.zeros(carry_ref.shape, F32)
    acc_ref[...] = jnp.zeros(acc_ref.shape, F32)
    stage(i)
    absorb(i, True, jnp.maximum(i - 1, 0))

    def body(jj, carry):
        j = i - 1 - jj
        absorb(j, False, jnp.maximum(j - 1, 0))
        return carry

    lax.fori_loop(0, i, body, 0)
    for pair in range(2):
        o_t = jnp.concatenate([acc_ref[2 * pair], acc_ref[2 * pair + 1]], axis=0)
        o_ref[:, pair * LANES:(pair + 1) * LANES] = o_t.T.astype(BF16)


def _stick_breaking(proj3, vdt, B, S):
    u = np.triu(np.ones((TB, TB), np.float32))
    return pl.pallas_call(
        _stick_kernel,
        grid=(B, S // TB),
        in_specs=[pl.BlockSpec((None, TB, BLK), lambda b, i: (b, i, PB_DQ)),
                  pl.BlockSpec((None, S, BLK), lambda b, i: (b, 0, PB_DK)),
                  pl.BlockSpec((None,) + vdt.shape[1:], lambda b, i: (b, 0, 0, 0)),
                  pl.BlockSpec(u.shape, lambda b, i: (0, 0))],
        out_specs=pl.BlockSpec((None, TB, BLK), lambda b, i: (b, i, 0)),
        out_shape=jax.ShapeDtypeStruct((B, S, BLK), BF16),
        scratch_shapes=[pltpu.VMEM((N_HEADS, TB, TB), F32),
                        pltpu.VMEM((N_HEADS, 1, TB), F32),
                        pltpu.VMEM((N_HEADS, HEAD_DIM, TB), F32)],
        compiler_params=pltpu.CompilerParams(dimension_semantics=("arbitrary", "arbitrary"),
                                             vmem_limit_bytes=VMEM_LIMIT),
        name="stick_breaking",
    )(proj3, proj3, vdt, jnp.asarray(u, BF16))


def _outproj_kernel(x_ref, oa, ob, oc, os_, ow, od,
                    ga, gb, gl, gc, gd, expand_ref, w_ref, g_ref, out_ref):
    f = lambda r: r[...].astype(F32)
    sig = lambda t: 1.0 / (1.0 + jnp.exp(-t))
    silu = lambda t: t * sig(t)
    gates = sig(_dot(gl[...], expand_ref[...]))
    o_c = (gates[:, 0:BLK] * f(oc) + gates[:, BLK:2 * BLK] * f(os_)
           + gates[:, 2 * BLK:3 * BLK] * f(ow))
    mixed = [f(oa) * silu(f(ga)), f(ob) * silu(f(gb)), o_c * silu(f(gc)), f(od) * silu(f(gd))]
    y = _dot(mixed[0].astype(BF16), w_ref[0:BLK, :])
    for g in range(1, 4):
        y = y + _dot(mixed[g].astype(BF16), w_ref[g * BLK:(g + 1) * BLK, :])
    ms = jnp.mean(y * y, axis=-1, keepdims=True)
    out_ref[...] = x_ref[...] + (y * lax.rsqrt(ms + EPS)) * g_ref[...]


def _out_projection(x2, mixer_outs, proj, w_out, g_post):
    rows = x2.shape[0]
    tm = TILE
    blk = lambda c: pl.BlockSpec((tm, BLK), lambda i: (i, c))
    expand = np.zeros((LANES, N_BRANCH * BLK), np.float32)
    for h in range(N_HEADS):
        for r in range(N_BRANCH):
            expand[N_BRANCH * h + r, r * BLK + h * HEAD_DIM:r * BLK + (h + 1) * HEAD_DIM] = 1.0
    gate_specs = [blk(PB_AG), blk(PB_BG),
                  pl.BlockSpec((tm, LANES), lambda i: (i, 2 * PB_C1 + 1)), blk(PB_CG), blk(PB_DG)]
    return pl.pallas_call(
        _outproj_kernel,
        grid=(rows // tm,),
        in_specs=[pl.BlockSpec((tm, D_MODEL), lambda i: (i, 0))]
                 + [blk(0)] * len(mixer_outs) + gate_specs
                 + [pl.BlockSpec(expand.shape, lambda i: (0, 0)),
                    pl.BlockSpec((D_MODEL, D_MODEL), lambda i: (0, 0)),
                    pl.BlockSpec((1, D_MODEL), lambda i: (0, 0))],
        out_specs=pl.BlockSpec((tm, D_MODEL), lambda i: (i, 0)),
        out_shape=jax.ShapeDtypeStruct((rows, D_MODEL), F32),
        compiler_params=pltpu.CompilerParams(dimension_semantics=("arbitrary",),
                                             vmem_limit_bytes=VMEM_LIMIT),
        name="out_projection",
    )(x2, *mixer_outs, *([proj] * len(gate_specs)), jnp.asarray(expand, BF16),
      w_out.astype(BF16), g_post.reshape(1, D_MODEL))


def _layer(x2, tabs, B, S, w_in, w_out, g_pre, g_post, mla_g_q, mla_g_kv, mla_w_uq, mla_w_ukv,
           nsa_pos_k, nsa_pos_v, nsa_k_w1, nsa_k_w2, nsa_v_w1, nsa_v_w2):
    rows = B * S
    c64, s64, c32, s32 = tabs
    w_rows, w_t = _pack_w_in(w_in)
    aqkv, aqkv4, aqkv16, proj, vct, vdt = _in_projection(x2, g_pre, w_rows, w_t, tabs, B, S)
    proj3 = proj.reshape(B, S, N_PBLK * BLK)

    o_a = _dilated(aqkv, aqkv4, aqkv16, B, S)

    qn, qr, kn, vbt = _mla_up(proj, mla_g_q, mla_g_kv, mla_w_uq, mla_w_ukv, c32, s32, B, S)
    r3 = lambda t: t.reshape(B, S, t.shape[-1])
    o_b = _flash_t(r3(qn), 0, r3(kn), 0, BLK, vbt, B=B, S=S,
                   aux=("per_head", r3(qr), 0, proj3, 2 * PB_BKV + 1), name="mla_attention")

    kcvc = proj3[:, :, PB_C0 * BLK:PB_C0 * BLK + LANES]
    kc2, vc2 = _compress(kcvc, nsa_pos_k, nsa_pos_v, nsa_k_w1, nsa_k_w2, nsa_v_w1, nsa_v_w2, B, S)
    o_cmp, nsel = _cmp_topk(proj3, kc2, vc2, B, S)
    penalty = np.zeros((S, LANES), np.float32)
    penalty[np.arange(S), np.arange(S) // NSA_SEL_LEN] = MASK_PENALTY
    o_slc = _flash_t(proj3, PB_CQ, proj3, 2 * PB_C0 + 1, LANES, vct, B=B, S=S, vrow0=0,
                     aux=("shared", nsel, 0, jnp.asarray(penalty, BF16), 0), name="nsa_selected")
    o_win = _flash_t(proj3, PB_CQ, proj3, 2 * PB_C1, LANES, vct, B=B, S=S, vrow0=HEAD_DIM,
                     window=NSA_WINDOW - 1, name="nsa_window")

    o_d = _stick_breaking(proj3, vdt, B, S)

    flat = lambda t: t.reshape(rows, BLK)
    outs = [flat(o) for o in (o_a, o_b, o_cmp, o_slc, o_win, o_d)]
    return _out_projection(x2, outs, proj, w_out, g_post)


def kernel(x, positions, w_in, w_out, g_pre, g_post, mla_g_q, mla_g_kv, mla_w_uq, mla_w_ukv,
           nsa_pos_k, nsa_pos_v, nsa_k_w1, nsa_k_w2, nsa_v_w1, nsa_v_w2):
    B, S, D = x.shape
    assert D == D_MODEL and S % 1024 == 0
    x2 = x.reshape(B * S, D)
    tabs = _rope_tables(positions)
    for l in range(w_in.shape[0]):
        x2 = _layer(x2, tabs, B, S, w_in[l], w_out[l], g_pre[l], g_post[l],
                    mla_g_q[l], mla_g_kv[l], mla_w_uq[l], mla_w_ukv[l],
                    nsa_pos_k[l], nsa_pos_v[l], nsa_k_w1[l], nsa_k_w2[l],
                    nsa_v_w1[l], nsa_v_w2[l])
    return x2.reshape(B, S, D)
```

```python
import functools
import math

import numpy as np
import jax
import jax.numpy as jnp
from jax import lax
from jax.experimental import pallas as pl
from jax.experimental.pallas import tpu as pltpu

D_MODEL = 1024
HEAD_DIM = 64
N_HEADS = 4
ROPE_THETA = 10000.0
EPS = 1e-6
NEG = -1e30
LOG2E = math.log2(math.e)
MASK_PENALTY = -(2.0 ** 100)
BIG = 1e9

MLA_Q_RANK = 256
MLA_KV_RANK = 128
MLA_NOPE = 64
MLA_ROPE = 32
MLA_V = 64

NSA_CMP_LEN = 32
NSA_CMP_STRIDE = 16
NSA_CMP_HID = 256
NSA_SEL_LEN = 64
NSA_N_SEL = 16
NSA_WINDOW = 512

LANES = 128
BLK = 256
TILE = 512
DILATIONS = (1, 4, 16)
DIL_BLOCK = 128
LSE_PARTS = 2
TB = 256
ONES_ROWS = 16
VMEM_LIMIT = 48 * 1024 * 1024

F32 = jnp.float32
BF16 = jnp.bfloat16

(PB_AG, PB_BCQ, PB_BKV, PB_BG, PB_CQ, PB_C0, PB_C1, PB_CG, PB_DQ, PB_DK, PB_DG) = range(11)
N_PBLK = 11
N_BRANCH = 3
ROPE_NONE, ROPE_64, ROPE_32, ROPE_64_LO, ACT_SILU, ACT_SIGMOID = range(6)


def _dot_nt(a, b):
    return lax.dot_general(a, b, (((1,), (1,)), ((), ())), preferred_element_type=F32)


def _dot(a, b):
    return jnp.dot(a, b, preferred_element_type=F32)


def _rope_half(y, cos, sin_signed, group):
    half = group // 2
    lane = lax.broadcasted_iota(jnp.int32, y.shape, 1)
    first = (lane & (group - 1)) < half
    partner = jnp.where(first, pltpu.roll(y, LANES - half, 1), pltpu.roll(y, half, 1))
    return y * cos + partner * sin_signed


def _rope_table_kernel(pos_ref, const_ref, c64_ref, s64_ref, c32_ref, s32_ref):
    pos = pos_ref[...].astype(F32)
    a64 = pos * const_ref[0:1, :]
    a32 = pos * const_ref[2:3, :]
    c64_ref[...] = jnp.cos(a64)
    s64_ref[...] = jnp.sin(a64) * const_ref[1:2, :]
    c32_ref[...] = jnp.cos(a32)
    s32_ref[...] = jnp.sin(a32) * const_ref[3:4, :]


def _rope_tables(positions):
    rows = positions.size
    tm = 1024
    lane = np.arange(LANES)
    inv64 = ROPE_THETA ** (-(lane % 32).astype(np.float64) / 32.0)
    sgn64 = np.where((lane % 64) < 32, -1.0, 1.0)
    inv32 = ROPE_THETA ** (-(lane % 16).astype(np.float64) / 16.0)
    sgn32 = np.where((lane % 32) < 16, -1.0, 1.0)
    const = np.zeros((8, LANES), np.float32)
    const[0], const[1], const[2], const[3] = inv64, sgn64, inv32, sgn32
    out = jax.ShapeDtypeStruct((rows, LANES), F32)
    tab_spec = pl.BlockSpec((tm, LANES), lambda i: (i, 0))
    return pl.pallas_call(
        _rope_table_kernel,
        grid=(rows // tm,),
        in_specs=[pl.BlockSpec((tm, 1), lambda i: (i, 0)),
                  pl.BlockSpec((8, LANES), lambda i: (0, 0))],
        out_specs=[tab_spec] * 4,
        out_shape=[out] * 4,
        name="rope_tables",
    )(positions.reshape(rows, 1), jnp.asarray(const))


def _inproj_kernel(x_ref, g_ref, w_ref, wt_ref, perm_ref, c64_ref, s64_ref, c32_ref, s32_ref,
                   aqkv_ref, aqkv4_ref, aqkv16_ref, proj_ref, vct_ref, vdt_ref, qct_ref, qdt_ref,
                   *, plan_a, plan_p):
    x = x_ref[...]
    tm = x.shape[0]
    ms = jnp.mean(x * x, axis=-1, keepdims=True)
    h = ((x * lax.rsqrt(ms + EPS)) * g_ref[...]).astype(BF16)
    lane = lax.broadcasted_iota(jnp.int32, (tm, LANES), 1)

    def epilogue(y, ops, scale):
        halves = []
        for hf, op in enumerate(ops):
            yh = y[:, hf * LANES:(hf + 1) * LANES]
            if op == ROPE_64:
                yh = _rope_half(yh, c64_ref[...], s64_ref[...], 64)
            elif op == ROPE_32:
                yh = _rope_half(yh, c32_ref[...], s32_ref[...], 32)
            elif op == ROPE_64_LO:
                yh = jnp.where(lane < 64, _rope_half(yh, c64_ref[...], s64_ref[...], 64), yh)
            elif op == ACT_SILU:
                yh = yh * (1.0 / (1.0 + jnp.exp(-yh)))
            elif op == ACT_SIGMOID:
                yh = 1.0 / (1.0 + jnp.exp(-yh))
            if scale != 1.0:
                yh = yh * scale
            halves.append(yh)
        return jnp.concatenate(halves, axis=1)

    feature_major = {PB_CQ: qct_ref, PB_DQ: qdt_ref}
    col = 0
    for out_ref, plan in ((aqkv_ref, plan_a), (proj_ref, plan_p)):
        for b, (ops, scale) in enumerate(plan):
            y = epilogue(_dot(h, w_ref[:, col:col + BLK]), ops, scale)
            out_ref[:, b * BLK:(b + 1) * BLK] = y.astype(BF16)
            if out_ref is proj_ref and b in feature_major:
                y_t = y.T
                for blk in range(tm // TB):
                    feature_major[b][blk] = y_t[:, blk * TB:(blk + 1) * TB].astype(BF16)
            col += BLK

    a_rows = aqkv_ref[...]
    aqkv4_ref[...] = _dot(perm_ref[0], a_rows).astype(BF16)
    aqkv16_ref[...] = _dot(perm_ref[1], a_rows).astype(BF16)

    vt = _dot_nt(wt_ref[...], h)
    for blk in range(tm // TB):
        cols = slice(blk * TB, (blk + 1) * TB)
        vct_ref[blk] = vt[0:LANES, cols].astype(BF16)
        vdt_ref[blk] = vt[LANES:LANES + BLK, cols].astype(BF16)


_PLAN_A = (((ROPE_64, ROPE_64), LOG2E * HEAD_DIM ** -0.5),
           ((ROPE_64, ROPE_64), 1.0),
           ((ROPE_NONE, ROPE_NONE), 1.0))
_PLAN_P = (
    ((ACT_SILU, ACT_SILU), 1.0),
    ((ROPE_NONE, ROPE_NONE), 1.0),
    ((ROPE_NONE, ROPE_32), 1.0),
    ((ACT_SILU, ACT_SILU), 1.0),
    ((ROPE_64, ROPE_64), LOG2E * HEAD_DIM ** -0.5),
    ((ROPE_64_LO, ROPE_64), 1.0),
    ((ROPE_64, ACT_SIGMOID), 1.0),
    ((ACT_SILU, ACT_SILU), 1.0),
    ((ROPE_NONE, ROPE_NONE), LOG2E * HEAD_DIM ** -0.5),
    ((ROPE_NONE, ROPE_NONE), 1.0),
    ((ACT_SILU, ACT_SILU), 1.0),
)
assert len(_PLAN_P) == N_PBLK


def _pack_w_in(w):
    o = 0
    seg = {}
    for name, width in (("aq", 256), ("ak", 256), ("av", 256), ("ag", 256),
                        ("bcq", 256), ("bckv", 128), ("bkr", 32), ("bg", 256),
                        ("cq", 256), ("ckc", 64), ("cvc", 64), ("cks", 64), ("cvs", 64),
                        ("ckw", 64), ("cvw", 64), ("cgl", 12), ("cg", 256),
                        ("dq", 256), ("dk", 256), ("dv", 256), ("dg", 256)):
        seg[name] = w[:, o:o + width]
        o += width
    gl_pad = jnp.zeros((D_MODEL, LANES - N_HEADS * N_BRANCH), w.dtype)
    cols = [seg["aq"], seg["ak"], seg["av"],
            seg["ag"], seg["bcq"],
            seg["bckv"], jnp.tile(seg["bkr"], (1, 4)),
            seg["bg"], seg["cq"],
            seg["ckc"], seg["cvc"], seg["cks"], seg["cks"],
            seg["ckw"], seg["ckw"], seg["cgl"], gl_pad,
            seg["cg"], seg["dq"], seg["dk"], seg["dg"]]
    w_rows = jnp.concatenate(cols, axis=1).astype(BF16)
    w_t = jnp.concatenate([seg["cvs"], seg["cvw"], seg["dv"]], axis=1).T.astype(BF16)
    return w_rows, w_t


def _class_major_perm(dil):
    n = TILE // dil
    p = np.zeros((TILE, TILE), np.float32)
    pos = np.arange(TILE)
    p[(pos % dil) * n + pos // dil, pos] = 1.0
    return p


def _in_projection(x2, g_pre, w_rows, w_t, tabs, B, S):
    rows = x2.shape[0]
    tm = TILE
    per_b = S // tm
    nkb = tm // TB
    perm = jnp.asarray(np.stack([_class_major_perm(d) for d in DILATIONS[1:]]), BF16)
    tab_spec = pl.BlockSpec((tm, LANES), lambda i: (i, 0))
    vt_spec = lambda r: pl.BlockSpec((None, nkb, r, TB), lambda i: (i // per_b, i % per_b, 0, 0))
    a_spec = pl.BlockSpec((tm, 3 * BLK), lambda i: (i, 0))
    a_shape = jax.ShapeDtypeStruct((rows, 3 * BLK), BF16)
    return pl.pallas_call(
        functools.partial(_inproj_kernel, plan_a=_PLAN_A, plan_p=_PLAN_P),
        grid=(rows // tm,),
        in_specs=[pl.BlockSpec((tm, D_MODEL), lambda i: (i, 0)),
                  pl.BlockSpec((1, D_MODEL), lambda i: (0, 0)),
                  pl.BlockSpec(w_rows.shape, lambda i: (0, 0)),
                  pl.BlockSpec(w_t.shape, lambda i: (0, 0)),
                  pl.BlockSpec(perm.shape, lambda i: (0, 0, 0))] + [tab_spec] * 4,
        out_specs=[a_spec, a_spec, a_spec,
                   pl.BlockSpec((tm, N_PBLK * BLK), lambda i: (i, 0)),
                   vt_spec(LANES), vt_spec(BLK), vt_spec(BLK), vt_spec(BLK)],
        out_shape=[a_shape, a_shape, a_shape,
                   jax.ShapeDtypeStruct((rows, N_PBLK * BLK), BF16),
                   jax.ShapeDtypeStruct((B, S // TB, LANES, TB), BF16)]
                  + [jax.ShapeDtypeStruct((B, S // TB, BLK, TB), BF16)] * 3,
        compiler_params=pltpu.CompilerParams(dimension_semantics=("arbitrary",),
                                             vmem_limit_bytes=VMEM_LIMIT),
        name="in_projection",
    )(x2, g_pre.reshape(1, D_MODEL), w_rows, w_t, perm, *tabs)


def _head_masks(shape):
    lane = lax.broadcasted_iota(jnp.int32, shape, 1)
    return lane < HEAD_DIM


def _masked_heads(q_ref, lo_mask):
    out = []
    for h in range(N_HEADS):
        pair, e = divmod(h, 2)
        qp = q_ref[:, pair * LANES:(pair + 1) * LANES]
        keep = lo_mask if e == 0 else jnp.logical_not(lo_mask)
        out.append(jnp.where(keep, qp, jnp.zeros_like(qp)))
    return out


def _dilated_kernel(an_ref, a4_ref, a16_ref, unperm_ref, expand_ref, o_ref, res_ref, s_ref):
    T = DIL_BLOCK
    S = an_ref.shape[0]
    n_tiles = S // TILE
    lo = _head_masks((T, LANES))
    lane = lax.broadcasted_iota(jnp.int32, (T, LANES), 1)
    dist2 = (lax.broadcasted_iota(jnp.int32, (T, 2 * T), 0)
             - lax.broadcasted_iota(jnp.int32, (T, 2 * T), 1))
    causal1 = (lax.broadcasted_iota(jnp.int32, (T, T), 1)
               <= lax.broadcasted_iota(jnp.int32, (T, T), 0))

    def scores(q, k):
        qm = _masked_heads(q, lo)
        return tuple(_dot_nt(qm[h], k[:, (h // 2) * LANES:(h // 2 + 1) * LANES])
                     for h in range(N_HEADS))

    def finish(n_keys, v, mask, stage_next):
        probs = []
        for h in range(N_HEADS):
            s = jnp.where(mask, s_ref[h, :, 0:n_keys], NEG)
            m = jnp.max(s, axis=-1, keepdims=True)
            p = jnp.exp2(s - m)
            probs.append((m, jnp.sum(p, axis=-1, keepdims=True), p.astype(BF16)))
        stage_next()
        outs = []
        parts = jnp.zeros((T, LANES), F32)
        for h in range(N_HEADS):
            m, l, p = probs[h]
            outs.append(_dot(p, v[:, (h // 2) * LANES:(h // 2 + 1) * LANES]) * (1.0 / l))
            rest = m + jnp.log2(l)
            for part in range(LSE_PARTS):
                piece = rest.astype(BF16).astype(F32)
                parts = jnp.where(lane == LSE_PARTS * h + part, piece, parts)
                rest = rest - piece
        return jnp.concatenate(
            [jnp.where(lo, outs[0], outs[1]), jnp.where(lo, outs[2], outs[3]), parts],
            axis=1).astype(BF16)

    def band_mask(delta):
        d = dist2 + delta
        return jnp.logical_and(d >= 0, d <= DIL_BLOCK)

    def run(n_units, n_keys, load_qk, load_v, store, mask_of):
        def stage(u):
            for h, s in enumerate(scores(*load_qk(u))):
                s_ref[h, :, 0:n_keys] = s

        def body(u, carry):
            nxt = jnp.minimum(u + 1, n_units - 1)
            store(u, finish(n_keys, load_v(u), mask_of(u), lambda: stage(nxt)))
            return carry

        stage(0)
        lax.fori_loop(0, n_units, body, 0)

    def rows(start, size):
        return pl.ds(pl.multiple_of(start, 32), size)

    def k0_1(u):
        return T * jnp.maximum(u - 1, 0)

    def store1(u, r):
        res_ref[0, rows(T * u, T), :] = r

    run(S // T, 2 * T,
        lambda u: (an_ref[rows(T * u, T), 0:BLK], an_ref[rows(k0_1(u), 2 * T), BLK:2 * BLK]),
        lambda u: an_ref[rows(k0_1(u), 2 * T), 2 * BLK:3 * BLK],
        store1,
        lambda u: band_mask(T * u - k0_1(u)))

    def piece4(u, blk):
        return rows(TILE * blk + T * (u // n_tiles), T)

    def kb0_4(u):
        return jnp.maximum(u % n_tiles - 1, 0)

    def load4(u, cols):
        return jnp.concatenate([a4_ref[piece4(u, kb0_4(u)), cols],
                                a4_ref[piece4(u, kb0_4(u) + 1), cols]], axis=0)

    def store4(u, r):
        res_ref[1, piece4(u, u % n_tiles), :] = r

    run(4 * n_tiles, 2 * T,
        lambda u: (a4_ref[piece4(u, u % n_tiles), 0:BLK], load4(u, slice(BLK, 2 * BLK))),
        lambda u: load4(u, slice(2 * BLK, 3 * BLK)),
        store4,
        lambda u: band_mask(T * (u % n_tiles - kb0_4(u))))

    per = TILE // 16

    def load16(c, cols):
        return jnp.concatenate([a16_ref[rows(TILE * t + per * c, per), cols]
                                for t in range(n_tiles)], axis=0)

    def store16(c, r):
        for t in range(n_tiles):
            res_ref[2, rows(TILE * t + per * c, per), :] = r[per * t:per * (t + 1), :]

    run(16, T,
        lambda c: (load16(c, slice(0, BLK)), load16(c, slice(BLK, 2 * BLK))),
        lambda c: load16(c, slice(2 * BLK, 3 * BLK)),
        store16,
        lambda c: causal1)

    expand = expand_ref[...]
    for t in range(n_tiles):
        tile = slice(TILE * t, TILE * (t + 1))
        nat = [res_ref[0, tile, :].astype(F32),
               _dot(unperm_ref[0], res_ref[1, tile, :]),
               _dot(unperm_ref[1], res_ref[2, tile, :])]
        lse = [_dot(r[:, BLK:].astype(BF16), expand) for r in nat]
        top = jnp.maximum(lse[0], jnp.maximum(lse[1], lse[2]))
        w = [jnp.exp2(l - top) for l in lse]
        mixed = (w[0] * nat[0][:, :BLK] + w[1] * nat[1][:, :BLK] + w[2] * nat[2][:, :BLK])
        o_ref[tile, :] = (mixed / (w[0] + w[1] + w[2])).astype(BF16)


def _dilated(aqkv, aqkv4, aqkv16, B, S):
    assert S == 16 * DIL_BLOCK and S % TILE == 0 and TILE == 4 * DIL_BLOCK
    r3 = lambda t: t.reshape(B, S, 3 * BLK)
    unperm = jnp.asarray(np.stack([_class_major_perm(d).T for d in DILATIONS[1:]]), BF16)
    expand = np.zeros((LANES, BLK), np.float32)
    for h in range(N_HEADS):
        expand[LSE_PARTS * h:LSE_PARTS * (h + 1), h * HEAD_DIM:(h + 1) * HEAD_DIM] = 1.0
    a_spec = pl.BlockSpec((None, S, 3 * BLK), lambda b: (b, 0, 0))
    return pl.pallas_call(
        _dilated_kernel,
        grid=(B,),
        in_specs=[a_spec, a_spec, a_spec,
                  pl.BlockSpec(unperm.shape, lambda b: (0, 0, 0)),
                  pl.BlockSpec(expand.shape, lambda b: (0, 0))],
        out_specs=pl.BlockSpec((None, S, BLK), lambda b: (b, 0, 0)),
        out_shape=jax.ShapeDtypeStruct((B, S, BLK), BF16),
        scratch_shapes=[pltpu.VMEM((3, S, BLK + LANES), BF16),
                        pltpu.VMEM((N_HEADS, DIL_BLOCK, 2 * DIL_BLOCK), F32)],
        compiler_params=pltpu.CompilerParams(dimension_semantics=("arbitrary",),
                                             vmem_limit_bytes=VMEM_LIMIT),
        name="dilated_mixture",
    )(r3(aqkv), r3(aqkv4), r3(aqkv16), unperm, jnp.asarray(expand, BF16))


def _masked_head_rows(qt_ref):
    out = []
    zeros = jnp.zeros((HEAD_DIM, qt_ref.shape[-1]), BF16)
    for h in range(N_HEADS):
        own = qt_ref[h * HEAD_DIM:(h + 1) * HEAD_DIM, :]
        out.append(jnp.concatenate([own, zeros] if h % 2 == 0 else [zeros, own], axis=0))
    return out


def _flash_t_kernel(*refs, window, kshared, vrow0, aux):
    it = iter(refs)
    q_ref = next(it)
    qx_ref = next(it) if aux else None
    k_ref = next(it)
    kx_ref = next(it) if aux else None
    vt_ref = next(it)
    o_ref = next(it)
    s_ref, m_ref, acc_ref = next(it), next(it), next(it)
    T = TB

    i = pl.program_id(1)
    qcat = _masked_head_rows(q_ref)
    if aux == "per_head":
        group = LANES // N_HEADS
        qcat = [jnp.concatenate(
            [qcat[h]] + [qx_ref[g * group:(g + 1) * group, :] if g == h
                         else jnp.zeros((group, T), BF16) for g in range(N_HEADS)], axis=0)
            for h in range(N_HEADS)]
    elif aux == "shared":
        qx = qx_ref[...]
        qcat = [jnp.concatenate([qcat[h], qx], axis=0) for h in range(N_HEADS)]
    krow = lax.broadcasted_iota(jnp.int32, (T, T), 0)
    qcol = lax.broadcasted_iota(jnp.int32, (T, T), 1)
    ones = jnp.ones((ONES_ROWS, T), BF16)

    def scores(j):
        rows = pl.ds(pl.multiple_of(j * T, T), T)
        out = []
        for h in range(N_HEADS):
            pair = h // 2
            cols = slice(0, LANES) if kshared else slice(pair * LANES, (pair + 1) * LANES)
            kb = k_ref[rows, cols]
            if aux:
                kb = jnp.concatenate([kb, kx_ref[rows, :]], axis=1)
            out.append(_dot(kb, qcat[h]))
        return out

    def stage(j):
        for h, s in enumerate(scores(j)):
            s_ref[h] = s

    def absorb(j, mask, staged_next=None, s_all=None):
        vt_all = vt_ref[j]
        probs = []
        for h in range(N_HEADS):
            s = s_ref[h] if s_all is None else s_all[h]
            if mask is not None:
                s = jnp.where(mask, s, NEG)
            m = m_ref[h]
            m_new = jnp.maximum(m, jnp.max(s, axis=0, keepdims=True))
            alpha = jnp.exp2(m - m_new)
            p = jnp.exp2(s - m_new)
            probs.append((m_new, alpha, p.astype(BF16)))
        if staged_next is not None:
            stage(staged_next)
        for h in range(N_HEADS):
            m_new, alpha, p = probs[h]
            r0 = vrow0 if kshared else h * HEAD_DIM
            lhs = jnp.concatenate([vt_all[r0:r0 + HEAD_DIM, :], ones], axis=0)
            acc_ref[h] = alpha * acc_ref[h] + _dot(lhs, p)
            m_ref[h] = m_new

    m_ref[...] = jnp.full(m_ref.shape, NEG, F32)
    acc_ref[...] = jnp.zeros(acc_ref.shape, F32)
    if window is None:
        first = 0
        stage(first)
    else:
        o_full = (window - T + 1) // T
        o_none = (window + T - 1) // T + 1
        assert o_none - o_full == 2
        o = o_full + 1
        j = i - o
        jc = jnp.maximum(j, 0)
        first = jnp.maximum(i - o_full, 0)
        s_band = scores(jc)
        stage(first)
        limit = jnp.where(j >= 0, window - o * T, -2 * T)
        absorb(jc, (qcol - krow) <= limit, s_all=s_band)

    def body(j, carry):
        absorb(j, None, staged_next=j + 1)
        return carry

    lax.fori_loop(first, i, body, 0)
    absorb(i, krow <= qcol)

    for pair in range(2):
        outs = []
        for e in range(2):
            acc = acc_ref[2 * pair + e]
            outs.append(acc[0:HEAD_DIM, :] * (1.0 / acc[HEAD_DIM:HEAD_DIM + 1, :]))
        o_ref[:, pair * LANES:(pair + 1) * LANES] = jnp.concatenate(outs, axis=0).T.astype(BF16)


def _flash_t(qt, k, kcol, kw, vt, *, B, S, window=None, vrow0=0, aux=None, name):
    kshared = kw == LANES
    in_specs = [pl.BlockSpec((None, None, BLK, TB), lambda b, i: (b, i, 0, 0))]
    args = [qt]
    if aux is not None:
        kind, qx, kx, kxcol = aux
        in_specs.append(pl.BlockSpec((None, None, LANES, TB), lambda b, i: (b, i, 0, 0)))
        args.append(qx)
    in_specs.append(pl.BlockSpec((None, S, kw), lambda b, i: (b, 0, kcol)))
    args.append(k)
    if aux is not None:
        if kx.ndim == 3:
            in_specs.append(pl.BlockSpec((None, S, LANES), lambda b, i: (b, 0, kxcol)))
        else:
            in_specs.append(pl.BlockSpec((S, LANES), lambda b, i: (0, kxcol)))
        args.append(kx)
    in_specs.append(pl.BlockSpec((None,) + vt.shape[1:], lambda b, i: (b, 0, 0, 0)))
    args.append(vt)
    return pl.pallas_call(
        functools.partial(_flash_t_kernel, window=window, kshared=kshared, vrow0=vrow0,
                          aux=None if aux is None else aux[0]),
        grid=(B, S // TB),
        in_specs=in_specs,
        out_specs=pl.BlockSpec((None, TB, BLK), lambda b, i: (b, i, 0)),
        out_shape=jax.ShapeDtypeStruct((B, S, BLK), BF16),
        scratch_shapes=[pltpu.VMEM((N_HEADS, TB, TB), F32),
                        pltpu.VMEM((N_HEADS, 1, TB), F32),
                        pltpu.VMEM((N_HEADS, HEAD_DIM + ONES_ROWS, TB), F32)],
        compiler_params=pltpu.CompilerParams(dimension_semantics=("arbitrary", "arbitrary"),
                                             vmem_limit_bytes=VMEM_LIMIT),
        name=name,
    )(*args)


def _mla_up_kernel(cq_ref, ckv_ref, gq_ref, gkv_ref, wq_ref, wk_ref, wvt_ref, c32_ref, s32_ref,
                   qn_ref, qr_ref, kn_ref, vt_ref):
    def norm(t, g):
        t = t.astype(F32)
        ms = jnp.mean(t * t, axis=-1, keepdims=True)
        return ((t * lax.rsqrt(ms + EPS)) * g).astype(BF16)

    scale = LOG2E * (MLA_NOPE + MLA_ROPE) ** -0.5
    q = _dot(norm(cq_ref[...], gq_ref[...]), wq_ref[...])
    qn_t = (q[:, :BLK] * scale).T
    qr_t = (_rope_half(q[:, BLK:], c32_ref[...], s32_ref[...], MLA_ROPE) * scale).T
    ckv = norm(ckv_ref[...], gkv_ref[...])
    kn_ref[...] = _dot(ckv, wk_ref[...]).astype(BF16)
    vt = _dot_nt(wvt_ref[...], ckv)
    for blk in range(vt.shape[1] // TB):
        cols = slice(blk * TB, (blk + 1) * TB)
        qn_ref[blk] = qn_t[:, cols].astype(BF16)
        qr_ref[blk] = qr_t[:, cols].astype(BF16)
        vt_ref[blk] = vt[:, cols].astype(BF16)


def _mla_up(proj, g_q, g_kv, w_uq, w_ukv, c32, s32, B, S):
    rows = proj.shape[0]
    tm = 1024
    per_b = S // tm
    wq = w_uq.reshape(MLA_Q_RANK, N_HEADS, MLA_NOPE + MLA_ROPE)
    wq = jnp.concatenate([wq[:, :, :MLA_NOPE].reshape(MLA_Q_RANK, -1),
                          wq[:, :, MLA_NOPE:].reshape(MLA_Q_RANK, -1)], axis=1).astype(BF16)
    wkv = w_ukv.reshape(MLA_KV_RANK, N_HEADS, MLA_NOPE + MLA_V)
    wk = wkv[:, :, :MLA_NOPE].reshape(MLA_KV_RANK, -1).astype(BF16)
    wvt = wkv[:, :, MLA_NOPE:].reshape(MLA_KV_RANK, -1).T.astype(BF16)
    row_spec = lambda w, c: pl.BlockSpec((tm, w), lambda i: (i, c))
    full = lambda a: pl.BlockSpec(a.shape, lambda i: (0, 0))
    gq = g_q.reshape(1, -1)
    gkv = g_kv.reshape(1, -1)
    fm_shape = lambda r: jax.ShapeDtypeStruct((B, S // TB, r, TB), BF16)
    fm_spec = lambda r: pl.BlockSpec((None, tm // TB, r, TB),
                                     lambda i: (i // per_b, i % per_b, 0, 0))
    outs = [fm_shape(BLK), fm_shape(LANES), jax.ShapeDtypeStruct((rows, BLK), BF16), fm_shape(BLK)]
    return pl.pallas_call(
        _mla_up_kernel,
        grid=(rows // tm,),
        in_specs=[row_spec(BLK, PB_BCQ), row_spec(LANES, 2 * PB_BKV), full(gq), full(gkv),
                  full(wq), full(wk), full(wvt), row_spec(LANES, 0), row_spec(LANES, 0)],
        out_specs=[fm_spec(BLK), fm_spec(LANES), row_spec(BLK, 0), fm_spec(BLK)],
        out_shape=outs,
        compiler_params=pltpu.CompilerParams(dimension_semantics=("arbitrary",),
                                             vmem_limit_bytes=VMEM_LIMIT),
        name="mla_up",
    )(proj, proj, gq, gkv, wq, wk, wvt, c32, s32)


def _compress_kernel(x_ref, pos_ref, w1_ref, w2_ref, kc_ref, vc_ref):
    x = x_ref[...].astype(F32)
    half = x.shape[1]
    top = (x + pos_ref[0:1, :]).astype(BF16)
    bot = (x + pos_ref[1:2, :]).astype(BF16)
    y_top = _dot(top, w1_ref[0:half, :])
    y_bot = _dot(bot, w1_ref[half:2 * half, :])
    pre = y_top + pltpu.roll(y_bot, y_bot.shape[0] - 1, 0)
    hid = (pre * (1.0 / (1.0 + jnp.exp(-pre)))).astype(BF16)
    kc_ref[...] = _dot(hid[:, :NSA_CMP_HID], w2_ref[0]).astype(BF16)
    vc_ref[...] = _dot(hid[:, NSA_CMP_HID:], w2_ref[1]).astype(BF16)


def _compress(kcvc, pos_k, pos_v, kw1, kw2, vw1, vw2, B, S):
    n16 = S // NSA_CMP_STRIDE
    x16 = kcvc.reshape(B, n16, NSA_CMP_STRIDE * LANES)
    pos = jnp.concatenate([pos_k, pos_v], axis=1).reshape(2, NSA_CMP_STRIDE * LANES)
    w1 = jnp.zeros((NSA_CMP_LEN, LANES, 2 * NSA_CMP_HID), F32)
    w1 = w1.at[:, :HEAD_DIM, :NSA_CMP_HID].set(kw1.reshape(NSA_CMP_LEN, HEAD_DIM, NSA_CMP_HID))
    w1 = w1.at[:, HEAD_DIM:, NSA_CMP_HID:].set(vw1.reshape(NSA_CMP_LEN, HEAD_DIM, NSA_CMP_HID))
    w1 = w1.reshape(NSA_CMP_LEN * LANES, 2 * NSA_CMP_HID).astype(BF16)
    w2 = jnp.stack([jnp.tile(kw2, (1, 2)), jnp.tile(vw2, (1, 2))]).astype(BF16)
    out = jax.ShapeDtypeStruct((B, n16, LANES), BF16)
    return pl.pallas_call(
        _compress_kernel,
        grid=(B,),
        in_specs=[pl.BlockSpec((None, n16, NSA_CMP_STRIDE * LANES), lambda b: (b, 0, 0)),
                  pl.BlockSpec(pos.shape, lambda b: (0, 0)),
                  pl.BlockSpec(w1.shape, lambda b: (0, 0)),
                  pl.BlockSpec(w2.shape, lambda b: (0, 0, 0))],
        out_specs=[pl.BlockSpec((None, n16, LANES), lambda b: (b, 0, 0))] * 2,
        out_shape=[out, out],
        compiler_params=pltpu.CompilerParams(dimension_semantics=("arbitrary",),
                                             vmem_limit_bytes=VMEM_LIMIT),
        name="nsa_compress",
    )(x16, pos, w1, w2)


def _cmp_topk_kernel(q_ref, kc_ref, vc_ref, ovt_ref, o_ref, nsel_ref, *, T, n_cmp, n_slc):
    i = pl.program_id(1)
    t0 = i * T
    lane = lax.broadcasted_iota(jnp.int32, (T, LANES), 1)
    tok = t0 + lax.broadcasted_iota(jnp.int32, (T, LANES), 0)
    ok = jnp.logical_and(lane * NSA_CMP_STRIDE + (NSA_CMP_LEN - 1) <= tok, lane < n_cmp)
    lo = lane < HEAD_DIM
    qm = _masked_heads(q_ref, lo)
    kc = kc_ref[...]
    vc = vc_ref[...]
    psum = jnp.zeros((T, LANES), F32)
    outs = []
    for h in range(N_HEADS):
        s = jnp.where(ok, _dot_nt(qm[h], kc), NEG)
        mx = jnp.max(s, axis=-1, keepdims=True)
        e = jnp.where(ok, jnp.exp2(s - mx), 0.0)
        den = jnp.sum(e, axis=-1, keepdims=True)
        p = e * (1.0 / jnp.maximum(den, 1e-30))
        psum = psum + p
        outs.append(_dot(p.astype(BF16), vc))
    for pair in range(2):
        o_ref[:, pair * LANES:(pair + 1) * LANES] = jnp.where(
            lo, outs[2 * pair], outs[2 * pair + 1]).astype(BF16)

    p_hi = psum.astype(BF16)
    p_lo = (psum - p_hi.astype(F32)).astype(BF16)
    ovt = ovt_ref[...]
    imp = _dot_nt(ovt, p_hi) + _dot_nt(ovt, p_lo)
    jj = lax.broadcasted_iota(jnp.int32, (n_slc, T), 0)
    bt = (t0 + lax.broadcasted_iota(jnp.int32, (n_slc, T), 1)) >> 6
    forced = jnp.logical_or(jj == 0, jnp.logical_or(jj == bt, jj == bt - 1))
    val = jnp.where(forced, BIG, jnp.where(jj > bt, -BIG, imp))
    sel_rows = []
    for j in range(n_slc):
        vj = val[j:j + 1, :]
        beats = jnp.logical_or(val > vj, jnp.logical_and(val == vj, jj < j))
        rank = jnp.sum(beats.astype(F32), axis=0, keepdims=True)
        sel_rows.append((rank >= NSA_N_SEL).astype(F32))
    sel_rows.append(jnp.zeros((LANES - n_slc, T), F32))
    nsel = jnp.concatenate(sel_rows, axis=0).astype(BF16)
    for blk in range(T // TB):
        nsel_ref[blk] = nsel[:, blk * TB:(blk + 1) * TB]


def _cmp_topk(proj3, kc2, vc2, B, S):
    T = 2 * TB
    n_cmp = (S - NSA_CMP_LEN) // NSA_CMP_STRIDE + 1
    n_slc = S // NSA_SEL_LEN
    assert NSA_SEL_LEN == 64 and n_slc <= LANES and n_cmp <= LANES
    ci = np.arange(LANES)[None, :] * NSA_CMP_STRIDE
    sj = np.arange(n_slc)[:, None] * NSA_SEL_LEN
    ovt = ((ci < sj + NSA_SEL_LEN) & (ci + NSA_CMP_LEN > sj) & (np.arange(LANES)[None, :] < n_cmp))
    ovt = jnp.asarray(ovt.astype(np.float32), BF16)
    return pl.pallas_call(
        functools.partial(_cmp_topk_kernel, T=T, n_cmp=n_cmp, n_slc=n_slc),
        grid=(B, S // T),
        in_specs=[pl.BlockSpec((None, T, BLK), lambda b, i: (b, i, PB_CQ)),
                  pl.BlockSpec((None, LANES, LANES), lambda b, i: (b, 0, 0)),
                  pl.BlockSpec((None, LANES, LANES), lambda b, i: (b, 0, 0)),
                  pl.BlockSpec(ovt.shape, lambda b, i: (0, 0))],
        out_specs=[pl.BlockSpec((None, T, BLK), lambda b, i: (b, i, 0)),
                   pl.BlockSpec((None, T // TB, LANES, TB), lambda b, i: (b, i, 0, 0))],
        out_shape=[jax.ShapeDtypeStruct((B, S, BLK), BF16),
                   jax.ShapeDtypeStruct((B, S // TB, LANES, TB), BF16)],
        compiler_params=pltpu.CompilerParams(dimension_semantics=("arbitrary", "arbitrary"),
                                             vmem_limit_bytes=VMEM_LIMIT),
        name="nsa_cmp_topk",
    )(proj3, kc2, vc2, ovt)


def _stick_kernel(q_ref, k_ref, vt_ref, u_ref, o_ref, z_ref, carry_ref, acc_ref):
    T = TB
    i = pl.program_id(1)
    qm = _masked_head_rows(q_ref)
    krow = lax.broadcasted_iota(jnp.int32, (T, T), 0)
    qcol = lax.broadcasted_iota(jnp.int32, (T, T), 1)
    strict = krow < qcol
    u = u_ref[...]

    def stage(j):
        rows = pl.ds(pl.multiple_of(j * T, T), T)
        for h in range(N_HEADS):
            z_ref[h] = _dot(k_ref[rows, (h // 2) * LANES:(h // 2 + 1) * LANES], qm[h])

    def absorb(j, diag, staged_next):
        vt_all = vt_ref[j]
        softplus = []
        for h in range(N_HEADS):
            z = z_ref[h]
            sp = jnp.maximum(z, 0.0) + jnp.log2(1.0 + jnp.exp2(-jnp.abs(z)))
            if diag:
                sp = jnp.where(strict, sp, 0.0)
            softplus.append(sp.astype(BF16))
        sums = [_dot(u, sp) for sp in softplus]
        weights = []
        for h in range(N_HEADS):
            a = jnp.exp2(z_ref[h] - (sums[h] + carry_ref[h]))
            if diag:
                a = jnp.where(strict, a, 0.0)
            weights.append(a.astype(BF16))
        if staged_next is not None:
            stage(staged_next)
        for h in range(N_HEADS):
            acc_ref[h] = acc_ref[h] + _dot(vt_all[h * HEAD_DIM:(h + 1) * HEAD_DIM, :], weights[h])
            carry_ref[h] = carry_ref[h] + sums[h][0:1, :]

    carry_ref[...] = jnp.zeros(carry_ref.shape, F32)
    acc_ref[...] = jnp.zeros(acc_ref.shape, F32)
    stage(i)
    absorb(i, True, jnp.maximum(i - 1, 0))

    def body(jj, carry):
        j = i - 1 - jj
        absorb(j, False, j - 1)
        return carry

    lax.fori_loop(0, i - 1, body, 0)

    @pl.when(i > 0)
    def _():
        absorb(0, False, None)

    for pair in range(2):
        o_t = jnp.concatenate([acc_ref[2 * pair], acc_ref[2 * pair + 1]], axis=0)
        o_ref[:, pair * LANES:(pair + 1) * LANES] = o_t.T.astype(BF16)


def _stick_breaking(qdt, proj3, vdt, B, S):
    u = np.triu(np.ones((TB, TB), np.float32))
    return pl.pallas_call(
        _stick_kernel,
        grid=(B, S // TB),
        in_specs=[pl.BlockSpec((None, None, BLK, TB), lambda b, i: (b, i, 0, 0)),
                  pl.BlockSpec((None, S, BLK), lambda b, i: (b, 0, PB_DK)),
                  pl.BlockSpec((None,) + vdt.shape[1:], lambda b, i: (b, 0, 0, 0)),
                  pl.BlockSpec(u.shape, lambda b, i: (0, 0))],
        out_specs=pl.BlockSpec((None, TB, BLK), lambda b, i: (b, i, 0)),
        out_shape=jax.ShapeDtypeStruct((B, S, BLK), BF16),
        scratch_shapes=[pltpu.VMEM((N_HEADS, TB, TB), F32),
                        pltpu.VMEM((N_HEADS, 1, TB), F32),
                        pltpu.VMEM((N_HEADS, HEAD_DIM, TB), F32)],
        compiler_params=pltpu.CompilerParams(dimension_semantics=("arbitrary", "arbitrary"),
                                             vmem_limit_bytes=VMEM_LIMIT),
        name="stick_breaking",
    )(qdt, proj3, vdt, jnp.asarray(u, BF16))


def _outproj_kernel(x_ref, oa, ob, oc, os_, ow, od,
                    ga, gb, gl, gc, gd, expand_ref, w_ref, g_ref, out_ref):
    f = lambda r: r[...].astype(F32)
    gates = _dot(gl[...], expand_ref[...])
    o_c = (gates[:, 0:BLK] * f(oc) + gates[:, BLK:2 * BLK] * f(os_)
           + gates[:, 2 * BLK:3 * BLK] * f(ow))
    mixed = [f(oa) * f(ga), f(ob) * f(gb), o_c * f(gc), f(od) * f(gd)]
    y = _dot(mixed[0].astype(BF16), w_ref[0:BLK, :])
    for g in range(1, 4):
        y = y + _dot(mixed[g].astype(BF16), w_ref[g * BLK:(g + 1) * BLK, :])
    ms = jnp.mean(y * y, axis=-1, keepdims=True)
    out_ref[...] = x_ref[...] + (y * lax.rsqrt(ms + EPS)) * g_ref[...]


def _out_projection(x2, mixer_outs, proj, w_out, g_post):
    rows = x2.shape[0]
    tm = TILE
    blk = lambda c: pl.BlockSpec((tm, BLK), lambda i: (i, c))
    expand = np.zeros((LANES, N_BRANCH * BLK), np.float32)
    for h in range(N_HEADS):
        for r in range(N_BRANCH):
            expand[N_BRANCH * h + r, r * BLK + h * HEAD_DIM:r * BLK + (h + 1) * HEAD_DIM] = 1.0
    gate_specs = [blk(PB_AG), blk(PB_BG),
                  pl.BlockSpec((tm, LANES), lambda i: (i, 2 * PB_C1 + 1)), blk(PB_CG), blk(PB_DG)]
    return pl.pallas_call(
        _outproj_kernel,
        grid=(rows // tm,),
        in_specs=[pl.BlockSpec((tm, D_MODEL), lambda i: (i, 0))]
                 + [blk(0)] * len(mixer_outs) + gate_specs
                 + [pl.BlockSpec(expand.shape, lambda i: (0, 0)),
                    pl.BlockSpec((D_MODEL, D_MODEL), lambda i: (0, 0)),
                    pl.BlockSpec((1, D_MODEL), lambda i: (0, 0))],
        out_specs=pl.BlockSpec((tm, D_MODEL), lambda i: (i, 0)),
        out_shape=jax.ShapeDtypeStruct((rows, D_MODEL), F32),
        compiler_params=pltpu.CompilerParams(dimension_semantics=("arbitrary",),
                                             vmem_limit_bytes=VMEM_LIMIT),
        name="out_projection",
    )(x2, *mixer_outs, *([proj] * len(gate_specs)), jnp.asarray(expand, BF16),
      w_out.astype(BF16), g_post.reshape(1, D_MODEL))


def _layer(x2, tabs, B, S, w_in, w_out, g_pre, g_post, mla_g_q, mla_g_kv, mla_w_uq, mla_w_ukv,
           nsa_pos_k, nsa_pos_v, nsa_k_w1, nsa_k_w2, nsa_v_w1, nsa_v_w2):
    rows = B * S
    c64, s64, c32, s32 = tabs
    w_rows, w_t = _pack_w_in(w_in)
    aqkv, aqkv4, aqkv16, proj, vct, vdt, qct, qdt = _in_projection(
        x2, g_pre, w_rows, w_t, tabs, B, S)
    proj3 = proj.reshape(B, S, N_PBLK * BLK)

    o_a = _dilated(aqkv, aqkv4, aqkv16, B, S)

    qnt, qrt, kn, vbt = _mla_up(proj, mla_g_q, mla_g_kv, mla_w_uq, mla_w_ukv, c32, s32, B, S)
    o_b = _flash_t(qnt, kn.reshape(B, S, BLK), 0, BLK, vbt, B=B, S=S,
                   aux=("per_head", qrt, proj3, 2 * PB_BKV + 1), name="mla_attention")

    kcvc = proj3[:, :, PB_C0 * BLK:PB_C0 * BLK + LANES]
    kc2, vc2 = _compress(kcvc, nsa_pos_k, nsa_pos_v, nsa_k_w1, nsa_k_w2, nsa_v_w1, nsa_v_w2, B, S)
    o_cmp, nsel = _cmp_topk(proj3, kc2, vc2, B, S)
    penalty = np.zeros((S, LANES), np.float32)
    penalty[np.arange(S), np.arange(S) // NSA_SEL_LEN] = MASK_PENALTY
    o_slc = _flash_t(qct, proj3, 2 * PB_C0 + 1, LANES, vct, B=B, S=S, vrow0=0,
                     aux=("shared", nsel, jnp.asarray(penalty, BF16), 0), name="nsa_selected")
    o_win = _flash_t(qct, proj3, 2 * PB_C1, LANES, vct, B=B, S=S, vrow0=HEAD_DIM,
                     window=NSA_WINDOW - 1, name="nsa_window")

    o_d = _stick_breaking(qdt, proj3, vdt, B, S)

    flat = lambda t: t.reshape(rows, BLK)
    outs = [flat(o) for o in (o_a, o_b, o_cmp, o_slc, o_win, o_d)]
    return _out_projection(x2, outs, proj, w_out, g_post)


def kernel(x, positions, w_in, w_out, g_pre, g_post, mla_g_q, mla_g_kv, mla_w_uq, mla_w_ukv,
           nsa_pos_k, nsa_pos_v, nsa_k_w1, nsa_k_w2, nsa_v_w1, nsa_v_w2):
    B, S, D = x.shape
    assert D == D_MODEL and S % 1024 == 0
    x2 = x.reshape(B * S, D)
    tabs = _rope_tables(positions)
    for l in range(w_in.shape[0]):
        x2 = _layer(x2, tabs, B, S, w_in[l], w_out[l], g_pre[l], g_post[l],
                    mla_g_q[l], mla_g_kv[l], mla_w_uq[l], mla_w_ukv[l],
                    nsa_pos_k[l], nsa_pos_v[l], nsa_k_w1[l], nsa_k_w2[l],
                    nsa_v_w1[l], nsa_v_w2[l])
    return x2.reshape(B, S, D)
```

```python
import functools
import math

import numpy as np
import jax
import jax.numpy as jnp
from jax import lax
from jax.experimental import pallas as pl
from jax.experimental.pallas import tpu as pltpu

D_MODEL = 1024
HEAD_DIM = 64
N_HEADS = 4
ROPE_THETA = 10000.0
EPS = 1e-6
NEG = -1e30
LOG2E = math.log2(math.e)
MASK_PENALTY = -(2.0 ** 100)
BIG = 1e9

MLA_Q_RANK = 256
MLA_KV_RANK = 128
MLA_NOPE = 64
MLA_ROPE = 32
MLA_V = 64

NSA_CMP_LEN = 32
NSA_CMP_STRIDE = 16
NSA_CMP_HID = 256
NSA_SEL_LEN = 64
NSA_N_SEL = 16
NSA_WINDOW = 512

LANES = 128
BLK = 256
TILE = 512
DILATIONS = (1, 4, 16)
DIL_BLOCK = 128
LSE_PARTS = 2
TB = 256
ONES_ROWS = 16
VMEM_LIMIT = 48 * 1024 * 1024

F32 = jnp.float32
BF16 = jnp.bfloat16

(PB_AG, PB_BCQ, PB_BKV, PB_BG, PB_CQ, PB_C0, PB_C1, PB_CG, PB_DQ, PB_DK, PB_DG) = range(11)
N_PBLK = 11
N_BRANCH = 3
ROPE_NONE, ROPE_64, ROPE_32, ROPE_64_LO, ACT_SILU, ACT_SIGMOID = range(6)


def _dot_nt(a, b):
    return lax.dot_general(a, b, (((1,), (1,)), ((), ())), preferred_element_type=F32)


def _dot(a, b):
    return jnp.dot(a, b, preferred_element_type=F32)


def _rope_half(y, cos, sin_signed, group):
    half = group // 2
    lane = lax.broadcasted_iota(jnp.int32, y.shape, 1)
    first = (lane & (group - 1)) < half
    partner = jnp.where(first, pltpu.roll(y, LANES - half, 1), pltpu.roll(y, half, 1))
    return y * cos + partner * sin_signed


def _rope_table_kernel(pos_ref, inv_ref, cos_ref, sin_ref):
    ang = inv_ref[...] * pos_ref[...].astype(F32)
    cos_ref[...] = jnp.cos(ang)
    sin_ref[...] = jnp.sin(ang)


def _rope_tables(positions):
    rows = positions.size
    tm = 2048
    inv = np.concatenate([ROPE_THETA ** (-np.arange(n, dtype=np.float64) / n)
                          for n in (HEAD_DIM // 2, MLA_ROPE // 2)]).astype(np.float32)[:, None]
    out = jax.ShapeDtypeStruct((inv.shape[0], rows), F32)
    tab_spec = pl.BlockSpec((inv.shape[0], tm), lambda i: (0, i))
    return pl.pallas_call(
        _rope_table_kernel,
        grid=(rows // tm,),
        in_specs=[pl.BlockSpec((1, tm), lambda i: (0, i)),
                  pl.BlockSpec(inv.shape, lambda i: (0, 0))],
        out_specs=[tab_spec] * 2,
        out_shape=[out] * 2,
        name="rope_tables",
    )(positions.reshape(1, rows), jnp.asarray(inv))


def _token_major_rope(cos_ref, sin_ref, group):
    n = group // 2
    lo = 0 if group == HEAD_DIM else HEAD_DIM // 2
    c = cos_ref[lo:lo + n, :]
    s = sin_ref[lo:lo + n, :]
    reps = LANES // n
    cos = jnp.concatenate([c] * reps, axis=0).T
    sin = jnp.concatenate([-s if k % 2 == 0 else s for k in range(reps)], axis=0).T
    return cos, sin


def _inproj_kernel(x_ref, g_ref, w_ref, wt_ref, perm_ref, cos_ref, sin_ref,
                   aqkv_ref, aqkv4_ref, aqkv16_ref, proj_ref, vct_ref, vdt_ref, qct_ref, qdt_ref,
                   *, plan_a, plan_p):
    x = x_ref[...]
    tm = x.shape[0]
    ms = jnp.mean(x * x, axis=-1, keepdims=True)
    h = ((x * lax.rsqrt(ms + EPS)) * g_ref[...]).astype(BF16)
    lane = lax.broadcasted_iota(jnp.int32, (tm, LANES), 1)
    c64, s64 = _token_major_rope(cos_ref, sin_ref, HEAD_DIM)
    c32, s32 = _token_major_rope(cos_ref, sin_ref, MLA_ROPE)

    def epilogue(y, ops, scale):
        halves = []
        for hf, op in enumerate(ops):
            yh = y[:, hf * LANES:(hf + 1) * LANES]
            if op == ROPE_64:
                yh = _rope_half(yh, c64, s64, 64)
            elif op == ROPE_32:
                yh = _rope_half(yh, c32, s32, 32)
            elif op == ROPE_64_LO:
                yh = jnp.where(lane < 64, _rope_half(yh, c64, s64, 64), yh)
            elif op == ACT_SILU:
                yh = yh * (1.0 / (1.0 + jnp.exp(-yh)))
            elif op == ACT_SIGMOID:
                yh = 1.0 / (1.0 + jnp.exp(-yh))
            if scale != 1.0:
                yh = yh * scale
            halves.append(yh)
        return jnp.concatenate(halves, axis=1)

    feature_major = {PB_CQ: qct_ref, PB_DQ: qdt_ref}
    col = 0
    for out_ref, plan in ((aqkv_ref, plan_a), (proj_ref, plan_p)):
        for b, (ops, scale) in enumerate(plan):
            y = epilogue(_dot(h, w_ref[:, col:col + BLK]), ops, scale)
            out_ref[:, b * BLK:(b + 1) * BLK] = y.astype(BF16)
            if out_ref is proj_ref and b in feature_major:
                y_t = y.T
                for blk in range(tm // TB):
                    feature_major[b][blk] = y_t[:, blk * TB:(blk + 1) * TB].astype(BF16)
            col += BLK

    a_rows = aqkv_ref[...]
    aqkv4_ref[...] = _dot(perm_ref[0], a_rows).astype(BF16)
    aqkv16_ref[...] = _dot(perm_ref[1], a_rows).astype(BF16)

    vt = _dot_nt(wt_ref[...], h)
    for blk in range(tm // TB):
        cols = slice(blk * TB, (blk + 1) * TB)
        vct_ref[blk] = vt[0:LANES, cols].astype(BF16)
        vdt_ref[blk] = vt[LANES:LANES + BLK, cols].astype(BF16)


_PLAN_A = (((ROPE_64, ROPE_64), LOG2E * HEAD_DIM ** -0.5),
           ((ROPE_64, ROPE_64), 1.0),
           ((ROPE_NONE, ROPE_NONE), 1.0))
_PLAN_P = (
    ((ACT_SILU, ACT_SILU), 1.0),
    ((ROPE_NONE, ROPE_NONE), 1.0),
    ((ROPE_NONE, ROPE_32), 1.0),
    ((ACT_SILU, ACT_SILU), 1.0),
    ((ROPE_64, ROPE_64), LOG2E * HEAD_DIM ** -0.5),
    ((ROPE_64_LO, ROPE_64), 1.0),
    ((ROPE_64, ACT_SIGMOID), 1.0),
    ((ACT_SILU, ACT_SILU), 1.0),
    ((ROPE_NONE, ROPE_NONE), LOG2E * HEAD_DIM ** -0.5),
    ((ROPE_NONE, ROPE_NONE), 1.0),
    ((ACT_SILU, ACT_SILU), 1.0),
)
assert len(_PLAN_P) == N_PBLK


def _pack_w_in(w):
    w = w.astype(BF16)
    o = 0
    seg = {}
    for name, width in (("aq", 256), ("ak", 256), ("av", 256), ("ag", 256),
                        ("bcq", 256), ("bckv", 128), ("bkr", 32), ("bg", 256),
                        ("cq", 256), ("ckc", 64), ("cvc", 64), ("cks", 64), ("cvs", 64),
                        ("ckw", 64), ("cvw", 64), ("cgl", 12), ("cg", 256),
                        ("dq", 256), ("dk", 256), ("dv", 256), ("dg", 256)):
        seg[name] = w[:, o:o + width]
        o += width
    gl_pad = jnp.zeros((D_MODEL, LANES - N_HEADS * N_BRANCH), w.dtype)
    cols = [seg["aq"], seg["ak"], seg["av"],
            seg["ag"], seg["bcq"],
            seg["bckv"], jnp.tile(seg["bkr"], (1, 4)),
            seg["bg"], seg["cq"],
            seg["ckc"], seg["cvc"], seg["cks"], seg["cks"],
            seg["ckw"], seg["ckw"], seg["cgl"], gl_pad,
            seg["cg"], seg["dq"], seg["dk"], seg["dg"]]
    w_rows = jnp.concatenate(cols, axis=1).astype(BF16)
    w_t = jnp.concatenate([seg["cvs"], seg["cvw"], seg["dv"]], axis=1).T.astype(BF16)
    return w_rows, w_t


def _class_major_perm(dil):
    n = TILE // dil
    p = np.zeros((TILE, TILE), np.float32)
    pos = np.arange(TILE)
    p[(pos % dil) * n + pos // dil, pos] = 1.0
    return p


def _in_projection(x2, g_pre, w_rows, w_t, tabs, B, S):
    rows = x2.shape[0]
    tm = TILE
    per_b = S // tm
    nkb = tm // TB
    perm = jnp.asarray(np.stack([_class_major_perm(d) for d in DILATIONS[1:]]), BF16)
    tab_spec = pl.BlockSpec((tabs[0].shape[0], tm), lambda i: (0, i))
    vt_spec = lambda r: pl.BlockSpec((None, nkb, r, TB), lambda i: (i // per_b, i % per_b, 0, 0))
    a_spec = pl.BlockSpec((tm, 3 * BLK), lambda i: (i, 0))
    a_shape = jax.ShapeDtypeStruct((rows, 3 * BLK), BF16)
    return pl.pallas_call(
        functools.partial(_inproj_kernel, plan_a=_PLAN_A, plan_p=_PLAN_P),
        grid=(rows // tm,),
        in_specs=[pl.BlockSpec((tm, D_MODEL), lambda i: (i, 0)),
                  pl.BlockSpec((1, D_MODEL), lambda i: (0, 0)),
                  pl.BlockSpec(w_rows.shape, lambda i: (0, 0)),
                  pl.BlockSpec(w_t.shape, lambda i: (0, 0)),
                  pl.BlockSpec(perm.shape, lambda i: (0, 0, 0))] + [tab_spec] * 2,
        out_specs=[a_spec, a_spec, a_spec,
                   pl.BlockSpec((tm, N_PBLK * BLK), lambda i: (i, 0)),
                   vt_spec(LANES), vt_spec(BLK), vt_spec(BLK), vt_spec(BLK)],
        out_shape=[a_shape, a_shape, a_shape,
                   jax.ShapeDtypeStruct((rows, N_PBLK * BLK), BF16),
                   jax.ShapeDtypeStruct((B, S // TB, LANES, TB), BF16)]
                  + [jax.ShapeDtypeStruct((B, S // TB, BLK, TB), BF16)] * 3,
        compiler_params=pltpu.CompilerParams(dimension_semantics=("arbitrary",),
                                             vmem_limit_bytes=VMEM_LIMIT),
        name="in_projection",
    )(x2, g_pre.reshape(1, D_MODEL), w_rows, w_t, perm, *tabs)


def _head_masks(shape):
    lane = lax.broadcasted_iota(jnp.int32, shape, 1)
    return lane < HEAD_DIM


def _masked_heads(q_ref, lo_mask):
    out = []
    for h in range(N_HEADS):
        pair, e = divmod(h, 2)
        qp = q_ref[:, pair * LANES:(pair + 1) * LANES]
        keep = lo_mask if e == 0 else jnp.logical_not(lo_mask)
        out.append(jnp.where(keep, qp, jnp.zeros_like(qp)))
    return out


def _dilated_kernel(an_ref, a4_ref, a16_ref, unperm_ref, expand_ref, o_ref, res_ref, s_ref):
    T = DIL_BLOCK
    S = an_ref.shape[0]
    n_tiles = S // TILE
    lo = _head_masks((T, LANES))
    lane = lax.broadcasted_iota(jnp.int32, (T, LANES), 1)
    dist2 = (lax.broadcasted_iota(jnp.int32, (T, 2 * T), 0)
             - lax.broadcasted_iota(jnp.int32, (T, 2 * T), 1))
    causal1 = (lax.broadcasted_iota(jnp.int32, (T, T), 1)
               <= lax.broadcasted_iota(jnp.int32, (T, T), 0))

    def scores(q, k):
        qm = _masked_heads(q, lo)
        return tuple(_dot_nt(qm[h], k[:, (h // 2) * LANES:(h // 2 + 1) * LANES])
                     for h in range(N_HEADS))

    def finish(n_keys, v, mask, stage_next):
        probs = []
        for h in range(N_HEADS):
            s = jnp.where(mask, s_ref[h, :, 0:n_keys], NEG)
            m = jnp.max(s, axis=-1, keepdims=True)
            p = jnp.exp2(s - m)
            probs.append((m, jnp.sum(p, axis=-1, keepdims=True), p.astype(BF16)))
        stage_next()
        outs = []
        top = jnp.zeros((T, LANES), F32)
        den = jnp.ones((T, LANES), F32)
        for h in range(N_HEADS):
            m, l, p = probs[h]
            outs.append(_dot(p, v[:, (h // 2) * LANES:(h // 2 + 1) * LANES]) * (1.0 / l))
            mine = (lane & (N_HEADS - 1)) == h
            top = jnp.where(mine, m, top)
            den = jnp.where(mine, l, den)
        lse = top + jnp.log2(den)
        hi = lse.astype(BF16).astype(F32)
        parts = jnp.where(lane < N_HEADS, hi,
                          jnp.where(lane < LSE_PARTS * N_HEADS, lse - hi, 0.0))
        return jnp.concatenate(
            [jnp.where(lo, outs[0], outs[1]), jnp.where(lo, outs[2], outs[3]), parts],
            axis=1).astype(BF16)

    def band_mask(delta):
        d = dist2 + delta
        return jnp.logical_and(d >= 0, d <= DIL_BLOCK)

    def run(n_units, n_keys, load_qk, load_v, store, mask_of):
        def stage(u):
            for h, s in enumerate(scores(*load_qk(u))):
                s_ref[h, :, 0:n_keys] = s

        def body(u, carry):
            nxt = jnp.minimum(u + 1, n_units - 1)
            store(u, finish(n_keys, load_v(u), mask_of(u), lambda: stage(nxt)))
            return carry

        stage(0)
        lax.fori_loop(0, n_units, body, 0)

    def rows(start, size):
        return pl.ds(pl.multiple_of(start, 32), size)

    def k0_1(u):
        return T * jnp.maximum(u - 1, 0)

    def store1(u, r):
        res_ref[0, rows(T * u, T), :] = r

    run(S // T, 2 * T,
        lambda u: (an_ref[rows(T * u, T), 0:BLK], an_ref[rows(k0_1(u), 2 * T), BLK:2 * BLK]),
        lambda u: an_ref[rows(k0_1(u), 2 * T), 2 * BLK:3 * BLK],
        store1,
        lambda u: band_mask(T * u - k0_1(u)))

    def piece4(u, blk):
        return rows(TILE * blk + T * (u // n_tiles), T)

    def kb0_4(u):
        return jnp.maximum(u % n_tiles - 1, 0)

    def load4(u, cols):
        return jnp.concatenate([a4_ref[piece4(u, kb0_4(u)), cols],
                                a4_ref[piece4(u, kb0_4(u) + 1), cols]], axis=0)

    def store4(u, r):
        res_ref[1, piece4(u, u % n_tiles), :] = r

    run(4 * n_tiles, 2 * T,
        lambda u: (a4_ref[piece4(u, u % n_tiles), 0:BLK], load4(u, slice(BLK, 2 * BLK))),
        lambda u: load4(u, slice(2 * BLK, 3 * BLK)),
        store4,
        lambda u: band_mask(T * (u % n_tiles - kb0_4(u))))

    per = TILE // 16

    def load16(c, cols):
        return jnp.concatenate([a16_ref[rows(TILE * t + per * c, per), cols]
                                for t in range(n_tiles)], axis=0)

    def store16(c, r):
        for t in range(n_tiles):
            res_ref[2, rows(TILE * t + per * c, per), :] = r[per * t:per * (t + 1), :]

    run(16, T,
        lambda c: (load16(c, slice(0, BLK)), load16(c, slice(BLK, 2 * BLK))),
        lambda c: load16(c, slice(2 * BLK, 3 * BLK)),
        store16,
        lambda c: causal1)

    expand = expand_ref[...]
    for t in range(n_tiles):
        tile = slice(TILE * t, TILE * (t + 1))
        nat = [res_ref[0, tile, :].astype(F32),
               _dot(unperm_ref[0], res_ref[1, tile, :]),
               _dot(unperm_ref[1], res_ref[2, tile, :])]
        lse = [_dot(r[:, BLK:].astype(BF16), expand) for r in nat]
        top = jnp.maximum(lse[0], jnp.maximum(lse[1], lse[2]))
        w = [jnp.exp2(l - top) for l in lse]
        mixed = (w[0] * nat[0][:, :BLK] + w[1] * nat[1][:, :BLK] + w[2] * nat[2][:, :BLK])
        o_ref[tile, :] = (mixed / (w[0] + w[1] + w[2])).astype(BF16)


def _dilated(aqkv, aqkv4, aqkv16, B, S):
    assert S == 16 * DIL_BLOCK and S % TILE == 0 and TILE == 4 * DIL_BLOCK
    r3 = lambda t: t.reshape(B, S, 3 * BLK)
    unperm = jnp.asarray(np.stack([_class_major_perm(d).T for d in DILATIONS[1:]]), BF16)
    expand = np.zeros((LANES, BLK), np.float32)
    for h in range(N_HEADS):
        for part in range(LSE_PARTS):
            expand[N_HEADS * part + h, h * HEAD_DIM:(h + 1) * HEAD_DIM] = 1.0
    a_spec = pl.BlockSpec((None, S, 3 * BLK), lambda b: (b, 0, 0))
    return pl.pallas_call(
        _dilated_kernel,
        grid=(B,),
        in_specs=[a_spec, a_spec, a_spec,
                  pl.BlockSpec(unperm.shape, lambda b: (0, 0, 0)),
                  pl.BlockSpec(expand.shape, lambda b: (0, 0))],
        out_specs=pl.BlockSpec((None, S, BLK), lambda b: (b, 0, 0)),
        out_shape=jax.ShapeDtypeStruct((B, S, BLK), BF16),
        scratch_shapes=[pltpu.VMEM((3, S, BLK + LANES), BF16),
                        pltpu.VMEM((N_HEADS, DIL_BLOCK, 2 * DIL_BLOCK), F32)],
        compiler_params=pltpu.CompilerParams(dimension_semantics=("arbitrary",),
                                             vmem_limit_bytes=VMEM_LIMIT),
        name="dilated_mixture",
    )(r3(aqkv), r3(aqkv4), r3(aqkv16), unperm, jnp.asarray(expand, BF16))


def _masked_head_rows(qt_ref):
    out = []
    zeros = jnp.zeros((HEAD_DIM, qt_ref.shape[-1]), BF16)
    for h in range(N_HEADS):
        own = qt_ref[h * HEAD_DIM:(h + 1) * HEAD_DIM, :]
        out.append(jnp.concatenate([own, zeros] if h % 2 == 0 else [zeros, own], axis=0))
    return out


def _flash_t_kernel(*refs, window, kshared, vrow0, aux):
    it = iter(refs)
    q_ref = next(it)
    qx_ref = next(it) if aux else None
    k_ref = next(it)
    kx_ref = next(it) if aux else None
    vt_ref = next(it)
    o_ref = next(it)
    s_ref, m_ref, acc_ref = next(it), next(it), next(it)
    T = TB

    i = pl.program_id(1)
    qcat = _masked_head_rows(q_ref)
    if aux == "per_head":
        group = LANES // N_HEADS
        qcat = [jnp.concatenate(
            [qcat[h]] + [qx_ref[g * group:(g + 1) * group, :] if g == h
                         else jnp.zeros((group, T), BF16) for g in range(N_HEADS)], axis=0)
            for h in range(N_HEADS)]
    elif aux == "shared":
        qx = qx_ref[...]
        qcat = [jnp.concatenate([qcat[h], qx], axis=0) for h in range(N_HEADS)]
    krow = lax.broadcasted_iota(jnp.int32, (T, T), 0)
    qcol = lax.broadcasted_iota(jnp.int32, (T, T), 1)
    ones = jnp.ones((ONES_ROWS, T), BF16)

    def scores(j):
        rows = pl.ds(pl.multiple_of(j * T, T), T)
        out = []
        for h in range(N_HEADS):
            pair = h // 2
            cols = slice(0, LANES) if kshared else slice(pair * LANES, (pair + 1) * LANES)
            kb = k_ref[rows, cols]
            if aux:
                kb = jnp.concatenate([kb, kx_ref[rows, :]], axis=1)
            out.append(_dot(kb, qcat[h]))
        return out

    def stage(j):
        for h, s in enumerate(scores(j)):
            s_ref[h] = s

    def absorb(j, mask, staged_next=None, s_all=None):
        vt_all = vt_ref[j]
        probs = []
        for h in range(N_HEADS):
            s = s_ref[h] if s_all is None else s_all[h]
            if mask is not None:
                s = jnp.where(mask, s, NEG)
            m = m_ref[h]
            m_new = jnp.maximum(m, jnp.max(s, axis=0, keepdims=True))
            alpha = jnp.exp2(m - m_new)
            p = jnp.exp2(s - m_new)
            probs.append((m_new, alpha, p.astype(BF16)))
        if staged_next is not None:
            stage(staged_next)
        for h in range(N_HEADS):
            m_new, alpha, p = probs[h]
            r0 = vrow0 if kshared else h * HEAD_DIM
            lhs = jnp.concatenate([vt_all[r0:r0 + HEAD_DIM, :], ones], axis=0)
            acc_ref[h] = alpha * acc_ref[h] + _dot(lhs, p)
            m_ref[h] = m_new

    m_ref[...] = jnp.full(m_ref.shape, NEG, F32)
    acc_ref[...] = jnp.zeros(acc_ref.shape, F32)
    if window is None:
        first = 0
        stage(first)
    else:
        o_full = (window - T + 1) // T
        o_none = (window + T - 1) // T + 1
        assert o_none - o_full == 2
        o = o_full + 1
        j = i - o
        jc = jnp.maximum(j, 0)
        first = jnp.maximum(i - o_full, 0)
        s_band = scores(jc)
        stage(first)
        limit = jnp.where(j >= 0, window - o * T, -2 * T)
        absorb(jc, (qcol - krow) <= limit, s_all=s_band)

    def body(j, carry):
        absorb(j, None, staged_next=j + 1)
        return carry

    lax.fori_loop(first, i, body, 0)
    absorb(i, krow <= qcol)

    for pair in range(2):
        outs = []
        for e in range(2):
            acc = acc_ref[2 * pair + e]
            outs.append(acc[0:HEAD_DIM, :] * (1.0 / acc[HEAD_DIM:HEAD_DIM + 1, :]))
        o_ref[:, pair * LANES:(pair + 1) * LANES] = jnp.concatenate(outs, axis=0).T.astype(BF16)


def _flash_t(qt, k, kcol, kw, vt, *, B, S, window=None, vrow0=0, aux=None, name):
    kshared = kw == LANES
    in_specs = [pl.BlockSpec((None, None, BLK, TB), lambda b, i: (b, i, 0, 0))]
    args = [qt]
    if aux is not None:
        kind, qx, kx, kxcol = aux
        in_specs.append(pl.BlockSpec((None, None, LANES, TB), lambda b, i: (b, i, 0, 0)))
        args.append(qx)
    in_specs.append(pl.BlockSpec((None, S, kw), lambda b, i: (b, 0, kcol)))
    args.append(k)
    if aux is not None:
        if kx.ndim == 3:
            in_specs.append(pl.BlockSpec((None, S, LANES), lambda b, i: (b, 0, kxcol)))
        else:
            in_specs.append(pl.BlockSpec((S, LANES), lambda b, i: (0, kxcol)))
        args.append(kx)
    in_specs.append(pl.BlockSpec((None,) + vt.shape[1:], lambda b, i: (b, 0, 0, 0)))
    args.append(vt)
    return pl.pallas_call(
        functools.partial(_flash_t_kernel, window=window, kshared=kshared, vrow0=vrow0,
                          aux=None if aux is None else aux[0]),
        grid=(B, S // TB),
        in_specs=in_specs,
        out_specs=pl.BlockSpec((None, TB, BLK), lambda b, i: (b, i, 0)),
        out_shape=jax.ShapeDtypeStruct((B, S, BLK), BF16),
        scratch_shapes=[pltpu.VMEM((N_HEADS, TB, TB), F32),
                        pltpu.VMEM((N_HEADS, 1, TB), F32),
                        pltpu.VMEM((N_HEADS, HEAD_DIM + ONES_ROWS, TB), F32)],
        compiler_params=pltpu.CompilerParams(dimension_semantics=("arbitrary", "arbitrary"),
                                             vmem_limit_bytes=VMEM_LIMIT),
        name=name,
    )(*args)


def _mla_up_kernel(cq_ref, ckv_ref, gq_ref, gkv_ref, wq_ref, wk_ref, wvt_ref, cos_ref, sin_ref,
                   qn_ref, qr_ref, kn_ref, vt_ref):
    def norm(t, g):
        t = t.astype(F32)
        ms = jnp.mean(t * t, axis=-1, keepdims=True)
        return ((t * lax.rsqrt(ms + EPS)) * g).astype(BF16)

    scale = LOG2E * (MLA_NOPE + MLA_ROPE) ** -0.5
    q = _dot(norm(cq_ref[...], gq_ref[...]), wq_ref[...])
    qn_t = (q[:, :BLK] * scale).T
    c32, s32 = _token_major_rope(cos_ref, sin_ref, MLA_ROPE)
    qr_t = (_rope_half(q[:, BLK:], c32, s32, MLA_ROPE) * scale).T
    ckv = norm(ckv_ref[...], gkv_ref[...])
    kn_ref[...] = _dot(ckv, wk_ref[...]).astype(BF16)
    vt = _dot_nt(wvt_ref[...], ckv)
    for blk in range(vt.shape[1] // TB):
        cols = slice(blk * TB, (blk + 1) * TB)
        qn_ref[blk] = qn_t[:, cols].astype(BF16)
        qr_ref[blk] = qr_t[:, cols].astype(BF16)
        vt_ref[blk] = vt[:, cols].astype(BF16)


def _mla_up(proj, g_q, g_kv, w_uq, w_ukv, tabs, B, S):
    rows = proj.shape[0]
    tm = 1024
    per_b = S // tm
    wq = w_uq.reshape(MLA_Q_RANK, N_HEADS, MLA_NOPE + MLA_ROPE)
    wq = jnp.concatenate([wq[:, :, :MLA_NOPE].reshape(MLA_Q_RANK, -1),
                          wq[:, :, MLA_NOPE:].reshape(MLA_Q_RANK, -1)], axis=1).astype(BF16)
    wkv = w_ukv.reshape(MLA_KV_RANK, N_HEADS, MLA_NOPE + MLA_V)
    wk = wkv[:, :, :MLA_NOPE].reshape(MLA_KV_RANK, -1).astype(BF16)
    wvt = wkv[:, :, MLA_NOPE:].reshape(MLA_KV_RANK, -1).T.astype(BF16)
    row_spec = lambda w, c: pl.BlockSpec((tm, w), lambda i: (i, c))
    full = lambda a: pl.BlockSpec(a.shape, lambda i: (0, 0))
    gq = g_q.reshape(1, -1)
    gkv = g_kv.reshape(1, -1)
    fm_shape = lambda r: jax.ShapeDtypeStruct((B, S // TB, r, TB), BF16)
    fm_spec = lambda r: pl.BlockSpec((None, tm // TB, r, TB),
                                     lambda i: (i // per_b, i % per_b, 0, 0))
    outs = [fm_shape(BLK), fm_shape(LANES), jax.ShapeDtypeStruct((rows, BLK), BF16), fm_shape(BLK)]
    return pl.pallas_call(
        _mla_up_kernel,
        grid=(rows // tm,),
        in_specs=[row_spec(BLK, PB_BCQ), row_spec(LANES, 2 * PB_BKV), full(gq), full(gkv),
                  full(wq), full(wk), full(wvt)]
                 + [pl.BlockSpec((tabs[0].shape[0], tm), lambda i: (0, i))] * 2,
        out_specs=[fm_spec(BLK), fm_spec(LANES), row_spec(BLK, 0), fm_spec(BLK)],
        out_shape=outs,
        compiler_params=pltpu.CompilerParams(dimension_semantics=("arbitrary",),
                                             vmem_limit_bytes=VMEM_LIMIT),
        name="mla_up",
    )(proj, proj, gq, gkv, wq, wk, wvt, *tabs)


def _compress_kernel(x_ref, perm_ref, pos_ref, w1_ref, w2_ref, kc_ref, vc_ref):
    n_tiles = x_ref.shape[0] // TILE
    per = TILE // NSA_CMP_STRIDE
    slabs = []
    for t in range(n_tiles):
        cm = _dot(perm_ref[...], x_ref[TILE * t:TILE * (t + 1), :])
        slabs.append(jnp.concatenate([cm[per * c:per * (c + 1), :]
                                      for c in range(NSA_CMP_STRIDE)], axis=1))
    x = jnp.concatenate(slabs, axis=0)
    half = x.shape[1]
    top = (x + pos_ref[0:1, :]).astype(BF16)
    bot = (x + pos_ref[1:2, :]).astype(BF16)
    y_top = _dot(top, w1_ref[0:half, :])
    y_bot = _dot(bot, w1_ref[half:2 * half, :])
    pre = y_top + pltpu.roll(y_bot, y_bot.shape[0] - 1, 0)
    hid = (pre * (1.0 / (1.0 + jnp.exp(-pre)))).astype(BF16)
    kc_ref[...] = _dot(hid[:, :NSA_CMP_HID], w2_ref[0]).astype(BF16)
    vc_ref[...] = _dot(hid[:, NSA_CMP_HID:], w2_ref[1]).astype(BF16)


def _compress(proj3, pos_k, pos_v, kw1, kw2, vw1, vw2, B, S):
    assert S % TILE == 0 and NSA_CMP_LEN == 2 * NSA_CMP_STRIDE
    n16 = S // NSA_CMP_STRIDE
    perm = jnp.asarray(_class_major_perm(NSA_CMP_STRIDE), BF16)
    pos = jnp.concatenate([pos_k, pos_v], axis=1).reshape(2, NSA_CMP_STRIDE * LANES)
    w1 = jnp.zeros((NSA_CMP_LEN, LANES, 2 * NSA_CMP_HID), F32)
    w1 = w1.at[:, :HEAD_DIM, :NSA_CMP_HID].set(kw1.reshape(NSA_CMP_LEN, HEAD_DIM, NSA_CMP_HID))
    w1 = w1.at[:, HEAD_DIM:, NSA_CMP_HID:].set(vw1.reshape(NSA_CMP_LEN, HEAD_DIM, NSA_CMP_HID))
    w1 = w1.reshape(NSA_CMP_LEN * LANES, 2 * NSA_CMP_HID).astype(BF16)
    w2 = jnp.stack([jnp.tile(kw2, (1, 2)), jnp.tile(vw2, (1, 2))]).astype(BF16)
    out = jax.ShapeDtypeStruct((B, n16, LANES), BF16)
    return pl.pallas_call(
        _compress_kernel,
        grid=(B,),
        in_specs=[pl.BlockSpec((None, S, LANES), lambda b: (b, 0, 2 * PB_C0)),
                  pl.BlockSpec(perm.shape, lambda b: (0, 0)),
                  pl.BlockSpec(pos.shape, lambda b: (0, 0)),
                  pl.BlockSpec(w1.shape, lambda b: (0, 0)),
                  pl.BlockSpec(w2.shape, lambda b: (0, 0, 0))],
        out_specs=[pl.BlockSpec((None, n16, LANES), lambda b: (b, 0, 0))] * 2,
        out_shape=[out, out],
        compiler_params=pltpu.CompilerParams(dimension_semantics=("arbitrary",),
                                             vmem_limit_bytes=VMEM_LIMIT),
        name="nsa_compress",
    )(proj3, perm, pos, w1, w2)


def _cmp_topk_kernel(q_ref, kc_ref, vc_ref, ovt_ref, o_ref, nsel_ref, *, T, n_cmp, n_slc):
    i = pl.program_id(1)
    t0 = i * T
    lane = lax.broadcasted_iota(jnp.int32, (T, LANES), 1)
    tok = t0 + lax.broadcasted_iota(jnp.int32, (T, LANES), 0)
    ok = jnp.logical_and(lane * NSA_CMP_STRIDE + (NSA_CMP_LEN - 1) <= tok, lane < n_cmp)
    lo = lane < HEAD_DIM
    qm = _masked_heads(q_ref, lo)
    kc = kc_ref[...]
    vc = vc_ref[...]
    psum = jnp.zeros((T, LANES), F32)
    outs = []
    for h in range(N_HEADS):
        s = jnp.where(ok, _dot_nt(qm[h], kc), NEG)
        mx = jnp.max(s, axis=-1, keepdims=True)
        e = jnp.where(ok, jnp.exp2(s - mx), 0.0)
        den = jnp.sum(e, axis=-1, keepdims=True)
        p = e * (1.0 / jnp.maximum(den, 1e-30))
        psum = psum + p
        outs.append(_dot(p.astype(BF16), vc))
    for pair in range(2):
        o_ref[:, pair * LANES:(pair + 1) * LANES] = jnp.where(
            lo, outs[2 * pair], outs[2 * pair + 1]).astype(BF16)

    p_hi = psum.astype(BF16)
    p_lo = (psum - p_hi.astype(F32)).astype(BF16)
    ovt = ovt_ref[...]
    imp = _dot_nt(ovt, p_hi) + _dot_nt(ovt, p_lo)
    jj = lax.broadcasted_iota(jnp.int32, (n_slc, T), 0)
    bt = (t0 + lax.broadcasted_iota(jnp.int32, (n_slc, T), 1)) >> 6
    forced = jnp.logical_or(jj == 0, jnp.logical_or(jj == bt, jj == bt - 1))
    val = jnp.where(forced, BIG, jnp.where(jj > bt, -BIG, imp))
    sel_rows = []
    for j in range(n_slc):
        vj = val[j:j + 1, :]
        beats = jnp.logical_or(val > vj, jnp.logical_and(val == vj, jj < j))
        rank = jnp.sum(beats.astype(F32), axis=0, keepdims=True)
        sel_rows.append((rank >= NSA_N_SEL).astype(F32))
    sel_rows.append(jnp.zeros((LANES - n_slc, T), F32))
    nsel = jnp.concatenate(sel_rows, axis=0).astype(BF16)
    for blk in range(T // TB):
        nsel_ref[blk] = nsel[:, blk * TB:(blk + 1) * TB]


def _cmp_topk(proj3, kc2, vc2, B, S):
    T = 2 * TB
    n_cmp = (S - NSA_CMP_LEN) // NSA_CMP_STRIDE + 1
    n_slc = S // NSA_SEL_LEN
    assert NSA_SEL_LEN == 64 and n_slc <= LANES and n_cmp <= LANES
    ci = np.arange(LANES)[None, :] * NSA_CMP_STRIDE
    sj = np.arange(n_slc)[:, None] * NSA_SEL_LEN
    ovt = ((ci < sj + NSA_SEL_LEN) & (ci + NSA_CMP_LEN > sj) & (np.arange(LANES)[None, :] < n_cmp))
    ovt = jnp.asarray(ovt.astype(np.float32), BF16)
    return pl.pallas_call(
        functools.partial(_cmp_topk_kernel, T=T, n_cmp=n_cmp, n_slc=n_slc),
        grid=(B, S // T),
        in_specs=[pl.BlockSpec((None, T, BLK), lambda b, i: (b, i, PB_CQ)),
                  pl.BlockSpec((None, LANES, LANES), lambda b, i: (b, 0, 0)),
                  pl.BlockSpec((None, LANES, LANES), lambda b, i: (b, 0, 0)),
                  pl.BlockSpec(ovt.shape, lambda b, i: (0, 0))],
        out_specs=[pl.BlockSpec((None, T, BLK), lambda b, i: (b, i, 0)),
                   pl.BlockSpec((None, T // TB, LANES, TB), lambda b, i: (b, i, 0, 0))],
        out_shape=[jax.ShapeDtypeStruct((B, S, BLK), BF16),
                   jax.ShapeDtypeStruct((B, S // TB, LANES, TB), BF16)],
        compiler_params=pltpu.CompilerParams(dimension_semantics=("arbitrary", "arbitrary"),
                                             vmem_limit_bytes=VMEM_LIMIT),
        name="nsa_cmp_topk",
    )(proj3, kc2, vc2, ovt)


def _stick_kernel(q_ref, k_ref, vt_ref, u_ref, o_ref, za_ref, zb_ref, w_ref, carry_ref, acc_ref):
    T = TB
    i = pl.program_id(1)
    qm = _masked_head_rows(q_ref)
    krow = lax.broadcasted_iota(jnp.int32, (T, T), 0)
    qcol = lax.broadcasted_iota(jnp.int32, (T, T), 1)
    strict = krow < qcol
    u = u_ref[...]

    z_refs = (za_ref, zb_ref)

    def stage(j, slot):
        rows = pl.ds(pl.multiple_of(j * T, T), T)
        for h in range(N_HEADS):
            z_refs[slot][h] = _dot(k_ref[rows, (h // 2) * LANES:(h // 2 + 1) * LANES], qm[h])

    def apply_weights(j):
        vt_all = vt_ref[j]
        for h in range(N_HEADS):
            acc_ref[h] = acc_ref[h] + _dot(vt_all[h * HEAD_DIM:(h + 1) * HEAD_DIM, :], w_ref[h])

    def absorb(j, slot, diag, staged_next, applied_prev):
        z_ref = z_refs[slot]
        softplus = []
        for h in range(N_HEADS):
            z = z_ref[h]
            sp = jnp.maximum(z, 0.0) + jnp.log2(1.0 + jnp.exp2(-jnp.abs(z)))
            if diag:
                sp = jnp.where(strict, sp, 0.0)
            softplus.append(sp.astype(BF16))
        if staged_next is not None:
            stage(staged_next, 1 - slot)
        if applied_prev is not None:
            apply_weights(applied_prev)
        sums = [_dot(u, sp) for sp in softplus]
        for h in range(N_HEADS):
            a = jnp.exp2(z_ref[h] - (sums[h] + carry_ref[h]))
            if diag:
                a = jnp.where(strict, a, 0.0)
            w_ref[h] = a.astype(BF16)
            carry_ref[h] = carry_ref[h] + sums[h][0:1, :]

    carry_ref[...] = jnp.zeros(carry_ref.shape, F32)
    acc_ref[...] = jnp.zeros(acc_ref.shape, F32)
    stage(i, 0)
    absorb(i, 0, True, jnp.maximum(i - 1, 0), None)

    def body(t, carry):
        j = i - 1 - 2 * t
        absorb(j, 1, False, j - 1, j + 1)
        absorb(j - 1, 0, False, jnp.maximum(j - 2, 0), j)
        return carry

    lax.fori_loop(0, i // 2, body, 0)

    @pl.when(i % 2 == 1)
    def _():
        absorb(0, 1, False, None, 1)

    apply_weights(0)
    for pair in range(2):
        o_t = jnp.concatenate([acc_ref[2 * pair], acc_ref[2 * pair + 1]], axis=0)
        o_ref[:, pair * LANES:(pair + 1) * LANES] = o_t.T.astype(BF16)


def _stick_breaking(qdt, proj3, vdt, B, S):
    u = np.triu(np.ones((TB, TB), np.float32))
    return pl.pallas_call(
        _stick_kernel,
        grid=(B, S // TB),
        in_specs=[pl.BlockSpec((None, None, BLK, TB), lambda b, i: (b, i, 0, 0)),
                  pl.BlockSpec((None, S, BLK), lambda b, i: (b, 0, PB_DK)),
                  pl.BlockSpec((None,) + vdt.shape[1:], lambda b, i: (b, 0, 0, 0)),
                  pl.BlockSpec(u.shape, lambda b, i: (0, 0))],
        out_specs=pl.BlockSpec((None, TB, BLK), lambda b, i: (b, i, 0)),
        out_shape=jax.ShapeDtypeStruct((B, S, BLK), BF16),
        scratch_shapes=[pltpu.VMEM((N_HEADS, TB, TB), F32),
                        pltpu.VMEM((N_HEADS, TB, TB), F32),
                        pltpu.VMEM((N_HEADS, TB, TB), BF16),
                        pltpu.VMEM((N_HEADS, 1, TB), F32),
                        pltpu.VMEM((N_HEADS, HEAD_DIM, TB), F32)],
        compiler_params=pltpu.CompilerParams(dimension_semantics=("arbitrary", "arbitrary"),
                                             vmem_limit_bytes=VMEM_LIMIT),
        name="stick_breaking",
    )(qdt, proj3, vdt, jnp.asarray(u, BF16))


def _outproj_kernel(x_ref, oa, ob, oc, os_, ow, od,
                    ga, gb, gl, gc, gd, expand_ref, w_ref, g_ref, out_ref):
    f = lambda r: r[...].astype(F32)
    gates = _dot(gl[...], expand_ref[...])
    o_c = (gates[:, 0:BLK] * f(oc) + gates[:, BLK:2 * BLK] * f(os_)
           + gates[:, 2 * BLK:3 * BLK] * f(ow))
    mixed = [f(oa) * f(ga), f(ob) * f(gb), o_c * f(gc), f(od) * f(gd)]
    y = _dot(mixed[0].astype(BF16), w_ref[0:BLK, :])
    for g in range(1, 4):
        y = y + _dot(mixed[g].astype(BF16), w_ref[g * BLK:(g + 1) * BLK, :])
    ms = jnp.mean(y * y, axis=-1, keepdims=True)
    out_ref[...] = x_ref[...] + (y * lax.rsqrt(ms + EPS)) * g_ref[...]


def _out_projection(x2, mixer_outs, proj, w_out, g_post):
    rows = x2.shape[0]
    tm = TILE
    blk = lambda c: pl.BlockSpec((tm, BLK), lambda i: (i, c))
    expand = np.zeros((LANES, N_BRANCH * BLK), np.float32)
    for h in range(N_HEADS):
        for r in range(N_BRANCH):
            expand[N_BRANCH * h + r, r * BLK + h * HEAD_DIM:r * BLK + (h + 1) * HEAD_DIM] = 1.0
    gate_specs = [blk(PB_AG), blk(PB_BG),
                  pl.BlockSpec((tm, LANES), lambda i: (i, 2 * PB_C1 + 1)), blk(PB_CG), blk(PB_DG)]
    return pl.pallas_call(
        _outproj_kernel,
        grid=(rows // tm,),
        in_specs=[pl.BlockSpec((tm, D_MODEL), lambda i: (i, 0))]
                 + [blk(0)] * len(mixer_outs) + gate_specs
                 + [pl.BlockSpec(expand.shape, lambda i: (0, 0)),
                    pl.BlockSpec((D_MODEL, D_MODEL), lambda i: (0, 0)),
                    pl.BlockSpec((1, D_MODEL), lambda i: (0, 0))],
        out_specs=pl.BlockSpec((tm, D_MODEL), lambda i: (i, 0)),
        out_shape=jax.ShapeDtypeStruct((rows, D_MODEL), F32),
        compiler_params=pltpu.CompilerParams(dimension_semantics=("arbitrary",),
                                             vmem_limit_bytes=VMEM_LIMIT),
        name="out_projection",
    )(x2, *mixer_outs, *([proj] * len(gate_specs)), jnp.asarray(expand, BF16),
      w_out.astype(BF16), g_post.reshape(1, D_MODEL))


def _layer(x2, tabs, B, S, w_in, w_out, g_pre, g_post, mla_g_q, mla_g_kv, mla_w_uq, mla_w_ukv,
           nsa_pos_k, nsa_pos_v, nsa_k_w1, nsa_k_w2, nsa_v_w1, nsa_v_w2):
    rows = B * S
    w_rows, w_t = _pack_w_in(w_in)
    aqkv, aqkv4, aqkv16, proj, vct, vdt, qct, qdt = _in_projection(
        x2, g_pre, w_rows, w_t, tabs, B, S)
    proj3 = proj.reshape(B, S, N_PBLK * BLK)

    o_a = _dilated(aqkv, aqkv4, aqkv16, B, S)

    qnt, qrt, kn, vbt = _mla_up(proj, mla_g_q, mla_g_kv, mla_w_uq, mla_w_ukv, tabs, B, S)
    o_b = _flash_t(qnt, kn.reshape(B, S, BLK), 0, BLK, vbt, B=B, S=S,
                   aux=("per_head", qrt, proj3, 2 * PB_BKV + 1), name="mla_attention")

    kc2, vc2 = _compress(proj3, nsa_pos_k, nsa_pos_v, nsa_k_w1, nsa_k_w2, nsa_v_w1, nsa_v_w2, B, S)
    o_cmp, nsel = _cmp_topk(proj3, kc2, vc2, B, S)
    penalty = np.zeros((S, LANES), np.float32)
    penalty[np.arange(S), np.arange(S) // NSA_SEL_LEN] = MASK_PENALTY
    o_slc = _flash_t(qct, proj3, 2 * PB_C0 + 1, LANES, vct, B=B, S=S, vrow0=0,
                     aux=("shared", nsel, jnp.asarray(penalty, BF16), 0), name="nsa_selected")
    o_win = _flash_t(qct, proj3, 2 * PB_C1, LANES, vct, B=B, S=S, vrow0=HEAD_DIM,
                     window=NSA_WINDOW - 1, name="nsa_window")

    o_d = _stick_breaking(qdt, proj3, vdt, B, S)

    flat = lambda t: t.reshape(rows, BLK)
    outs = [flat(o) for o in (o_a, o_b, o_cmp, o_slc, o_win, o_d)]
    return _out_projection(x2, outs, proj, w_out, g_post)


def kernel(x, positions, w_in, w_out, g_pre, g_post, mla_g_q, mla_g_kv, mla_w_uq, mla_w_ukv,
           nsa_pos_k, nsa_pos_v, nsa_k_w1, nsa_k_w2, nsa_v_w1, nsa_v_w2):
    B, S, D = x.shape
    assert D == D_MODEL and S % 1024 == 0
    x2 = x.reshape(B * S, D)
    tabs = _rope_tables(positions)
    for l in range(w_in.shape[0]):
        x2 = _layer(x2, tabs, B, S, w_in[l], w_out[l], g_pre[l], g_post[l],
                    mla_g_q[l], mla_g_kv[l], mla_w_uq[l], mla_w_ukv[l],
                    nsa_pos_k[l], nsa_pos_v[l], nsa_k_w1[l], nsa_k_w2[l],
                    nsa_v_w1[l], nsa_v_w2[l])
    return x2.reshape(B, S, D)
```

```python
import functools
import math

import numpy as np
import jax
import jax.numpy as jnp
from jax import lax
from jax.experimental import pallas as pl
from jax.experimental.pallas import tpu as pltpu

D_MODEL = 1024
HEAD_DIM = 64
N_HEADS = 4
ROPE_THETA = 10000.0
EPS = 1e-6
NEG = -1e30
LOG2E = math.log2(math.e)
MASK_PENALTY = -(2.0 ** 100)
BIG = 1e9

MLA_Q_RANK = 256
MLA_KV_RANK = 128
MLA_NOPE = 64
MLA_ROPE = 32
MLA_V = 64

NSA_CMP_LEN = 32
NSA_CMP_STRIDE = 16
NSA_CMP_HID = 256
NSA_SEL_LEN = 64
NSA_N_SEL = 16
NSA_WINDOW = 512

LANES = 128
BLK = 256
TILE = 512
DILATIONS = (1, 4, 16)
DIL_BLOCK = 128
LSE_PARTS = 2
TB = 256
ONES_ROWS = 16
VMEM_LIMIT = 48 * 1024 * 1024

F32 = jnp.float32
BF16 = jnp.bfloat16

(PB_AG, PB_BCQ, PB_BKV, PB_BG, PB_CQ, PB_C0, PB_C1, PB_CG, PB_DQ, PB_DK, PB_DG) = range(11)
N_PBLK = 11
N_BRANCH = 3
ROPE_NONE, ROPE_64, ROPE_32, ROPE_64_LO, ACT_SILU, ACT_SIGMOID = range(6)


def _dot_nt(a, b):
    return lax.dot_general(a, b, (((1,), (1,)), ((), ())), preferred_element_type=F32)


def _dot(a, b):
    return jnp.dot(a, b, preferred_element_type=F32)


def _rope_half(y, cos, sin_signed, group):
    half = group // 2
    lane = lax.broadcasted_iota(jnp.int32, y.shape, 1)
    first = (lane & (group - 1)) < half
    partner = jnp.where(first, pltpu.roll(y, LANES - half, 1), pltpu.roll(y, half, 1))
    return y * cos + partner * sin_signed


def _rope_table_kernel(pos_ref, inv_ref, cos_ref, sin_ref):
    ang = inv_ref[...] * pos_ref[...].astype(F32)
    cos_ref[...] = jnp.cos(ang)
    sin_ref[...] = jnp.sin(ang)


def _rope_tables(positions):
    rows = positions.size
    tm = 2048
    inv = np.concatenate([ROPE_THETA ** (-np.arange(n, dtype=np.float64) / n)
                          for n in (HEAD_DIM // 2, MLA_ROPE // 2)]).astype(np.float32)[:, None]
    out = jax.ShapeDtypeStruct((inv.shape[0], rows), F32)
    tab_spec = pl.BlockSpec((inv.shape[0], tm), lambda i: (0, i))
    return pl.pallas_call(
        _rope_table_kernel,
        grid=(rows // tm,),
        in_specs=[pl.BlockSpec((1, tm), lambda i: (0, i)),
                  pl.BlockSpec(inv.shape, lambda i: (0, 0))],
        out_specs=[tab_spec] * 2,
        out_shape=[out] * 2,
        name="rope_tables",
    )(positions.reshape(1, rows), jnp.asarray(inv))


def _token_major_rope(cos_ref, sin_ref, group):
    n = group // 2
    lo = 0 if group == HEAD_DIM else HEAD_DIM // 2
    c = cos_ref[lo:lo + n, :]
    s = sin_ref[lo:lo + n, :]
    reps = LANES // n
    cos = jnp.concatenate([c] * reps, axis=0).T
    sin = jnp.concatenate([-s if k % 2 == 0 else s for k in range(reps)], axis=0).T
    return cos, sin


def _inproj_kernel(x_ref, g_ref, w_ref, wt_ref, perm_ref, cos_ref, sin_ref,
                   aqkv_ref, aqkv4_ref, aqkv16_ref, proj_ref, vct_ref, vdt_ref, qct_ref, qdt_ref,
                   *, plan_a, plan_p):
    x = x_ref[...]
    tm = x.shape[0]
    ms = jnp.mean(x * x, axis=-1, keepdims=True)
    h = ((x * lax.rsqrt(ms + EPS)) * g_ref[...]).astype(BF16)
    lane = lax.broadcasted_iota(jnp.int32, (tm, LANES), 1)
    c64, s64 = _token_major_rope(cos_ref, sin_ref, HEAD_DIM)
    c32, s32 = _token_major_rope(cos_ref, sin_ref, MLA_ROPE)

    def epilogue(y, ops, scale):
        halves = []
        for hf, op in enumerate(ops):
            yh = y[:, hf * LANES:(hf + 1) * LANES]
            if op == ROPE_64:
                yh = _rope_half(yh, c64, s64, 64)
            elif op == ROPE_32:
                yh = _rope_half(yh, c32, s32, 32)
            elif op == ROPE_64_LO:
                yh = jnp.where(lane < 64, _rope_half(yh, c64, s64, 64), yh)
            elif op == ACT_SILU:
                yh = yh * (1.0 / (1.0 + jnp.exp(-yh)))
            elif op == ACT_SIGMOID:
                yh = 1.0 / (1.0 + jnp.exp(-yh))
            if scale != 1.0:
                yh = yh * scale
            halves.append(yh)
        return jnp.concatenate(halves, axis=1)

    feature_major = {PB_CQ: qct_ref, PB_DQ: qdt_ref}
    col = 0
    for out_ref, plan in ((aqkv_ref, plan_a), (proj_ref, plan_p)):
        for b, (ops, scale) in enumerate(plan):
            y = epilogue(_dot(h, w_ref[:, col:col + BLK]), ops, scale)
            out_ref[:, b * BLK:(b + 1) * BLK] = y.astype(BF16)
            if out_ref is proj_ref and b in feature_major:
                y_t = y.T
                for blk in range(tm // TB):
                    feature_major[b][blk] = y_t[:, blk * TB:(blk + 1) * TB].astype(BF16)
            col += BLK

    a_rows = aqkv_ref[...]
    aqkv4_ref[...] = _dot(perm_ref[0], a_rows).astype(BF16)
    aqkv16_ref[...] = _dot(perm_ref[1], a_rows).astype(BF16)

    vt = _dot_nt(wt_ref[...], h)
    for blk in range(tm // TB):
        cols = slice(blk * TB, (blk + 1) * TB)
        vct_ref[blk] = vt[0:LANES, cols].astype(BF16)
        vdt_ref[blk] = vt[LANES:LANES + BLK, cols].astype(BF16)


_PLAN_A = (((ROPE_64, ROPE_64), LOG2E * HEAD_DIM ** -0.5),
           ((ROPE_64, ROPE_64), 1.0),
           ((ROPE_NONE, ROPE_NONE), 1.0))
_PLAN_P = (
    ((ACT_SILU, ACT_SILU), 1.0),
    ((ROPE_NONE, ROPE_NONE), 1.0),
    ((ROPE_NONE, ROPE_32), 1.0),
    ((ACT_SILU, ACT_SILU), 1.0),
    ((ROPE_64, ROPE_64), LOG2E * HEAD_DIM ** -0.5),
    ((ROPE_64_LO, ROPE_64), 1.0),
    ((ROPE_64, ACT_SIGMOID), 1.0),
    ((ACT_SILU, ACT_SILU), 1.0),
    ((ROPE_NONE, ROPE_NONE), LOG2E * HEAD_DIM ** -0.5),
    ((ROPE_NONE, ROPE_NONE), 1.0),
    ((ACT_SILU, ACT_SILU), 1.0),
)
assert len(_PLAN_P) == N_PBLK


_W_IN_SEGMENTS = (("aq", 256), ("ak", 256), ("av", 256), ("ag", 256),
                  ("bcq", 256), ("bckv", 128), ("bkr", 32), ("bg", 256),
                  ("cq", 256), ("ckc", 64), ("cvc", 64), ("cks", 64), ("cvs", 64),
                  ("ckw", 64), ("cvw", 64), ("cgl", 12), ("cg", 256),
                  ("dq", 256), ("dk", 256), ("dv", 256), ("dg", 256))
W_ROWS_COLS = (3 + N_PBLK) * BLK
W_T_ROWS = LANES + BLK


def _pack_w_kernel(w_ref, rows_ref, t_ref):
    w = w_ref[...]
    o = 0
    seg = {}
    for name, width in _W_IN_SEGMENTS:
        seg[name] = w[:, o:o + width]
        o += width
    gl_pad = jnp.zeros((w.shape[0], LANES - N_HEADS * N_BRANCH), w.dtype)
    cols = [seg["aq"], seg["ak"], seg["av"],
            seg["ag"], seg["bcq"],
            seg["bckv"], seg["bkr"], seg["bkr"], seg["bkr"], seg["bkr"],
            seg["bg"], seg["cq"],
            seg["ckc"], seg["cvc"], seg["cks"], seg["cks"],
            seg["ckw"], seg["ckw"], seg["cgl"], gl_pad,
            seg["cg"], seg["dq"], seg["dk"], seg["dg"]]
    rows_ref[...] = jnp.concatenate(cols, axis=1).astype(BF16)
    t_ref[...] = jnp.concatenate([seg["cvs"], seg["cvw"], seg["dv"]], axis=1).T.astype(BF16)


def _pack_w_in(w_in):
    depth, d_model, d_in = w_in.shape
    assert d_model == D_MODEL and d_in == sum(width for _, width in _W_IN_SEGMENTS)
    tr = 256
    return pl.pallas_call(
        _pack_w_kernel,
        grid=(depth, D_MODEL // tr),
        in_specs=[pl.BlockSpec((None, tr, d_in), lambda l, i: (l, i, 0))],
        out_specs=[pl.BlockSpec((None, tr, W_ROWS_COLS), lambda l, i: (l, i, 0)),
                   pl.BlockSpec((None, W_T_ROWS, tr), lambda l, i: (l, 0, i))],
        out_shape=[jax.ShapeDtypeStruct((depth, D_MODEL, W_ROWS_COLS), BF16),
                   jax.ShapeDtypeStruct((depth, W_T_ROWS, D_MODEL), BF16)],
        compiler_params=pltpu.CompilerParams(dimension_semantics=("arbitrary", "arbitrary"),
                                             vmem_limit_bytes=VMEM_LIMIT),
        name="pack_w_in",
    )(w_in)


def _class_major_perm(dil):
    n = TILE // dil
    p = np.zeros((TILE, TILE), np.float32)
    pos = np.arange(TILE)
    p[(pos % dil) * n + pos // dil, pos] = 1.0
    return p


def _in_projection(x2, g_pre, w_rows, w_t, layer, tabs, B, S):
    rows = x2.shape[0]
    tm = TILE
    per_b = S // tm
    nkb = tm // TB
    perm = jnp.asarray(np.stack([_class_major_perm(d) for d in DILATIONS[1:]]), BF16)
    tab_spec = pl.BlockSpec((tabs[0].shape[0], tm), lambda i: (0, i))
    vt_spec = lambda r: pl.BlockSpec((None, nkb, r, TB), lambda i: (i // per_b, i % per_b, 0, 0))
    a_spec = pl.BlockSpec((tm, 3 * BLK), lambda i: (i, 0))
    a_shape = jax.ShapeDtypeStruct((rows, 3 * BLK), BF16)
    return pl.pallas_call(
        functools.partial(_inproj_kernel, plan_a=_PLAN_A, plan_p=_PLAN_P),
        grid=(rows // tm,),
        in_specs=[pl.BlockSpec((tm, D_MODEL), lambda i: (i, 0)),
                  pl.BlockSpec((1, D_MODEL), lambda i: (0, 0)),
                  pl.BlockSpec((None,) + w_rows.shape[1:], lambda i: (layer, 0, 0)),
                  pl.BlockSpec((None,) + w_t.shape[1:], lambda i: (layer, 0, 0)),
                  pl.BlockSpec(perm.shape, lambda i: (0, 0, 0))] + [tab_spec] * 2,
        out_specs=[a_spec, a_spec, a_spec,
                   pl.BlockSpec((tm, N_PBLK * BLK), lambda i: (i, 0)),
                   vt_spec(LANES), vt_spec(BLK), vt_spec(BLK), vt_spec(BLK)],
        out_shape=[a_shape, a_shape, a_shape,
                   jax.ShapeDtypeStruct((rows, N_PBLK * BLK), BF16),
                   jax.ShapeDtypeStruct((B, S // TB, LANES, TB), BF16)]
                  + [jax.ShapeDtypeStruct((B, S // TB, BLK, TB), BF16)] * 3,
        compiler_params=pltpu.CompilerParams(dimension_semantics=("arbitrary",),
                                             vmem_limit_bytes=VMEM_LIMIT),
        name="in_projection",
    )(x2, g_pre.reshape(1, D_MODEL), w_rows, w_t, perm, *tabs)


def _head_masks(shape):
    lane = lax.broadcasted_iota(jnp.int32, shape, 1)
    return lane < HEAD_DIM


def _masked_heads(q_ref, lo_mask):
    out = []
    for h in range(N_HEADS):
        pair, e = divmod(h, 2)
        qp = q_ref[:, pair * LANES:(pair + 1) * LANES]
        keep = lo_mask if e == 0 else jnp.logical_not(lo_mask)
        out.append(jnp.where(keep, qp, jnp.zeros_like(qp)))
    return out


def _dilated_kernel(an_ref, a4_ref, a16_ref, unperm_ref, expand_ref, o_ref, res_ref, s_ref):
    T = DIL_BLOCK
    S = an_ref.shape[0]
    n_tiles = S // TILE
    lo = _head_masks((T, LANES))
    lane = lax.broadcasted_iota(jnp.int32, (T, LANES), 1)
    dist2 = (lax.broadcasted_iota(jnp.int32, (T, 2 * T), 0)
             - lax.broadcasted_iota(jnp.int32, (T, 2 * T), 1))
    causal1 = (lax.broadcasted_iota(jnp.int32, (T, T), 1)
               <= lax.broadcasted_iota(jnp.int32, (T, T), 0))

    def scores(q, k):
        qm = _masked_heads(q, lo)
        return tuple(_dot_nt(qm[h], k[:, (h // 2) * LANES:(h // 2 + 1) * LANES])
                     for h in range(N_HEADS))

    def finish(n_keys, v, mask, stage_next):
        probs = []
        for h in range(N_HEADS):
            s = jnp.where(mask, s_ref[h, :, 0:n_keys], NEG)
            m = jnp.max(s, axis=-1, keepdims=True)
            p = jnp.exp2(s - m)
            probs.append((m, jnp.sum(p, axis=-1, keepdims=True), p.astype(BF16)))
        stage_next()
        outs = []
        top = jnp.zeros((T, LANES), F32)
        den = jnp.ones((T, LANES), F32)
        for h in range(N_HEADS):
            m, l, p = probs[h]
            outs.append(_dot(p, v[:, (h // 2) * LANES:(h // 2 + 1) * LANES]) * (1.0 / l))
            mine = (lane & (N_HEADS - 1)) == h
            top = jnp.where(mine, m, top)
            den = jnp.where(mine, l, den)
        lse = top + jnp.log2(den)
        hi = lse.astype(BF16).astype(F32)
        parts = jnp.where(lane < N_HEADS, hi,
                          jnp.where(lane < LSE_PARTS * N_HEADS, lse - hi, 0.0))
        return jnp.concatenate(
            [jnp.where(lo, outs[0], outs[1]), jnp.where(lo, outs[2], outs[3]), parts],
            axis=1).astype(BF16)

    def band_mask(delta):
        d = dist2 + delta
        return jnp.logical_and(d >= 0, d <= DIL_BLOCK)

    def run(n_units, n_keys, load_qk, load_v, store, mask_of):
        def stage(u):
            for h, s in enumerate(scores(*load_qk(u))):
                s_ref[h, :, 0:n_keys] = s

        def body(u, carry):
            nxt = jnp.minimum(u + 1, n_units - 1)
            store(u, finish(n_keys, load_v(u), mask_of(u), lambda: stage(nxt)))
            return carry

        stage(0)
        lax.fori_loop(0, n_units, body, 0)

    def rows(start, size):
        return pl.ds(pl.multiple_of(start, 32), size)

    def k0_1(u):
        return T * jnp.maximum(u - 1, 0)

    def store1(u, r):
        res_ref[0, rows(T * u, T), :] = r

    run(S // T, 2 * T,
        lambda u: (an_ref[rows(T * u, T), 0:BLK], an_ref[rows(k0_1(u), 2 * T), BLK:2 * BLK]),
        lambda u: an_ref[rows(k0_1(u), 2 * T), 2 * BLK:3 * BLK],
        store1,
        lambda u: band_mask(T * u - k0_1(u)))

    def piece4(u, blk):
        return rows(TILE * blk + T * (u // n_tiles), T)

    def kb0_4(u):
        return jnp.maximum(u % n_tiles - 1, 0)

    def load4(u, cols):
        return jnp.concatenate([a4_ref[piece4(u, kb0_4(u)), cols],
                                a4_ref[piece4(u, kb0_4(u) + 1), cols]], axis=0)

    def store4(u, r):
        res_ref[1, piece4(u, u % n_tiles), :] = r

    run(4 * n_tiles, 2 * T,
        lambda u: (a4_ref[piece4(u, u % n_tiles), 0:BLK], load4(u, slice(BLK, 2 * BLK))),
        lambda u: load4(u, slice(2 * BLK, 3 * BLK)),
        store4,
        lambda u: band_mask(T * (u % n_tiles - kb0_4(u))))

    per = TILE // 16

    def load16(c, cols):
        return jnp.concatenate([a16_ref[rows(TILE * t + per * c, per), cols]
                                for t in range(n_tiles)], axis=0)

    def store16(c, r):
        for t in range(n_tiles):
            res_ref[2, rows(TILE * t + per * c, per), :] = r[per * t:per * (t + 1), :]

    run(16, T,
        lambda c: (load16(c, slice(0, BLK)), load16(c, slice(BLK, 2 * BLK))),
        lambda c: load16(c, slice(2 * BLK, 3 * BLK)),
        store16,
        lambda c: causal1)

    expand = expand_ref[...]
    for t in range(n_tiles):
        tile = slice(TILE * t, TILE * (t + 1))
        nat = [res_ref[0, tile, :].astype(F32),
               _dot(unperm_ref[0], res_ref[1, tile, :]),
               _dot(unperm_ref[1], res_ref[2, tile, :])]
        lse = [_dot(r[:, BLK:].astype(BF16), expand) for r in nat]
        top = jnp.maximum(lse[0], jnp.maximum(lse[1], lse[2]))
        w = [jnp.exp2(l - top) for l in lse]
        mixed = (w[0] * nat[0][:, :BLK] + w[1] * nat[1][:, :BLK] + w[2] * nat[2][:, :BLK])
        o_ref[tile, :] = (mixed / (w[0] + w[1] + w[2])).astype(BF16)


def _dilated(aqkv, aqkv4, aqkv16, B, S):
    assert S == 16 * DIL_BLOCK and S % TILE == 0 and TILE == 4 * DIL_BLOCK
    r3 = lambda t: t.reshape(B, S, 3 * BLK)
    unperm = jnp.asarray(np.stack([_class_major_perm(d).T for d in DILATIONS[1:]]), BF16)
    expand = np.zeros((LANES, BLK), np.float32)
    for h in range(N_HEADS):
        for part in range(LSE_PARTS):
            expand[N_HEADS * part + h, h * HEAD_DIM:(h + 1) * HEAD_DIM] = 1.0
    a_spec = pl.BlockSpec((None, S, 3 * BLK), lambda b: (b, 0, 0))
    return pl.pallas_call(
        _dilated_kernel,
        grid=(B,),
        in_specs=[a_spec, a_spec, a_spec,
                  pl.BlockSpec(unperm.shape, lambda b: (0, 0, 0)),
                  pl.BlockSpec(expand.shape, lambda b: (0, 0))],
        out_specs=pl.BlockSpec((None, S, BLK), lambda b: (b, 0, 0)),
        out_shape=jax.ShapeDtypeStruct((B, S, BLK), BF16),
        scratch_shapes=[pltpu.VMEM((3, S, BLK + LANES), BF16),
                        pltpu.VMEM((N_HEADS, DIL_BLOCK, 2 * DIL_BLOCK), F32)],
        compiler_params=pltpu.CompilerParams(dimension_semantics=("arbitrary",),
                                             vmem_limit_bytes=VMEM_LIMIT),
        name="dilated_mixture",
    )(r3(aqkv), r3(aqkv4), r3(aqkv16), unperm, jnp.asarray(expand, BF16))


def _masked_head_rows(qt_ref):
    out = []
    zeros = jnp.zeros((HEAD_DIM, qt_ref.shape[-1]), BF16)
    for h in range(N_HEADS):
        own = qt_ref[h * HEAD_DIM:(h + 1) * HEAD_DIM, :]
        out.append(jnp.concatenate([own, zeros] if h % 2 == 0 else [zeros, own], axis=0))
    return out


def _flash_t_kernel(*refs, window, kshared, vrow0, aux):
    it = iter(refs)
    q_ref = next(it)
    qx_ref = next(it) if aux else None
    k_ref = next(it)
    kx_ref = next(it) if aux else None
    vt_ref = next(it)
    o_ref = next(it)
    s_ref, m_ref, acc_ref = next(it), next(it), next(it)
    T = TB

    i = pl.program_id(1)
    qcat = _masked_head_rows(q_ref)
    if aux == "per_head":
        group = LANES // N_HEADS
        qcat = [jnp.concatenate(
            [qcat[h]] + [qx_ref[g * group:(g + 1) * group, :] if g == h
                         else jnp.zeros((group, T), BF16) for g in range(N_HEADS)], axis=0)
            for h in range(N_HEADS)]
    elif aux == "shared":
        qx = qx_ref[...]
        qcat = [jnp.concatenate([qcat[h], qx], axis=0) for h in range(N_HEADS)]
    krow = lax.broadcasted_iota(jnp.int32, (T, T), 0)
    qcol = lax.broadcasted_iota(jnp.int32, (T, T), 1)
    ones = jnp.ones((ONES_ROWS, T), BF16)

    def scores(j):
        rows = pl.ds(pl.multiple_of(j * T, T), T)
        out = []
        for h in range(N_HEADS):
            pair = h // 2
            cols = slice(0, LANES) if kshared else slice(pair * LANES, (pair + 1) * LANES)
            kb = k_ref[rows, cols]
            if aux:
                kb = jnp.concatenate([kb, kx_ref[rows, :]], axis=1)
            out.append(_dot(kb, qcat[h]))
        return out

    def stage(j):
        for h, s in enumerate(scores(j)):
            s_ref[h] = s

    def absorb(j, mask, staged_next=None, s_all=None):
        vt_all = vt_ref[j]
        probs = []
        for h in range(N_HEADS):
            s = s_ref[h] if s_all is None else s_all[h]
            if mask is not None:
                s = jnp.where(mask, s, NEG)
            m = m_ref[h]
            m_new = jnp.maximum(m, jnp.max(s, axis=0, keepdims=True))
            alpha = jnp.exp2(m - m_new)
            p = jnp.exp2(s - m_new)
            probs.append((m_new, alpha, p.astype(BF16)))
        if staged_next is not None:
            stage(staged_next)
        for h in range(N_HEADS):
            m_new, alpha, p = probs[h]
            r0 = vrow0 if kshared else h * HEAD_DIM
            lhs = jnp.concatenate([vt_all[r0:r0 + HEAD_DIM, :], ones], axis=0)
            acc_ref[h] = alpha * acc_ref[h] + _dot(lhs, p)
            m_ref[h] = m_new

    m_ref[...] = jnp.full(m_ref.shape, NEG, F32)
    acc_ref[...] = jnp.zeros(acc_ref.shape, F32)
    if window is None:
        first = 0
        stage(first)
    else:
        o_full = (window - T + 1) // T
        o_none = (window + T - 1) // T + 1
        assert o_none - o_full == 2
        o = o_full + 1
        j = i - o
        jc = jnp.maximum(j, 0)
        first = jnp.maximum(i - o_full, 0)
        s_band = scores(jc)
        stage(first)
        limit = jnp.where(j >= 0, window - o * T, -2 * T)
        absorb(jc, (qcol - krow) <= limit, s_all=s_band)

    def body(j, carry):
        absorb(j, None, staged_next=j + 1)
        return carry

    lax.fori_loop(first, i, body, 0)
    absorb(i, krow <= qcol)

    for pair in range(2):
        outs = []
        for e in range(2):
            acc = acc_ref[2 * pair + e]
            outs.append(acc[0:HEAD_DIM, :] * (1.0 / acc[HEAD_DIM:HEAD_DIM + 1, :]))
        o_ref[:, pair * LANES:(pair + 1) * LANES] = jnp.concatenate(outs, axis=0).T.astype(BF16)


def _flash_t(qt, k, kcol, kw, vt, *, B, S, window=None, vrow0=0, aux=None, name):
    kshared = kw == LANES
    in_specs = [pl.BlockSpec((None, None, BLK, TB), lambda b, i: (b, i, 0, 0))]
    args = [qt]
    if aux is not None:
        kind, qx, kx, kxcol = aux
        in_specs.append(pl.BlockSpec((None, None, LANES, TB), lambda b, i: (b, i, 0, 0)))
        args.append(qx)
    in_specs.append(pl.BlockSpec((None, S, kw), lambda b, i: (b, 0, kcol)))
    args.append(k)
    if aux is not None:
        if kx.ndim == 3:
            in_specs.append(pl.BlockSpec((None, S, LANES), lambda b, i: (b, 0, kxcol)))
        else:
            in_specs.append(pl.BlockSpec((S, LANES), lambda b, i: (0, kxcol)))
        args.append(kx)
    in_specs.append(pl.BlockSpec((None,) + vt.shape[1:], lambda b, i: (b, 0, 0, 0)))
    args.append(vt)
    return pl.pallas_call(
        functools.partial(_flash_t_kernel, window=window, kshared=kshared, vrow0=vrow0,
                          aux=None if aux is None else aux[0]),
        grid=(B, S // TB),
        in_specs=in_specs,
        out_specs=pl.BlockSpec((None, TB, BLK), lambda b, i: (b, i, 0)),
        out_shape=jax.ShapeDtypeStruct((B, S, BLK), BF16),
        scratch_shapes=[pltpu.VMEM((N_HEADS, TB, TB), F32),
                        pltpu.VMEM((N_HEADS, 1, TB), F32),
                        pltpu.VMEM((N_HEADS, HEAD_DIM + ONES_ROWS, TB), F32)],
        compiler_params=pltpu.CompilerParams(dimension_semantics=("arbitrary", "arbitrary"),
                                             vmem_limit_bytes=VMEM_LIMIT),
        name=name,
    )(*args)


def _mla_up_kernel(cq_ref, ckv_ref, gq_ref, gkv_ref, wq_ref, wk_ref, wvt_ref, cos_ref, sin_ref,
                   qn_ref, qr_ref, kn_ref, vt_ref):
    def norm(t, g):
        t = t.astype(F32)
        ms = jnp.mean(t * t, axis=-1, keepdims=True)
        return ((t * lax.rsqrt(ms + EPS)) * g).astype(BF16)

    scale = LOG2E * (MLA_NOPE + MLA_ROPE) ** -0.5
    q = _dot(norm(cq_ref[...], gq_ref[...]), wq_ref[...])
    qn_t = (q[:, :BLK] * scale).T
    c32, s32 = _token_major_rope(cos_ref, sin_ref, MLA_ROPE)
    qr_t = (_rope_half(q[:, BLK:], c32, s32, MLA_ROPE) * scale).T
    ckv = norm(ckv_ref[...], gkv_ref[...])
    kn_ref[...] = _dot(ckv, wk_ref[...]).astype(BF16)
    vt = _dot_nt(wvt_ref[...], ckv)
    for blk in range(vt.shape[1] // TB):
        cols = slice(blk * TB, (blk + 1) * TB)
        qn_ref[blk] = qn_t[:, cols].astype(BF16)
        qr_ref[blk] = qr_t[:, cols].astype(BF16)
        vt_ref[blk] = vt[:, cols].astype(BF16)


def _mla_up(proj, g_q, g_kv, w_uq, w_ukv, tabs, B, S):
    rows = proj.shape[0]
    tm = 1024
    per_b = S // tm
    wq = w_uq.reshape(MLA_Q_RANK, N_HEADS, MLA_NOPE + MLA_ROPE)
    wq = jnp.concatenate([wq[:, :, :MLA_NOPE].reshape(MLA_Q_RANK, -1),
                          wq[:, :, MLA_NOPE:].reshape(MLA_Q_RANK, -1)], axis=1).astype(BF16)
    wkv = w_ukv.reshape(MLA_KV_RANK, N_HEADS, MLA_NOPE + MLA_V)
    wk = wkv[:, :, :MLA_NOPE].reshape(MLA_KV_RANK, -1).astype(BF16)
    wvt = wkv[:, :, MLA_NOPE:].reshape(MLA_KV_RANK, -1).T.astype(BF16)
    row_spec = lambda w, c: pl.BlockSpec((tm, w), lambda i: (i, c))
    full = lambda a: pl.BlockSpec(a.shape, lambda i: (0, 0))
    gq = g_q.reshape(1, -1)
    gkv = g_kv.reshape(1, -1)
    fm_shape = lambda r: jax.ShapeDtypeStruct((B, S // TB, r, TB), BF16)
    fm_spec = lambda r: pl.BlockSpec((None, tm // TB, r, TB),
                                     lambda i: (i // per_b, i % per_b, 0, 0))
    outs = [fm_shape(BLK), fm_shape(LANES), jax.ShapeDtypeStruct((rows, BLK), BF16), fm_shape(BLK)]
    return pl.pallas_call(
        _mla_up_kernel,
        grid=(rows // tm,),
        in_specs=[row_spec(BLK, PB_BCQ), row_spec(LANES, 2 * PB_BKV), full(gq), full(gkv),
                  full(wq), full(wk), full(wvt)]
                 + [pl.BlockSpec((tabs[0].shape[0], tm), lambda i: (0, i))] * 2,
        out_specs=[fm_spec(BLK), fm_spec(LANES), row_spec(BLK, 0), fm_spec(BLK)],
        out_shape=outs,
        compiler_params=pltpu.CompilerParams(dimension_semantics=("arbitrary",),
                                             vmem_limit_bytes=VMEM_LIMIT),
        name="mla_up",
    )(proj, proj, gq, gkv, wq, wk, wvt, *tabs)


def _compress_kernel(x_ref, perm_ref, pos_ref, w1_ref, w2_ref, kc_ref, vc_ref):
    n_tiles = x_ref.shape[0] // TILE
    per = TILE // NSA_CMP_STRIDE
    slabs = []
    for t in range(n_tiles):
        cm = _dot(perm_ref[...], x_ref[TILE * t:TILE * (t + 1), :])
        slabs.append(jnp.concatenate([cm[per * c:per * (c + 1), :]
                                      for c in range(NSA_CMP_STRIDE)], axis=1))
    x = jnp.concatenate(slabs, axis=0)
    half = x.shape[1]
    top = (x + pos_ref[0:1, :]).astype(BF16)
    bot = (x + pos_ref[1:2, :]).astype(BF16)
    y_top = _dot(top, w1_ref[0:half, :])
    y_bot = _dot(bot, w1_ref[half:2 * half, :])
    pre = y_top + pltpu.roll(y_bot, y_bot.shape[0] - 1, 0)
    hid = (pre * (1.0 / (1.0 + jnp.exp(-pre)))).astype(BF16)
    kc_ref[...] = _dot(hid[:, :NSA_CMP_HID], w2_ref[0]).astype(BF16)
    vc_ref[...] = _dot(hid[:, NSA_CMP_HID:], w2_ref[1]).astype(BF16)


def _compress(proj3, pos_k, pos_v, kw1, kw2, vw1, vw2, B, S):
    assert S % TILE == 0 and NSA_CMP_LEN == 2 * NSA_CMP_STRIDE
    n16 = S // NSA_CMP_STRIDE
    perm = jnp.asarray(_class_major_perm(NSA_CMP_STRIDE), BF16)
    pos = jnp.concatenate([pos_k, pos_v], axis=1).reshape(2, NSA_CMP_STRIDE * LANES)
    w1 = jnp.zeros((NSA_CMP_LEN, LANES, 2 * NSA_CMP_HID), F32)
    w1 = w1.at[:, :HEAD_DIM, :NSA_CMP_HID].set(kw1.reshape(NSA_CMP_LEN, HEAD_DIM, NSA_CMP_HID))
    w1 = w1.at[:, HEAD_DIM:, NSA_CMP_HID:].set(vw1.reshape(NSA_CMP_LEN, HEAD_DIM, NSA_CMP_HID))
    w1 = w1.reshape(NSA_CMP_LEN * LANES, 2 * NSA_CMP_HID).astype(BF16)
    w2 = jnp.stack([jnp.tile(kw2, (1, 2)), jnp.tile(vw2, (1, 2))]).astype(BF16)
    out = jax.ShapeDtypeStruct((B, n16, LANES), BF16)
    return pl.pallas_call(
        _compress_kernel,
        grid=(B,),
        in_specs=[pl.BlockSpec((None, S, LANES), lambda b: (b, 0, 2 * PB_C0)),
                  pl.BlockSpec(perm.shape, lambda b: (0, 0)),
                  pl.BlockSpec(pos.shape, lambda b: (0, 0)),
                  pl.BlockSpec(w1.shape, lambda b: (0, 0)),
                  pl.BlockSpec(w2.shape, lambda b: (0, 0, 0))],
        out_specs=[pl.BlockSpec((None, n16, LANES), lambda b: (b, 0, 0))] * 2,
        out_shape=[out, out],
        compiler_params=pltpu.CompilerParams(dimension_semantics=("arbitrary",),
                                             vmem_limit_bytes=VMEM_LIMIT),
        name="nsa_compress",
    )(proj3, perm, pos, w1, w2)


def _cmp_topk_kernel(q_ref, kc_ref, vc_ref, ovt_ref, o_ref, nsel_ref, *, T, n_cmp, n_slc):
    i = pl.program_id(1)
    t0 = i * T
    lane = lax.broadcasted_iota(jnp.int32, (T, LANES), 1)
    tok = t0 + lax.broadcasted_iota(jnp.int32, (T, LANES), 0)
    ok = jnp.logical_and(lane * NSA_CMP_STRIDE + (NSA_CMP_LEN - 1) <= tok, lane < n_cmp)
    lo = lane < HEAD_DIM
    qm = _masked_heads(q_ref, lo)
    kc = kc_ref[...]
    vc = vc_ref[...]
    psum = jnp.zeros((T, LANES), F32)
    outs = []
    for h in range(N_HEADS):
        s = jnp.where(ok, _dot_nt(qm[h], kc), NEG)
        mx = jnp.max(s, axis=-1, keepdims=True)
        e = jnp.where(ok, jnp.exp2(s - mx), 0.0)
        den = jnp.sum(e, axis=-1, keepdims=True)
        p = e * (1.0 / jnp.maximum(den, 1e-30))
        psum = psum + p
        outs.append(_dot(p.astype(BF16), vc))
    for pair in range(2):
        o_ref[:, pair * LANES:(pair + 1) * LANES] = jnp.where(
            lo, outs[2 * pair], outs[2 * pair + 1]).astype(BF16)

    p_hi = psum.astype(BF16)
    p_lo = (psum - p_hi.astype(F32)).astype(BF16)
    ovt = ovt_ref[...]
    imp = _dot_nt(ovt, p_hi) + _dot_nt(ovt, p_lo)
    jj = lax.broadcasted_iota(jnp.int32, (n_slc, T), 0)
    bt = (t0 + lax.broadcasted_iota(jnp.int32, (n_slc, T), 1)) >> 6
    forced = jnp.logical_or(jj == 0, jnp.logical_or(jj == bt, jj == bt - 1))
    val = jnp.where(forced, BIG, jnp.where(jj > bt, -BIG, imp))
    sel_rows = []
    for j in range(n_slc):
        vj = val[j:j + 1, :]
        beats = jnp.logical_or(val > vj, jnp.logical_and(val == vj, jj < j))
        rank = jnp.sum(beats.astype(F32), axis=0, keepdims=True)
        sel_rows.append((rank >= NSA_N_SEL).astype(F32))
    sel_rows.append(jnp.zeros((LANES - n_slc, T), F32))
    nsel = jnp.concatenate(sel_rows, axis=0).astype(BF16)
    for blk in range(T // TB):
        nsel_ref[blk] = nsel[:, blk * TB:(blk + 1) * TB]


def _cmp_topk(proj3, kc2, vc2, B, S):
    T = 2 * TB
    n_cmp = (S - NSA_CMP_LEN) // NSA_CMP_STRIDE + 1
    n_slc = S // NSA_SEL_LEN
    assert NSA_SEL_LEN == 64 and n_slc <= LANES and n_cmp <= LANES
    ci = np.arange(LANES)[None, :] * NSA_CMP_STRIDE
    sj = np.arange(n_slc)[:, None] * NSA_SEL_LEN
    ovt = ((ci < sj + NSA_SEL_LEN) & (ci + NSA_CMP_LEN > sj) & (np.arange(LANES)[None, :] < n_cmp))
    ovt = jnp.asarray(ovt.astype(np.float32), BF16)
    return pl.pallas_call(
        functools.partial(_cmp_topk_kernel, T=T, n_cmp=n_cmp, n_slc=n_slc),
        grid=(B, S // T),
        in_specs=[pl.BlockSpec((None, T, BLK), lambda b, i: (b, i, PB_CQ)),
                  pl.BlockSpec((None, LANES, LANES), lambda b, i: (b, 0, 0)),
                  pl.BlockSpec((None, LANES, LANES), lambda b, i: (b, 0, 0)),
                  pl.BlockSpec(ovt.shape, lambda b, i: (0, 0))],
        out_specs=[pl.BlockSpec((None, T, BLK), lambda b, i: (b, i, 0)),
                   pl.BlockSpec((None, T // TB, LANES, TB), lambda b, i: (b, i, 0, 0))],
        out_shape=[jax.ShapeDtypeStruct((B, S, BLK), BF16),
                   jax.ShapeDtypeStruct((B, S // TB, LANES, TB), BF16)],
        compiler_params=pltpu.CompilerParams(dimension_semantics=("arbitrary", "arbitrary"),
                                             vmem_limit_bytes=VMEM_LIMIT),
        name="nsa_cmp_topk",
    )(proj3, kc2, vc2, ovt)


def _stick_kernel(q_ref, k_ref, vt_ref, u_ref, o_ref, za_ref, zb_ref, w_ref, carry_ref, acc_ref):
    T = TB
    i = pl.program_id(1)
    qm = _masked_head_rows(q_ref)
    krow = lax.broadcasted_iota(jnp.int32, (T, T), 0)
    qcol = lax.broadcasted_iota(jnp.int32, (T, T), 1)
    strict = krow < qcol
    u = u_ref[...]

    z_refs = (za_ref, zb_ref)

    def stage(j, slot):
        rows = pl.ds(pl.multiple_of(j * T, T), T)
        for h in range(N_HEADS):
            z_refs[slot][h] = _dot(k_ref[rows, (h // 2) * LANES:(h // 2 + 1) * LANES], qm[h])

    def apply_weights(j):
        vt_all = vt_ref[j]
        for h in range(N_HEADS):
            acc_ref[h] = acc_ref[h] + _dot(vt_all[h * HEAD_DIM:(h + 1) * HEAD_DIM, :], w_ref[h])

    def absorb(j, slot, diag, staged_next, applied_prev):
        z_ref = z_refs[slot]
        softplus = []
        for h in range(N_HEADS):
            z = z_ref[h]
            sp = jnp.maximum(z, 0.0) + jnp.log2(1.0 + jnp.exp2(-jnp.abs(z)))
            if diag:
                sp = jnp.where(strict, sp, 0.0)
            softplus.append(sp.astype(BF16))
        if staged_next is not None:
            stage(staged_next, 1 - slot)
        if applied_prev is not None:
            apply_weights(applied_prev)
        sums = [_dot(u, sp) for sp in softplus]
        for h in range(N_HEADS):
            a = jnp.exp2(z_ref[h] - (sums[h] + carry_ref[h]))
            if diag:
                a = jnp.where(strict, a, 0.0)
            w_ref[h] = a.astype(BF16)
            carry_ref[h] = carry_ref[h] + sums[h][0:1, :]

    carry_ref[...] = jnp.zeros(carry_ref.shape, F32)
    acc_ref[...] = jnp.zeros(acc_ref.shape, F32)
    stage(i, 0)
    absorb(i, 0, True, jnp.maximum(i - 1, 0), None)

    def body(t, carry):
        j = i - 1 - 2 * t
        absorb(j, 1, False, j - 1, j + 1)
        absorb(j - 1, 0, False, jnp.maximum(j - 2, 0), j)
        return carry

    lax.fori_loop(0, i // 2, body, 0)

    @pl.when(i % 2 == 1)
    def _():
        absorb(0, 1, False, None, 1)

    apply_weights(0)
    for pair in range(2):
        o_t = jnp.concatenate([acc_ref[2 * pair], acc_ref[2 * pair + 1]], axis=0)
        o_ref[:, pair * LANES:(pair + 1) * LANES] = o_t.T.astype(BF16)


def _stick_breaking(qdt, proj3, vdt, B, S):
    u = np.triu(np.ones((TB, TB), np.float32))
    return pl.pallas_call(
        _stick_kernel,
        grid=(B, S // TB),
        in_specs=[pl.BlockSpec((None, None, BLK, TB), lambda b, i: (b, i, 0, 0)),
                  pl.BlockSpec((None, S, BLK), lambda b, i: (b, 0, PB_DK)),
                  pl.BlockSpec((None,) + vdt.shape[1:], lambda b, i: (b, 0, 0, 0)),
                  pl.BlockSpec(u.shape, lambda b, i: (0, 0))],
        out_specs=pl.BlockSpec((None, TB, BLK), lambda b, i: (b, i, 0)),
        out_shape=jax.ShapeDtypeStruct((B, S, BLK), BF16),
        scratch_shapes=[pltpu.VMEM((N_HEADS, TB, TB), F32),
                        pltpu.VMEM((N_HEADS, TB, TB), F32),
                        pltpu.VMEM((N_HEADS, TB, TB), BF16),
                        pltpu.VMEM((N_HEADS, 1, TB), F32),
                        pltpu.VMEM((N_HEADS, HEAD_DIM, TB), F32)],
        compiler_params=pltpu.CompilerParams(dimension_semantics=("arbitrary", "arbitrary"),
                                             vmem_limit_bytes=VMEM_LIMIT),
        name="stick_breaking",
    )(qdt, proj3, vdt, jnp.asarray(u, BF16))


def _outproj_kernel(x_ref, oa, ob, oc, os_, ow, od,
                    ga, gb, gl, gc, gd, expand_ref, w_ref, g_ref, out_ref):
    f = lambda r: r[...].astype(F32)
    gates = _dot(gl[...], expand_ref[...])
    o_c = (gates[:, 0:BLK] * f(oc) + gates[:, BLK:2 * BLK] * f(os_)
           + gates[:, 2 * BLK:3 * BLK] * f(ow))
    mixed = [f(oa) * f(ga), f(ob) * f(gb), o_c * f(gc), f(od) * f(gd)]
    y = _dot(mixed[0].astype(BF16), w_ref[0:BLK, :])
    for g in range(1, 4):
        y = y + _dot(mixed[g].astype(BF16), w_ref[g * BLK:(g + 1) * BLK, :])
    ms = jnp.mean(y * y, axis=-1, keepdims=True)
    out_ref[...] = x_ref[...] + (y * lax.rsqrt(ms + EPS)) * g_ref[...]


def _out_projection(x2, mixer_outs, proj, w_out, g_post):
    rows = x2.shape[0]
    tm = TILE
    blk = lambda c: pl.BlockSpec((tm, BLK), lambda i: (i, c))
    expand = np.zeros((LANES, N_BRANCH * BLK), np.float32)
    for h in range(N_HEADS):
        for r in range(N_BRANCH):
            expand[N_BRANCH * h + r, r * BLK + h * HEAD_DIM:r * BLK + (h + 1) * HEAD_DIM] = 1.0
    gate_specs = [blk(PB_AG), blk(PB_BG),
                  pl.BlockSpec((tm, LANES), lambda i: (i, 2 * PB_C1 + 1)), blk(PB_CG), blk(PB_DG)]
    return pl.pallas_call(
        _outproj_kernel,
        grid=(rows // tm,),
        in_specs=[pl.BlockSpec((tm, D_MODEL), lambda i: (i, 0))]
                 + [blk(0)] * len(mixer_outs) + gate_specs
                 + [pl.BlockSpec(expand.shape, lambda i: (0, 0)),
                    pl.BlockSpec((D_MODEL, D_MODEL), lambda i: (0, 0)),
                    pl.BlockSpec((1, D_MODEL), lambda i: (0, 0))],
        out_specs=pl.BlockSpec((tm, D_MODEL), lambda i: (i, 0)),
        out_shape=jax.ShapeDtypeStruct((rows, D_MODEL), F32),
        compiler_params=pltpu.CompilerParams(dimension_semantics=("arbitrary",),
                                             vmem_limit_bytes=VMEM_LIMIT),
        name="out_projection",
    )(x2, *mixer_outs, *([proj] * len(gate_specs)), jnp.asarray(expand, BF16),
      w_out.astype(BF16), g_post.reshape(1, D_MODEL))


def _layer(x2, tabs, B, S, layer, w_rows, w_t, w_out, g_pre, g_post, mla_g_q, mla_g_kv,
           mla_w_uq, mla_w_ukv, nsa_pos_k, nsa_pos_v, nsa_k_w1, nsa_k_w2, nsa_v_w1, nsa_v_w2):
    rows = B * S
    aqkv, aqkv4, aqkv16, proj, vct, vdt, qct, qdt = _in_projection(
        x2, g_pre, w_rows, w_t, layer, tabs, B, S)
    proj3 = proj.reshape(B, S, N_PBLK * BLK)

    o_a = _dilated(aqkv, aqkv4, aqkv16, B, S)

    qnt, qrt, kn, vbt = _mla_up(proj, mla_g_q, mla_g_kv, mla_w_uq, mla_w_ukv, tabs, B, S)
    o_b = _flash_t(qnt, kn.reshape(B, S, BLK), 0, BLK, vbt, B=B, S=S,
                   aux=("per_head", qrt, proj3, 2 * PB_BKV + 1), name="mla_attention")

    kc2, vc2 = _compress(proj3, nsa_pos_k, nsa_pos_v, nsa_k_w1, nsa_k_w2, nsa_v_w1, nsa_v_w2, B, S)
    o_cmp, nsel = _cmp_topk(proj3, kc2, vc2, B, S)
    penalty = np.zeros((S, LANES), np.float32)
    penalty[np.arange(S), np.arange(S) // NSA_SEL_LEN] = MASK_PENALTY
    o_slc = _flash_t(qct, proj3, 2 * PB_C0 + 1, LANES, vct, B=B, S=S, vrow0=0,
                     aux=("shared", nsel, jnp.asarray(penalty, BF16), 0), name="nsa_selected")
    o_win = _flash_t(qct, proj3, 2 * PB_C1, LANES, vct, B=B, S=S, vrow0=HEAD_DIM,
                     window=NSA_WINDOW - 1, name="nsa_window")

    o_d = _stick_breaking(qdt, proj3, vdt, B, S)

    flat = lambda t: t.reshape(rows, BLK)
    outs = [flat(o) for o in (o_a, o_b, o_cmp, o_slc, o_win, o_d)]
    return _out_projection(x2, outs, proj, w_out, g_post)


def kernel(x, positions, w_in, w_out, g_pre, g_post, mla_g_q, mla_g_kv, mla_w_uq, mla_w_ukv,
           nsa_pos_k, nsa_pos_v, nsa_k_w1, nsa_k_w2, nsa_v_w1, nsa_v_w2):
    B, S, D = x.shape
    assert D == D_MODEL and S % 1024 == 0
    x2 = x.reshape(B * S, D)
    tabs = _rope_tables(positions)
    w_rows, w_t = _pack_w_in(w_in)
    for l in range(w_in.shape[0]):
        x2 = _layer(x2, tabs, B, S, l, w_rows, w_t, w_out[l], g_pre[l], g_post[l],
                    mla_g_q[l], mla_g_kv[l], mla_w_uq[l], mla_w_ukv[l],
                    nsa_pos_k[l], nsa_pos_v[l], nsa_k_w1[l], nsa_k_w2[l],
                    nsa_v_w1[l], nsa_v_w2[l])
    return x2.reshape(B, S, D)
```

```python
import functools
import math

import numpy as np
import jax
import jax.numpy as jnp
from jax import lax
from jax.experimental import pallas as pl
from jax.experimental.pallas import tpu as pltpu

D_MODEL = 1024
HEAD_DIM = 64
N_HEADS = 4
ROPE_THETA = 10000.0
EPS = 1e-6
NEG = -1e30
LOG2E = math.log2(math.e)
MASK_PENALTY = -(2.0 ** 100)
BIG = 1e9

MLA_Q_RANK = 256
MLA_KV_RANK = 128
MLA_NOPE = 64
MLA_ROPE = 32
MLA_V = 64

NSA_CMP_LEN = 32
NSA_CMP_STRIDE = 16
NSA_CMP_HID = 256
NSA_SEL_LEN = 64
NSA_N_SEL = 16
NSA_WINDOW = 512

LANES = 128
BLK = 256
TILE = 512
DILATIONS = (1, 4, 16)
DIL_BLOCK = 128
LSE_PARTS = 2
TB = 256
ONES_ROWS = 16
VMEM_LIMIT = 48 * 1024 * 1024

F32 = jnp.float32
BF16 = jnp.bfloat16

(PB_AG, PB_BCQ, PB_BKV, PB_BG, PB_CQ, PB_C0, PB_C1, PB_CG, PB_DQ, PB_DK, PB_DG) = range(11)
N_PBLK = 11
N_BRANCH = 3
ROPE_NONE, ROPE_64, ROPE_32, ROPE_64_LO, ACT_SILU, ACT_SIGMOID = range(6)


def _dot_nt(a, b):
    return lax.dot_general(a, b, (((1,), (1,)), ((), ())), preferred_element_type=F32)


def _dot(a, b):
    return jnp.dot(a, b, preferred_element_type=F32)


def _rope_half(y, cos, sin_signed, group):
    half = group // 2
    lane = lax.broadcasted_iota(jnp.int32, y.shape, 1)
    first = (lane & (group - 1)) < half
    partner = jnp.where(first, pltpu.roll(y, LANES - half, 1), pltpu.roll(y, half, 1))
    return y * cos + partner * sin_signed


def _rope_table_kernel(pos_ref, inv_ref, cos_ref, sin_ref):
    ang = inv_ref[...] * pos_ref[...].astype(F32)
    cos_ref[...] = jnp.cos(ang)
    sin_ref[...] = jnp.sin(ang)


def _rope_tables(positions):
    rows = positions.size
    tm = 2048
    inv = np.concatenate([ROPE_THETA ** (-np.arange(n, dtype=np.float64) / n)
                          for n in (HEAD_DIM // 2, MLA_ROPE // 2)]).astype(np.float32)[:, None]
    out = jax.ShapeDtypeStruct((inv.shape[0], rows), F32)
    tab_spec = pl.BlockSpec((inv.shape[0], tm), lambda i: (0, i))
    return pl.pallas_call(
        _rope_table_kernel,
        grid=(rows // tm,),
        in_specs=[pl.BlockSpec((1, tm), lambda i: (0, i)),
                  pl.BlockSpec(inv.shape, lambda i: (0, 0))],
        out_specs=[tab_spec] * 2,
        out_shape=[out] * 2,
        name="rope_tables",
    )(positions.reshape(1, rows), jnp.asarray(inv))


def _token_major_rope(cos_ref, sin_ref, group):
    n = group // 2
    lo = 0 if group == HEAD_DIM else HEAD_DIM // 2
    c = cos_ref[lo:lo + n, :]
    s = sin_ref[lo:lo + n, :]
    reps = LANES // n
    cos = jnp.concatenate([c] * reps, axis=0).T
    sin = jnp.concatenate([-s if k % 2 == 0 else s for k in range(reps)], axis=0).T
    return cos, sin


def _inproj_kernel(x_ref, g_ref, w_ref, wt_ref, perm_ref, cos_ref, sin_ref,
                   aqkv_ref, aqkv4_ref, aqkv16_ref, proj_ref, vct_ref, vdt_ref, qct_ref, qdt_ref,
                   *, plan_a, plan_p):
    x = x_ref[...]
    tm = x.shape[0]
    ms = jnp.mean(x * x, axis=-1, keepdims=True)
    h = ((x * lax.rsqrt(ms + EPS)) * g_ref[...]).astype(BF16)
    lane = lax.broadcasted_iota(jnp.int32, (tm, LANES), 1)
    c64, s64 = _token_major_rope(cos_ref, sin_ref, HEAD_DIM)
    c32, s32 = _token_major_rope(cos_ref, sin_ref, MLA_ROPE)

    def epilogue(y, ops, scale):
        halves = []
        for hf, op in enumerate(ops):
            yh = y[:, hf * LANES:(hf + 1) * LANES]
            if op == ROPE_64:
                yh = _rope_half(yh, c64, s64, 64)
            elif op == ROPE_32:
                yh = _rope_half(yh, c32, s32, 32)
            elif op == ROPE_64_LO:
                yh = jnp.where(lane < 64, _rope_half(yh, c64, s64, 64), yh)
            elif op == ACT_SILU:
                yh = yh * (1.0 / (1.0 + jnp.exp(-yh)))
            elif op == ACT_SIGMOID:
                yh = 1.0 / (1.0 + jnp.exp(-yh))
            if scale != 1.0:
                yh = yh * scale
            halves.append(yh)
        return jnp.concatenate(halves, axis=1)

    feature_major = {PB_CQ: qct_ref, PB_DQ: qdt_ref}
    col = 0
    for out_ref, plan in ((aqkv_ref, plan_a), (proj_ref, plan_p)):
        for b, (ops, scale) in enumerate(plan):
            y = epilogue(_dot(h, w_ref[:, col:col + BLK]), ops, scale)
            out_ref[:, b * BLK:(b + 1) * BLK] = y.astype(BF16)
            if out_ref is proj_ref and b in feature_major:
                y_t = y.T
                for blk in range(tm // TB):
                    feature_major[b][blk] = y_t[:, blk * TB:(blk + 1) * TB].astype(BF16)
            col += BLK

    a_rows = aqkv_ref[...]
    aqkv4_ref[...] = _dot(perm_ref[0], a_rows).astype(BF16)
    aqkv16_ref[...] = _dot(perm_ref[1], a_rows).astype(BF16)

    vt = _dot_nt(wt_ref[...], h)
    for blk in range(tm // TB):
        cols = slice(blk * TB, (blk + 1) * TB)
        vct_ref[blk] = vt[0:LANES, cols].astype(BF16)
        vdt_ref[blk] = vt[LANES:LANES + BLK, cols].astype(BF16)


_PLAN_A = (((ROPE_64, ROPE_64), LOG2E * HEAD_DIM ** -0.5),
           ((ROPE_64, ROPE_64), 1.0),
           ((ROPE_NONE, ROPE_NONE), 1.0))
_PLAN_P = (
    ((ACT_SILU, ACT_SILU), 1.0),
    ((ROPE_NONE, ROPE_NONE), 1.0),
    ((ROPE_NONE, ROPE_32), 1.0),
    ((ACT_SILU, ACT_SILU), 1.0),
    ((ROPE_64, ROPE_64), LOG2E * HEAD_DIM ** -0.5),
    ((ROPE_64_LO, ROPE_64), 1.0),
    ((ROPE_64, ACT_SIGMOID), 1.0),
    ((ACT_SILU, ACT_SILU), 1.0),
    ((ROPE_NONE, ROPE_NONE), LOG2E * HEAD_DIM ** -0.5),
    ((ROPE_NONE, ROPE_NONE), 1.0),
    ((ACT_SILU, ACT_SILU), 1.0),
)
assert len(_PLAN_P) == N_PBLK


_W_IN_SEGMENTS = (("aq", 256), ("ak", 256), ("av", 256), ("ag", 256),
                  ("bcq", 256), ("bckv", 128), ("bkr", 32), ("bg", 256),
                  ("cq", 256), ("ckc", 64), ("cvc", 64), ("cks", 64), ("cvs", 64),
                  ("ckw", 64), ("cvw", 64), ("cgl", 12), ("cg", 256),
                  ("dq", 256), ("dk", 256), ("dv", 256), ("dg", 256))
W_ROWS_COLS = (3 + N_PBLK) * BLK
W_T_ROWS = LANES + BLK


def _pack_w_kernel(wt_ref, rows_ref, t_ref):
    wt = wt_ref[...].astype(F32)
    o = 0
    seg = {}
    for name, width in _W_IN_SEGMENTS:
        seg[name] = wt[o:o + width, :]
        o += width
    gl_pad = jnp.zeros((LANES - N_HEADS * N_BRANCH, wt.shape[1]), wt.dtype)
    rows = [seg["aq"], seg["ak"], seg["av"],
            seg["ag"], seg["bcq"],
            seg["bckv"], seg["bkr"], seg["bkr"], seg["bkr"], seg["bkr"],
            seg["bg"], seg["cq"],
            seg["ckc"], seg["cvc"], seg["cks"], seg["cks"],
            seg["ckw"], seg["ckw"], seg["cgl"], gl_pad,
            seg["cg"], seg["dq"], seg["dk"], seg["dg"]]
    rows_ref[...] = jnp.concatenate(rows, axis=0).T.astype(BF16)
    t_ref[...] = jnp.concatenate([seg["cvs"], seg["cvw"], seg["dv"]], axis=0).astype(BF16)


def _pack_w_in(w_in):
    depth, d_model, d_in = w_in.shape
    assert d_model == D_MODEL and d_in == sum(width for _, width in _W_IN_SEGMENTS)
    tc = 256
    return pl.pallas_call(
        _pack_w_kernel,
        grid=(depth, D_MODEL // tc),
        in_specs=[pl.BlockSpec((None, d_in, tc), lambda l, i: (l, 0, i))],
        out_specs=[pl.BlockSpec((None, tc, W_ROWS_COLS), lambda l, i: (l, i, 0)),
                   pl.BlockSpec((None, W_T_ROWS, tc), lambda l, i: (l, 0, i))],
        out_shape=[jax.ShapeDtypeStruct((depth, D_MODEL, W_ROWS_COLS), BF16),
                   jax.ShapeDtypeStruct((depth, W_T_ROWS, D_MODEL), BF16)],
        compiler_params=pltpu.CompilerParams(dimension_semantics=("arbitrary", "arbitrary"),
                                             vmem_limit_bytes=VMEM_LIMIT),
        name="pack_w_in",
    )(jnp.transpose(w_in, (0, 2, 1)).astype(BF16))


def _class_major_perm(dil):
    n = TILE // dil
    p = np.zeros((TILE, TILE), np.float32)
    pos = np.arange(TILE)
    p[(pos % dil) * n + pos // dil, pos] = 1.0
    return p


def _in_projection(x2, g_pre, w_rows, w_t, layer, tabs, B, S):
    rows = x2.shape[0]
    tm = TILE
    per_b = S // tm
    nkb = tm // TB
    perm = jnp.asarray(np.stack([_class_major_perm(d) for d in DILATIONS[1:]]), BF16)
    tab_spec = pl.BlockSpec((tabs[0].shape[0], tm), lambda i: (0, i))
    vt_spec = lambda r: pl.BlockSpec((None, nkb, r, TB), lambda i: (i // per_b, i % per_b, 0, 0))
    a_spec = pl.BlockSpec((tm, 3 * BLK), lambda i: (i, 0))
    a_shape = jax.ShapeDtypeStruct((rows, 3 * BLK), BF16)
    return pl.pallas_call(
        functools.partial(_inproj_kernel, plan_a=_PLAN_A, plan_p=_PLAN_P),
        grid=(rows // tm,),
        in_specs=[pl.BlockSpec((tm, D_MODEL), lambda i: (i, 0)),
                  pl.BlockSpec((1, D_MODEL), lambda i: (0, 0)),
                  pl.BlockSpec((None,) + w_rows.shape[1:], lambda i: (layer, 0, 0)),
                  pl.BlockSpec((None,) + w_t.shape[1:], lambda i: (layer, 0, 0)),
                  pl.BlockSpec(perm.shape, lambda i: (0, 0, 0))] + [tab_spec] * 2,
        out_specs=[a_spec, a_spec, a_spec,
                   pl.BlockSpec((tm, N_PBLK * BLK), lambda i: (i, 0)),
                   vt_spec(LANES), vt_spec(BLK), vt_spec(BLK), vt_spec(BLK)],
        out_shape=[a_shape, a_shape, a_shape,
                   jax.ShapeDtypeStruct((rows, N_PBLK * BLK), BF16),
                   jax.ShapeDtypeStruct((B, S // TB, LANES, TB), BF16)]
                  + [jax.ShapeDtypeStruct((B, S // TB, BLK, TB), BF16)] * 3,
        compiler_params=pltpu.CompilerParams(dimension_semantics=("arbitrary",),
                                             vmem_limit_bytes=VMEM_LIMIT),
        name="in_projection",
    )(x2, g_pre.reshape(1, D_MODEL), w_rows, w_t, perm, *tabs)


def _head_masks(shape):
    lane = lax.broadcasted_iota(jnp.int32, shape, 1)
    return lane < HEAD_DIM


def _masked_heads(q_ref, lo_mask):
    out = []
    for h in range(N_HEADS):
        pair, e = divmod(h, 2)
        qp = q_ref[:, pair * LANES:(pair + 1) * LANES]
        keep = lo_mask if e == 0 else jnp.logical_not(lo_mask)
        out.append(jnp.where(keep, qp, jnp.zeros_like(qp)))
    return out


def _dilated_kernel(an_ref, a4_ref, a16_ref, unperm_ref, expand_ref, o_ref, res_ref, s_ref):
    T = DIL_BLOCK
    S = an_ref.shape[0]
    n_tiles = S // TILE
    lo = _head_masks((T, LANES))
    lane = lax.broadcasted_iota(jnp.int32, (T, LANES), 1)
    dist2 = (lax.broadcasted_iota(jnp.int32, (T, 2 * T), 0)
             - lax.broadcasted_iota(jnp.int32, (T, 2 * T), 1))
    causal1 = (lax.broadcasted_iota(jnp.int32, (T, T), 1)
               <= lax.broadcasted_iota(jnp.int32, (T, T), 0))

    def scores(q, k):
        qm = _masked_heads(q, lo)
        return tuple(_dot_nt(qm[h], k[:, (h // 2) * LANES:(h // 2 + 1) * LANES])
                     for h in range(N_HEADS))

    def finish(n_keys, v, mask, stage_next):
        probs = []
        for h in range(N_HEADS):
            s = jnp.where(mask, s_ref[h, :, 0:n_keys], NEG)
            m = jnp.max(s, axis=-1, keepdims=True)
            p = jnp.exp2(s - m)
            probs.append((m, jnp.sum(p, axis=-1, keepdims=True), p.astype(BF16)))
        stage_next()
        outs = []
        top = jnp.zeros((T, LANES), F32)
        den = jnp.ones((T, LANES), F32)
        for h in range(N_HEADS):
            m, l, p = probs[h]
            outs.append(_dot(p, v[:, (h // 2) * LANES:(h // 2 + 1) * LANES]) * (1.0 / l))
            mine = (lane & (N_HEADS - 1)) == h
            top = jnp.where(mine, m, top)
            den = jnp.where(mine, l, den)
        lse = top + jnp.log2(den)
        hi = lse.astype(BF16).astype(F32)
        parts = jnp.where(lane < N_HEADS, hi,
                          jnp.where(lane < LSE_PARTS * N_HEADS, lse - hi, 0.0))
        return jnp.concatenate(
            [jnp.where(lo, outs[0], outs[1]), jnp.where(lo, outs[2], outs[3]), parts],
            axis=1).astype(BF16)

    def band_mask(delta):
        d = dist2 + delta
        return jnp.logical_and(d >= 0, d <= DIL_BLOCK)

    def run(n_units, n_keys, load_qk, load_v, store, mask_of):
        def stage(u):
            for h, s in enumerate(scores(*load_qk(u))):
                s_ref[h, :, 0:n_keys] = s

        def body(u, carry):
            nxt = jnp.minimum(u + 1, n_units - 1)
            store(u, finish(n_keys, load_v(u), mask_of(u), lambda: stage(nxt)))
            return carry

        stage(0)
        lax.fori_loop(0, n_units, body, 0)

    def rows(start, size):
        return pl.ds(pl.multiple_of(start, 32), size)

    def k0_1(u):
        return T * jnp.maximum(u - 1, 0)

    def store1(u, r):
        res_ref[0, rows(T * u, T), :] = r

    run(S // T, 2 * T,
        lambda u: (an_ref[rows(T * u, T), 0:BLK], an_ref[rows(k0_1(u), 2 * T), BLK:2 * BLK]),
        lambda u: an_ref[rows(k0_1(u), 2 * T), 2 * BLK:3 * BLK],
        store1,
        lambda u: band_mask(T * u - k0_1(u)))

    def piece4(u, blk):
        return rows(TILE * blk + T * (u // n_tiles), T)

    def kb0_4(u):
        return jnp.maximum(u % n_tiles - 1, 0)

    def load4(u, cols):
        return jnp.concatenate([a4_ref[piece4(u, kb0_4(u)), cols],
                                a4_ref[piece4(u, kb0_4(u) + 1), cols]], axis=0)

    def store4(u, r):
        res_ref[1, piece4(u, u % n_tiles), :] = r

    run(4 * n_tiles, 2 * T,
        lambda u: (a4_ref[piece4(u, u % n_tiles), 0:BLK], load4(u, slice(BLK, 2 * BLK))),
        lambda u: load4(u, slice(2 * BLK, 3 * BLK)),
        store4,
        lambda u: band_mask(T * (u % n_tiles - kb0_4(u))))

    per = TILE // 16

    def load16(c, cols):
        return jnp.concatenate([a16_ref[rows(TILE * t + per * c, per), cols]
                                for t in range(n_tiles)], axis=0)

    def store16(c, r):
        for t in range(n_tiles):
            res_ref[2, rows(TILE * t + per * c, per), :] = r[per * t:per * (t + 1), :]

    run(16, T,
        lambda c: (load16(c, slice(0, BLK)), load16(c, slice(BLK, 2 * BLK))),
        lambda c: load16(c, slice(2 * BLK, 3 * BLK)),
        store16,
        lambda c: causal1)

    expand = expand_ref[...]
    for t in range(n_tiles):
        tile = slice(TILE * t, TILE * (t + 1))
        nat = [res_ref[0, tile, :].astype(F32),
               _dot(unperm_ref[0], res_ref[1, tile, :]),
               _dot(unperm_ref[1], res_ref[2, tile, :])]
        lse = [_dot(r[:, BLK:].astype(BF16), expand) for r in nat]
        top = jnp.maximum(lse[0], jnp.maximum(lse[1], lse[2]))
        w = [jnp.exp2(l - top) for l in lse]
        mixed = (w[0] * nat[0][:, :BLK] + w[1] * nat[1][:, :BLK] + w[2] * nat[2][:, :BLK])
        o_ref[tile, :] = (mixed / (w[0] + w[1] + w[2])).astype(BF16)


def _dilated(aqkv, aqkv4, aqkv16, B, S):
    assert S == 16 * DIL_BLOCK and S % TILE == 0 and TILE == 4 * DIL_BLOCK
    r3 = lambda t: t.reshape(B, S, 3 * BLK)
    unperm = jnp.asarray(np.stack([_class_major_perm(d).T for d in DILATIONS[1:]]), BF16)
    expand = np.zeros((LANES, BLK), np.float32)
    for h in range(N_HEADS):
        for part in range(LSE_PARTS):
            expand[N_HEADS * part + h, h * HEAD_DIM:(h + 1) * HEAD_DIM] = 1.0
    a_spec = pl.BlockSpec((None, S, 3 * BLK), lambda b: (b, 0, 0))
    return pl.pallas_call(
        _dilated_kernel,
        grid=(B,),
        in_specs=[a_spec, a_spec, a_spec,
                  pl.BlockSpec(unperm.shape, lambda b: (0, 0, 0)),
                  pl.BlockSpec(expand.shape, lambda b: (0, 0))],
        out_specs=pl.BlockSpec((None, S, BLK), lambda b: (b, 0, 0)),
        out_shape=jax.ShapeDtypeStruct((B, S, BLK), BF16),
        scratch_shapes=[pltpu.VMEM((3, S, BLK + LANES), BF16),
                        pltpu.VMEM((N_HEADS, DIL_BLOCK, 2 * DIL_BLOCK), F32)],
        compiler_params=pltpu.CompilerParams(dimension_semantics=("arbitrary",),
                                             vmem_limit_bytes=VMEM_LIMIT),
        name="dilated_mixture",
    )(r3(aqkv), r3(aqkv4), r3(aqkv16), unperm, jnp.asarray(expand, BF16))


def _masked_head_rows(qt_ref):
    out = []
    zeros = jnp.zeros((HEAD_DIM, qt_ref.shape[-1]), BF16)
    for h in range(N_HEADS):
        own = qt_ref[h * HEAD_DIM:(h + 1) * HEAD_DIM, :]
        out.append(jnp.concatenate([own, zeros] if h % 2 == 0 else [zeros, own], axis=0))
    return out


def _flash_t_kernel(*refs, window, kshared, vrow0, aux):
    it = iter(refs)
    q_ref = next(it)
    qx_ref = next(it) if aux else None
    k_ref = next(it)
    kx_ref = next(it) if aux else None
    vt_ref = next(it)
    o_ref = next(it)
    s_ref, m_ref, acc_ref = next(it), next(it), next(it)
    T = TB

    i = pl.program_id(1)
    qcat = _masked_head_rows(q_ref)
    if aux == "per_head":
        group = LANES // N_HEADS
        qcat = [jnp.concatenate(
            [qcat[h]] + [qx_ref[g * group:(g + 1) * group, :] if g == h
                         else jnp.zeros((group, T), BF16) for g in range(N_HEADS)], axis=0)
            for h in range(N_HEADS)]
    elif aux == "shared":
        qx = qx_ref[...]
        qcat = [jnp.concatenate([qcat[h], qx], axis=0) for h in range(N_HEADS)]
    krow = lax.broadcasted_iota(jnp.int32, (T, T), 0)
    qcol = lax.broadcasted_iota(jnp.int32, (T, T), 1)
    ones = jnp.ones((ONES_ROWS, T), BF16)

    def scores(j):
        rows = pl.ds(pl.multiple_of(j * T, T), T)
        out = []
        for h in range(N_HEADS):
            pair = h // 2
            cols = slice(0, LANES) if kshared else slice(pair * LANES, (pair + 1) * LANES)
            kb = k_ref[rows, cols]
            if aux:
                kb = jnp.concatenate([kb, kx_ref[rows, :]], axis=1)
            out.append(_dot(kb, qcat[h]))
        return out

    def stage(j):
        for h, s in enumerate(scores(j)):
            s_ref[h] = s

    def absorb(j, mask, staged_next=None, s_all=None):
        vt_all = vt_ref[j]
        probs = []
        for h in range(N_HEADS):
            s = s_ref[h] if s_all is None else s_all[h]
            if mask is not None:
                s = jnp.where(mask, s, NEG)
            m = m_ref[h]
            m_new = jnp.maximum(m, jnp.max(s, axis=0, keepdims=True))
            alpha = jnp.exp2(m - m_new)
            p = jnp.exp2(s - m_new)
            probs.append((m_new, alpha, p.astype(BF16)))
        if staged_next is not None:
            stage(staged_next)
        for h in range(N_HEADS):
            m_new, alpha, p = probs[h]
            r0 = vrow0 if kshared else h * HEAD_DIM
            lhs = jnp.concatenate([vt_all[r0:r0 + HEAD_DIM, :], ones], axis=0)
            acc_ref[h] = alpha * acc_ref[h] + _dot(lhs, p)
            m_ref[h] = m_new

    m_ref[...] = jnp.full(m_ref.shape, NEG, F32)
    acc_ref[...] = jnp.zeros(acc_ref.shape, F32)
    if window is None:
        first = 0
        stage(first)
    else:
        o_full = (window - T + 1) // T
        o_none = (window + T - 1) // T + 1
        assert o_none - o_full == 2
        o = o_full + 1
        j = i - o
        jc = jnp.maximum(j, 0)
        first = jnp.maximum(i - o_full, 0)
        s_band = scores(jc)
        stage(first)
        limit = jnp.where(j >= 0, window - o * T, -2 * T)
        absorb(jc, (qcol - krow) <= limit, s_all=s_band)

    def body(j, carry):
        absorb(j, None, staged_next=j + 1)
        return carry

    lax.fori_loop(first, i, body, 0)
    absorb(i, krow <= qcol)

    for pair in range(2):
        outs = []
        for e in range(2):
            acc = acc_ref[2 * pair + e]
            outs.append(acc[0:HEAD_DIM, :] * (1.0 / acc[HEAD_DIM:HEAD_DIM + 1, :]))
        o_ref[:, pair * LANES:(pair + 1) * LANES] = jnp.concatenate(outs, axis=0).T.astype(BF16)


def _flash_t(qt, k, kcol, kw, vt, *, B, S, window=None, vrow0=0, aux=None, name):
    kshared = kw == LANES
    in_specs = [pl.BlockSpec((None, None, BLK, TB), lambda b, i: (b, i, 0, 0))]
    args = [qt]
    if aux is not None:
        kind, qx, kx, kxcol = aux
        in_specs.append(pl.BlockSpec((None, None, LANES, TB), lambda b, i: (b, i, 0, 0)))
        args.append(qx)
    in_specs.append(pl.BlockSpec((None, S, kw), lambda b, i: (b, 0, kcol)))
    args.append(k)
    if aux is not None:
        if kx.ndim == 3:
            in_specs.append(pl.BlockSpec((None, S, LANES), lambda b, i: (b, 0, kxcol)))
        else:
            in_specs.append(pl.BlockSpec((S, LANES), lambda b, i: (0, kxcol)))
        args.append(kx)
    in_specs.append(pl.BlockSpec((None,) + vt.shape[1:], lambda b, i: (b, 0, 0, 0)))
    args.append(vt)
    return pl.pallas_call(
        functools.partial(_flash_t_kernel, window=window, kshared=kshared, vrow0=vrow0,
                          aux=None if aux is None else aux[0]),
        grid=(B, S // TB),
        in_specs=in_specs,
        out_specs=pl.BlockSpec((None, TB, BLK), lambda b, i: (b, i, 0)),
        out_shape=jax.ShapeDtypeStruct((B, S, BLK), BF16),
        scratch_shapes=[pltpu.VMEM((N_HEADS, TB, TB), F32),
                        pltpu.VMEM((N_HEADS, 1, TB), F32),
                        pltpu.VMEM((N_HEADS, HEAD_DIM + ONES_ROWS, TB), F32)],
        compiler_params=pltpu.CompilerParams(dimension_semantics=("arbitrary", "arbitrary"),
                                             vmem_limit_bytes=VMEM_LIMIT),
        name=name,
    )(*args)


def _mla_up_kernel(cq_ref, ckv_ref, gq_ref, gkv_ref, wq_ref, wk_ref, wvt_ref, cos_ref, sin_ref,
                   qn_ref, qr_ref, kn_ref, vt_ref):
    def norm(t, g):
        t = t.astype(F32)
        ms = jnp.mean(t * t, axis=-1, keepdims=True)
        return ((t * lax.rsqrt(ms + EPS)) * g).astype(BF16)

    scale = LOG2E * (MLA_NOPE + MLA_ROPE) ** -0.5
    q_t = _dot_nt(wq_ref[...], norm(cq_ref[...], gq_ref[...])) * scale
    qn_t = q_t[:BLK, :]
    half = MLA_ROPE // 2
    cos = cos_ref[HEAD_DIM // 2:HEAD_DIM // 2 + half, :]
    sin = sin_ref[HEAD_DIM // 2:HEAD_DIM // 2 + half, :]
    rot = []
    for h in range(N_HEADS):
        x1 = q_t[BLK + h * MLA_ROPE:BLK + h * MLA_ROPE + half, :]
        x2 = q_t[BLK + h * MLA_ROPE + half:BLK + (h + 1) * MLA_ROPE, :]
        rot += [x1 * cos - x2 * sin, x1 * sin + x2 * cos]
    qr_t = jnp.concatenate(rot, axis=0)
    ckv = norm(ckv_ref[...], gkv_ref[...])
    kn_ref[...] = _dot(ckv, wk_ref[...]).astype(BF16)
    vt = _dot_nt(wvt_ref[...], ckv)
    for blk in range(vt.shape[1] // TB):
        cols = slice(blk * TB, (blk + 1) * TB)
        qn_ref[blk] = qn_t[:, cols].astype(BF16)
        qr_ref[blk] = qr_t[:, cols].astype(BF16)
        vt_ref[blk] = vt[:, cols].astype(BF16)


def _mla_up(proj, g_q, g_kv, w_uq, w_ukv, tabs, B, S):
    rows = proj.shape[0]
    tm = 1024
    per_b = S // tm
    wq = w_uq.reshape(MLA_Q_RANK, N_HEADS, MLA_NOPE + MLA_ROPE)
    wq = jnp.concatenate([wq[:, :, :MLA_NOPE].reshape(MLA_Q_RANK, -1),
                          wq[:, :, MLA_NOPE:].reshape(MLA_Q_RANK, -1)], axis=1).T.astype(BF16)
    wkv = w_ukv.reshape(MLA_KV_RANK, N_HEADS, MLA_NOPE + MLA_V)
    wk = wkv[:, :, :MLA_NOPE].reshape(MLA_KV_RANK, -1).astype(BF16)
    wvt = wkv[:, :, MLA_NOPE:].reshape(MLA_KV_RANK, -1).T.astype(BF16)
    row_spec = lambda w, c: pl.BlockSpec((tm, w), lambda i: (i, c))
    full = lambda a: pl.BlockSpec(a.shape, lambda i: (0, 0))
    gq = g_q.reshape(1, -1)
    gkv = g_kv.reshape(1, -1)
    fm_shape = lambda r: jax.ShapeDtypeStruct((B, S // TB, r, TB), BF16)
    fm_spec = lambda r: pl.BlockSpec((None, tm // TB, r, TB),
                                     lambda i: (i // per_b, i % per_b, 0, 0))
    outs = [fm_shape(BLK), fm_shape(LANES), jax.ShapeDtypeStruct((rows, BLK), BF16), fm_shape(BLK)]
    return pl.pallas_call(
        _mla_up_kernel,
        grid=(rows // tm,),
        in_specs=[row_spec(BLK, PB_BCQ), row_spec(LANES, 2 * PB_BKV), full(gq), full(gkv),
                  full(wq), full(wk), full(wvt)]
                 + [pl.BlockSpec((tabs[0].shape[0], tm), lambda i: (0, i))] * 2,
        out_specs=[fm_spec(BLK), fm_spec(LANES), row_spec(BLK, 0), fm_spec(BLK)],
        out_shape=outs,
        compiler_params=pltpu.CompilerParams(dimension_semantics=("arbitrary",),
                                             vmem_limit_bytes=VMEM_LIMIT),
        name="mla_up",
    )(proj, proj, gq, gkv, wq, wk, wvt, *tabs)


def _compress_kernel(x_ref, perm_ref, pos_ref, w1_ref, w2_ref, kc_ref, vc_ref):
    n_tiles = x_ref.shape[0] // TILE
    per = TILE // NSA_CMP_STRIDE
    slabs = []
    for t in range(n_tiles):
        cm = _dot(perm_ref[...], x_ref[TILE * t:TILE * (t + 1), :])
        slabs.append(jnp.concatenate([cm[per * c:per * (c + 1), :]
                                      for c in range(NSA_CMP_STRIDE)], axis=1))
    x = jnp.concatenate(slabs, axis=0)
    half = x.shape[1]
    top = (x + pos_ref[0:1, :]).astype(BF16)
    bot = (x + pos_ref[1:2, :]).astype(BF16)
    y_top = _dot(top, w1_ref[0:half, :])
    y_bot = _dot(bot, w1_ref[half:2 * half, :])
    pre = y_top + pltpu.roll(y_bot, y_bot.shape[0] - 1, 0)
    hid = (pre * (1.0 / (1.0 + jnp.exp(-pre)))).astype(BF16)
    kc_ref[...] = _dot(hid[:, :NSA_CMP_HID], w2_ref[0]).astype(BF16)
    vc_ref[...] = _dot(hid[:, NSA_CMP_HID:], w2_ref[1]).astype(BF16)


def _compress(proj3, pos_k, pos_v, kw1, kw2, vw1, vw2, B, S):
    assert S % TILE == 0 and NSA_CMP_LEN == 2 * NSA_CMP_STRIDE
    n16 = S // NSA_CMP_STRIDE
    perm = jnp.asarray(_class_major_perm(NSA_CMP_STRIDE), BF16)
    pos = jnp.concatenate([pos_k, pos_v], axis=1).reshape(2, NSA_CMP_STRIDE * LANES)
    w1 = jnp.zeros((NSA_CMP_LEN, LANES, 2 * NSA_CMP_HID), F32)
    w1 = w1.at[:, :HEAD_DIM, :NSA_CMP_HID].set(kw1.reshape(NSA_CMP_LEN, HEAD_DIM, NSA_CMP_HID))
    w1 = w1.at[:, HEAD_DIM:, NSA_CMP_HID:].set(vw1.reshape(NSA_CMP_LEN, HEAD_DIM, NSA_CMP_HID))
    w1 = w1.reshape(NSA_CMP_LEN * LANES, 2 * NSA_CMP_HID).astype(BF16)
    w2 = jnp.stack([jnp.tile(kw2, (1, 2)), jnp.tile(vw2, (1, 2))]).astype(BF16)
    out = jax.ShapeDtypeStruct((B, n16, LANES), BF16)
    return pl.pallas_call(
        _compress_kernel,
        grid=(B,),
        in_specs=[pl.BlockSpec((None, S, LANES), lambda b: (b, 0, 2 * PB_C0)),
                  pl.BlockSpec(perm.shape, lambda b: (0, 0)),
                  pl.BlockSpec(pos.shape, lambda b: (0, 0)),
                  pl.BlockSpec(w1.shape, lambda b: (0, 0)),
                  pl.BlockSpec(w2.shape, lambda b: (0, 0, 0))],
        out_specs=[pl.BlockSpec((None, n16, LANES), lambda b: (b, 0, 0))] * 2,
        out_shape=[out, out],
        compiler_params=pltpu.CompilerParams(dimension_semantics=("arbitrary",),
                                             vmem_limit_bytes=VMEM_LIMIT),
        name="nsa_compress",
    )(proj3, perm, pos, w1, w2)


def _cmp_topk_kernel(q_ref, kc_ref, vc_ref, ovt_ref, o_ref, nsel_ref, *, T, n_cmp, n_slc):
    i = pl.program_id(1)
    t0 = i * T
    lane = lax.broadcasted_iota(jnp.int32, (T, LANES), 1)
    tok = t0 + lax.broadcasted_iota(jnp.int32, (T, LANES), 0)
    ok = jnp.logical_and(lane * NSA_CMP_STRIDE + (NSA_CMP_LEN - 1) <= tok, lane < n_cmp)
    lo = lane < HEAD_DIM
    qm = _masked_heads(q_ref, lo)
    kc = kc_ref[...]
    vc = vc_ref[...]
    psum = jnp.zeros((T, LANES), F32)
    outs = []
    for h in range(N_HEADS):
        s = jnp.where(ok, _dot_nt(qm[h], kc), NEG)
        mx = jnp.max(s, axis=-1, keepdims=True)
        e = jnp.where(ok, jnp.exp2(s - mx), 0.0)
        den = jnp.sum(e, axis=-1, keepdims=True)
        p = e * (1.0 / jnp.maximum(den, 1e-30))
        psum = psum + p
        outs.append(_dot(p.astype(BF16), vc))
    for pair in range(2):
        o_ref[:, pair * LANES:(pair + 1) * LANES] = jnp.where(
            lo, outs[2 * pair], outs[2 * pair + 1]).astype(BF16)

    p_hi = psum.astype(BF16)
    p_lo = (psum - p_hi.astype(F32)).astype(BF16)
    ovt = ovt_ref[...]
    imp = _dot_nt(ovt, p_hi) + _dot_nt(ovt, p_lo)
    jj = lax.broadcasted_iota(jnp.int32, (n_slc, T), 0)
    bt = (t0 + lax.broadcasted_iota(jnp.int32, (n_slc, T), 1)) >> 6
    forced = jnp.logical_or(jj == 0, jnp.logical_or(jj == bt, jj == bt - 1))
    val = jnp.where(forced, BIG, jnp.where(jj > bt, -BIG, imp))
    sel_rows = []
    for j in range(n_slc):
        vj = val[j:j + 1, :]
        beats = jnp.logical_or(val > vj, jnp.logical_and(val == vj, jj < j))
        rank = jnp.sum(beats.astype(F32), axis=0, keepdims=True)
        sel_rows.append((rank >= NSA_N_SEL).astype(F32))
    sel_rows.append(jnp.zeros((LANES - n_slc, T), F32))
    nsel = jnp.concatenate(sel_rows, axis=0).astype(BF16)
    for blk in range(T // TB):
        nsel_ref[blk] = nsel[:, blk * TB:(blk + 1) * TB]


def _cmp_topk(proj3, kc2, vc2, B, S):
    T = 2 * TB
    n_cmp = (S - NSA_CMP_LEN) // NSA_CMP_STRIDE + 1
    n_slc = S // NSA_SEL_LEN
    assert NSA_SEL_LEN == 64 and n_slc <= LANES and n_cmp <= LANES
    ci = np.arange(LANES)[None, :] * NSA_CMP_STRIDE
    sj = np.arange(n_slc)[:, None] * NSA_SEL_LEN
    ovt = ((ci < sj + NSA_SEL_LEN) & (ci + NSA_CMP_LEN > sj) & (np.arange(LANES)[None, :] < n_cmp))
    ovt = jnp.asarray(ovt.astype(np.float32), BF16)
    return pl.pallas_call(
        functools.partial(_cmp_topk_kernel, T=T, n_cmp=n_cmp, n_slc=n_slc),
        grid=(B, S // T),
        in_specs=[pl.BlockSpec((None, T, BLK), lambda b, i: (b, i, PB_CQ)),
                  pl.BlockSpec((None, LANES, LANES), lambda b, i: (b, 0, 0)),
                  pl.BlockSpec((None, LANES, LANES), lambda b, i: (b, 0, 0)),
                  pl.BlockSpec(ovt.shape, lambda b, i: (0, 0))],
        out_specs=[pl.BlockSpec((None, T, BLK), lambda b, i: (b, i, 0)),
                   pl.BlockSpec((None, T // TB, LANES, TB), lambda b, i: (b, i, 0, 0))],
        out_shape=[jax.ShapeDtypeStruct((B, S, BLK), BF16),
                   jax.ShapeDtypeStruct((B, S // TB, LANES, TB), BF16)],
        compiler_params=pltpu.CompilerParams(dimension_semantics=("arbitrary", "arbitrary"),
                                             vmem_limit_bytes=VMEM_LIMIT),
        name="nsa_cmp_topk",
    )(proj3, kc2, vc2, ovt)


def _stick_kernel(q_ref, k_ref, vt_ref, u_ref, o_ref, za_ref, zb_ref, w_ref, carry_ref, acc_ref):
    T = TB
    i = pl.program_id(1)
    qm = _masked_head_rows(q_ref)
    krow = lax.broadcasted_iota(jnp.int32, (T, T), 0)
    qcol = lax.broadcasted_iota(jnp.int32, (T, T), 1)
    strict = krow < qcol
    u = u_ref[...]

    z_refs = (za_ref, zb_ref)

    def stage(j, slot):
        rows = pl.ds(pl.multiple_of(j * T, T), T)
        for h in range(N_HEADS):
            z_refs[slot][h] = _dot(k_ref[rows, (h // 2) * LANES:(h // 2 + 1) * LANES], qm[h])

    def apply_weights(j):
        vt_all = vt_ref[j]
        for h in range(N_HEADS):
            acc_ref[h] = acc_ref[h] + _dot(vt_all[h * HEAD_DIM:(h + 1) * HEAD_DIM, :], w_ref[h])

    def absorb(j, slot, diag, staged_next, applied_prev):
        z_ref = z_refs[slot]
        softplus = []
        for h in range(N_HEADS):
            z = z_ref[h]
            sp = jnp.maximum(z, 0.0) + jnp.log2(1.0 + jnp.exp2(-jnp.abs(z)))
            if diag:
                sp = jnp.where(strict, sp, 0.0)
            softplus.append(sp.astype(BF16))
        if staged_next is not None:
            stage(staged_next, 1 - slot)
        if applied_prev is not None:
            apply_weights(applied_prev)
        sums = [_dot(u, sp) for sp in softplus]
        for h in range(N_HEADS):
            a = jnp.exp2(z_ref[h] - (sums[h] + carry_ref[h]))
            if diag:
                a = jnp.where(strict, a, 0.0)
            w_ref[h] = a.astype(BF16)
            carry_ref[h] = carry_ref[h] + sums[h][0:1, :]

    carry_ref[...] = jnp.zeros(carry_ref.shape, F32)
    acc_ref[...] = jnp.zeros(acc_ref.shape, F32)
    stage(i, 0)
    absorb(i, 0, True, jnp.maximum(i - 1, 0), None)

    def body(t, carry):
        j = i - 1 - 2 * t
        absorb(j, 1, False, j - 1, j + 1)
        absorb(j - 1, 0, False, jnp.maximum(j - 2, 0), j)
        return carry

    lax.fori_loop(0, i // 2, body, 0)

    @pl.when(i % 2 == 1)
    def _():
        absorb(0, 1, False, None, 1)

    apply_weights(0)
    for pair in range(2):
        o_t = jnp.concatenate([acc_ref[2 * pair], acc_ref[2 * pair + 1]], axis=0)
        o_ref[:, pair * LANES:(pair + 1) * LANES] = o_t.T.astype(BF16)


def _stick_breaking(qdt, proj3, vdt, B, S):
    u = np.triu(np.ones((TB, TB), np.float32))
    return pl.pallas_call(
        _stick_kernel,
        grid=(B, S // TB),
        in_specs=[pl.BlockSpec((None, None, BLK, TB), lambda b, i: (b, i, 0, 0)),
                  pl.BlockSpec((None, S, BLK), lambda b, i: (b, 0, PB_DK)),
                  pl.BlockSpec((None,) + vdt.shape[1:], lambda b, i: (b, 0, 0, 0)),
                  pl.BlockSpec(u.shape, lambda b, i: (0, 0))],
        out_specs=pl.BlockSpec((None, TB, BLK), lambda b, i: (b, i, 0)),
        out_shape=jax.ShapeDtypeStruct((B, S, BLK), BF16),
        scratch_shapes=[pltpu.VMEM((N_HEADS, TB, TB), F32),
                        pltpu.VMEM((N_HEADS, TB, TB), F32),
                        pltpu.VMEM((N_HEADS, TB, TB), BF16),
                        pltpu.VMEM((N_HEADS, 1, TB), F32),
                        pltpu.VMEM((N_HEADS, HEAD_DIM, TB), F32)],
        compiler_params=pltpu.CompilerParams(dimension_semantics=("arbitrary", "arbitrary"),
                                             vmem_limit_bytes=VMEM_LIMIT),
        name="stick_breaking",
    )(qdt, proj3, vdt, jnp.asarray(u, BF16))


def _outproj_kernel(x_ref, oa, ob, oc, os_, ow, od,
                    ga, gb, gl, gc, gd, expand_ref, w_ref, g_ref, out_ref):
    f = lambda r: r[...].astype(F32)
    gates = _dot(gl[...], expand_ref[...])
    o_c = (gates[:, 0:BLK] * f(oc) + gates[:, BLK:2 * BLK] * f(os_)
           + gates[:, 2 * BLK:3 * BLK] * f(ow))
    mixed = [f(oa) * f(ga), f(ob) * f(gb), o_c * f(gc), f(od) * f(gd)]
    y = _dot(mixed[0].astype(BF16), w_ref[0:BLK, :])
    for g in range(1, 4):
        y = y + _dot(mixed[g].astype(BF16), w_ref[g * BLK:(g + 1) * BLK, :])
    ms = jnp.mean(y * y, axis=-1, keepdims=True)
    out_ref[...] = x_ref[...] + (y * lax.rsqrt(ms + EPS)) * g_ref[...]


def _out_projection(x2, mixer_outs, proj, w_out, g_post):
    rows = x2.shape[0]
    tm = TILE
    blk = lambda c: pl.BlockSpec((tm, BLK), lambda i: (i, c))
    expand = np.zeros((LANES, N_BRANCH * BLK), np.float32)
    for h in range(N_HEADS):
        for r in range(N_BRANCH):
            expand[N_BRANCH * h + r, r * BLK + h * HEAD_DIM:r * BLK + (h + 1) * HEAD_DIM] = 1.0
    gate_specs = [blk(PB_AG), blk(PB_BG),
                  pl.BlockSpec((tm, LANES), lambda i: (i, 2 * PB_C1 + 1)), blk(PB_CG), blk(PB_DG)]
    return pl.pallas_call(
        _outproj_kernel,
        grid=(rows // tm,),
        in_specs=[pl.BlockSpec((tm, D_MODEL), lambda i: (i, 0))]
                 + [blk(0)] * len(mixer_outs) + gate_specs
                 + [pl.BlockSpec(expand.shape, lambda i: (0, 0)),
                    pl.BlockSpec((D_MODEL, D_MODEL), lambda i: (0, 0)),
                    pl.BlockSpec((1, D_MODEL), lambda i: (0, 0))],
        out_specs=pl.BlockSpec((tm, D_MODEL), lambda i: (i, 0)),
        out_shape=jax.ShapeDtypeStruct((rows, D_MODEL), F32),
        compiler_params=pltpu.CompilerParams(dimension_semantics=("arbitrary",),
                                             vmem_limit_bytes=VMEM_LIMIT),
        name="out_projection",
    )(x2, *mixer_outs, *([proj] * len(gate_specs)), jnp.asarray(expand, BF16),
      w_out.astype(BF16), g_post.reshape(1, D_MODEL))


def _layer(x2, tabs, B, S, layer, w_rows, w_t, w_out, g_pre, g_post, mla_g_q, mla_g_kv,
           mla_w_uq, mla_w_ukv, nsa_pos_k, nsa_pos_v, nsa_k_w1, nsa_k_w2, nsa_v_w1, nsa_v_w2):
    rows = B * S
    aqkv, aqkv4, aqkv16, proj, vct, vdt, qct, qdt = _in_projection(
        x2, g_pre, w_rows, w_t, layer, tabs, B, S)
    proj3 = proj.reshape(B, S, N_PBLK * BLK)

    o_a = _dilated(aqkv, aqkv4, aqkv16, B, S)

    qnt, qrt, kn, vbt = _mla_up(proj, mla_g_q, mla_g_kv, mla_w_uq, mla_w_ukv, tabs, B, S)
    o_b = _flash_t(qnt, kn.reshape(B, S, BLK), 0, BLK, vbt, B=B, S=S,
                   aux=("per_head", qrt, proj3, 2 * PB_BKV + 1), name="mla_attention")

    kc2, vc2 = _compress(proj3, nsa_pos_k, nsa_pos_v, nsa_k_w1, nsa_k_w2, nsa_v_w1, nsa_v_w2, B, S)
    o_cmp, nsel = _cmp_topk(proj3, kc2, vc2, B, S)
    penalty = np.zeros((S, LANES), np.float32)
    penalty[np.arange(S), np.arange(S) // NSA_SEL_LEN] = MASK_PENALTY
    o_slc = _flash_t(qct, proj3, 2 * PB_C0 + 1, LANES, vct, B=B, S=S, vrow0=0,
                     aux=("shared", nsel, jnp.asarray(penalty, BF16), 0), name="nsa_selected")
    o_win = _flash_t(qct, proj3, 2 * PB_C1, LANES, vct, B=B, S=S, vrow0=HEAD_DIM,
                     window=NSA_WINDOW - 1, name="nsa_window")

    o_d = _stick_breaking(qdt, proj3, vdt, B, S)

    flat = lambda t: t.reshape(rows, BLK)
    outs = [flat(o) for o in (o_a, o_b, o_cmp, o_slc, o_win, o_d)]
    return _out_projection(x2, outs, proj, w_out, g_post)


def kernel(x, positions, w_in, w_out, g_pre, g_post, mla_g_q, mla_g_kv, mla_w_uq, mla_w_ukv,
           nsa_pos_k, nsa_pos_v, nsa_k_w1, nsa_k_w2, nsa_v_w1, nsa_v_w2):
    B, S, D = x.shape
    assert D == D_MODEL and S % 1024 == 0
    x2 = x.reshape(B * S, D)
    tabs = _rope_tables(positions)
    w_rows, w_t = _pack_w_in(w_in)
    for l in range(w_in.shape[0]):
        x2 = _layer(x2, tabs, B, S, l, w_rows, w_t, w_out[l], g_pre[l], g_post[l],
                    mla_g_q[l], mla_g_kv[l], mla_w_uq[l], mla_w_ukv[l],
                    nsa_pos_k[l], nsa_pos_v[l], nsa_k_w1[l], nsa_k_w2[l],
                    nsa_v_w1[l], nsa_v_w2[l])
    return x2.reshape(B, S, D)
```

```python
import functools
import math

import numpy as np
import jax
import jax.numpy as jnp
from jax import lax
from jax.experimental import pallas as pl
from jax.experimental.pallas import tpu as pltpu

D_MODEL = 1024
HEAD_DIM = 64
N_HEADS = 4
ROPE_THETA = 10000.0
EPS = 1e-6
NEG = -1e30
LOG2E = math.log2(math.e)
MASK_PENALTY = -(2.0 ** 100)
BIG = 1e9

MLA_Q_RANK = 256
MLA_KV_RANK = 128
MLA_NOPE = 64
MLA_ROPE = 32
MLA_V = 64

NSA_CMP_LEN = 32
NSA_CMP_STRIDE = 16
NSA_CMP_HID = 256
NSA_SEL_LEN = 64
NSA_N_SEL = 16
NSA_WINDOW = 512

LANES = 128
BLK = 256
TILE = 512
DILATIONS = (1, 4, 16)
DIL_BLOCK = 128
LSE_PARTS = 2
TB = 256
ONES_ROWS = 16
VMEM_LIMIT = 48 * 1024 * 1024

F32 = jnp.float32
BF16 = jnp.bfloat16

(PB_AG, PB_BCQ, PB_BKV, PB_BG, PB_CQ, PB_C0, PB_C1, PB_CG, PB_DQ, PB_DK, PB_DG) = range(11)
N_PBLK = 11
N_BRANCH = 3
ROPE_NONE, ROPE_64, ROPE_32, ROPE_64_LO, ACT_SILU, ACT_SIGMOID = range(6)


def _dot_nt(a, b):
    return lax.dot_general(a, b, (((1,), (1,)), ((), ())), preferred_element_type=F32)


def _dot(a, b):
    return jnp.dot(a, b, preferred_element_type=F32)


def _rope_half(y, cos, sin_signed, group):
    half = group // 2
    lane = lax.broadcasted_iota(jnp.int32, y.shape, 1)
    first = (lane & (group - 1)) < half
    partner = jnp.where(first, pltpu.roll(y, LANES - half, 1), pltpu.roll(y, half, 1))
    return y * cos + partner * sin_signed


def _rope_table_kernel(pos_ref, inv_ref, cos_ref, sin_ref):
    ang = inv_ref[...] * pos_ref[...].astype(F32)
    cos_ref[...] = jnp.cos(ang)
    sin_ref[...] = jnp.sin(ang)


def _rope_tables(positions):
    rows = positions.size
    tm = 2048
    inv = np.concatenate([ROPE_THETA ** (-np.arange(n, dtype=np.float64) / n)
                          for n in (HEAD_DIM // 2, MLA_ROPE // 2)]).astype(np.float32)[:, None]
    out = jax.ShapeDtypeStruct((inv.shape[0], rows), F32)
    tab_spec = pl.BlockSpec((inv.shape[0], tm), lambda i: (0, i))
    return pl.pallas_call(
        _rope_table_kernel,
        grid=(rows // tm,),
        in_specs=[pl.BlockSpec((1, tm), lambda i: (0, i)),
                  pl.BlockSpec(inv.shape, lambda i: (0, 0))],
        out_specs=[tab_spec] * 2,
        out_shape=[out] * 2,
        name="rope_tables",
    )(positions.reshape(1, rows), jnp.asarray(inv))


def _token_major_rope(cos_ref, sin_ref, group):
    n = group // 2
    lo = 0 if group == HEAD_DIM else HEAD_DIM // 2
    c = cos_ref[lo:lo + n, :]
    s = sin_ref[lo:lo + n, :]
    reps = LANES // n
    cos = jnp.concatenate([c] * reps, axis=0).T
    sin = jnp.concatenate([-s if k % 2 == 0 else s for k in range(reps)], axis=0).T
    return cos, sin


def _inproj_kernel(x_ref, g_ref, w_ref, wt_ref, perm_ref, cos_ref, sin_ref,
                   aqkv_ref, aqkv4_ref, aqkv16_ref, proj_ref, vct_ref, vdt_ref, qct_ref, qdt_ref,
                   *, plan_a, plan_p):
    x = x_ref[...]
    tm = x.shape[0]
    ms = jnp.mean(x * x, axis=-1, keepdims=True)
    h = ((x * lax.rsqrt(ms + EPS)) * g_ref[...]).astype(BF16)
    lane = lax.broadcasted_iota(jnp.int32, (tm, LANES), 1)
    c64, s64 = _token_major_rope(cos_ref, sin_ref, HEAD_DIM)
    c32, s32 = _token_major_rope(cos_ref, sin_ref, MLA_ROPE)

    def epilogue(y, ops, scale):
        halves = []
        for hf, op in enumerate(ops):
            yh = y[:, hf * LANES:(hf + 1) * LANES]
            if op == ROPE_64:
                yh = _rope_half(yh, c64, s64, 64)
            elif op == ROPE_32:
                yh = _rope_half(yh, c32, s32, 32)
            elif op == ROPE_64_LO:
                yh = jnp.where(lane < 64, _rope_half(yh, c64, s64, 64), yh)
            elif op == ACT_SILU:
                yh = yh * (1.0 / (1.0 + jnp.exp(-yh)))
            elif op == ACT_SIGMOID:
                yh = 1.0 / (1.0 + jnp.exp(-yh))
            if scale != 1.0:
                yh = yh * scale
            halves.append(yh)
        return jnp.concatenate(halves, axis=1)

    feature_major = {PB_CQ: qct_ref, PB_DQ: qdt_ref}
    col = 0
    for out_ref, plan in ((aqkv_ref, plan_a), (proj_ref, plan_p)):
        for b, (ops, scale) in enumerate(plan):
            y = epilogue(_dot(h, w_ref[:, col:col + BLK]), ops, scale)
            out_ref[:, b * BLK:(b + 1) * BLK] = y.astype(BF16)
            if out_ref is proj_ref and b in feature_major:
                y_t = y.T
                for blk in range(tm // TB):
                    feature_major[b][blk] = y_t[:, blk * TB:(blk + 1) * TB].astype(BF16)
            col += BLK

    a_rows = aqkv_ref[...]
    aqkv4_ref[...] = _dot(perm_ref[0], a_rows).astype(BF16)
    aqkv16_ref[...] = _dot(perm_ref[1], a_rows).astype(BF16)

    vt = _dot_nt(wt_ref[...], h)
    for blk in range(tm // TB):
        cols = slice(blk * TB, (blk + 1) * TB)
        vct_ref[blk] = vt[0:LANES, cols].astype(BF16)
        vdt_ref[blk] = vt[LANES:LANES + BLK, cols].astype(BF16)


_PLAN_A = (((ROPE_64, ROPE_64), LOG2E * HEAD_DIM ** -0.5),
           ((ROPE_64, ROPE_64), 1.0),
           ((ROPE_NONE, ROPE_NONE), 1.0))
_PLAN_P = (
    ((ACT_SILU, ACT_SILU), 1.0),
    ((ROPE_NONE, ROPE_NONE), 1.0),
    ((ROPE_NONE, ROPE_32), 1.0),
    ((ACT_SILU, ACT_SILU), 1.0),
    ((ROPE_64, ROPE_64), LOG2E * HEAD_DIM ** -0.5),
    ((ROPE_64_LO, ROPE_64), 1.0),
    ((ROPE_64, ACT_SIGMOID), 1.0),
    ((ACT_SILU, ACT_SILU), 1.0),
    ((ROPE_NONE, ROPE_NONE), LOG2E * HEAD_DIM ** -0.5),
    ((ROPE_NONE, ROPE_NONE), 1.0),
    ((ACT_SILU, ACT_SILU), 1.0),
)
assert len(_PLAN_P) == N_PBLK


_W_IN_SEGMENTS = (("aq", 256), ("ak", 256), ("av", 256), ("ag", 256),
                  ("bcq", 256), ("bckv", 128), ("bkr", 32), ("bg", 256),
                  ("cq", 256), ("ckc", 64), ("cvc", 64), ("cks", 64), ("cvs", 64),
                  ("ckw", 64), ("cvw", 64), ("cgl", 12), ("cg", 256),
                  ("dq", 256), ("dk", 256), ("dv", 256), ("dg", 256))
W_ROWS_COLS = (3 + N_PBLK) * BLK
W_T_ROWS = LANES + BLK


def _pack_w_kernel(wt_ref, rows_ref, t_ref):
    wt = wt_ref[...].astype(F32)
    o = 0
    seg = {}
    for name, width in _W_IN_SEGMENTS:
        seg[name] = wt[o:o + width, :]
        o += width
    gl_pad = jnp.zeros((LANES - N_HEADS * N_BRANCH, wt.shape[1]), wt.dtype)
    rows = [seg["aq"], seg["ak"], seg["av"],
            seg["ag"], seg["bcq"],
            seg["bckv"], seg["bkr"], seg["bkr"], seg["bkr"], seg["bkr"],
            seg["bg"], seg["cq"],
            seg["ckc"], seg["cvc"], seg["cks"], seg["cks"],
            seg["ckw"], seg["ckw"], seg["cgl"], gl_pad,
            seg["cg"], seg["dq"], seg["dk"], seg["dg"]]
    rows_ref[...] = jnp.concatenate(rows, axis=0).T.astype(BF16)
    t_ref[...] = jnp.concatenate([seg["cvs"], seg["cvw"], seg["dv"]], axis=0).astype(BF16)


def _pack_w_in(w_in):
    depth, d_model, d_in = w_in.shape
    assert d_model == D_MODEL and d_in == sum(width for _, width in _W_IN_SEGMENTS)
    tc = 256
    return pl.pallas_call(
        _pack_w_kernel,
        grid=(depth, D_MODEL // tc),
        in_specs=[pl.BlockSpec((None, d_in, tc), lambda l, i: (l, 0, i))],
        out_specs=[pl.BlockSpec((None, tc, W_ROWS_COLS), lambda l, i: (l, i, 0)),
                   pl.BlockSpec((None, W_T_ROWS, tc), lambda l, i: (l, 0, i))],
        out_shape=[jax.ShapeDtypeStruct((depth, D_MODEL, W_ROWS_COLS), BF16),
                   jax.ShapeDtypeStruct((depth, W_T_ROWS, D_MODEL), BF16)],
        compiler_params=pltpu.CompilerParams(dimension_semantics=("arbitrary", "arbitrary"),
                                             vmem_limit_bytes=VMEM_LIMIT),
        name="pack_w_in",
    )(jnp.transpose(w_in, (0, 2, 1)).astype(BF16))


def _class_major_perm(dil):
    n = TILE // dil
    p = np.zeros((TILE, TILE), np.float32)
    pos = np.arange(TILE)
    p[(pos % dil) * n + pos // dil, pos] = 1.0
    return p


def _in_projection(x2, g_pre, w_rows, w_t, layer, tabs, B, S):
    rows = x2.shape[0]
    tm = TILE
    per_b = S // tm
    nkb = tm // TB
    perm = jnp.asarray(np.stack([_class_major_perm(d) for d in DILATIONS[1:]]), BF16)
    tab_spec = pl.BlockSpec((tabs[0].shape[0], tm), lambda i: (0, i))
    vt_spec = lambda r: pl.BlockSpec((None, nkb, r, TB), lambda i: (i // per_b, i % per_b, 0, 0))
    a_spec = pl.BlockSpec((tm, 3 * BLK), lambda i: (i, 0))
    a_shape = jax.ShapeDtypeStruct((rows, 3 * BLK), BF16)
    return pl.pallas_call(
        functools.partial(_inproj_kernel, plan_a=_PLAN_A, plan_p=_PLAN_P),
        grid=(rows // tm,),
        in_specs=[pl.BlockSpec((tm, D_MODEL), lambda i: (i, 0)),
                  pl.BlockSpec((1, D_MODEL), lambda i: (0, 0)),
                  pl.BlockSpec((None,) + w_rows.shape[1:], lambda i: (layer, 0, 0)),
                  pl.BlockSpec((None,) + w_t.shape[1:], lambda i: (layer, 0, 0)),
                  pl.BlockSpec(perm.shape, lambda i: (0, 0, 0))] + [tab_spec] * 2,
        out_specs=[a_spec, a_spec, a_spec,
                   pl.BlockSpec((tm, N_PBLK * BLK), lambda i: (i, 0)),
                   vt_spec(LANES), vt_spec(BLK), vt_spec(BLK), vt_spec(BLK)],
        out_shape=[a_shape, a_shape, a_shape,
                   jax.ShapeDtypeStruct((rows, N_PBLK * BLK), BF16),
                   jax.ShapeDtypeStruct((B, S // TB, LANES, TB), BF16)]
                  + [jax.ShapeDtypeStruct((B, S // TB, BLK, TB), BF16)] * 3,
        compiler_params=pltpu.CompilerParams(dimension_semantics=("arbitrary",),
                                             vmem_limit_bytes=VMEM_LIMIT),
        name="in_projection",
    )(x2, g_pre.reshape(1, D_MODEL), w_rows, w_t, perm, *tabs)


def _head_masks(shape):
    lane = lax.broadcasted_iota(jnp.int32, shape, 1)
    return lane < HEAD_DIM


def _masked_heads(q_ref, lo_mask):
    out = []
    for h in range(N_HEADS):
        pair, e = divmod(h, 2)
        qp = q_ref[:, pair * LANES:(pair + 1) * LANES]
        keep = lo_mask if e == 0 else jnp.logical_not(lo_mask)
        out.append(jnp.where(keep, qp, jnp.zeros_like(qp)))
    return out


def _dilated_kernel(an_ref, a4_ref, a16_ref, unperm_ref, expand_ref, o_ref, res_ref, s_ref):
    T = DIL_BLOCK
    S = an_ref.shape[0]
    n_tiles = S // TILE
    lo = _head_masks((T, LANES))
    lane = lax.broadcasted_iota(jnp.int32, (T, LANES), 1)
    dist2 = (lax.broadcasted_iota(jnp.int32, (T, 2 * T), 0)
             - lax.broadcasted_iota(jnp.int32, (T, 2 * T), 1))
    causal1 = (lax.broadcasted_iota(jnp.int32, (T, T), 1)
               <= lax.broadcasted_iota(jnp.int32, (T, T), 0))

    def scores(q, k):
        qm = _masked_heads(q, lo)
        return tuple(_dot_nt(qm[h], k[:, (h // 2) * LANES:(h // 2 + 1) * LANES])
                     for h in range(N_HEADS))

    def finish(n_keys, v, mask, stage_next):
        probs = []
        for h in range(N_HEADS):
            s = jnp.where(mask, s_ref[h, :, 0:n_keys], NEG)
            m = jnp.max(s, axis=-1, keepdims=True)
            p = jnp.exp2(s - m)
            probs.append((m, jnp.sum(p, axis=-1, keepdims=True), p.astype(BF16)))
        stage_next()
        outs = []
        top = jnp.zeros((T, LANES), F32)
        den = jnp.ones((T, LANES), F32)
        for h in range(N_HEADS):
            m, l, p = probs[h]
            outs.append(_dot(p, v[:, (h // 2) * LANES:(h // 2 + 1) * LANES]) * (1.0 / l))
            mine = (lane & (N_HEADS - 1)) == h
            top = jnp.where(mine, m, top)
            den = jnp.where(mine, l, den)
        lse = top + jnp.log2(den)
        hi = lse.astype(BF16).astype(F32)
        parts = jnp.where(lane < N_HEADS, hi,
                          jnp.where(lane < LSE_PARTS * N_HEADS, lse - hi, 0.0))
        return jnp.concatenate(
            [jnp.where(lo, outs[0], outs[1]), jnp.where(lo, outs[2], outs[3]), parts],
            axis=1).astype(BF16)

    def band_mask(delta):
        d = dist2 + delta
        return jnp.logical_and(d >= 0, d <= DIL_BLOCK)

    def run(n_units, n_keys, load_qk, load_v, store, mask_of):
        def stage(u):
            for h, s in enumerate(scores(*load_qk(u))):
                s_ref[h, :, 0:n_keys] = s

        def body(u, carry):
            nxt = jnp.minimum(u + 1, n_units - 1)
            store(u, finish(n_keys, load_v(u), mask_of(u), lambda: stage(nxt)))
            return carry

        stage(0)
        lax.fori_loop(0, n_units, body, 0)

    def rows(start, size):
        return pl.ds(pl.multiple_of(start, 32), size)

    def k0_1(u):
        return T * jnp.maximum(u - 1, 0)

    def store1(u, r):
        res_ref[0, rows(T * u, T), :] = r

    run(S // T, 2 * T,
        lambda u: (an_ref[rows(T * u, T), 0:BLK], an_ref[rows(k0_1(u), 2 * T), BLK:2 * BLK]),
        lambda u: an_ref[rows(k0_1(u), 2 * T), 2 * BLK:3 * BLK],
        store1,
        lambda u: band_mask(T * u - k0_1(u)))

    def piece4(u, blk):
        return rows(TILE * blk + T * (u // n_tiles), T)

    def kb0_4(u):
        return jnp.maximum(u % n_tiles - 1, 0)

    def load4(u, cols):
        return jnp.concatenate([a4_ref[piece4(u, kb0_4(u)), cols],
                                a4_ref[piece4(u, kb0_4(u) + 1), cols]], axis=0)

    def store4(u, r):
        res_ref[1, piece4(u, u % n_tiles), :] = r

    run(4 * n_tiles, 2 * T,
        lambda u: (a4_ref[piece4(u, u % n_tiles), 0:BLK], load4(u, slice(BLK, 2 * BLK))),
        lambda u: load4(u, slice(2 * BLK, 3 * BLK)),
        store4,
        lambda u: band_mask(T * (u % n_tiles - kb0_4(u))))

    per = TILE // 16

    def load16(c, cols):
        return jnp.concatenate([a16_ref[rows(TILE * t + per * c, per), cols]
                                for t in range(n_tiles)], axis=0)

    def store16(c, r):
        for t in range(n_tiles):
            res_ref[2, rows(TILE * t + per * c, per), :] = r[per * t:per * (t + 1), :]

    run(16, T,
        lambda c: (load16(c, slice(0, BLK)), load16(c, slice(BLK, 2 * BLK))),
        lambda c: load16(c, slice(2 * BLK, 3 * BLK)),
        store16,
        lambda c: causal1)

    expand = expand_ref[...]
    for t in range(n_tiles):
        tile = slice(TILE * t, TILE * (t + 1))
        nat = [res_ref[0, tile, :].astype(F32),
               _dot(unperm_ref[0], res_ref[1, tile, :]),
               _dot(unperm_ref[1], res_ref[2, tile, :])]
        lse = [_dot(r[:, BLK:].astype(BF16), expand) for r in nat]
        top = jnp.maximum(lse[0], jnp.maximum(lse[1], lse[2]))
        w = [jnp.exp2(l - top) for l in lse]
        mixed = (w[0] * nat[0][:, :BLK] + w[1] * nat[1][:, :BLK] + w[2] * nat[2][:, :BLK])
        o_ref[tile, :] = (mixed / (w[0] + w[1] + w[2])).astype(BF16)


def _dilated(aqkv, aqkv4, aqkv16, B, S):
    assert S == 16 * DIL_BLOCK and S % TILE == 0 and TILE == 4 * DIL_BLOCK
    r3 = lambda t: t.reshape(B, S, 3 * BLK)
    unperm = jnp.asarray(np.stack([_class_major_perm(d).T for d in DILATIONS[1:]]), BF16)
    expand = np.zeros((LANES, BLK), np.float32)
    for h in range(N_HEADS):
        for part in range(LSE_PARTS):
            expand[N_HEADS * part + h, h * HEAD_DIM:(h + 1) * HEAD_DIM] = 1.0
    a_spec = pl.BlockSpec((None, S, 3 * BLK), lambda b: (b, 0, 0))
    return pl.pallas_call(
        _dilated_kernel,
        grid=(B,),
        in_specs=[a_spec, a_spec, a_spec,
                  pl.BlockSpec(unperm.shape, lambda b: (0, 0, 0)),
                  pl.BlockSpec(expand.shape, lambda b: (0, 0))],
        out_specs=pl.BlockSpec((None, S, BLK), lambda b: (b, 0, 0)),
        out_shape=jax.ShapeDtypeStruct((B, S, BLK), BF16),
        scratch_shapes=[pltpu.VMEM((3, S, BLK + LANES), BF16),
                        pltpu.VMEM((N_HEADS, DIL_BLOCK, 2 * DIL_BLOCK), F32)],
        compiler_params=pltpu.CompilerParams(dimension_semantics=("arbitrary",),
                                             vmem_limit_bytes=VMEM_LIMIT),
        name="dilated_mixture",
    )(r3(aqkv), r3(aqkv4), r3(aqkv16), unperm, jnp.asarray(expand, BF16))


def _masked_head_rows(qt_ref):
    out = []
    zeros = jnp.zeros((HEAD_DIM, qt_ref.shape[-1]), BF16)
    for h in range(N_HEADS):
        own = qt_ref[h * HEAD_DIM:(h + 1) * HEAD_DIM, :]
        out.append(jnp.concatenate([own, zeros] if h % 2 == 0 else [zeros, own], axis=0))
    return out


def _flash_t_kernel(*refs, window, kshared, vrow0, aux):
    it = iter(refs)
    q_all_ref = next(it)
    qx_all_ref = next(it) if aux else None
    k_ref = next(it)
    kx_ref = next(it) if aux else None
    vt_ref = next(it)
    o_ref = next(it)
    s_ref, m_ref, acc_ref = next(it), next(it), next(it)
    T = TB

    krow = lax.broadcasted_iota(jnp.int32, (T, T), 0)
    qcol = lax.broadcasted_iota(jnp.int32, (T, T), 1)
    ones = jnp.ones((ONES_ROWS, T), BF16)

    def query_block(i, carry):
        qcat = _masked_head_rows(q_all_ref.at[i])
        if aux == "per_head":
            group = LANES // N_HEADS
            qcat = [jnp.concatenate(
                [qcat[h]] + [qx_all_ref[i, g * group:(g + 1) * group, :] if g == h
                             else jnp.zeros((group, T), BF16) for g in range(N_HEADS)], axis=0)
                for h in range(N_HEADS)]
        elif aux == "shared":
            qx = qx_all_ref[i]
            qcat = [jnp.concatenate([qcat[h], qx], axis=0) for h in range(N_HEADS)]

        def scores(j):
            rows = pl.ds(pl.multiple_of(j * T, T), T)
            out = []
            for h in range(N_HEADS):
                pair = h // 2
                cols = slice(0, LANES) if kshared else slice(pair * LANES, (pair + 1) * LANES)
                kb = k_ref[rows, cols]
                if aux:
                    kb = jnp.concatenate([kb, kx_ref[rows, :]], axis=1)
                out.append(_dot(kb, qcat[h]))
            return out

        def stage(j):
            for h, s in enumerate(scores(j)):
                s_ref[h] = s

        def absorb(j, mask, staged_next=None, s_all=None):
            vt_all = vt_ref[j]
            probs = []
            for h in range(N_HEADS):
                s = s_ref[h] if s_all is None else s_all[h]
                if mask is not None:
                    s = jnp.where(mask, s, NEG)
                m = m_ref[h]
                m_new = jnp.maximum(m, jnp.max(s, axis=0, keepdims=True))
                alpha = jnp.exp2(m - m_new)
                p = jnp.exp2(s - m_new)
                probs.append((m_new, alpha, p.astype(BF16)))
            if staged_next is not None:
                stage(staged_next)
            for h in range(N_HEADS):
                m_new, alpha, p = probs[h]
                r0 = vrow0 if kshared else h * HEAD_DIM
                lhs = jnp.concatenate([vt_all[r0:r0 + HEAD_DIM, :], ones], axis=0)
                acc_ref[h] = alpha * acc_ref[h] + _dot(lhs, p)
                m_ref[h] = m_new

        m_ref[...] = jnp.full(m_ref.shape, NEG, F32)
        acc_ref[...] = jnp.zeros(acc_ref.shape, F32)
        if window is None:
            first = 0
            stage(first)
        else:
            o_full = (window - T + 1) // T
            o_none = (window + T - 1) // T + 1
            assert o_none - o_full == 2
            o = o_full + 1
            j = i - o
            jc = jnp.maximum(j, 0)
            first = jnp.maximum(i - o_full, 0)
            s_band = scores(jc)
            stage(first)
            limit = jnp.where(j >= 0, window - o * T, -2 * T)
            absorb(jc, (qcol - krow) <= limit, s_all=s_band)

        def body(j, c):
            absorb(j, None, staged_next=j + 1)
            return c

        lax.fori_loop(first, i, body, 0)
        absorb(i, krow <= qcol)

        q_rows = pl.ds(pl.multiple_of(i * T, T), T)
        for pair in range(2):
            outs = []
            for e in range(2):
                acc = acc_ref[2 * pair + e]
                outs.append(acc[0:HEAD_DIM, :] * (1.0 / acc[HEAD_DIM:HEAD_DIM + 1, :]))
            o_ref[q_rows, pair * LANES:(pair + 1) * LANES] = (
                jnp.concatenate(outs, axis=0).T.astype(BF16))
        return carry

    lax.fori_loop(0, q_all_ref.shape[0], query_block, 0)


def _flash_t(qt, k, kcol, kw, vt, *, B, S, window=None, vrow0=0, aux=None, name):
    kshared = kw == LANES
    whole = lambda a: pl.BlockSpec((None,) + a.shape[1:], lambda b: (b,) + (0,) * (a.ndim - 1))
    in_specs = [whole(qt)]
    args = [qt]
    if aux is not None:
        kind, qx, kx, kxcol = aux
        in_specs.append(whole(qx))
        args.append(qx)
    in_specs.append(pl.BlockSpec((None, S, kw), lambda b: (b, 0, kcol)))
    args.append(k)
    if aux is not None:
        if kx.ndim == 3:
            in_specs.append(pl.BlockSpec((None, S, LANES), lambda b: (b, 0, kxcol)))
        else:
            in_specs.append(pl.BlockSpec((S, LANES), lambda b: (0, kxcol)))
        args.append(kx)
    in_specs.append(whole(vt))
    args.append(vt)
    return pl.pallas_call(
        functools.partial(_flash_t_kernel, window=window, kshared=kshared, vrow0=vrow0,
                          aux=None if aux is None else aux[0]),
        grid=(B,),
        in_specs=in_specs,
        out_specs=pl.BlockSpec((None, S, BLK), lambda b: (b, 0, 0)),
        out_shape=jax.ShapeDtypeStruct((B, S, BLK), BF16),
        scratch_shapes=[pltpu.VMEM((N_HEADS, TB, TB), F32),
                        pltpu.VMEM((N_HEADS, 1, TB), F32),
                        pltpu.VMEM((N_HEADS, HEAD_DIM + ONES_ROWS, TB), F32)],
        compiler_params=pltpu.CompilerParams(dimension_semantics=("arbitrary",),
                                             vmem_limit_bytes=VMEM_LIMIT),
        name=name,
    )(*args)


def _mla_up_kernel(cq_ref, ckv_ref, gq_ref, gkv_ref, wq_ref, wk_ref, wvt_ref, cos_ref, sin_ref,
                   qn_ref, qr_ref, kn_ref, vt_ref):
    def norm(t, g):
        t = t.astype(F32)
        ms = jnp.mean(t * t, axis=-1, keepdims=True)
        return ((t * lax.rsqrt(ms + EPS)) * g).astype(BF16)

    scale = LOG2E * (MLA_NOPE + MLA_ROPE) ** -0.5
    q_t = _dot_nt(wq_ref[...], norm(cq_ref[...], gq_ref[...])) * scale
    qn_t = q_t[:BLK, :]
    half = MLA_ROPE // 2
    cos = cos_ref[HEAD_DIM // 2:HEAD_DIM // 2 + half, :]
    sin = sin_ref[HEAD_DIM // 2:HEAD_DIM // 2 + half, :]
    rot = []
    for h in range(N_HEADS):
        x1 = q_t[BLK + h * MLA_ROPE:BLK + h * MLA_ROPE + half, :]
        x2 = q_t[BLK + h * MLA_ROPE + half:BLK + (h + 1) * MLA_ROPE, :]
        rot += [x1 * cos - x2 * sin, x1 * sin + x2 * cos]
    qr_t = jnp.concatenate(rot, axis=0)
    ckv = norm(ckv_ref[...], gkv_ref[...])
    kn_ref[...] = _dot(ckv, wk_ref[...]).astype(BF16)
    vt = _dot_nt(wvt_ref[...], ckv)
    for blk in range(vt.shape[1] // TB):
        cols = slice(blk * TB, (blk + 1) * TB)
        qn_ref[blk] = qn_t[:, cols].astype(BF16)
        qr_ref[blk] = qr_t[:, cols].astype(BF16)
        vt_ref[blk] = vt[:, cols].astype(BF16)


def _mla_up(proj, g_q, g_kv, w_uq, w_ukv, tabs, B, S):
    rows = proj.shape[0]
    tm = 1024
    per_b = S // tm
    wq = w_uq.reshape(MLA_Q_RANK, N_HEADS, MLA_NOPE + MLA_ROPE)
    wq = jnp.concatenate([wq[:, :, :MLA_NOPE].reshape(MLA_Q_RANK, -1),
                          wq[:, :, MLA_NOPE:].reshape(MLA_Q_RANK, -1)], axis=1).T.astype(BF16)
    wkv = w_ukv.reshape(MLA_KV_RANK, N_HEADS, MLA_NOPE + MLA_V)
    wk = wkv[:, :, :MLA_NOPE].reshape(MLA_KV_RANK, -1).astype(BF16)
    wvt = wkv[:, :, MLA_NOPE:].reshape(MLA_KV_RANK, -1).T.astype(BF16)
    row_spec = lambda w, c: pl.BlockSpec((tm, w), lambda i: (i, c))
    full = lambda a: pl.BlockSpec(a.shape, lambda i: (0, 0))
    gq = g_q.reshape(1, -1)
    gkv = g_kv.reshape(1, -1)
    fm_shape = lambda r: jax.ShapeDtypeStruct((B, S // TB, r, TB), BF16)
    fm_spec = lambda r: pl.BlockSpec((None, tm // TB, r, TB),
                                     lambda i: (i // per_b, i % per_b, 0, 0))
    outs = [fm_shape(BLK), fm_shape(LANES), jax.ShapeDtypeStruct((rows, BLK), BF16), fm_shape(BLK)]
    return pl.pallas_call(
        _mla_up_kernel,
        grid=(rows // tm,),
        in_specs=[row_spec(BLK, PB_BCQ), row_spec(LANES, 2 * PB_BKV), full(gq), full(gkv),
                  full(wq), full(wk), full(wvt)]
                 + [pl.BlockSpec((tabs[0].shape[0], tm), lambda i: (0, i))] * 2,
        out_specs=[fm_spec(BLK), fm_spec(LANES), row_spec(BLK, 0), fm_spec(BLK)],
        out_shape=outs,
        compiler_params=pltpu.CompilerParams(dimension_semantics=("arbitrary",),
                                             vmem_limit_bytes=VMEM_LIMIT),
        name="mla_up",
    )(proj, proj, gq, gkv, wq, wk, wvt, *tabs)


def _compress_kernel(x_ref, perm_ref, pos_ref, w1_ref, w2_ref, kc_ref, vc_ref):
    n_tiles = x_ref.shape[0] // TILE
    per = TILE // NSA_CMP_STRIDE
    slabs = []
    for t in range(n_tiles):
        cm = _dot(perm_ref[...], x_ref[TILE * t:TILE * (t + 1), :])
        slabs.append(jnp.concatenate([cm[per * c:per * (c + 1), :]
                                      for c in range(NSA_CMP_STRIDE)], axis=1))
    x = jnp.concatenate(slabs, axis=0)
    half = x.shape[1]
    top = (x + pos_ref[0:1, :]).astype(BF16)
    bot = (x + pos_ref[1:2, :]).astype(BF16)
    y_top = _dot(top, w1_ref[0:half, :])
    y_bot = _dot(bot, w1_ref[half:2 * half, :])
    pre = y_top + pltpu.roll(y_bot, y_bot.shape[0] - 1, 0)
    hid = (pre * (1.0 / (1.0 + jnp.exp(-pre)))).astype(BF16)
    kc_ref[...] = _dot(hid[:, :NSA_CMP_HID], w2_ref[0]).astype(BF16)
    vc_ref[...] = _dot(hid[:, NSA_CMP_HID:], w2_ref[1]).astype(BF16)


def _compress(proj3, pos_k, pos_v, kw1, kw2, vw1, vw2, B, S):
    assert S % TILE == 0 and NSA_CMP_LEN == 2 * NSA_CMP_STRIDE
    n16 = S // NSA_CMP_STRIDE
    perm = jnp.asarray(_class_major_perm(NSA_CMP_STRIDE), BF16)
    pos = jnp.concatenate([pos_k, pos_v], axis=1).reshape(2, NSA_CMP_STRIDE * LANES)
    w1 = jnp.zeros((NSA_CMP_LEN, LANES, 2 * NSA_CMP_HID), F32)
    w1 = w1.at[:, :HEAD_DIM, :NSA_CMP_HID].set(kw1.reshape(NSA_CMP_LEN, HEAD_DIM, NSA_CMP_HID))
    w1 = w1.at[:, HEAD_DIM:, NSA_CMP_HID:].set(vw1.reshape(NSA_CMP_LEN, HEAD_DIM, NSA_CMP_HID))
    w1 = w1.reshape(NSA_CMP_LEN * LANES, 2 * NSA_CMP_HID).astype(BF16)
    w2 = jnp.stack([jnp.tile(kw2, (1, 2)), jnp.tile(vw2, (1, 2))]).astype(BF16)
    out = jax.ShapeDtypeStruct((B, n16, LANES), BF16)
    return pl.pallas_call(
        _compress_kernel,
        grid=(B,),
        in_specs=[pl.BlockSpec((None, S, LANES), lambda b: (b, 0, 2 * PB_C0)),
                  pl.BlockSpec(perm.shape, lambda b: (0, 0)),
                  pl.BlockSpec(pos.shape, lambda b: (0, 0)),
                  pl.BlockSpec(w1.shape, lambda b: (0, 0)),
                  pl.BlockSpec(w2.shape, lambda b: (0, 0, 0))],
        out_specs=[pl.BlockSpec((None, n16, LANES), lambda b: (b, 0, 0))] * 2,
        out_shape=[out, out],
        compiler_params=pltpu.CompilerParams(dimension_semantics=("arbitrary",),
                                             vmem_limit_bytes=VMEM_LIMIT),
        name="nsa_compress",
    )(proj3, perm, pos, w1, w2)


def _cmp_topk_kernel(q_ref, kc_ref, vc_ref, ovt_ref, o_ref, nsel_ref, *, T, n_cmp, n_slc):
    i = pl.program_id(1)
    t0 = i * T
    lane = lax.broadcasted_iota(jnp.int32, (T, LANES), 1)
    tok = t0 + lax.broadcasted_iota(jnp.int32, (T, LANES), 0)
    ok = jnp.logical_and(lane * NSA_CMP_STRIDE + (NSA_CMP_LEN - 1) <= tok, lane < n_cmp)
    lo = lane < HEAD_DIM
    qm = _masked_heads(q_ref, lo)
    kc = kc_ref[...]
    vc = vc_ref[...]
    psum = jnp.zeros((T, LANES), F32)
    outs = []
    for h in range(N_HEADS):
        s = jnp.where(ok, _dot_nt(qm[h], kc), NEG)
        mx = jnp.max(s, axis=-1, keepdims=True)
        e = jnp.where(ok, jnp.exp2(s - mx), 0.0)
        den = jnp.sum(e, axis=-1, keepdims=True)
        p = e * (1.0 / jnp.maximum(den, 1e-30))
        psum = psum + p
        outs.append(_dot(p.astype(BF16), vc))
    for pair in range(2):
        o_ref[:, pair * LANES:(pair + 1) * LANES] = jnp.where(
            lo, outs[2 * pair], outs[2 * pair + 1]).astype(BF16)

    p_hi = psum.astype(BF16)
    p_lo = (psum - p_hi.astype(F32)).astype(BF16)
    ovt = ovt_ref[...]
    imp = _dot_nt(ovt, p_hi) + _dot_nt(ovt, p_lo)
    jj = lax.broadcasted_iota(jnp.int32, (n_slc, T), 0)
    bt = (t0 + lax.broadcasted_iota(jnp.int32, (n_slc, T), 1)) >> 6
    forced = jnp.logical_or(jj == 0, jnp.logical_or(jj == bt, jj == bt - 1))
    val = jnp.where(forced, BIG, jnp.where(jj > bt, -BIG, imp))
    sel_rows = []
    for j in range(n_slc):
        vj = val[j:j + 1, :]
        beats = jnp.logical_or(val > vj, jnp.logical_and(val == vj, jj < j))
        rank = jnp.sum(beats.astype(F32), axis=0, keepdims=True)
        sel_rows.append((rank >= NSA_N_SEL).astype(F32))
    sel_rows.append(jnp.zeros((LANES - n_slc, T), F32))
    nsel = jnp.concatenate(sel_rows, axis=0).astype(BF16)
    for blk in range(T // TB):
        nsel_ref[blk] = nsel[:, blk * TB:(blk + 1) * TB]


def _cmp_topk(proj3, kc2, vc2, B, S):
    T = 2 * TB
    n_cmp = (S - NSA_CMP_LEN) // NSA_CMP_STRIDE + 1
    n_slc = S // NSA_SEL_LEN
    assert NSA_SEL_LEN == 64 and n_slc <= LANES and n_cmp <= LANES
    ci = np.arange(LANES)[None, :] * NSA_CMP_STRIDE
    sj = np.arange(n_slc)[:, None] * NSA_SEL_LEN
    ovt = ((ci < sj + NSA_SEL_LEN) & (ci + NSA_CMP_LEN > sj) & (np.arange(LANES)[None, :] < n_cmp))
    ovt = jnp.asarray(ovt.astype(np.float32), BF16)
    return pl.pallas_call(
        functools.partial(_cmp_topk_kernel, T=T, n_cmp=n_cmp, n_slc=n_slc),
        grid=(B, S // T),
        in_specs=[pl.BlockSpec((None, T, BLK), lambda b, i: (b, i, PB_CQ)),
                  pl.BlockSpec((None, LANES, LANES), lambda b, i: (b, 0, 0)),
                  pl.BlockSpec((None, LANES, LANES), lambda b, i: (b, 0, 0)),
                  pl.BlockSpec(ovt.shape, lambda b, i: (0, 0))],
        out_specs=[pl.BlockSpec((None, T, BLK), lambda b, i: (b, i, 0)),
                   pl.BlockSpec((None, T // TB, LANES, TB), lambda b, i: (b, i, 0, 0))],
        out_shape=[jax.ShapeDtypeStruct((B, S, BLK), BF16),
                   jax.ShapeDtypeStruct((B, S // TB, LANES, TB), BF16)],
        compiler_params=pltpu.CompilerParams(dimension_semantics=("arbitrary", "arbitrary"),
                                             vmem_limit_bytes=VMEM_LIMIT),
        name="nsa_cmp_topk",
    )(proj3, kc2, vc2, ovt)


def _stick_kernel(q_all_ref, k_ref, vt_ref, u_ref, o_ref, za_ref, zb_ref, w_ref, carry_ref,
                  acc_ref):
    T = TB
    krow = lax.broadcasted_iota(jnp.int32, (T, T), 0)
    qcol = lax.broadcasted_iota(jnp.int32, (T, T), 1)
    strict = krow < qcol
    u = u_ref[...]
    z_refs = (za_ref, zb_ref)

    def query_block(i, carry):
        qm = _masked_head_rows(q_all_ref.at[i])

        def stage(j, slot):
            rows = pl.ds(pl.multiple_of(j * T, T), T)
            for h in range(N_HEADS):
                z_refs[slot][h] = _dot(k_ref[rows, (h // 2) * LANES:(h // 2 + 1) * LANES],
                                       qm[h])

        def apply_weights(j):
            vt_all = vt_ref[j]
            for h in range(N_HEADS):
                acc_ref[h] = acc_ref[h] + _dot(vt_all[h * HEAD_DIM:(h + 1) * HEAD_DIM, :], w_ref[h])

        def absorb(j, slot, diag, staged_next, applied_prev):
            z_ref = z_refs[slot]
            softplus = []
            for h in range(N_HEADS):
                z = z_ref[h]
                sp = jnp.maximum(z, 0.0) + jnp.log2(1.0 + jnp.exp2(-jnp.abs(z)))
                if diag:
                    sp = jnp.where(strict, sp, 0.0)
                softplus.append(sp.astype(BF16))
            if staged_next is not None:
                stage(staged_next, 1 - slot)
            if applied_prev is not None:
                apply_weights(applied_prev)
            sums = [_dot(u, sp) for sp in softplus]
            for h in range(N_HEADS):
                a = jnp.exp2(z_ref[h] - (sums[h] + carry_ref[h]))
                if diag:
                    a = jnp.where(strict, a, 0.0)
                w_ref[h] = a.astype(BF16)
                carry_ref[h] = carry_ref[h] + sums[h][0:1, :]

        carry_ref[...] = jnp.zeros(carry_ref.shape, F32)
        acc_ref[...] = jnp.zeros(acc_ref.shape, F32)
        stage(i, 0)
        absorb(i, 0, True, jnp.maximum(i - 1, 0), None)

        def body(t, c):
            j = i - 1 - 2 * t
            absorb(j, 1, False, j - 1, j + 1)
            absorb(j - 1, 0, False, jnp.maximum(j - 2, 0), j)
            return c

        lax.fori_loop(0, i // 2, body, 0)

        @pl.when(i % 2 == 1)
        def _():
            absorb(0, 1, False, None, 1)

        apply_weights(0)
        q_rows = pl.ds(pl.multiple_of(i * T, T), T)
        for pair in range(2):
            o_t = jnp.concatenate([acc_ref[2 * pair], acc_ref[2 * pair + 1]], axis=0)
            o_ref[q_rows, pair * LANES:(pair + 1) * LANES] = o_t.T.astype(BF16)
        return carry

    lax.fori_loop(0, q_all_ref.shape[0], query_block, 0)


def _stick_breaking(qdt, proj3, vdt, B, S):
    u = np.triu(np.ones((TB, TB), np.float32))
    return pl.pallas_call(
        _stick_kernel,
        grid=(B,),
        in_specs=[pl.BlockSpec((None,) + qdt.shape[1:], lambda b: (b, 0, 0, 0)),
                  pl.BlockSpec((None, S, BLK), lambda b: (b, 0, PB_DK)),
                  pl.BlockSpec((None,) + vdt.shape[1:], lambda b: (b, 0, 0, 0)),
                  pl.BlockSpec(u.shape, lambda b: (0, 0))],
        out_specs=pl.BlockSpec((None, S, BLK), lambda b: (b, 0, 0)),
        out_shape=jax.ShapeDtypeStruct((B, S, BLK), BF16),
        scratch_shapes=[pltpu.VMEM((N_HEADS, TB, TB), F32),
                        pltpu.VMEM((N_HEADS, TB, TB), F32),
                        pltpu.VMEM((N_HEADS, TB, TB), BF16),
                        pltpu.VMEM((N_HEADS, 1, TB), F32),
                        pltpu.VMEM((N_HEADS, HEAD_DIM, TB), F32)],
        compiler_params=pltpu.CompilerParams(dimension_semantics=("arbitrary",),
                                             vmem_limit_bytes=VMEM_LIMIT),
        name="stick_breaking",
    )(qdt, proj3, vdt, jnp.asarray(u, BF16))


def _outproj_kernel(x_ref, oa, ob, oc, os_, ow, od,
                    ga, gb, gl, gc, gd, expand_ref, w_ref, g_ref, out_ref):
    f = lambda r: r[...].astype(F32)
    gates = _dot(gl[...], expand_ref[...])
    o_c = (gates[:, 0:BLK] * f(oc) + gates[:, BLK:2 * BLK] * f(os_)
           + gates[:, 2 * BLK:3 * BLK] * f(ow))
    mixed = [f(oa) * f(ga), f(ob) * f(gb), o_c * f(gc), f(od) * f(gd)]
    y = _dot(mixed[0].astype(BF16), w_ref[0:BLK, :])
    for g in range(1, 4):
        y = y + _dot(mixed[g].astype(BF16), w_ref[g * BLK:(g + 1) * BLK, :])
    ms = jnp.mean(y * y, axis=-1, keepdims=True)
    out_ref[...] = x_ref[...] + (y * lax.rsqrt(ms + EPS)) * g_ref[...]


def _out_projection(x2, mixer_outs, proj, w_out, g_post):
    rows = x2.shape[0]
    tm = TILE
    blk = lambda c: pl.BlockSpec((tm, BLK), lambda i: (i, c))
    expand = np.zeros((LANES, N_BRANCH * BLK), np.float32)
    for h in range(N_HEADS):
        for r in range(N_BRANCH):
            expand[N_BRANCH * h + r, r * BLK + h * HEAD_DIM:r * BLK + (h + 1) * HEAD_DIM] = 1.0
    gate_specs = [blk(PB_AG), blk(PB_BG),
                  pl.BlockSpec((tm, LANES), lambda i: (i, 2 * PB_C1 + 1)), blk(PB_CG), blk(PB_DG)]
    return pl.pallas_call(
        _outproj_kernel,
        grid=(rows // tm,),
        in_specs=[pl.BlockSpec((tm, D_MODEL), lambda i: (i, 0))]
                 + [blk(0)] * len(mixer_outs) + gate_specs
                 + [pl.BlockSpec(expand.shape, lambda i: (0, 0)),
                    pl.BlockSpec((D_MODEL, D_MODEL), lambda i: (0, 0)),
                    pl.BlockSpec((1, D_MODEL), lambda i: (0, 0))],
        out_specs=pl.BlockSpec((tm, D_MODEL), lambda i: (i, 0)),
        out_shape=jax.ShapeDtypeStruct((rows, D_MODEL), F32),
        compiler_params=pltpu.CompilerParams(dimension_semantics=("arbitrary",),
                                             vmem_limit_bytes=VMEM_LIMIT),
        name="out_projection",
    )(x2, *mixer_outs, *([proj] * len(gate_specs)), jnp.asarray(expand, BF16),
      w_out.astype(BF16), g_post.reshape(1, D_MODEL))


def _layer(x2, tabs, B, S, layer, w_rows, w_t, w_out, g_pre, g_post, mla_g_q, mla_g_kv,
           mla_w_uq, mla_w_ukv, nsa_pos_k, nsa_pos_v, nsa_k_w1, nsa_k_w2, nsa_v_w1, nsa_v_w2):
    rows = B * S
    aqkv, aqkv4, aqkv16, proj, vct, vdt, qct, qdt = _in_projection(
        x2, g_pre, w_rows, w_t, layer, tabs, B, S)
    proj3 = proj.reshape(B, S, N_PBLK * BLK)

    o_a = _dilated(aqkv, aqkv4, aqkv16, B, S)

    qnt, qrt, kn, vbt = _mla_up(proj, mla_g_q, mla_g_kv, mla_w_uq, mla_w_ukv, tabs, B, S)
    o_b = _flash_t(qnt, kn.reshape(B, S, BLK), 0, BLK, vbt, B=B, S=S,
                   aux=("per_head", qrt, proj3, 2 * PB_BKV + 1), name="mla_attention")

    kc2, vc2 = _compress(proj3, nsa_pos_k, nsa_pos_v, nsa_k_w1, nsa_k_w2, nsa_v_w1, nsa_v_w2, B, S)
    o_cmp, nsel = _cmp_topk(proj3, kc2, vc2, B, S)
    penalty = np.zeros((S, LANES), np.float32)
    penalty[np.arange(S), np.arange(S) // NSA_SEL_LEN] = MASK_PENALTY
    o_slc = _flash_t(qct, proj3, 2 * PB_C0 + 1, LANES, vct, B=B, S=S, vrow0=0,
                     aux=("shared", nsel, jnp.asarray(penalty, BF16), 0), name="nsa_selected")
    o_win = _flash_t(qct, proj3, 2 * PB_C1, LANES, vct, B=B, S=S, vrow0=HEAD_DIM,
                     window=NSA_WINDOW - 1, name="nsa_window")

    o_d = _stick_breaking(qdt, proj3, vdt, B, S)

    flat = lambda t: t.reshape(rows, BLK)
    outs = [flat(o) for o in (o_a, o_b, o_cmp, o_slc, o_win, o_d)]
    return _out_projection(x2, outs, proj, w_out, g_post)


def kernel(x, positions, w_in, w_out, g_pre, g_post, mla_g_q, mla_g_kv, mla_w_uq, mla_w_ukv,
           nsa_pos_k, nsa_pos_v, nsa_k_w1, nsa_k_w2, nsa_v_w1, nsa_v_w2):
    B, S, D = x.shape
    assert D == D_MODEL and S % 1024 == 0
    x2 = x.reshape(B * S, D)
    tabs = _rope_tables(positions)
    w_rows, w_t = _pack_w_in(w_in)
    for l in range(w_in.shape[0]):
        x2 = _layer(x2, tabs, B, S, l, w_rows, w_t, w_out[l], g_pre[l], g_post[l],
                    mla_g_q[l], mla_g_kv[l], mla_w_uq[l], mla_w_ukv[l],
                    nsa_pos_k[l], nsa_pos_v[l], nsa_k_w1[l], nsa_k_w2[l],
                    nsa_v_w1[l], nsa_v_w2[l])
    return x2.reshape(B, S, D)
```

```python
import functools
import math

import numpy as np
import jax
import jax.numpy as jnp
from jax import lax
from jax.experimental import pallas as pl
from jax.experimental.pallas import tpu as pltpu

D_MODEL = 1024
HEAD_DIM = 64
N_HEADS = 4
ROPE_THETA = 10000.0
EPS = 1e-6
NEG = -1e30
LOG2E = math.log2(math.e)
MASK_PENALTY = -(2.0 ** 100)
BIG = 1e9

MLA_Q_RANK = 256
MLA_KV_RANK = 128
MLA_NOPE = 64
MLA_ROPE = 32
MLA_V = 64

NSA_CMP_LEN = 32
NSA_CMP_STRIDE = 16
NSA_CMP_HID = 256
NSA_SEL_LEN = 64
NSA_N_SEL = 16
NSA_WINDOW = 512

LANES = 128
BLK = 256
TILE = 512
CM_TILE = 256
DILATIONS = (1, 4, 16)
DIL_BLOCK = 128
LSE_PARTS = 2
TB = 256
ONES_ROWS = 16
VMEM_LIMIT = 48 * 1024 * 1024

F32 = jnp.float32
BF16 = jnp.bfloat16

(PB_AG, PB_BCQ, PB_BKV, PB_BG, PB_CQ, PB_C0, PB_C1, PB_CG, PB_DQ, PB_DK, PB_DG) = range(11)
N_PBLK = 11
N_BRANCH = 3
ROPE_NONE, ROPE_64, ROPE_32, ROPE_64_LO, ACT_SILU, ACT_SIGMOID = range(6)


def _dot_nt(a, b):
    return lax.dot_general(a, b, (((1,), (1,)), ((), ())), preferred_element_type=F32)


def _dot(a, b):
    return jnp.dot(a, b, preferred_element_type=F32)


def _rope_half(y, cos, sin_signed, group):
    half = group // 2
    lane = lax.broadcasted_iota(jnp.int32, y.shape, 1)
    first = (lane & (group - 1)) < half
    partner = jnp.where(first, pltpu.roll(y, LANES - half, 1), pltpu.roll(y, half, 1))
    return y * cos + partner * sin_signed


def _rope_table_kernel(pos_ref, inv_ref, cos_ref, sin_ref):
    ang = inv_ref[...] * pos_ref[...].astype(F32)
    cos_ref[...] = jnp.cos(ang)
    sin_ref[...] = jnp.sin(ang)


def _rope_tables(positions):
    rows = positions.size
    tm = 2048
    inv = np.concatenate([ROPE_THETA ** (-np.arange(n, dtype=np.float64) / n)
                          for n in (HEAD_DIM // 2, MLA_ROPE // 2)]).astype(np.float32)[:, None]
    out = jax.ShapeDtypeStruct((inv.shape[0], rows), F32)
    tab_spec = pl.BlockSpec((inv.shape[0], tm), lambda i: (0, i))
    return pl.pallas_call(
        _rope_table_kernel,
        grid=(rows // tm,),
        in_specs=[pl.BlockSpec((1, tm), lambda i: (0, i)),
                  pl.BlockSpec(inv.shape, lambda i: (0, 0))],
        out_specs=[tab_spec] * 2,
        out_shape=[out] * 2,
        name="rope_tables",
    )(positions.reshape(1, rows), jnp.asarray(inv))


def _token_major_rope(cos_ref, sin_ref, group):
    n = group // 2
    lo = 0 if group == HEAD_DIM else HEAD_DIM // 2
    c = cos_ref[lo:lo + n, :]
    s = sin_ref[lo:lo + n, :]
    reps = LANES // n
    cos = jnp.concatenate([c] * reps, axis=0).T
    sin = jnp.concatenate([-s if k % 2 == 0 else s for k in range(reps)], axis=0).T
    return cos, sin


def _inproj_kernel(x_ref, g_ref, w_ref, wt_ref, perm_ref, cos_ref, sin_ref,
                   aqkv_ref, aqkv4_ref, aqkv16_ref, proj_ref, vct_ref, vdt_ref, qct_ref, qdt_ref,
                   *, plan_a, plan_p):
    x = x_ref[...]
    tm = x.shape[0]
    ms = jnp.mean(x * x, axis=-1, keepdims=True)
    h = ((x * lax.rsqrt(ms + EPS)) * g_ref[...]).astype(BF16)
    lane = lax.broadcasted_iota(jnp.int32, (tm, LANES), 1)
    c64, s64 = _token_major_rope(cos_ref, sin_ref, HEAD_DIM)
    c32, s32 = _token_major_rope(cos_ref, sin_ref, MLA_ROPE)

    def epilogue(y, ops, scale):
        halves = []
        for hf, op in enumerate(ops):
            yh = y[:, hf * LANES:(hf + 1) * LANES]
            if op == ROPE_64:
                yh = _rope_half(yh, c64, s64, 64)
            elif op == ROPE_32:
                yh = _rope_half(yh, c32, s32, 32)
            elif op == ROPE_64_LO:
                yh = jnp.where(lane < 64, _rope_half(yh, c64, s64, 64), yh)
            elif op == ACT_SILU:
                yh = yh * (1.0 / (1.0 + jnp.exp(-yh)))
            elif op == ACT_SIGMOID:
                yh = 1.0 / (1.0 + jnp.exp(-yh))
            if scale != 1.0:
                yh = yh * scale
            halves.append(yh)
        return jnp.concatenate(halves, axis=1)

    feature_major = {PB_CQ: qct_ref, PB_DQ: qdt_ref}
    col = 0
    for out_ref, plan in ((aqkv_ref, plan_a), (proj_ref, plan_p)):
        for b, (ops, scale) in enumerate(plan):
            y = epilogue(_dot(h, w_ref[:, col:col + BLK]), ops, scale)
            out_ref[:, b * BLK:(b + 1) * BLK] = y.astype(BF16)
            if out_ref is proj_ref and b in feature_major:
                y_t = y.T
                for blk in range(tm // TB):
                    feature_major[b][blk] = y_t[:, blk * TB:(blk + 1) * TB].astype(BF16)
            col += BLK

    for t in range(tm // CM_TILE):
        tile = slice(CM_TILE * t, CM_TILE * (t + 1))
        a_rows = aqkv_ref[tile, :]
        aqkv4_ref[tile, :] = _dot(perm_ref[0], a_rows).astype(BF16)
        aqkv16_ref[tile, :] = _dot(perm_ref[1], a_rows).astype(BF16)

    vt = _dot_nt(wt_ref[...], h)
    for blk in range(tm // TB):
        cols = slice(blk * TB, (blk + 1) * TB)
        vct_ref[blk] = vt[0:LANES, cols].astype(BF16)
        vdt_ref[blk] = vt[LANES:LANES + BLK, cols].astype(BF16)


_PLAN_A = (((ROPE_64, ROPE_64), LOG2E * HEAD_DIM ** -0.5),
           ((ROPE_64, ROPE_64), 1.0),
           ((ROPE_NONE, ROPE_NONE), 1.0))
_PLAN_P = (
    ((ACT_SILU, ACT_SILU), 1.0),
    ((ROPE_NONE, ROPE_NONE), 1.0),
    ((ROPE_NONE, ROPE_32), 1.0),
    ((ACT_SILU, ACT_SILU), 1.0),
    ((ROPE_64, ROPE_64), LOG2E * HEAD_DIM ** -0.5),
    ((ROPE_64_LO, ROPE_64), 1.0),
    ((ROPE_64, ACT_SIGMOID), 1.0),
    ((ACT_SILU, ACT_SILU), 1.0),
    ((ROPE_NONE, ROPE_NONE), LOG2E * HEAD_DIM ** -0.5),
    ((ROPE_NONE, ROPE_NONE), 1.0),
    ((ACT_SILU, ACT_SILU), 1.0),
)
assert len(_PLAN_P) == N_PBLK


_W_IN_SEGMENTS = (("aq", 256), ("ak", 256), ("av", 256), ("ag", 256),
                  ("bcq", 256), ("bckv", 128), ("bkr", 32), ("bg", 256),
                  ("cq", 256), ("ckc", 64), ("cvc", 64), ("cks", 64), ("cvs", 64),
                  ("ckw", 64), ("cvw", 64), ("cgl", 12), ("cg", 256),
                  ("dq", 256), ("dk", 256), ("dv", 256), ("dg", 256))
W_ROWS_COLS = (3 + N_PBLK) * BLK
W_T_ROWS = LANES + BLK


def _pack_w_kernel(wt_ref, rows_ref, t_ref):
    wt = wt_ref[...].astype(F32)
    o = 0
    seg = {}
    for name, width in _W_IN_SEGMENTS:
        seg[name] = wt[o:o + width, :]
        o += width
    gl_pad = jnp.zeros((LANES - N_HEADS * N_BRANCH, wt.shape[1]), wt.dtype)
    rows = [seg["aq"], seg["ak"], seg["av"],
            seg["ag"], seg["bcq"],
            seg["bckv"], seg["bkr"], seg["bkr"], seg["bkr"], seg["bkr"],
            seg["bg"], seg["cq"],
            seg["ckc"], seg["cvc"], seg["cks"], seg["cks"],
            seg["ckw"], seg["ckw"], seg["cgl"], gl_pad,
            seg["cg"], seg["dq"], seg["dk"], seg["dg"]]
    rows_ref[...] = jnp.concatenate(rows, axis=0).T.astype(BF16)
    t_ref[...] = jnp.concatenate([seg["cvs"], seg["cvw"], seg["dv"]], axis=0).astype(BF16)


def _pack_w_in(w_in):
    depth, d_model, d_in = w_in.shape
    assert d_model == D_MODEL and d_in == sum(width for _, width in _W_IN_SEGMENTS)
    tc = 256
    return pl.pallas_call(
        _pack_w_kernel,
        grid=(depth, D_MODEL // tc),
        in_specs=[pl.BlockSpec((None, d_in, tc), lambda l, i: (l, 0, i))],
        out_specs=[pl.BlockSpec((None, tc, W_ROWS_COLS), lambda l, i: (l, i, 0)),
                   pl.BlockSpec((None, W_T_ROWS, tc), lambda l, i: (l, 0, i))],
        out_shape=[jax.ShapeDtypeStruct((depth, D_MODEL, W_ROWS_COLS), BF16),
                   jax.ShapeDtypeStruct((depth, W_T_ROWS, D_MODEL), BF16)],
        compiler_params=pltpu.CompilerParams(dimension_semantics=("arbitrary", "arbitrary"),
                                             vmem_limit_bytes=VMEM_LIMIT),
        name="pack_w_in",
    )(jnp.transpose(w_in, (0, 2, 1)).astype(BF16))


def _class_major_perm(dil, tile):
    n = tile // dil
    p = np.zeros((tile, tile), np.float32)
    pos = np.arange(tile)
    p[(pos % dil) * n + pos // dil, pos] = 1.0
    return p


def _in_projection(x2, g_pre, w_rows, w_t, layer, tabs, B, S):
    rows = x2.shape[0]
    tm = TILE
    per_b = S // tm
    nkb = tm // TB
    perm = jnp.asarray(np.stack([_class_major_perm(d, CM_TILE) for d in DILATIONS[1:]]), BF16)
    tab_spec = pl.BlockSpec((tabs[0].shape[0], tm), lambda i: (0, i))
    vt_spec = lambda r: pl.BlockSpec((None, nkb, r, TB), lambda i: (i // per_b, i % per_b, 0, 0))
    a_spec = pl.BlockSpec((tm, 3 * BLK), lambda i: (i, 0))
    a_shape = jax.ShapeDtypeStruct((rows, 3 * BLK), BF16)
    return pl.pallas_call(
        functools.partial(_inproj_kernel, plan_a=_PLAN_A, plan_p=_PLAN_P),
        grid=(rows // tm,),
        in_specs=[pl.BlockSpec((tm, D_MODEL), lambda i: (i, 0)),
                  pl.BlockSpec((1, D_MODEL), lambda i: (0, 0)),
                  pl.BlockSpec((None,) + w_rows.shape[1:], lambda i: (layer, 0, 0)),
                  pl.BlockSpec((None,) + w_t.shape[1:], lambda i: (layer, 0, 0)),
                  pl.BlockSpec(perm.shape, lambda i: (0, 0, 0))] + [tab_spec] * 2,
        out_specs=[a_spec, a_spec, a_spec,
                   pl.BlockSpec((tm, N_PBLK * BLK), lambda i: (i, 0)),
                   vt_spec(LANES), vt_spec(BLK), vt_spec(BLK), vt_spec(BLK)],
        out_shape=[a_shape, a_shape, a_shape,
                   jax.ShapeDtypeStruct((rows, N_PBLK * BLK), BF16),
                   jax.ShapeDtypeStruct((B, S // TB, LANES, TB), BF16)]
                  + [jax.ShapeDtypeStruct((B, S // TB, BLK, TB), BF16)] * 3,
        compiler_params=pltpu.CompilerParams(dimension_semantics=("arbitrary",),
                                             vmem_limit_bytes=VMEM_LIMIT),
        name="in_projection",
    )(x2, g_pre.reshape(1, D_MODEL), w_rows, w_t, perm, *tabs)


def _head_masks(shape):
    lane = lax.broadcasted_iota(jnp.int32, shape, 1)
    return lane < HEAD_DIM


def _masked_heads(q_ref, lo_mask):
    out = []
    for h in range(N_HEADS):
        pair, e = divmod(h, 2)
        qp = q_ref[:, pair * LANES:(pair + 1) * LANES]
        keep = lo_mask if e == 0 else jnp.logical_not(lo_mask)
        out.append(jnp.where(keep, qp, jnp.zeros_like(qp)))
    return out


def _dilated_kernel(an_ref, a4_ref, a16_ref, unperm_ref, expand_ref, o_ref, res_ref, s_ref):
    T = DIL_BLOCK
    S = an_ref.shape[0]
    n_tiles = S // CM_TILE
    lo = _head_masks((T, LANES))
    lane = lax.broadcasted_iota(jnp.int32, (T, LANES), 1)
    dist2 = (lax.broadcasted_iota(jnp.int32, (T, 2 * T), 0)
             - lax.broadcasted_iota(jnp.int32, (T, 2 * T), 1))
    causal1 = (lax.broadcasted_iota(jnp.int32, (T, T), 1)
               <= lax.broadcasted_iota(jnp.int32, (T, T), 0))

    def scores(q, k):
        qm = _masked_heads(q, lo)
        return tuple(_dot_nt(qm[h], k[:, (h // 2) * LANES:(h // 2 + 1) * LANES])
                     for h in range(N_HEADS))

    def finish(n_keys, v, mask, stage_next):
        probs = []
        for h in range(N_HEADS):
            s = jnp.where(mask, s_ref[h, :, 0:n_keys], NEG)
            m = jnp.max(s, axis=-1, keepdims=True)
            p = jnp.exp2(s - m)
            probs.append((m, jnp.sum(p, axis=-1, keepdims=True), p.astype(BF16)))
        stage_next()
        outs = []
        top = jnp.zeros((T, LANES), F32)
        den = jnp.ones((T, LANES), F32)
        for h in range(N_HEADS):
            m, l, p = probs[h]
            outs.append(_dot(p, v[:, (h // 2) * LANES:(h // 2 + 1) * LANES]) * (1.0 / l))
            mine = (lane & (N_HEADS - 1)) == h
            top = jnp.where(mine, m, top)
            den = jnp.where(mine, l, den)
        lse = top + jnp.log2(den)
        hi = lse.astype(BF16).astype(F32)
        parts = jnp.where(lane < N_HEADS, hi,
                          jnp.where(lane < LSE_PARTS * N_HEADS, lse - hi, 0.0))
        return jnp.concatenate(
            [jnp.where(lo, outs[0], outs[1]), jnp.where(lo, outs[2], outs[3]), parts],
            axis=1).astype(BF16)

    def band_mask(delta):
        d = dist2 + delta
        return jnp.logical_and(d >= 0, d <= DIL_BLOCK)

    def run(n_units, n_keys, load_qk, load_v, store, mask_of):
        def stage(u):
            for h, s in enumerate(scores(*load_qk(u))):
                s_ref[h, :, 0:n_keys] = s

        def body(u, carry):
            nxt = jnp.minimum(u + 1, n_units - 1)
            store(u, finish(n_keys, load_v(u), mask_of(u), lambda: stage(nxt)))
            return carry

        stage(0)
        lax.fori_loop(0, n_units, body, 0)

    def rows(start, size):
        return pl.ds(pl.multiple_of(start, 16), size)

    def k0_1(u):
        return T * jnp.maximum(u - 1, 0)

    def store1(u, r):
        res_ref[0, rows(T * u, T), :] = r

    run(S // T, 2 * T,
        lambda u: (an_ref[rows(T * u, T), 0:BLK], an_ref[rows(k0_1(u), 2 * T), BLK:2 * BLK]),
        lambda u: an_ref[rows(k0_1(u), 2 * T), 2 * BLK:3 * BLK],
        store1,
        lambda u: band_mask(T * u - k0_1(u)))

    per4 = CM_TILE // 4
    nq4 = S // 4 // T

    def pieces4(u, blk):
        first_tile = blk * (T // per4)
        return [rows(CM_TILE * (first_tile + k) + per4 * (u // nq4), per4)
                for k in range(T // per4)]

    def kb0_4(u):
        return jnp.maximum(u % nq4 - 1, 0)

    def load4(u, blks, cols):
        return jnp.concatenate([a4_ref[p, cols] for blk in blks for p in pieces4(u, blk)], axis=0)

    def store4(u, r):
        for k, p in enumerate(pieces4(u, u % nq4)):
            res_ref[1, p, :] = r[per4 * k:per4 * (k + 1), :]

    run(4 * nq4, 2 * T,
        lambda u: (load4(u, [u % nq4], slice(0, BLK)),
                   load4(u, [kb0_4(u), kb0_4(u) + 1], slice(BLK, 2 * BLK))),
        lambda u: load4(u, [kb0_4(u), kb0_4(u) + 1], slice(2 * BLK, 3 * BLK)),
        store4,
        lambda u: band_mask(T * (u % nq4 - kb0_4(u))))

    per16 = CM_TILE // 16

    def load16(c, cols):
        return jnp.concatenate([a16_ref[rows(CM_TILE * t + per16 * c, per16), cols]
                                for t in range(n_tiles)], axis=0)

    def store16(c, r):
        for t in range(n_tiles):
            res_ref[2, rows(CM_TILE * t + per16 * c, per16), :] = r[per16 * t:per16 * (t + 1), :]

    run(16, T,
        lambda c: (load16(c, slice(0, BLK)), load16(c, slice(BLK, 2 * BLK))),
        lambda c: load16(c, slice(2 * BLK, 3 * BLK)),
        store16,
        lambda c: causal1)

    expand = expand_ref[...]
    for t in range(n_tiles):
        tile = slice(CM_TILE * t, CM_TILE * (t + 1))
        nat = [res_ref[0, tile, :].astype(F32),
               _dot(unperm_ref[0], res_ref[1, tile, :]),
               _dot(unperm_ref[1], res_ref[2, tile, :])]
        lse = [_dot(r[:, BLK:].astype(BF16), expand) for r in nat]
        top = jnp.maximum(lse[0], jnp.maximum(lse[1], lse[2]))
        w = [jnp.exp2(l - top) for l in lse]
        mixed = (w[0] * nat[0][:, :BLK] + w[1] * nat[1][:, :BLK] + w[2] * nat[2][:, :BLK])
        o_ref[tile, :] = (mixed / (w[0] + w[1] + w[2])).astype(BF16)


def _dilated(aqkv, aqkv4, aqkv16, B, S):
    assert S == 16 * DIL_BLOCK and S % CM_TILE == 0 and DIL_BLOCK % (CM_TILE // 4) == 0
    r3 = lambda t: t.reshape(B, S, 3 * BLK)
    unperm = jnp.asarray(np.stack([_class_major_perm(d, CM_TILE).T for d in DILATIONS[1:]]), BF16)
    expand = np.zeros((LANES, BLK), np.float32)
    for h in range(N_HEADS):
        for part in range(LSE_PARTS):
            expand[N_HEADS * part + h, h * HEAD_DIM:(h + 1) * HEAD_DIM] = 1.0
    a_spec = pl.BlockSpec((None, S, 3 * BLK), lambda b: (b, 0, 0))
    return pl.pallas_call(
        _dilated_kernel,
        grid=(B,),
        in_specs=[a_spec, a_spec, a_spec,
                  pl.BlockSpec(unperm.shape, lambda b: (0, 0, 0)),
                  pl.BlockSpec(expand.shape, lambda b: (0, 0))],
        out_specs=pl.BlockSpec((None, S, BLK), lambda b: (b, 0, 0)),
        out_shape=jax.ShapeDtypeStruct((B, S, BLK), BF16),
        scratch_shapes=[pltpu.VMEM((3, S, BLK + LANES), BF16),
                        pltpu.VMEM((N_HEADS, DIL_BLOCK, 2 * DIL_BLOCK), F32)],
        compiler_params=pltpu.CompilerParams(dimension_semantics=("arbitrary",),
                                             vmem_limit_bytes=VMEM_LIMIT),
        name="dilated_mixture",
    )(r3(aqkv), r3(aqkv4), r3(aqkv16), unperm, jnp.asarray(expand, BF16))


def _masked_head_rows(qt_ref):
    out = []
    zeros = jnp.zeros((HEAD_DIM, qt_ref.shape[-1]), BF16)
    for h in range(N_HEADS):
        own = qt_ref[h * HEAD_DIM:(h + 1) * HEAD_DIM, :]
        out.append(jnp.concatenate([own, zeros] if h % 2 == 0 else [zeros, own], axis=0))
    return out


def _flash_t_kernel(*refs, window, kshared, vrow0, aux):
    it = iter(refs)
    q_all_ref = next(it)
    qx_all_ref = next(it) if aux else None
    k_ref = next(it)
    kx_ref = next(it) if aux else None
    vt_ref = next(it)
    o_ref = next(it)
    s_ref, m_ref, acc_ref = next(it), next(it), next(it)
    T = TB

    krow = lax.broadcasted_iota(jnp.int32, (T, T), 0)
    qcol = lax.broadcasted_iota(jnp.int32, (T, T), 1)
    ones = jnp.ones((ONES_ROWS, T), BF16)

    def query_block(i, carry):
        qcat = _masked_head_rows(q_all_ref.at[i])
        if aux == "per_head":
            group = LANES // N_HEADS
            qcat = [jnp.concatenate(
                [qcat[h]] + [qx_all_ref[i, g * group:(g + 1) * group, :] if g == h
                             else jnp.zeros((group, T), BF16) for g in range(N_HEADS)], axis=0)
                for h in range(N_HEADS)]
        elif aux == "shared":
            qx = qx_all_ref[i]
            qcat = [jnp.concatenate([qcat[h], qx], axis=0) for h in range(N_HEADS)]

        def scores(j):
            rows = pl.ds(pl.multiple_of(j * T, T), T)
            out = []
            for h in range(N_HEADS):
                pair = h // 2
                cols = slice(0, LANES) if kshared else slice(pair * LANES, (pair + 1) * LANES)
                kb = k_ref[rows, cols]
                if aux:
                    kb = jnp.concatenate([kb, kx_ref[rows, :]], axis=1)
                out.append(_dot(kb, qcat[h]))
            return out

        def stage(j):
            for h, s in enumerate(scores(j)):
                s_ref[h] = s

        def absorb(j, mask, staged_next=None, s_all=None):
            vt_all = vt_ref[j]
            probs = []
            for h in range(N_HEADS):
                s = s_ref[h] if s_all is None else s_all[h]
                if mask is not None:
                    s = jnp.where(mask, s, NEG)
                m = m_ref[h]
                m_new = jnp.maximum(m, jnp.max(s, axis=0, keepdims=True))
                alpha = jnp.exp2(m - m_new)
                p = jnp.exp2(s - m_new)
                probs.append((m_new, alpha, p.astype(BF16)))
            if staged_next is not None:
                stage(staged_next)
            for h in range(N_HEADS):
                m_new, alpha, p = probs[h]
                r0 = vrow0 if kshared else h * HEAD_DIM
                lhs = jnp.concatenate([vt_all[r0:r0 + HEAD_DIM, :], ones], axis=0)
                acc_ref[h] = alpha * acc_ref[h] + _dot(lhs, p)
                m_ref[h] = m_new

        m_ref[...] = jnp.full(m_ref.shape, NEG, F32)
        acc_ref[...] = jnp.zeros(acc_ref.shape, F32)
        if window is None:
            first = 0
            stage(first)
        else:
            o_full = (window - T + 1) // T
            o_none = (window + T - 1) // T + 1
            assert o_none - o_full == 2
            o = o_full + 1
            j = i - o
            jc = jnp.maximum(j, 0)
            first = jnp.maximum(i - o_full, 0)
            s_band = scores(jc)
            stage(first)
            limit = jnp.where(j >= 0, window - o * T, -2 * T)
            absorb(jc, (qcol - krow) <= limit, s_all=s_band)

        def body(j, c):
            absorb(j, None, staged_next=j + 1)
            return c

        lax.fori_loop(first, i, body, 0)
        absorb(i, krow <= qcol)

        q_rows = pl.ds(pl.multiple_of(i * T, T), T)
        for pair in range(2):
            outs = []
            for e in range(2):
                acc = acc_ref[2 * pair + e]
                outs.append(acc[0:HEAD_DIM, :] * (1.0 / acc[HEAD_DIM:HEAD_DIM + 1, :]))
            o_ref[q_rows, pair * LANES:(pair + 1) * LANES] = (
                jnp.concatenate(outs, axis=0).T.astype(BF16))
        return carry

    lax.fori_loop(0, q_all_ref.shape[0], query_block, 0)


def _flash_t(qt, k, kcol, kw, vt, *, B, S, window=None, vrow0=0, aux=None, name):
    kshared = kw == LANES
    whole = lambda a: pl.BlockSpec((None,) + a.shape[1:], lambda b: (b,) + (0,) * (a.ndim - 1))
    in_specs = [whole(qt)]
    args = [qt]
    if aux is not None:
        kind, qx, kx, kxcol = aux
        in_specs.append(whole(qx))
        args.append(qx)
    in_specs.append(pl.BlockSpec((None, S, kw), lambda b: (b, 0, kcol)))
    args.append(k)
    if aux is not None:
        if kx.ndim == 3:
            in_specs.append(pl.BlockSpec((None, S, LANES), lambda b: (b, 0, kxcol)))
        else:
            in_specs.append(pl.BlockSpec((S, LANES), lambda b: (0, kxcol)))
        args.append(kx)
    in_specs.append(whole(vt))
    args.append(vt)
    return pl.pallas_call(
        functools.partial(_flash_t_kernel, window=window, kshared=kshared, vrow0=vrow0,
                          aux=None if aux is None else aux[0]),
        grid=(B,),
        in_specs=in_specs,
        out_specs=pl.BlockSpec((None, S, BLK), lambda b: (b, 0, 0)),
        out_shape=jax.ShapeDtypeStruct((B, S, BLK), BF16),
        scratch_shapes=[pltpu.VMEM((N_HEADS, TB, TB), F32),
                        pltpu.VMEM((N_HEADS, 1, TB), F32),
                        pltpu.VMEM((N_HEADS, HEAD_DIM + ONES_ROWS, TB), F32)],
        compiler_params=pltpu.CompilerParams(dimension_semantics=("arbitrary",),
                                             vmem_limit_bytes=VMEM_LIMIT),
        name=name,
    )(*args)


def _mla_up_kernel(cq_ref, ckv_ref, gq_ref, gkv_ref, wq_ref, wk_ref, wvt_ref, cos_ref, sin_ref,
                   qn_ref, qr_ref, kn_ref, vt_ref):
    def norm(t, g):
        t = t.astype(F32)
        ms = jnp.mean(t * t, axis=-1, keepdims=True)
        return ((t * lax.rsqrt(ms + EPS)) * g).astype(BF16)

    scale = LOG2E * (MLA_NOPE + MLA_ROPE) ** -0.5
    q_t = _dot_nt(wq_ref[...], norm(cq_ref[...], gq_ref[...])) * scale
    qn_t = q_t[:BLK, :]
    half = MLA_ROPE // 2
    cos = cos_ref[HEAD_DIM // 2:HEAD_DIM // 2 + half, :]
    sin = sin_ref[HEAD_DIM // 2:HEAD_DIM // 2 + half, :]
    rot = []
    for h in range(N_HEADS):
        x1 = q_t[BLK + h * MLA_ROPE:BLK + h * MLA_ROPE + half, :]
        x2 = q_t[BLK + h * MLA_ROPE + half:BLK + (h + 1) * MLA_ROPE, :]
        rot += [x1 * cos - x2 * sin, x1 * sin + x2 * cos]
    qr_t = jnp.concatenate(rot, axis=0)
    ckv = norm(ckv_ref[...], gkv_ref[...])
    kn_ref[...] = _dot(ckv, wk_ref[...]).astype(BF16)
    vt = _dot_nt(wvt_ref[...], ckv)
    for blk in range(vt.shape[1] // TB):
        cols = slice(blk * TB, (blk + 1) * TB)
        qn_ref[blk] = qn_t[:, cols].astype(BF16)
        qr_ref[blk] = qr_t[:, cols].astype(BF16)
        vt_ref[blk] = vt[:, cols].astype(BF16)


def _mla_up(proj, g_q, g_kv, w_uq, w_ukv, tabs, B, S):
    rows = proj.shape[0]
    tm = 1024
    per_b = S // tm
    wq = w_uq.reshape(MLA_Q_RANK, N_HEADS, MLA_NOPE + MLA_ROPE)
    wq = jnp.concatenate([wq[:, :, :MLA_NOPE].reshape(MLA_Q_RANK, -1),
                          wq[:, :, MLA_NOPE:].reshape(MLA_Q_RANK, -1)], axis=1).T.astype(BF16)
    wkv = w_ukv.reshape(MLA_KV_RANK, N_HEADS, MLA_NOPE + MLA_V)
    wk = wkv[:, :, :MLA_NOPE].reshape(MLA_KV_RANK, -1).astype(BF16)
    wvt = wkv[:, :, MLA_NOPE:].reshape(MLA_KV_RANK, -1).T.astype(BF16)
    row_spec = lambda w, c: pl.BlockSpec((tm, w), lambda i: (i, c))
    full = lambda a: pl.BlockSpec(a.shape, lambda i: (0, 0))
    gq = g_q.reshape(1, -1)
    gkv = g_kv.reshape(1, -1)
    fm_shape = lambda r: jax.ShapeDtypeStruct((B, S // TB, r, TB), BF16)
    fm_spec = lambda r: pl.BlockSpec((None, tm // TB, r, TB),
                                     lambda i: (i // per_b, i % per_b, 0, 0))
    outs = [fm_shape(BLK), fm_shape(LANES), jax.ShapeDtypeStruct((rows, BLK), BF16), fm_shape(BLK)]
    return pl.pallas_call(
        _mla_up_kernel,
        grid=(rows // tm,),
        in_specs=[row_spec(BLK, PB_BCQ), row_spec(LANES, 2 * PB_BKV), full(gq), full(gkv),
                  full(wq), full(wk), full(wvt)]
                 + [pl.BlockSpec((tabs[0].shape[0], tm), lambda i: (0, i))] * 2,
        out_specs=[fm_spec(BLK), fm_spec(LANES), row_spec(BLK, 0), fm_spec(BLK)],
        out_shape=outs,
        compiler_params=pltpu.CompilerParams(dimension_semantics=("arbitrary",),
                                             vmem_limit_bytes=VMEM_LIMIT),
        name="mla_up",
    )(proj, proj, gq, gkv, wq, wk, wvt, *tabs)


def _compress_kernel(x_ref, perm_ref, pos_ref, w1_ref, w2_ref, kc_ref, vc_ref):
    n_tiles = x_ref.shape[0] // TILE
    per = TILE // NSA_CMP_STRIDE
    slabs = []
    for t in range(n_tiles):
        cm = _dot(perm_ref[...], x_ref[TILE * t:TILE * (t + 1), :])
        slabs.append(jnp.concatenate([cm[per * c:per * (c + 1), :]
                                      for c in range(NSA_CMP_STRIDE)], axis=1))
    x = jnp.concatenate(slabs, axis=0)
    half = x.shape[1]
    top = (x + pos_ref[0:1, :]).astype(BF16)
    bot = (x + pos_ref[1:2, :]).astype(BF16)
    y_top = _dot(top, w1_ref[0:half, :])
    y_bot = _dot(bot, w1_ref[half:2 * half, :])
    pre = y_top + pltpu.roll(y_bot, y_bot.shape[0] - 1, 0)
    hid = (pre * (1.0 / (1.0 + jnp.exp(-pre)))).astype(BF16)
    kc_ref[...] = _dot(hid[:, :NSA_CMP_HID], w2_ref[0]).astype(BF16)
    vc_ref[...] = _dot(hid[:, NSA_CMP_HID:], w2_ref[1]).astype(BF16)


def _compress(proj3, pos_k, pos_v, kw1, kw2, vw1, vw2, B, S):
    assert S % TILE == 0 and NSA_CMP_LEN == 2 * NSA_CMP_STRIDE
    n16 = S // NSA_CMP_STRIDE
    perm = jnp.asarray(_class_major_perm(NSA_CMP_STRIDE, TILE), BF16)
    pos = jnp.concatenate([pos_k, pos_v], axis=1).reshape(2, NSA_CMP_STRIDE * LANES)
    w1 = jnp.zeros((NSA_CMP_LEN, LANES, 2 * NSA_CMP_HID), F32)
    w1 = w1.at[:, :HEAD_DIM, :NSA_CMP_HID].set(kw1.reshape(NSA_CMP_LEN, HEAD_DIM, NSA_CMP_HID))
    w1 = w1.at[:, HEAD_DIM:, NSA_CMP_HID:].set(vw1.reshape(NSA_CMP_LEN, HEAD_DIM, NSA_CMP_HID))
    w1 = w1.reshape(NSA_CMP_LEN * LANES, 2 * NSA_CMP_HID).astype(BF16)
    w2 = jnp.stack([jnp.tile(kw2, (1, 2)), jnp.tile(vw2, (1, 2))]).astype(BF16)
    out = jax.ShapeDtypeStruct((B, n16, LANES), BF16)
    return pl.pallas_call(
        _compress_kernel,
        grid=(B,),
        in_specs=[pl.BlockSpec((None, S, LANES), lambda b: (b, 0, 2 * PB_C0)),
                  pl.BlockSpec(perm.shape, lambda b: (0, 0)),
                  pl.BlockSpec(pos.shape, lambda b: (0, 0)),
                  pl.BlockSpec(w1.shape, lambda b: (0, 0)),
                  pl.BlockSpec(w2.shape, lambda b: (0, 0, 0))],
        out_specs=[pl.BlockSpec((None, n16, LANES), lambda b: (b, 0, 0))] * 2,
        out_shape=[out, out],
        compiler_params=pltpu.CompilerParams(dimension_semantics=("arbitrary",),
                                             vmem_limit_bytes=VMEM_LIMIT),
        name="nsa_compress",
    )(proj3, perm, pos, w1, w2)


def _cmp_topk_kernel(q_ref, kc_ref, vc_ref, ovt_ref, o_ref, nsel_ref, *, T, n_cmp, n_slc):
    i = pl.program_id(1)
    t0 = i * T
    lane = lax.broadcasted_iota(jnp.int32, (T, LANES), 1)
    tok = t0 + lax.broadcasted_iota(jnp.int32, (T, LANES), 0)
    ok = jnp.logical_and(lane * NSA_CMP_STRIDE + (NSA_CMP_LEN - 1) <= tok, lane < n_cmp)
    lo = lane < HEAD_DIM
    qm = _masked_heads(q_ref, lo)
    kc = kc_ref[...]
    vc = vc_ref[...]
    psum = jnp.zeros((T, LANES), F32)
    outs = []
    for h in range(N_HEADS):
        s = jnp.where(ok, _dot_nt(qm[h], kc), NEG)
        mx = jnp.max(s, axis=-1, keepdims=True)
        e = jnp.where(ok, jnp.exp2(s - mx), 0.0)
        den = jnp.sum(e, axis=-1, keepdims=True)
        p = e * (1.0 / jnp.maximum(den, 1e-30))
        psum = psum + p
        outs.append(_dot(p.astype(BF16), vc))
    for pair in range(2):
        o_ref[:, pair * LANES:(pair + 1) * LANES] = jnp.where(
            lo, outs[2 * pair], outs[2 * pair + 1]).astype(BF16)

    p_hi = psum.astype(BF16)
    p_lo = (psum - p_hi.astype(F32)).astype(BF16)
    ovt = ovt_ref[...]
    imp = _dot_nt(ovt, p_hi) + _dot_nt(ovt, p_lo)
    jj = lax.broadcasted_iota(jnp.int32, (n_slc, T), 0)
    bt = (t0 + lax.broadcasted_iota(jnp.int32, (n_slc, T), 1)) >> 6
    forced = jnp.logical_or(jj == 0, jnp.logical_or(jj == bt, jj == bt - 1))
    val = jnp.where(forced, BIG, jnp.where(jj > bt, -BIG, imp))
    sel_rows = []
    for j in range(n_slc):
        vj = val[j:j + 1, :]
        beats = jnp.logical_or(val > vj, jnp.logical_and(val == vj, jj < j))
        rank = jnp.sum(beats.astype(F32), axis=0, keepdims=True)
        sel_rows.append((rank >= NSA_N_SEL).astype(F32))
    sel_rows.append(jnp.zeros((LANES - n_slc, T), F32))
    nsel = jnp.concatenate(sel_rows, axis=0).astype(BF16)
    for blk in range(T // TB):
        nsel_ref[blk] = nsel[:, blk * TB:(blk + 1) * TB]


def _cmp_topk(proj3, kc2, vc2, B, S):
    T = 2 * TB
    n_cmp = (S - NSA_CMP_LEN) // NSA_CMP_STRIDE + 1
    n_slc = S // NSA_SEL_LEN
    assert NSA_SEL_LEN == 64 and n_slc <= LANES and n_cmp <= LANES
    ci = np.arange(LANES)[None, :] * NSA_CMP_STRIDE
    sj = np.arange(n_slc)[:, None] * NSA_SEL_LEN
    ovt = ((ci < sj + NSA_SEL_LEN) & (ci + NSA_CMP_LEN > sj) & (np.arange(LANES)[None, :] < n_cmp))
    ovt = jnp.asarray(ovt.astype(np.float32), BF16)
    return pl.pallas_call(
        functools.partial(_cmp_topk_kernel, T=T, n_cmp=n_cmp, n_slc=n_slc),
        grid=(B, S // T),
        in_specs=[pl.BlockSpec((None, T, BLK), lambda b, i: (b, i, PB_CQ)),
                  pl.BlockSpec((None, LANES, LANES), lambda b, i: (b, 0, 0)),
                  pl.BlockSpec((None, LANES, LANES), lambda b, i: (b, 0, 0)),
                  pl.BlockSpec(ovt.shape, lambda b, i: (0, 0))],
        out_specs=[pl.BlockSpec((None, T, BLK), lambda b, i: (b, i, 0)),
                   pl.BlockSpec((None, T // TB, LANES, TB), lambda b, i: (b, i, 0, 0))],
        out_shape=[jax.ShapeDtypeStruct((B, S, BLK), BF16),
                   jax.ShapeDtypeStruct((B, S // TB, LANES, TB), BF16)],
        compiler_params=pltpu.CompilerParams(dimension_semantics=("arbitrary", "arbitrary"),
                                             vmem_limit_bytes=VMEM_LIMIT),
        name="nsa_cmp_topk",
    )(proj3, kc2, vc2, ovt)


def _stick_kernel(q_all_ref, k_ref, vt_ref, u_ref, o_ref, za_ref, zb_ref, w_ref, carry_ref,
                  acc_ref):
    T = TB
    krow = lax.broadcasted_iota(jnp.int32, (T, T), 0)
    qcol = lax.broadcasted_iota(jnp.int32, (T, T), 1)
    strict = krow < qcol
    u = u_ref[...]
    z_refs = (za_ref, zb_ref)

    def query_block(i, carry):
        qm = _masked_head_rows(q_all_ref.at[i])

        def stage(j, slot):
            rows = pl.ds(pl.multiple_of(j * T, T), T)
            for h in range(N_HEADS):
                z_refs[slot][h] = _dot(k_ref[rows, (h // 2) * LANES:(h // 2 + 1) * LANES],
                                       qm[h])

        def apply_weights(j):
            vt_all = vt_ref[j]
            for h in range(N_HEADS):
                acc_ref[h] = acc_ref[h] + _dot(vt_all[h * HEAD_DIM:(h + 1) * HEAD_DIM, :], w_ref[h])

        def absorb(j, slot, diag, staged_next, applied_prev):
            z_ref = z_refs[slot]
            softplus = []
            for h in range(N_HEADS):
                z = z_ref[h]
                sp = jnp.maximum(z, 0.0) + jnp.log2(1.0 + jnp.exp2(-jnp.abs(z)))
                if diag:
                    sp = jnp.where(strict, sp, 0.0)
                softplus.append(sp.astype(BF16))
            if staged_next is not None:
                stage(staged_next, 1 - slot)
            if applied_prev is not None:
                apply_weights(applied_prev)
            sums = [_dot(u, sp) for sp in softplus]
            for h in range(N_HEADS):
                a = jnp.exp2(z_ref[h] - (sums[h] + carry_ref[h]))
                if diag:
                    a = jnp.where(strict, a, 0.0)
                w_ref[h] = a.astype(BF16)
                carry_ref[h] = carry_ref[h] + sums[h][0:1, :]

        carry_ref[...] = jnp.zeros(carry_ref.shape, F32)
        acc_ref[...] = jnp.zeros(acc_ref.shape, F32)
        stage(i, 0)
        absorb(i, 0, True, jnp.maximum(i - 1, 0), None)

        def body(t, c):
            j = i - 1 - 2 * t
            absorb(j, 1, False, j - 1, j + 1)
            absorb(j - 1, 0, False, jnp.maximum(j - 2, 0), j)
            return c

        lax.fori_loop(0, i // 2, body, 0)

        @pl.when(i % 2 == 1)
        def _():
            absorb(0, 1, False, None, 1)

        apply_weights(0)
        q_rows = pl.ds(pl.multiple_of(i * T, T), T)
        for pair in range(2):
            o_t = jnp.concatenate([acc_ref[2 * pair], acc_ref[2 * pair + 1]], axis=0)
            o_ref[q_rows, pair * LANES:(pair + 1) * LANES] = o_t.T.astype(BF16)
        return carry

    lax.fori_loop(0, q_all_ref.shape[0], query_block, 0)


def _stick_breaking(qdt, proj3, vdt, B, S):
    u = np.triu(np.ones((TB, TB), np.float32))
    return pl.pallas_call(
        _stick_kernel,
        grid=(B,),
        in_specs=[pl.BlockSpec((None,) + qdt.shape[1:], lambda b: (b, 0, 0, 0)),
                  pl.BlockSpec((None, S, BLK), lambda b: (b, 0, PB_DK)),
                  pl.BlockSpec((None,) + vdt.shape[1:], lambda b: (b, 0, 0, 0)),
                  pl.BlockSpec(u.shape, lambda b: (0, 0))],
        out_specs=pl.BlockSpec((None, S, BLK), lambda b: (b, 0, 0)),
        out_shape=jax.ShapeDtypeStruct((B, S, BLK), BF16),
        scratch_shapes=[pltpu.VMEM((N_HEADS, TB, TB), F32),
                        pltpu.VMEM((N_HEADS, TB, TB), F32),
                        pltpu.VMEM((N_HEADS, TB, TB), BF16),
                        pltpu.VMEM((N_HEADS, 1, TB), F32),
                        pltpu.VMEM((N_HEADS, HEAD_DIM, TB), F32)],
        compiler_params=pltpu.CompilerParams(dimension_semantics=("arbitrary",),
                                             vmem_limit_bytes=VMEM_LIMIT),
        name="stick_breaking",
    )(qdt, proj3, vdt, jnp.asarray(u, BF16))


def _outproj_kernel(x_ref, oa, ob, oc, os_, ow, od,
                    ga, gb, gl, gc, gd, expand_ref, w_ref, g_ref, out_ref):
    f = lambda r: r[...].astype(F32)
    gates = _dot(gl[...], expand_ref[...])
    o_c = (gates[:, 0:BLK] * f(oc) + gates[:, BLK:2 * BLK] * f(os_)
           + gates[:, 2 * BLK:3 * BLK] * f(ow))
    mixed = [f(oa) * f(ga), f(ob) * f(gb), o_c * f(gc), f(od) * f(gd)]
    y = _dot(mixed[0].astype(BF16), w_ref[0:BLK, :])
    for g in range(1, 4):
        y = y + _dot(mixed[g].astype(BF16), w_ref[g * BLK:(g + 1) * BLK, :])
    ms = jnp.mean(y * y, axis=-1, keepdims=True)
    out_ref[...] = x_ref[...] + (y * lax.rsqrt(ms + EPS)) * g_ref[...]


def _out_projection(x2, mixer_outs, proj, w_out, g_post):
    rows = x2.shape[0]
    tm = TILE
    blk = lambda c: pl.BlockSpec((tm, BLK), lambda i: (i, c))
    expand = np.zeros((LANES, N_BRANCH * BLK), np.float32)
    for h in range(N_HEADS):
        for r in range(N_BRANCH):
            expand[N_BRANCH * h + r, r * BLK + h * HEAD_DIM:r * BLK + (h + 1) * HEAD_DIM] = 1.0
    gate_specs = [blk(PB_AG), blk(PB_BG),
                  pl.BlockSpec((tm, LANES), lambda i: (i, 2 * PB_C1 + 1)), blk(PB_CG), blk(PB_DG)]
    return pl.pallas_call(
        _outproj_kernel,
        grid=(rows // tm,),
        in_specs=[pl.BlockSpec((tm, D_MODEL), lambda i: (i, 0))]
                 + [blk(0)] * len(mixer_outs) + gate_specs
                 + [pl.BlockSpec(expand.shape, lambda i: (0, 0)),
                    pl.BlockSpec((D_MODEL, D_MODEL), lambda i: (0, 0)),
                    pl.BlockSpec((1, D_MODEL), lambda i: (0, 0))],
        out_specs=pl.BlockSpec((tm, D_MODEL), lambda i: (i, 0)),
        out_shape=jax.ShapeDtypeStruct((rows, D_MODEL), F32),
        compiler_params=pltpu.CompilerParams(dimension_semantics=("arbitrary",),
                                             vmem_limit_bytes=VMEM_LIMIT),
        name="out_projection",
    )(x2, *mixer_outs, *([proj] * len(gate_specs)), jnp.asarray(expand, BF16),
      w_out.astype(BF16), g_post.reshape(1, D_MODEL))


def _layer(x2, tabs, B, S, layer, w_rows, w_t, w_out, g_pre, g_post, mla_g_q, mla_g_kv,
           mla_w_uq, mla_w_ukv, nsa_pos_k, nsa_pos_v, nsa_k_w1, nsa_k_w2, nsa_v_w1, nsa_v_w2):
    rows = B * S
    aqkv, aqkv4, aqkv16, proj, vct, vdt, qct, qdt = _in_projection(
        x2, g_pre, w_rows, w_t, layer, tabs, B, S)
    proj3 = proj.reshape(B, S, N_PBLK * BLK)

    o_a = _dilated(aqkv, aqkv4, aqkv16, B, S)

    qnt, qrt, kn, vbt = _mla_up(proj, mla_g_q, mla_g_kv, mla_w_uq, mla_w_ukv, tabs, B, S)
    o_b = _flash_t(qnt, kn.reshape(B, S, BLK), 0, BLK, vbt, B=B, S=S,
                   aux=("per_head", qrt, proj3, 2 * PB_BKV + 1), name="mla_attention")

    kc2, vc2 = _compress(proj3, nsa_pos_k, nsa_pos_v, nsa_k_w1, nsa_k_w2, nsa_v_w1, nsa_v_w2, B, S)
    o_cmp, nsel = _cmp_topk(proj3, kc2, vc2, B, S)
    penalty = np.zeros((S, LANES), np.float32)
    penalty[np.arange(S), np.arange(S) // NSA_SEL_LEN] = MASK_PENALTY
    o_slc = _flash_t(qct, proj3, 2 * PB_C0 + 1, LANES, vct, B=B, S=S, vrow0=0,
                     aux=("shared", nsel, jnp.asarray(penalty, BF16), 0), name="nsa_selected")
    o_win = _flash_t(qct, proj3, 2 * PB_C1, LANES, vct, B=B, S=S, vrow0=HEAD_DIM,
                     window=NSA_WINDOW - 1, name="nsa_window")

    o_d = _stick_breaking(qdt, proj3, vdt, B, S)

    flat = lambda t: t.reshape(rows, BLK)
    outs = [flat(o) for o in (o_a, o_b, o_cmp, o_slc, o_win, o_d)]
    return _out_projection(x2, outs, proj, w_out, g_post)


def kernel(x, positions, w_in, w_out, g_pre, g_post, mla_g_q, mla_g_kv, mla_w_uq, mla_w_ukv,
           nsa_pos_k, nsa_pos_v, nsa_k_w1, nsa_k_w2, nsa_v_w1, nsa_v_w2):
    B, S, D = x.shape
    assert D == D_MODEL and S % 1024 == 0
    x2 = x.reshape(B * S, D)
    tabs = _rope_tables(positions)
    w_rows, w_t = _pack_w_in(w_in)
    for l in range(w_in.shape[0]):
        x2 = _layer(x2, tabs, B, S, l, w_rows, w_t, w_out[l], g_pre[l], g_post[l],
                    mla_g_q[l], mla_g_kv[l], mla_w_uq[l], mla_w_ukv[l],
                    nsa_pos_k[l], nsa_pos_v[l], nsa_k_w1[l], nsa_k_w2[l],
                    nsa_v_w1[l], nsa_v_w2[l])
    return x2.reshape(B, S, D)
```

```python
import functools
import math

import numpy as np
import jax
import jax.numpy as jnp
from jax import lax
from jax.experimental import pallas as pl
from jax.experimental.pallas import tpu as pltpu

D_MODEL = 1024
HEAD_DIM = 64
N_HEADS = 4
ROPE_THETA = 10000.0
EPS = 1e-6
NEG = -1e30
LOG2E = math.log2(math.e)
MASK_PENALTY = -(2.0 ** 100)
BIG = 1e9

MLA_Q_RANK = 256
MLA_KV_RANK = 128
MLA_NOPE = 64
MLA_ROPE = 32
MLA_V = 64

NSA_CMP_LEN = 32
NSA_CMP_STRIDE = 16
NSA_CMP_HID = 256
NSA_SEL_LEN = 64
NSA_N_SEL = 16
NSA_WINDOW = 512

LANES = 128
BLK = 256
TILE = 512
CM_TILE = 256
DILATIONS = (1, 4, 16)
DIL_BLOCK = 128
LSE_PARTS = 2
TB = 256
ONES_ROWS = 16
VMEM_LIMIT = 48 * 1024 * 1024

F32 = jnp.float32
BF16 = jnp.bfloat16

(PB_AG, PB_BCQ, PB_BKV, PB_BG, PB_CQ, PB_C0, PB_C1, PB_CG, PB_DQ, PB_DK, PB_DG) = range(11)
N_PBLK = 11
N_BRANCH = 3
ROPE_NONE, ROPE_64, ROPE_32, ROPE_64_LO, ACT_SILU, ACT_SIGMOID = range(6)


def _dot_nt(a, b):
    return lax.dot_general(a, b, (((1,), (1,)), ((), ())), preferred_element_type=F32)


def _dot(a, b):
    return jnp.dot(a, b, preferred_element_type=F32)


def _rope_half(y, cos, sin_signed, group):
    half = group // 2
    lane = lax.broadcasted_iota(jnp.int32, y.shape, 1)
    first = (lane & (group - 1)) < half
    partner = jnp.where(first, pltpu.roll(y, LANES - half, 1), pltpu.roll(y, half, 1))
    return y * cos + partner * sin_signed


def _rope_table_kernel(pos_ref, inv_ref, cos_ref, sin_ref):
    ang = inv_ref[...] * pos_ref[...].astype(F32)
    cos_ref[...] = jnp.cos(ang)
    sin_ref[...] = jnp.sin(ang)


def _rope_tables(positions):
    rows = positions.size
    tm = 2048
    inv = np.concatenate([ROPE_THETA ** (-np.arange(n, dtype=np.float64) / n)
                          for n in (HEAD_DIM // 2, MLA_ROPE // 2)]).astype(np.float32)[:, None]
    out = jax.ShapeDtypeStruct((inv.shape[0], rows), F32)
    tab_spec = pl.BlockSpec((inv.shape[0], tm), lambda i: (0, i))
    return pl.pallas_call(
        _rope_table_kernel,
        grid=(rows // tm,),
        in_specs=[pl.BlockSpec((1, tm), lambda i: (0, i)),
                  pl.BlockSpec(inv.shape, lambda i: (0, 0))],
        out_specs=[tab_spec] * 2,
        out_shape=[out] * 2,
        name="rope_tables",
    )(positions.reshape(1, rows), jnp.asarray(inv))


def _token_major_rope(cos_ref, sin_ref, group):
    n = group // 2
    lo = 0 if group == HEAD_DIM else HEAD_DIM // 2
    c = cos_ref[lo:lo + n, :]
    s = sin_ref[lo:lo + n, :]
    reps = LANES // n
    cos = jnp.concatenate([c] * reps, axis=0).T
    sin = jnp.concatenate([-s if k % 2 == 0 else s for k in range(reps)], axis=0).T
    return cos, sin


def _inproj_kernel(x_ref, g_ref, w_ref, wt_ref, perm_ref, cos_ref, sin_ref,
                   aqkv_ref, aqkv4_ref, aqkv16_ref, proj_ref, vct_ref, vdt_ref, qct_ref, qdt_ref,
                   *, plan_a, plan_p):
    x = x_ref[...]
    tm = x.shape[0]
    ms = jnp.mean(x * x, axis=-1, keepdims=True)
    h = ((x * lax.rsqrt(ms + EPS)) * g_ref[...]).astype(BF16)
    lane = lax.broadcasted_iota(jnp.int32, (tm, LANES), 1)
    c64, s64 = _token_major_rope(cos_ref, sin_ref, HEAD_DIM)
    c32, s32 = _token_major_rope(cos_ref, sin_ref, MLA_ROPE)

    def epilogue(y, ops, scale):
        halves = []
        for hf, op in enumerate(ops):
            yh = y[:, hf * LANES:(hf + 1) * LANES]
            if op == ROPE_64:
                yh = _rope_half(yh, c64, s64, 64)
            elif op == ROPE_32:
                yh = _rope_half(yh, c32, s32, 32)
            elif op == ROPE_64_LO:
                yh = jnp.where(lane < 64, _rope_half(yh, c64, s64, 64), yh)
            elif op == ACT_SILU:
                yh = yh * (1.0 / (1.0 + jnp.exp(-yh)))
            elif op == ACT_SIGMOID:
                yh = 1.0 / (1.0 + jnp.exp(-yh))
            if scale != 1.0:
                yh = yh * scale
            halves.append(yh)
        return jnp.concatenate(halves, axis=1)

    feature_major = {PB_CQ: qct_ref, PB_DQ: qdt_ref}
    col = 0
    for out_ref, plan in ((aqkv_ref, plan_a), (proj_ref, plan_p)):
        for b, (ops, scale) in enumerate(plan):
            y = epilogue(_dot(h, w_ref[:, col:col + BLK]), ops, scale)
            out_ref[:, b * BLK:(b + 1) * BLK] = y.astype(BF16)
            if out_ref is proj_ref and b in feature_major:
                y_t = y.T
                for blk in range(tm // TB):
                    feature_major[b][blk] = y_t[:, blk * TB:(blk + 1) * TB].astype(BF16)
            col += BLK

    for t in range(tm // CM_TILE):
        tile = slice(CM_TILE * t, CM_TILE * (t + 1))
        a_rows = aqkv_ref[tile, :]
        aqkv4_ref[tile, :] = _dot(perm_ref[0], a_rows).astype(BF16)
        aqkv16_ref[tile, :] = _dot(perm_ref[1], a_rows).astype(BF16)

    vt = _dot_nt(wt_ref[...], h)
    for blk in range(tm // TB):
        cols = slice(blk * TB, (blk + 1) * TB)
        vct_ref[blk] = vt[0:LANES, cols].astype(BF16)
        vdt_ref[blk] = vt[LANES:LANES + BLK, cols].astype(BF16)


_PLAN_A = (((ROPE_64, ROPE_64), LOG2E * HEAD_DIM ** -0.5),
           ((ROPE_64, ROPE_64), 1.0),
           ((ROPE_NONE, ROPE_NONE), 1.0))
_PLAN_P = (
    ((ACT_SILU, ACT_SILU), 1.0),
    ((ROPE_NONE, ROPE_NONE), 1.0),
    ((ROPE_NONE, ROPE_32), 1.0),
    ((ACT_SILU, ACT_SILU), 1.0),
    ((ROPE_64, ROPE_64), LOG2E * HEAD_DIM ** -0.5),
    ((ROPE_64_LO, ROPE_64), 1.0),
    ((ROPE_64, ACT_SIGMOID), 1.0),
    ((ACT_SILU, ACT_SILU), 1.0),
    ((ROPE_NONE, ROPE_NONE), LOG2E * HEAD_DIM ** -0.5),
    ((ROPE_NONE, ROPE_NONE), 1.0),
    ((ACT_SILU, ACT_SILU), 1.0),
)
assert len(_PLAN_P) == N_PBLK


_W_IN_SEGMENTS = (("aq", 256), ("ak", 256), ("av", 256), ("ag", 256),
                  ("bcq", 256), ("bckv", 128), ("bkr", 32), ("bg", 256),
                  ("cq", 256), ("ckc", 64), ("cvc", 64), ("cks", 64), ("cvs", 64),
                  ("ckw", 64), ("cvw", 64), ("cgl", 12), ("cg", 256),
                  ("dq", 256), ("dk", 256), ("dv", 256), ("dg", 256))
W_ROWS_COLS = (3 + N_PBLK) * BLK
W_T_ROWS = LANES + BLK


def _pack_w_kernel(wt_ref, rows_ref, t_ref):
    wt = wt_ref[...].astype(F32)
    o = 0
    seg = {}
    for name, width in _W_IN_SEGMENTS:
        seg[name] = wt[o:o + width, :]
        o += width
    gl_pad = jnp.zeros((LANES - N_HEADS * N_BRANCH, wt.shape[1]), wt.dtype)
    rows = [seg["aq"], seg["ak"], seg["av"],
            seg["ag"], seg["bcq"],
            seg["bckv"], seg["bkr"], seg["bkr"], seg["bkr"], seg["bkr"],
            seg["bg"], seg["cq"],
            seg["ckc"], seg["cvc"], seg["cks"], seg["cks"],
            seg["ckw"], seg["ckw"], seg["cgl"], gl_pad,
            seg["cg"], seg["dq"], seg["dk"], seg["dg"]]
    rows_ref[...] = jnp.concatenate(rows, axis=0).T.astype(BF16)
    t_ref[...] = jnp.concatenate([seg["cvs"], seg["cvw"], seg["dv"]], axis=0).astype(BF16)


def _pack_w_in(w_in):
    depth, d_model, d_in = w_in.shape
    assert d_model == D_MODEL and d_in == sum(width for _, width in _W_IN_SEGMENTS)
    tc = 256
    return pl.pallas_call(
        _pack_w_kernel,
        grid=(depth, D_MODEL // tc),
        in_specs=[pl.BlockSpec((None, d_in, tc), lambda l, i: (l, 0, i))],
        out_specs=[pl.BlockSpec((None, tc, W_ROWS_COLS), lambda l, i: (l, i, 0)),
                   pl.BlockSpec((None, W_T_ROWS, tc), lambda l, i: (l, 0, i))],
        out_shape=[jax.ShapeDtypeStruct((depth, D_MODEL, W_ROWS_COLS), BF16),
                   jax.ShapeDtypeStruct((depth, W_T_ROWS, D_MODEL), BF16)],
        compiler_params=pltpu.CompilerParams(dimension_semantics=("arbitrary", "arbitrary"),
                                             vmem_limit_bytes=VMEM_LIMIT),
        name="pack_w_in",
    )(jnp.transpose(w_in, (0, 2, 1)).astype(BF16))


def _class_major_perm(dil, tile):
    n = tile // dil
    p = np.zeros((tile, tile), np.float32)
    pos = np.arange(tile)
    p[(pos % dil) * n + pos // dil, pos] = 1.0
    return p


def _in_projection(x2, g_pre, w_rows, w_t, layer, tabs, B, S):
    rows = x2.shape[0]
    tm = TILE
    per_b = S // tm
    nkb = tm // TB
    perm = jnp.asarray(np.stack([_class_major_perm(d, CM_TILE) for d in DILATIONS[1:]]), BF16)
    tab_spec = pl.BlockSpec((tabs[0].shape[0], tm), lambda i: (0, i))
    vt_spec = lambda r: pl.BlockSpec((None, nkb, r, TB), lambda i: (i // per_b, i % per_b, 0, 0))
    a_spec = pl.BlockSpec((tm, 3 * BLK), lambda i: (i, 0))
    a_shape = jax.ShapeDtypeStruct((rows, 3 * BLK), BF16)
    return pl.pallas_call(
        functools.partial(_inproj_kernel, plan_a=_PLAN_A, plan_p=_PLAN_P),
        grid=(rows // tm,),
        in_specs=[pl.BlockSpec((tm, D_MODEL), lambda i: (i, 0)),
                  pl.BlockSpec((1, D_MODEL), lambda i: (0, 0)),
                  pl.BlockSpec((None,) + w_rows.shape[1:], lambda i: (layer, 0, 0)),
                  pl.BlockSpec((None,) + w_t.shape[1:], lambda i: (layer, 0, 0)),
                  pl.BlockSpec(perm.shape, lambda i: (0, 0, 0))] + [tab_spec] * 2,
        out_specs=[a_spec, a_spec, a_spec,
                   pl.BlockSpec((tm, N_PBLK * BLK), lambda i: (i, 0)),
                   vt_spec(LANES), vt_spec(BLK), vt_spec(BLK), vt_spec(BLK)],
        out_shape=[a_shape, a_shape, a_shape,
                   jax.ShapeDtypeStruct((rows, N_PBLK * BLK), BF16),
                   jax.ShapeDtypeStruct((B, S // TB, LANES, TB), BF16)]
                  + [jax.ShapeDtypeStruct((B, S // TB, BLK, TB), BF16)] * 3,
        compiler_params=pltpu.CompilerParams(dimension_semantics=("arbitrary",),
                                             vmem_limit_bytes=VMEM_LIMIT),
        name="in_projection",
    )(x2, g_pre.reshape(1, D_MODEL), w_rows, w_t, perm, *tabs)


def _head_masks(shape):
    lane = lax.broadcasted_iota(jnp.int32, shape, 1)
    return lane < HEAD_DIM


def _masked_heads(q_ref, lo_mask):
    out = []
    for h in range(N_HEADS):
        pair, e = divmod(h, 2)
        qp = q_ref[:, pair * LANES:(pair + 1) * LANES]
        keep = lo_mask if e == 0 else jnp.logical_not(lo_mask)
        out.append(jnp.where(keep, qp, jnp.zeros_like(qp)))
    return out


def _dilated_kernel(an_ref, a4_ref, a16_ref, unperm_ref, expand_ref, o_ref, res_ref, s_ref):
    T = DIL_BLOCK
    S = an_ref.shape[0]
    n_tiles = S // CM_TILE
    lo = _head_masks((T, LANES))
    lane = lax.broadcasted_iota(jnp.int32, (T, LANES), 1)
    dist2 = (lax.broadcasted_iota(jnp.int32, (T, 2 * T), 0)
             - lax.broadcasted_iota(jnp.int32, (T, 2 * T), 1))
    causal1 = (lax.broadcasted_iota(jnp.int32, (T, T), 1)
               <= lax.broadcasted_iota(jnp.int32, (T, T), 0))

    def scores(q, k):
        qm = _masked_heads(q, lo)
        return tuple(_dot_nt(qm[h], k[:, (h // 2) * LANES:(h // 2 + 1) * LANES])
                     for h in range(N_HEADS))

    def finish(n_keys, v, mask, stage_next):
        probs = []
        for h in range(N_HEADS):
            s = jnp.where(mask, s_ref[h, :, 0:n_keys], NEG)
            m = jnp.max(s, axis=-1, keepdims=True)
            p = jnp.exp2(s - m)
            probs.append((m, jnp.sum(p, axis=-1, keepdims=True), p.astype(BF16)))
        stage_next()
        outs = []
        top = jnp.zeros((T, LANES), F32)
        den = jnp.ones((T, LANES), F32)
        for h in range(N_HEADS):
            m, l, p = probs[h]
            outs.append(_dot(p, v[:, (h // 2) * LANES:(h // 2 + 1) * LANES]) * (1.0 / l))
            mine = (lane & (N_HEADS - 1)) == h
            top = jnp.where(mine, m, top)
            den = jnp.where(mine, l, den)
        lse = top + jnp.log2(den)
        hi = lse.astype(BF16).astype(F32)
        parts = jnp.where(lane < N_HEADS, hi,
                          jnp.where(lane < LSE_PARTS * N_HEADS, lse - hi, 0.0))
        return jnp.concatenate(
            [jnp.where(lo, outs[0], outs[1]), jnp.where(lo, outs[2], outs[3]), parts],
            axis=1).astype(BF16)

    def band_mask(delta):
        d = dist2 + delta
        return jnp.logical_and(d >= 0, d <= DIL_BLOCK)

    def run(n_units, n_keys, load_qk, load_v, store, mask_of):
        def stage(u):
            for h, s in enumerate(scores(*load_qk(u))):
                s_ref[h, :, 0:n_keys] = s

        def body(u, carry):
            nxt = jnp.minimum(u + 1, n_units - 1)
            store(u, finish(n_keys, load_v(u), mask_of(u), lambda: stage(nxt)))
            return carry

        stage(0)
        lax.fori_loop(0, n_units, body, 0)

    def rows(start, size):
        return pl.ds(pl.multiple_of(start, 16), size)

    def k0_1(u):
        return T * jnp.maximum(u - 1, 0)

    def store1(u, r):
        res_ref[0, rows(T * u, T), :] = r

    run(S // T, 2 * T,
        lambda u: (an_ref[rows(T * u, T), 0:BLK], an_ref[rows(k0_1(u), 2 * T), BLK:2 * BLK]),
        lambda u: an_ref[rows(k0_1(u), 2 * T), 2 * BLK:3 * BLK],
        store1,
        lambda u: band_mask(T * u - k0_1(u)))

    per4 = CM_TILE // 4
    nq4 = S // 4 // T

    def pieces4(u, blk):
        first_tile = blk * (T // per4)
        return [rows(CM_TILE * (first_tile + k) + per4 * (u // nq4), per4)
                for k in range(T // per4)]

    def kb0_4(u):
        return jnp.maximum(u % nq4 - 1, 0)

    def load4(u, blks, cols):
        return jnp.concatenate([a4_ref[p, cols] for blk in blks for p in pieces4(u, blk)], axis=0)

    def store4(u, r):
        for k, p in enumerate(pieces4(u, u % nq4)):
            res_ref[1, p, :] = r[per4 * k:per4 * (k + 1), :]

    run(4 * nq4, 2 * T,
        lambda u: (load4(u, [u % nq4], slice(0, BLK)),
                   load4(u, [kb0_4(u), kb0_4(u) + 1], slice(BLK, 2 * BLK))),
        lambda u: load4(u, [kb0_4(u), kb0_4(u) + 1], slice(2 * BLK, 3 * BLK)),
        store4,
        lambda u: band_mask(T * (u % nq4 - kb0_4(u))))

    per16 = CM_TILE // 16

    def load16(c, cols):
        return jnp.concatenate([a16_ref[rows(CM_TILE * t + per16 * c, per16), cols]
                                for t in range(n_tiles)], axis=0)

    def store16(c, r):
        for t in range(n_tiles):
            res_ref[2, rows(CM_TILE * t + per16 * c, per16), :] = r[per16 * t:per16 * (t + 1), :]

    run(16, T,
        lambda c: (load16(c, slice(0, BLK)), load16(c, slice(BLK, 2 * BLK))),
        lambda c: load16(c, slice(2 * BLK, 3 * BLK)),
        store16,
        lambda c: causal1)

    expand = expand_ref[...]
    for t in range(n_tiles):
        tile = slice(CM_TILE * t, CM_TILE * (t + 1))
        nat = [res_ref[0, tile, :].astype(F32),
               _dot(unperm_ref[0], res_ref[1, tile, :]),
               _dot(unperm_ref[1], res_ref[2, tile, :])]
        lse = [_dot(r[:, BLK:].astype(BF16), expand) for r in nat]
        top = jnp.maximum(lse[0], jnp.maximum(lse[1], lse[2]))
        w = [jnp.exp2(l - top) for l in lse]
        mixed = (w[0] * nat[0][:, :BLK] + w[1] * nat[1][:, :BLK] + w[2] * nat[2][:, :BLK])
        o_ref[tile, :] = (mixed / (w[0] + w[1] + w[2])).astype(BF16)


def _dilated(aqkv, aqkv4, aqkv16, B, S):
    assert S == 16 * DIL_BLOCK and S % CM_TILE == 0 and DIL_BLOCK % (CM_TILE // 4) == 0
    r3 = lambda t: t.reshape(B, S, 3 * BLK)
    unperm = jnp.asarray(np.stack([_class_major_perm(d, CM_TILE).T for d in DILATIONS[1:]]), BF16)
    expand = np.zeros((LANES, BLK), np.float32)
    for h in range(N_HEADS):
        for part in range(LSE_PARTS):
            expand[N_HEADS * part + h, h * HEAD_DIM:(h + 1) * HEAD_DIM] = 1.0
    a_spec = pl.BlockSpec((None, S, 3 * BLK), lambda b: (b, 0, 0))
    return pl.pallas_call(
        _dilated_kernel,
        grid=(B,),
        in_specs=[a_spec, a_spec, a_spec,
                  pl.BlockSpec(unperm.shape, lambda b: (0, 0, 0)),
                  pl.BlockSpec(expand.shape, lambda b: (0, 0))],
        out_specs=pl.BlockSpec((None, S, BLK), lambda b: (b, 0, 0)),
        out_shape=jax.ShapeDtypeStruct((B, S, BLK), BF16),
        scratch_shapes=[pltpu.VMEM((3, S, BLK + LANES), BF16),
                        pltpu.VMEM((N_HEADS, DIL_BLOCK, 2 * DIL_BLOCK), F32)],
        compiler_params=pltpu.CompilerParams(dimension_semantics=("arbitrary",),
                                             vmem_limit_bytes=VMEM_LIMIT),
        name="dilated_mixture",
    )(r3(aqkv), r3(aqkv4), r3(aqkv16), unperm, jnp.asarray(expand, BF16))


def _masked_head_rows(qt_ref):
    out = []
    zeros = jnp.zeros((HEAD_DIM, qt_ref.shape[-1]), BF16)
    for h in range(N_HEADS):
        own = qt_ref[h * HEAD_DIM:(h + 1) * HEAD_DIM, :]
        out.append(jnp.concatenate([own, zeros] if h % 2 == 0 else [zeros, own], axis=0))
    return out


def _flash_t_kernel(*refs, window, kshared, vrow0, aux):
    it = iter(refs)
    q_all_ref = next(it)
    qx_all_ref = next(it) if aux else None
    k_ref = next(it)
    kx_ref = next(it) if aux else None
    vt_ref = next(it)
    o_ref = next(it)
    s_ref, m_ref, acc_ref = next(it), next(it), next(it)
    band_ref = next(it) if window is not None else None
    T = TB

    krow = lax.broadcasted_iota(jnp.int32, (T, T), 0)
    qcol = lax.broadcasted_iota(jnp.int32, (T, T), 1)
    ones = jnp.ones((ONES_ROWS, T), BF16)

    n_q = q_all_ref.shape[0]
    if window is not None:
        o_full = (window - T + 1) // T
        o_none = (window + T - 1) // T + 1
        assert o_none - o_full == 2

    def query_weights(i):
        qcat = _masked_head_rows(q_all_ref.at[i])
        if aux == "per_head":
            group = LANES // N_HEADS
            qcat = [jnp.concatenate(
                [qcat[h]] + [qx_all_ref[i, g * group:(g + 1) * group, :] if g == h
                             else jnp.zeros((group, T), BF16) for g in range(N_HEADS)], axis=0)
                for h in range(N_HEADS)]
        elif aux == "shared":
            qx = qx_all_ref[i]
            qcat = [jnp.concatenate([qcat[h], qx], axis=0) for h in range(N_HEADS)]
        return qcat

    def first_full_block(i):
        return 0 if window is None else jnp.maximum(i - o_full, 0)

    def scores(j, qcat):
        rows = pl.ds(pl.multiple_of(j * T, T), T)
        out = []
        for h in range(N_HEADS):
            pair = h // 2
            cols = slice(0, LANES) if kshared else slice(pair * LANES, (pair + 1) * LANES)
            kb = k_ref[rows, cols]
            if aux:
                kb = jnp.concatenate([kb, kx_ref[rows, :]], axis=1)
            out.append(_dot(kb, qcat[h]))
        return out

    def stage(j, qcat, dst=None):
        dst = s_ref if dst is None else dst
        for h, s in enumerate(scores(j, qcat)):
            dst[h] = s

    def band_block(i):
        o = o_full + 1
        j = i - o
        limit = jnp.where(j >= 0, window - o * T, -2 * T)
        return jnp.maximum(j, 0), limit

    def stage_query_block(i):
        qcat = query_weights(i)
        stage(first_full_block(i), qcat)
        if window is not None:
            stage(band_block(i)[0], qcat, band_ref)

    def query_block(i, carry):
        qcat = query_weights(i)

        def absorb(j, mask, staged_next=None, src=s_ref):
            vt_all = vt_ref[j]
            probs = []
            for h in range(N_HEADS):
                s = src[h]
                if mask is not None:
                    s = jnp.where(mask, s, NEG)
                m = m_ref[h]
                m_new = jnp.maximum(m, jnp.max(s, axis=0, keepdims=True))
                alpha = jnp.exp2(m - m_new)
                p = jnp.exp2(s - m_new)
                probs.append((m_new, alpha, p.astype(BF16)))
            if staged_next is not None:
                staged_next()
            for h in range(N_HEADS):
                m_new, alpha, p = probs[h]
                r0 = vrow0 if kshared else h * HEAD_DIM
                lhs = jnp.concatenate([vt_all[r0:r0 + HEAD_DIM, :], ones], axis=0)
                acc_ref[h] = alpha * acc_ref[h] + _dot(lhs, p)
                m_ref[h] = m_new

        m_ref[...] = jnp.full(m_ref.shape, NEG, F32)
        acc_ref[...] = jnp.zeros(acc_ref.shape, F32)
        first = first_full_block(i)
        if window is not None:
            jc, limit = band_block(i)
            absorb(jc, (qcol - krow) <= limit, src=band_ref)

        def body(j, c):
            absorb(j, None, staged_next=lambda: stage(j + 1, qcat))
            return c

        lax.fori_loop(first, i, body, 0)
        nxt = jnp.minimum(i + 1, n_q - 1)
        absorb(i, krow <= qcol, staged_next=lambda: stage_query_block(nxt))

        q_rows = pl.ds(pl.multiple_of(i * T, T), T)
        for pair in range(2):
            outs = []
            for e in range(2):
                acc = acc_ref[2 * pair + e]
                outs.append(acc[0:HEAD_DIM, :] * (1.0 / acc[HEAD_DIM:HEAD_DIM + 1, :]))
            o_ref[q_rows, pair * LANES:(pair + 1) * LANES] = (
                jnp.concatenate(outs, axis=0).T.astype(BF16))
        return carry

    stage_query_block(0)
    lax.fori_loop(0, n_q, query_block, 0)


def _flash_t(qt, k, kcol, kw, vt, *, B, S, window=None, vrow0=0, aux=None, name):
    kshared = kw == LANES
    whole = lambda a: pl.BlockSpec((None,) + a.shape[1:], lambda b: (b,) + (0,) * (a.ndim - 1))
    in_specs = [whole(qt)]
    args = [qt]
    if aux is not None:
        kind, qx, kx, kxcol = aux
        in_specs.append(whole(qx))
        args.append(qx)
    in_specs.append(pl.BlockSpec((None, S, kw), lambda b: (b, 0, kcol)))
    args.append(k)
    if aux is not None:
        if kx.ndim == 3:
            in_specs.append(pl.BlockSpec((None, S, LANES), lambda b: (b, 0, kxcol)))
        else:
            in_specs.append(pl.BlockSpec((S, LANES), lambda b: (0, kxcol)))
        args.append(kx)
    in_specs.append(whole(vt))
    args.append(vt)
    return pl.pallas_call(
        functools.partial(_flash_t_kernel, window=window, kshared=kshared, vrow0=vrow0,
                          aux=None if aux is None else aux[0]),
        grid=(B,),
        in_specs=in_specs,
        out_specs=pl.BlockSpec((None, S, BLK), lambda b: (b, 0, 0)),
        out_shape=jax.ShapeDtypeStruct((B, S, BLK), BF16),
        scratch_shapes=[pltpu.VMEM((N_HEADS, TB, TB), F32),
                        pltpu.VMEM((N_HEADS, 1, TB), F32),
                        pltpu.VMEM((N_HEADS, HEAD_DIM + ONES_ROWS, TB), F32)]
                       + ([pltpu.VMEM((N_HEADS, TB, TB), F32)] if window is not None else []),
        compiler_params=pltpu.CompilerParams(dimension_semantics=("arbitrary",),
                                             vmem_limit_bytes=VMEM_LIMIT),
        name=name,
    )(*args)


def _mla_up_kernel(cq_ref, ckv_ref, gq_ref, gkv_ref, wq_ref, wk_ref, wvt_ref, cos_ref, sin_ref,
                   qn_ref, qr_ref, kn_ref, vt_ref):
    def norm(t, g):
        t = t.astype(F32)
        ms = jnp.mean(t * t, axis=-1, keepdims=True)
        return ((t * lax.rsqrt(ms + EPS)) * g).astype(BF16)

    scale = LOG2E * (MLA_NOPE + MLA_ROPE) ** -0.5
    q_t = _dot_nt(wq_ref[...], norm(cq_ref[...], gq_ref[...])) * scale
    qn_t = q_t[:BLK, :]
    half = MLA_ROPE // 2
    cos = cos_ref[HEAD_DIM // 2:HEAD_DIM // 2 + half, :]
    sin = sin_ref[HEAD_DIM // 2:HEAD_DIM // 2 + half, :]
    rot = []
    for h in range(N_HEADS):
        x1 = q_t[BLK + h * MLA_ROPE:BLK + h * MLA_ROPE + half, :]
        x2 = q_t[BLK + h * MLA_ROPE + half:BLK + (h + 1) * MLA_ROPE, :]
        rot += [x1 * cos - x2 * sin, x1 * sin + x2 * cos]
    qr_t = jnp.concatenate(rot, axis=0)
    ckv = norm(ckv_ref[...], gkv_ref[...])
    kn_ref[...] = _dot(ckv, wk_ref[...]).astype(BF16)
    vt = _dot_nt(wvt_ref[...], ckv)
    for blk in range(vt.shape[1] // TB):
        cols = slice(blk * TB, (blk + 1) * TB)
        qn_ref[blk] = qn_t[:, cols].astype(BF16)
        qr_ref[blk] = qr_t[:, cols].astype(BF16)
        vt_ref[blk] = vt[:, cols].astype(BF16)


def _mla_up(proj, g_q, g_kv, w_uq, w_ukv, tabs, B, S):
    rows = proj.shape[0]
    tm = 1024
    per_b = S // tm
    wq = w_uq.reshape(MLA_Q_RANK, N_HEADS, MLA_NOPE + MLA_ROPE)
    wq = jnp.concatenate([wq[:, :, :MLA_NOPE].reshape(MLA_Q_RANK, -1),
                          wq[:, :, MLA_NOPE:].reshape(MLA_Q_RANK, -1)], axis=1).T.astype(BF16)
    wkv = w_ukv.reshape(MLA_KV_RANK, N_HEADS, MLA_NOPE + MLA_V)
    wk = wkv[:, :, :MLA_NOPE].reshape(MLA_KV_RANK, -1).astype(BF16)
    wvt = wkv[:, :, MLA_NOPE:].reshape(MLA_KV_RANK, -1).T.astype(BF16)
    row_spec = lambda w, c: pl.BlockSpec((tm, w), lambda i: (i, c))
    full = lambda a: pl.BlockSpec(a.shape, lambda i: (0, 0))
    gq = g_q.reshape(1, -1)
    gkv = g_kv.reshape(1, -1)
    fm_shape = lambda r: jax.ShapeDtypeStruct((B, S // TB, r, TB), BF16)
    fm_spec = lambda r: pl.BlockSpec((None, tm // TB, r, TB),
                                     lambda i: (i // per_b, i % per_b, 0, 0))
    outs = [fm_shape(BLK), fm_shape(LANES), jax.ShapeDtypeStruct((rows, BLK), BF16), fm_shape(BLK)]
    return pl.pallas_call(
        _mla_up_kernel,
        grid=(rows // tm,),
        in_specs=[row_spec(BLK, PB_BCQ), row_spec(LANES, 2 * PB_BKV), full(gq), full(gkv),
                  full(wq), full(wk), full(wvt)]
                 + [pl.BlockSpec((tabs[0].shape[0], tm), lambda i: (0, i))] * 2,
        out_specs=[fm_spec(BLK), fm_spec(LANES), row_spec(BLK, 0), fm_spec(BLK)],
        out_shape=outs,
        compiler_params=pltpu.CompilerParams(dimension_semantics=("arbitrary",),
                                             vmem_limit_bytes=VMEM_LIMIT),
        name="mla_up",
    )(proj, proj, gq, gkv, wq, wk, wvt, *tabs)


def _compress_kernel(x_ref, perm_ref, pos_ref, w1_ref, w2_ref, kc_ref, vc_ref):
    n_tiles = x_ref.shape[0] // TILE
    per = TILE // NSA_CMP_STRIDE
    slabs = []
    for t in range(n_tiles):
        cm = _dot(perm_ref[...], x_ref[TILE * t:TILE * (t + 1), :])
        slabs.append(jnp.concatenate([cm[per * c:per * (c + 1), :]
                                      for c in range(NSA_CMP_STRIDE)], axis=1))
    x = jnp.concatenate(slabs, axis=0)
    half = x.shape[1]
    top = (x + pos_ref[0:1, :]).astype(BF16)
    bot = (x + pos_ref[1:2, :]).astype(BF16)
    y_top = _dot(top, w1_ref[0:half, :])
    y_bot = _dot(bot, w1_ref[half:2 * half, :])
    pre = y_top + pltpu.roll(y_bot, y_bot.shape[0] - 1, 0)
    hid = (pre * (1.0 / (1.0 + jnp.exp(-pre)))).astype(BF16)
    kc_ref[...] = _dot(hid[:, :NSA_CMP_HID], w2_ref[0]).astype(BF16)
    vc_ref[...] = _dot(hid[:, NSA_CMP_HID:], w2_ref[1]).astype(BF16)


def _compress(proj3, pos_k, pos_v, kw1, kw2, vw1, vw2, B, S):
    assert S % TILE == 0 and NSA_CMP_LEN == 2 * NSA_CMP_STRIDE
    n16 = S // NSA_CMP_STRIDE
    perm = jnp.asarray(_class_major_perm(NSA_CMP_STRIDE, TILE), BF16)
    pos = jnp.concatenate([pos_k, pos_v], axis=1).reshape(2, NSA_CMP_STRIDE * LANES)
    w1 = jnp.zeros((NSA_CMP_LEN, LANES, 2 * NSA_CMP_HID), F32)
    w1 = w1.at[:, :HEAD_DIM, :NSA_CMP_HID].set(kw1.reshape(NSA_CMP_LEN, HEAD_DIM, NSA_CMP_HID))
    w1 = w1.at[:, HEAD_DIM:, NSA_CMP_HID:].set(vw1.reshape(NSA_CMP_LEN, HEAD_DIM, NSA_CMP_HID))
    w1 = w1.reshape(NSA_CMP_LEN * LANES, 2 * NSA_CMP_HID).astype(BF16)
    w2 = jnp.stack([jnp.tile(kw2, (1, 2)), jnp.tile(vw2, (1, 2))]).astype(BF16)
    out = jax.ShapeDtypeStruct((B, n16, LANES), BF16)
    return pl.pallas_call(
        _compress_kernel,
        grid=(B,),
        in_specs=[pl.BlockSpec((None, S, LANES), lambda b: (b, 0, 2 * PB_C0)),
                  pl.BlockSpec(perm.shape, lambda b: (0, 0)),
                  pl.BlockSpec(pos.shape, lambda b: (0, 0)),
                  pl.BlockSpec(w1.shape, lambda b: (0, 0)),
                  pl.BlockSpec(w2.shape, lambda b: (0, 0, 0))],
        out_specs=[pl.BlockSpec((None, n16, LANES), lambda b: (b, 0, 0))] * 2,
        out_shape=[out, out],
        compiler_params=pltpu.CompilerParams(dimension_semantics=("arbitrary",),
                                             vmem_limit_bytes=VMEM_LIMIT),
        name="nsa_compress",
    )(proj3, perm, pos, w1, w2)


def _cmp_topk_kernel(q_ref, kc_ref, vc_ref, ovt_ref, o_ref, nsel_ref, *, T, n_cmp, n_slc):
    i = pl.program_id(1)
    t0 = i * T
    lane = lax.broadcasted_iota(jnp.int32, (T, LANES), 1)
    tok = t0 + lax.broadcasted_iota(jnp.int32, (T, LANES), 0)
    ok = jnp.logical_and(lane * NSA_CMP_STRIDE + (NSA_CMP_LEN - 1) <= tok, lane < n_cmp)
    lo = lane < HEAD_DIM
    qm = _masked_heads(q_ref, lo)
    kc = kc_ref[...]
    vc = vc_ref[...]
    psum = jnp.zeros((T, LANES), F32)
    outs = []
    for h in range(N_HEADS):
        s = jnp.where(ok, _dot_nt(qm[h], kc), NEG)
        mx = jnp.max(s, axis=-1, keepdims=True)
        e = jnp.where(ok, jnp.exp2(s - mx), 0.0)
        den = jnp.sum(e, axis=-1, keepdims=True)
        p = e * (1.0 / jnp.maximum(den, 1e-30))
        psum = psum + p
        outs.append(_dot(p.astype(BF16), vc))
    for pair in range(2):
        o_ref[:, pair * LANES:(pair + 1) * LANES] = jnp.where(
            lo, outs[2 * pair], outs[2 * pair + 1]).astype(BF16)

    p_hi = psum.astype(BF16)
    p_lo = (psum - p_hi.astype(F32)).astype(BF16)
    ovt = ovt_ref[...]
    imp = _dot_nt(ovt, p_hi) + _dot_nt(ovt, p_lo)
    jj = lax.broadcasted_iota(jnp.int32, (n_slc, T), 0)
    bt = (t0 + lax.broadcasted_iota(jnp.int32, (n_slc, T), 1)) >> 6
    forced = jnp.logical_or(jj == 0, jnp.logical_or(jj == bt, jj == bt - 1))
    val = jnp.where(forced, BIG, jnp.where(jj > bt, -BIG, imp))
    sel_rows = []
    for j in range(n_slc):
        vj = val[j:j + 1, :]
        beats = jnp.logical_or(val > vj, jnp.logical_and(val == vj, jj < j))
        rank = jnp.sum(beats.astype(F32), axis=0, keepdims=True)
        sel_rows.append((rank >= NSA_N_SEL).astype(F32))
    sel_rows.append(jnp.zeros((LANES - n_slc, T), F32))
    nsel = jnp.concatenate(sel_rows, axis=0).astype(BF16)
    for blk in range(T // TB):
        nsel_ref[blk] = nsel[:, blk * TB:(blk + 1) * TB]


def _cmp_topk(proj3, kc2, vc2, B, S):
    T = 2 * TB
    n_cmp = (S - NSA_CMP_LEN) // NSA_CMP_STRIDE + 1
    n_slc = S // NSA_SEL_LEN
    assert NSA_SEL_LEN == 64 and n_slc <= LANES and n_cmp <= LANES
    ci = np.arange(LANES)[None, :] * NSA_CMP_STRIDE
    sj = np.arange(n_slc)[:, None] * NSA_SEL_LEN
    ovt = ((ci < sj + NSA_SEL_LEN) & (ci + NSA_CMP_LEN > sj) & (np.arange(LANES)[None, :] < n_cmp))
    ovt = jnp.asarray(ovt.astype(np.float32), BF16)
    return pl.pallas_call(
        functools.partial(_cmp_topk_kernel, T=T, n_cmp=n_cmp, n_slc=n_slc),
        grid=(B, S // T),
        in_specs=[pl.BlockSpec((None, T, BLK), lambda b, i: (b, i, PB_CQ)),
                  pl.BlockSpec((None, LANES, LANES), lambda b, i: (b, 0, 0)),
                  pl.BlockSpec((None, LANES, LANES), lambda b, i: (b, 0, 0)),
                  pl.BlockSpec(ovt.shape, lambda b, i: (0, 0))],
        out_specs=[pl.BlockSpec((None, T, BLK), lambda b, i: (b, i, 0)),
                   pl.BlockSpec((None, T // TB, LANES, TB), lambda b, i: (b, i, 0, 0))],
        out_shape=[jax.ShapeDtypeStruct((B, S, BLK), BF16),
                   jax.ShapeDtypeStruct((B, S // TB, LANES, TB), BF16)],
        compiler_params=pltpu.CompilerParams(dimension_semantics=("arbitrary", "arbitrary"),
                                             vmem_limit_bytes=VMEM_LIMIT),
        name="nsa_cmp_topk",
    )(proj3, kc2, vc2, ovt)


def _stick_kernel(q_all_ref, k_ref, vt_ref, u_ref, o_ref, za_ref, zb_ref, w_ref, carry_ref,
                  acc_ref):
    T = TB
    krow = lax.broadcasted_iota(jnp.int32, (T, T), 0)
    qcol = lax.broadcasted_iota(jnp.int32, (T, T), 1)
    strict = krow < qcol
    u = u_ref[...]
    z_refs = (za_ref, zb_ref)

    def query_block(i, carry):
        qm = _masked_head_rows(q_all_ref.at[i])

        def stage(j, slot):
            rows = pl.ds(pl.multiple_of(j * T, T), T)
            for h in range(N_HEADS):
                z_refs[slot][h] = _dot(k_ref[rows, (h // 2) * LANES:(h // 2 + 1) * LANES],
                                       qm[h])

        def apply_weights(j):
            vt_all = vt_ref[j]
            for h in range(N_HEADS):
                acc_ref[h] = acc_ref[h] + _dot(vt_all[h * HEAD_DIM:(h + 1) * HEAD_DIM, :], w_ref[h])

        def absorb(j, slot, diag, staged_next, applied_prev):
            z_ref = z_refs[slot]
            softplus = []
            for h in range(N_HEADS):
                z = z_ref[h]
                sp = jnp.maximum(z, 0.0) + jnp.log2(1.0 + jnp.exp2(-jnp.abs(z)))
                if diag:
                    sp = jnp.where(strict, sp, 0.0)
                softplus.append(sp.astype(BF16))
            if staged_next is not None:
                stage(staged_next, 1 - slot)
            if applied_prev is not None:
                apply_weights(applied_prev)
            sums = [_dot(u, sp) for sp in softplus]
            for h in range(N_HEADS):
                a = jnp.exp2(z_ref[h] - (sums[h] + carry_ref[h]))
                if diag:
                    a = jnp.where(strict, a, 0.0)
                w_ref[h] = a.astype(BF16)
                carry_ref[h] = carry_ref[h] + sums[h][0:1, :]

        carry_ref[...] = jnp.zeros(carry_ref.shape, F32)
        acc_ref[...] = jnp.zeros(acc_ref.shape, F32)
        stage(i, 0)
        absorb(i, 0, True, jnp.maximum(i - 1, 0), None)

        def body(t, c):
            j = i - 1 - 2 * t
            absorb(j, 1, False, j - 1, j + 1)
            absorb(j - 1, 0, False, jnp.maximum(j - 2, 0), j)
            return c

        lax.fori_loop(0, i // 2, body, 0)

        @pl.when(i % 2 == 1)
        def _():
            absorb(0, 1, False, None, 1)

        apply_weights(0)
        q_rows = pl.ds(pl.multiple_of(i * T, T), T)
        for pair in range(2):
            o_t = jnp.concatenate([acc_ref[2 * pair], acc_ref[2 * pair + 1]], axis=0)
            o_ref[q_rows, pair * LANES:(pair + 1) * LANES] = o_t.T.astype(BF16)
        return carry

    lax.fori_loop(0, q_all_ref.shape[0], query_block, 0)


def _stick_breaking(qdt, proj3, vdt, B, S):
    u = np.triu(np.ones((TB, TB), np.float32))
    return pl.pallas_call(
        _stick_kernel,
        grid=(B,),
        in_specs=[pl.BlockSpec((None,) + qdt.shape[1:], lambda b: (b, 0, 0, 0)),
                  pl.BlockSpec((None, S, BLK), lambda b: (b, 0, PB_DK)),
                  pl.BlockSpec((None,) + vdt.shape[1:], lambda b: (b, 0, 0, 0)),
                  pl.BlockSpec(u.shape, lambda b: (0, 0))],
        out_specs=pl.BlockSpec((None, S, BLK), lambda b: (b, 0, 0)),
        out_shape=jax.ShapeDtypeStruct((B, S, BLK), BF16),
        scratch_shapes=[pltpu.VMEM((N_HEADS, TB, TB), F32),
                        pltpu.VMEM((N_HEADS, TB, TB), F32),
                        pltpu.VMEM((N_HEADS, TB, TB), BF16),
                        pltpu.VMEM((N_HEADS, 1, TB), F32),
                        pltpu.VMEM((N_HEADS, HEAD_DIM, TB), F32)],
        compiler_params=pltpu.CompilerParams(dimension_semantics=("arbitrary",),
                                             vmem_limit_bytes=VMEM_LIMIT),
        name="stick_breaking",
    )(qdt, proj3, vdt, jnp.asarray(u, BF16))


def _outproj_kernel(x_ref, oa, ob, oc, os_, ow, od,
                    ga, gb, gl, gc, gd, expand_ref, w_ref, g_ref, out_ref):
    f = lambda r: r[...].astype(F32)
    gates = _dot(gl[...], expand_ref[...])
    o_c = (gates[:, 0:BLK] * f(oc) + gates[:, BLK:2 * BLK] * f(os_)
           + gates[:, 2 * BLK:3 * BLK] * f(ow))
    mixed = [f(oa) * f(ga), f(ob) * f(gb), o_c * f(gc), f(od) * f(gd)]
    y = _dot(mixed[0].astype(BF16), w_ref[0:BLK, :])
    for g in range(1, 4):
        y = y + _dot(mixed[g].astype(BF16), w_ref[g * BLK:(g + 1) * BLK, :])
    ms = jnp.mean(y * y, axis=-1, keepdims=True)
    out_ref[...] = x_ref[...] + (y * lax.rsqrt(ms + EPS)) * g_ref[...]


def _out_projection(x2, mixer_outs, proj, w_out, g_post):
    rows = x2.shape[0]
    tm = TILE
    blk = lambda c: pl.BlockSpec((tm, BLK), lambda i: (i, c))
    expand = np.zeros((LANES, N_BRANCH * BLK), np.float32)
    for h in range(N_HEADS):
        for r in range(N_BRANCH):
            expand[N_BRANCH * h + r, r * BLK + h * HEAD_DIM:r * BLK + (h + 1) * HEAD_DIM] = 1.0
    gate_specs = [blk(PB_AG), blk(PB_BG),
                  pl.BlockSpec((tm, LANES), lambda i: (i, 2 * PB_C1 + 1)), blk(PB_CG), blk(PB_DG)]
    return pl.pallas_call(
        _outproj_kernel,
        grid=(rows // tm,),
        in_specs=[pl.BlockSpec((tm, D_MODEL), lambda i: (i, 0))]
                 + [blk(0)] * len(mixer_outs) + gate_specs
                 + [pl.BlockSpec(expand.shape, lambda i: (0, 0)),
                    pl.BlockSpec((D_MODEL, D_MODEL), lambda i: (0, 0)),
                    pl.BlockSpec((1, D_MODEL), lambda i: (0, 0))],
        out_specs=pl.BlockSpec((tm, D_MODEL), lambda i: (i, 0)),
        out_shape=jax.ShapeDtypeStruct((rows, D_MODEL), F32),
        compiler_params=pltpu.CompilerParams(dimension_semantics=("arbitrary",),
                                             vmem_limit_bytes=VMEM_LIMIT),
        name="out_projection",
    )(x2, *mixer_outs, *([proj] * len(gate_specs)), jnp.asarray(expand, BF16),
      w_out.astype(BF16), g_post.reshape(1, D_MODEL))


def _layer(x2, tabs, B, S, layer, w_rows, w_t, w_out, g_pre, g_post, mla_g_q, mla_g_kv,
           mla_w_uq, mla_w_ukv, nsa_pos_k, nsa_pos_v, nsa_k_w1, nsa_k_w2, nsa_v_w1, nsa_v_w2):
    rows = B * S
    aqkv, aqkv4, aqkv16, proj, vct, vdt, qct, qdt = _in_projection(
        x2, g_pre, w_rows, w_t, layer, tabs, B, S)
    proj3 = proj.reshape(B, S, N_PBLK * BLK)

    o_a = _dilated(aqkv, aqkv4, aqkv16, B, S)

    qnt, qrt, kn, vbt = _mla_up(proj, mla_g_q, mla_g_kv, mla_w_uq, mla_w_ukv, tabs, B, S)
    o_b = _flash_t(qnt, kn.reshape(B, S, BLK), 0, BLK, vbt, B=B, S=S,
                   aux=("per_head", qrt, proj3, 2 * PB_BKV + 1), name="mla_attention")

    kc2, vc2 = _compress(proj3, nsa_pos_k, nsa_pos_v, nsa_k_w1, nsa_k_w2, nsa_v_w1, nsa_v_w2, B, S)
    o_cmp, nsel = _cmp_topk(proj3, kc2, vc2, B, S)
    penalty = np.zeros((S, LANES), np.float32)
    penalty[np.arange(S), np.arange(S) // NSA_SEL_LEN] = MASK_PENALTY
    o_slc = _flash_t(qct, proj3, 2 * PB_C0 + 1, LANES, vct, B=B, S=S, vrow0=0,
                     aux=("shared", nsel, jnp.asarray(penalty, BF16), 0), name="nsa_selected")
    o_win = _flash_t(qct, proj3, 2 * PB_C1, LANES, vct, B=B, S=S, vrow0=HEAD_DIM,
                     window=NSA_WINDOW - 1, name="nsa_window")

    o_d = _stick_breaking(qdt, proj3, vdt, B, S)

    flat = lambda t: t.reshape(rows, BLK)
    outs = [flat(o) for o in (o_a, o_b, o_cmp, o_slc, o_win, o_d)]
    return _out_projection(x2, outs, proj, w_out, g_post)


def kernel(x, positions, w_in, w_out, g_pre, g_post, mla_g_q, mla_g_kv, mla_w_uq, mla_w_ukv,
           nsa_pos_k, nsa_pos_v, nsa_k_w1, nsa_k_w2, nsa_v_w1, nsa_v_w2):
    B, S, D = x.shape
    assert D == D_MODEL and S % 1024 == 0
    x2 = x.reshape(B * S, D)
    tabs = _rope_tables(positions)
    w_rows, w_t = _pack_w_in(w_in)
    for l in range(w_in.shape[0]):
        x2 = _layer(x2, tabs, B, S, l, w_rows, w_t, w_out[l], g_pre[l], g_post[l],
                    mla_g_q[l], mla_g_kv[l], mla_w_uq[l], mla_w_ukv[l],
                    nsa_pos_k[l], nsa_pos_v[l], nsa_k_w1[l], nsa_k_w2[l],
                    nsa_v_w1[l], nsa_v_w2[l])
    return x2.reshape(B, S, D)
```

```python
import functools
import math

import numpy as np
import jax
import jax.numpy as jnp
from jax import lax
from jax.experimental import pallas as pl
from jax.experimental.pallas import tpu as pltpu

D_MODEL = 1024
HEAD_DIM = 64
N_HEADS = 4
ROPE_THETA = 10000.0
EPS = 1e-6
NEG = -1e30
LOG2E = math.log2(math.e)
MASK_PENALTY = -(2.0 ** 100)
BIG = 1e9

MLA_Q_RANK = 256
MLA_KV_RANK = 128
MLA_NOPE = 64
MLA_ROPE = 32
MLA_V = 64

NSA_CMP_LEN = 32
NSA_CMP_STRIDE = 16
NSA_CMP_HID = 256
NSA_SEL_LEN = 64
NSA_N_SEL = 16
NSA_WINDOW = 512

LANES = 128
BLK = 256
TILE = 512
CM_TILE = 256
DILATIONS = (1, 4, 16)
DIL_BLOCK = 128
LSE_PARTS = 2
TB = 256
ONES_ROWS = 16
VMEM_LIMIT = 48 * 1024 * 1024

F32 = jnp.float32
BF16 = jnp.bfloat16

(PB_AG, PB_BCQ, PB_BKV, PB_BG, PB_CQ, PB_C0, PB_C1, PB_CG, PB_DQ, PB_DK, PB_DG) = range(11)
N_PBLK = 11
N_BRANCH = 3
ROPE_NONE, ROPE_64, ROPE_32, ROPE_64_LO, ACT_SILU, ACT_SIGMOID = range(6)


def _dot_nt(a, b):
    return lax.dot_general(a, b, (((1,), (1,)), ((), ())), preferred_element_type=F32)


def _dot(a, b):
    return jnp.dot(a, b, preferred_element_type=F32)


def _rope_half(y, cos, sin_signed, group):
    half = group // 2
    lane = lax.broadcasted_iota(jnp.int32, y.shape, 1)
    first = (lane & (group - 1)) < half
    partner = jnp.where(first, pltpu.roll(y, LANES - half, 1), pltpu.roll(y, half, 1))
    return y * cos + partner * sin_signed


def _rope_table_kernel(pos_ref, inv_ref, cos_ref, sin_ref):
    ang = inv_ref[...] * pos_ref[...].astype(F32)
    cos_ref[...] = jnp.cos(ang)
    sin_ref[...] = jnp.sin(ang)


def _rope_tables(positions):
    rows = positions.size
    tm = 2048
    inv = np.concatenate([ROPE_THETA ** (-np.arange(n, dtype=np.float64) / n)
                          for n in (HEAD_DIM // 2, MLA_ROPE // 2)]).astype(np.float32)[:, None]
    out = jax.ShapeDtypeStruct((inv.shape[0], rows), F32)
    tab_spec = pl.BlockSpec((inv.shape[0], tm), lambda i: (0, i))
    return pl.pallas_call(
        _rope_table_kernel,
        grid=(rows // tm,),
        in_specs=[pl.BlockSpec((1, tm), lambda i: (0, i)),
                  pl.BlockSpec(inv.shape, lambda i: (0, 0))],
        out_specs=[tab_spec] * 2,
        out_shape=[out] * 2,
        name="rope_tables",
    )(positions.reshape(1, rows), jnp.asarray(inv))


def _token_major_rope(cos_ref, sin_ref, group):
    n = group // 2
    lo = 0 if group == HEAD_DIM else HEAD_DIM // 2
    c = cos_ref[lo:lo + n, :]
    s = sin_ref[lo:lo + n, :]
    reps = LANES // n
    cos = jnp.concatenate([c] * reps, axis=0).T
    sin = jnp.concatenate([-s if k % 2 == 0 else s for k in range(reps)], axis=0).T
    return cos, sin


def _inproj_kernel(x_ref, g_ref, w_ref, wt_ref, perm_ref, cos_ref, sin_ref,
                   aqkv_ref, aqkv4_ref, aqkv16_ref, proj_ref, vct_ref, vdt_ref, qct_ref, qdt_ref,
                   *, plan_a, plan_p):
    x = x_ref[...]
    tm = x.shape[0]
    ms = jnp.mean(x * x, axis=-1, keepdims=True)
    h = ((x * lax.rsqrt(ms + EPS)) * g_ref[...]).astype(BF16)
    lane = lax.broadcasted_iota(jnp.int32, (tm, LANES), 1)
    c64, s64 = _token_major_rope(cos_ref, sin_ref, HEAD_DIM)
    c32, s32 = _token_major_rope(cos_ref, sin_ref, MLA_ROPE)

    def epilogue(y, ops, scale):
        halves = []
        for hf, op in enumerate(ops):
            yh = y[:, hf * LANES:(hf + 1) * LANES]
            if op == ROPE_64:
                yh = _rope_half(yh, c64, s64, 64)
            elif op == ROPE_32:
                yh = _rope_half(yh, c32, s32, 32)
            elif op == ROPE_64_LO:
                yh = jnp.where(lane < 64, _rope_half(yh, c64, s64, 64), yh)
            elif op == ACT_SILU:
                yh = yh * (1.0 / (1.0 + jnp.exp(-yh)))
            elif op == ACT_SIGMOID:
                yh = 1.0 / (1.0 + jnp.exp(-yh))
            if scale != 1.0:
                yh = yh * scale
            halves.append(yh)
        return jnp.concatenate(halves, axis=1)

    feature_major = {PB_CQ: qct_ref, PB_DQ: qdt_ref}
    col = 0
    for out_ref, plan in ((aqkv_ref, plan_a), (proj_ref, plan_p)):
        for b, (ops, scale) in enumerate(plan):
            y = epilogue(_dot(h, w_ref[:, col:col + BLK]), ops, scale)
            out_ref[:, b * BLK:(b + 1) * BLK] = y.astype(BF16)
            if out_ref is proj_ref and b in feature_major:
                y_t = y.T
                for blk in range(tm // TB):
                    feature_major[b][blk] = y_t[:, blk * TB:(blk + 1) * TB].astype(BF16)
            col += BLK

    for t in range(tm // CM_TILE):
        tile = slice(CM_TILE * t, CM_TILE * (t + 1))
        a_rows = aqkv_ref[tile, :]
        aqkv4_ref[tile, :] = _dot(perm_ref[0], a_rows).astype(BF16)
        aqkv16_ref[tile, :] = _dot(perm_ref[1], a_rows).astype(BF16)

    vt = _dot_nt(wt_ref[...], h)
    for blk in range(tm // TB):
        cols = slice(blk * TB, (blk + 1) * TB)
        vct_ref[blk] = vt[0:LANES, cols].astype(BF16)
        vdt_ref[blk] = vt[LANES:LANES + BLK, cols].astype(BF16)


_PLAN_A = (((ROPE_64, ROPE_64), LOG2E * HEAD_DIM ** -0.5),
           ((ROPE_64, ROPE_64), 1.0),
           ((ROPE_NONE, ROPE_NONE), 1.0))
_PLAN_P = (
    ((ACT_SILU, ACT_SILU), 1.0),
    ((ROPE_NONE, ROPE_NONE), 1.0),
    ((ROPE_NONE, ROPE_32), 1.0),
    ((ACT_SILU, ACT_SILU), 1.0),
    ((ROPE_64, ROPE_64), LOG2E * HEAD_DIM ** -0.5),
    ((ROPE_64_LO, ROPE_64), 1.0),
    ((ROPE_64, ACT_SIGMOID), 1.0),
    ((ACT_SILU, ACT_SILU), 1.0),
    ((ROPE_NONE, ROPE_NONE), LOG2E * HEAD_DIM ** -0.5),
    ((ROPE_NONE, ROPE_NONE), 1.0),
    ((ACT_SILU, ACT_SILU), 1.0),
)
assert len(_PLAN_P) == N_PBLK


_W_IN_SEGMENTS = (("aq", 256), ("ak", 256), ("av", 256), ("ag", 256),
                  ("bcq", 256), ("bckv", 128), ("bkr", 32), ("bg", 256),
                  ("cq", 256), ("ckc", 64), ("cvc", 64), ("cks", 64), ("cvs", 64),
                  ("ckw", 64), ("cvw", 64), ("cgl", 12), ("cg", 256),
                  ("dq", 256), ("dk", 256), ("dv", 256), ("dg", 256))
W_ROWS_COLS = (3 + N_PBLK) * BLK
W_T_ROWS = LANES + BLK


def _pack_w_kernel(wt_ref, rows_ref, t_ref):
    wt = wt_ref[...].astype(F32)
    o = 0
    seg = {}
    for name, width in _W_IN_SEGMENTS:
        seg[name] = wt[o:o + width, :]
        o += width
    gl_pad = jnp.zeros((LANES - N_HEADS * N_BRANCH, wt.shape[1]), wt.dtype)
    rows = [seg["aq"], seg["ak"], seg["av"],
            seg["ag"], seg["bcq"],
            seg["bckv"], seg["bkr"], seg["bkr"], seg["bkr"], seg["bkr"],
            seg["bg"], seg["cq"],
            seg["ckc"], seg["cvc"], seg["cks"], seg["cks"],
            seg["ckw"], seg["ckw"], seg["cgl"], gl_pad,
            seg["cg"], seg["dq"], seg["dk"], seg["dg"]]
    rows_ref[...] = jnp.concatenate(rows, axis=0).T.astype(BF16)
    t_ref[...] = jnp.concatenate([seg["cvs"], seg["cvw"], seg["dv"]], axis=0).astype(BF16)


def _pack_w_in(w_in):
    depth, d_model, d_in = w_in.shape
    assert d_model == D_MODEL and d_in == sum(width for _, width in _W_IN_SEGMENTS)
    tc = 256
    return pl.pallas_call(
        _pack_w_kernel,
        grid=(depth, D_MODEL // tc),
        in_specs=[pl.BlockSpec((None, d_in, tc), lambda l, i: (l, 0, i))],
        out_specs=[pl.BlockSpec((None, tc, W_ROWS_COLS), lambda l, i: (l, i, 0)),
                   pl.BlockSpec((None, W_T_ROWS, tc), lambda l, i: (l, 0, i))],
        out_shape=[jax.ShapeDtypeStruct((depth, D_MODEL, W_ROWS_COLS), BF16),
                   jax.ShapeDtypeStruct((depth, W_T_ROWS, D_MODEL), BF16)],
        compiler_params=pltpu.CompilerParams(dimension_semantics=("arbitrary", "arbitrary"),
                                             vmem_limit_bytes=VMEM_LIMIT),
        name="pack_w_in",
    )(jnp.transpose(w_in, (0, 2, 1)).astype(BF16))


def _class_major_perm(dil, tile):
    n = tile // dil
    p = np.zeros((tile, tile), np.float32)
    pos = np.arange(tile)
    p[(pos % dil) * n + pos // dil, pos] = 1.0
    return p


def _in_projection(x2, g_pre, w_rows, w_t, layer, tabs, B, S):
    rows = x2.shape[0]
    tm = TILE
    per_b = S // tm
    nkb = tm // TB
    perm = jnp.asarray(np.stack([_class_major_perm(d, CM_TILE) for d in DILATIONS[1:]]), BF16)
    tab_spec = pl.BlockSpec((tabs[0].shape[0], tm), lambda i: (0, i))
    vt_spec = lambda r: pl.BlockSpec((None, nkb, r, TB), lambda i: (i // per_b, i % per_b, 0, 0))
    a_spec = pl.BlockSpec((tm, 3 * BLK), lambda i: (i, 0))
    a_shape = jax.ShapeDtypeStruct((rows, 3 * BLK), BF16)
    return pl.pallas_call(
        functools.partial(_inproj_kernel, plan_a=_PLAN_A, plan_p=_PLAN_P),
        grid=(rows // tm,),
        in_specs=[pl.BlockSpec((tm, D_MODEL), lambda i: (i, 0)),
                  pl.BlockSpec((1, D_MODEL), lambda i: (0, 0)),
                  pl.BlockSpec((None,) + w_rows.shape[1:], lambda i: (layer, 0, 0)),
                  pl.BlockSpec((None,) + w_t.shape[1:], lambda i: (layer, 0, 0)),
                  pl.BlockSpec(perm.shape, lambda i: (0, 0, 0))] + [tab_spec] * 2,
        out_specs=[a_spec, a_spec, a_spec,
                   pl.BlockSpec((tm, N_PBLK * BLK), lambda i: (i, 0)),
                   vt_spec(LANES), vt_spec(BLK), vt_spec(BLK), vt_spec(BLK)],
        out_shape=[a_shape, a_shape, a_shape,
                   jax.ShapeDtypeStruct((rows, N_PBLK * BLK), BF16),
                   jax.ShapeDtypeStruct((B, S // TB, LANES, TB), BF16)]
                  + [jax.ShapeDtypeStruct((B, S // TB, BLK, TB), BF16)] * 3,
        compiler_params=pltpu.CompilerParams(dimension_semantics=("arbitrary",),
                                             vmem_limit_bytes=VMEM_LIMIT),
        name="in_projection",
    )(x2, g_pre.reshape(1, D_MODEL), w_rows, w_t, perm, *tabs)


def _head_masks(shape):
    lane = lax.broadcasted_iota(jnp.int32, shape, 1)
    return lane < HEAD_DIM


def _masked_heads(q_ref, lo_mask):
    out = []
    for h in range(N_HEADS):
        pair, e = divmod(h, 2)
        qp = q_ref[:, pair * LANES:(pair + 1) * LANES]
        keep = lo_mask if e == 0 else jnp.logical_not(lo_mask)
        out.append(jnp.where(keep, qp, jnp.zeros_like(qp)))
    return out


def _dilated_kernel(an_ref, a4_ref, a16_ref, unperm_ref, expand_ref, o_ref, res_ref, s_ref):
    T = DIL_BLOCK
    S = an_ref.shape[0]
    n_tiles = S // CM_TILE
    lo = _head_masks((T, LANES))
    lane = lax.broadcasted_iota(jnp.int32, (T, LANES), 1)
    dist2 = (lax.broadcasted_iota(jnp.int32, (T, 2 * T), 0)
             - lax.broadcasted_iota(jnp.int32, (T, 2 * T), 1))
    causal1 = (lax.broadcasted_iota(jnp.int32, (T, T), 1)
               <= lax.broadcasted_iota(jnp.int32, (T, T), 0))

    def scores(q, k):
        qm = _masked_heads(q, lo)
        return tuple(_dot_nt(qm[h], k[:, (h // 2) * LANES:(h // 2 + 1) * LANES])
                     for h in range(N_HEADS))

    def finish(n_keys, v, mask, stage_next):
        probs = []
        for h in range(N_HEADS):
            s = jnp.where(mask, s_ref[h, :, 0:n_keys], NEG)
            m = jnp.max(s, axis=-1, keepdims=True)
            p = jnp.exp2(s - m)
            probs.append((m, jnp.sum(p, axis=-1, keepdims=True), p.astype(BF16)))
        stage_next()
        outs = []
        top = jnp.zeros((T, LANES), F32)
        den = jnp.ones((T, LANES), F32)
        for h in range(N_HEADS):
            m, l, p = probs[h]
            outs.append(_dot(p, v[:, (h // 2) * LANES:(h // 2 + 1) * LANES]) * (1.0 / l))
            mine = (lane & (N_HEADS - 1)) == h
            top = jnp.where(mine, m, top)
            den = jnp.where(mine, l, den)
        lse = top + jnp.log2(den)
        hi = lse.astype(BF16).astype(F32)
        parts = jnp.where(lane < N_HEADS, hi,
                          jnp.where(lane < LSE_PARTS * N_HEADS, lse - hi, 0.0))
        return jnp.concatenate(
            [jnp.where(lo, outs[0], outs[1]), jnp.where(lo, outs[2], outs[3]), parts],
            axis=1).astype(BF16)

    def band_mask(delta):
        d = dist2 + delta
        return jnp.logical_and(d >= 0, d <= DIL_BLOCK)

    def run(n_units, n_keys, load_qk, load_v, store, mask_of):
        def stage(u):
            for h, s in enumerate(scores(*load_qk(u))):
                s_ref[h, :, 0:n_keys] = s

        def body(u, carry):
            nxt = jnp.minimum(u + 1, n_units - 1)
            store(u, finish(n_keys, load_v(u), mask_of(u), lambda: stage(nxt)))
            return carry

        stage(0)
        lax.fori_loop(0, n_units, body, 0)

    def rows(start, size):
        return pl.ds(pl.multiple_of(start, 16), size)

    def k0_1(u):
        return T * jnp.maximum(u - 1, 0)

    def store1(u, r):
        res_ref[0, rows(T * u, T), :] = r

    run(S // T, 2 * T,
        lambda u: (an_ref[rows(T * u, T), 0:BLK], an_ref[rows(k0_1(u), 2 * T), BLK:2 * BLK]),
        lambda u: an_ref[rows(k0_1(u), 2 * T), 2 * BLK:3 * BLK],
        store1,
        lambda u: band_mask(T * u - k0_1(u)))

    per4 = CM_TILE // 4
    nq4 = S // 4 // T

    def pieces4(u, blk):
        first_tile = blk * (T // per4)
        return [rows(CM_TILE * (first_tile + k) + per4 * (u // nq4), per4)
                for k in range(T // per4)]

    def kb0_4(u):
        return jnp.maximum(u % nq4 - 1, 0)

    def load4(u, blks, cols):
        return jnp.concatenate([a4_ref[p, cols] for blk in blks for p in pieces4(u, blk)], axis=0)

    def store4(u, r):
        for k, p in enumerate(pieces4(u, u % nq4)):
            res_ref[1, p, :] = r[per4 * k:per4 * (k + 1), :]

    run(4 * nq4, 2 * T,
        lambda u: (load4(u, [u % nq4], slice(0, BLK)),
                   load4(u, [kb0_4(u), kb0_4(u) + 1], slice(BLK, 2 * BLK))),
        lambda u: load4(u, [kb0_4(u), kb0_4(u) + 1], slice(2 * BLK, 3 * BLK)),
        store4,
        lambda u: band_mask(T * (u % nq4 - kb0_4(u))))

    per16 = CM_TILE // 16

    def load16(c, cols):
        return jnp.concatenate([a16_ref[rows(CM_TILE * t + per16 * c, per16), cols]
                                for t in range(n_tiles)], axis=0)

    def store16(c, r):
        for t in range(n_tiles):
            res_ref[2, rows(CM_TILE * t + per16 * c, per16), :] = r[per16 * t:per16 * (t + 1), :]

    run(16, T,
        lambda c: (load16(c, slice(0, BLK)), load16(c, slice(BLK, 2 * BLK))),
        lambda c: load16(c, slice(2 * BLK, 3 * BLK)),
        store16,
        lambda c: causal1)

    expand = expand_ref[...]
    for t in range(n_tiles):
        tile = slice(CM_TILE * t, CM_TILE * (t + 1))
        nat = [res_ref[0, tile, :].astype(F32),
               _dot(unperm_ref[0], res_ref[1, tile, :]),
               _dot(unperm_ref[1], res_ref[2, tile, :])]
        lse = [_dot(r[:, BLK:].astype(BF16), expand) for r in nat]
        top = jnp.maximum(lse[0], jnp.maximum(lse[1], lse[2]))
        w = [jnp.exp2(l - top) for l in lse]
        mixed = (w[0] * nat[0][:, :BLK] + w[1] * nat[1][:, :BLK] + w[2] * nat[2][:, :BLK])
        o_ref[tile, :] = (mixed / (w[0] + w[1] + w[2])).astype(BF16)


def _dilated(aqkv, aqkv4, aqkv16, B, S):
    assert S == 16 * DIL_BLOCK and S % CM_TILE == 0 and DIL_BLOCK % (CM_TILE // 4) == 0
    r3 = lambda t: t.reshape(B, S, 3 * BLK)
    unperm = jnp.asarray(np.stack([_class_major_perm(d, CM_TILE).T for d in DILATIONS[1:]]), BF16)
    expand = np.zeros((LANES, BLK), np.float32)
    for h in range(N_HEADS):
        for part in range(LSE_PARTS):
            expand[N_HEADS * part + h, h * HEAD_DIM:(h + 1) * HEAD_DIM] = 1.0
    a_spec = pl.BlockSpec((None, S, 3 * BLK), lambda b: (b, 0, 0))
    return pl.pallas_call(
        _dilated_kernel,
        grid=(B,),
        in_specs=[a_spec, a_spec, a_spec,
                  pl.BlockSpec(unperm.shape, lambda b: (0, 0, 0)),
                  pl.BlockSpec(expand.shape, lambda b: (0, 0))],
        out_specs=pl.BlockSpec((None, S, BLK), lambda b: (b, 0, 0)),
        out_shape=jax.ShapeDtypeStruct((B, S, BLK), BF16),
        scratch_shapes=[pltpu.VMEM((3, S, BLK + LANES), BF16),
                        pltpu.VMEM((N_HEADS, DIL_BLOCK, 2 * DIL_BLOCK), F32)],
        compiler_params=pltpu.CompilerParams(dimension_semantics=("arbitrary",),
                                             vmem_limit_bytes=VMEM_LIMIT),
        name="dilated_mixture",
    )(r3(aqkv), r3(aqkv4), r3(aqkv16), unperm, jnp.asarray(expand, BF16))


def _masked_head_rows(qt_ref):
    out = []
    zeros = jnp.zeros((HEAD_DIM, qt_ref.shape[-1]), BF16)
    for h in range(N_HEADS):
        own = qt_ref[h * HEAD_DIM:(h + 1) * HEAD_DIM, :]
        out.append(jnp.concatenate([own, zeros] if h % 2 == 0 else [zeros, own], axis=0))
    return out


def _flash_t_kernel(*refs, window, kshared, vrow0, aux):
    it = iter(refs)
    q_all_ref = next(it)
    qx_all_ref = next(it) if aux else None
    k_ref = next(it)
    kx_ref = next(it) if aux else None
    vt_ref = next(it)
    o_ref = next(it)
    s_ref, m_ref, acc_ref = next(it), next(it), next(it)
    T = TB

    krow = lax.broadcasted_iota(jnp.int32, (T, T), 0)
    qcol = lax.broadcasted_iota(jnp.int32, (T, T), 1)
    ones = jnp.ones((ONES_ROWS, T), BF16)

    n_q = q_all_ref.shape[0]
    if window is not None:
        o_full = (window - T + 1) // T
        o_none = (window + T - 1) // T + 1
        assert o_none - o_full == 2

    def query_weights(i):
        qcat = _masked_head_rows(q_all_ref.at[i])
        if aux == "per_head":
            group = LANES // N_HEADS
            qcat = [jnp.concatenate(
                [qcat[h]] + [qx_all_ref[i, g * group:(g + 1) * group, :] if g == h
                             else jnp.zeros((group, T), BF16) for g in range(N_HEADS)], axis=0)
                for h in range(N_HEADS)]
        elif aux == "shared":
            qx = qx_all_ref[i]
            qcat = [jnp.concatenate([qcat[h], qx], axis=0) for h in range(N_HEADS)]
        return qcat

    def first_full_block(i):
        return 0 if window is None else jnp.maximum(i - o_full, 0)

    def scores(j, qcat):
        rows = pl.ds(pl.multiple_of(j * T, T), T)
        out = []
        for h in range(N_HEADS):
            pair = h // 2
            cols = slice(0, LANES) if kshared else slice(pair * LANES, (pair + 1) * LANES)
            kb = k_ref[rows, cols]
            if aux:
                kb = jnp.concatenate([kb, kx_ref[rows, :]], axis=1)
            out.append(_dot(kb, qcat[h]))
        return out

    def stage(j, qcat):
        for h, s in enumerate(scores(j, qcat)):
            s_ref[h] = s

    def band_block(i):
        o = o_full + 1
        j = i - o
        limit = jnp.where(j >= 0, window - o * T, -2 * T)
        return jnp.maximum(j, 0), limit

    def stage_query_block(i):
        stage(first_full_block(i), query_weights(i))

    def query_block(i, carry):
        qcat = query_weights(i)

        def absorb(j, mask, staged_next=None, s_all=None):
            vt_all = vt_ref[j]
            probs = []
            for h in range(N_HEADS):
                s = s_ref[h] if s_all is None else s_all[h]
                if mask is not None:
                    s = jnp.where(mask, s, NEG)
                m = m_ref[h]
                m_new = jnp.maximum(m, jnp.max(s, axis=0, keepdims=True))
                alpha = jnp.exp2(m - m_new)
                p = jnp.exp2(s - m_new)
                probs.append((m_new, alpha, p.astype(BF16)))
            if staged_next is not None:
                staged_next()
            for h in range(N_HEADS):
                m_new, alpha, p = probs[h]
                r0 = vrow0 if kshared else h * HEAD_DIM
                lhs = jnp.concatenate([vt_all[r0:r0 + HEAD_DIM, :], ones], axis=0)
                acc_ref[h] = alpha * acc_ref[h] + _dot(lhs, p)
                m_ref[h] = m_new

        m_ref[...] = jnp.full(m_ref.shape, NEG, F32)
        acc_ref[...] = jnp.zeros(acc_ref.shape, F32)
        first = first_full_block(i)
        if window is not None:
            jc, limit = band_block(i)
            absorb(jc, (qcol - krow) <= limit, s_all=scores(jc, qcat))

        def body(j, c):
            absorb(j, None, staged_next=lambda: stage(j + 1, qcat))
            return c

        lax.fori_loop(first, i, body, 0)
        nxt = jnp.minimum(i + 1, n_q - 1)
        absorb(i, krow <= qcol, staged_next=lambda: stage_query_block(nxt))

        q_rows = pl.ds(pl.multiple_of(i * T, T), T)
        for pair in range(2):
            outs = []
            for e in range(2):
                acc = acc_ref[2 * pair + e]
                outs.append(acc[0:HEAD_DIM, :] * (1.0 / acc[HEAD_DIM:HEAD_DIM + 1, :]))
            o_ref[q_rows, pair * LANES:(pair + 1) * LANES] = (
                jnp.concatenate(outs, axis=0).T.astype(BF16))
        return carry

    stage_query_block(0)
    lax.fori_loop(0, n_q, query_block, 0)


def _flash_t(qt, k, kcol, kw, vt, *, B, S, window=None, vrow0=0, aux=None, name):
    kshared = kw == LANES
    whole = lambda a: pl.BlockSpec((None,) + a.shape[1:], lambda b: (b,) + (0,) * (a.ndim - 1))
    in_specs = [whole(qt)]
    args = [qt]
    if aux is not None:
        kind, qx, kx, kxcol = aux
        in_specs.append(whole(qx))
        args.append(qx)
    in_specs.append(pl.BlockSpec((None, S, kw), lambda b: (b, 0, kcol)))
    args.append(k)
    if aux is not None:
        if kx.ndim == 3:
            in_specs.append(pl.BlockSpec((None, S, LANES), lambda b: (b, 0, kxcol)))
        else:
            in_specs.append(pl.BlockSpec((S, LANES), lambda b: (0, kxcol)))
        args.append(kx)
    in_specs.append(whole(vt))
    args.append(vt)
    return pl.pallas_call(
        functools.partial(_flash_t_kernel, window=window, kshared=kshared, vrow0=vrow0,
                          aux=None if aux is None else aux[0]),
        grid=(B,),
        in_specs=in_specs,
        out_specs=pl.BlockSpec((None, S, BLK), lambda b: (b, 0, 0)),
        out_shape=jax.ShapeDtypeStruct((B, S, BLK), BF16),
        scratch_shapes=[pltpu.VMEM((N_HEADS, TB, TB), F32),
                        pltpu.VMEM((N_HEADS, 1, TB), F32),
                        pltpu.VMEM((N_HEADS, HEAD_DIM + ONES_ROWS, TB), F32)],
        compiler_params=pltpu.CompilerParams(dimension_semantics=("arbitrary",),
                                             vmem_limit_bytes=VMEM_LIMIT),
        name=name,
    )(*args)


def _mla_up_kernel(cq_ref, ckv_ref, gq_ref, gkv_ref, wq_ref, wk_ref, wvt_ref, cos_ref, sin_ref,
                   qn_ref, qr_ref, kn_ref, vt_ref):
    def norm(t, g):
        t = t.astype(F32)
        ms = jnp.mean(t * t, axis=-1, keepdims=True)
        return ((t * lax.rsqrt(ms + EPS)) * g).astype(BF16)

    scale = LOG2E * (MLA_NOPE + MLA_ROPE) ** -0.5
    q_t = _dot_nt(wq_ref[...], norm(cq_ref[...], gq_ref[...])) * scale
    qn_t = q_t[:BLK, :]
    half = MLA_ROPE // 2
    cos = cos_ref[HEAD_DIM // 2:HEAD_DIM // 2 + half, :]
    sin = sin_ref[HEAD_DIM // 2:HEAD_DIM // 2 + half, :]
    rot = []
    for h in range(N_HEADS):
        x1 = q_t[BLK + h * MLA_ROPE:BLK + h * MLA_ROPE + half, :]
        x2 = q_t[BLK + h * MLA_ROPE + half:BLK + (h + 1) * MLA_ROPE, :]
        rot += [x1 * cos - x2 * sin, x1 * sin + x2 * cos]
    qr_t = jnp.concatenate(rot, axis=0)
    ckv = norm(ckv_ref[...], gkv_ref[...])
    kn_ref[...] = _dot(ckv, wk_ref[...]).astype(BF16)
    vt = _dot_nt(wvt_ref[...], ckv)
    for blk in range(vt.shape[1] // TB):
        cols = slice(blk * TB, (blk + 1) * TB)
        qn_ref[blk] = qn_t[:, cols].astype(BF16)
        qr_ref[blk] = qr_t[:, cols].astype(BF16)
        vt_ref[blk] = vt[:, cols].astype(BF16)


def _mla_up(proj, g_q, g_kv, w_uq, w_ukv, tabs, B, S):
    rows = proj.shape[0]
    tm = 1024
    per_b = S // tm
    wq = w_uq.reshape(MLA_Q_RANK, N_HEADS, MLA_NOPE + MLA_ROPE)
    wq = jnp.concatenate([wq[:, :, :MLA_NOPE].reshape(MLA_Q_RANK, -1),
                          wq[:, :, MLA_NOPE:].reshape(MLA_Q_RANK, -1)], axis=1).T.astype(BF16)
    wkv = w_ukv.reshape(MLA_KV_RANK, N_HEADS, MLA_NOPE + MLA_V)
    wk = wkv[:, :, :MLA_NOPE].reshape(MLA_KV_RANK, -1).astype(BF16)
    wvt = wkv[:, :, MLA_NOPE:].reshape(MLA_KV_RANK, -1).T.astype(BF16)
    row_spec = lambda w, c: pl.BlockSpec((tm, w), lambda i: (i, c))
    full = lambda a: pl.BlockSpec(a.shape, lambda i: (0, 0))
    gq = g_q.reshape(1, -1)
    gkv = g_kv.reshape(1, -1)
    fm_shape = lambda r: jax.ShapeDtypeStruct((B, S // TB, r, TB), BF16)
    fm_spec = lambda r: pl.BlockSpec((None, tm // TB, r, TB),
                                     lambda i: (i // per_b, i % per_b, 0, 0))
    outs = [fm_shape(BLK), fm_shape(LANES), jax.ShapeDtypeStruct((rows, BLK), BF16), fm_shape(BLK)]
    return pl.pallas_call(
        _mla_up_kernel,
        grid=(rows // tm,),
        in_specs=[row_spec(BLK, PB_BCQ), row_spec(LANES, 2 * PB_BKV), full(gq), full(gkv),
                  full(wq), full(wk), full(wvt)]
                 + [pl.BlockSpec((tabs[0].shape[0], tm), lambda i: (0, i))] * 2,
        out_specs=[fm_spec(BLK), fm_spec(LANES), row_spec(BLK, 0), fm_spec(BLK)],
        out_shape=outs,
        compiler_params=pltpu.CompilerParams(dimension_semantics=("arbitrary",),
                                             vmem_limit_bytes=VMEM_LIMIT),
        name="mla_up",
    )(proj, proj, gq, gkv, wq, wk, wvt, *tabs)


def _compress_kernel(x_ref, perm_ref, pos_ref, w1_ref, w2_ref, kc_ref, vc_ref):
    n_tiles = x_ref.shape[0] // TILE
    per = TILE // NSA_CMP_STRIDE
    slabs = []
    for t in range(n_tiles):
        cm = _dot(perm_ref[...], x_ref[TILE * t:TILE * (t + 1), :])
        slabs.append(jnp.concatenate([cm[per * c:per * (c + 1), :]
                                      for c in range(NSA_CMP_STRIDE)], axis=1))
    x = jnp.concatenate(slabs, axis=0)
    half = x.shape[1]
    top = (x + pos_ref[0:1, :]).astype(BF16)
    bot = (x + pos_ref[1:2, :]).astype(BF16)
    y_top = _dot(top, w1_ref[0:half, :])
    y_bot = _dot(bot, w1_ref[half:2 * half, :])
    pre = y_top + pltpu.roll(y_bot, y_bot.shape[0] - 1, 0)
    hid = (pre * (1.0 / (1.0 + jnp.exp(-pre)))).astype(BF16)
    kc_ref[...] = _dot(hid[:, :NSA_CMP_HID], w2_ref[0]).astype(BF16)
    vc_ref[...] = _dot(hid[:, NSA_CMP_HID:], w2_ref[1]).astype(BF16)


def _compress(proj3, pos_k, pos_v, kw1, kw2, vw1, vw2, B, S):
    assert S % TILE == 0 and NSA_CMP_LEN == 2 * NSA_CMP_STRIDE
    n16 = S // NSA_CMP_STRIDE
    perm = jnp.asarray(_class_major_perm(NSA_CMP_STRIDE, TILE), BF16)
    pos = jnp.concatenate([pos_k, pos_v], axis=1).reshape(2, NSA_CMP_STRIDE * LANES)
    w1 = jnp.zeros((NSA_CMP_LEN, LANES, 2 * NSA_CMP_HID), F32)
    w1 = w1.at[:, :HEAD_DIM, :NSA_CMP_HID].set(kw1.reshape(NSA_CMP_LEN, HEAD_DIM, NSA_CMP_HID))
    w1 = w1.at[:, HEAD_DIM:, NSA_CMP_HID:].set(vw1.reshape(NSA_CMP_LEN, HEAD_DIM, NSA_CMP_HID))
    w1 = w1.reshape(NSA_CMP_LEN * LANES, 2 * NSA_CMP_HID).astype(BF16)
    w2 = jnp.stack([jnp.tile(kw2, (1, 2)), jnp.tile(vw2, (1, 2))]).astype(BF16)
    out = jax.ShapeDtypeStruct((B, n16, LANES), BF16)
    return pl.pallas_call(
        _compress_kernel,
        grid=(B,),
        in_specs=[pl.BlockSpec((None, S, LANES), lambda b: (b, 0, 2 * PB_C0)),
                  pl.BlockSpec(perm.shape, lambda b: (0, 0)),
                  pl.BlockSpec(pos.shape, lambda b: (0, 0)),
                  pl.BlockSpec(w1.shape, lambda b: (0, 0)),
                  pl.BlockSpec(w2.shape, lambda b: (0, 0, 0))],
        out_specs=[pl.BlockSpec((None, n16, LANES), lambda b: (b, 0, 0))] * 2,
        out_shape=[out, out],
        compiler_params=pltpu.CompilerParams(dimension_semantics=("arbitrary",),
                                             vmem_limit_bytes=VMEM_LIMIT),
        name="nsa_compress",
    )(proj3, perm, pos, w1, w2)


def _cmp_topk_kernel(q_ref, kc_ref, vc_ref, ovt_ref, o_ref, nsel_ref, *, T, n_cmp, n_slc):
    i = pl.program_id(1)
    t0 = i * T
    lane = lax.broadcasted_iota(jnp.int32, (T, LANES), 1)
    tok = t0 + lax.broadcasted_iota(jnp.int32, (T, LANES), 0)
    ok = jnp.logical_and(lane * NSA_CMP_STRIDE + (NSA_CMP_LEN - 1) <= tok, lane < n_cmp)
    lo = lane < HEAD_DIM
    qm = _masked_heads(q_ref, lo)
    kc = kc_ref[...]
    vc = vc_ref[...]
    psum = jnp.zeros((T, LANES), F32)
    outs = []
    for h in range(N_HEADS):
        s = jnp.where(ok, _dot_nt(qm[h], kc), NEG)
        mx = jnp.max(s, axis=-1, keepdims=True)
        e = jnp.where(ok, jnp.exp2(s - mx), 0.0)
        den = jnp.sum(e, axis=-1, keepdims=True)
        p = e * (1.0 / jnp.maximum(den, 1e-30))
        psum = psum + p
        outs.append(_dot(p.astype(BF16), vc))
    for pair in range(2):
        o_ref[:, pair * LANES:(pair + 1) * LANES] = jnp.where(
            lo, outs[2 * pair], outs[2 * pair + 1]).astype(BF16)

    p_hi = psum.astype(BF16)
    p_lo = (psum - p_hi.astype(F32)).astype(BF16)
    ovt = ovt_ref[...]
    imp = _dot_nt(ovt, p_hi) + _dot_nt(ovt, p_lo)
    jj = lax.broadcasted_iota(jnp.int32, (n_slc, T), 0)
    bt = (t0 + lax.broadcasted_iota(jnp.int32, (n_slc, T), 1)) >> 6
    forced = jnp.logical_or(jj == 0, jnp.logical_or(jj == bt, jj == bt - 1))
    val = jnp.where(forced, BIG, jnp.where(jj > bt, -BIG, imp))
    sel_rows = []
    for j in range(n_slc):
        vj = val[j:j + 1, :]
        beats = jnp.logical_or(val > vj, jnp.logical_and(val == vj, jj < j))
        rank = jnp.sum(beats.astype(F32), axis=0, keepdims=True)
        sel_rows.append((rank >= NSA_N_SEL).astype(F32))
    sel_rows.append(jnp.zeros((LANES - n_slc, T), F32))
    nsel = jnp.concatenate(sel_rows, axis=0).astype(BF16)
    for blk in range(T // TB):
        nsel_ref[blk] = nsel[:, blk * TB:(blk + 1) * TB]


def _cmp_topk(proj3, kc2, vc2, B, S):
    T = 4 * TB
    n_cmp =(S - NSA_CMP_LEN) // NSA_CMP_STRIDE + 1
    n_slc = S // NSA_SEL_LEN
    assert NSA_SEL_LEN == 64 and n_slc <= LANES and n_cmp <= LANES
    ci = np.arange(LANES)[None, :] * NSA_CMP_STRIDE
    sj = np.arange(n_slc)[:, None] * NSA_SEL_LEN
    ovt = ((ci < sj + NSA_SEL_LEN) & (ci + NSA_CMP_LEN > sj) & (np.arange(LANES)[None, :] < n_cmp))
    ovt = jnp.asarray(ovt.astype(np.float32), BF16)
    return pl.pallas_call(
        functools.partial(_cmp_topk_kernel, T=T, n_cmp=n_cmp, n_slc=n_slc),
        grid=(B, S // T),
        in_specs=[pl.BlockSpec((None, T, BLK), lambda b, i: (b, i, PB_CQ)),
                  pl.BlockSpec((None, LANES, LANES), lambda b, i: (b, 0, 0)),
                  pl.BlockSpec((None, LANES, LANES), lambda b, i: (b, 0, 0)),
                  pl.BlockSpec(ovt.shape, lambda b, i: (0, 0))],
        out_specs=[pl.BlockSpec((None, T, BLK), lambda b, i: (b, i, 0)),
                   pl.BlockSpec((None, T // TB, LANES, TB), lambda b, i: (b, i, 0, 0))],
        out_shape=[jax.ShapeDtypeStruct((B, S, BLK), BF16),
                   jax.ShapeDtypeStruct((B, S // TB, LANES, TB), BF16)],
        compiler_params=pltpu.CompilerParams(dimension_semantics=("arbitrary", "arbitrary"),
                                             vmem_limit_bytes=VMEM_LIMIT),
        name="nsa_cmp_topk",
    )(proj3, kc2, vc2, ovt)


def _stick_kernel(q_all_ref, k_ref, vt_ref, u_ref, o_ref, za_ref, zb_ref, w_ref, carry_ref,
                  acc_ref):
    T = TB
    krow = lax.broadcasted_iota(jnp.int32, (T, T), 0)
    qcol = lax.broadcasted_iota(jnp.int32, (T, T), 1)
    strict = krow < qcol
    u = u_ref[...]
    z_refs = (za_ref, zb_ref)

    def query_block(i, carry):
        qm = _masked_head_rows(q_all_ref.at[i])

        def stage(j, slot):
            rows = pl.ds(pl.multiple_of(j * T, T), T)
            for h in range(N_HEADS):
                z_refs[slot][h] = _dot(k_ref[rows, (h // 2) * LANES:(h // 2 + 1) * LANES],
                                       qm[h])

        def apply_weights(j):
            vt_all = vt_ref[j]
            for h in range(N_HEADS):
                acc_ref[h] = acc_ref[h] + _dot(vt_all[h * HEAD_DIM:(h + 1) * HEAD_DIM, :], w_ref[h])

        def absorb(j, slot, diag, staged_next, applied_prev):
            z_ref = z_refs[slot]
            softplus = []
            for h in range(N_HEADS):
                z = z_ref[h]
                sp = jnp.maximum(z, 0.0) + jnp.log2(1.0 + jnp.exp2(-jnp.abs(z)))
                if diag:
                    sp = jnp.where(strict, sp, 0.0)
                softplus.append(sp.astype(BF16))
            if staged_next is not None:
                stage(staged_next, 1 - slot)
            if applied_prev is not None:
                apply_weights(applied_prev)
            sums = [_dot(u, sp) for sp in softplus]
            for h in range(N_HEADS):
                a = jnp.exp2(z_ref[h] - (sums[h] + carry_ref[h]))
                if diag:
                    a = jnp.where(strict, a, 0.0)
                w_ref[h] = a.astype(BF16)
                carry_ref[h] = carry_ref[h] + sums[h][0:1, :]

        carry_ref[...] = jnp.zeros(carry_ref.shape, F32)
        acc_ref[...] = jnp.zeros(acc_ref.shape, F32)
        stage(i, 0)
        absorb(i, 0, True, jnp.maximum(i - 1, 0), None)

        def body(t, c):
            j = i - 1 - 2 * t
            absorb(j, 1, False, j - 1, j + 1)
            absorb(j - 1, 0, False, jnp.maximum(j - 2, 0), j)
            return c

        lax.fori_loop(0, i // 2, body, 0)

        @pl.when(i % 2 == 1)
        def _():
            absorb(0, 1, False, None, 1)

        apply_weights(0)
        q_rows = pl.ds(pl.multiple_of(i * T, T), T)
        for pair in range(2):
            o_t = jnp.concatenate([acc_ref[2 * pair], acc_ref[2 * pair + 1]], axis=0)
            o_ref[q_rows, pair * LANES:(pair + 1) * LANES] = o_t.T.astype(BF16)
        return carry

    lax.fori_loop(0, q_all_ref.shape[0], query_block, 0)


def _stick_breaking(qdt, proj3, vdt, B, S):
    u = np.triu(np.ones((TB, TB), np.float32))
    return pl.pallas_call(
        _stick_kernel,
        grid=(B,),
        in_specs=[pl.BlockSpec((None,) + qdt.shape[1:], lambda b: (b, 0, 0, 0)),
                  pl.BlockSpec((None, S, BLK), lambda b: (b, 0, PB_DK)),
                  pl.BlockSpec((None,) + vdt.shape[1:], lambda b: (b, 0, 0, 0)),
                  pl.BlockSpec(u.shape, lambda b: (0, 0))],
        out_specs=pl.BlockSpec((None, S, BLK), lambda b: (b, 0, 0)),
        out_shape=jax.ShapeDtypeStruct((B, S, BLK), BF16),
        scratch_shapes=[pltpu.VMEM((N_HEADS, TB, TB), F32),
                        pltpu.VMEM((N_HEADS, TB, TB), F32),
                        pltpu.VMEM((N_HEADS, TB, TB), BF16),
                        pltpu.VMEM((N_HEADS, 1, TB), F32),
                        pltpu.VMEM((N_HEADS, HEAD_DIM, TB), F32)],
        compiler_params=pltpu.CompilerParams(dimension_semantics=("arbitrary",),
                                             vmem_limit_bytes=VMEM_LIMIT),
        name="stick_breaking",
    )(qdt, proj3, vdt, jnp.asarray(u, BF16))


def _outproj_kernel(x_ref, oa, ob, oc, os_, ow, od,
                    ga, gb, gl, gc, gd, expand_ref, w_ref, g_ref, out_ref):
    f = lambda r: r[...].astype(F32)
    gates = _dot(gl[...], expand_ref[...])
    o_c = (gates[:, 0:BLK] * f(oc) + gates[:, BLK:2 * BLK] * f(os_)
           + gates[:, 2 * BLK:3 * BLK] * f(ow))
    mixed = [f(oa) * f(ga), f(ob) * f(gb), o_c * f(gc), f(od) * f(gd)]
    y = _dot(mixed[0].astype(BF16), w_ref[0:BLK, :])
    for g in range(1, 4):
        y = y + _dot(mixed[g].astype(BF16), w_ref[g * BLK:(g + 1) * BLK, :])
    ms = jnp.mean(y * y, axis=-1, keepdims=True)
    out_ref[...] = x_ref[...] + (y * lax.rsqrt(ms + EPS)) * g_ref[...]


def _out_projection(x2, mixer_outs, proj, w_out, g_post):
    rows = x2.shape[0]
    tm = TILE
    blk = lambda c: pl.BlockSpec((tm, BLK), lambda i: (i, c))
    expand = np.zeros((LANES, N_BRANCH * BLK), np.float32)
    for h in range(N_HEADS):
        for r in range(N_BRANCH):
            expand[N_BRANCH * h + r, r * BLK + h * HEAD_DIM:r * BLK + (h + 1) * HEAD_DIM] = 1.0
    gate_specs = [blk(PB_AG), blk(PB_BG),
                  pl.BlockSpec((tm, LANES), lambda i: (i, 2 * PB_C1 + 1)), blk(PB_CG), blk(PB_DG)]
    return pl.pallas_call(
        _outproj_kernel,
        grid=(rows // tm,),
        in_specs=[pl.BlockSpec((tm, D_MODEL), lambda i: (i, 0))]
                 + [blk(0)] * len(mixer_outs) + gate_specs
                 + [pl.BlockSpec(expand.shape, lambda i: (0, 0)),
                    pl.BlockSpec((D_MODEL, D_MODEL), lambda i: (0, 0)),
                    pl.BlockSpec((1, D_MODEL), lambda i: (0, 0))],
        out_specs=pl.BlockSpec((tm, D_MODEL), lambda i: (i, 0)),
        out_shape=jax.ShapeDtypeStruct((rows, D_MODEL), F32),
        compiler_params=pltpu.CompilerParams(dimension_semantics=("arbitrary",),
                                             vmem_limit_bytes=VMEM_LIMIT),
        name="out_projection",
    )(x2, *mixer_outs, *([proj] * len(gate_specs)), jnp.asarray(expand, BF16),
      w_out.astype(BF16), g_post.reshape(1, D_MODEL))


def _layer(x2, tabs, B, S, layer, w_rows, w_t, w_out, g_pre, g_post, mla_g_q, mla_g_kv,
           mla_w_uq, mla_w_ukv, nsa_pos_k, nsa_pos_v, nsa_k_w1, nsa_k_w2, nsa_v_w1, nsa_v_w2):
    rows = B * S
    aqkv, aqkv4, aqkv16, proj, vct, vdt, qct, qdt = _in_projection(
        x2, g_pre, w_rows, w_t, layer, tabs, B, S)
    proj3 = proj.reshape(B, S, N_PBLK * BLK)

    o_a = _dilated(aqkv, aqkv4, aqkv16, B, S)

    qnt, qrt, kn, vbt = _mla_up(proj, mla_g_q, mla_g_kv, mla_w_uq, mla_w_ukv, tabs, B, S)
    o_b = _flash_t(qnt, kn.reshape(B, S, BLK), 0, BLK, vbt, B=B, S=S,
                   aux=("per_head", qrt, proj3, 2 * PB_BKV + 1), name="mla_attention")

    kc2, vc2 = _compress(proj3, nsa_pos_k, nsa_pos_v, nsa_k_w1, nsa_k_w2, nsa_v_w1, nsa_v_w2, B, S)
    o_cmp, nsel = _cmp_topk(proj3, kc2, vc2, B, S)
    penalty = np.zeros((S, LANES), np.float32)
    penalty[np.arange(S), np.arange(S) // NSA_SEL_LEN] = MASK_PENALTY
    o_slc = _flash_t(qct, proj3, 2 * PB_C0 + 1, LANES, vct, B=B, S=S, vrow0=0,
                     aux=("shared", nsel, jnp.asarray(penalty, BF16), 0), name="nsa_selected")
    o_win = _flash_t(qct, proj3, 2 * PB_C1, LANES, vct, B=B, S=S, vrow0=HEAD_DIM,
                     window=NSA_WINDOW - 1, name="nsa_window")

    o_d = _stick_breaking(qdt, proj3, vdt, B, S)

    flat = lambda t: t.reshape(rows, BLK)
    outs = [flat(o) for o in (o_a, o_b, o_cmp, o_slc, o_win, o_d)]
    return _out_projection(x2, outs, proj, w_out, g_post)


def kernel(x, positions, w_in, w_out, g_pre, g_post, mla_g_q, mla_g_kv, mla_w_uq, mla_w_ukv,
           nsa_pos_k, nsa_pos_v, nsa_k_w1, nsa_k_w2, nsa_v_w1, nsa_v_w2):
    B, S, D = x.shape
    assert D == D_MODEL and S % 1024 == 0
    x2 = x.reshape(B * S, D)
    tabs = _rope_tables(positions)
    w_rows, w_t = _pack_w_in(w_in)
    for l in range(w_in.shape[0]):
        x2 = _layer(x2, tabs, B, S, l, w_rows, w_t, w_out[l], g_pre[l], g_post[l],
                    mla_g_q[l], mla_g_kv[l], mla_w_uq[l], mla_w_ukv[l],
                    nsa_pos_k[l], nsa_pos_v[l], nsa_k_w1[l], nsa_k_w2[l],
                    nsa_v_w1[l], nsa_v_w2[l])
    return x2.reshape(B, S, D)
```

```python
import functools
import math

import numpy as np
import jax
import jax.numpy as jnp
from jax import lax
from jax.experimental import pallas as pl
from jax.experimental.pallas import tpu as pltpu

D_MODEL = 1024
HEAD_DIM = 64
N_HEADS = 4
ROPE_THETA = 10000.0
EPS = 1e-6
NEG = -1e30
LOG2E = math.log2(math.e)
MASK_PENALTY = -(2.0 ** 100)
BIG = 1e9

MLA_Q_RANK = 256
MLA_KV_RANK = 128
MLA_NOPE = 64
MLA_ROPE = 32
MLA_V = 64

NSA_CMP_LEN = 32
NSA_CMP_STRIDE = 16
NSA_CMP_HID = 256
NSA_SEL_LEN = 64
NSA_N_SEL = 16
NSA_WINDOW = 512

LANES = 128
BLK = 256
TILE = 512
CM_TILE = 256
DILATIONS = (1, 4, 16)
DIL_BLOCK = 128
LSE_PARTS = 2
TB = 256
ONES_ROWS = 16
VMEM_LIMIT = 48 * 1024 * 1024

F32 = jnp.float32
BF16 = jnp.bfloat16

(PB_AG, PB_BCQ, PB_BKV, PB_BG, PB_CQ, PB_C0, PB_C1, PB_CG, PB_DQ, PB_DK, PB_DG) = range(11)
N_PBLK = 11
N_BRANCH = 3
ROPE_NONE, ROPE_64, ROPE_32, ROPE_64_LO, ACT_SILU, ACT_SIGMOID = range(6)


def _dot_nt(a, b):
    return lax.dot_general(a, b, (((1,), (1,)), ((), ())), preferred_element_type=F32)


def _dot(a, b):
    return jnp.dot(a, b, preferred_element_type=F32)


def _rope_half(y, cos, sin_signed, group):
    half = group // 2
    lane = lax.broadcasted_iota(jnp.int32, y.shape, 1)
    first = (lane & (group - 1)) < half
    partner = jnp.where(first, pltpu.roll(y, LANES - half, 1), pltpu.roll(y, half, 1))
    return y * cos + partner * sin_signed


def _rope_table_kernel(pos_ref, inv_ref, cos_ref, sin_ref):
    ang = inv_ref[...] * pos_ref[...].astype(F32)
    cos_ref[...] = jnp.cos(ang)
    sin_ref[...] = jnp.sin(ang)


def _rope_tables(positions):
    rows = positions.size
    tm = 2048
    inv = np.concatenate([ROPE_THETA ** (-np.arange(n, dtype=np.float64) / n)
                          for n in (HEAD_DIM // 2, MLA_ROPE // 2)]).astype(np.float32)[:, None]
    out = jax.ShapeDtypeStruct((inv.shape[0], rows), F32)
    tab_spec = pl.BlockSpec((inv.shape[0], tm), lambda i: (0, i))
    return pl.pallas_call(
        _rope_table_kernel,
        grid=(rows // tm,),
        in_specs=[pl.BlockSpec((1, tm), lambda i: (0, i)),
                  pl.BlockSpec(inv.shape, lambda i: (0, 0))],
        out_specs=[tab_spec] * 2,
        out_shape=[out] * 2,
        name="rope_tables",
    )(positions.reshape(1, rows), jnp.asarray(inv))


def _token_major_rope(cos_ref, sin_ref, group):
    n = group // 2
    lo = 0 if group == HEAD_DIM else HEAD_DIM // 2
    c = cos_ref[lo:lo + n, :]
    s = sin_ref[lo:lo + n, :]
    reps = LANES // n
    cos = jnp.concatenate([c] * reps, axis=0).T
    sin = jnp.concatenate([-s if k % 2 == 0 else s for k in range(reps)], axis=0).T
    return cos, sin


def _inproj_kernel(x_ref, g_ref, w_ref, wt_ref, perm_ref, cos_ref, sin_ref,
                   aqkv_ref, aqkv4_ref, aqkv16_ref, proj_ref, vct_ref, vdt_ref, qct_ref, qdt_ref,
                   *, plan_a, plan_p):
    x = x_ref[...]
    tm = x.shape[0]
    ms = jnp.mean(x * x, axis=-1, keepdims=True)
    h = ((x * lax.rsqrt(ms + EPS)) * g_ref[...]).astype(BF16)
    lane = lax.broadcasted_iota(jnp.int32, (tm, LANES), 1)
    c64, s64 = _token_major_rope(cos_ref, sin_ref, HEAD_DIM)
    c32, s32 = _token_major_rope(cos_ref, sin_ref, MLA_ROPE)

    def epilogue(y, ops, scale):
        halves = []
        for hf, op in enumerate(ops):
            yh = y[:, hf * LANES:(hf + 1) * LANES]
            if op == ROPE_64:
                yh = _rope_half(yh, c64, s64, 64)
            elif op == ROPE_32:
                yh = _rope_half(yh, c32, s32, 32)
            elif op == ROPE_64_LO:
                yh = jnp.where(lane < 64, _rope_half(yh, c64, s64, 64), yh)
            elif op == ACT_SILU:
                yh = yh * (1.0 / (1.0 + jnp.exp(-yh)))
            elif op == ACT_SIGMOID:
                yh = 1.0 / (1.0 + jnp.exp(-yh))
            if scale != 1.0:
                yh = yh * scale
            halves.append(yh)
        return jnp.concatenate(halves, axis=1)

    feature_major = {PB_CQ: qct_ref, PB_DQ: qdt_ref}
    col = 0
    for out_ref, plan in ((aqkv_ref, plan_a), (proj_ref, plan_p)):
        for b, (ops, scale) in enumerate(plan):
            y = epilogue(_dot(h, w_ref[:, col:col + BLK]), ops, scale)
            out_ref[:, b * BLK:(b + 1) * BLK] = y.astype(BF16)
            if out_ref is proj_ref and b in feature_major:
                y_t = y.T
                for blk in range(tm // TB):
                    feature_major[b][blk] = y_t[:, blk * TB:(blk + 1) * TB].astype(BF16)
            col += BLK

    for t in range(tm // CM_TILE):
        tile = slice(CM_TILE * t, CM_TILE * (t + 1))
        a_rows = aqkv_ref[tile, :]
        aqkv4_ref[tile, :] = _dot(perm_ref[0], a_rows).astype(BF16)
        aqkv16_ref[tile, :] = _dot(perm_ref[1], a_rows).astype(BF16)

    vt = _dot_nt(wt_ref[...], h)
    for blk in range(tm // TB):
        cols = slice(blk * TB, (blk + 1) * TB)
        vct_ref[blk] = vt[0:LANES, cols].astype(BF16)
        vdt_ref[blk] = vt[LANES:LANES + BLK, cols].astype(BF16)


_PLAN_A = (((ROPE_64, ROPE_64), LOG2E * HEAD_DIM ** -0.5),
           ((ROPE_64, ROPE_64), 1.0),
           ((ROPE_NONE, ROPE_NONE), 1.0))
_PLAN_P = (
    ((ACT_SILU, ACT_SILU), 1.0),
    ((ROPE_NONE, ROPE_NONE), 1.0),
    ((ROPE_NONE, ROPE_32), 1.0),
    ((ACT_SILU, ACT_SILU), 1.0),
    ((ROPE_64, ROPE_64), LOG2E * HEAD_DIM ** -0.5),
    ((ROPE_64_LO, ROPE_64), 1.0),
    ((ROPE_64, ACT_SIGMOID), 1.0),
    ((ACT_SILU, ACT_SILU), 1.0),
    ((ROPE_NONE, ROPE_NONE), LOG2E * HEAD_DIM ** -0.5),
    ((ROPE_NONE, ROPE_NONE), 1.0),
    ((ACT_SILU, ACT_SILU), 1.0),
)
assert len(_PLAN_P) == N_PBLK


_W_IN_SEGMENTS = (("aq", 256), ("ak", 256), ("av", 256), ("ag", 256),
                  ("bcq", 256), ("bckv", 128), ("bkr", 32), ("bg", 256),
                  ("cq", 256), ("ckc", 64), ("cvc", 64), ("cks", 64), ("cvs", 64),
                  ("ckw", 64), ("cvw", 64), ("cgl", 12), ("cg", 256),
                  ("dq", 256), ("dk", 256), ("dv", 256), ("dg", 256))
W_ROWS_COLS = (3 + N_PBLK) * BLK
W_T_ROWS = LANES + BLK


def _pack_w_kernel(wt_ref, rows_ref, t_ref):
    wt = wt_ref[...].astype(F32)
    o = 0
    seg = {}
    for name, width in _W_IN_SEGMENTS:
        seg[name] = wt[o:o + width, :]
        o += width
    gl_pad = jnp.zeros((LANES - N_HEADS * N_BRANCH, wt.shape[1]), wt.dtype)
    rows = [seg["aq"], seg["ak"], seg["av"],
            seg["ag"], seg["bcq"],
            seg["bckv"], seg["bkr"], seg["bkr"], seg["bkr"], seg["bkr"],
            seg["bg"], seg["cq"],
            seg["ckc"], seg["cvc"], seg["cks"], seg["cks"],
            seg["ckw"], seg["ckw"], seg["cgl"], gl_pad,
            seg["cg"], seg["dq"], seg["dk"], seg["dg"]]
    rows_ref[...] = jnp.concatenate(rows, axis=0).T.astype(BF16)
    t_ref[...] = jnp.concatenate([seg["cvs"], seg["cvw"], seg["dv"]], axis=0).astype(BF16)


def _pack_w_in(w_in):
    depth, d_model, d_in = w_in.shape
    assert d_model == D_MODEL and d_in == sum(width for _, width in _W_IN_SEGMENTS)
    tc = 256
    return pl.pallas_call(
        _pack_w_kernel,
        grid=(depth, D_MODEL // tc),
        in_specs=[pl.BlockSpec((None, d_in, tc), lambda l, i: (l, 0, i))],
        out_specs=[pl.BlockSpec((None, tc, W_ROWS_COLS), lambda l, i: (l, i, 0)),
                   pl.BlockSpec((None, W_T_ROWS, tc), lambda l, i: (l, 0, i))],
        out_shape=[jax.ShapeDtypeStruct((depth, D_MODEL, W_ROWS_COLS), BF16),
                   jax.ShapeDtypeStruct((depth, W_T_ROWS, D_MODEL), BF16)],
        compiler_params=pltpu.CompilerParams(dimension_semantics=("arbitrary", "arbitrary"),
                                             vmem_limit_bytes=VMEM_LIMIT),
        name="pack_w_in",
    )(jnp.transpose(w_in, (0, 2, 1)).astype(BF16))


def _class_major_perm(dil, tile):
    n = tile // dil
    p = np.zeros((tile, tile), np.float32)
    pos = np.arange(tile)
    p[(pos % dil) * n + pos // dil, pos] = 1.0
    return p


def _in_projection(x2, g_pre, w_rows, w_t, layer, tabs, B, S):
    rows = x2.shape[0]
    tm = TILE
    per_b = S // tm
    nkb = tm // TB
    perm = jnp.asarray(np.stack([_class_major_perm(d, CM_TILE) for d in DILATIONS[1:]]), BF16)
    tab_spec = pl.BlockSpec((tabs[0].shape[0], tm), lambda i: (0, i))
    vt_spec = lambda r: pl.BlockSpec((None, nkb, r, TB), lambda i: (i // per_b, i % per_b, 0, 0))
    a_spec = pl.BlockSpec((tm, 3 * BLK), lambda i: (i, 0))
    a_shape = jax.ShapeDtypeStruct((rows, 3 * BLK), BF16)
    return pl.pallas_call(
        functools.partial(_inproj_kernel, plan_a=_PLAN_A, plan_p=_PLAN_P),
        grid=(rows // tm,),
        in_specs=[pl.BlockSpec((tm, D_MODEL), lambda i: (i, 0)),
                  pl.BlockSpec((1, D_MODEL), lambda i: (0, 0)),
                  pl.BlockSpec((None,) + w_rows.shape[1:], lambda i: (layer, 0, 0)),
                  pl.BlockSpec((None,) + w_t.shape[1:], lambda i: (layer, 0, 0)),
                  pl.BlockSpec(perm.shape, lambda i: (0, 0, 0))] + [tab_spec] * 2,
        out_specs=[a_spec, a_spec, a_spec,
                   pl.BlockSpec((tm, N_PBLK * BLK), lambda i: (i, 0)),
                   vt_spec(LANES), vt_spec(BLK), vt_spec(BLK), vt_spec(BLK)],
        out_shape=[a_shape, a_shape, a_shape,
                   jax.ShapeDtypeStruct((rows, N_PBLK * BLK), BF16),
                   jax.ShapeDtypeStruct((B, S // TB, LANES, TB), BF16)]
                  + [jax.ShapeDtypeStruct((B, S // TB, BLK, TB), BF16)] * 3,
        compiler_params=pltpu.CompilerParams(dimension_semantics=("arbitrary",),
                                             vmem_limit_bytes=VMEM_LIMIT),
        name="in_projection",
    )(x2, g_pre.reshape(1, D_MODEL), w_rows, w_t, perm, *tabs)


def _head_masks(shape):
    lane = lax.broadcasted_iota(jnp.int32, shape, 1)
    return lane < HEAD_DIM


def _masked_heads(q_ref, lo_mask):
    out = []
    for h in range(N_HEADS):
        pair, e = divmod(h, 2)
        qp = q_ref[:, pair * LANES:(pair + 1) * LANES]
        keep = lo_mask if e == 0 else jnp.logical_not(lo_mask)
        out.append(jnp.where(keep, qp, jnp.zeros_like(qp)))
    return out


def _dilated_kernel(an_ref, a4_ref, a16_ref, unperm_ref, expand_ref, o_ref, res_ref, s_ref):
    T = DIL_BLOCK
    S = an_ref.shape[0]
    n_tiles = S // CM_TILE
    lo = _head_masks((T, LANES))
    lane = lax.broadcasted_iota(jnp.int32, (T, LANES), 1)
    dist2 = (lax.broadcasted_iota(jnp.int32, (T, 2 * T), 0)
             - lax.broadcasted_iota(jnp.int32, (T, 2 * T), 1))
    causal1 = (lax.broadcasted_iota(jnp.int32, (T, T), 1)
               <= lax.broadcasted_iota(jnp.int32, (T, T), 0))

    def scores(q, k):
        qm = _masked_heads(q, lo)
        return tuple(_dot_nt(qm[h], k[:, (h // 2) * LANES:(h // 2 + 1) * LANES])
                     for h in range(N_HEADS))

    def finish(n_keys, v, mask, stage_next):
        probs = []
        for h in range(N_HEADS):
            s = jnp.where(mask, s_ref[h, :, 0:n_keys], NEG)
            m = jnp.max(s, axis=-1, keepdims=True)
            p = jnp.exp2(s - m)
            probs.append((m, jnp.sum(p, axis=-1, keepdims=True), p.astype(BF16)))
        stage_next()
        outs = []
        top = jnp.zeros((T, LANES), F32)
        den = jnp.ones((T, LANES), F32)
        for h in range(N_HEADS):
            m, l, p = probs[h]
            outs.append(_dot(p, v[:, (h // 2) * LANES:(h // 2 + 1) * LANES]) * (1.0 / l))
            mine = (lane & (N_HEADS - 1)) == h
            top = jnp.where(mine, m, top)
            den = jnp.where(mine, l, den)
        lse = top + jnp.log2(den)
        hi = lse.astype(BF16).astype(F32)
        parts = jnp.where(lane < N_HEADS, hi,
                          jnp.where(lane < LSE_PARTS * N_HEADS, lse - hi, 0.0))
        return jnp.concatenate(
            [jnp.where(lo, outs[0], outs[1]), jnp.where(lo, outs[2], outs[3]), parts],
            axis=1).astype(BF16)

    def band_mask(delta):
        d = dist2 + delta
        return jnp.logical_and(d >= 0, d <= DIL_BLOCK)

    def run(n_units, n_keys, load_qk, load_v, store, mask_of):
        def stage(u):
            for h, s in enumerate(scores(*load_qk(u))):
                s_ref[h, :, 0:n_keys] = s

        def body(u, carry):
            nxt = jnp.minimum(u + 1, n_units - 1)
            store(u, finish(n_keys, load_v(u), mask_of(u), lambda: stage(nxt)))
            return carry

        stage(0)
        lax.fori_loop(0, n_units, body, 0)

    def rows(start, size):
        return pl.ds(pl.multiple_of(start, 16), size)

    def k0_1(u):
        return T * jnp.maximum(u - 1, 0)

    def store1(u, r):
        res_ref[0, rows(T * u, T), :] = r

    run(S // T, 2 * T,
        lambda u: (an_ref[rows(T * u, T), 0:BLK], an_ref[rows(k0_1(u), 2 * T), BLK:2 * BLK]),
        lambda u: an_ref[rows(k0_1(u), 2 * T), 2 * BLK:3 * BLK],
        store1,
        lambda u: band_mask(T * u - k0_1(u)))

    per4 = CM_TILE // 4
    nq4 = S // 4 // T

    def pieces4(u, blk):
        first_tile = blk * (T // per4)
        return [rows(CM_TILE * (first_tile + k) + per4 * (u // nq4), per4)
                for k in range(T // per4)]

    def kb0_4(u):
        return jnp.maximum(u % nq4 - 1, 0)

    def load4(u, blks, cols):
        return jnp.concatenate([a4_ref[p, cols] for blk in blks for p in pieces4(u, blk)], axis=0)

    def store4(u, r):
        for k, p in enumerate(pieces4(u, u % nq4)):
            res_ref[1, p, :] = r[per4 * k:per4 * (k + 1), :]

    run(4 * nq4, 2 * T,
        lambda u: (load4(u, [u % nq4], slice(0, BLK)),
                   load4(u, [kb0_4(u), kb0_4(u) + 1], slice(BLK, 2 * BLK))),
        lambda u: load4(u, [kb0_4(u), kb0_4(u) + 1], slice(2 * BLK, 3 * BLK)),
        store4,
        lambda u: band_mask(T * (u % nq4 - kb0_4(u))))

    per16 = CM_TILE // 16

    def load16(c, cols):
        return jnp.concatenate([a16_ref[rows(CM_TILE * t + per16 * c, per16), cols]
                                for t in range(n_tiles)], axis=0)

    def store16(c, r):
        for t in range(n_tiles):
            res_ref[2, rows(CM_TILE * t + per16 * c, per16), :] = r[per16 * t:per16 * (t + 1), :]

    run(16, T,
        lambda c: (load16(c, slice(0, BLK)), load16(c, slice(BLK, 2 * BLK))),
        lambda c: load16(c, slice(2 * BLK, 3 * BLK)),
        store16,
        lambda c: causal1)

    expand = expand_ref[...]
    for t in range(n_tiles):
        tile = slice(CM_TILE * t, CM_TILE * (t + 1))
        nat = [res_ref[0, tile, :].astype(F32),
               _dot(unperm_ref[0], res_ref[1, tile, :]),
               _dot(unperm_ref[1], res_ref[2, tile, :])]
        lse = [_dot(r[:, BLK:].astype(BF16), expand) for r in nat]
        top = jnp.maximum(lse[0], jnp.maximum(lse[1], lse[2]))
        w = [jnp.exp2(l - top) for l in lse]
        mixed = (w[0] * nat[0][:, :BLK] + w[1] * nat[1][:, :BLK] + w[2] * nat[2][:, :BLK])
        o_ref[tile, :] = (mixed / (w[0] + w[1] + w[2])).astype(BF16)


def _dilated(aqkv, aqkv4, aqkv16, B, S):
    assert S == 16 * DIL_BLOCK and S % CM_TILE == 0 and DIL_BLOCK % (CM_TILE // 4) == 0
    r3 = lambda t: t.reshape(B, S, 3 * BLK)
    unperm = jnp.asarray(np.stack([_class_major_perm(d, CM_TILE).T for d in DILATIONS[1:]]), BF16)
    expand = np.zeros((LANES, BLK), np.float32)
    for h in range(N_HEADS):
        for part in range(LSE_PARTS):
            expand[N_HEADS * part + h, h * HEAD_DIM:(h + 1) * HEAD_DIM] = 1.0
    a_spec = pl.BlockSpec((None, S, 3 * BLK), lambda b: (b, 0, 0))
    return pl.pallas_call(
        _dilated_kernel,
        grid=(B,),
        in_specs=[a_spec, a_spec, a_spec,
                  pl.BlockSpec(unperm.shape, lambda b: (0, 0, 0)),
                  pl.BlockSpec(expand.shape, lambda b: (0, 0))],
        out_specs=pl.BlockSpec((None, S, BLK), lambda b: (b, 0, 0)),
        out_shape=jax.ShapeDtypeStruct((B, S, BLK), BF16),
        scratch_shapes=[pltpu.VMEM((3, S, BLK + LANES), BF16),
                        pltpu.VMEM((N_HEADS, DIL_BLOCK, 2 * DIL_BLOCK), F32)],
        compiler_params=pltpu.CompilerParams(dimension_semantics=("arbitrary",),
                                             vmem_limit_bytes=VMEM_LIMIT),
        name="dilated_mixture",
    )(r3(aqkv), r3(aqkv4), r3(aqkv16), unperm, jnp.asarray(expand, BF16))


def _masked_head_rows(qt_ref):
    out = []
    zeros = jnp.zeros((HEAD_DIM, qt_ref.shape[-1]), BF16)
    for h in range(N_HEADS):
        own = qt_ref[h * HEAD_DIM:(h + 1) * HEAD_DIM, :]
        out.append(jnp.concatenate([own, zeros] if h % 2 == 0 else [zeros, own], axis=0))
    return out


def _flash_t_kernel(*refs, window, kshared, vrow0, aux):
    it = iter(refs)
    q_all_ref = next(it)
    qx_all_ref = next(it) if aux else None
    k_ref = next(it)
    kx_ref = next(it) if aux else None
    vt_ref = next(it)
    o_ref = next(it)
    s_ref, m_ref, acc_ref = next(it), next(it), next(it)
    T = TB

    krow = lax.broadcasted_iota(jnp.int32, (T, T), 0)
    qcol = lax.broadcasted_iota(jnp.int32, (T, T), 1)
    ones = jnp.ones((ONES_ROWS, T), BF16)

    n_q = q_all_ref.shape[0]
    cross_stage = window is None
    if window is not None:
        o_full = (window - T + 1) // T
        o_none = (window + T - 1) // T + 1
        assert o_none - o_full == 2

    def query_weights(i):
        qcat = _masked_head_rows(q_all_ref.at[i])
        if aux == "per_head":
            group = LANES // N_HEADS
            qcat = [jnp.concatenate(
                [qcat[h]] + [qx_all_ref[i, g * group:(g + 1) * group, :] if g == h
                             else jnp.zeros((group, T), BF16) for g in range(N_HEADS)], axis=0)
                for h in range(N_HEADS)]
        elif aux == "shared":
            qx = qx_all_ref[i]
            qcat = [jnp.concatenate([qcat[h], qx], axis=0) for h in range(N_HEADS)]
        return qcat

    def first_full_block(i):
        return 0 if window is None else jnp.maximum(i - o_full, 0)

    def scores(j, qcat):
        rows = pl.ds(pl.multiple_of(j * T, T), T)
        out = []
        for h in range(N_HEADS):
            pair = h // 2
            cols = slice(0, LANES) if kshared else slice(pair * LANES, (pair + 1) * LANES)
            kb = k_ref[rows, cols]
            if aux:
                kb = jnp.concatenate([kb, kx_ref[rows, :]], axis=1)
            out.append(_dot(kb, qcat[h]))
        return out

    def stage(j, qcat):
        for h, s in enumerate(scores(j, qcat)):
            s_ref[h] = s

    def band_block(i):
        o = o_full + 1
        j = i - o
        limit = jnp.where(j >= 0, window - o * T, -2 * T)
        return jnp.maximum(j, 0), limit

    def stage_query_block(i):
        stage(first_full_block(i), query_weights(i))

    def query_block(i, carry):
        qcat = query_weights(i)

        def absorb(j, mask, staged_next=None, s_all=None):
            vt_all = vt_ref[j]
            probs = []
            for h in range(N_HEADS):
                s = s_ref[h] if s_all is None else s_all[h]
                if mask is not None:
                    s = jnp.where(mask, s, NEG)
                m = m_ref[h]
                m_new = jnp.maximum(m, jnp.max(s, axis=0, keepdims=True))
                alpha = jnp.exp2(m - m_new)
                p = jnp.exp2(s - m_new)
                probs.append((m_new, alpha, p.astype(BF16)))
            if staged_next is not None:
                staged_next()
            for h in range(N_HEADS):
                m_new, alpha, p = probs[h]
                r0 = vrow0 if kshared else h * HEAD_DIM
                lhs = jnp.concatenate([vt_all[r0:r0 + HEAD_DIM, :], ones], axis=0)
                acc_ref[h] = alpha * acc_ref[h] + _dot(lhs, p)
                m_ref[h] = m_new

        m_ref[...] = jnp.full(m_ref.shape, NEG, F32)
        acc_ref[...] = jnp.zeros(acc_ref.shape, F32)
        first = first_full_block(i)
        if window is not None:
            jc, limit = band_block(i)
            s_band = scores(jc, qcat)
            stage(first, qcat)
            absorb(jc, (qcol - krow) <= limit, s_all=s_band)

        def body(j, c):
            absorb(j, None, staged_next=lambda: stage(j + 1, qcat))
            return c

        lax.fori_loop(first, i, body, 0)
        if cross_stage:
            nxt = jnp.minimum(i + 1, n_q - 1)
            absorb(i, krow <= qcol, staged_next=lambda: stage_query_block(nxt))
        else:
            absorb(i, krow <= qcol)

        q_rows = pl.ds(pl.multiple_of(i * T, T), T)
        for pair in range(2):
            outs = []
            for e in range(2):
                acc = acc_ref[2 * pair + e]
                outs.append(acc[0:HEAD_DIM, :] * (1.0 / acc[HEAD_DIM:HEAD_DIM + 1, :]))
            o_ref[q_rows, pair * LANES:(pair + 1) * LANES] = (
                jnp.concatenate(outs, axis=0).T.astype(BF16))
        return carry

    if cross_stage:
        stage_query_block(0)
    lax.fori_loop(0, n_q, query_block, 0)


def _flash_t(qt, k, kcol, kw, vt, *, B, S, window=None, vrow0=0, aux=None, name):
    kshared = kw == LANES
    whole = lambda a: pl.BlockSpec((None,) + a.shape[1:], lambda b: (b,) + (0,) * (a.ndim - 1))
    in_specs = [whole(qt)]
    args = [qt]
    if aux is not None:
        kind, qx, kx, kxcol = aux
        in_specs.append(whole(qx))
        args.append(qx)
    in_specs.append(pl.BlockSpec((None, S, kw), lambda b: (b, 0, kcol)))
    args.append(k)
    if aux is not None:
        if kx.ndim == 3:
            in_specs.append(pl.BlockSpec((None, S, LANES), lambda b: (b, 0, kxcol)))
        else:
            in_specs.append(pl.BlockSpec((S, LANES), lambda b: (0, kxcol)))
        args.append(kx)
    in_specs.append(whole(vt))
    args.append(vt)
    return pl.pallas_call(
        functools.partial(_flash_t_kernel, window=window, kshared=kshared, vrow0=vrow0,
                          aux=None if aux is None else aux[0]),
        grid=(B,),
        in_specs=in_specs,
        out_specs=pl.BlockSpec((None, S, BLK), lambda b: (b, 0, 0)),
        out_shape=jax.ShapeDtypeStruct((B, S, BLK), BF16),
        scratch_shapes=[pltpu.VMEM((N_HEADS, TB, TB), F32),
                        pltpu.VMEM((N_HEADS, 1, TB), F32),
                        pltpu.VMEM((N_HEADS, HEAD_DIM + ONES_ROWS, TB), F32)],
        compiler_params=pltpu.CompilerParams(dimension_semantics=("arbitrary",),
                                             vmem_limit_bytes=VMEM_LIMIT),
        name=name,
    )(*args)


def _mla_up_kernel(cq_ref, ckv_ref, gq_ref, gkv_ref, wq_ref, wk_ref, wvt_ref, cos_ref, sin_ref,
                   qn_ref, qr_ref, kn_ref, vt_ref):
    def norm(t, g):
        t = t.astype(F32)
        ms = jnp.mean(t * t, axis=-1, keepdims=True)
        return ((t * lax.rsqrt(ms + EPS)) * g).astype(BF16)

    scale = LOG2E * (MLA_NOPE + MLA_ROPE) ** -0.5
    q_t = _dot_nt(wq_ref[...], norm(cq_ref[...], gq_ref[...])) * scale
    qn_t = q_t[:BLK, :]
    half = MLA_ROPE // 2
    cos = cos_ref[HEAD_DIM // 2:HEAD_DIM // 2 + half, :]
    sin = sin_ref[HEAD_DIM // 2:HEAD_DIM // 2 + half, :]
    rot = []
    for h in range(N_HEADS):
        x1 = q_t[BLK + h * MLA_ROPE:BLK + h * MLA_ROPE + half, :]
        x2 = q_t[BLK + h * MLA_ROPE + half:BLK + (h + 1) * MLA_ROPE, :]
        rot += [x1 * cos - x2 * sin, x1 * sin + x2 * cos]
    qr_t = jnp.concatenate(rot, axis=0)
    ckv = norm(ckv_ref[...], gkv_ref[...])
    kn_ref[...] = _dot(ckv, wk_ref[...]).astype(BF16)
    vt = _dot_nt(wvt_ref[...], ckv)
    for blk in range(vt.shape[1] // TB):
        cols = slice(blk * TB, (blk + 1) * TB)
        qn_ref[blk] = qn_t[:, cols].astype(BF16)
        qr_ref[blk] = qr_t[:, cols].astype(BF16)
        vt_ref[blk] = vt[:, cols].astype(BF16)


def _mla_up(proj, g_q, g_kv, w_uq, w_ukv, tabs, B, S):
    rows = proj.shape[0]
    tm = 1024
    per_b = S // tm
    wq = w_uq.reshape(MLA_Q_RANK, N_HEADS, MLA_NOPE + MLA_ROPE)
    wq = jnp.concatenate([wq[:, :, :MLA_NOPE].reshape(MLA_Q_RANK, -1),
                          wq[:, :, MLA_NOPE:].reshape(MLA_Q_RANK, -1)], axis=1).T.astype(BF16)
    wkv = w_ukv.reshape(MLA_KV_RANK, N_HEADS, MLA_NOPE + MLA_V)
    wk = wkv[:, :, :MLA_NOPE].reshape(MLA_KV_RANK, -1).astype(BF16)
    wvt = wkv[:, :, MLA_NOPE:].reshape(MLA_KV_RANK, -1).T.astype(BF16)
    row_spec = lambda w, c: pl.BlockSpec((tm, w), lambda i: (i, c))
    full = lambda a: pl.BlockSpec(a.shape, lambda i: (0, 0))
    gq = g_q.reshape(1, -1)
    gkv = g_kv.reshape(1, -1)
    fm_shape = lambda r: jax.ShapeDtypeStruct((B, S // TB, r, TB), BF16)
    fm_spec = lambda r: pl.BlockSpec((None, tm // TB, r, TB),
                                     lambda i: (i // per_b, i % per_b, 0, 0))
    outs = [fm_shape(BLK), fm_shape(LANES), jax.ShapeDtypeStruct((rows, BLK), BF16), fm_shape(BLK)]
    return pl.pallas_call(
        _mla_up_kernel,
        grid=(rows // tm,),
        in_specs=[row_spec(BLK, PB_BCQ), row_spec(LANES, 2 * PB_BKV), full(gq), full(gkv),
                  full(wq), full(wk), full(wvt)]
                 + [pl.BlockSpec((tabs[0].shape[0], tm), lambda i: (0, i))] * 2,
        out_specs=[fm_spec(BLK), fm_spec(LANES), row_spec(BLK, 0), fm_spec(BLK)],
        out_shape=outs,
        compiler_params=pltpu.CompilerParams(dimension_semantics=("arbitrary",),
                                             vmem_limit_bytes=VMEM_LIMIT),
        name="mla_up",
    )(proj, proj, gq, gkv, wq, wk, wvt, *tabs)


def _compress_kernel(x_ref, perm_ref, pos_ref, w1_ref, w2_ref, kc_ref, vc_ref):
    n_tiles = x_ref.shape[0] // TILE
    per = TILE // NSA_CMP_STRIDE
    slabs = []
    for t in range(n_tiles):
        cm = _dot(perm_ref[...], x_ref[TILE * t:TILE * (t + 1), :])
        slabs.append(jnp.concatenate([cm[per * c:per * (c + 1), :]
                                      for c in range(NSA_CMP_STRIDE)], axis=1))
    x = jnp.concatenate(slabs, axis=0)
    half = x.shape[1]
    top = (x + pos_ref[0:1, :]).astype(BF16)
    bot = (x + pos_ref[1:2, :]).astype(BF16)
    y_top = _dot(top, w1_ref[0:half, :])
    y_bot = _dot(bot, w1_ref[half:2 * half, :])
    pre = y_top + pltpu.roll(y_bot, y_bot.shape[0] - 1, 0)
    hid = (pre * (1.0 / (1.0 + jnp.exp(-pre)))).astype(BF16)
    kc_ref[...] = _dot(hid[:, :NSA_CMP_HID], w2_ref[0]).astype(BF16)
    vc_ref[...] = _dot(hid[:, NSA_CMP_HID:], w2_ref[1]).astype(BF16)


def _compress(proj3, pos_k, pos_v, kw1, kw2, vw1, vw2, B, S):
    assert S % TILE == 0 and NSA_CMP_LEN == 2 * NSA_CMP_STRIDE
    n16 = S // NSA_CMP_STRIDE
    perm = jnp.asarray(_class_major_perm(NSA_CMP_STRIDE, TILE), BF16)
    pos = jnp.concatenate([pos_k, pos_v], axis=1).reshape(2, NSA_CMP_STRIDE * LANES)
    w1 = jnp.zeros((NSA_CMP_LEN, LANES, 2 * NSA_CMP_HID), F32)
    w1 = w1.at[:, :HEAD_DIM, :NSA_CMP_HID].set(kw1.reshape(NSA_CMP_LEN, HEAD_DIM, NSA_CMP_HID))
    w1 = w1.at[:, HEAD_DIM:, NSA_CMP_HID:].set(vw1.reshape(NSA_CMP_LEN, HEAD_DIM, NSA_CMP_HID))
    w1 = w1.reshape(NSA_CMP_LEN * LANES, 2 * NSA_CMP_HID).astype(BF16)
    w2 = jnp.stack([jnp.tile(kw2, (1, 2)), jnp.tile(vw2, (1, 2))]).astype(BF16)
    out = jax.ShapeDtypeStruct((B, n16, LANES), BF16)
    return pl.pallas_call(
        _compress_kernel,
        grid=(B,),
        in_specs=[pl.BlockSpec((None, S, LANES), lambda b: (b, 0, 2 * PB_C0)),
                  pl.BlockSpec(perm.shape, lambda b: (0, 0)),
                  pl.BlockSpec(pos.shape, lambda b: (0, 0)),
                  pl.BlockSpec(w1.shape, lambda b: (0, 0)),
                  pl.BlockSpec(w2.shape, lambda b: (0, 0, 0))],
        out_specs=[pl.BlockSpec((None, n16, LANES), lambda b: (b, 0, 0))] * 2,
        out_shape=[out, out],
        compiler_params=pltpu.CompilerParams(dimension_semantics=("arbitrary",),
                                             vmem_limit_bytes=VMEM_LIMIT),
        name="nsa_compress",
    )(proj3, perm, pos, w1, w2)


def _cmp_topk_kernel(q_ref, kc_ref, vc_ref, ovt_ref, o_ref, nsel_ref, *, T, n_cmp, n_slc):
    i = pl.program_id(1)
    t0 = i * T
    lane = lax.broadcasted_iota(jnp.int32, (T, LANES), 1)
    tok = t0 + lax.broadcasted_iota(jnp.int32, (T, LANES), 0)
    ok = jnp.logical_and(lane * NSA_CMP_STRIDE + (NSA_CMP_LEN - 1) <= tok, lane < n_cmp)
    lo = lane < HEAD_DIM
    qm = _masked_heads(q_ref, lo)
    kc = kc_ref[...]
    vc = vc_ref[...]
    psum = jnp.zeros((T, LANES), F32)
    outs = []
    for h in range(N_HEADS):
        s = jnp.where(ok, _dot_nt(qm[h], kc), NEG)
        mx = jnp.max(s, axis=-1, keepdims=True)
        e = jnp.where(ok, jnp.exp2(s - mx), 0.0)
        den = jnp.sum(e, axis=-1, keepdims=True)
        p = e * (1.0 / jnp.maximum(den, 1e-30))
        psum = psum + p
        outs.append(_dot(p.astype(BF16), vc))
    for pair in range(2):
        o_ref[:, pair * LANES:(pair + 1) * LANES] = jnp.where(
            lo, outs[2 * pair], outs[2 * pair + 1]).astype(BF16)

    p_hi = psum.astype(BF16)
    p_lo = (psum - p_hi.astype(F32)).astype(BF16)
    ovt = ovt_ref[...]
    imp = _dot_nt(ovt, p_hi) + _dot_nt(ovt, p_lo)
    jj = lax.broadcasted_iota(jnp.int32, (n_slc, T), 0)
    bt = (t0 + lax.broadcasted_iota(jnp.int32, (n_slc, T), 1)) >> 6
    forced = jnp.logical_or(jj == 0, jnp.logical_or(jj == bt, jj == bt - 1))
    val = jnp.where(forced, BIG, jnp.where(jj > bt, -BIG, imp))
    sel_rows = []
    for j in range(n_slc):
        vj = val[j:j + 1, :]
        beats = jnp.logical_or(val > vj, jnp.logical_and(val == vj, jj < j))
        rank = jnp.sum(beats.astype(F32), axis=0, keepdims=True)
        sel_rows.append((rank >= NSA_N_SEL).astype(F32))
    sel_rows.append(jnp.zeros((LANES - n_slc, T), F32))
    nsel = jnp.concatenate(sel_rows, axis=0).astype(BF16)
    for blk in range(T // TB):
        nsel_ref[blk] = nsel[:, blk * TB:(blk + 1) * TB]


def _cmp_topk(proj3, kc2, vc2, B, S):
    T = 4 * TB
    n_cmp =(S - NSA_CMP_LEN) // NSA_CMP_STRIDE + 1
    n_slc = S // NSA_SEL_LEN
    assert NSA_SEL_LEN == 64 and n_slc <= LANES and n_cmp <= LANES
    ci = np.arange(LANES)[None, :] * NSA_CMP_STRIDE
    sj = np.arange(n_slc)[:, None] * NSA_SEL_LEN
    ovt = ((ci < sj + NSA_SEL_LEN) & (ci + NSA_CMP_LEN > sj) & (np.arange(LANES)[None, :] < n_cmp))
    ovt = jnp.asarray(ovt.astype(np.float32), BF16)
    return pl.pallas_call(
        functools.partial(_cmp_topk_kernel, T=T, n_cmp=n_cmp, n_slc=n_slc),
        grid=(B, S // T),
        in_specs=[pl.BlockSpec((None, T, BLK), lambda b, i: (b, i, PB_CQ)),
                  pl.BlockSpec((None, LANES, LANES), lambda b, i: (b, 0, 0)),
                  pl.BlockSpec((None, LANES, LANES), lambda b, i: (b, 0, 0)),
                  pl.BlockSpec(ovt.shape, lambda b, i: (0, 0))],
        out_specs=[pl.BlockSpec((None, T, BLK), lambda b, i: (b, i, 0)),
                   pl.BlockSpec((None, T // TB, LANES, TB), lambda b, i: (b, i, 0, 0))],
        out_shape=[jax.ShapeDtypeStruct((B, S, BLK), BF16),
                   jax.ShapeDtypeStruct((B, S // TB, LANES, TB), BF16)],
        compiler_params=pltpu.CompilerParams(dimension_semantics=("arbitrary", "arbitrary"),
                                             vmem_limit_bytes=VMEM_LIMIT),
        name="nsa_cmp_topk",
    )(proj3, kc2, vc2, ovt)


def _stick_kernel(q_all_ref, k_ref, vt_ref, u_ref, o_ref, za_ref, zb_ref, w_ref, carry_ref,
                  acc_ref):
    T = TB
    krow = lax.broadcasted_iota(jnp.int32, (T, T), 0)
    qcol = lax.broadcasted_iota(jnp.int32, (T, T), 1)
    strict = krow < qcol
    u = u_ref[...]
    z_refs = (za_ref, zb_ref)

    def query_block(i, carry):
        qm = _masked_head_rows(q_all_ref.at[i])

        def stage(j, slot):
            rows = pl.ds(pl.multiple_of(j * T, T), T)
            for h in range(N_HEADS):
                z_refs[slot][h] = _dot(k_ref[rows, (h // 2) * LANES:(h // 2 + 1) * LANES],
                                       qm[h])

        def apply_weights(j):
            vt_all = vt_ref[j]
            for h in range(N_HEADS):
                acc_ref[h] = acc_ref[h] + _dot(vt_all[h * HEAD_DIM:(h + 1) * HEAD_DIM, :], w_ref[h])

        def absorb(j, slot, diag, staged_next, applied_prev):
            z_ref = z_refs[slot]
            softplus = []
            for h in range(N_HEADS):
                z = z_ref[h]
                sp = jnp.maximum(z, 0.0) + jnp.log2(1.0 + jnp.exp2(-jnp.abs(z)))
                if diag:
                    sp = jnp.where(strict, sp, 0.0)
                softplus.append(sp.astype(BF16))
            if staged_next is not None:
                stage(staged_next, 1 - slot)
            if applied_prev is not None:
                apply_weights(applied_prev)
            sums = [_dot(u, sp) for sp in softplus]
            for h in range(N_HEADS):
                a = jnp.exp2(z_ref[h] - (sums[h] + carry_ref[h]))
                if diag:
                    a = jnp.where(strict, a, 0.0)
                w_ref[h] = a.astype(BF16)
                carry_ref[h] = carry_ref[h] + sums[h][0:1, :]

        carry_ref[...] = jnp.zeros(carry_ref.shape, F32)
        acc_ref[...] = jnp.zeros(acc_ref.shape, F32)
        stage(i, 0)
        absorb(i, 0, True, jnp.maximum(i - 1, 0), None)

        def body(t, c):
            j = i - 1 - 2 * t
            absorb(j, 1, False, j - 1, j + 1)
            absorb(j - 1, 0, False, jnp.maximum(j - 2, 0), j)
            return c

        lax.fori_loop(0, i // 2, body, 0)

        @pl.when(i % 2 == 1)
        def _():
            absorb(0, 1, False, None, 1)

        apply_weights(0)
        q_rows = pl.ds(pl.multiple_of(i * T, T), T)
        for pair in range(2):
            o_t = jnp.concatenate([acc_ref[2 * pair], acc_ref[2 * pair + 1]], axis=0)
            o_ref[q_rows, pair * LANES:(pair + 1) * LANES] = o_t.T.astype(BF16)
        return carry

    lax.fori_loop(0, q_all_ref.shape[0], query_block, 0)


def _stick_breaking(qdt, proj3, vdt, B, S):
    u = np.triu(np.ones((TB, TB), np.float32))
    return pl.pallas_call(
        _stick_kernel,
        grid=(B,),
        in_specs=[pl.BlockSpec((None,) + qdt.shape[1:], lambda b: (b, 0, 0, 0)),
                  pl.BlockSpec((None, S, BLK), lambda b: (b, 0, PB_DK)),
                  pl.BlockSpec((None,) + vdt.shape[1:], lambda b: (b, 0, 0, 0)),
                  pl.BlockSpec(u.shape, lambda b: (0, 0))],
        out_specs=pl.BlockSpec((None, S, BLK), lambda b: (b, 0, 0)),
        out_shape=jax.ShapeDtypeStruct((B, S, BLK), BF16),
        scratch_shapes=[pltpu.VMEM((N_HEADS, TB, TB), F32),
                        pltpu.VMEM((N_HEADS, TB, TB), F32),
                        pltpu.VMEM((N_HEADS, TB, TB), BF16),
                        pltpu.VMEM((N_HEADS, 1, TB), F32),
                        pltpu.VMEM((N_HEADS, HEAD_DIM, TB), F32)],
        compiler_params=pltpu.CompilerParams(dimension_semantics=("arbitrary",),
                                             vmem_limit_bytes=VMEM_LIMIT),
        name="stick_breaking",
    )(qdt, proj3, vdt, jnp.asarray(u, BF16))


def _outproj_kernel(x_ref, oa, ob, oc, os_, ow, od,
                    ga, gb, gl, gc, gd, expand_ref, w_ref, g_ref, out_ref):
    f = lambda r: r[...].astype(F32)
    gates = _dot(gl[...], expand_ref[...])
    o_c = (gates[:, 0:BLK] * f(oc) + gates[:, BLK:2 * BLK] * f(os_)
           + gates[:, 2 * BLK:3 * BLK] * f(ow))
    mixed = [f(oa) * f(ga), f(ob) * f(gb), o_c * f(gc), f(od) * f(gd)]
    y = _dot(mixed[0].astype(BF16), w_ref[0:BLK, :])
    for g in range(1, 4):
        y = y + _dot(mixed[g].astype(BF16), w_ref[g * BLK:(g + 1) * BLK, :])
    ms = jnp.mean(y * y, axis=-1, keepdims=True)
    out_ref[...] = x_ref[...] + (y * lax.rsqrt(ms + EPS)) * g_ref[...]


def _out_projection(x2, mixer_outs, proj, w_out, g_post):
    rows = x2.shape[0]
    tm = 2 * TILE
    blk = lambda c: pl.BlockSpec((tm, BLK), lambda i: (i, c))
    expand = np.zeros((LANES, N_BRANCH * BLK), np.float32)
    for h in range(N_HEADS):
        for r in range(N_BRANCH):
            expand[N_BRANCH * h + r, r * BLK + h * HEAD_DIM:r * BLK + (h + 1) * HEAD_DIM] = 1.0
    gate_specs = [blk(PB_AG), blk(PB_BG),
                  pl.BlockSpec((tm, LANES), lambda i: (i, 2 * PB_C1 + 1)), blk(PB_CG), blk(PB_DG)]
    return pl.pallas_call(
        _outproj_kernel,
        grid=(rows // tm,),
        in_specs=[pl.BlockSpec((tm, D_MODEL), lambda i: (i, 0))]
                 + [blk(0)] * len(mixer_outs) + gate_specs
                 + [pl.BlockSpec(expand.shape, lambda i: (0, 0)),
                    pl.BlockSpec((D_MODEL, D_MODEL), lambda i: (0, 0)),
                    pl.BlockSpec((1, D_MODEL), lambda i: (0, 0))],
        out_specs=pl.BlockSpec((tm, D_MODEL), lambda i: (i, 0)),
        out_shape=jax.ShapeDtypeStruct((rows, D_MODEL), F32),
        compiler_params=pltpu.CompilerParams(dimension_semantics=("arbitrary",),
                                             vmem_limit_bytes=VMEM_LIMIT),
        name="out_projection",
    )(x2, *mixer_outs, *([proj] * len(gate_specs)), jnp.asarray(expand, BF16),
      w_out.astype(BF16), g_post.reshape(1, D_MODEL))


def _layer(x2, tabs, B, S, layer, w_rows, w_t, w_out, g_pre, g_post, mla_g_q, mla_g_kv,
           mla_w_uq, mla_w_ukv, nsa_pos_k, nsa_pos_v, nsa_k_w1, nsa_k_w2, nsa_v_w1, nsa_v_w2):
    rows = B * S
    aqkv, aqkv4, aqkv16, proj, vct, vdt, qct, qdt = _in_projection(
        x2, g_pre, w_rows, w_t, layer, tabs, B, S)
    proj3 = proj.reshape(B, S, N_PBLK * BLK)

    o_a = _dilated(aqkv, aqkv4, aqkv16, B, S)

    qnt, qrt, kn, vbt = _mla_up(proj, mla_g_q, mla_g_kv, mla_w_uq, mla_w_ukv, tabs, B, S)
    o_b = _flash_t(qnt, kn.reshape(B, S, BLK), 0, BLK, vbt, B=B, S=S,
                   aux=("per_head", qrt, proj3, 2 * PB_BKV + 1), name="mla_attention")

    kc2, vc2 = _compress(proj3, nsa_pos_k, nsa_pos_v, nsa_k_w1, nsa_k_w2, nsa_v_w1, nsa_v_w2, B, S)
    o_cmp, nsel = _cmp_topk(proj3, kc2, vc2, B, S)
    penalty = np.zeros((S, LANES), np.float32)
    penalty[np.arange(S), np.arange(S) // NSA_SEL_LEN] = MASK_PENALTY
    o_slc = _flash_t(qct, proj3, 2 * PB_C0 + 1, LANES, vct, B=B, S=S, vrow0=0,
                     aux=("shared", nsel, jnp.asarray(penalty, BF16), 0), name="nsa_selected")
    o_win = _flash_t(qct, proj3, 2 * PB_C1, LANES, vct, B=B, S=S, vrow0=HEAD_DIM,
                     window=NSA_WINDOW - 1, name="nsa_window")

    o_d = _stick_breaking(qdt, proj3, vdt, B, S)

    flat = lambda t: t.reshape(rows, BLK)
    outs = [flat(o) for o in (o_a, o_b, o_cmp, o_slc, o_win, o_d)]
    return _out_projection(x2, outs, proj, w_out, g_post)


def kernel(x, positions, w_in, w_out, g_pre, g_post, mla_g_q, mla_g_kv, mla_w_uq, mla_w_ukv,
           nsa_pos_k, nsa_pos_v, nsa_k_w1, nsa_k_w2, nsa_v_w1, nsa_v_w2):
    B, S, D = x.shape
    assert D == D_MODEL and S % 1024 == 0
    x2 = x.reshape(B * S, D)
    tabs = _rope_tables(positions)
    w_rows, w_t = _pack_w_in(w_in)
    for l in range(w_in.shape[0]):
        x2 = _layer(x2, tabs, B, S, l, w_rows, w_t, w_out[l], g_pre[l], g_post[l],
                    mla_g_q[l], mla_g_kv[l], mla_w_uq[l], mla_w_ukv[l],
                    nsa_pos_k[l], nsa_pos_v[l], nsa_k_w1[l], nsa_k_w2[l],
                    nsa_v_w1[l], nsa_v_w2[l])
    return x2.reshape(B, S, D)
```

```python
import functools
import math

import numpy as np
import jax
import jax.numpy as jnp
from jax import lax
from jax.experimental import pallas as pl
from jax.experimental.pallas import tpu as pltpu

D_MODEL = 1024
HEAD_DIM = 64
N_HEADS = 4
ROPE_THETA = 10000.0
EPS = 1e-6
NEG = -1e30
LOG2E = math.log2(math.e)
MASK_PENALTY = -(2.0 ** 100)
BIG = 1e9

MLA_Q_RANK = 256
MLA_KV_RANK = 128
MLA_NOPE = 64
MLA_ROPE = 32
MLA_V = 64

NSA_CMP_LEN = 32
NSA_CMP_STRIDE = 16
NSA_CMP_HID = 256
NSA_SEL_LEN = 64
NSA_N_SEL = 16
NSA_WINDOW = 512

LANES = 128
BLK = 256
TILE = 512
OUT_TILE = 1024
MLA_UP_TILE = 2048
ROPE_TILE = 2048
CMP_TILE = 1024
PACK_COLS = 256
CM_TILE = 256
DILATIONS = (1, 4, 16)
DIL_BLOCK = 128
LSE_PARTS = 2
TB = 256
ONES_ROWS = 16
VMEM_LIMIT = 48 * 1024 * 1024

F32 = jnp.float32
BF16 = jnp.bfloat16

(PB_AG, PB_BCQ, PB_BKV, PB_BG, PB_CQ, PB_C0, PB_C1, PB_CG, PB_DQ, PB_DK, PB_DG) = range(11)
N_PBLK = 11
N_BRANCH = 3
ROPE_NONE, ROPE_64, ROPE_32, ROPE_64_LO, ACT_SILU, ACT_SIGMOID = range(6)


def _dot_nt(a, b):
    return lax.dot_general(a, b, (((1,), (1,)), ((), ())), preferred_element_type=F32)


def _dot(a, b):
    return jnp.dot(a, b, preferred_element_type=F32)


def _rope_half(y, cos, sin_signed, group):
    half = group // 2
    lane = lax.broadcasted_iota(jnp.int32, y.shape, 1)
    first = (lane & (group - 1)) < half
    partner = jnp.where(first, pltpu.roll(y, LANES - half, 1), pltpu.roll(y, half, 1))
    return y * cos + partner * sin_signed


def _rope_table_kernel(pos_ref, inv_ref, cos_ref, sin_ref):
    ang = inv_ref[...] * pos_ref[...].astype(F32)
    cos_ref[...] = jnp.cos(ang)
    sin_ref[...] = jnp.sin(ang)


def _rope_tables(positions):
    rows = positions.size
    tm = ROPE_TILE
    inv =np.concatenate([ROPE_THETA ** (-np.arange(n, dtype=np.float64) / n)
                          for n in (HEAD_DIM // 2, MLA_ROPE // 2)]).astype(np.float32)[:, None]
    out = jax.ShapeDtypeStruct((inv.shape[0], rows), F32)
    tab_spec = pl.BlockSpec((inv.shape[0], tm), lambda i: (0, i))
    return pl.pallas_call(
        _rope_table_kernel,
        grid=(rows // tm,),
        in_specs=[pl.BlockSpec((1, tm), lambda i: (0, i)),
                  pl.BlockSpec(inv.shape, lambda i: (0, 0))],
        out_specs=[tab_spec] * 2,
        out_shape=[out] * 2,
        name="rope_tables",
    )(positions.reshape(1, rows), jnp.asarray(inv))


def _token_major_rope(cos_ref, sin_ref, group):
    n = group // 2
    lo = 0 if group == HEAD_DIM else HEAD_DIM // 2
    c = cos_ref[lo:lo + n, :]
    s = sin_ref[lo:lo + n, :]
    reps = LANES // n
    cos = jnp.concatenate([c] * reps, axis=0).T
    sin = jnp.concatenate([-s if k % 2 == 0 else s for k in range(reps)], axis=0).T
    return cos, sin


def _inproj_kernel(x_ref, g_ref, w_ref, wt_ref, perm_ref, cos_ref, sin_ref,
                   aqkv_ref, aqkv4_ref, aqkv16_ref, proj_ref, vct_ref, vdt_ref, qct_ref, qdt_ref,
                   *, plan_a, plan_p):
    x = x_ref[...]
    tm = x.shape[0]
    ms = jnp.mean(x * x, axis=-1, keepdims=True)
    h = ((x * lax.rsqrt(ms + EPS)) * g_ref[...]).astype(BF16)
    lane = lax.broadcasted_iota(jnp.int32, (tm, LANES), 1)
    c64, s64 = _token_major_rope(cos_ref, sin_ref, HEAD_DIM)
    c32, s32 = _token_major_rope(cos_ref, sin_ref, MLA_ROPE)

    def epilogue(y, ops, scale):
        halves = []
        for hf, op in enumerate(ops):
            yh = y[:, hf * LANES:(hf + 1) * LANES]
            if op == ROPE_64:
                yh = _rope_half(yh, c64, s64, 64)
            elif op == ROPE_32:
                yh = _rope_half(yh, c32, s32, 32)
            elif op == ROPE_64_LO:
                yh = jnp.where(lane < 64, _rope_half(yh, c64, s64, 64), yh)
            elif op == ACT_SILU:
                yh = yh * (1.0 / (1.0 + jnp.exp(-yh)))
            elif op == ACT_SIGMOID:
                yh = 1.0 / (1.0 + jnp.exp(-yh))
            if scale != 1.0:
                yh = yh * scale
            halves.append(yh)
        return jnp.concatenate(halves, axis=1)

    feature_major = {PB_CQ: qct_ref, PB_DQ: qdt_ref}
    col = 0
    for out_ref, plan in ((aqkv_ref, plan_a), (proj_ref, plan_p)):
        for b, (ops, scale) in enumerate(plan):
            y = epilogue(_dot(h, w_ref[:, col:col + BLK]), ops, scale)
            out_ref[:, b * BLK:(b + 1) * BLK] = y.astype(BF16)
            if out_ref is proj_ref and b in feature_major:
                y_t = y.T
                for blk in range(tm // TB):
                    feature_major[b][blk] = y_t[:, blk * TB:(blk + 1) * TB].astype(BF16)
            col += BLK

    for t in range(tm // CM_TILE):
        tile = slice(CM_TILE * t, CM_TILE * (t + 1))
        a_rows = aqkv_ref[tile, :]
        aqkv4_ref[tile, :] = _dot(perm_ref[0], a_rows).astype(BF16)
        aqkv16_ref[tile, :] = _dot(perm_ref[1], a_rows).astype(BF16)

    vt = _dot_nt(wt_ref[...], h)
    for blk in range(tm // TB):
        cols = slice(blk * TB, (blk + 1) * TB)
        vct_ref[blk] = vt[0:LANES, cols].astype(BF16)
        vdt_ref[blk] = vt[LANES:LANES + BLK, cols].astype(BF16)


_PLAN_A = (((ROPE_64, ROPE_64), LOG2E * HEAD_DIM ** -0.5),
           ((ROPE_64, ROPE_64), 1.0),
           ((ROPE_NONE, ROPE_NONE), 1.0))
_PLAN_P = (
    ((ACT_SILU, ACT_SILU), 1.0),
    ((ROPE_NONE, ROPE_NONE), 1.0),
    ((ROPE_NONE, ROPE_32), 1.0),
    ((ACT_SILU, ACT_SILU), 1.0),
    ((ROPE_64, ROPE_64), LOG2E * HEAD_DIM ** -0.5),
    ((ROPE_64_LO, ROPE_64), 1.0),
    ((ROPE_64, ACT_SIGMOID), 1.0),
    ((ACT_SILU, ACT_SILU), 1.0),
    ((ROPE_NONE, ROPE_NONE), LOG2E * HEAD_DIM ** -0.5),
    ((ROPE_NONE, ROPE_NONE), 1.0),
    ((ACT_SILU, ACT_SILU), 1.0),
)
assert len(_PLAN_P) == N_PBLK


_W_IN_SEGMENTS = (("aq", 256), ("ak", 256), ("av", 256), ("ag", 256),
                  ("bcq", 256), ("bckv", 128), ("bkr", 32), ("bg", 256),
                  ("cq", 256), ("ckc", 64), ("cvc", 64), ("cks", 64), ("cvs", 64),
                  ("ckw", 64), ("cvw", 64), ("cgl", 12), ("cg", 256),
                  ("dq", 256), ("dk", 256), ("dv", 256), ("dg", 256))
W_ROWS_COLS = (3 + N_PBLK) * BLK
W_T_ROWS = LANES + BLK


def _pack_w_kernel(wt_ref, rows_ref, t_ref):
    wt = wt_ref[...].astype(F32)
    o = 0
    seg = {}
    for name, width in _W_IN_SEGMENTS:
        seg[name] = wt[o:o + width, :]
        o += width
    gl_pad = jnp.zeros((LANES - N_HEADS * N_BRANCH, wt.shape[1]), wt.dtype)
    rows = [seg["aq"], seg["ak"], seg["av"],
            seg["ag"], seg["bcq"],
            seg["bckv"], seg["bkr"], seg["bkr"], seg["bkr"], seg["bkr"],
            seg["bg"], seg["cq"],
            seg["ckc"], seg["cvc"], seg["cks"], seg["cks"],
            seg["ckw"], seg["ckw"], seg["cgl"], gl_pad,
            seg["cg"], seg["dq"], seg["dk"], seg["dg"]]
    rows_ref[...] = jnp.concatenate(rows, axis=0).T.astype(BF16)
    t_ref[...] = jnp.concatenate([seg["cvs"], seg["cvw"], seg["dv"]], axis=0).astype(BF16)


def _pack_w_in(w_in):
    depth, d_model, d_in = w_in.shape
    assert d_model == D_MODEL and d_in == sum(width for _, width in _W_IN_SEGMENTS)
    tc = PACK_COLS
    return pl.pallas_call(
        _pack_w_kernel,
        grid=(depth, D_MODEL // tc),
        in_specs=[pl.BlockSpec((None, d_in, tc), lambda l, i: (l, 0, i))],
        out_specs=[pl.BlockSpec((None, tc, W_ROWS_COLS), lambda l, i: (l, i, 0)),
                   pl.BlockSpec((None, W_T_ROWS, tc), lambda l, i: (l, 0, i))],
        out_shape=[jax.ShapeDtypeStruct((depth, D_MODEL, W_ROWS_COLS), BF16),
                   jax.ShapeDtypeStruct((depth, W_T_ROWS, D_MODEL), BF16)],
        compiler_params=pltpu.CompilerParams(dimension_semantics=("arbitrary", "arbitrary"),
                                             vmem_limit_bytes=VMEM_LIMIT),
        name="pack_w_in",
    )(jnp.transpose(w_in, (0, 2, 1)).astype(BF16))


def _class_major_perm(dil, tile):
    n = tile // dil
    p = np.zeros((tile, tile), np.float32)
    pos = np.arange(tile)
    p[(pos % dil) * n + pos // dil, pos] = 1.0
    return p


def _in_projection(x2, g_pre, w_rows, w_t, layer, tabs, B, S):
    rows = x2.shape[0]
    tm = TILE
    per_b = S // tm
    nkb = tm // TB
    perm = jnp.asarray(np.stack([_class_major_perm(d, CM_TILE) for d in DILATIONS[1:]]), BF16)
    tab_spec = pl.BlockSpec((tabs[0].shape[0], tm), lambda i: (0, i))
    vt_spec = lambda r: pl.BlockSpec((None, nkb, r, TB), lambda i: (i // per_b, i % per_b, 0, 0))
    a_spec = pl.BlockSpec((tm, 3 * BLK), lambda i: (i, 0))
    a_shape = jax.ShapeDtypeStruct((rows, 3 * BLK), BF16)
    return pl.pallas_call(
        functools.partial(_inproj_kernel, plan_a=_PLAN_A, plan_p=_PLAN_P),
        grid=(rows // tm,),
        in_specs=[pl.BlockSpec((tm, D_MODEL), lambda i: (i, 0)),
                  pl.BlockSpec((1, D_MODEL), lambda i: (0, 0)),
                  pl.BlockSpec((None,) + w_rows.shape[1:], lambda i: (layer, 0, 0)),
                  pl.BlockSpec((None,) + w_t.shape[1:], lambda i: (layer, 0, 0)),
                  pl.BlockSpec(perm.shape, lambda i: (0, 0, 0))] + [tab_spec] * 2,
        out_specs=[a_spec, a_spec, a_spec,
                   pl.BlockSpec((tm, N_PBLK * BLK), lambda i: (i, 0)),
                   vt_spec(LANES), vt_spec(BLK), vt_spec(BLK), vt_spec(BLK)],
        out_shape=[a_shape, a_shape, a_shape,
                   jax.ShapeDtypeStruct((rows, N_PBLK * BLK), BF16),
                   jax.ShapeDtypeStruct((B, S // TB, LANES, TB), BF16)]
                  + [jax.ShapeDtypeStruct((B, S // TB, BLK, TB), BF16)] * 3,
        compiler_params=pltpu.CompilerParams(dimension_semantics=("arbitrary",),
                                             vmem_limit_bytes=VMEM_LIMIT),
        name="in_projection",
    )(x2, g_pre.reshape(1, D_MODEL), w_rows, w_t, perm, *tabs)


def _head_masks(shape):
    lane = lax.broadcasted_iota(jnp.int32, shape, 1)
    return lane < HEAD_DIM


def _masked_heads(q_ref, lo_mask):
    out = []
    for h in range(N_HEADS):
        pair, e = divmod(h, 2)
        qp = q_ref[:, pair * LANES:(pair + 1) * LANES]
        keep = lo_mask if e == 0 else jnp.logical_not(lo_mask)
        out.append(jnp.where(keep, qp, jnp.zeros_like(qp)))
    return out


def _dilated_kernel(an_ref, a4_ref, a16_ref, unperm_ref, expand_ref, o_ref, res_ref, s_ref):
    T = DIL_BLOCK
    S = an_ref.shape[0]
    n_tiles = S // CM_TILE
    lo = _head_masks((T, LANES))
    lane = lax.broadcasted_iota(jnp.int32, (T, LANES), 1)
    dist2 = (lax.broadcasted_iota(jnp.int32, (T, 2 * T), 0)
             - lax.broadcasted_iota(jnp.int32, (T, 2 * T), 1))
    causal1 = (lax.broadcasted_iota(jnp.int32, (T, T), 1)
               <= lax.broadcasted_iota(jnp.int32, (T, T), 0))

    def scores(q, k):
        qm = _masked_heads(q, lo)
        return tuple(_dot_nt(qm[h], k[:, (h // 2) * LANES:(h // 2 + 1) * LANES])
                     for h in range(N_HEADS))

    def finish(n_keys, v, mask, stage_next):
        probs = []
        for h in range(N_HEADS):
            s = jnp.where(mask, s_ref[h, :, 0:n_keys], NEG)
            m = jnp.max(s, axis=-1, keepdims=True)
            p = jnp.exp2(s - m)
            probs.append((m, jnp.sum(p, axis=-1, keepdims=True), p.astype(BF16)))
        stage_next()
        outs = []
        top = jnp.zeros((T, LANES), F32)
        den = jnp.ones((T, LANES), F32)
        for h in range(N_HEADS):
            m, l, p = probs[h]
            outs.append(_dot(p, v[:, (h // 2) * LANES:(h // 2 + 1) * LANES]) * (1.0 / l))
            mine = (lane & (N_HEADS - 1)) == h
            top = jnp.where(mine, m, top)
            den = jnp.where(mine, l, den)
        lse = top + jnp.log2(den)
        hi = lse.astype(BF16).astype(F32)
        parts = jnp.where(lane < N_HEADS, hi,
                          jnp.where(lane < LSE_PARTS * N_HEADS, lse - hi, 0.0))
        return jnp.concatenate(
            [jnp.where(lo, outs[0], outs[1]), jnp.where(lo, outs[2], outs[3]), parts],
            axis=1).astype(BF16)

    def band_mask(delta):
        d = dist2 + delta
        return jnp.logical_and(d >= 0, d <= DIL_BLOCK)

    def run(n_units, n_keys, load_qk, load_v, store, mask_of):
        def stage(u):
            for h, s in enumerate(scores(*load_qk(u))):
                s_ref[h, :, 0:n_keys] = s

        def body(u, carry):
            nxt = jnp.minimum(u + 1, n_units - 1)
            store(u, finish(n_keys, load_v(u), mask_of(u), lambda: stage(nxt)))
            return carry

        stage(0)
        lax.fori_loop(0, n_units, body, 0)

    def rows(start, size):
        return pl.ds(pl.multiple_of(start, 16), size)

    def k0_1(u):
        return T * jnp.maximum(u - 1, 0)

    def store1(u, r):
        res_ref[0, rows(T * u, T), :] = r

    run(S // T, 2 * T,
        lambda u: (an_ref[rows(T * u, T), 0:BLK], an_ref[rows(k0_1(u), 2 * T), BLK:2 * BLK]),
        lambda u: an_ref[rows(k0_1(u), 2 * T), 2 * BLK:3 * BLK],
        store1,
        lambda u: band_mask(T * u - k0_1(u)))

    per4 = CM_TILE // 4
    nq4 = S // 4 // T

    def pieces4(u, blk):
        first_tile = blk * (T // per4)
        return [rows(CM_TILE * (first_tile + k) + per4 * (u // nq4), per4)
                for k in range(T // per4)]

    def kb0_4(u):
        return jnp.maximum(u % nq4 - 1, 0)

    def load4(u, blks, cols):
        return jnp.concatenate([a4_ref[p, cols] for blk in blks for p in pieces4(u, blk)], axis=0)

    def store4(u, r):
        for k, p in enumerate(pieces4(u, u % nq4)):
            res_ref[1, p, :] = r[per4 * k:per4 * (k + 1), :]

    run(4 * nq4, 2 * T,
        lambda u: (load4(u, [u % nq4], slice(0, BLK)),
                   load4(u, [kb0_4(u), kb0_4(u) + 1], slice(BLK, 2 * BLK))),
        lambda u: load4(u, [kb0_4(u), kb0_4(u) + 1], slice(2 * BLK, 3 * BLK)),
        store4,
        lambda u: band_mask(T * (u % nq4 - kb0_4(u))))

    per16 = CM_TILE // 16

    def load16(c, cols):
        return jnp.concatenate([a16_ref[rows(CM_TILE * t + per16 * c, per16), cols]
                                for t in range(n_tiles)], axis=0)

    def store16(c, r):
        for t in range(n_tiles):
            res_ref[2, rows(CM_TILE * t + per16 * c, per16), :] = r[per16 * t:per16 * (t + 1), :]

    run(16, T,
        lambda c: (load16(c, slice(0, BLK)), load16(c, slice(BLK, 2 * BLK))),
        lambda c: load16(c, slice(2 * BLK, 3 * BLK)),
        store16,
        lambda c: causal1)

    expand = expand_ref[...]
    for t in range(n_tiles):
        tile = slice(CM_TILE * t, CM_TILE * (t + 1))
        nat = [res_ref[0, tile, :].astype(F32),
               _dot(unperm_ref[0], res_ref[1, tile, :]),
               _dot(unperm_ref[1], res_ref[2, tile, :])]
        lse = [_dot(r[:, BLK:].astype(BF16), expand) for r in nat]
        top = jnp.maximum(lse[0], jnp.maximum(lse[1], lse[2]))
        w = [jnp.exp2(l - top) for l in lse]
        mixed = (w[0] * nat[0][:, :BLK] + w[1] * nat[1][:, :BLK] + w[2] * nat[2][:, :BLK])
        o_ref[tile, :] = (mixed / (w[0] + w[1] + w[2])).astype(BF16)


def _dilated(aqkv, aqkv4, aqkv16, B, S):
    assert S == 16 * DIL_BLOCK and S % CM_TILE == 0 and DIL_BLOCK % (CM_TILE // 4) == 0
    r3 = lambda t: t.reshape(B, S, 3 * BLK)
    unperm = jnp.asarray(np.stack([_class_major_perm(d, CM_TILE).T for d in DILATIONS[1:]]), BF16)
    expand = np.zeros((LANES, BLK), np.float32)
    for h in range(N_HEADS):
        for part in range(LSE_PARTS):
            expand[N_HEADS * part + h, h * HEAD_DIM:(h + 1) * HEAD_DIM] = 1.0
    a_spec = pl.BlockSpec((None, S, 3 * BLK), lambda b: (b, 0, 0))
    return pl.pallas_call(
        _dilated_kernel,
        grid=(B,),
        in_specs=[a_spec, a_spec, a_spec,
                  pl.BlockSpec(unperm.shape, lambda b: (0, 0, 0)),
                  pl.BlockSpec(expand.shape, lambda b: (0, 0))],
        out_specs=pl.BlockSpec((None, S, BLK), lambda b: (b, 0, 0)),
        out_shape=jax.ShapeDtypeStruct((B, S, BLK), BF16),
        scratch_shapes=[pltpu.VMEM((3, S, BLK + LANES), BF16),
                        pltpu.VMEM((N_HEADS, DIL_BLOCK, 2 * DIL_BLOCK), F32)],
        compiler_params=pltpu.CompilerParams(dimension_semantics=("arbitrary",),
                                             vmem_limit_bytes=VMEM_LIMIT),
        name="dilated_mixture",
    )(r3(aqkv), r3(aqkv4), r3(aqkv16), unperm, jnp.asarray(expand, BF16))


def _masked_head_rows(qt_ref):
    out = []
    zeros = jnp.zeros((HEAD_DIM, qt_ref.shape[-1]), BF16)
    for h in range(N_HEADS):
        own = qt_ref[h * HEAD_DIM:(h + 1) * HEAD_DIM, :]
        out.append(jnp.concatenate([own, zeros] if h % 2 == 0 else [zeros, own], axis=0))
    return out


def _flash_t_kernel(*refs, window, kshared, vrow0, aux):
    it = iter(refs)
    q_all_ref = next(it)
    qx_all_ref = next(it) if aux else None
    k_ref = next(it)
    kx_ref = next(it) if aux else None
    vt_ref = next(it)
    o_ref = next(it)
    s_ref, m_ref, acc_ref = next(it), next(it), next(it)
    T = TB

    krow = lax.broadcasted_iota(jnp.int32, (T, T), 0)
    qcol = lax.broadcasted_iota(jnp.int32, (T, T), 1)
    ones = jnp.ones((ONES_ROWS, T), BF16)

    n_q = q_all_ref.shape[0]
    cross_stage = window is None
    if window is not None:
        o_full = (window - T + 1) // T
        o_none = (window + T - 1) // T + 1
        assert o_none - o_full == 2

    def query_weights(i):
        qcat = _masked_head_rows(q_all_ref.at[i])
        if aux == "per_head":
            group = LANES // N_HEADS
            qcat = [jnp.concatenate(
                [qcat[h]] + [qx_all_ref[i, g * group:(g + 1) * group, :] if g == h
                             else jnp.zeros((group, T), BF16) for g in range(N_HEADS)], axis=0)
                for h in range(N_HEADS)]
        elif aux == "shared":
            qx = qx_all_ref[i]
            qcat = [jnp.concatenate([qcat[h], qx], axis=0) for h in range(N_HEADS)]
        return qcat

    def first_full_block(i):
        return 0 if window is None else jnp.maximum(i - o_full, 0)

    def scores(j, qcat):
        rows = pl.ds(pl.multiple_of(j * T, T), T)
        out = []
        for h in range(N_HEADS):
            pair = h // 2
            cols = slice(0, LANES) if kshared else slice(pair * LANES, (pair + 1) * LANES)
            kb = k_ref[rows, cols]
            if aux:
                kb = jnp.concatenate([kb, kx_ref[rows, :]], axis=1)
            out.append(_dot(kb, qcat[h]))
        return out

    def stage(j, qcat):
        for h, s in enumerate(scores(j, qcat)):
            s_ref[h] = s

    def band_block(i):
        o = o_full + 1
        j = i - o
        limit = jnp.where(j >= 0, window - o * T, -2 * T)
        return jnp.maximum(j, 0), limit

    def stage_query_block(i):
        stage(first_full_block(i), query_weights(i))

    def query_block(i, carry):
        qcat = query_weights(i)

        def absorb(j, mask, staged_next=None, s_all=None):
            vt_all = vt_ref[j]
            probs = []
            for h in range(N_HEADS):
                s = s_ref[h] if s_all is None else s_all[h]
                if mask is not None:
                    s = jnp.where(mask, s, NEG)
                m = m_ref[h]
                m_new = jnp.maximum(m, jnp.max(s, axis=0, keepdims=True))
                alpha = jnp.exp2(m - m_new)
                p = jnp.exp2(s - m_new)
                probs.append((m_new, alpha, p.astype(BF16)))
            if staged_next is not None:
                staged_next()
            for h in range(N_HEADS):
                m_new, alpha, p = probs[h]
                r0 = vrow0 if kshared else h * HEAD_DIM
                lhs = jnp.concatenate([vt_all[r0:r0 + HEAD_DIM, :], ones], axis=0)
                acc_ref[h] = alpha * acc_ref[h] + _dot(lhs, p)
                m_ref[h] = m_new

        m_ref[...] = jnp.full(m_ref.shape, NEG, F32)
        acc_ref[...] = jnp.zeros(acc_ref.shape, F32)
        first = first_full_block(i)
        if window is not None:
            jc, limit = band_block(i)
            s_band = scores(jc, qcat)
            stage(first, qcat)
            absorb(jc, (qcol - krow) <= limit, s_all=s_band)

        def body(j, c):
            absorb(j, None, staged_next=lambda: stage(j + 1, qcat))
            return c

        lax.fori_loop(first, i, body, 0)
        if cross_stage:
            nxt = jnp.minimum(i + 1, n_q - 1)
            absorb(i, krow <= qcol, staged_next=lambda: stage_query_block(nxt))
        else:
            absorb(i, krow <= qcol)

        q_rows = pl.ds(pl.multiple_of(i * T, T), T)
        for pair in range(2):
            outs = []
            for e in range(2):
                acc = acc_ref[2 * pair + e]
                outs.append(acc[0:HEAD_DIM, :] * (1.0 / acc[HEAD_DIM:HEAD_DIM + 1, :]))
            o_ref[q_rows, pair * LANES:(pair + 1) * LANES] = (
                jnp.concatenate(outs, axis=0).T.astype(BF16))
        return carry

    if cross_stage:
        stage_query_block(0)
    lax.fori_loop(0, n_q, query_block, 0)


def _flash_t(qt, k, kcol, kw, vt, *, B, S, window=None, vrow0=0, aux=None, name):
    kshared = kw == LANES
    whole = lambda a: pl.BlockSpec((None,) + a.shape[1:], lambda b: (b,) + (0,) * (a.ndim - 1))
    in_specs = [whole(qt)]
    args = [qt]
    if aux is not None:
        kind, qx, kx, kxcol = aux
        in_specs.append(whole(qx))
        args.append(qx)
    in_specs.append(pl.BlockSpec((None, S, kw), lambda b: (b, 0, kcol)))
    args.append(k)
    if aux is not None:
        if kx.ndim == 3:
            in_specs.append(pl.BlockSpec((None, S, LANES), lambda b: (b, 0, kxcol)))
        else:
            in_specs.append(pl.BlockSpec((S, LANES), lambda b: (0, kxcol)))
        args.append(kx)
    in_specs.append(whole(vt))
    args.append(vt)
    return pl.pallas_call(
        functools.partial(_flash_t_kernel, window=window, kshared=kshared, vrow0=vrow0,
                          aux=None if aux is None else aux[0]),
        grid=(B,),
        in_specs=in_specs,
        out_specs=pl.BlockSpec((None, S, BLK), lambda b: (b, 0, 0)),
        out_shape=jax.ShapeDtypeStruct((B, S, BLK), BF16),
        scratch_shapes=[pltpu.VMEM((N_HEADS, TB, TB), F32),
                        pltpu.VMEM((N_HEADS, 1, TB), F32),
                        pltpu.VMEM((N_HEADS, HEAD_DIM + ONES_ROWS, TB), F32)],
        compiler_params=pltpu.CompilerParams(dimension_semantics=("arbitrary",),
                                             vmem_limit_bytes=VMEM_LIMIT),
        name=name,
    )(*args)


def _mla_up_kernel(cq_ref, ckv_ref, gq_ref, gkv_ref, wq_ref, wk_ref, wvt_ref, cos_ref, sin_ref,
                   qn_ref, qr_ref, kn_ref, vt_ref):
    def norm(t, g):
        t = t.astype(F32)
        ms = jnp.mean(t * t, axis=-1, keepdims=True)
        return ((t * lax.rsqrt(ms + EPS)) * g).astype(BF16)

    scale = LOG2E * (MLA_NOPE + MLA_ROPE) ** -0.5
    q_t = _dot_nt(wq_ref[...], norm(cq_ref[...], gq_ref[...])) * scale
    qn_t = q_t[:BLK, :]
    half = MLA_ROPE // 2
    cos = cos_ref[HEAD_DIM // 2:HEAD_DIM // 2 + half, :]
    sin = sin_ref[HEAD_DIM // 2:HEAD_DIM // 2 + half, :]
    rot = []
    for h in range(N_HEADS):
        x1 = q_t[BLK + h * MLA_ROPE:BLK + h * MLA_ROPE + half, :]
        x2 = q_t[BLK + h * MLA_ROPE + half:BLK + (h + 1) * MLA_ROPE, :]
        rot += [x1 * cos - x2 * sin, x1 * sin + x2 * cos]
    qr_t = jnp.concatenate(rot, axis=0)
    ckv = norm(ckv_ref[...], gkv_ref[...])
    kn_ref[...] = _dot(ckv, wk_ref[...]).astype(BF16)
    vt = _dot_nt(wvt_ref[...], ckv)
    for blk in range(vt.shape[1] // TB):
        cols = slice(blk * TB, (blk + 1) * TB)
        qn_ref[blk] = qn_t[:, cols].astype(BF16)
        qr_ref[blk] = qr_t[:, cols].astype(BF16)
        vt_ref[blk] = vt[:, cols].astype(BF16)


def _mla_up(proj, g_q, g_kv, w_uq, w_ukv, tabs, B, S):
    rows = proj.shape[0]
    tm = MLA_UP_TILE
    per_b = S // tm
    wq = w_uq.reshape(MLA_Q_RANK, N_HEADS, MLA_NOPE + MLA_ROPE)
    wq = jnp.concatenate([wq[:, :, :MLA_NOPE].reshape(MLA_Q_RANK, -1),
                          wq[:, :, MLA_NOPE:].reshape(MLA_Q_RANK, -1)], axis=1).T.astype(BF16)
    wkv = w_ukv.reshape(MLA_KV_RANK, N_HEADS, MLA_NOPE + MLA_V)
    wk = wkv[:, :, :MLA_NOPE].reshape(MLA_KV_RANK, -1).astype(BF16)
    wvt = wkv[:, :, MLA_NOPE:].reshape(MLA_KV_RANK, -1).T.astype(BF16)
    row_spec = lambda w, c: pl.BlockSpec((tm, w), lambda i: (i, c))
    full = lambda a: pl.BlockSpec(a.shape, lambda i: (0, 0))
    gq = g_q.reshape(1, -1)
    gkv = g_kv.reshape(1, -1)
    fm_shape = lambda r: jax.ShapeDtypeStruct((B, S // TB, r, TB), BF16)
    fm_spec = lambda r: pl.BlockSpec((None, tm // TB, r, TB),
                                     lambda i: (i // per_b, i % per_b, 0, 0))
    outs = [fm_shape(BLK), fm_shape(LANES), jax.ShapeDtypeStruct((rows, BLK), BF16), fm_shape(BLK)]
    return pl.pallas_call(
        _mla_up_kernel,
        grid=(rows // tm,),
        in_specs=[row_spec(BLK, PB_BCQ), row_spec(LANES, 2 * PB_BKV), full(gq), full(gkv),
                  full(wq), full(wk), full(wvt)]
                 + [pl.BlockSpec((tabs[0].shape[0], tm), lambda i: (0, i))] * 2,
        out_specs=[fm_spec(BLK), fm_spec(LANES), row_spec(BLK, 0), fm_spec(BLK)],
        out_shape=outs,
        compiler_params=pltpu.CompilerParams(dimension_semantics=("arbitrary",),
                                             vmem_limit_bytes=VMEM_LIMIT),
        name="mla_up",
    )(proj, proj, gq, gkv, wq, wk, wvt, *tabs)


def _compress_kernel(x_ref, perm_ref, pos_ref, w1_ref, w2_ref, kc_ref, vc_ref):
    n_tiles = x_ref.shape[0] // TILE
    per = TILE // NSA_CMP_STRIDE
    slabs = []
    for t in range(n_tiles):
        cm = _dot(perm_ref[...], x_ref[TILE * t:TILE * (t + 1), :])
        slabs.append(jnp.concatenate([cm[per * c:per * (c + 1), :]
                                      for c in range(NSA_CMP_STRIDE)], axis=1))
    x = jnp.concatenate(slabs, axis=0)
    half = x.shape[1]
    top = (x + pos_ref[0:1, :]).astype(BF16)
    bot = (x + pos_ref[1:2, :]).astype(BF16)
    y_top = _dot(top, w1_ref[0:half, :])
    y_bot = _dot(bot, w1_ref[half:2 * half, :])
    pre = y_top + pltpu.roll(y_bot, y_bot.shape[0] - 1, 0)
    hid = (pre * (1.0 / (1.0 + jnp.exp(-pre)))).astype(BF16)
    kc_ref[...] = _dot(hid[:, :NSA_CMP_HID], w2_ref[0]).astype(BF16)
    vc_ref[...] = _dot(hid[:, NSA_CMP_HID:], w2_ref[1]).astype(BF16)


def _compress(proj3, pos_k, pos_v, kw1, kw2, vw1, vw2, B, S):
    assert S % TILE == 0 and NSA_CMP_LEN == 2 * NSA_CMP_STRIDE
    n16 = S // NSA_CMP_STRIDE
    perm = jnp.asarray(_class_major_perm(NSA_CMP_STRIDE, TILE), BF16)
    pos = jnp.concatenate([pos_k, pos_v], axis=1).reshape(2, NSA_CMP_STRIDE * LANES)
    w1 = jnp.zeros((NSA_CMP_LEN, LANES, 2 * NSA_CMP_HID), F32)
    w1 = w1.at[:, :HEAD_DIM, :NSA_CMP_HID].set(kw1.reshape(NSA_CMP_LEN, HEAD_DIM, NSA_CMP_HID))
    w1 = w1.at[:, HEAD_DIM:, NSA_CMP_HID:].set(vw1.reshape(NSA_CMP_LEN, HEAD_DIM, NSA_CMP_HID))
    w1 = w1.reshape(NSA_CMP_LEN * LANES, 2 * NSA_CMP_HID).astype(BF16)
    w2 = jnp.stack([jnp.tile(kw2, (1, 2)), jnp.tile(vw2, (1, 2))]).astype(BF16)
    out = jax.ShapeDtypeStruct((B, n16, LANES), BF16)
    return pl.pallas_call(
        _compress_kernel,
        grid=(B,),
        in_specs=[pl.BlockSpec((None, S, LANES), lambda b: (b, 0, 2 * PB_C0)),
                  pl.BlockSpec(perm.shape, lambda b: (0, 0)),
                  pl.BlockSpec(pos.shape, lambda b: (0, 0)),
                  pl.BlockSpec(w1.shape, lambda b: (0, 0)),
                  pl.BlockSpec(w2.shape, lambda b: (0, 0, 0))],
        out_specs=[pl.BlockSpec((None, n16, LANES), lambda b: (b, 0, 0))] * 2,
        out_shape=[out, out],
        compiler_params=pltpu.CompilerParams(dimension_semantics=("arbitrary",),
                                             vmem_limit_bytes=VMEM_LIMIT),
        name="nsa_compress",
    )(proj3, perm, pos, w1, w2)


def _cmp_topk_kernel(q_ref, kc_ref, vc_ref, ovt_ref, o_ref, nsel_ref, *, T, n_cmp, n_slc):
    i = pl.program_id(1)
    t0 = i * T
    lane = lax.broadcasted_iota(jnp.int32, (T, LANES), 1)
    tok = t0 + lax.broadcasted_iota(jnp.int32, (T, LANES), 0)
    ok = jnp.logical_and(lane * NSA_CMP_STRIDE + (NSA_CMP_LEN - 1) <= tok, lane < n_cmp)
    lo = lane < HEAD_DIM
    qm = _masked_heads(q_ref, lo)
    kc = kc_ref[...]
    vc = vc_ref[...]
    psum = jnp.zeros((T, LANES), F32)
    outs = []
    for h in range(N_HEADS):
        s = jnp.where(ok, _dot_nt(qm[h], kc), NEG)
        mx = jnp.max(s, axis=-1, keepdims=True)
        e = jnp.where(ok, jnp.exp2(s - mx), 0.0)
        den = jnp.sum(e, axis=-1, keepdims=True)
        p = e * (1.0 / jnp.maximum(den, 1e-30))
        psum = psum + p
        outs.append(_dot(p.astype(BF16), vc))
    for pair in range(2):
        o_ref[:, pair * LANES:(pair + 1) * LANES] = jnp.where(
            lo, outs[2 * pair], outs[2 * pair + 1]).astype(BF16)

    p_hi = psum.astype(BF16)
    p_lo = (psum - p_hi.astype(F32)).astype(BF16)
    ovt = ovt_ref[...]
    imp = _dot_nt(ovt, p_hi) + _dot_nt(ovt, p_lo)
    jj = lax.broadcasted_iota(jnp.int32, (n_slc, T), 0)
    bt = (t0 + lax.broadcasted_iota(jnp.int32, (n_slc, T), 1)) >> 6
    forced = jnp.logical_or(jj == 0, jnp.logical_or(jj == bt, jj == bt - 1))
    val = jnp.where(forced, BIG, jnp.where(jj > bt, -BIG, imp))
    sel_rows = []
    for j in range(n_slc):
        vj = val[j:j + 1, :]
        beats = jnp.logical_or(val > vj, jnp.logical_and(val == vj, jj < j))
        rank = jnp.sum(beats.astype(F32), axis=0, keepdims=True)
        sel_rows.append((rank >= NSA_N_SEL).astype(F32))
    sel_rows.append(jnp.zeros((LANES - n_slc, T), F32))
    nsel = jnp.concatenate(sel_rows, axis=0).astype(BF16)
    for blk in range(T // TB):
        nsel_ref[blk] = nsel[:, blk * TB:(blk + 1) * TB]


def _cmp_topk(proj3, kc2, vc2, B, S):
    T = CMP_TILE
    n_cmp = (S - NSA_CMP_LEN) // NSA_CMP_STRIDE + 1
    n_slc = S // NSA_SEL_LEN
    assert NSA_SEL_LEN == 64 and n_slc <= LANES and n_cmp <= LANES
    ci = np.arange(LANES)[None, :] * NSA_CMP_STRIDE
    sj = np.arange(n_slc)[:, None] * NSA_SEL_LEN
    ovt = ((ci < sj + NSA_SEL_LEN) & (ci + NSA_CMP_LEN > sj) & (np.arange(LANES)[None, :] < n_cmp))
    ovt = jnp.asarray(ovt.astype(np.float32), BF16)
    return pl.pallas_call(
        functools.partial(_cmp_topk_kernel, T=T, n_cmp=n_cmp, n_slc=n_slc),
        grid=(B, S // T),
        in_specs=[pl.BlockSpec((None, T, BLK), lambda b, i: (b, i, PB_CQ)),
                  pl.BlockSpec((None, LANES, LANES), lambda b, i: (b, 0, 0)),
                  pl.BlockSpec((None, LANES, LANES), lambda b, i: (b, 0, 0)),
                  pl.BlockSpec(ovt.shape, lambda b, i: (0, 0))],
        out_specs=[pl.BlockSpec((None, T, BLK), lambda b, i: (b, i, 0)),
                   pl.BlockSpec((None, T // TB, LANES, TB), lambda b, i: (b, i, 0, 0))],
        out_shape=[jax.ShapeDtypeStruct((B, S, BLK), BF16),
                   jax.ShapeDtypeStruct((B, S // TB, LANES, TB), BF16)],
        compiler_params=pltpu.CompilerParams(dimension_semantics=("arbitrary", "arbitrary"),
                                             vmem_limit_bytes=VMEM_LIMIT),
        name="nsa_cmp_topk",
    )(proj3, kc2, vc2, ovt)


def _stick_kernel(q_all_ref, k_ref, vt_ref, u_ref, o_ref, za_ref, zb_ref, w_ref, carry_ref,
                  acc_ref):
    T = TB
    krow = lax.broadcasted_iota(jnp.int32, (T, T), 0)
    qcol = lax.broadcasted_iota(jnp.int32, (T, T), 1)
    strict = krow < qcol
    u = u_ref[...]
    z_refs = (za_ref, zb_ref)

    def query_block(i, carry):
        qm = _masked_head_rows(q_all_ref.at[i])

        def stage(j, slot):
            rows = pl.ds(pl.multiple_of(j * T, T), T)
            for h in range(N_HEADS):
                z_refs[slot][h] = _dot(k_ref[rows, (h // 2) * LANES:(h // 2 + 1) * LANES],
                                       qm[h])

        def apply_weights(j):
            vt_all = vt_ref[j]
            for h in range(N_HEADS):
                acc_ref[h] = acc_ref[h] + _dot(vt_all[h * HEAD_DIM:(h + 1) * HEAD_DIM, :], w_ref[h])

        def absorb(j, slot, diag, staged_next, applied_prev):
            z_ref = z_refs[slot]
            softplus = []
            for h in range(N_HEADS):
                z = z_ref[h]
                sp = jnp.maximum(z, 0.0) + jnp.log2(1.0 + jnp.exp2(-jnp.abs(z)))
                if diag:
                    sp = jnp.where(strict, sp, 0.0)
                softplus.append(sp.astype(BF16))
            if staged_next is not None:
                stage(staged_next, 1 - slot)
            if applied_prev is not None:
                apply_weights(applied_prev)
            sums = [_dot(u, sp) for sp in softplus]
            for h in range(N_HEADS):
                a = jnp.exp2(z_ref[h] - (sums[h] + carry_ref[h]))
                if diag:
                    a = jnp.where(strict, a, 0.0)
                w_ref[h] = a.astype(BF16)
                carry_ref[h] = carry_ref[h] + sums[h][0:1, :]

        carry_ref[...] = jnp.zeros(carry_ref.shape, F32)
        acc_ref[...] = jnp.zeros(acc_ref.shape, F32)
        stage(i, 0)
        absorb(i, 0, True, jnp.maximum(i - 1, 0), None)

        def body(t, c):
            j = i - 1 - 2 * t
            absorb(j, 1, False, j - 1, j + 1)
            absorb(j - 1, 0, False, jnp.maximum(j - 2, 0), j)
            return c

        lax.fori_loop(0, i // 2, body, 0)

        @pl.when(i % 2 == 1)
        def _():
            absorb(0, 1, False, None, 1)

        apply_weights(0)
        q_rows = pl.ds(pl.multiple_of(i * T, T), T)
        for pair in range(2):
            o_t = jnp.concatenate([acc_ref[2 * pair], acc_ref[2 * pair + 1]], axis=0)
            o_ref[q_rows, pair * LANES:(pair + 1) * LANES] = o_t.T.astype(BF16)
        return carry

    lax.fori_loop(0, q_all_ref.shape[0], query_block, 0)


def _stick_breaking(qdt, proj3, vdt, B, S):
    u = np.triu(np.ones((TB, TB), np.float32))
    return pl.pallas_call(
        _stick_kernel,
        grid=(B,),
        in_specs=[pl.BlockSpec((None,) + qdt.shape[1:], lambda b: (b, 0, 0, 0)),
                  pl.BlockSpec((None, S, BLK), lambda b: (b, 0, PB_DK)),
                  pl.BlockSpec((None,) + vdt.shape[1:], lambda b: (b, 0, 0, 0)),
                  pl.BlockSpec(u.shape, lambda b: (0, 0))],
        out_specs=pl.BlockSpec((None, S, BLK), lambda b: (b, 0, 0)),
        out_shape=jax.ShapeDtypeStruct((B, S, BLK), BF16),
        scratch_shapes=[pltpu.VMEM((N_HEADS, TB, TB), F32),
                        pltpu.VMEM((N_HEADS, TB, TB), F32),
                        pltpu.VMEM((N_HEADS, TB, TB), BF16),
                        pltpu.VMEM((N_HEADS, 1, TB), F32),
                        pltpu.VMEM((N_HEADS, HEAD_DIM, TB), F32)],
        compiler_params=pltpu.CompilerParams(dimension_semantics=("arbitrary",),
                                             vmem_limit_bytes=VMEM_LIMIT),
        name="stick_breaking",
    )(qdt, proj3, vdt, jnp.asarray(u, BF16))


def _outproj_kernel(x_ref, oa, ob, oc, os_, ow, od,
                    ga, gb, gl, gc, gd, expand_ref, w_ref, g_ref, out_ref):
    f = lambda r: r[...].astype(F32)
    gates = _dot(gl[...], expand_ref[...])
    o_c = (gates[:, 0:BLK] * f(oc) + gates[:, BLK:2 * BLK] * f(os_)
           + gates[:, 2 * BLK:3 * BLK] * f(ow))
    mixed = [f(oa) * f(ga), f(ob) * f(gb), o_c * f(gc), f(od) * f(gd)]
    y = _dot(mixed[0].astype(BF16), w_ref[0:BLK, :])
    for g in range(1, 4):
        y = y + _dot(mixed[g].astype(BF16), w_ref[g * BLK:(g + 1) * BLK, :])
    ms = jnp.mean(y * y, axis=-1, keepdims=True)
    out_ref[...] = x_ref[...] + (y * lax.rsqrt(ms + EPS)) * g_ref[...]


def _out_projection(x2, mixer_outs, proj, w_out, g_post):
    rows = x2.shape[0]
    tm = OUT_TILE
    blk = lambda c: pl.BlockSpec((tm, BLK), lambda i: (i, c))
    expand = np.zeros((LANES, N_BRANCH * BLK), np.float32)
    for h in range(N_HEADS):
        for r in range(N_BRANCH):
            expand[N_BRANCH * h + r, r * BLK + h * HEAD_DIM:r * BLK + (h + 1) * HEAD_DIM] = 1.0
    gate_specs = [blk(PB_AG), blk(PB_BG),
                  pl.BlockSpec((tm, LANES), lambda i: (i, 2 * PB_C1 + 1)), blk(PB_CG), blk(PB_DG)]
    return pl.pallas_call(
        _outproj_kernel,
        grid=(rows // tm,),
        in_specs=[pl.BlockSpec((tm, D_MODEL), lambda i: (i, 0))]
                 + [blk(0)] * len(mixer_outs) + gate_specs
                 + [pl.BlockSpec(expand.shape, lambda i: (0, 0)),
                    pl.BlockSpec((D_MODEL, D_MODEL), lambda i: (0, 0)),
                    pl.BlockSpec((1, D_MODEL), lambda i: (0, 0))],
        out_specs=pl.BlockSpec((tm, D_MODEL), lambda i: (i, 0)),
        out_shape=jax.ShapeDtypeStruct((rows, D_MODEL), F32),
        compiler_params=pltpu.CompilerParams(dimension_semantics=("arbitrary",),
                                             vmem_limit_bytes=VMEM_LIMIT),
        name="out_projection",
    )(x2, *mixer_outs, *([proj] * len(gate_specs)), jnp.asarray(expand, BF16),
      w_out.astype(BF16), g_post.reshape(1, D_MODEL))


def _layer(x2, tabs, B, S, layer, w_rows, w_t, w_out, g_pre, g_post, mla_g_q, mla_g_kv,
           mla_w_uq, mla_w_ukv, nsa_pos_k, nsa_pos_v, nsa_k_w1, nsa_k_w2, nsa_v_w1, nsa_v_w2):
    rows = B * S
    aqkv, aqkv4, aqkv16, proj, vct, vdt, qct, qdt = _in_projection(
        x2, g_pre, w_rows, w_t, layer, tabs, B, S)
    proj3 = proj.reshape(B, S, N_PBLK * BLK)

    o_a = _dilated(aqkv, aqkv4, aqkv16, B, S)

    qnt, qrt, kn, vbt = _mla_up(proj, mla_g_q, mla_g_kv, mla_w_uq, mla_w_ukv, tabs, B, S)
    o_b = _flash_t(qnt, kn.reshape(B, S, BLK), 0, BLK, vbt, B=B, S=S,
                   aux=("per_head", qrt, proj3, 2 * PB_BKV + 1), name="mla_attention")

    kc2, vc2 = _compress(proj3, nsa_pos_k, nsa_pos_v, nsa_k_w1, nsa_k_w2, nsa_v_w1, nsa_v_w2, B, S)
    o_cmp, nsel = _cmp_topk(proj3, kc2, vc2, B, S)
    penalty = np.zeros((S, LANES), np.float32)
    penalty[np.arange(S), np.arange(S) // NSA_SEL_LEN] = MASK_PENALTY
    o_slc = _flash_t(qct, proj3, 2 * PB_C0 + 1, LANES, vct, B=B, S=S, vrow0=0,
                     aux=("shared", nsel, jnp.asarray(penalty, BF16), 0), name="nsa_selected")
    o_win = _flash_t(qct, proj3, 2 * PB_C1, LANES, vct, B=B, S=S, vrow0=HEAD_DIM,
                     window=NSA_WINDOW - 1, name="nsa_window")

    o_d = _stick_breaking(qdt, proj3, vdt, B, S)

    flat = lambda t: t.reshape(rows, BLK)
    outs = [flat(o) for o in (o_a, o_b, o_cmp, o_slc, o_win, o_d)]
    return _out_projection(x2, outs, proj, w_out, g_post)


def kernel(x, positions, w_in, w_out, g_pre, g_post, mla_g_q, mla_g_kv, mla_w_uq, mla_w_ukv,
           nsa_pos_k, nsa_pos_v, nsa_k_w1, nsa_k_w2, nsa_v_w1, nsa_v_w2):
    B, S, D = x.shape
    assert D == D_MODEL and S % 1024 == 0
    x2 = x.reshape(B * S, D)
    tabs = _rope_tables(positions)
    w_rows, w_t = _pack_w_in(w_in)
    for l in range(w_in.shape[0]):
        x2 = _layer(x2, tabs, B, S, l, w_rows, w_t, w_out[l], g_pre[l], g_post[l],
                    mla_g_q[l], mla_g_kv[l], mla_w_uq[l], mla_w_ukv[l],
                    nsa_pos_k[l], nsa_pos_v[l], nsa_k_w1[l], nsa_k_w2[l],
                    nsa_v_w1[l], nsa_v_w2[l])
    return x2.reshape(B, S, D)
```

```python
import functools
import math

import numpy as np
import jax
import jax.numpy as jnp
from jax import lax
from jax.experimental import pallas as pl
from jax.experimental.pallas import tpu as pltpu

D_MODEL = 1024
HEAD_DIM = 64
N_HEADS = 4
ROPE_THETA = 10000.0
EPS = 1e-6
NEG = -1e30
LOG2E = math.log2(math.e)
MASK_PENALTY = -(2.0 ** 100)
BIG = 1e9

MLA_Q_RANK = 256
MLA_KV_RANK = 128
MLA_NOPE = 64
MLA_ROPE = 32
MLA_V = 64

NSA_CMP_LEN = 32
NSA_CMP_STRIDE = 16
NSA_CMP_HID = 256
NSA_SEL_LEN = 64
NSA_N_SEL = 16
NSA_WINDOW = 512

LANES = 128
BLK = 256
IN_TILE = 1024
OUT_TILE = 1024
TILE = 512
MLA_UP_TILE = 2048
ROPE_TILE = 2048
CMP_TILE = 1024
PACK_COLS = 256
CM_TILE = 256
DILATIONS = (1, 4, 16)
DIL_BLOCK = 128
LSE_PARTS = 2
TB = 256
ONES_ROWS = 16
VMEM_LIMIT = 48 * 1024 * 1024

F32 = jnp.float32
BF16 = jnp.bfloat16

(PB_AG, PB_BCQ, PB_BKV, PB_BG, PB_CQ, PB_C0, PB_C1, PB_CG, PB_DQ, PB_DK, PB_DG) = range(11)
N_PBLK = 11
N_BRANCH = 3
ROPE_NONE, ROPE_64, ROPE_32, ROPE_64_LO, ACT_SILU, ACT_SIGMOID = range(6)


def _dot_nt(a, b):
    return lax.dot_general(a, b, (((1,), (1,)), ((), ())), preferred_element_type=F32)


def _dot(a, b):
    return jnp.dot(a, b, preferred_element_type=F32)


def _rope_half(y, cos, sin_signed, group):
    half = group // 2
    lane = lax.broadcasted_iota(jnp.int32, y.shape, 1)
    first = (lane & (group - 1)) < half
    partner = jnp.where(first, pltpu.roll(y, LANES - half, 1), pltpu.roll(y, half, 1))
    return y * cos + partner * sin_signed


def _rope_table_kernel(pos_ref, inv_ref, cos_ref, sin_ref):
    ang = inv_ref[...] * pos_ref[...].astype(F32)
    cos_ref[...] = jnp.cos(ang)
    sin_ref[...] = jnp.sin(ang)


def _rope_tables(positions):
    rows = positions.size
    tm = ROPE_TILE
    inv =np.concatenate([ROPE_THETA ** (-np.arange(n, dtype=np.float64) / n)
                          for n in (HEAD_DIM // 2, MLA_ROPE // 2)]).astype(np.float32)[:, None]
    out = jax.ShapeDtypeStruct((inv.shape[0], rows), F32)
    tab_spec = pl.BlockSpec((inv.shape[0], tm), lambda i: (0, i))
    return pl.pallas_call(
        _rope_table_kernel,
        grid=(rows // tm,),
        in_specs=[pl.BlockSpec((1, tm), lambda i: (0, i)),
                  pl.BlockSpec(inv.shape, lambda i: (0, 0))],
        out_specs=[tab_spec] * 2,
        out_shape=[out] * 2,
        name="rope_tables",
    )(positions.reshape(1, rows), jnp.asarray(inv))


def _token_major_rope(cos_ref, sin_ref, group):
    n = group // 2
    lo = 0 if group == HEAD_DIM else HEAD_DIM // 2
    c = cos_ref[lo:lo + n, :]
    s = sin_ref[lo:lo + n, :]
    reps = LANES // n
    cos = jnp.concatenate([c] * reps, axis=0).T
    sin = jnp.concatenate([-s if k % 2 == 0 else s for k in range(reps)], axis=0).T
    return cos, sin


def _inproj_kernel(x_ref, g_ref, w_ref, wt_ref, perm_ref, cos_ref, sin_ref,
                   aqkv_ref, aqkv4_ref, aqkv16_ref, proj_ref, vct_ref, vdt_ref, qct_ref, qdt_ref,
                   *, plan_a, plan_p):
    x = x_ref[...]
    tm = x.shape[0]
    ms = jnp.mean(x * x, axis=-1, keepdims=True)
    h = ((x * lax.rsqrt(ms + EPS)) * g_ref[...]).astype(BF16)
    lane = lax.broadcasted_iota(jnp.int32, (tm, LANES), 1)
    c64, s64 = _token_major_rope(cos_ref, sin_ref, HEAD_DIM)
    c32, s32 = _token_major_rope(cos_ref, sin_ref, MLA_ROPE)

    def epilogue(y, ops, scale):
        halves = []
        for hf, op in enumerate(ops):
            yh = y[:, hf * LANES:(hf + 1) * LANES]
            if op == ROPE_64:
                yh = _rope_half(yh, c64, s64, 64)
            elif op == ROPE_32:
                yh = _rope_half(yh, c32, s32, 32)
            elif op == ROPE_64_LO:
                yh = jnp.where(lane < 64, _rope_half(yh, c64, s64, 64), yh)
            elif op == ACT_SILU:
                yh = yh * (1.0 / (1.0 + jnp.exp(-yh)))
            elif op == ACT_SIGMOID:
                yh = 1.0 / (1.0 + jnp.exp(-yh))
            if scale != 1.0:
                yh = yh * scale
            halves.append(yh)
        return jnp.concatenate(halves, axis=1)

    feature_major = {PB_CQ: qct_ref, PB_DQ: qdt_ref}
    col = 0
    for out_ref, plan in ((aqkv_ref, plan_a), (proj_ref, plan_p)):
        for b, (ops, scale) in enumerate(plan):
            y = epilogue(_dot(h, w_ref[:, col:col + BLK]), ops, scale)
            out_ref[:, b * BLK:(b + 1) * BLK] = y.astype(BF16)
            if out_ref is proj_ref and b in feature_major:
                y_t = y.T
                for blk in range(tm // TB):
                    feature_major[b][blk] = y_t[:, blk * TB:(blk + 1) * TB].astype(BF16)
            col += BLK

    for t in range(tm // CM_TILE):
        tile = slice(CM_TILE * t, CM_TILE * (t + 1))
        a_rows = aqkv_ref[tile, :]
        aqkv4_ref[tile, :] = _dot(perm_ref[0], a_rows).astype(BF16)
        aqkv16_ref[tile, :] = _dot(perm_ref[1], a_rows).astype(BF16)

    vt = _dot_nt(wt_ref[...], h)
    for blk in range(tm // TB):
        cols = slice(blk * TB, (blk + 1) * TB)
        vct_ref[blk] = vt[0:LANES, cols].astype(BF16)
        vdt_ref[blk] = vt[LANES:LANES + BLK, cols].astype(BF16)


_PLAN_A = (((ROPE_64, ROPE_64), LOG2E * HEAD_DIM ** -0.5),
           ((ROPE_64, ROPE_64), 1.0),
           ((ROPE_NONE, ROPE_NONE), 1.0))
_PLAN_P = (
    ((ACT_SILU, ACT_SILU), 1.0),
    ((ROPE_NONE, ROPE_NONE), 1.0),
    ((ROPE_NONE, ROPE_32), 1.0),
    ((ACT_SILU, ACT_SILU), 1.0),
    ((ROPE_64, ROPE_64), LOG2E * HEAD_DIM ** -0.5),
    ((ROPE_64_LO, ROPE_64), 1.0),
    ((ROPE_64, ACT_SIGMOID), 1.0),
    ((ACT_SILU, ACT_SILU), 1.0),
    ((ROPE_NONE, ROPE_NONE), LOG2E * HEAD_DIM ** -0.5),
    ((ROPE_NONE, ROPE_NONE), 1.0),
    ((ACT_SILU, ACT_SILU), 1.0),
)
assert len(_PLAN_P) == N_PBLK


_W_IN_SEGMENTS = (("aq", 256), ("ak", 256), ("av", 256), ("ag", 256),
                  ("bcq", 256), ("bckv", 128), ("bkr", 32), ("bg", 256),
                  ("cq", 256), ("ckc", 64), ("cvc", 64), ("cks", 64), ("cvs", 64),
                  ("ckw", 64), ("cvw", 64), ("cgl", 12), ("cg", 256),
                  ("dq", 256), ("dk", 256), ("dv", 256), ("dg", 256))
W_ROWS_COLS = (3 + N_PBLK) * BLK
W_T_ROWS = LANES + BLK


def _pack_w_kernel(wt_ref, rows_ref, t_ref):
    wt = wt_ref[...].astype(F32)
    o = 0
    seg = {}
    for name, width in _W_IN_SEGMENTS:
        seg[name] = wt[o:o + width, :]
        o += width
    gl_pad = jnp.zeros((LANES - N_HEADS * N_BRANCH, wt.shape[1]), wt.dtype)
    rows = [seg["aq"], seg["ak"], seg["av"],
            seg["ag"], seg["bcq"],
            seg["bckv"], seg["bkr"], seg["bkr"], seg["bkr"], seg["bkr"],
            seg["bg"], seg["cq"],
            seg["ckc"], seg["cvc"], seg["cks"], seg["cks"],
            seg["ckw"], seg["ckw"], seg["cgl"], gl_pad,
            seg["cg"], seg["dq"], seg["dk"], seg["dg"]]
    rows_ref[...] = jnp.concatenate(rows, axis=0).T.astype(BF16)
    t_ref[...] = jnp.concatenate([seg["cvs"], seg["cvw"], seg["dv"]], axis=0).astype(BF16)


def _pack_w_in(w_in):
    depth, d_model, d_in = w_in.shape
    assert d_model == D_MODEL and d_in == sum(width for _, width in _W_IN_SEGMENTS)
    tc = PACK_COLS
    return pl.pallas_call(
        _pack_w_kernel,
        grid=(depth, D_MODEL // tc),
        in_specs=[pl.BlockSpec((None, d_in, tc), lambda l, i: (l, 0, i))],
        out_specs=[pl.BlockSpec((None, tc, W_ROWS_COLS), lambda l, i: (l, i, 0)),
                   pl.BlockSpec((None, W_T_ROWS, tc), lambda l, i: (l, 0, i))],
        out_shape=[jax.ShapeDtypeStruct((depth, D_MODEL, W_ROWS_COLS), BF16),
                   jax.ShapeDtypeStruct((depth, W_T_ROWS, D_MODEL), BF16)],
        compiler_params=pltpu.CompilerParams(dimension_semantics=("arbitrary", "arbitrary"),
                                             vmem_limit_bytes=VMEM_LIMIT),
        name="pack_w_in",
    )(jnp.transpose(w_in, (0, 2, 1)).astype(BF16))


def _class_major_perm(dil, tile):
    n = tile // dil
    p = np.zeros((tile, tile), np.float32)
    pos = np.arange(tile)
    p[(pos % dil) * n + pos // dil, pos] = 1.0
    return p


def _in_projection(x2, g_pre, w_rows, w_t, layer, tabs, B, S):
    rows = x2.shape[0]
    tm = IN_TILE
    per_b = S // tm
    nkb = tm // TB
    perm = jnp.asarray(np.stack([_class_major_perm(d, CM_TILE) for d in DILATIONS[1:]]), BF16)
    tab_spec = pl.BlockSpec((tabs[0].shape[0], tm), lambda i: (0, i))
    vt_spec = lambda r: pl.BlockSpec((None, nkb, r, TB), lambda i: (i // per_b, i % per_b, 0, 0))
    a_spec = pl.BlockSpec((tm, 3 * BLK), lambda i: (i, 0))
    a_shape = jax.ShapeDtypeStruct((rows, 3 * BLK), BF16)
    return pl.pallas_call(
        functools.partial(_inproj_kernel, plan_a=_PLAN_A, plan_p=_PLAN_P),
        grid=(rows // tm,),
        in_specs=[pl.BlockSpec((tm, D_MODEL), lambda i: (i, 0)),
                  pl.BlockSpec((1, D_MODEL), lambda i: (0, 0)),
                  pl.BlockSpec((None,) + w_rows.shape[1:], lambda i: (layer, 0, 0)),
                  pl.BlockSpec((None,) + w_t.shape[1:], lambda i: (layer, 0, 0)),
                  pl.BlockSpec(perm.shape, lambda i: (0, 0, 0))] + [tab_spec] * 2,
        out_specs=[a_spec, a_spec, a_spec,
                   pl.BlockSpec((tm, N_PBLK * BLK), lambda i: (i, 0)),
                   vt_spec(LANES), vt_spec(BLK), vt_spec(BLK), vt_spec(BLK)],
        out_shape=[a_shape, a_shape, a_shape,
                   jax.ShapeDtypeStruct((rows, N_PBLK * BLK), BF16),
                   jax.ShapeDtypeStruct((B, S // TB, LANES, TB), BF16)]
                  + [jax.ShapeDtypeStruct((B, S // TB, BLK, TB), BF16)] * 3,
        compiler_params=pltpu.CompilerParams(dimension_semantics=("arbitrary",),
                                             vmem_limit_bytes=VMEM_LIMIT),
        name="in_projection",
    )(x2, g_pre.reshape(1, D_MODEL), w_rows, w_t, perm, *tabs)


def _head_masks(shape):
    lane = lax.broadcasted_iota(jnp.int32, shape, 1)
    return lane < HEAD_DIM


def _masked_heads(q_ref, lo_mask):
    out = []
    for h in range(N_HEADS):
        pair, e = divmod(h, 2)
        qp = q_ref[:, pair * LANES:(pair + 1) * LANES]
        keep = lo_mask if e == 0 else jnp.logical_not(lo_mask)
        out.append(jnp.where(keep, qp, jnp.zeros_like(qp)))
    return out


def _dilated_kernel(an_ref, a4_ref, a16_ref, unperm_ref, expand_ref, o_ref, res_ref, s_ref):
    T = DIL_BLOCK
    S = an_ref.shape[0]
    n_tiles = S // CM_TILE
    lo = _head_masks((T, LANES))
    lane = lax.broadcasted_iota(jnp.int32, (T, LANES), 1)
    dist2 = (lax.broadcasted_iota(jnp.int32, (T, 2 * T), 0)
             - lax.broadcasted_iota(jnp.int32, (T, 2 * T), 1))
    causal1 = (lax.broadcasted_iota(jnp.int32, (T, T), 1)
               <= lax.broadcasted_iota(jnp.int32, (T, T), 0))

    def scores(q, k):
        qm = _masked_heads(q, lo)
        out = []
        for pair in range(N_HEADS // 2):
            both = _dot_nt(jnp.concatenate([qm[2 * pair], qm[2 * pair + 1]], axis=0),
                           k[:, pair * LANES:(pair + 1) * LANES])
            out += [both[:T], both[T:]]
        return tuple(out)

    def finish(n_keys, v, mask, stage_next):
        probs = []
        for h in range(N_HEADS):
            s = jnp.where(mask, s_ref[h, :, 0:n_keys], NEG)
            m = jnp.max(s, axis=-1, keepdims=True)
            p = jnp.exp2(s - m)
            probs.append((m, jnp.sum(p, axis=-1, keepdims=True), p.astype(BF16)))
        stage_next()
        pv = []
        for pair in range(N_HEADS // 2):
            both = _dot(jnp.concatenate([probs[2 * pair][2], probs[2 * pair + 1][2]], axis=0),
                        v[:, pair * LANES:(pair + 1) * LANES])
            pv += [both[:T], both[T:]]
        outs = []
        top = jnp.zeros((T, LANES), F32)
        den = jnp.ones((T, LANES), F32)
        for h in range(N_HEADS):
            m, l, _ = probs[h]
            outs.append(pv[h] * (1.0 / l))
            mine = (lane & (N_HEADS - 1)) == h
            top = jnp.where(mine, m, top)
            den = jnp.where(mine, l, den)
        lse = top + jnp.log2(den)
        hi = lse.astype(BF16).astype(F32)
        parts = jnp.where(lane < N_HEADS, hi,
                          jnp.where(lane < LSE_PARTS * N_HEADS, lse - hi, 0.0))
        return jnp.concatenate(
            [jnp.where(lo, outs[0], outs[1]), jnp.where(lo, outs[2], outs[3]), parts],
            axis=1).astype(BF16)

    def band_mask(delta):
        d = dist2 + delta
        return jnp.logical_and(d >= 0, d <= DIL_BLOCK)

    def run(n_units, n_keys, load_qk, load_v, store, mask_of):
        def stage(u):
            for h, s in enumerate(scores(*load_qk(u))):
                s_ref[h, :, 0:n_keys] = s

        def body(u, carry):
            nxt = jnp.minimum(u + 1, n_units - 1)
            store(u, finish(n_keys, load_v(u), mask_of(u), lambda: stage(nxt)))
            return carry

        stage(0)
        lax.fori_loop(0, n_units, body, 0)

    def rows(start, size):
        return pl.ds(pl.multiple_of(start, 16), size)

    def k0_1(u):
        return T * jnp.maximum(u - 1, 0)

    def store1(u, r):
        res_ref[0, rows(T * u, T), :] = r

    run(S // T, 2 * T,
        lambda u: (an_ref[rows(T * u, T), 0:BLK], an_ref[rows(k0_1(u), 2 * T), BLK:2 * BLK]),
        lambda u: an_ref[rows(k0_1(u), 2 * T), 2 * BLK:3 * BLK],
        store1,
        lambda u: band_mask(T * u - k0_1(u)))

    per4 = CM_TILE // 4
    nq4 = S // 4 // T

    def pieces4(u, blk):
        first_tile = blk * (T // per4)
        return [rows(CM_TILE * (first_tile + k) + per4 * (u // nq4), per4)
                for k in range(T // per4)]

    def kb0_4(u):
        return jnp.maximum(u % nq4 - 1, 0)

    def load4(u, blks, cols):
        return jnp.concatenate([a4_ref[p, cols] for blk in blks for p in pieces4(u, blk)], axis=0)

    def store4(u, r):
        for k, p in enumerate(pieces4(u, u % nq4)):
            res_ref[1, p, :] = r[per4 * k:per4 * (k + 1), :]

    run(4 * nq4, 2 * T,
        lambda u: (load4(u, [u % nq4], slice(0, BLK)),
                   load4(u, [kb0_4(u), kb0_4(u) + 1], slice(BLK, 2 * BLK))),
        lambda u: load4(u, [kb0_4(u), kb0_4(u) + 1], slice(2 * BLK, 3 * BLK)),
        store4,
        lambda u: band_mask(T * (u % nq4 - kb0_4(u))))

    per16 = CM_TILE // 16

    def load16(c, cols):
        return jnp.concatenate([a16_ref[rows(CM_TILE * t + per16 * c, per16), cols]
                                for t in range(n_tiles)], axis=0)

    def store16(c, r):
        for t in range(n_tiles):
            res_ref[2, rows(CM_TILE * t + per16 * c, per16), :] = r[per16 * t:per16 * (t + 1), :]

    run(16, T,
        lambda c: (load16(c, slice(0, BLK)), load16(c, slice(BLK, 2 * BLK))),
        lambda c: load16(c, slice(2 * BLK, 3 * BLK)),
        store16,
        lambda c: causal1)

    expand = expand_ref[...]
    for t in range(n_tiles):
        tile = slice(CM_TILE * t, CM_TILE * (t + 1))
        nat = [res_ref[0, tile, :].astype(F32),
               _dot(unperm_ref[0], res_ref[1, tile, :]),
               _dot(unperm_ref[1], res_ref[2, tile, :])]
        lse = [_dot(r[:, BLK:].astype(BF16), expand) for r in nat]
        top = jnp.maximum(lse[0], jnp.maximum(lse[1], lse[2]))
        w = [jnp.exp2(l - top) for l in lse]
        mixed = (w[0] * nat[0][:, :BLK] + w[1] * nat[1][:, :BLK] + w[2] * nat[2][:, :BLK])
        o_ref[tile, :] = (mixed / (w[0] + w[1] + w[2])).astype(BF16)


def _dilated(aqkv, aqkv4, aqkv16, B, S):
    assert S == 16 * DIL_BLOCK and S % CM_TILE == 0 and DIL_BLOCK % (CM_TILE // 4) == 0
    r3 = lambda t: t.reshape(B, S, 3 * BLK)
    unperm = jnp.asarray(np.stack([_class_major_perm(d, CM_TILE).T for d in DILATIONS[1:]]), BF16)
    expand = np.zeros((LANES, BLK), np.float32)
    for h in range(N_HEADS):
        for part in range(LSE_PARTS):
            expand[N_HEADS * part + h, h * HEAD_DIM:(h + 1) * HEAD_DIM] = 1.0
    a_spec = pl.BlockSpec((None, S, 3 * BLK), lambda b: (b, 0, 0))
    return pl.pallas_call(
        _dilated_kernel,
        grid=(B,),
        in_specs=[a_spec, a_spec, a_spec,
                  pl.BlockSpec(unperm.shape, lambda b: (0, 0, 0)),
                  pl.BlockSpec(expand.shape, lambda b: (0, 0))],
        out_specs=pl.BlockSpec((None, S, BLK), lambda b: (b, 0, 0)),
        out_shape=jax.ShapeDtypeStruct((B, S, BLK), BF16),
        scratch_shapes=[pltpu.VMEM((3, S, BLK + LANES), BF16),
                        pltpu.VMEM((N_HEADS, DIL_BLOCK, 2 * DIL_BLOCK), F32)],
        compiler_params=pltpu.CompilerParams(dimension_semantics=("arbitrary",),
                                             vmem_limit_bytes=VMEM_LIMIT),
        name="dilated_mixture",
    )(r3(aqkv), r3(aqkv4), r3(aqkv16), unperm, jnp.asarray(expand, BF16))


def _masked_head_rows(qt_ref):
    out = []
    zeros = jnp.zeros((HEAD_DIM, qt_ref.shape[-1]), BF16)
    for h in range(N_HEADS):
        own = qt_ref[h * HEAD_DIM:(h + 1) * HEAD_DIM, :]
        out.append(jnp.concatenate([own, zeros] if h % 2 == 0 else [zeros, own], axis=0))
    return out


def _flash_t_kernel(*refs, window, kshared, vrow0, aux):
    it = iter(refs)
    q_all_ref = next(it)
    qx_all_ref = next(it) if aux else None
    k_ref = next(it)
    kx_ref = next(it) if aux else None
    vt_ref = next(it)
    o_ref = next(it)
    s_ref, m_ref, acc_ref = next(it), next(it), next(it)
    T = TB

    krow = lax.broadcasted_iota(jnp.int32, (T, T), 0)
    qcol = lax.broadcasted_iota(jnp.int32, (T, T), 1)
    ones = jnp.ones((ONES_ROWS, T), BF16)

    n_q = q_all_ref.shape[0]
    cross_stage = window is None
    if window is not None:
        o_full = (window - T + 1) // T
        o_none = (window + T - 1) // T + 1
        assert o_none - o_full == 2

    def query_weights(i):
        qcat = _masked_head_rows(q_all_ref.at[i])
        if aux == "per_head":
            group = LANES // N_HEADS
            qcat = [jnp.concatenate(
                [qcat[h]] + [qx_all_ref[i, g * group:(g + 1) * group, :] if g == h
                             else jnp.zeros((group, T), BF16) for g in range(N_HEADS)], axis=0)
                for h in range(N_HEADS)]
        elif aux == "shared":
            qx = qx_all_ref[i]
            qcat = [jnp.concatenate([qcat[h], qx], axis=0) for h in range(N_HEADS)]
        return qcat

    def first_full_block(i):
        return 0 if window is None else jnp.maximum(i - o_full, 0)

    def scores(j, qcat):
        rows = pl.ds(pl.multiple_of(j * T, T), T)
        out = []
        for h in range(N_HEADS):
            pair = h // 2
            cols = slice(0, LANES) if kshared else slice(pair * LANES, (pair + 1) * LANES)
            kb = k_ref[rows, cols]
            if aux:
                kb = jnp.concatenate([kb, kx_ref[rows, :]], axis=1)
            out.append(_dot(kb, qcat[h]))
        return out

    def stage(j, qcat):
        for h, s in enumerate(scores(j, qcat)):
            s_ref[h] = s

    def band_block(i):
        o = o_full + 1
        j = i - o
        limit = jnp.where(j >= 0, window - o * T, -2 * T)
        return jnp.maximum(j, 0), limit

    def stage_query_block(i):
        stage(first_full_block(i), query_weights(i))

    def query_block(i, carry):
        qcat = query_weights(i)

        def absorb(j, mask, staged_next=None, s_all=None):
            vt_all = vt_ref[j]
            probs = []
            for h in range(N_HEADS):
                s = s_ref[h] if s_all is None else s_all[h]
                if mask is not None:
                    s = jnp.where(mask, s, NEG)
                m = m_ref[h]
                m_new = jnp.maximum(m, jnp.max(s, axis=0, keepdims=True))
                alpha = jnp.exp2(m - m_new)
                p = jnp.exp2(s - m_new)
                probs.append((m_new, alpha, p.astype(BF16)))
            if staged_next is not None:
                staged_next()
            for h in range(N_HEADS):
                m_new, alpha, p = probs[h]
                r0 = vrow0 if kshared else h * HEAD_DIM
                lhs = jnp.concatenate([vt_all[r0:r0 + HEAD_DIM, :], ones], axis=0)
                acc_ref[h] = alpha * acc_ref[h] + _dot(lhs, p)
                m_ref[h] = m_new

        m_ref[...] = jnp.full(m_ref.shape, NEG, F32)
        acc_ref[...] = jnp.zeros(acc_ref.shape, F32)
        first = first_full_block(i)
        if window is not None:
            jc, limit = band_block(i)
            s_band = scores(jc, qcat)
            stage(first, qcat)
            absorb(jc, (qcol - krow) <= limit, s_all=s_band)

        def body(j, c):
            absorb(j, None, staged_next=lambda: stage(j + 1, qcat))
            return c

        lax.fori_loop(first, i, body, 0)
        if cross_stage:
            nxt = jnp.minimum(i + 1, n_q - 1)
            absorb(i, krow <= qcol, staged_next=lambda: stage_query_block(nxt))
        else:
            absorb(i, krow <= qcol)

        q_rows = pl.ds(pl.multiple_of(i * T, T), T)
        for pair in range(2):
            outs = []
            for e in range(2):
                acc = acc_ref[2 * pair + e]
                outs.append(acc[0:HEAD_DIM, :] * (1.0 / acc[HEAD_DIM:HEAD_DIM + 1, :]))
            o_ref[q_rows, pair * LANES:(pair + 1) * LANES] = (
                jnp.concatenate(outs, axis=0).T.astype(BF16))
        return carry

    if cross_stage:
        stage_query_block(0)
    lax.fori_loop(0, n_q, query_block, 0)


def _flash_t(qt, k, kcol, kw, vt, *, B, S, window=None, vrow0=0, aux=None, name):
    kshared = kw == LANES
    whole = lambda a: pl.BlockSpec((None,) + a.shape[1:], lambda b: (b,) + (0,) * (a.ndim - 1))
    in_specs = [whole(qt)]
    args = [qt]
    if aux is not None:
        kind, qx, kx, kxcol = aux
        in_specs.append(whole(qx))
        args.append(qx)
    in_specs.append(pl.BlockSpec((None, S, kw), lambda b: (b, 0, kcol)))
    args.append(k)
    if aux is not None:
        if kx.ndim == 3:
            in_specs.append(pl.BlockSpec((None, S, LANES), lambda b: (b, 0, kxcol)))
        else:
            in_specs.append(pl.BlockSpec((S, LANES), lambda b: (0, kxcol)))
        args.append(kx)
    in_specs.append(whole(vt))
    args.append(vt)
    return pl.pallas_call(
        functools.partial(_flash_t_kernel, window=window, kshared=kshared, vrow0=vrow0,
                          aux=None if aux is None else aux[0]),
        grid=(B,),
        in_specs=in_specs,
        out_specs=pl.BlockSpec((None, S, BLK), lambda b: (b, 0, 0)),
        out_shape=jax.ShapeDtypeStruct((B, S, BLK), BF16),
        scratch_shapes=[pltpu.VMEM((N_HEADS, TB, TB), F32),
                        pltpu.VMEM((N_HEADS, 1, TB), F32),
                        pltpu.VMEM((N_HEADS, HEAD_DIM + ONES_ROWS, TB), F32)],
        compiler_params=pltpu.CompilerParams(dimension_semantics=("arbitrary",),
                                             vmem_limit_bytes=VMEM_LIMIT),
        name=name,
    )(*args)


def _mla_up_kernel(cq_ref, ckv_ref, gq_ref, gkv_ref, wq_ref, wk_ref, wvt_ref, cos_ref, sin_ref,
                   qn_ref, qr_ref, kn_ref, vt_ref):
    def norm(t, g):
        t = t.astype(F32)
        ms = jnp.mean(t * t, axis=-1, keepdims=True)
        return ((t * lax.rsqrt(ms + EPS)) * g).astype(BF16)

    scale = LOG2E * (MLA_NOPE + MLA_ROPE) ** -0.5
    q_t = _dot_nt(wq_ref[...], norm(cq_ref[...], gq_ref[...])) * scale
    qn_t = q_t[:BLK, :]
    half = MLA_ROPE // 2
    cos = cos_ref[HEAD_DIM // 2:HEAD_DIM // 2 + half, :]
    sin = sin_ref[HEAD_DIM // 2:HEAD_DIM // 2 + half, :]
    rot = []
    for h in range(N_HEADS):
        x1 = q_t[BLK + h * MLA_ROPE:BLK + h * MLA_ROPE + half, :]
        x2 = q_t[BLK + h * MLA_ROPE + half:BLK + (h + 1) * MLA_ROPE, :]
        rot += [x1 * cos - x2 * sin, x1 * sin + x2 * cos]
    qr_t = jnp.concatenate(rot, axis=0)
    ckv = norm(ckv_ref[...], gkv_ref[...])
    kn_ref[...] = _dot(ckv, wk_ref[...]).astype(BF16)
    vt = _dot_nt(wvt_ref[...], ckv)
    for blk in range(vt.shape[1] // TB):
        cols = slice(blk * TB, (blk + 1) * TB)
        qn_ref[blk] = qn_t[:, cols].astype(BF16)
        qr_ref[blk] = qr_t[:, cols].astype(BF16)
        vt_ref[blk] = vt[:, cols].astype(BF16)


def _mla_up(proj, g_q, g_kv, w_uq, w_ukv, tabs, B, S):
    rows = proj.shape[0]
    tm = MLA_UP_TILE
    per_b = S // tm
    wq = w_uq.reshape(MLA_Q_RANK, N_HEADS, MLA_NOPE + MLA_ROPE)
    wq = jnp.concatenate([wq[:, :, :MLA_NOPE].reshape(MLA_Q_RANK, -1),
                          wq[:, :, MLA_NOPE:].reshape(MLA_Q_RANK, -1)], axis=1).T.astype(BF16)
    wkv = w_ukv.reshape(MLA_KV_RANK, N_HEADS, MLA_NOPE + MLA_V)
    wk = wkv[:, :, :MLA_NOPE].reshape(MLA_KV_RANK, -1).astype(BF16)
    wvt = wkv[:, :, MLA_NOPE:].reshape(MLA_KV_RANK, -1).T.astype(BF16)
    row_spec = lambda w, c: pl.BlockSpec((tm, w), lambda i: (i, c))
    full = lambda a: pl.BlockSpec(a.shape, lambda i: (0, 0))
    gq = g_q.reshape(1, -1)
    gkv = g_kv.reshape(1, -1)
    fm_shape = lambda r: jax.ShapeDtypeStruct((B, S // TB, r, TB), BF16)
    fm_spec = lambda r: pl.BlockSpec((None, tm // TB, r, TB),
                                     lambda i: (i // per_b, i % per_b, 0, 0))
    outs = [fm_shape(BLK), fm_shape(LANES), jax.ShapeDtypeStruct((rows, BLK), BF16), fm_shape(BLK)]
    return pl.pallas_call(
        _mla_up_kernel,
        grid=(rows // tm,),
        in_specs=[row_spec(BLK, PB_BCQ), row_spec(LANES, 2 * PB_BKV), full(gq), full(gkv),
                  full(wq), full(wk), full(wvt)]
                 + [pl.BlockSpec((tabs[0].shape[0], tm), lambda i: (0, i))] * 2,
        out_specs=[fm_spec(BLK), fm_spec(LANES), row_spec(BLK, 0), fm_spec(BLK)],
        out_shape=outs,
        compiler_params=pltpu.CompilerParams(dimension_semantics=("arbitrary",),
                                             vmem_limit_bytes=VMEM_LIMIT),
        name="mla_up",
    )(proj, proj, gq, gkv, wq, wk, wvt, *tabs)


def _compress_kernel(x_ref, perm_ref, pos_ref, w1_ref, w2_ref, kc_ref, vc_ref):
    n_tiles = x_ref.shape[0] // TILE
    per = TILE // NSA_CMP_STRIDE
    slabs = []
    for t in range(n_tiles):
        cm = _dot(perm_ref[...], x_ref[TILE * t:TILE * (t + 1), :])
        slabs.append(jnp.concatenate([cm[per * c:per * (c + 1), :]
                                      for c in range(NSA_CMP_STRIDE)], axis=1))
    x = jnp.concatenate(slabs, axis=0)
    half = x.shape[1]
    top = (x + pos_ref[0:1, :]).astype(BF16)
    bot = (x + pos_ref[1:2, :]).astype(BF16)
    y_top = _dot(top, w1_ref[0:half, :])
    y_bot = _dot(bot, w1_ref[half:2 * half, :])
    pre = y_top + pltpu.roll(y_bot, y_bot.shape[0] - 1, 0)
    hid = (pre * (1.0 / (1.0 + jnp.exp(-pre)))).astype(BF16)
    kc_ref[...] = _dot(hid[:, :NSA_CMP_HID], w2_ref[0]).astype(BF16)
    vc_ref[...] = _dot(hid[:, NSA_CMP_HID:], w2_ref[1]).astype(BF16)


def _compress(proj3, pos_k, pos_v, kw1, kw2, vw1, vw2, B, S):
    assert S % TILE == 0 and NSA_CMP_LEN == 2 * NSA_CMP_STRIDE
    n16 = S // NSA_CMP_STRIDE
    perm = jnp.asarray(_class_major_perm(NSA_CMP_STRIDE, TILE), BF16)
    pos = jnp.concatenate([pos_k, pos_v], axis=1).reshape(2, NSA_CMP_STRIDE * LANES)
    w1 = jnp.zeros((NSA_CMP_LEN, LANES, 2 * NSA_CMP_HID), F32)
    w1 = w1.at[:, :HEAD_DIM, :NSA_CMP_HID].set(kw1.reshape(NSA_CMP_LEN, HEAD_DIM, NSA_CMP_HID))
    w1 = w1.at[:, HEAD_DIM:, NSA_CMP_HID:].set(vw1.reshape(NSA_CMP_LEN, HEAD_DIM, NSA_CMP_HID))
    w1 = w1.reshape(NSA_CMP_LEN * LANES, 2 * NSA_CMP_HID).astype(BF16)
    w2 = jnp.stack([jnp.tile(kw2, (1, 2)), jnp.tile(vw2, (1, 2))]).astype(BF16)
    out = jax.ShapeDtypeStruct((B, n16, LANES), BF16)
    return pl.pallas_call(
        _compress_kernel,
        grid=(B,),
        in_specs=[pl.BlockSpec((None, S, LANES), lambda b: (b, 0, 2 * PB_C0)),
                  pl.BlockSpec(perm.shape, lambda b: (0, 0)),
                  pl.BlockSpec(pos.shape, lambda b: (0, 0)),
                  pl.BlockSpec(w1.shape, lambda b: (0, 0)),
                  pl.BlockSpec(w2.shape, lambda b: (0, 0, 0))],
        out_specs=[pl.BlockSpec((None, n16, LANES), lambda b: (b, 0, 0))] * 2,
        out_shape=[out, out],
        compiler_params=pltpu.CompilerParams(dimension_semantics=("arbitrary",),
                                             vmem_limit_bytes=VMEM_LIMIT),
        name="nsa_compress",
    )(proj3, perm, pos, w1, w2)


def _cmp_topk_kernel(q_ref, kc_ref, vc_ref, ovt_ref, o_ref, nsel_ref, *, T, n_cmp, n_slc):
    i = pl.program_id(1)
    t0 = i * T
    lane = lax.broadcasted_iota(jnp.int32, (T, LANES), 1)
    tok = t0 + lax.broadcasted_iota(jnp.int32, (T, LANES), 0)
    ok = jnp.logical_and(lane * NSA_CMP_STRIDE + (NSA_CMP_LEN - 1) <= tok, lane < n_cmp)
    lo = lane < HEAD_DIM
    qm = _masked_heads(q_ref, lo)
    kc = kc_ref[...]
    vc = vc_ref[...]
    psum = jnp.zeros((T, LANES), F32)
    outs = []
    for h in range(N_HEADS):
        s = jnp.where(ok, _dot_nt(qm[h], kc), NEG)
        mx = jnp.max(s, axis=-1, keepdims=True)
        e = jnp.where(ok, jnp.exp2(s - mx), 0.0)
        den = jnp.sum(e, axis=-1, keepdims=True)
        p = e * (1.0 / jnp.maximum(den, 1e-30))
        psum = psum + p
        outs.append(_dot(p.astype(BF16), vc))
    for pair in range(2):
        o_ref[:, pair * LANES:(pair + 1) * LANES] = jnp.where(
            lo, outs[2 * pair], outs[2 * pair + 1]).astype(BF16)

    p_hi = psum.astype(BF16)
    p_lo = (psum - p_hi.astype(F32)).astype(BF16)
    ovt = ovt_ref[...]
    imp = _dot_nt(ovt, p_hi) + _dot_nt(ovt, p_lo)
    jj = lax.broadcasted_iota(jnp.int32, (n_slc, T), 0)
    bt = (t0 + lax.broadcasted_iota(jnp.int32, (n_slc, T), 1)) >> 6
    forced = jnp.logical_or(jj == 0, jnp.logical_or(jj == bt, jj == bt - 1))
    val = jnp.where(forced, BIG, jnp.where(jj > bt, -BIG, imp))
    sel_rows = []
    for j in range(n_slc):
        vj = val[j:j + 1, :]
        beats = jnp.logical_or(val > vj, jnp.logical_and(val == vj, jj < j))
        rank = jnp.sum(beats.astype(F32), axis=0, keepdims=True)
        sel_rows.append((rank >= NSA_N_SEL).astype(F32))
    sel_rows.append(jnp.zeros((LANES - n_slc, T), F32))
    nsel = jnp.concatenate(sel_rows, axis=0).astype(BF16)
    for blk in range(T // TB):
        nsel_ref[blk] = nsel[:, blk * TB:(blk + 1) * TB]


def _cmp_topk(proj3, kc2, vc2, B, S):
    T = CMP_TILE
    n_cmp = (S - NSA_CMP_LEN) // NSA_CMP_STRIDE + 1
    n_slc = S // NSA_SEL_LEN
    assert NSA_SEL_LEN == 64 and n_slc <= LANES and n_cmp <= LANES
    ci = np.arange(LANES)[None, :] * NSA_CMP_STRIDE
    sj = np.arange(n_slc)[:, None] * NSA_SEL_LEN
    ovt = ((ci < sj + NSA_SEL_LEN) & (ci + NSA_CMP_LEN > sj) & (np.arange(LANES)[None, :] < n_cmp))
    ovt = jnp.asarray(ovt.astype(np.float32), BF16)
    return pl.pallas_call(
        functools.partial(_cmp_topk_kernel, T=T, n_cmp=n_cmp, n_slc=n_slc),
        grid=(B, S // T),
        in_specs=[pl.BlockSpec((None, T, BLK), lambda b, i: (b, i, PB_CQ)),
                  pl.BlockSpec((None, LANES, LANES), lambda b, i: (b, 0, 0)),
                  pl.BlockSpec((None, LANES, LANES), lambda b, i: (b, 0, 0)),
                  pl.BlockSpec(ovt.shape, lambda b, i: (0, 0))],
        out_specs=[pl.BlockSpec((None, T, BLK), lambda b, i: (b, i, 0)),
                   pl.BlockSpec((None, T // TB, LANES, TB), lambda b, i: (b, i, 0, 0))],
        out_shape=[jax.ShapeDtypeStruct((B, S, BLK), BF16),
                   jax.ShapeDtypeStruct((B, S // TB, LANES, TB), BF16)],
        compiler_params=pltpu.CompilerParams(dimension_semantics=("arbitrary", "arbitrary"),
                                             vmem_limit_bytes=VMEM_LIMIT),
        name="nsa_cmp_topk",
    )(proj3, kc2, vc2, ovt)


def _stick_kernel(q_all_ref, k_ref, vt_ref, u_ref, o_ref, za_ref, zb_ref, w_ref, carry_ref,
                  acc_ref):
    T = TB
    krow = lax.broadcasted_iota(jnp.int32, (T, T), 0)
    qcol = lax.broadcasted_iota(jnp.int32, (T, T), 1)
    strict = krow < qcol
    u = u_ref[...]
    z_refs = (za_ref, zb_ref)

    def query_block(i, carry):
        qm = _masked_head_rows(q_all_ref.at[i])

        def stage(j, slot):
            rows = pl.ds(pl.multiple_of(j * T, T), T)
            for h in range(N_HEADS):
                z_refs[slot][h] = _dot(k_ref[rows, (h // 2) * LANES:(h // 2 + 1) * LANES],
                                       qm[h])

        def apply_weights(j):
            vt_all = vt_ref[j]
            for h in range(N_HEADS):
                acc_ref[h] = acc_ref[h] + _dot(vt_all[h * HEAD_DIM:(h + 1) * HEAD_DIM, :], w_ref[h])

        def absorb(j, slot, diag, staged_next, applied_prev):
            z_ref = z_refs[slot]
            softplus = []
            for h in range(N_HEADS):
                z = z_ref[h]
                sp = jnp.maximum(z, 0.0) + jnp.log2(1.0 + jnp.exp2(-jnp.abs(z)))
                if diag:
                    sp = jnp.where(strict, sp, 0.0)
                softplus.append(sp.astype(BF16))
            if staged_next is not None:
                stage(staged_next, 1 - slot)
            if applied_prev is not None:
                apply_weights(applied_prev)
            sums = [_dot(u, sp) for sp in softplus]
            for h in range(N_HEADS):
                a = jnp.exp2(z_ref[h] - (sums[h] + carry_ref[h]))
                if diag:
                    a = jnp.where(strict, a, 0.0)
                w_ref[h] = a.astype(BF16)
                carry_ref[h] = carry_ref[h] + sums[h][0:1, :]

        carry_ref[...] = jnp.zeros(carry_ref.shape, F32)
        acc_ref[...] = jnp.zeros(acc_ref.shape, F32)
        stage(i, 0)
        absorb(i, 0, True, jnp.maximum(i - 1, 0), None)

        def body(t, c):
            j = i - 1 - 2 * t
            absorb(j, 1, False, j - 1, j + 1)
            absorb(j - 1, 0, False, jnp.maximum(j - 2, 0), j)
            return c

        lax.fori_loop(0, i // 2, body, 0)

        @pl.when(i % 2 == 1)
        def _():
            absorb(0, 1, False, None, 1)

        apply_weights(0)
        q_rows = pl.ds(pl.multiple_of(i * T, T), T)
        for pair in range(2):
            o_t = jnp.concatenate([acc_ref[2 * pair], acc_ref[2 * pair + 1]], axis=0)
            o_ref[q_rows, pair * LANES:(pair + 1) * LANES] = o_t.T.astype(BF16)
        return carry

    lax.fori_loop(0, q_all_ref.shape[0], query_block, 0)


def _stick_breaking(qdt, proj3, vdt, B, S):
    u = np.triu(np.ones((TB, TB), np.float32))
    return pl.pallas_call(
        _stick_kernel,
        grid=(B,),
        in_specs=[pl.BlockSpec((None,) + qdt.shape[1:], lambda b: (b, 0, 0, 0)),
                  pl.BlockSpec((None, S, BLK), lambda b: (b, 0, PB_DK)),
                  pl.BlockSpec((None,) + vdt.shape[1:], lambda b: (b, 0, 0, 0)),
                  pl.BlockSpec(u.shape, lambda b: (0, 0))],
        out_specs=pl.BlockSpec((None, S, BLK), lambda b: (b, 0, 0)),
        out_shape=jax.ShapeDtypeStruct((B, S, BLK), BF16),
        scratch_shapes=[pltpu.VMEM((N_HEADS, TB, TB), F32),
                        pltpu.VMEM((N_HEADS, TB, TB), F32),
                        pltpu.VMEM((N_HEADS, TB, TB), BF16),
                        pltpu.VMEM((N_HEADS, 1, TB), F32),
                        pltpu.VMEM((N_HEADS, HEAD_DIM, TB), F32)],
        compiler_params=pltpu.CompilerParams(dimension_semantics=("arbitrary",),
                                             vmem_limit_bytes=VMEM_LIMIT),
        name="stick_breaking",
    )(qdt, proj3, vdt, jnp.asarray(u, BF16))


def _outproj_kernel(x_ref, oa, ob, oc, os_, ow, od,
                    ga, gb, gl, gc, gd, expand_ref, w_ref, g_ref, out_ref):
    f = lambda r: r[...].astype(F32)
    gates = _dot(gl[...], expand_ref[...])
    o_c = (gates[:, 0:BLK] * f(oc) + gates[:, BLK:2 * BLK] * f(os_)
           + gates[:, 2 * BLK:3 * BLK] * f(ow))
    mixed = [f(oa) * f(ga), f(ob) * f(gb), o_c * f(gc), f(od) * f(gd)]
    y = _dot(mixed[0].astype(BF16), w_ref[0:BLK, :])
    for g in range(1, 4):
        y = y + _dot(mixed[g].astype(BF16), w_ref[g * BLK:(g + 1) * BLK, :])
    ms = jnp.mean(y * y, axis=-1, keepdims=True)
    out_ref[...] = x_ref[...] + (y * lax.rsqrt(ms + EPS)) * g_ref[...]


def _out_projection(x2, mixer_outs, proj, w_out, g_post):
    rows = x2.shape[0]
    tm = OUT_TILE
    blk = lambda c: pl.BlockSpec((tm, BLK), lambda i: (i, c))
    expand = np.zeros((LANES, N_BRANCH * BLK), np.float32)
    for h in range(N_HEADS):
        for r in range(N_BRANCH):
            expand[N_BRANCH * h + r, r * BLK + h * HEAD_DIM:r * BLK + (h + 1) * HEAD_DIM] = 1.0
    gate_specs = [blk(PB_AG), blk(PB_BG),
                  pl.BlockSpec((tm, LANES), lambda i: (i, 2 * PB_C1 + 1)), blk(PB_CG), blk(PB_DG)]
    return pl.pallas_call(
        _outproj_kernel,
        grid=(rows // tm,),
        in_specs=[pl.BlockSpec((tm, D_MODEL), lambda i: (i, 0))]
                 + [blk(0)] * len(mixer_outs) + gate_specs
                 + [pl.BlockSpec(expand.shape, lambda i: (0, 0)),
                    pl.BlockSpec((D_MODEL, D_MODEL), lambda i: (0, 0)),
                    pl.BlockSpec((1, D_MODEL), lambda i: (0, 0))],
        out_specs=pl.BlockSpec((tm, D_MODEL), lambda i: (i, 0)),
        out_shape=jax.ShapeDtypeStruct((rows, D_MODEL), F32),
        compiler_params=pltpu.CompilerParams(dimension_semantics=("arbitrary",),
                                             vmem_limit_bytes=VMEM_LIMIT),
        name="out_projection",
    )(x2, *mixer_outs, *([proj] * len(gate_specs)), jnp.asarray(expand, BF16),
      w_out.astype(BF16), g_post.reshape(1, D_MODEL))


def _layer(x2, tabs, B, S, layer, w_rows, w_t, w_out, g_pre, g_post, mla_g_q, mla_g_kv,
           mla_w_uq, mla_w_ukv, nsa_pos_k, nsa_pos_v, nsa_k_w1, nsa_k_w2, nsa_v_w1, nsa_v_w2):
    rows = B * S
    aqkv, aqkv4, aqkv16, proj, vct, vdt, qct, qdt = _in_projection(
        x2, g_pre, w_rows, w_t, layer, tabs, B, S)
    proj3 = proj.reshape(B, S, N_PBLK * BLK)

    o_a = _dilated(aqkv, aqkv4, aqkv16, B, S)

    qnt, qrt, kn, vbt = _mla_up(proj, mla_g_q, mla_g_kv, mla_w_uq, mla_w_ukv, tabs, B, S)
    o_b = _flash_t(qnt, kn.reshape(B, S, BLK), 0, BLK, vbt, B=B, S=S,
                   aux=("per_head", qrt, proj3, 2 * PB_BKV + 1), name="mla_attention")

    kc2, vc2 = _compress(proj3, nsa_pos_k, nsa_pos_v, nsa_k_w1, nsa_k_w2, nsa_v_w1, nsa_v_w2, B, S)
    o_cmp, nsel = _cmp_topk(proj3, kc2, vc2, B, S)
    penalty = np.zeros((S, LANES), np.float32)
    penalty[np.arange(S), np.arange(S) // NSA_SEL_LEN] = MASK_PENALTY
    o_slc = _flash_t(qct, proj3, 2 * PB_C0 + 1, LANES, vct, B=B, S=S, vrow0=0,
                     aux=("shared", nsel, jnp.asarray(penalty, BF16), 0), name="nsa_selected")
    o_win = _flash_t(qct, proj3, 2 * PB_C1, LANES, vct, B=B, S=S, vrow0=HEAD_DIM,
                     window=NSA_WINDOW - 1, name="nsa_window")

    o_d = _stick_breaking(qdt, proj3, vdt, B, S)

    flat = lambda t: t.reshape(rows, BLK)
    outs = [flat(o) for o in (o_a, o_b, o_cmp, o_slc, o_win, o_d)]
    return _out_projection(x2, outs, proj, w_out, g_post)


def kernel(x, positions, w_in, w_out, g_pre, g_post, mla_g_q, mla_g_kv, mla_w_uq, mla_w_ukv,
           nsa_pos_k, nsa_pos_v, nsa_k_w1, nsa_k_w2, nsa_v_w1, nsa_v_w2):
    B, S, D = x.shape
    assert D == D_MODEL and S % 1024 == 0
    x2 = x.reshape(B * S, D)
    tabs = _rope_tables(positions)
    w_rows, w_t = _pack_w_in(w_in)
    for l in range(w_in.shape[0]):
        x2 = _layer(x2, tabs, B, S, l, w_rows, w_t, w_out[l], g_pre[l], g_post[l],
                    mla_g_q[l], mla_g_kv[l], mla_w_uq[l], mla_w_ukv[l],
                    nsa_pos_k[l], nsa_pos_v[l], nsa_k_w1[l], nsa_k_w2[l],
                    nsa_v_w1[l], nsa_v_w2[l])
    return x2.reshape(B, S, D)
```

```python
import functools
import math

import numpy as np
import jax
import jax.numpy as jnp
from jax import lax
from jax.experimental import pallas as pl
from jax.experimental.pallas import tpu as pltpu

D_MODEL = 1024
HEAD_DIM = 64
N_HEADS = 4
ROPE_THETA = 10000.0
EPS = 1e-6
NEG = -1e30
LOG2E = math.log2(math.e)
MASK_PENALTY = -(2.0 ** 100)
BIG = 1e9

MLA_Q_RANK = 256
MLA_KV_RANK = 128
MLA_NOPE = 64
MLA_ROPE = 32
MLA_V = 64

NSA_CMP_LEN = 32
NSA_CMP_STRIDE = 16
NSA_CMP_HID = 256
NSA_SEL_LEN = 64
NSA_N_SEL = 16
NSA_WINDOW = 512

LANES = 128
BLK = 256
IN_TILE = 1024
OUT_TILE = 1024
TILE = 512
MLA_UP_TILE = 2048
ROPE_TILE = 2048
CMP_TILE = 1024
PACK_COLS = 256
CM_TILE = 256
DILATIONS = (1, 4, 16)
DIL_BLOCK = 128
LSE_PARTS = 2
TB = 256
ONES_ROWS = 16
VMEM_LIMIT = 48 * 1024 * 1024

F32 = jnp.float32
BF16 = jnp.bfloat16

(PB_AG, PB_BCQ, PB_BKV, PB_BG, PB_CQ, PB_C0, PB_C1, PB_CG, PB_DQ, PB_DK, PB_DG) = range(11)
N_PBLK = 11
N_BRANCH = 3
ROPE_NONE, ROPE_64, ROPE_32, ROPE_64_LO, ACT_SILU, ACT_SIGMOID = range(6)


def _dot_nt(a, b):
    return lax.dot_general(a, b, (((1,), (1,)), ((), ())), preferred_element_type=F32)


def _dot(a, b):
    return jnp.dot(a, b, preferred_element_type=F32)


def _rope_half(y, cos, sin_signed, group):
    half = group // 2
    lane = lax.broadcasted_iota(jnp.int32, y.shape, 1)
    first = (lane & (group - 1)) < half
    partner = jnp.where(first, pltpu.roll(y, LANES - half, 1), pltpu.roll(y, half, 1))
    return y * cos + partner * sin_signed


def _rope_table_kernel(pos_ref, inv_ref, cos_ref, sin_ref):
    ang = inv_ref[...] * pos_ref[...].astype(F32)
    cos_ref[...] = jnp.cos(ang)
    sin_ref[...] = jnp.sin(ang)


def _rope_tables(positions):
    rows = positions.size
    tm = ROPE_TILE
    inv =np.concatenate([ROPE_THETA ** (-np.arange(n, dtype=np.float64) / n)
                          for n in (HEAD_DIM // 2, MLA_ROPE // 2)]).astype(np.float32)[:, None]
    out = jax.ShapeDtypeStruct((inv.shape[0], rows), F32)
    tab_spec = pl.BlockSpec((inv.shape[0], tm), lambda i: (0, i))
    return pl.pallas_call(
        _rope_table_kernel,
        grid=(rows // tm,),
        in_specs=[pl.BlockSpec((1, tm), lambda i: (0, i)),
                  pl.BlockSpec(inv.shape, lambda i: (0, 0))],
        out_specs=[tab_spec] * 2,
        out_shape=[out] * 2,
        name="rope_tables",
    )(positions.reshape(1, rows), jnp.asarray(inv))


def _token_major_rope(cos_ref, sin_ref, group):
    n = group // 2
    lo = 0 if group == HEAD_DIM else HEAD_DIM // 2
    c = cos_ref[lo:lo + n, :]
    s = sin_ref[lo:lo + n, :]
    reps = LANES // n
    cos = jnp.concatenate([c] * reps, axis=0).T
    sin = jnp.concatenate([-s if k % 2 == 0 else s for k in range(reps)], axis=0).T
    return cos, sin


def _inproj_kernel(x_ref, g_ref, w_ref, wt_ref, perm_ref, cos_ref, sin_ref,
                   aqkv_ref, aqkv4_ref, aqkv16_ref, proj_ref, vct_ref, vdt_ref, qct_ref, qdt_ref,
                   *, plan_a, plan_p):
    x = x_ref[...]
    tm = x.shape[0]
    ms = jnp.mean(x * x, axis=-1, keepdims=True)
    h = ((x * lax.rsqrt(ms + EPS)) * g_ref[...]).astype(BF16)
    lane = lax.broadcasted_iota(jnp.int32, (tm, LANES), 1)
    c64, s64 = _token_major_rope(cos_ref, sin_ref, HEAD_DIM)
    c32, s32 = _token_major_rope(cos_ref, sin_ref, MLA_ROPE)

    def epilogue(y, ops, scale):
        halves = []
        for hf, op in enumerate(ops):
            yh = y[:, hf * LANES:(hf + 1) * LANES]
            if op == ROPE_64:
                yh = _rope_half(yh, c64, s64, 64)
            elif op == ROPE_32:
                yh = _rope_half(yh, c32, s32, 32)
            elif op == ROPE_64_LO:
                yh = jnp.where(lane < 64, _rope_half(yh, c64, s64, 64), yh)
            elif op == ACT_SILU:
                yh = yh * (1.0 / (1.0 + jnp.exp(-yh)))
            elif op == ACT_SIGMOID:
                yh = 1.0 / (1.0 + jnp.exp(-yh))
            if scale != 1.0:
                yh = yh * scale
            halves.append(yh)
        return jnp.concatenate(halves, axis=1)

    feature_major = {PB_CQ: qct_ref, PB_DQ: qdt_ref}
    col = 0
    for out_ref, plan in ((aqkv_ref, plan_a), (proj_ref, plan_p)):
        for b, (ops, scale) in enumerate(plan):
            y = epilogue(_dot(h, w_ref[:, col:col + BLK]), ops, scale)
            out_ref[:, b * BLK:(b + 1) * BLK] = y.astype(BF16)
            if out_ref is proj_ref and b in feature_major:
                y_t = y.T
                for blk in range(tm // TB):
                    feature_major[b][blk] = y_t[:, blk * TB:(blk + 1) * TB].astype(BF16)
            col += BLK

    for t in range(tm // CM_TILE):
        tile = slice(CM_TILE * t, CM_TILE * (t + 1))
        a_rows = aqkv_ref[tile, :]
        aqkv4_ref[tile, :] = _dot(perm_ref[0], a_rows).astype(BF16)
        aqkv16_ref[tile, :] = _dot(perm_ref[1], a_rows).astype(BF16)

    vt = _dot_nt(wt_ref[...], h)
    for blk in range(tm // TB):
        cols = slice(blk * TB, (blk + 1) * TB)
        vct_ref[blk] = vt[0:LANES, cols].astype(BF16)
        vdt_ref[blk] = vt[LANES:LANES + BLK, cols].astype(BF16)


_PLAN_A = (((ROPE_64, ROPE_64), LOG2E * HEAD_DIM ** -0.5),
           ((ROPE_64, ROPE_64), 1.0),
           ((ROPE_NONE, ROPE_NONE), 1.0))
_PLAN_P = (
    ((ACT_SILU, ACT_SILU), 1.0),
    ((ROPE_NONE, ROPE_NONE), 1.0),
    ((ROPE_NONE, ROPE_32), 1.0),
    ((ACT_SILU, ACT_SILU), 1.0),
    ((ROPE_64, ROPE_64), LOG2E * HEAD_DIM ** -0.5),
    ((ROPE_64_LO, ROPE_64), 1.0),
    ((ROPE_64, ACT_SIGMOID), 1.0),
    ((ACT_SILU, ACT_SILU), 1.0),
    ((ROPE_NONE, ROPE_NONE), LOG2E * HEAD_DIM ** -0.5),
    ((ROPE_NONE, ROPE_NONE), 1.0),
    ((ACT_SILU, ACT_SILU), 1.0),
)
assert len(_PLAN_P) == N_PBLK


_W_IN_SEGMENTS = (("aq", 256), ("ak", 256), ("av", 256), ("ag", 256),
                  ("bcq", 256), ("bckv", 128), ("bkr", 32), ("bg", 256),
                  ("cq", 256), ("ckc", 64), ("cvc", 64), ("cks", 64), ("cvs", 64),
                  ("ckw", 64), ("cvw", 64), ("cgl", 12), ("cg", 256),
                  ("dq", 256), ("dk", 256), ("dv", 256), ("dg", 256))
W_ROWS_COLS = (3 + N_PBLK) * BLK
W_T_ROWS = LANES + BLK


def _pack_w_kernel(wt_ref, rows_ref, t_ref):
    wt = wt_ref[...].astype(F32)
    o = 0
    seg = {}
    for name, width in _W_IN_SEGMENTS:
        seg[name] = wt[o:o + width, :]
        o += width
    gl_pad = jnp.zeros((LANES - N_HEADS * N_BRANCH, wt.shape[1]), wt.dtype)
    rows = [seg["aq"], seg["ak"], seg["av"],
            seg["ag"], seg["bcq"],
            seg["bckv"], seg["bkr"], seg["bkr"], seg["bkr"], seg["bkr"],
            seg["bg"], seg["cq"],
            seg["ckc"], seg["cvc"], seg["cks"], seg["cks"],
            seg["ckw"], seg["ckw"], seg["cgl"], gl_pad,
            seg["cg"], seg["dq"], seg["dk"], seg["dg"]]
    rows_ref[...] = jnp.concatenate(rows, axis=0).T.astype(BF16)
    t_ref[...] = jnp.concatenate([seg["cvs"], seg["cvw"], seg["dv"]], axis=0).astype(BF16)


def _pack_w_in(w_in):
    depth, d_model, d_in = w_in.shape
    assert d_model == D_MODEL and d_in == sum(width for _, width in _W_IN_SEGMENTS)
    tc = PACK_COLS
    return pl.pallas_call(
        _pack_w_kernel,
        grid=(depth, D_MODEL // tc),
        in_specs=[pl.BlockSpec((None, d_in, tc), lambda l, i: (l, 0, i))],
        out_specs=[pl.BlockSpec((None, tc, W_ROWS_COLS), lambda l, i: (l, i, 0)),
                   pl.BlockSpec((None, W_T_ROWS, tc), lambda l, i: (l, 0, i))],
        out_shape=[jax.ShapeDtypeStruct((depth, D_MODEL, W_ROWS_COLS), BF16),
                   jax.ShapeDtypeStruct((depth, W_T_ROWS, D_MODEL), BF16)],
        compiler_params=pltpu.CompilerParams(dimension_semantics=("arbitrary", "arbitrary"),
                                             vmem_limit_bytes=VMEM_LIMIT),
        name="pack_w_in",
    )(jnp.transpose(w_in, (0, 2, 1)).astype(BF16))


def _class_major_perm(dil, tile):
    n = tile // dil
    p = np.zeros((tile, tile), np.float32)
    pos = np.arange(tile)
    p[(pos % dil) * n + pos // dil, pos] = 1.0
    return p


def _in_projection(x2, g_pre, w_rows, w_t, layer, tabs, B, S):
    rows = x2.shape[0]
    tm = IN_TILE
    per_b = S // tm
    nkb = tm // TB
    perm = jnp.asarray(np.stack([_class_major_perm(d, CM_TILE) for d in DILATIONS[1:]]), BF16)
    tab_spec = pl.BlockSpec((tabs[0].shape[0], tm), lambda i: (0, i))
    vt_spec = lambda r: pl.BlockSpec((None, nkb, r, TB), lambda i: (i // per_b, i % per_b, 0, 0))
    a_spec = pl.BlockSpec((tm, 3 * BLK), lambda i: (i, 0))
    a_shape = jax.ShapeDtypeStruct((rows, 3 * BLK), BF16)
    return pl.pallas_call(
        functools.partial(_inproj_kernel, plan_a=_PLAN_A, plan_p=_PLAN_P),
        grid=(rows // tm,),
        in_specs=[pl.BlockSpec((tm, D_MODEL), lambda i: (i, 0)),
                  pl.BlockSpec((1, D_MODEL), lambda i: (0, 0)),
                  pl.BlockSpec((None,) + w_rows.shape[1:], lambda i: (layer, 0, 0)),
                  pl.BlockSpec((None,) + w_t.shape[1:], lambda i: (layer, 0, 0)),
                  pl.BlockSpec(perm.shape, lambda i: (0, 0, 0))] + [tab_spec] * 2,
        out_specs=[a_spec, a_spec, a_spec,
                   pl.BlockSpec((tm, N_PBLK * BLK), lambda i: (i, 0)),
                   vt_spec(LANES), vt_spec(BLK), vt_spec(BLK), vt_spec(BLK)],
        out_shape=[a_shape, a_shape, a_shape,
                   jax.ShapeDtypeStruct((rows, N_PBLK * BLK), BF16),
                   jax.ShapeDtypeStruct((B, S // TB, LANES, TB), BF16)]
                  + [jax.ShapeDtypeStruct((B, S // TB, BLK, TB), BF16)] * 3,
        compiler_params=pltpu.CompilerParams(dimension_semantics=("arbitrary",),
                                             vmem_limit_bytes=VMEM_LIMIT),
        name="in_projection",
    )(x2, g_pre.reshape(1, D_MODEL), w_rows, w_t, perm, *tabs)


def _head_masks(shape):
    lane = lax.broadcasted_iota(jnp.int32, shape, 1)
    return lane < HEAD_DIM


def _masked_heads(q_ref, lo_mask):
    out = []
    for h in range(N_HEADS):
        pair, e = divmod(h, 2)
        qp = q_ref[:, pair * LANES:(pair + 1) * LANES]
        keep = lo_mask if e == 0 else jnp.logical_not(lo_mask)
        out.append(jnp.where(keep, qp, jnp.zeros_like(qp)))
    return out


def _dilated_kernel(an_ref, a4_ref, a16_ref, unperm_ref, expand_ref, o_ref, res_ref, s_ref):
    T = DIL_BLOCK
    S = an_ref.shape[0]
    n_tiles = S // CM_TILE
    lo = _head_masks((T, LANES))
    lane = lax.broadcasted_iota(jnp.int32, (T, LANES), 1)
    dist2 = (lax.broadcasted_iota(jnp.int32, (T, 2 * T), 0)
             - lax.broadcasted_iota(jnp.int32, (T, 2 * T), 1))
    causal1 = (lax.broadcasted_iota(jnp.int32, (T, T), 1)
               <= lax.broadcasted_iota(jnp.int32, (T, T), 0))

    def scores(q, k):
        qm = _masked_heads(q, lo)
        out = []
        for pair in range(N_HEADS // 2):
            both = _dot_nt(jnp.concatenate([qm[2 * pair], qm[2 * pair + 1]], axis=0),
                           k[:, pair * LANES:(pair + 1) * LANES])
            out += [both[:T], both[T:]]
        return tuple(out)

    def finish(n_keys, v, mask, stage_next):
        probs = []
        for h in range(N_HEADS):
            s = jnp.where(mask, s_ref[h, :, 0:n_keys], NEG)
            m = jnp.max(s, axis=-1, keepdims=True)
            p = jnp.exp2(s - m)
            probs.append((m, jnp.sum(p, axis=-1, keepdims=True), p.astype(BF16)))
        stage_next()
        pv = []
        for pair in range(N_HEADS // 2):
            both = _dot(jnp.concatenate([probs[2 * pair][2], probs[2 * pair + 1][2]], axis=0),
                        v[:, pair * LANES:(pair + 1) * LANES])
            pv += [both[:T], both[T:]]
        outs = []
        top = jnp.zeros((T, LANES), F32)
        den = jnp.ones((T, LANES), F32)
        for h in range(N_HEADS):
            m, l, _ = probs[h]
            outs.append(pv[h] * (1.0 / l))
            mine = (lane & (N_HEADS - 1)) == h
            top = jnp.where(mine, m, top)
            den = jnp.where(mine, l, den)
        lse = top + jnp.log2(den)
        hi = lse.astype(BF16).astype(F32)
        parts = jnp.where(lane < N_HEADS, hi,
                          jnp.where(lane < LSE_PARTS * N_HEADS, lse - hi, 0.0))
        return jnp.concatenate(
            [jnp.where(lo, outs[0], outs[1]), jnp.where(lo, outs[2], outs[3]), parts],
            axis=1).astype(BF16)

    def band_mask(delta):
        d = dist2 + delta
        return jnp.logical_and(d >= 0, d <= DIL_BLOCK)

    def run(n_units, n_keys, load_qk, load_v, store, mask_of):
        def stage(u):
            for h, s in enumerate(scores(*load_qk(u))):
                s_ref[h, :, 0:n_keys] = s

        def body(u, carry):
            nxt = jnp.minimum(u + 1, n_units - 1)
            store(u, finish(n_keys, load_v(u), mask_of(u), lambda: stage(nxt)))
            return carry

        stage(0)
        lax.fori_loop(0, n_units, body, 0)

    def rows(start, size):
        return pl.ds(pl.multiple_of(start, 16), size)

    def k0_1(u):
        return T * jnp.maximum(u - 1, 0)

    def store1(u, r):
        res_ref[0, rows(T * u, T), :] = r

    run(S // T, 2 * T,
        lambda u: (an_ref[rows(T * u, T), 0:BLK], an_ref[rows(k0_1(u), 2 * T), BLK:2 * BLK]),
        lambda u: an_ref[rows(k0_1(u), 2 * T), 2 * BLK:3 * BLK],
        store1,
        lambda u: band_mask(T * u - k0_1(u)))

    per4 = CM_TILE // 4
    nq4 = S // 4 // T

    def pieces4(u, blk):
        first_tile = blk * (T // per4)
        return [rows(CM_TILE * (first_tile + k) + per4 * (u // nq4), per4)
                for k in range(T // per4)]

    def kb0_4(u):
        return jnp.maximum(u % nq4 - 1, 0)

    def load4(u, blks, cols):
        return jnp.concatenate([a4_ref[p, cols] for blk in blks for p in pieces4(u, blk)], axis=0)

    def store4(u, r):
        for k, p in enumerate(pieces4(u, u % nq4)):
            res_ref[1, p, :] = r[per4 * k:per4 * (k + 1), :]

    run(4 * nq4, 2 * T,
        lambda u: (load4(u, [u % nq4], slice(0, BLK)),
                   load4(u, [kb0_4(u), kb0_4(u) + 1], slice(BLK, 2 * BLK))),
        lambda u: load4(u, [kb0_4(u), kb0_4(u) + 1], slice(2 * BLK, 3 * BLK)),
        store4,
        lambda u: band_mask(T * (u % nq4 - kb0_4(u))))

    per16 = CM_TILE // 16

    def load16(c, cols):
        return jnp.concatenate([a16_ref[rows(CM_TILE * t + per16 * c, per16), cols]
                                for t in range(n_tiles)], axis=0)

    def store16(c, r):
        for t in range(n_tiles):
            res_ref[2, rows(CM_TILE * t + per16 * c, per16), :] = r[per16 * t:per16 * (t + 1), :]

    run(16, T,
        lambda c: (load16(c, slice(0, BLK)), load16(c, slice(BLK, 2 * BLK))),
        lambda c: load16(c, slice(2 * BLK, 3 * BLK)),
        store16,
        lambda c: causal1)

    expand = expand_ref[...]
    for t in range(n_tiles):
        tile = slice(CM_TILE * t, CM_TILE * (t + 1))
        nat = [res_ref[0, tile, :].astype(F32),
               _dot(unperm_ref[0], res_ref[1, tile, :]),
               _dot(unperm_ref[1], res_ref[2, tile, :])]
        lse = [_dot(r[:, BLK:].astype(BF16), expand) for r in nat]
        top = jnp.maximum(lse[0], jnp.maximum(lse[1], lse[2]))
        w = [jnp.exp2(l - top) for l in lse]
        mixed = (w[0] * nat[0][:, :BLK] + w[1] * nat[1][:, :BLK] + w[2] * nat[2][:, :BLK])
        o_ref[tile, :] = (mixed / (w[0] + w[1] + w[2])).astype(BF16)


def _dilated(aqkv, aqkv4, aqkv16, B, S):
    assert S == 16 * DIL_BLOCK and S % CM_TILE == 0 and DIL_BLOCK % (CM_TILE // 4) == 0
    r3 = lambda t: t.reshape(B, S, 3 * BLK)
    unperm = jnp.asarray(np.stack([_class_major_perm(d, CM_TILE).T for d in DILATIONS[1:]]), BF16)
    expand = np.zeros((LANES, BLK), np.float32)
    for h in range(N_HEADS):
        for part in range(LSE_PARTS):
            expand[N_HEADS * part + h, h * HEAD_DIM:(h + 1) * HEAD_DIM] = 1.0
    a_spec = pl.BlockSpec((None, S, 3 * BLK), lambda b: (b, 0, 0))
    return pl.pallas_call(
        _dilated_kernel,
        grid=(B,),
        in_specs=[a_spec, a_spec, a_spec,
                  pl.BlockSpec(unperm.shape, lambda b: (0, 0, 0)),
                  pl.BlockSpec(expand.shape, lambda b: (0, 0))],
        out_specs=pl.BlockSpec((None, S, BLK), lambda b: (b, 0, 0)),
        out_shape=jax.ShapeDtypeStruct((B, S, BLK), BF16),
        scratch_shapes=[pltpu.VMEM((3, S, BLK + LANES), BF16),
                        pltpu.VMEM((N_HEADS, DIL_BLOCK, 2 * DIL_BLOCK), F32)],
        compiler_params=pltpu.CompilerParams(dimension_semantics=("arbitrary",),
                                             vmem_limit_bytes=VMEM_LIMIT),
        name="dilated_mixture",
    )(r3(aqkv), r3(aqkv4), r3(aqkv16), unperm, jnp.asarray(expand, BF16))


def _masked_head_rows(qt_ref):
    out = []
    zeros = jnp.zeros((HEAD_DIM, qt_ref.shape[-1]), BF16)
    for h in range(N_HEADS):
        own = qt_ref[h * HEAD_DIM:(h + 1) * HEAD_DIM, :]
        out.append(jnp.concatenate([own, zeros] if h % 2 == 0 else [zeros, own], axis=0))
    return out


def _flash_t_kernel(*refs, window, kshared, vrow0, aux):
    it = iter(refs)
    q_all_ref = next(it)
    qx_all_ref = next(it) if aux else None
    k_ref = next(it)
    kx_ref = next(it) if aux else None
    vt_ref = next(it)
    o_ref = next(it)
    s_ref, m_ref, acc_ref = next(it), next(it), next(it)
    T = TB

    krow = lax.broadcasted_iota(jnp.int32, (T, T), 0)
    qcol = lax.broadcasted_iota(jnp.int32, (T, T), 1)
    ones = jnp.ones((ONES_ROWS, T), BF16)

    n_q = q_all_ref.shape[0]
    cross_stage = window is None
    if window is not None:
        o_full = (window - T + 1) // T
        o_none = (window + T - 1) // T + 1
        assert o_none - o_full == 2

    def query_weights(i):
        qcat = _masked_head_rows(q_all_ref.at[i])
        if aux == "per_head":
            group = LANES // N_HEADS
            qcat = [jnp.concatenate(
                [qcat[h]] + [qx_all_ref[i, g * group:(g + 1) * group, :] if g == h
                             else jnp.zeros((group, T), BF16) for g in range(N_HEADS)], axis=0)
                for h in range(N_HEADS)]
        elif aux == "shared":
            qx = qx_all_ref[i]
            qcat = [jnp.concatenate([qcat[h], qx], axis=0) for h in range(N_HEADS)]
        return qcat

    def first_full_block(i):
        return 0 if window is None else jnp.maximum(i - o_full, 0)

    def scores(j, qcat):
        rows = pl.ds(pl.multiple_of(j * T, T), T)
        out = []
        for h in range(N_HEADS):
            pair = h // 2
            cols = slice(0, LANES) if kshared else slice(pair * LANES, (pair + 1) * LANES)
            kb = k_ref[rows, cols]
            if aux:
                kb = jnp.concatenate([kb, kx_ref[rows, :]], axis=1)
            out.append(_dot(kb, qcat[h]))
        return out

    def stage(j, qcat):
        for h, s in enumerate(scores(j, qcat)):
            s_ref[h] = s

    def band_block(i):
        o = o_full + 1
        j = i - o
        limit = jnp.where(j >= 0, window - o * T, -2 * T)
        return jnp.maximum(j, 0), limit

    def stage_query_block(i):
        stage(first_full_block(i), query_weights(i))

    def query_block(i, carry):
        qcat = query_weights(i)

        def absorb(j, mask, staged_next=None, s_all=None):
            vt_all = vt_ref[j]
            probs = []
            for h in range(N_HEADS):
                s = s_ref[h] if s_all is None else s_all[h]
                if mask is not None:
                    s = jnp.where(mask, s, NEG)
                m = m_ref[h]
                m_new = jnp.maximum(m, jnp.max(s, axis=0, keepdims=True))
                alpha = jnp.exp2(m - m_new)
                p = jnp.exp2(s - m_new)
                probs.append((m_new, alpha, p.astype(BF16)))
            if staged_next is not None:
                staged_next()
            for h in range(N_HEADS):
                m_new, alpha, p = probs[h]
                r0 = vrow0 if kshared else h * HEAD_DIM
                lhs = jnp.concatenate([vt_all[r0:r0 + HEAD_DIM, :], ones], axis=0)
                acc_ref[h] = alpha * acc_ref[h] + _dot(lhs, p)
                m_ref[h] = m_new

        m_ref[...] = jnp.full(m_ref.shape, NEG, F32)
        acc_ref[...] = jnp.zeros(acc_ref.shape, F32)
        first = first_full_block(i)
        if window is not None:
            jc, limit = band_block(i)
            s_band = scores(jc, qcat)
            stage(first, qcat)
            absorb(jc, (qcol - krow) <= limit, s_all=s_band)

        def body(j, c):
            absorb(j, None, staged_next=lambda: stage(j + 1, qcat))
            return c

        lax.fori_loop(first, i, body, 0)
        if cross_stage:
            nxt = jnp.minimum(i + 1, n_q - 1)
            absorb(i, krow <= qcol, staged_next=lambda: stage_query_block(nxt))
        else:
            absorb(i, krow <= qcol)

        q_rows = pl.ds(pl.multiple_of(i * T, T), T)
        for pair in range(2):
            outs = []
            for e in range(2):
                acc = acc_ref[2 * pair + e]
                outs.append(acc[0:HEAD_DIM, :] * (1.0 / acc[HEAD_DIM:HEAD_DIM + 1, :]))
            o_ref[q_rows, pair * LANES:(pair + 1) * LANES] = (
                jnp.concatenate(outs, axis=0).T.astype(BF16))
        return carry

    if cross_stage:
        stage_query_block(0)
    lax.fori_loop(0, n_q, query_block, 0)


def _flash_t(qt, k, kcol, kw, vt, *, B, S, window=None, vrow0=0, aux=None, name):
    kshared = kw == LANES
    whole = lambda a: pl.BlockSpec((None,) + a.shape[1:], lambda b: (b,) + (0,) * (a.ndim - 1))
    in_specs = [whole(qt)]
    args = [qt]
    if aux is not None:
        kind, qx, kx, kxcol = aux
        in_specs.append(whole(qx))
        args.append(qx)
    in_specs.append(pl.BlockSpec((None, S, kw), lambda b: (b, 0, kcol)))
    args.append(k)
    if aux is not None:
        if kx.ndim == 3:
            in_specs.append(pl.BlockSpec((None, S, LANES), lambda b: (b, 0, kxcol)))
        else:
            in_specs.append(pl.BlockSpec((S, LANES), lambda b: (0, kxcol)))
        args.append(kx)
    in_specs.append(whole(vt))
    args.append(vt)
    return pl.pallas_call(
        functools.partial(_flash_t_kernel, window=window, kshared=kshared, vrow0=vrow0,
                          aux=None if aux is None else aux[0]),
        grid=(B,),
        in_specs=in_specs,
        out_specs=pl.BlockSpec((None, S, BLK), lambda b: (b, 0, 0)),
        out_shape=jax.ShapeDtypeStruct((B, S, BLK), BF16),
        scratch_shapes=[pltpu.VMEM((N_HEADS, TB, TB), F32),
                        pltpu.VMEM((N_HEADS, 1, TB), F32),
                        pltpu.VMEM((N_HEADS, HEAD_DIM + ONES_ROWS, TB), F32)],
        compiler_params=pltpu.CompilerParams(dimension_semantics=("arbitrary",),
                                             vmem_limit_bytes=VMEM_LIMIT),
        name=name,
    )(*args)


def _mla_up_kernel(cq_ref, ckv_ref, gq_ref, gkv_ref, wq_ref, wk_ref, wvt_ref, cos_ref, sin_ref,
                   qn_ref, qr_ref, kn_ref, vt_ref):
    def norm(t, g):
        t = t.astype(F32)
        ms = jnp.mean(t * t, axis=-1, keepdims=True)
        return ((t * lax.rsqrt(ms + EPS)) * g).astype(BF16)

    scale = LOG2E * (MLA_NOPE + MLA_ROPE) ** -0.5
    q_t = _dot_nt(wq_ref[...], norm(cq_ref[...], gq_ref[...])) * scale
    qn_t = q_t[:BLK, :]
    half = MLA_ROPE // 2
    cos = cos_ref[HEAD_DIM // 2:HEAD_DIM // 2 + half, :]
    sin = sin_ref[HEAD_DIM // 2:HEAD_DIM // 2 + half, :]
    rot = []
    for h in range(N_HEADS):
        x1 = q_t[BLK + h * MLA_ROPE:BLK + h * MLA_ROPE + half, :]
        x2 = q_t[BLK + h * MLA_ROPE + half:BLK + (h + 1) * MLA_ROPE, :]
        rot += [x1 * cos - x2 * sin, x1 * sin + x2 * cos]
    qr_t = jnp.concatenate(rot, axis=0)
    ckv = norm(ckv_ref[...], gkv_ref[...])
    kn_ref[...] = _dot(ckv, wk_ref[...]).astype(BF16)
    vt = _dot_nt(wvt_ref[...], ckv)
    for blk in range(vt.shape[1] // TB):
        cols = slice(blk * TB, (blk + 1) * TB)
        qn_ref[blk] = qn_t[:, cols].astype(BF16)
        qr_ref[blk] = qr_t[:, cols].astype(BF16)
        vt_ref[blk] = vt[:, cols].astype(BF16)


def _mla_up(proj, g_q, g_kv, w_uq, w_ukv, tabs, B, S):
    rows = proj.shape[0]
    tm = MLA_UP_TILE
    per_b = S // tm
    wq = w_uq.reshape(MLA_Q_RANK, N_HEADS, MLA_NOPE + MLA_ROPE)
    wq = jnp.concatenate([wq[:, :, :MLA_NOPE].reshape(MLA_Q_RANK, -1),
                          wq[:, :, MLA_NOPE:].reshape(MLA_Q_RANK, -1)], axis=1).T.astype(BF16)
    wkv = w_ukv.reshape(MLA_KV_RANK, N_HEADS, MLA_NOPE + MLA_V)
    wk = wkv[:, :, :MLA_NOPE].reshape(MLA_KV_RANK, -1).astype(BF16)
    wvt = wkv[:, :, MLA_NOPE:].reshape(MLA_KV_RANK, -1).T.astype(BF16)
    row_spec = lambda w, c: pl.BlockSpec((tm, w), lambda i: (i, c))
    full = lambda a: pl.BlockSpec(a.shape, lambda i: (0, 0))
    gq = g_q.reshape(1, -1)
    gkv = g_kv.reshape(1, -1)
    fm_shape = lambda r: jax.ShapeDtypeStruct((B, S // TB, r, TB), BF16)
    fm_spec = lambda r: pl.BlockSpec((None, tm // TB, r, TB),
                                     lambda i: (i // per_b, i % per_b, 0, 0))
    outs = [fm_shape(BLK), fm_shape(LANES), jax.ShapeDtypeStruct((rows, BLK), BF16), fm_shape(BLK)]
    return pl.pallas_call(
        _mla_up_kernel,
        grid=(rows // tm,),
        in_specs=[row_spec(BLK, PB_BCQ), row_spec(LANES, 2 * PB_BKV), full(gq), full(gkv),
                  full(wq), full(wk), full(wvt)]
                 + [pl.BlockSpec((tabs[0].shape[0], tm), lambda i: (0, i))] * 2,
        out_specs=[fm_spec(BLK), fm_spec(LANES), row_spec(BLK, 0), fm_spec(BLK)],
        out_shape=outs,
        compiler_params=pltpu.CompilerParams(dimension_semantics=("arbitrary",),
                                             vmem_limit_bytes=VMEM_LIMIT),
        name="mla_up",
    )(proj, proj, gq, gkv, wq, wk, wvt, *tabs)


def _compress_kernel(x_ref, perm_ref, pos_ref, w1_ref, w2_ref, kc_ref, vc_ref):
    n_tiles = x_ref.shape[0] // TILE
    per = TILE // NSA_CMP_STRIDE
    slabs = []
    for t in range(n_tiles):
        cm = _dot(perm_ref[...], x_ref[TILE * t:TILE * (t + 1), :])
        slabs.append(jnp.concatenate([cm[per * c:per * (c + 1), :]
                                      for c in range(NSA_CMP_STRIDE)], axis=1))
    x = jnp.concatenate(slabs, axis=0)
    half = x.shape[1]
    top = (x + pos_ref[0:1, :]).astype(BF16)
    bot = (x + pos_ref[1:2, :]).astype(BF16)
    y_top = _dot(top, w1_ref[0:half, :])
    y_bot = _dot(bot, w1_ref[half:2 * half, :])
    pre = y_top + pltpu.roll(y_bot, y_bot.shape[0] - 1, 0)
    hid = (pre * (1.0 / (1.0 + jnp.exp(-pre)))).astype(BF16)
    kc_ref[...] = _dot(hid[:, :NSA_CMP_HID], w2_ref[0]).astype(BF16)
    vc_ref[...] = _dot(hid[:, NSA_CMP_HID:], w2_ref[1]).astype(BF16)


def _compress(proj3, pos_k, pos_v, kw1, kw2, vw1, vw2, B, S):
    assert S % TILE == 0 and NSA_CMP_LEN == 2 * NSA_CMP_STRIDE
    n16 = S // NSA_CMP_STRIDE
    perm = jnp.asarray(_class_major_perm(NSA_CMP_STRIDE, TILE), BF16)
    pos = jnp.concatenate([pos_k, pos_v], axis=1).reshape(2, NSA_CMP_STRIDE * LANES)
    w1 = jnp.zeros((NSA_CMP_LEN, LANES, 2 * NSA_CMP_HID), F32)
    w1 = w1.at[:, :HEAD_DIM, :NSA_CMP_HID].set(kw1.reshape(NSA_CMP_LEN, HEAD_DIM, NSA_CMP_HID))
    w1 = w1.at[:, HEAD_DIM:, NSA_CMP_HID:].set(vw1.reshape(NSA_CMP_LEN, HEAD_DIM, NSA_CMP_HID))
    w1 = w1.reshape(NSA_CMP_LEN * LANES, 2 * NSA_CMP_HID).astype(BF16)
    w2 = jnp.stack([jnp.tile(kw2, (1, 2)), jnp.tile(vw2, (1, 2))]).astype(BF16)
    out = jax.ShapeDtypeStruct((B, n16, LANES), BF16)
    return pl.pallas_call(
        _compress_kernel,
        grid=(B,),
        in_specs=[pl.BlockSpec((None, S, LANES), lambda b: (b, 0, 2 * PB_C0)),
                  pl.BlockSpec(perm.shape, lambda b: (0, 0)),
                  pl.BlockSpec(pos.shape, lambda b: (0, 0)),
                  pl.BlockSpec(w1.shape, lambda b: (0, 0)),
                  pl.BlockSpec(w2.shape, lambda b: (0, 0, 0))],
        out_specs=[pl.BlockSpec((None, n16, LANES), lambda b: (b, 0, 0))] * 2,
        out_shape=[out, out],
        compiler_params=pltpu.CompilerParams(dimension_semantics=("arbitrary",),
                                             vmem_limit_bytes=VMEM_LIMIT),
        name="nsa_compress",
    )(proj3, perm, pos, w1, w2)


def _cmp_topk_kernel(q_ref, kc_ref, vc_ref, ovt_ref, o_ref, nsel_ref, *, T, n_cmp, n_slc):
    i = pl.program_id(1)
    t0 = i * T
    lane = lax.broadcasted_iota(jnp.int32, (T, LANES), 1)
    tok = t0 + lax.broadcasted_iota(jnp.int32, (T, LANES), 0)
    ok = jnp.logical_and(lane * NSA_CMP_STRIDE + (NSA_CMP_LEN - 1) <= tok, lane < n_cmp)
    lo = lane < HEAD_DIM
    qm = _masked_heads(q_ref, lo)
    kc = kc_ref[...]
    vc = vc_ref[...]
    psum = jnp.zeros((T, LANES), F32)
    outs = []
    for h in range(N_HEADS):
        s = jnp.where(ok, _dot_nt(qm[h], kc), NEG)
        mx = jnp.max(s, axis=-1, keepdims=True)
        e = jnp.where(ok, jnp.exp2(s - mx), 0.0)
        den = jnp.sum(e, axis=-1, keepdims=True)
        p = e * (1.0 / jnp.maximum(den, 1e-30))
        psum = psum + p
        outs.append(_dot(p.astype(BF16), vc))
    for pair in range(2):
        o_ref[:, pair * LANES:(pair + 1) * LANES] = jnp.where(
            lo, outs[2 * pair], outs[2 * pair + 1]).astype(BF16)

    p_hi = psum.astype(BF16)
    p_lo = (psum - p_hi.astype(F32)).astype(BF16)
    ovt = ovt_ref[...]
    imp = _dot_nt(ovt, p_hi) + _dot_nt(ovt, p_lo)
    jj = lax.broadcasted_iota(jnp.int32, (n_slc, T), 0)
    bt = (t0 + lax.broadcasted_iota(jnp.int32, (n_slc, T), 1)) >> 6
    forced = jnp.logical_or(jj == 0, jnp.logical_or(jj == bt, jj == bt - 1))
    val = jnp.where(forced, BIG, jnp.where(jj > bt, -BIG, imp))
    sel_rows = []
    for j in range(n_slc):
        vj = val[j:j + 1, :]
        beats = jnp.logical_or(val > vj, jnp.logical_and(val == vj, jj < j))
        rank = jnp.sum(beats.astype(F32), axis=0, keepdims=True)
        sel_rows.append((rank >= NSA_N_SEL).astype(F32))
    sel_rows.append(jnp.zeros((LANES - n_slc, T), F32))
    nsel = jnp.concatenate(sel_rows, axis=0).astype(BF16)
    for blk in range(T // TB):
        nsel_ref[blk] = nsel[:, blk * TB:(blk + 1) * TB]


def _cmp_topk(proj3, kc2, vc2, B, S):
    T = CMP_TILE
    n_cmp = (S - NSA_CMP_LEN) // NSA_CMP_STRIDE + 1
    n_slc = S // NSA_SEL_LEN
    assert NSA_SEL_LEN == 64 and n_slc <= LANES and n_cmp <= LANES
    ci = np.arange(LANES)[None, :] * NSA_CMP_STRIDE
    sj = np.arange(n_slc)[:, None] * NSA_SEL_LEN
    ovt = ((ci < sj + NSA_SEL_LEN) & (ci + NSA_CMP_LEN > sj) & (np.arange(LANES)[None, :] < n_cmp))
    ovt = jnp.asarray(ovt.astype(np.float32), BF16)
    return pl.pallas_call(
        functools.partial(_cmp_topk_kernel, T=T, n_cmp=n_cmp, n_slc=n_slc),
        grid=(B, S // T),
        in_specs=[pl.BlockSpec((None, T, BLK), lambda b, i: (b, i, PB_CQ)),
                  pl.BlockSpec((None, LANES, LANES), lambda b, i: (b, 0, 0)),
                  pl.BlockSpec((None, LANES, LANES), lambda b, i: (b, 0, 0)),
                  pl.BlockSpec(ovt.shape, lambda b, i: (0, 0))],
        out_specs=[pl.BlockSpec((None, T, BLK), lambda b, i: (b, i, 0)),
                   pl.BlockSpec((None, T // TB, LANES, TB), lambda b, i: (b, i, 0, 0))],
        out_shape=[jax.ShapeDtypeStruct((B, S, BLK), BF16),
                   jax.ShapeDtypeStruct((B, S // TB, LANES, TB), BF16)],
        compiler_params=pltpu.CompilerParams(dimension_semantics=("arbitrary", "arbitrary"),
                                             vmem_limit_bytes=VMEM_LIMIT),
        name="nsa_cmp_topk",
    )(proj3, kc2, vc2, ovt)


def _stick_kernel(q_all_ref, k_ref, vt_ref, u_ref, o_ref, za_ref, zb_ref, w_ref, carry_ref,
                  acc_ref):
    T = TB
    krow = lax.broadcasted_iota(jnp.int32, (T, T), 0)
    qcol = lax.broadcasted_iota(jnp.int32, (T, T), 1)
    strict = krow < qcol
    u = u_ref[...]
    z_refs = (za_ref, zb_ref)

    def query_block(i, carry):
        qm = _masked_head_rows(q_all_ref.at[i])

        def stage(j, slot):
            rows = pl.ds(pl.multiple_of(j * T, T), T)
            for h in range(N_HEADS):
                z_refs[slot][h] = _dot(k_ref[rows, (h // 2) * LANES:(h // 2 + 1) * LANES],
                                       qm[h])

        def apply_weights(j):
            vt_all = vt_ref[j]
            for h in range(N_HEADS):
                acc_ref[h] = acc_ref[h] + _dot(vt_all[h * HEAD_DIM:(h + 1) * HEAD_DIM, :], w_ref[h])

        def absorb(j, slot, diag, staged_next, applied_prev):
            z_ref = z_refs[slot]
            softplus = []
            for h in range(N_HEADS):
                z = z_ref[h]
                sp = jnp.maximum(z, jnp.log2(1.0 + jnp.exp2(jnp.minimum(z, 64.0))))
                if diag:
                    sp = jnp.where(strict, sp, 0.0)
                softplus.append(sp.astype(BF16))
            if staged_next is not None:
                stage(staged_next, 1 - slot)
            if applied_prev is not None:
                apply_weights(applied_prev)
            sums = [_dot(u, sp) for sp in softplus]
            for h in range(N_HEADS):
                a = jnp.exp2(z_ref[h] - (sums[h] + carry_ref[h]))
                if diag:
                    a = jnp.where(strict, a, 0.0)
                w_ref[h] = a.astype(BF16)
                carry_ref[h] = carry_ref[h] + sums[h][0:1, :]

        carry_ref[...] = jnp.zeros(carry_ref.shape, F32)
        acc_ref[...] = jnp.zeros(acc_ref.shape, F32)
        stage(i, 0)
        absorb(i, 0, True, jnp.maximum(i - 1, 0), None)

        def body(t, c):
            j = i - 1 - 2 * t
            absorb(j, 1, False, j - 1, j + 1)
            absorb(j - 1, 0, False, jnp.maximum(j - 2, 0), j)
            return c

        lax.fori_loop(0, i // 2, body, 0)

        @pl.when(i % 2 == 1)
        def _():
            absorb(0, 1, False, None, 1)

        apply_weights(0)
        q_rows = pl.ds(pl.multiple_of(i * T, T), T)
        for pair in range(2):
            o_t = jnp.concatenate([acc_ref[2 * pair], acc_ref[2 * pair + 1]], axis=0)
            o_ref[q_rows, pair * LANES:(pair + 1) * LANES] = o_t.T.astype(BF16)
        return carry

    lax.fori_loop(0, q_all_ref.shape[0], query_block, 0)


def _stick_breaking(qdt, proj3, vdt, B, S):
    u = np.triu(np.ones((TB, TB), np.float32))
    return pl.pallas_call(
        _stick_kernel,
        grid=(B,),
        in_specs=[pl.BlockSpec((None,) + qdt.shape[1:], lambda b: (b, 0, 0, 0)),
                  pl.BlockSpec((None, S, BLK), lambda b: (b, 0, PB_DK)),
                  pl.BlockSpec((None,) + vdt.shape[1:], lambda b: (b, 0, 0, 0)),
                  pl.BlockSpec(u.shape, lambda b: (0, 0))],
        out_specs=pl.BlockSpec((None, S, BLK), lambda b: (b, 0, 0)),
        out_shape=jax.ShapeDtypeStruct((B, S, BLK), BF16),
        scratch_shapes=[pltpu.VMEM((N_HEADS, TB, TB), F32),
                        pltpu.VMEM((N_HEADS, TB, TB), F32),
                        pltpu.VMEM((N_HEADS, TB, TB), BF16),
                        pltpu.VMEM((N_HEADS, 1, TB), F32),
                        pltpu.VMEM((N_HEADS, HEAD_DIM, TB), F32)],
        compiler_params=pltpu.CompilerParams(dimension_semantics=("arbitrary",),
                                             vmem_limit_bytes=VMEM_LIMIT),
        name="stick_breaking",
    )(qdt, proj3, vdt, jnp.asarray(u, BF16))


def _outproj_kernel(x_ref, oa, ob, oc, os_, ow, od,
                    ga, gb, gl, gc, gd, expand_ref, w_ref, g_ref, out_ref):
    f = lambda r: r[...].astype(F32)
    gates = _dot(gl[...], expand_ref[...])
    o_c = (gates[:, 0:BLK] * f(oc) + gates[:, BLK:2 * BLK] * f(os_)
           + gates[:, 2 * BLK:3 * BLK] * f(ow))
    mixed = [f(oa) * f(ga), f(ob) * f(gb), o_c * f(gc), f(od) * f(gd)]
    y = _dot(mixed[0].astype(BF16), w_ref[0:BLK, :])
    for g in range(1, 4):
        y = y + _dot(mixed[g].astype(BF16), w_ref[g * BLK:(g + 1) * BLK, :])
    ms = jnp.mean(y * y, axis=-1, keepdims=True)
    out_ref[...] = x_ref[...] + (y * lax.rsqrt(ms + EPS)) * g_ref[...]


def _out_projection(x2, mixer_outs, proj, w_out, g_post):
    rows = x2.shape[0]
    tm = OUT_TILE
    blk = lambda c: pl.BlockSpec((tm, BLK), lambda i: (i, c))
    expand = np.zeros((LANES, N_BRANCH * BLK), np.float32)
    for h in range(N_HEADS):
        for r in range(N_BRANCH):
            expand[N_BRANCH * h + r, r * BLK + h * HEAD_DIM:r * BLK + (h + 1) * HEAD_DIM] = 1.0
    gate_specs = [blk(PB_AG), blk(PB_BG),
                  pl.BlockSpec((tm, LANES), lambda i: (i, 2 * PB_C1 + 1)), blk(PB_CG), blk(PB_DG)]
    return pl.pallas_call(
        _outproj_kernel,
        grid=(rows // tm,),
        in_specs=[pl.BlockSpec((tm, D_MODEL), lambda i: (i, 0))]
                 + [blk(0)] * len(mixer_outs) + gate_specs
                 + [pl.BlockSpec(expand.shape, lambda i: (0, 0)),
                    pl.BlockSpec((D_MODEL, D_MODEL), lambda i: (0, 0)),
                    pl.BlockSpec((1, D_MODEL), lambda i: (0, 0))],
        out_specs=pl.BlockSpec((tm, D_MODEL), lambda i: (i, 0)),
        out_shape=jax.ShapeDtypeStruct((rows, D_MODEL), F32),
        compiler_params=pltpu.CompilerParams(dimension_semantics=("arbitrary",),
                                             vmem_limit_bytes=VMEM_LIMIT),
        name="out_projection",
    )(x2, *mixer_outs, *([proj] * len(gate_specs)), jnp.asarray(expand, BF16),
      w_out.astype(BF16), g_post.reshape(1, D_MODEL))


def _layer(x2, tabs, B, S, layer, w_rows, w_t, w_out, g_pre, g_post, mla_g_q, mla_g_kv,
           mla_w_uq, mla_w_ukv, nsa_pos_k, nsa_pos_v, nsa_k_w1, nsa_k_w2, nsa_v_w1, nsa_v_w2):
    rows = B * S
    aqkv, aqkv4, aqkv16, proj, vct, vdt, qct, qdt = _in_projection(
        x2, g_pre, w_rows, w_t, layer, tabs, B, S)
    proj3 = proj.reshape(B, S, N_PBLK * BLK)

    o_a = _dilated(aqkv, aqkv4, aqkv16, B, S)

    qnt, qrt, kn, vbt = _mla_up(proj, mla_g_q, mla_g_kv, mla_w_uq, mla_w_ukv, tabs, B, S)
    o_b = _flash_t(qnt, kn.reshape(B, S, BLK), 0, BLK, vbt, B=B, S=S,
                   aux=("per_head", qrt, proj3, 2 * PB_BKV + 1), name="mla_attention")

    kc2, vc2 = _compress(proj3, nsa_pos_k, nsa_pos_v, nsa_k_w1, nsa_k_w2, nsa_v_w1, nsa_v_w2, B, S)
    o_cmp, nsel = _cmp_topk(proj3, kc2, vc2, B, S)
    penalty = np.zeros((S, LANES), np.float32)
    penalty[np.arange(S), np.arange(S) // NSA_SEL_LEN] = MASK_PENALTY
    o_slc = _flash_t(qct, proj3, 2 * PB_C0 + 1, LANES, vct, B=B, S=S, vrow0=0,
                     aux=("shared", nsel, jnp.asarray(penalty, BF16), 0), name="nsa_selected")
    o_win = _flash_t(qct, proj3, 2 * PB_C1, LANES, vct, B=B, S=S, vrow0=HEAD_DIM,
                     window=NSA_WINDOW - 1, name="nsa_window")

    o_d = _stick_breaking(qdt, proj3, vdt, B, S)

    flat = lambda t: t.reshape(rows, BLK)
    outs = [flat(o) for o in (o_a, o_b, o_cmp, o_slc, o_win, o_d)]
    return _out_projection(x2, outs, proj, w_out, g_post)


def kernel(x, positions, w_in, w_out, g_pre, g_post, mla_g_q, mla_g_kv, mla_w_uq, mla_w_ukv,
           nsa_pos_k, nsa_pos_v, nsa_k_w1, nsa_k_w2, nsa_v_w1, nsa_v_w2):
    B, S, D = x.shape
    assert D == D_MODEL and S % 1024 == 0
    x2 = x.reshape(B * S, D)
    tabs = _rope_tables(positions)
    w_rows, w_t = _pack_w_in(w_in)
    for l in range(w_in.shape[0]):
        x2 = _layer(x2, tabs, B, S, l, w_rows, w_t, w_out[l], g_pre[l], g_post[l],
                    mla_g_q[l], mla_g_kv[l], mla_w_uq[l], mla_w_ukv[l],
                    nsa_pos_k[l], nsa_pos_v[l], nsa_k_w1[l], nsa_k_w2[l],
                    nsa_v_w1[l], nsa_v_w2[l])
    return x2.reshape(B, S, D)
```

```python
import functools
import math

import numpy as np
import jax
import jax.numpy as jnp
from jax import lax
from jax.experimental import pallas as pl
from jax.experimental.pallas import tpu as pltpu

D_MODEL = 1024
HEAD_DIM = 64
N_HEADS = 4
ROPE_THETA = 10000.0
EPS = 1e-6
NEG = -1e30
LOG2E = math.log2(math.e)
MASK_PENALTY = -(2.0 ** 100)
BIG = 1e9

MLA_Q_RANK = 256
MLA_KV_RANK = 128
MLA_NOPE = 64
MLA_ROPE = 32
MLA_V = 64

NSA_CMP_LEN = 32
NSA_CMP_STRIDE = 16
NSA_CMP_HID = 256
NSA_SEL_LEN = 64
NSA_N_SEL = 16
NSA_WINDOW = 512

LANES = 128
BLK = 256
IN_TILE = 1024
OUT_TILE = 1024
TILE = 512
MLA_UP_TILE = 2048
ROPE_TILE = 2048
CMP_TILE = 2048
PACK_COLS = 256
CM_TILE = 256
DILATIONS = (1, 4, 16)
DIL_BLOCK = 128
LSE_PARTS = 2
TB = 256
ONES_ROWS = 16
VMEM_LIMIT = 48 * 1024 * 1024

F32 = jnp.float32
BF16 = jnp.bfloat16

(PB_AG, PB_BCQ, PB_BKV, PB_BG, PB_CQ, PB_C0, PB_C1, PB_CG, PB_DQ, PB_DK, PB_DG) = range(11)
N_PBLK = 11
N_BRANCH = 3
ROPE_NONE, ROPE_64, ROPE_32, ROPE_64_LO, ACT_SILU, ACT_SIGMOID = range(6)


def _dot_nt(a, b):
    return lax.dot_general(a, b, (((1,), (1,)), ((), ())), preferred_element_type=F32)


def _dot(a, b):
    return jnp.dot(a, b, preferred_element_type=F32)


def _rope_half(y, cos, sin_signed, group):
    half = group // 2
    lane = lax.broadcasted_iota(jnp.int32, y.shape, 1)
    first = (lane & (group - 1)) < half
    partner = jnp.where(first, pltpu.roll(y, LANES - half, 1), pltpu.roll(y, half, 1))
    return y * cos + partner * sin_signed


def _rope_table_kernel(pos_ref, inv_ref, cos_ref, sin_ref):
    ang = inv_ref[...] * pos_ref[...].astype(F32)
    cos_ref[...] = jnp.cos(ang)
    sin_ref[...] = jnp.sin(ang)


def _rope_tables(positions):
    rows = positions.size
    tm = ROPE_TILE
    inv =np.concatenate([ROPE_THETA ** (-np.arange(n, dtype=np.float64) / n)
                          for n in (HEAD_DIM // 2, MLA_ROPE // 2)]).astype(np.float32)[:, None]
    out = jax.ShapeDtypeStruct((inv.shape[0], rows), F32)
    tab_spec = pl.BlockSpec((inv.shape[0], tm), lambda i: (0, i))
    return pl.pallas_call(
        _rope_table_kernel,
        grid=(rows // tm,),
        in_specs=[pl.BlockSpec((1, tm), lambda i: (0, i)),
                  pl.BlockSpec(inv.shape, lambda i: (0, 0))],
        out_specs=[tab_spec] * 2,
        out_shape=[out] * 2,
        name="rope_tables",
    )(positions.reshape(1, rows), jnp.asarray(inv))


def _token_major_rope(cos_ref, sin_ref, group):
    n = group // 2
    lo = 0 if group == HEAD_DIM else HEAD_DIM // 2
    c = cos_ref[lo:lo + n, :]
    s = sin_ref[lo:lo + n, :]
    reps = LANES // n
    cos = jnp.concatenate([c] * reps, axis=0).T
    sin = jnp.concatenate([-s if k % 2 == 0 else s for k in range(reps)], axis=0).T
    return cos, sin


def _inproj_kernel(x_ref, g_ref, w_ref, wt_ref, perm_ref, cos_ref, sin_ref,
                   aqkv_ref, aqkv4_ref, aqkv16_ref, proj_ref, vct_ref, vdt_ref, qct_ref, qdt_ref,
                   *, plan_a, plan_p):
    x = x_ref[...]
    tm = x.shape[0]
    ms = jnp.mean(x * x, axis=-1, keepdims=True)
    h = ((x * lax.rsqrt(ms + EPS)) * g_ref[...]).astype(BF16)
    lane = lax.broadcasted_iota(jnp.int32, (tm, LANES), 1)
    c64, s64 = _token_major_rope(cos_ref, sin_ref, HEAD_DIM)
    c32, s32 = _token_major_rope(cos_ref, sin_ref, MLA_ROPE)

    def epilogue(y, ops, scale):
        halves = []
        for hf, op in enumerate(ops):
            yh = y[:, hf * LANES:(hf + 1) * LANES]
            if op == ROPE_64:
                yh = _rope_half(yh, c64, s64, 64)
            elif op == ROPE_32:
                yh = _rope_half(yh, c32, s32, 32)
            elif op == ROPE_64_LO:
                yh = jnp.where(lane < 64, _rope_half(yh, c64, s64, 64), yh)
            elif op == ACT_SILU:
                yh = yh * (1.0 / (1.0 + jnp.exp(-yh)))
            elif op == ACT_SIGMOID:
                yh = 1.0 / (1.0 + jnp.exp(-yh))
            if scale != 1.0:
                yh = yh * scale
            halves.append(yh)
        return jnp.concatenate(halves, axis=1)

    feature_major = {PB_CQ: qct_ref, PB_DQ: qdt_ref}
    col = 0
    for out_ref, plan in ((aqkv_ref, plan_a), (proj_ref, plan_p)):
        for b, (ops, scale) in enumerate(plan):
            y = epilogue(_dot(h, w_ref[:, col:col + BLK]), ops, scale)
            out_ref[:, b * BLK:(b + 1) * BLK] = y.astype(BF16)
            if out_ref is proj_ref and b in feature_major:
                y_t = y.T
                for blk in range(tm // TB):
                    feature_major[b][blk] = y_t[:, blk * TB:(blk + 1) * TB].astype(BF16)
            col += BLK

    for t in range(tm // CM_TILE):
        tile = slice(CM_TILE * t, CM_TILE * (t + 1))
        a_rows = aqkv_ref[tile, :]
        aqkv4_ref[tile, :] = _dot(perm_ref[0], a_rows).astype(BF16)
        aqkv16_ref[tile, :] = _dot(perm_ref[1], a_rows).astype(BF16)

    vt = _dot_nt(wt_ref[...], h)
    for blk in range(tm // TB):
        cols = slice(blk * TB, (blk + 1) * TB)
        vct_ref[blk] = vt[0:LANES, cols].astype(BF16)
        vdt_ref[blk] = vt[LANES:LANES + BLK, cols].astype(BF16)


_PLAN_A = (((ROPE_64, ROPE_64), LOG2E * HEAD_DIM ** -0.5),
           ((ROPE_64, ROPE_64), 1.0),
           ((ROPE_NONE, ROPE_NONE), 1.0))
_PLAN_P = (
    ((ACT_SILU, ACT_SILU), 1.0),
    ((ROPE_NONE, ROPE_NONE), 1.0),
    ((ROPE_NONE, ROPE_32), 1.0),
    ((ACT_SILU, ACT_SILU), 1.0),
    ((ROPE_64, ROPE_64), LOG2E * HEAD_DIM ** -0.5),
    ((ROPE_64_LO, ROPE_64), 1.0),
    ((ROPE_64, ACT_SIGMOID), 1.0),
    ((ACT_SILU, ACT_SILU), 1.0),
    ((ROPE_NONE, ROPE_NONE), LOG2E * HEAD_DIM ** -0.5),
    ((ROPE_NONE, ROPE_NONE), 1.0),
    ((ACT_SILU, ACT_SILU), 1.0),
)
assert len(_PLAN_P) == N_PBLK


_W_IN_SEGMENTS = (("aq", 256), ("ak", 256), ("av", 256), ("ag", 256),
                  ("bcq", 256), ("bckv", 128), ("bkr", 32), ("bg", 256),
                  ("cq", 256), ("ckc", 64), ("cvc", 64), ("cks", 64), ("cvs", 64),
                  ("ckw", 64), ("cvw", 64), ("cgl", 12), ("cg", 256),
                  ("dq", 256), ("dk", 256), ("dv", 256), ("dg", 256))
W_ROWS_COLS = (3 + N_PBLK) * BLK
W_T_ROWS = LANES + BLK


def _pack_w_kernel(wt_ref, rows_ref, t_ref):
    wt = wt_ref[...].astype(F32)
    o = 0
    seg = {}
    for name, width in _W_IN_SEGMENTS:
        seg[name] = wt[o:o + width, :]
        o += width
    gl_pad = jnp.zeros((LANES - N_HEADS * N_BRANCH, wt.shape[1]), wt.dtype)
    rows = [seg["aq"], seg["ak"], seg["av"],
            seg["ag"], seg["bcq"],
            seg["bckv"], seg["bkr"], seg["bkr"], seg["bkr"], seg["bkr"],
            seg["bg"], seg["cq"],
            seg["ckc"], seg["cvc"], seg["cks"], seg["cks"],
            seg["ckw"], seg["ckw"], seg["cgl"], gl_pad,
            seg["cg"], seg["dq"], seg["dk"], seg["dg"]]
    rows_ref[...] = jnp.concatenate(rows, axis=0).T.astype(BF16)
    t_ref[...] = jnp.concatenate([seg["cvs"], seg["cvw"], seg["dv"]], axis=0).astype(BF16)


def _pack_w_in(w_in):
    depth, d_model, d_in = w_in.shape
    assert d_model == D_MODEL and d_in == sum(width for _, width in _W_IN_SEGMENTS)
    tc = PACK_COLS
    return pl.pallas_call(
        _pack_w_kernel,
        grid=(depth, D_MODEL // tc),
        in_specs=[pl.BlockSpec((None, d_in, tc), lambda l, i: (l, 0, i))],
        out_specs=[pl.BlockSpec((None, tc, W_ROWS_COLS), lambda l, i: (l, i, 0)),
                   pl.BlockSpec((None, W_T_ROWS, tc), lambda l, i: (l, 0, i))],
        out_shape=[jax.ShapeDtypeStruct((depth, D_MODEL, W_ROWS_COLS), BF16),
                   jax.ShapeDtypeStruct((depth, W_T_ROWS, D_MODEL), BF16)],
        compiler_params=pltpu.CompilerParams(dimension_semantics=("arbitrary", "arbitrary"),
                                             vmem_limit_bytes=VMEM_LIMIT),
        name="pack_w_in",
    )(jnp.transpose(w_in, (0, 2, 1)).astype(BF16))


def _class_major_perm(dil, tile):
    n = tile // dil
    p = np.zeros((tile, tile), np.float32)
    pos = np.arange(tile)
    p[(pos % dil) * n + pos // dil, pos] = 1.0
    return p


def _in_projection(x2, g_pre, w_rows, w_t, layer, tabs, B, S):
    rows = x2.shape[0]
    tm = IN_TILE
    per_b = S // tm
    nkb = tm // TB
    perm = jnp.asarray(np.stack([_class_major_perm(d, CM_TILE) for d in DILATIONS[1:]]), BF16)
    tab_spec = pl.BlockSpec((tabs[0].shape[0], tm), lambda i: (0, i))
    vt_spec = lambda r: pl.BlockSpec((None, nkb, r, TB), lambda i: (i // per_b, i % per_b, 0, 0))
    a_spec = pl.BlockSpec((tm, 3 * BLK), lambda i: (i, 0))
    a_shape = jax.ShapeDtypeStruct((rows, 3 * BLK), BF16)
    return pl.pallas_call(
        functools.partial(_inproj_kernel, plan_a=_PLAN_A, plan_p=_PLAN_P),
        grid=(rows // tm,),
        in_specs=[pl.BlockSpec((tm, D_MODEL), lambda i: (i, 0)),
                  pl.BlockSpec((1, D_MODEL), lambda i: (0, 0)),
                  pl.BlockSpec((None,) + w_rows.shape[1:], lambda i: (layer, 0, 0)),
                  pl.BlockSpec((None,) + w_t.shape[1:], lambda i: (layer, 0, 0)),
                  pl.BlockSpec(perm.shape, lambda i: (0, 0, 0))] + [tab_spec] * 2,
        out_specs=[a_spec, a_spec, a_spec,
                   pl.BlockSpec((tm, N_PBLK * BLK), lambda i: (i, 0)),
                   vt_spec(LANES), vt_spec(BLK), vt_spec(BLK), vt_spec(BLK)],
        out_shape=[a_shape, a_shape, a_shape,
                   jax.ShapeDtypeStruct((rows, N_PBLK * BLK), BF16),
                   jax.ShapeDtypeStruct((B, S // TB, LANES, TB), BF16)]
                  + [jax.ShapeDtypeStruct((B, S // TB, BLK, TB), BF16)] * 3,
        compiler_params=pltpu.CompilerParams(dimension_semantics=("arbitrary",),
                                             vmem_limit_bytes=VMEM_LIMIT),
        name="in_projection",
    )(x2, g_pre.reshape(1, D_MODEL), w_rows, w_t, perm, *tabs)


def _head_masks(shape):
    lane = lax.broadcasted_iota(jnp.int32, shape, 1)
    return lane < HEAD_DIM


def _masked_heads(q_ref, lo_mask):
    out = []
    for h in range(N_HEADS):
        pair, e = divmod(h, 2)
        qp = q_ref[:, pair * LANES:(pair + 1) * LANES]
        keep = lo_mask if e == 0 else jnp.logical_not(lo_mask)
        out.append(jnp.where(keep, qp, jnp.zeros_like(qp)))
    return out


def _dilated_kernel(an_ref, a4_ref, a16_ref, unperm_ref, expand_ref, o_ref, res_ref, s_ref):
    T = DIL_BLOCK
    S = an_ref.shape[0]
    n_tiles = S // CM_TILE
    lo = _head_masks((T, LANES))
    lane = lax.broadcasted_iota(jnp.int32, (T, LANES), 1)
    dist2 = (lax.broadcasted_iota(jnp.int32, (T, 2 * T), 0)
             - lax.broadcasted_iota(jnp.int32, (T, 2 * T), 1))
    causal1 = (lax.broadcasted_iota(jnp.int32, (T, T), 1)
               <= lax.broadcasted_iota(jnp.int32, (T, T), 0))

    def scores(q, k):
        qm = _masked_heads(q, lo)
        out = []
        for pair in range(N_HEADS // 2):
            both = _dot_nt(jnp.concatenate([qm[2 * pair], qm[2 * pair + 1]], axis=0),
                           k[:, pair * LANES:(pair + 1) * LANES])
            out += [both[:T], both[T:]]
        return tuple(out)

    def finish(n_keys, v, mask, stage_next):
        probs = []
        for h in range(N_HEADS):
            s = jnp.where(mask, s_ref[h, :, 0:n_keys], NEG)
            m = jnp.max(s, axis=-1, keepdims=True)
            p = jnp.exp2(s - m)
            probs.append((m, jnp.sum(p, axis=-1, keepdims=True), p.astype(BF16)))
        stage_next()
        pv = []
        for pair in range(N_HEADS // 2):
            both = _dot(jnp.concatenate([probs[2 * pair][2], probs[2 * pair + 1][2]], axis=0),
                        v[:, pair * LANES:(pair + 1) * LANES])
            pv += [both[:T], both[T:]]
        outs = []
        top = jnp.zeros((T, LANES), F32)
        den = jnp.ones((T, LANES), F32)
        for h in range(N_HEADS):
            m, l, _ = probs[h]
            outs.append(pv[h] * (1.0 / l))
            mine = (lane & (N_HEADS - 1)) == h
            top = jnp.where(mine, m, top)
            den = jnp.where(mine, l, den)
        lse = top + jnp.log2(den)
        hi = lse.astype(BF16).astype(F32)
        parts = jnp.where(lane < N_HEADS, hi,
                          jnp.where(lane < LSE_PARTS * N_HEADS, lse - hi, 0.0))
        return jnp.concatenate(
            [jnp.where(lo, outs[0], outs[1]), jnp.where(lo, outs[2], outs[3]), parts],
            axis=1).astype(BF16)

    def band_mask(delta):
        d = dist2 + delta
        return jnp.logical_and(d >= 0, d <= DIL_BLOCK)

    def run(n_units, n_keys, load_qk, load_v, store, mask_of):
        def stage(u):
            for h, s in enumerate(scores(*load_qk(u))):
                s_ref[h, :, 0:n_keys] = s

        def body(u, carry):
            nxt = jnp.minimum(u + 1, n_units - 1)
            store(u, finish(n_keys, load_v(u), mask_of(u), lambda: stage(nxt)))
            return carry

        stage(0)
        lax.fori_loop(0, n_units, body, 0)

    def rows(start, size):
        return pl.ds(pl.multiple_of(start, 16), size)

    def k0_1(u):
        return T * jnp.maximum(u - 1, 0)

    def store1(u, r):
        res_ref[0, rows(T * u, T), :] = r

    run(S // T, 2 * T,
        lambda u: (an_ref[rows(T * u, T), 0:BLK], an_ref[rows(k0_1(u), 2 * T), BLK:2 * BLK]),
        lambda u: an_ref[rows(k0_1(u), 2 * T), 2 * BLK:3 * BLK],
        store1,
        lambda u: band_mask(T * u - k0_1(u)))

    per4 = CM_TILE // 4
    nq4 = S // 4 // T

    def pieces4(u, blk):
        first_tile = blk * (T // per4)
        return [rows(CM_TILE * (first_tile + k) + per4 * (u // nq4), per4)
                for k in range(T // per4)]

    def kb0_4(u):
        return jnp.maximum(u % nq4 - 1, 0)

    def load4(u, blks, cols):
        return jnp.concatenate([a4_ref[p, cols] for blk in blks for p in pieces4(u, blk)], axis=0)

    def store4(u, r):
        for k, p in enumerate(pieces4(u, u % nq4)):
            res_ref[1, p, :] = r[per4 * k:per4 * (k + 1), :]

    run(4 * nq4, 2 * T,
        lambda u: (load4(u, [u % nq4], slice(0, BLK)),
                   load4(u, [kb0_4(u), kb0_4(u) + 1], slice(BLK, 2 * BLK))),
        lambda u: load4(u, [kb0_4(u), kb0_4(u) + 1], slice(2 * BLK, 3 * BLK)),
        store4,
        lambda u: band_mask(T * (u % nq4 - kb0_4(u))))

    per16 = CM_TILE // 16

    def load16(c, cols):
        return jnp.concatenate([a16_ref[rows(CM_TILE * t + per16 * c, per16), cols]
                                for t in range(n_tiles)], axis=0)

    def store16(c, r):
        for t in range(n_tiles):
            res_ref[2, rows(CM_TILE * t + per16 * c, per16), :] = r[per16 * t:per16 * (t + 1), :]

    run(16, T,
        lambda c: (load16(c, slice(0, BLK)), load16(c, slice(BLK, 2 * BLK))),
        lambda c: load16(c, slice(2 * BLK, 3 * BLK)),
        store16,
        lambda c: causal1)

    expand = expand_ref[...]
    for t in range(n_tiles):
        tile = slice(CM_TILE * t, CM_TILE * (t + 1))
        nat = [res_ref[0, tile, :].astype(F32),
               _dot(unperm_ref[0], res_ref[1, tile, :]),
               _dot(unperm_ref[1], res_ref[2, tile, :])]
        lse = [_dot(r[:, BLK:].astype(BF16), expand) for r in nat]
        top = jnp.maximum(lse[0], jnp.maximum(lse[1], lse[2]))
        w = [jnp.exp2(l - top) for l in lse]
        mixed = (w[0] * nat[0][:, :BLK] + w[1] * nat[1][:, :BLK] + w[2] * nat[2][:, :BLK])
        o_ref[tile, :] = (mixed / (w[0] + w[1] + w[2])).astype(BF16)


def _dilated(aqkv, aqkv4, aqkv16, B, S):
    assert S == 16 * DIL_BLOCK and S % CM_TILE == 0 and DIL_BLOCK % (CM_TILE // 4) == 0
    r3 = lambda t: t.reshape(B, S, 3 * BLK)
    unperm = jnp.asarray(np.stack([_class_major_perm(d, CM_TILE).T for d in DILATIONS[1:]]), BF16)
    expand = np.zeros((LANES, BLK), np.float32)
    for h in range(N_HEADS):
        for part in range(LSE_PARTS):
            expand[N_HEADS * part + h, h * HEAD_DIM:(h + 1) * HEAD_DIM] = 1.0
    a_spec = pl.BlockSpec((None, S, 3 * BLK), lambda b: (b, 0, 0))
    return pl.pallas_call(
        _dilated_kernel,
        grid=(B,),
        in_specs=[a_spec, a_spec, a_spec,
                  pl.BlockSpec(unperm.shape, lambda b: (0, 0, 0)),
                  pl.BlockSpec(expand.shape, lambda b: (0, 0))],
        out_specs=pl.BlockSpec((None, S, BLK), lambda b: (b, 0, 0)),
        out_shape=jax.ShapeDtypeStruct((B, S, BLK), BF16),
        scratch_shapes=[pltpu.VMEM((3, S, BLK + LANES), BF16),
                        pltpu.VMEM((N_HEADS, DIL_BLOCK, 2 * DIL_BLOCK), F32)],
        compiler_params=pltpu.CompilerParams(dimension_semantics=("arbitrary",),
                                             vmem_limit_bytes=VMEM_LIMIT),
        name="dilated_mixture",
    )(r3(aqkv), r3(aqkv4), r3(aqkv16), unperm, jnp.asarray(expand, BF16))


def _masked_head_rows(qt_ref):
    out = []
    zeros = jnp.zeros((HEAD_DIM, qt_ref.shape[-1]), BF16)
    for h in range(N_HEADS):
        own = qt_ref[h * HEAD_DIM:(h + 1) * HEAD_DIM, :]
        out.append(jnp.concatenate([own, zeros] if h % 2 == 0 else [zeros, own], axis=0))
    return out


def _flash_t_kernel(*refs, window, kshared, vrow0, aux):
    it = iter(refs)
    q_all_ref = next(it)
    qx_all_ref = next(it) if aux else None
    k_ref = next(it)
    kx_ref = next(it) if aux else None
    vt_ref = next(it)
    o_ref = next(it)
    s_ref, m_ref, acc_ref = next(it), next(it), next(it)
    T = TB

    krow = lax.broadcasted_iota(jnp.int32, (T, T), 0)
    qcol = lax.broadcasted_iota(jnp.int32, (T, T), 1)
    ones = jnp.ones((ONES_ROWS, T), BF16)

    n_q = q_all_ref.shape[0]
    cross_stage = window is None
    if window is not None:
        o_full = (window - T + 1) // T
        o_none = (window + T - 1) // T + 1
        assert o_none - o_full == 2

    def query_weights(i):
        qcat = _masked_head_rows(q_all_ref.at[i])
        if aux == "per_head":
            group = LANES // N_HEADS
            qcat = [jnp.concatenate(
                [qcat[h]] + [qx_all_ref[i, g * group:(g + 1) * group, :] if g == h
                             else jnp.zeros((group, T), BF16) for g in range(N_HEADS)], axis=0)
                for h in range(N_HEADS)]
        elif aux == "shared":
            qx = qx_all_ref[i]
            qcat = [jnp.concatenate([qcat[h], qx], axis=0) for h in range(N_HEADS)]
        return qcat

    def first_full_block(i):
        return 0 if window is None else jnp.maximum(i - o_full, 0)

    def scores(j, qcat):
        rows = pl.ds(pl.multiple_of(j * T, T), T)
        out = []
        for h in range(N_HEADS):
            pair = h // 2
            cols = slice(0, LANES) if kshared else slice(pair * LANES, (pair + 1) * LANES)
            kb = k_ref[rows, cols]
            if aux:
                kb = jnp.concatenate([kb, kx_ref[rows, :]], axis=1)
            out.append(_dot(kb, qcat[h]))
        return out

    def stage(j, qcat):
        for h, s in enumerate(scores(j, qcat)):
            s_ref[h] = s

    def band_block(i):
        o = o_full + 1
        j = i - o
        limit = jnp.where(j >= 0, window - o * T, -2 * T)
        return jnp.maximum(j, 0), limit

    def stage_query_block(i):
        stage(first_full_block(i), query_weights(i))

    def query_block(i, carry):
        qcat = query_weights(i)

        def absorb(j, mask, staged_next=None, s_all=None):
            vt_all = vt_ref[j]
            probs = []
            for h in range(N_HEADS):
                s = s_ref[h] if s_all is None else s_all[h]
                if mask is not None:
                    s = jnp.where(mask, s, NEG)
                m = m_ref[h]
                m_new = jnp.maximum(m, jnp.max(s, axis=0, keepdims=True))
                alpha = jnp.exp2(m - m_new)
                p = jnp.exp2(s - m_new)
                probs.append((m_new, alpha, p.astype(BF16)))
            if staged_next is not None:
                staged_next()
            for h in range(N_HEADS):
                m_new, alpha, p = probs[h]
                r0 = vrow0 if kshared else h * HEAD_DIM
                lhs = jnp.concatenate([vt_all[r0:r0 + HEAD_DIM, :], ones], axis=0)
                acc_ref[h] = alpha * acc_ref[h] + _dot(lhs, p)
                m_ref[h] = m_new

        m_ref[...] = jnp.full(m_ref.shape, NEG, F32)
        acc_ref[...] = jnp.zeros(acc_ref.shape, F32)
        first = first_full_block(i)
        if window is not None:
            jc, limit = band_block(i)
            s_band = scores(jc, qcat)
            stage(first, qcat)
            absorb(jc, (qcol - krow) <= limit, s_all=s_band)

        def body(j, c):
            absorb(j, None, staged_next=lambda: stage(j + 1, qcat))
            return c

        lax.fori_loop(first, i, body, 0)
        if cross_stage:
            nxt = jnp.minimum(i + 1, n_q - 1)
            absorb(i, krow <= qcol, staged_next=lambda: stage_query_block(nxt))
        else:
            absorb(i, krow <= qcol)

        q_rows = pl.ds(pl.multiple_of(i * T, T), T)
        for pair in range(2):
            outs = []
            for e in range(2):
                acc = acc_ref[2 * pair + e]
                outs.append(acc[0:HEAD_DIM, :] * (1.0 / acc[HEAD_DIM:HEAD_DIM + 1, :]))
            o_ref[q_rows, pair * LANES:(pair + 1) * LANES] = (
                jnp.concatenate(outs, axis=0).T.astype(BF16))
        return carry

    if cross_stage:
        stage_query_block(0)
    lax.fori_loop(0, n_q, query_block, 0)


def _flash_t(qt, k, kcol, kw, vt, *, B, S, window=None, vrow0=0, aux=None, name):
    kshared = kw == LANES
    whole = lambda a: pl.BlockSpec((None,) + a.shape[1:], lambda b: (b,) + (0,) * (a.ndim - 1))
    in_specs = [whole(qt)]
    args = [qt]
    if aux is not None:
        kind, qx, kx, kxcol = aux
        in_specs.append(whole(qx))
        args.append(qx)
    in_specs.append(pl.BlockSpec((None, S, kw), lambda b: (b, 0, kcol)))
    args.append(k)
    if aux is not None:
        if kx.ndim == 3:
            in_specs.append(pl.BlockSpec((None, S, LANES), lambda b: (b, 0, kxcol)))
        else:
            in_specs.append(pl.BlockSpec((S, LANES), lambda b: (0, kxcol)))
        args.append(kx)
    in_specs.append(whole(vt))
    args.append(vt)
    return pl.pallas_call(
        functools.partial(_flash_t_kernel, window=window, kshared=kshared, vrow0=vrow0,
                          aux=None if aux is None else aux[0]),
        grid=(B,),
        in_specs=in_specs,
        out_specs=pl.BlockSpec((None, S, BLK), lambda b: (b, 0, 0)),
        out_shape=jax.ShapeDtypeStruct((B, S, BLK), BF16),
        scratch_shapes=[pltpu.VMEM((N_HEADS, TB, TB), F32),
                        pltpu.VMEM((N_HEADS, 1, TB), F32),
                        pltpu.VMEM((N_HEADS, HEAD_DIM + ONES_ROWS, TB), F32)],
        compiler_params=pltpu.CompilerParams(dimension_semantics=("arbitrary",),
                                             vmem_limit_bytes=VMEM_LIMIT),
        name=name,
    )(*args)


def _mla_up_kernel(cq_ref, ckv_ref, gq_ref, gkv_ref, wq_ref, wk_ref, wvt_ref, cos_ref, sin_ref,
                   qn_ref, qr_ref, kn_ref, vt_ref):
    def norm(t, g):
        t = t.astype(F32)
        ms = jnp.mean(t * t, axis=-1, keepdims=True)
        return ((t * lax.rsqrt(ms + EPS)) * g).astype(BF16)

    scale = LOG2E * (MLA_NOPE + MLA_ROPE) ** -0.5
    q_t = _dot_nt(wq_ref[...], norm(cq_ref[...], gq_ref[...])) * scale
    qn_t = q_t[:BLK, :]
    half = MLA_ROPE // 2
    cos = cos_ref[HEAD_DIM // 2:HEAD_DIM // 2 + half, :]
    sin = sin_ref[HEAD_DIM // 2:HEAD_DIM // 2 + half, :]
    rot = []
    for h in range(N_HEADS):
        x1 = q_t[BLK + h * MLA_ROPE:BLK + h * MLA_ROPE + half, :]
        x2 = q_t[BLK + h * MLA_ROPE + half:BLK + (h + 1) * MLA_ROPE, :]
        rot += [x1 * cos - x2 * sin, x1 * sin + x2 * cos]
    qr_t = jnp.concatenate(rot, axis=0)
    ckv = norm(ckv_ref[...], gkv_ref[...])
    kn_ref[...] = _dot(ckv, wk_ref[...]).astype(BF16)
    vt = _dot_nt(wvt_ref[...], ckv)
    for blk in range(vt.shape[1] // TB):
        cols = slice(blk * TB, (blk + 1) * TB)
        qn_ref[blk] = qn_t[:, cols].astype(BF16)
        qr_ref[blk] = qr_t[:, cols].astype(BF16)
        vt_ref[blk] = vt[:, cols].astype(BF16)


def _mla_up(proj, g_q, g_kv, w_uq, w_ukv, tabs, B, S):
    rows = proj.shape[0]
    tm = MLA_UP_TILE
    per_b = S // tm
    wq = w_uq.reshape(MLA_Q_RANK, N_HEADS, MLA_NOPE + MLA_ROPE)
    wq = jnp.concatenate([wq[:, :, :MLA_NOPE].reshape(MLA_Q_RANK, -1),
                          wq[:, :, MLA_NOPE:].reshape(MLA_Q_RANK, -1)], axis=1).T.astype(BF16)
    wkv = w_ukv.reshape(MLA_KV_RANK, N_HEADS, MLA_NOPE + MLA_V)
    wk = wkv[:, :, :MLA_NOPE].reshape(MLA_KV_RANK, -1).astype(BF16)
    wvt = wkv[:, :, MLA_NOPE:].reshape(MLA_KV_RANK, -1).T.astype(BF16)
    row_spec = lambda w, c: pl.BlockSpec((tm, w), lambda i: (i, c))
    full = lambda a: pl.BlockSpec(a.shape, lambda i: (0, 0))
    gq = g_q.reshape(1, -1)
    gkv = g_kv.reshape(1, -1)
    fm_shape = lambda r: jax.ShapeDtypeStruct((B, S // TB, r, TB), BF16)
    fm_spec = lambda r: pl.BlockSpec((None, tm // TB, r, TB),
                                     lambda i: (i // per_b, i % per_b, 0, 0))
    outs = [fm_shape(BLK), fm_shape(LANES), jax.ShapeDtypeStruct((rows, BLK), BF16), fm_shape(BLK)]
    return pl.pallas_call(
        _mla_up_kernel,
        grid=(rows // tm,),
        in_specs=[row_spec(BLK, PB_BCQ), row_spec(LANES, 2 * PB_BKV), full(gq), full(gkv),
                  full(wq), full(wk), full(wvt)]
                 + [pl.BlockSpec((tabs[0].shape[0], tm), lambda i: (0, i))] * 2,
        out_specs=[fm_spec(BLK), fm_spec(LANES), row_spec(BLK, 0), fm_spec(BLK)],
        out_shape=outs,
        compiler_params=pltpu.CompilerParams(dimension_semantics=("arbitrary",),
                                             vmem_limit_bytes=VMEM_LIMIT),
        name="mla_up",
    )(proj, proj, gq, gkv, wq, wk, wvt, *tabs)


def _compress_kernel(x_ref, perm_ref, pos_ref, w1_ref, w2_ref, kc_ref, vc_ref):
    n_tiles = x_ref.shape[0] // TILE
    per = TILE // NSA_CMP_STRIDE
    slabs = []
    for t in range(n_tiles):
        cm = _dot(perm_ref[...], x_ref[TILE * t:TILE * (t + 1), :])
        slabs.append(jnp.concatenate([cm[per * c:per * (c + 1), :]
                                      for c in range(NSA_CMP_STRIDE)], axis=1))
    x = jnp.concatenate(slabs, axis=0)
    half = x.shape[1]
    top = (x + pos_ref[0:1, :]).astype(BF16)
    bot = (x + pos_ref[1:2, :]).astype(BF16)
    y_top = _dot(top, w1_ref[0:half, :])
    y_bot = _dot(bot, w1_ref[half:2 * half, :])
    pre = y_top + pltpu.roll(y_bot, y_bot.shape[0] - 1, 0)
    hid = (pre * (1.0 / (1.0 + jnp.exp(-pre)))).astype(BF16)
    kc_ref[...] = _dot(hid[:, :NSA_CMP_HID], w2_ref[0]).astype(BF16)
    vc_ref[...] = _dot(hid[:, NSA_CMP_HID:], w2_ref[1]).astype(BF16)


def _compress(proj3, pos_k, pos_v, kw1, kw2, vw1, vw2, B, S):
    assert S % TILE == 0 and NSA_CMP_LEN == 2 * NSA_CMP_STRIDE
    n16 = S // NSA_CMP_STRIDE
    perm = jnp.asarray(_class_major_perm(NSA_CMP_STRIDE, TILE), BF16)
    pos = jnp.concatenate([pos_k, pos_v], axis=1).reshape(2, NSA_CMP_STRIDE * LANES)
    w1 = jnp.zeros((NSA_CMP_LEN, LANES, 2 * NSA_CMP_HID), F32)
    w1 = w1.at[:, :HEAD_DIM, :NSA_CMP_HID].set(kw1.reshape(NSA_CMP_LEN, HEAD_DIM, NSA_CMP_HID))
    w1 = w1.at[:, HEAD_DIM:, NSA_CMP_HID:].set(vw1.reshape(NSA_CMP_LEN, HEAD_DIM, NSA_CMP_HID))
    w1 = w1.reshape(NSA_CMP_LEN * LANES, 2 * NSA_CMP_HID).astype(BF16)
    w2 = jnp.stack([jnp.tile(kw2, (1, 2)), jnp.tile(vw2, (1, 2))]).astype(BF16)
    out = jax.ShapeDtypeStruct((B, n16, LANES), BF16)
    return pl.pallas_call(
        _compress_kernel,
        grid=(B,),
        in_specs=[pl.BlockSpec((None, S, LANES), lambda b: (b, 0, 2 * PB_C0)),
                  pl.BlockSpec(perm.shape, lambda b: (0, 0)),
                  pl.BlockSpec(pos.shape, lambda b: (0, 0)),
                  pl.BlockSpec(w1.shape, lambda b: (0, 0)),
                  pl.BlockSpec(w2.shape, lambda b: (0, 0, 0))],
        out_specs=[pl.BlockSpec((None, n16, LANES), lambda b: (b, 0, 0))] * 2,
        out_shape=[out, out],
        compiler_params=pltpu.CompilerParams(dimension_semantics=("arbitrary",),
                                             vmem_limit_bytes=VMEM_LIMIT),
        name="nsa_compress",
    )(proj3, perm, pos, w1, w2)


def _cmp_topk_kernel(q_ref, kc_ref, vc_ref, ovt_ref, o_ref, nsel_ref, *, T, n_cmp, n_slc):
    i = pl.program_id(1)
    t0 = i * T
    lane = lax.broadcasted_iota(jnp.int32, (T, LANES), 1)
    tok = t0 + lax.broadcasted_iota(jnp.int32, (T, LANES), 0)
    ok = jnp.logical_and(lane * NSA_CMP_STRIDE + (NSA_CMP_LEN - 1) <= tok, lane < n_cmp)
    lo = lane < HEAD_DIM
    qm = _masked_heads(q_ref, lo)
    kc = kc_ref[...]
    vc = vc_ref[...]
    psum = jnp.zeros((T, LANES), F32)
    outs = []
    for h in range(N_HEADS):
        s = jnp.where(ok, _dot_nt(qm[h], kc), NEG)
        mx = jnp.max(s, axis=-1, keepdims=True)
        e = jnp.where(ok, jnp.exp2(s - mx), 0.0)
        den = jnp.sum(e, axis=-1, keepdims=True)
        p = e * (1.0 / jnp.maximum(den, 1e-30))
        psum = psum + p
        outs.append(_dot(p.astype(BF16), vc))
    for pair in range(2):
        o_ref[:, pair * LANES:(pair + 1) * LANES] = jnp.where(
            lo, outs[2 * pair], outs[2 * pair + 1]).astype(BF16)

    p_hi = psum.astype(BF16)
    p_lo = (psum - p_hi.astype(F32)).astype(BF16)
    ovt = ovt_ref[...]
    imp = _dot_nt(ovt, p_hi) + _dot_nt(ovt, p_lo)
    jj = lax.broadcasted_iota(jnp.int32, (n_slc, T), 0)
    bt = (t0 + lax.broadcasted_iota(jnp.int32, (n_slc, T), 1)) >> 6
    forced = jnp.logical_or(jj == 0, jnp.logical_or(jj == bt, jj == bt - 1))
    val = jnp.where(forced, BIG, jnp.where(jj > bt, -BIG, imp))
    sel_rows = []
    for j in range(n_slc):
        vj = val[j:j + 1, :]
        beats = jnp.logical_or(val > vj, jnp.logical_and(val == vj, jj < j))
        rank = jnp.sum(beats.astype(F32), axis=0, keepdims=True)
        sel_rows.append((rank >= NSA_N_SEL).astype(F32))
    sel_rows.append(jnp.zeros((LANES - n_slc, T), F32))
    nsel = jnp.concatenate(sel_rows, axis=0).astype(BF16)
    for blk in range(T // TB):
        nsel_ref[blk] = nsel[:, blk * TB:(blk + 1) * TB]


def _cmp_topk(proj3, kc2, vc2, B, S):
    T = CMP_TILE
    n_cmp = (S - NSA_CMP_LEN) // NSA_CMP_STRIDE + 1
    n_slc = S // NSA_SEL_LEN
    assert NSA_SEL_LEN == 64 and n_slc <= LANES and n_cmp <= LANES
    ci = np.arange(LANES)[None, :] * NSA_CMP_STRIDE
    sj = np.arange(n_slc)[:, None] * NSA_SEL_LEN
    ovt = ((ci < sj + NSA_SEL_LEN) & (ci + NSA_CMP_LEN > sj) & (np.arange(LANES)[None, :] < n_cmp))
    ovt = jnp.asarray(ovt.astype(np.float32), BF16)
    return pl.pallas_call(
        functools.partial(_cmp_topk_kernel, T=T, n_cmp=n_cmp, n_slc=n_slc),
        grid=(B, S // T),
        in_specs=[pl.BlockSpec((None, T, BLK), lambda b, i: (b, i, PB_CQ)),
                  pl.BlockSpec((None, LANES, LANES), lambda b, i: (b, 0, 0)),
                  pl.BlockSpec((None, LANES, LANES), lambda b, i: (b, 0, 0)),
                  pl.BlockSpec(ovt.shape, lambda b, i: (0, 0))],
        out_specs=[pl.BlockSpec((None, T, BLK), lambda b, i: (b, i, 0)),
                   pl.BlockSpec((None, T // TB, LANES, TB), lambda b, i: (b, i, 0, 0))],
        out_shape=[jax.ShapeDtypeStruct((B, S, BLK), BF16),
                   jax.ShapeDtypeStruct((B, S // TB, LANES, TB), BF16)],
        compiler_params=pltpu.CompilerParams(dimension_semantics=("arbitrary", "arbitrary"),
                                             vmem_limit_bytes=VMEM_LIMIT),
        name="nsa_cmp_topk",
    )(proj3, kc2, vc2, ovt)


def _stick_kernel(q_all_ref, k_ref, vt_ref, u_ref, o_ref, za_ref, zb_ref, w_ref, carry_ref,
                  acc_ref):
    T = TB
    krow = lax.broadcasted_iota(jnp.int32, (T, T), 0)
    qcol = lax.broadcasted_iota(jnp.int32, (T, T), 1)
    strict = krow < qcol
    u = u_ref[...]
    z_refs = (za_ref, zb_ref)

    def query_block(i, carry):
        qm = _masked_head_rows(q_all_ref.at[i])

        def stage(j, slot):
            rows = pl.ds(pl.multiple_of(j * T, T), T)
            for h in range(N_HEADS):
                z_refs[slot][h] = _dot(k_ref[rows, (h // 2) * LANES:(h // 2 + 1) * LANES],
                                       qm[h])

        def apply_weights(j):
            vt_all = vt_ref[j]
            for h in range(N_HEADS):
                acc_ref[h] = acc_ref[h] + _dot(vt_all[h * HEAD_DIM:(h + 1) * HEAD_DIM, :], w_ref[h])

        def absorb(j, slot, diag, staged_next, applied_prev):
            z_ref = z_refs[slot]
            softplus = []
            for h in range(N_HEADS):
                z = z_ref[h]
                sp = jnp.maximum(z, jnp.log2(1.0 + jnp.exp2(jnp.minimum(z, 64.0))))
                if diag:
                    sp = jnp.where(strict, sp, 0.0)
                softplus.append(sp.astype(BF16))
            if staged_next is not None:
                stage(staged_next, 1 - slot)
            if applied_prev is not None:
                apply_weights(applied_prev)
            sums = [_dot(u, sp) for sp in softplus]
            for h in range(N_HEADS):
                a = jnp.exp2(z_ref[h] - (sums[h] + carry_ref[h]))
                if diag:
                    a = jnp.where(strict, a, 0.0)
                w_ref[h] = a.astype(BF16)
                carry_ref[h] = carry_ref[h] + sums[h][0:1, :]

        carry_ref[...] = jnp.zeros(carry_ref.shape, F32)
        acc_ref[...] = jnp.zeros(acc_ref.shape, F32)
        stage(i, 0)
        absorb(i, 0, True, jnp.maximum(i - 1, 0), None)

        def body(t, c):
            j = i - 1 - 2 * t
            absorb(j, 1, False, j - 1, j + 1)
            absorb(j - 1, 0, False, jnp.maximum(j - 2, 0), j)
            return c

        lax.fori_loop(0, i // 2, body, 0)

        @pl.when(i % 2 == 1)
        def _():
            absorb(0, 1, False, None, 1)

        apply_weights(0)
        q_rows = pl.ds(pl.multiple_of(i * T, T), T)
        for pair in range(2):
            o_t = jnp.concatenate([acc_ref[2 * pair], acc_ref[2 * pair + 1]], axis=0)
            o_ref[q_rows, pair * LANES:(pair + 1) * LANES] = o_t.T.astype(BF16)
        return carry

    lax.fori_loop(0, q_all_ref.shape[0], query_block, 0)


def _stick_breaking(qdt, proj3, vdt, B, S):
    u = np.triu(np.ones((TB, TB), np.float32))
    return pl.pallas_call(
        _stick_kernel,
        grid=(B,),
        in_specs=[pl.BlockSpec((None,) + qdt.shape[1:], lambda b: (b, 0, 0, 0)),
                  pl.BlockSpec((None, S, BLK), lambda b: (b, 0, PB_DK)),
                  pl.BlockSpec((None,) + vdt.shape[1:], lambda b: (b, 0, 0, 0)),
                  pl.BlockSpec(u.shape, lambda b: (0, 0))],
        out_specs=pl.BlockSpec((None, S, BLK), lambda b: (b, 0, 0)),
        out_shape=jax.ShapeDtypeStruct((B, S, BLK), BF16),
        scratch_shapes=[pltpu.VMEM((N_HEADS, TB, TB), F32),
                        pltpu.VMEM((N_HEADS, TB, TB), F32),
                        pltpu.VMEM((N_HEADS, TB, TB), BF16),
                        pltpu.VMEM((N_HEADS, 1, TB), F32),
                        pltpu.VMEM((N_HEADS, HEAD_DIM, TB), F32)],
        compiler_params=pltpu.CompilerParams(dimension_semantics=("arbitrary",),
                                             vmem_limit_bytes=VMEM_LIMIT),
        name="stick_breaking",
    )(qdt, proj3, vdt, jnp.asarray(u, BF16))


def _outproj_kernel(x_ref, oa, ob, oc, os_, ow, od,
                    ga, gb, gl, gc, gd, expand_ref, w_ref, g_ref, out_ref):
    f = lambda r: r[...].astype(F32)
    gates = _dot(gl[...], expand_ref[...])
    o_c = (gates[:, 0:BLK] * f(oc) + gates[:, BLK:2 * BLK] * f(os_)
           + gates[:, 2 * BLK:3 * BLK] * f(ow))
    mixed = [f(oa) * f(ga), f(ob) * f(gb), o_c * f(gc), f(od) * f(gd)]
    y = _dot(mixed[0].astype(BF16), w_ref[0:BLK, :])
    for g in range(1, 4):
        y = y + _dot(mixed[g].astype(BF16), w_ref[g * BLK:(g + 1) * BLK, :])
    ms = jnp.mean(y * y, axis=-1, keepdims=True)
    out_ref[...] = x_ref[...] + (y * lax.rsqrt(ms + EPS)) * g_ref[...]


def _out_projection(x2, mixer_outs, proj, w_out, g_post):
    rows = x2.shape[0]
    tm = OUT_TILE
    blk = lambda c: pl.BlockSpec((tm, BLK), lambda i: (i, c))
    expand = np.zeros((LANES, N_BRANCH * BLK), np.float32)
    for h in range(N_HEADS):
        for r in range(N_BRANCH):
            expand[N_BRANCH * h + r, r * BLK + h * HEAD_DIM:r * BLK + (h + 1) * HEAD_DIM] = 1.0
    gate_specs = [blk(PB_AG), blk(PB_BG),
                  pl.BlockSpec((tm, LANES), lambda i: (i, 2 * PB_C1 + 1)), blk(PB_CG), blk(PB_DG)]
    return pl.pallas_call(
        _outproj_kernel,
        grid=(rows // tm,),
        in_specs=[pl.BlockSpec((tm, D_MODEL), lambda i: (i, 0))]
                 + [blk(0)] * len(mixer_outs) + gate_specs
                 + [pl.BlockSpec(expand.shape, lambda i: (0, 0)),
                    pl.BlockSpec((D_MODEL, D_MODEL), lambda i: (0, 0)),
                    pl.BlockSpec((1, D_MODEL), lambda i: (0, 0))],
        out_specs=pl.BlockSpec((tm, D_MODEL), lambda i: (i, 0)),
        out_shape=jax.ShapeDtypeStruct((rows, D_MODEL), F32),
        compiler_params=pltpu.CompilerParams(dimension_semantics=("arbitrary",),
                                             vmem_limit_bytes=VMEM_LIMIT),
        name="out_projection",
    )(x2, *mixer_outs, *([proj] * len(gate_specs)), jnp.asarray(expand, BF16),
      w_out.astype(BF16), g_post.reshape(1, D_MODEL))


def _layer(x2, tabs, B, S, layer, w_rows, w_t, w_out, g_pre, g_post, mla_g_q, mla_g_kv,
           mla_w_uq, mla_w_ukv, nsa_pos_k, nsa_pos_v, nsa_k_w1, nsa_k_w2, nsa_v_w1, nsa_v_w2):
    rows = B * S
    aqkv, aqkv4, aqkv16, proj, vct, vdt, qct, qdt = _in_projection(
        x2, g_pre, w_rows, w_t, layer, tabs, B, S)
    proj3 = proj.reshape(B, S, N_PBLK * BLK)

    o_a = _dilated(aqkv, aqkv4, aqkv16, B, S)

    qnt, qrt, kn, vbt = _mla_up(proj, mla_g_q, mla_g_kv, mla_w_uq, mla_w_ukv, tabs, B, S)
    o_b = _flash_t(qnt, kn.reshape(B, S, BLK), 0, BLK, vbt, B=B, S=S,
                   aux=("per_head", qrt, proj3, 2 * PB_BKV + 1), name="mla_attention")

    kc2, vc2 = _compress(proj3, nsa_pos_k, nsa_pos_v, nsa_k_w1, nsa_k_w2, nsa_v_w1, nsa_v_w2, B, S)
    o_cmp, nsel = _cmp_topk(proj3, kc2, vc2, B, S)
    penalty = np.zeros((S, LANES), np.float32)
    penalty[np.arange(S), np.arange(S) // NSA_SEL_LEN] = MASK_PENALTY
    o_slc = _flash_t(qct, proj3, 2 * PB_C0 + 1, LANES, vct, B=B, S=S, vrow0=0,
                     aux=("shared", nsel, jnp.asarray(penalty, BF16), 0), name="nsa_selected")
    o_win = _flash_t(qct, proj3, 2 * PB_C1, LANES, vct, B=B, S=S, vrow0=HEAD_DIM,
                     window=NSA_WINDOW - 1, name="nsa_window")

    o_d = _stick_breaking(qdt, proj3, vdt, B, S)

    flat = lambda t: t.reshape(rows, BLK)
    outs = [flat(o) for o in (o_a, o_b, o_cmp, o_slc, o_win, o_d)]
    return _out_projection(x2, outs, proj, w_out, g_post)


def kernel(x, positions, w_in, w_out, g_pre, g_post, mla_g_q, mla_g_kv, mla_w_uq, mla_w_ukv,
           nsa_pos_k, nsa_pos_v, nsa_k_w1, nsa_k_w2, nsa_v_w1, nsa_v_w2):
    B, S, D = x.shape
    assert D == D_MODEL and S % 1024 == 0
    x2 = x.reshape(B * S, D)
    tabs = _rope_tables(positions)
    w_rows, w_t = _pack_w_in(w_in)
    for l in range(w_in.shape[0]):
        x2 = _layer(x2, tabs, B, S, l, w_rows, w_t, w_out[l], g_pre[l], g_post[l],
                    mla_g_q[l], mla_g_kv[l], mla_w_uq[l], mla_w_ukv[l],
                    nsa_pos_k[l], nsa_pos_v[l], nsa_k_w1[l], nsa_k_w2[l],
                    nsa_v_w1[l], nsa_v_w2[l])
    return x2.reshape(B, S, D)
```

```python
import functools
import math

import numpy as np
import jax
import jax.numpy as jnp
from jax import lax
from jax.experimental import pallas as pl
from jax.experimental.pallas import tpu as pltpu

D_MODEL = 1024
HEAD_DIM = 64
N_HEADS = 4
ROPE_THETA = 10000.0
EPS = 1e-6
NEG = -1e30
LOG2E = math.log2(math.e)
MASK_PENALTY = -(2.0 ** 100)
BIG = 1e9
STICK_DEAD = 160.0

MLA_Q_RANK = 256
MLA_KV_RANK = 128
MLA_NOPE = 64
MLA_ROPE = 32
MLA_V = 64

NSA_CMP_LEN = 32
NSA_CMP_STRIDE = 16
NSA_CMP_HID = 256
NSA_SEL_LEN = 64
NSA_N_SEL = 16
NSA_WINDOW = 512

LANES = 128
BLK = 256
IN_TILE = 1024
OUT_TILE = 1024
TILE = 512
MLA_UP_TILE = 2048
ROPE_TILE = 2048
CMP_TILE = 2048
PACK_COLS = 256
CM_TILE = 256
DILATIONS = (1, 4, 16)
DIL_BLOCK = 128
LSE_PARTS = 2
TB = 256
ONES_ROWS = 16
VMEM_LIMIT = 48 * 1024 * 1024

F32 = jnp.float32
BF16 = jnp.bfloat16

(PB_AG, PB_BCQ, PB_BKV, PB_BG, PB_CQ, PB_C0, PB_C1, PB_CG, PB_DQ, PB_DK, PB_DG) = range(11)
N_PBLK = 11
N_BRANCH = 3
ROPE_NONE, ROPE_64, ROPE_32, ROPE_64_LO, ACT_SILU, ACT_SIGMOID = range(6)


def _dot_nt(a, b):
    return lax.dot_general(a, b, (((1,), (1,)), ((), ())), preferred_element_type=F32)


def _dot(a, b):
    return jnp.dot(a, b, preferred_element_type=F32)


def _rope_half(y, cos, sin_signed, group):
    half = group // 2
    lane = lax.broadcasted_iota(jnp.int32, y.shape, 1)
    first = (lane & (group - 1)) < half
    partner = jnp.where(first, pltpu.roll(y, LANES - half, 1), pltpu.roll(y, half, 1))
    return y * cos + partner * sin_signed


def _rope_table_kernel(pos_ref, inv_ref, cos_ref, sin_ref):
    ang = inv_ref[...] * pos_ref[...].astype(F32)
    cos_ref[...] = jnp.cos(ang)
    sin_ref[...] = jnp.sin(ang)


def _rope_tables(positions):
    rows = positions.size
    tm = ROPE_TILE
    inv =np.concatenate([ROPE_THETA ** (-np.arange(n, dtype=np.float64) / n)
                          for n in (HEAD_DIM // 2, MLA_ROPE // 2)]).astype(np.float32)[:, None]
    out = jax.ShapeDtypeStruct((inv.shape[0], rows), F32)
    tab_spec = pl.BlockSpec((inv.shape[0], tm), lambda i: (0, i))
    return pl.pallas_call(
        _rope_table_kernel,
        grid=(rows // tm,),
        in_specs=[pl.BlockSpec((1, tm), lambda i: (0, i)),
                  pl.BlockSpec(inv.shape, lambda i: (0, 0))],
        out_specs=[tab_spec] * 2,
        out_shape=[out] * 2,
        name="rope_tables",
    )(positions.reshape(1, rows), jnp.asarray(inv))


def _token_major_rope(cos_ref, sin_ref, group):
    n = group // 2
    lo = 0 if group == HEAD_DIM else HEAD_DIM // 2
    c = cos_ref[lo:lo + n, :]
    s = sin_ref[lo:lo + n, :]
    reps = LANES // n
    cos = jnp.concatenate([c] * reps, axis=0).T
    sin = jnp.concatenate([-s if k % 2 == 0 else s for k in range(reps)], axis=0).T
    return cos, sin


def _inproj_kernel(x_ref, g_ref, w_ref, wt_ref, perm_ref, cos_ref, sin_ref,
                   aqkv_ref, aqkv4_ref, aqkv16_ref, proj_ref, vct_ref, vdt_ref, qct_ref, qdt_ref,
                   *, plan_a, plan_p):
    x = x_ref[...]
    tm = x.shape[0]
    ms = jnp.mean(x * x, axis=-1, keepdims=True)
    h = ((x * lax.rsqrt(ms + EPS)) * g_ref[...]).astype(BF16)
    lane = lax.broadcasted_iota(jnp.int32, (tm, LANES), 1)
    c64, s64 = _token_major_rope(cos_ref, sin_ref, HEAD_DIM)
    c32, s32 = _token_major_rope(cos_ref, sin_ref, MLA_ROPE)

    def epilogue(y, ops, scale):
        halves = []
        for hf, op in enumerate(ops):
            yh = y[:, hf * LANES:(hf + 1) * LANES]
            if op == ROPE_64:
                yh = _rope_half(yh, c64, s64, 64)
            elif op == ROPE_32:
                yh = _rope_half(yh, c32, s32, 32)
            elif op == ROPE_64_LO:
                yh = jnp.where(lane < 64, _rope_half(yh, c64, s64, 64), yh)
            elif op == ACT_SILU:
                yh = yh * (1.0 / (1.0 + jnp.exp(-yh)))
            elif op == ACT_SIGMOID:
                yh = 1.0 / (1.0 + jnp.exp(-yh))
            if scale != 1.0:
                yh = yh * scale
            halves.append(yh)
        return jnp.concatenate(halves, axis=1)

    feature_major = {PB_CQ: qct_ref, PB_DQ: qdt_ref}
    col = 0
    for out_ref, plan in ((aqkv_ref, plan_a), (proj_ref, plan_p)):
        for b, (ops, scale) in enumerate(plan):
            y = epilogue(_dot(h, w_ref[:, col:col + BLK]), ops, scale)
            out_ref[:, b * BLK:(b + 1) * BLK] = y.astype(BF16)
            if out_ref is proj_ref and b in feature_major:
                y_t = y.T
                for blk in range(tm // TB):
                    feature_major[b][blk] = y_t[:, blk * TB:(blk + 1) * TB].astype(BF16)
            col += BLK

    for t in range(tm // CM_TILE):
        tile = slice(CM_TILE * t, CM_TILE * (t + 1))
        a_rows = aqkv_ref[tile, :]
        aqkv4_ref[tile, :] = _dot(perm_ref[0], a_rows).astype(BF16)
        aqkv16_ref[tile, :] = _dot(perm_ref[1], a_rows).astype(BF16)

    vt = _dot_nt(wt_ref[...], h)
    for blk in range(tm // TB):
        cols = slice(blk * TB, (blk + 1) * TB)
        vct_ref[blk] = vt[0:LANES, cols].astype(BF16)
        vdt_ref[blk] = vt[LANES:LANES + BLK, cols].astype(BF16)


_PLAN_A = (((ROPE_64, ROPE_64), LOG2E * HEAD_DIM ** -0.5),
           ((ROPE_64, ROPE_64), 1.0),
           ((ROPE_NONE, ROPE_NONE), 1.0))
_PLAN_P = (
    ((ACT_SILU, ACT_SILU), 1.0),
    ((ROPE_NONE, ROPE_NONE), 1.0),
    ((ROPE_NONE, ROPE_32), 1.0),
    ((ACT_SILU, ACT_SILU), 1.0),
    ((ROPE_64, ROPE_64), LOG2E * HEAD_DIM ** -0.5),
    ((ROPE_64_LO, ROPE_64), 1.0),
    ((ROPE_64, ACT_SIGMOID), 1.0),
    ((ACT_SILU, ACT_SILU), 1.0),
    ((ROPE_NONE, ROPE_NONE), LOG2E * HEAD_DIM ** -0.5),
    ((ROPE_NONE, ROPE_NONE), 1.0),
    ((ACT_SILU, ACT_SILU), 1.0),
)
assert len(_PLAN_P) == N_PBLK


_W_IN_SEGMENTS = (("aq", 256), ("ak", 256), ("av", 256), ("ag", 256),
                  ("bcq", 256), ("bckv", 128), ("bkr", 32), ("bg", 256),
                  ("cq", 256), ("ckc", 64), ("cvc", 64), ("cks", 64), ("cvs", 64),
                  ("ckw", 64), ("cvw", 64), ("cgl", 12), ("cg", 256),
                  ("dq", 256), ("dk", 256), ("dv", 256), ("dg", 256))
W_ROWS_COLS = (3 + N_PBLK) * BLK
W_T_ROWS = LANES + BLK


def _pack_w_kernel(wt_ref, rows_ref, t_ref):
    wt = wt_ref[...].astype(F32)
    o = 0
    seg = {}
    for name, width in _W_IN_SEGMENTS:
        seg[name] = wt[o:o + width, :]
        o += width
    gl_pad = jnp.zeros((LANES - N_HEADS * N_BRANCH, wt.shape[1]), wt.dtype)
    rows = [seg["aq"], seg["ak"], seg["av"],
            seg["ag"], seg["bcq"],
            seg["bckv"], seg["bkr"], seg["bkr"], seg["bkr"], seg["bkr"],
            seg["bg"], seg["cq"],
            seg["ckc"], seg["cvc"], seg["cks"], seg["cks"],
            seg["ckw"], seg["ckw"], seg["cgl"], gl_pad,
            seg["cg"], seg["dq"], seg["dk"], seg["dg"]]
    rows_ref[...] = jnp.concatenate(rows, axis=0).T.astype(BF16)
    t_ref[...] = jnp.concatenate([seg["cvs"], seg["cvw"], seg["dv"]], axis=0).astype(BF16)


def _pack_w_in(w_in):
    depth, d_model, d_in = w_in.shape
    assert d_model == D_MODEL and d_in == sum(width for _, width in _W_IN_SEGMENTS)
    tc = PACK_COLS
    return pl.pallas_call(
        _pack_w_kernel,
        grid=(depth, D_MODEL // tc),
        in_specs=[pl.BlockSpec((None, d_in, tc), lambda l, i: (l, 0, i))],
        out_specs=[pl.BlockSpec((None, tc, W_ROWS_COLS), lambda l, i: (l, i, 0)),
                   pl.BlockSpec((None, W_T_ROWS, tc), lambda l, i: (l, 0, i))],
        out_shape=[jax.ShapeDtypeStruct((depth, D_MODEL, W_ROWS_COLS), BF16),
                   jax.ShapeDtypeStruct((depth, W_T_ROWS, D_MODEL), BF16)],
        compiler_params=pltpu.CompilerParams(dimension_semantics=("arbitrary", "arbitrary"),
                                             vmem_limit_bytes=VMEM_LIMIT),
        name="pack_w_in",
    )(jnp.transpose(w_in, (0, 2, 1)).astype(BF16))


def _class_major_perm(dil, tile):
    n = tile // dil
    p = np.zeros((tile, tile), np.float32)
    pos = np.arange(tile)
    p[(pos % dil) * n + pos // dil, pos] = 1.0
    return p


def _in_projection(x2, g_pre, w_rows, w_t, layer, tabs, B, S):
    rows = x2.shape[0]
    tm = IN_TILE
    per_b = S // tm
    nkb = tm // TB
    perm = jnp.asarray(np.stack([_class_major_perm(d, CM_TILE) for d in DILATIONS[1:]]), BF16)
    tab_spec = pl.BlockSpec((tabs[0].shape[0], tm), lambda i: (0, i))
    vt_spec = lambda r: pl.BlockSpec((None, nkb, r, TB), lambda i: (i // per_b, i % per_b, 0, 0))
    a_spec = pl.BlockSpec((tm, 3 * BLK), lambda i: (i, 0))
    a_shape = jax.ShapeDtypeStruct((rows, 3 * BLK), BF16)
    return pl.pallas_call(
        functools.partial(_inproj_kernel, plan_a=_PLAN_A, plan_p=_PLAN_P),
        grid=(rows // tm,),
        in_specs=[pl.BlockSpec((tm, D_MODEL), lambda i: (i, 0)),
                  pl.BlockSpec((1, D_MODEL), lambda i: (0, 0)),
                  pl.BlockSpec((None,) + w_rows.shape[1:], lambda i: (layer, 0, 0)),
                  pl.BlockSpec((None,) + w_t.shape[1:], lambda i: (layer, 0, 0)),
                  pl.BlockSpec(perm.shape, lambda i: (0, 0, 0))] + [tab_spec] * 2,
        out_specs=[a_spec, a_spec, a_spec,
                   pl.BlockSpec((tm, N_PBLK * BLK), lambda i: (i, 0)),
                   vt_spec(LANES), vt_spec(BLK), vt_spec(BLK), vt_spec(BLK)],
        out_shape=[a_shape, a_shape, a_shape,
                   jax.ShapeDtypeStruct((rows, N_PBLK * BLK), BF16),
                   jax.ShapeDtypeStruct((B, S // TB, LANES, TB), BF16)]
                  + [jax.ShapeDtypeStruct((B, S // TB, BLK, TB), BF16)] * 3,
        compiler_params=pltpu.CompilerParams(dimension_semantics=("arbitrary",),
                                             vmem_limit_bytes=VMEM_LIMIT),
        name="in_projection",
    )(x2, g_pre.reshape(1, D_MODEL), w_rows, w_t, perm, *tabs)


def _head_masks(shape):
    lane = lax.broadcasted_iota(jnp.int32, shape, 1)
    return lane < HEAD_DIM


def _masked_heads(q_ref, lo_mask):
    out = []
    for h in range(N_HEADS):
        pair, e = divmod(h, 2)
        qp = q_ref[:, pair * LANES:(pair + 1) * LANES]
        keep = lo_mask if e == 0 else jnp.logical_not(lo_mask)
        out.append(jnp.where(keep, qp, jnp.zeros_like(qp)))
    return out


def _dilated_kernel(an_ref, a4_ref, a16_ref, unperm_ref, expand_ref, o_ref, res_ref, s_ref):
    T = DIL_BLOCK
    S = an_ref.shape[0]
    n_tiles = S // CM_TILE
    lo = _head_masks((T, LANES))
    lane = lax.broadcasted_iota(jnp.int32, (T, LANES), 1)
    dist2 = (lax.broadcasted_iota(jnp.int32, (T, 2 * T), 0)
             - lax.broadcasted_iota(jnp.int32, (T, 2 * T), 1))
    causal1 = (lax.broadcasted_iota(jnp.int32, (T, T), 1)
               <= lax.broadcasted_iota(jnp.int32, (T, T), 0))

    def scores(q, k):
        qm = _masked_heads(q, lo)
        out = []
        for pair in range(N_HEADS // 2):
            both = _dot_nt(jnp.concatenate([qm[2 * pair], qm[2 * pair + 1]], axis=0),
                           k[:, pair * LANES:(pair + 1) * LANES])
            out += [both[:T], both[T:]]
        return tuple(out)

    def finish(n_keys, v, mask, stage_next):
        probs = []
        for h in range(N_HEADS):
            s = jnp.where(mask, s_ref[h, :, 0:n_keys], NEG)
            m = jnp.max(s, axis=-1, keepdims=True)
            p = jnp.exp2(s - m)
            probs.append((m, jnp.sum(p, axis=-1, keepdims=True), p.astype(BF16)))
        stage_next()
        pv = []
        for pair in range(N_HEADS // 2):
            both = _dot(jnp.concatenate([probs[2 * pair][2], probs[2 * pair + 1][2]], axis=0),
                        v[:, pair * LANES:(pair + 1) * LANES])
            pv += [both[:T], both[T:]]
        outs = []
        top = jnp.zeros((T, LANES), F32)
        den = jnp.ones((T, LANES), F32)
        for h in range(N_HEADS):
            m, l, _ = probs[h]
            outs.append(pv[h] * (1.0 / l))
            mine = (lane & (N_HEADS - 1)) == h
            top = jnp.where(mine, m, top)
            den = jnp.where(mine, l, den)
        lse = top + jnp.log2(den)
        hi = lse.astype(BF16).astype(F32)
        parts = jnp.where(lane < N_HEADS, hi,
                          jnp.where(lane < LSE_PARTS * N_HEADS, lse - hi, 0.0))
        return jnp.concatenate(
            [jnp.where(lo, outs[0], outs[1]), jnp.where(lo, outs[2], outs[3]), parts],
            axis=1).astype(BF16)

    def band_mask(delta):
        d = dist2 + delta
        return jnp.logical_and(d >= 0, d <= DIL_BLOCK)

    def run(n_units, n_keys, load_qk, load_v, store, mask_of):
        def stage(u):
            for h, s in enumerate(scores(*load_qk(u))):
                s_ref[h, :, 0:n_keys] = s

        def body(u, carry):
            nxt = jnp.minimum(u + 1, n_units - 1)
            store(u, finish(n_keys, load_v(u), mask_of(u), lambda: stage(nxt)))
            return carry

        stage(0)
        lax.fori_loop(0, n_units, body, 0)

    def rows(start, size):
        return pl.ds(pl.multiple_of(start, 16), size)

    def k0_1(u):
        return T * jnp.maximum(u - 1, 0)

    def store1(u, r):
        res_ref[0, rows(T * u, T), :] = r

    run(S // T, 2 * T,
        lambda u: (an_ref[rows(T * u, T), 0:BLK], an_ref[rows(k0_1(u), 2 * T), BLK:2 * BLK]),
        lambda u: an_ref[rows(k0_1(u), 2 * T), 2 * BLK:3 * BLK],
        store1,
        lambda u: band_mask(T * u - k0_1(u)))

    per4 = CM_TILE // 4
    nq4 = S // 4 // T

    def pieces4(u, blk):
        first_tile = blk * (T // per4)
        return [rows(CM_TILE * (first_tile + k) + per4 * (u // nq4), per4)
                for k in range(T // per4)]

    def kb0_4(u):
        return jnp.maximum(u % nq4 - 1, 0)

    def load4(u, blks, cols):
        return jnp.concatenate([a4_ref[p, cols] for blk in blks for p in pieces4(u, blk)], axis=0)

    def store4(u, r):
        for k, p in enumerate(pieces4(u, u % nq4)):
            res_ref[1, p, :] = r[per4 * k:per4 * (k + 1), :]

    run(4 * nq4, 2 * T,
        lambda u: (load4(u, [u % nq4], slice(0, BLK)),
                   load4(u, [kb0_4(u), kb0_4(u) + 1], slice(BLK, 2 * BLK))),
        lambda u: load4(u, [kb0_4(u), kb0_4(u) + 1], slice(2 * BLK, 3 * BLK)),
        store4,
        lambda u: band_mask(T * (u % nq4 - kb0_4(u))))

    per16 = CM_TILE // 16

    def load16(c, cols):
        return jnp.concatenate([a16_ref[rows(CM_TILE * t + per16 * c, per16), cols]
                                for t in range(n_tiles)], axis=0)

    def store16(c, r):
        for t in range(n_tiles):
            res_ref[2, rows(CM_TILE * t + per16 * c, per16), :] = r[per16 * t:per16 * (t + 1), :]

    run(16, T,
        lambda c: (load16(c, slice(0, BLK)), load16(c, slice(BLK, 2 * BLK))),
        lambda c: load16(c, slice(2 * BLK, 3 * BLK)),
        store16,
        lambda c: causal1)

    expand = expand_ref[...]
    for t in range(n_tiles):
        tile = slice(CM_TILE * t, CM_TILE * (t + 1))
        nat = [res_ref[0, tile, :].astype(F32),
               _dot(unperm_ref[0], res_ref[1, tile, :]),
               _dot(unperm_ref[1], res_ref[2, tile, :])]
        lse = [_dot(r[:, BLK:].astype(BF16), expand) for r in nat]
        top = jnp.maximum(lse[0], jnp.maximum(lse[1], lse[2]))
        w = [jnp.exp2(l - top) for l in lse]
        mixed = (w[0] * nat[0][:, :BLK] + w[1] * nat[1][:, :BLK] + w[2] * nat[2][:, :BLK])
        o_ref[tile, :] = (mixed / (w[0] + w[1] + w[2])).astype(BF16)


def _dilated(aqkv, aqkv4, aqkv16, B, S):
    assert S == 16 * DIL_BLOCK and S % CM_TILE == 0 and DIL_BLOCK % (CM_TILE // 4) == 0
    r3 = lambda t: t.reshape(B, S, 3 * BLK)
    unperm = jnp.asarray(np.stack([_class_major_perm(d, CM_TILE).T for d in DILATIONS[1:]]), BF16)
    expand = np.zeros((LANES, BLK), np.float32)
    for h in range(N_HEADS):
        for part in range(LSE_PARTS):
            expand[N_HEADS * part + h, h * HEAD_DIM:(h + 1) * HEAD_DIM] = 1.0
    a_spec = pl.BlockSpec((None, S, 3 * BLK), lambda b: (b, 0, 0))
    return pl.pallas_call(
        _dilated_kernel,
        grid=(B,),
        in_specs=[a_spec, a_spec, a_spec,
                  pl.BlockSpec(unperm.shape, lambda b: (0, 0, 0)),
                  pl.BlockSpec(expand.shape, lambda b: (0, 0))],
        out_specs=pl.BlockSpec((None, S, BLK), lambda b: (b, 0, 0)),
        out_shape=jax.ShapeDtypeStruct((B, S, BLK), BF16),
        scratch_shapes=[pltpu.VMEM((3, S, BLK + LANES), BF16),
                        pltpu.VMEM((N_HEADS, DIL_BLOCK, 2 * DIL_BLOCK), F32)],
        compiler_params=pltpu.CompilerParams(dimension_semantics=("arbitrary",),
                                             vmem_limit_bytes=VMEM_LIMIT),
        name="dilated_mixture",
    )(r3(aqkv), r3(aqkv4), r3(aqkv16), unperm, jnp.asarray(expand, BF16))


def _masked_head_rows(qt_ref):
    out = []
    zeros = jnp.zeros((HEAD_DIM, qt_ref.shape[-1]), BF16)
    for h in range(N_HEADS):
        own = qt_ref[h * HEAD_DIM:(h + 1) * HEAD_DIM, :]
        out.append(jnp.concatenate([own, zeros] if h % 2 == 0 else [zeros, own], axis=0))
    return out


def _flash_t_kernel(*refs, window, kshared, vrow0, aux):
    it = iter(refs)
    q_all_ref = next(it)
    qx_all_ref = next(it) if aux else None
    k_ref = next(it)
    kx_ref = next(it) if aux else None
    vt_ref = next(it)
    o_ref = next(it)
    s_ref, m_ref, acc_ref = next(it), next(it), next(it)
    T = TB

    krow = lax.broadcasted_iota(jnp.int32, (T, T), 0)
    qcol = lax.broadcasted_iota(jnp.int32, (T, T), 1)
    ones = jnp.ones((ONES_ROWS, T), BF16)

    n_q = q_all_ref.shape[0]
    cross_stage = window is None
    if window is not None:
        o_full = (window - T + 1) // T
        o_none = (window + T - 1) // T + 1
        assert o_none - o_full == 2

    def query_weights(i):
        qcat = _masked_head_rows(q_all_ref.at[i])
        if aux == "per_head":
            group = LANES // N_HEADS
            qcat = [jnp.concatenate(
                [qcat[h]] + [qx_all_ref[i, g * group:(g + 1) * group, :] if g == h
                             else jnp.zeros((group, T), BF16) for g in range(N_HEADS)], axis=0)
                for h in range(N_HEADS)]
        elif aux == "shared":
            qx = qx_all_ref[i]
            qcat = [jnp.concatenate([qcat[h], qx], axis=0) for h in range(N_HEADS)]
        return qcat

    def first_full_block(i):
        return 0 if window is None else jnp.maximum(i - o_full, 0)

    def scores(j, qcat):
        rows = pl.ds(pl.multiple_of(j * T, T), T)
        out = []
        for h in range(N_HEADS):
            pair = h // 2
            cols = slice(0, LANES) if kshared else slice(pair * LANES, (pair + 1) * LANES)
            kb = k_ref[rows, cols]
            if aux:
                kb = jnp.concatenate([kb, kx_ref[rows, :]], axis=1)
            out.append(_dot(kb, qcat[h]))
        return out

    def stage(j, qcat):
        for h, s in enumerate(scores(j, qcat)):
            s_ref[h] = s

    def band_block(i):
        o = o_full + 1
        j = i - o
        limit = jnp.where(j >= 0, window - o * T, -2 * T)
        return jnp.maximum(j, 0), limit

    def stage_query_block(i):
        stage(first_full_block(i), query_weights(i))

    def query_block(i, carry):
        qcat = query_weights(i)

        def absorb(j, mask, staged_next=None, s_all=None):
            vt_all = vt_ref[j]
            probs = []
            for h in range(N_HEADS):
                s = s_ref[h] if s_all is None else s_all[h]
                if mask is not None:
                    s = jnp.where(mask, s, NEG)
                m = m_ref[h]
                m_new = jnp.maximum(m, jnp.max(s, axis=0, keepdims=True))
                alpha = jnp.exp2(m - m_new)
                p = jnp.exp2(s - m_new)
                probs.append((m_new, alpha, p.astype(BF16)))
            if staged_next is not None:
                staged_next()
            for h in range(N_HEADS):
                m_new, alpha, p = probs[h]
                r0 = vrow0 if kshared else h * HEAD_DIM
                lhs = jnp.concatenate([vt_all[r0:r0 + HEAD_DIM, :], ones], axis=0)
                acc_ref[h] = alpha * acc_ref[h] + _dot(lhs, p)
                m_ref[h] = m_new

        m_ref[...] = jnp.full(m_ref.shape, NEG, F32)
        acc_ref[...] = jnp.zeros(acc_ref.shape, F32)
        first = first_full_block(i)
        if window is not None:
            jc, limit = band_block(i)
            s_band = scores(jc, qcat)
            stage(first, qcat)
            absorb(jc, (qcol - krow) <= limit, s_all=s_band)

        def body(j, c):
            absorb(j, None, staged_next=lambda: stage(j + 1, qcat))
            return c

        lax.fori_loop(first, i, body, 0)
        if cross_stage:
            nxt = jnp.minimum(i + 1, n_q - 1)
            absorb(i, krow <= qcol, staged_next=lambda: stage_query_block(nxt))
        else:
            absorb(i, krow <= qcol)

        q_rows = pl.ds(pl.multiple_of(i * T, T), T)
        for pair in range(2):
            outs = []
            for e in range(2):
                acc = acc_ref[2 * pair + e]
                outs.append(acc[0:HEAD_DIM, :] * (1.0 / acc[HEAD_DIM:HEAD_DIM + 1, :]))
            o_ref[q_rows, pair * LANES:(pair + 1) * LANES] = (
                jnp.concatenate(outs, axis=0).T.astype(BF16))
        return carry

    if cross_stage:
        stage_query_block(0)
    lax.fori_loop(0, n_q, query_block, 0)


def _flash_t(qt, k, kcol, kw, vt, *, B, S, window=None, vrow0=0, aux=None, name):
    kshared = kw == LANES
    whole = lambda a: pl.BlockSpec((None,) + a.shape[1:], lambda b: (b,) + (0,) * (a.ndim - 1))
    in_specs = [whole(qt)]
    args = [qt]
    if aux is not None:
        kind, qx, kx, kxcol = aux
        in_specs.append(whole(qx))
        args.append(qx)
    in_specs.append(pl.BlockSpec((None, S, kw), lambda b: (b, 0, kcol)))
    args.append(k)
    if aux is not None:
        if kx.ndim == 3:
            in_specs.append(pl.BlockSpec((None, S, LANES), lambda b: (b, 0, kxcol)))
        else:
            in_specs.append(pl.BlockSpec((S, LANES), lambda b: (0, kxcol)))
        args.append(kx)
    in_specs.append(whole(vt))
    args.append(vt)
    return pl.pallas_call(
        functools.partial(_flash_t_kernel, window=window, kshared=kshared, vrow0=vrow0,
                          aux=None if aux is None else aux[0]),
        grid=(B,),
        in_specs=in_specs,
        out_specs=pl.BlockSpec((None, S, BLK), lambda b: (b, 0, 0)),
        out_shape=jax.ShapeDtypeStruct((B, S, BLK), BF16),
        scratch_shapes=[pltpu.VMEM((N_HEADS, TB, TB), F32),
                        pltpu.VMEM((N_HEADS, 1, TB), F32),
                        pltpu.VMEM((N_HEADS, HEAD_DIM + ONES_ROWS, TB), F32)],
        compiler_params=pltpu.CompilerParams(dimension_semantics=("arbitrary",),
                                             vmem_limit_bytes=VMEM_LIMIT),
        name=name,
    )(*args)


def _mla_up_kernel(cq_ref, ckv_ref, gq_ref, gkv_ref, wq_ref, wk_ref, wvt_ref, cos_ref, sin_ref,
                   qn_ref, qr_ref, kn_ref, vt_ref):
    def norm(t, g):
        t = t.astype(F32)
        ms = jnp.mean(t * t, axis=-1, keepdims=True)
        return ((t * lax.rsqrt(ms + EPS)) * g).astype(BF16)

    scale = LOG2E * (MLA_NOPE + MLA_ROPE) ** -0.5
    q_t = _dot_nt(wq_ref[...], norm(cq_ref[...], gq_ref[...])) * scale
    qn_t = q_t[:BLK, :]
    half = MLA_ROPE // 2
    cos = cos_ref[HEAD_DIM // 2:HEAD_DIM // 2 + half, :]
    sin = sin_ref[HEAD_DIM // 2:HEAD_DIM // 2 + half, :]
    rot = []
    for h in range(N_HEADS):
        x1 = q_t[BLK + h * MLA_ROPE:BLK + h * MLA_ROPE + half, :]
        x2 = q_t[BLK + h * MLA_ROPE + half:BLK + (h + 1) * MLA_ROPE, :]
        rot += [x1 * cos - x2 * sin, x1 * sin + x2 * cos]
    qr_t = jnp.concatenate(rot, axis=0)
    ckv = norm(ckv_ref[...], gkv_ref[...])
    kn_ref[...] = _dot(ckv, wk_ref[...]).astype(BF16)
    vt = _dot_nt(wvt_ref[...], ckv)
    for blk in range(vt.shape[1] // TB):
        cols = slice(blk * TB, (blk + 1) * TB)
        qn_ref[blk] = qn_t[:, cols].astype(BF16)
        qr_ref[blk] = qr_t[:, cols].astype(BF16)
        vt_ref[blk] = vt[:, cols].astype(BF16)


def _mla_up(proj, g_q, g_kv, w_uq, w_ukv, tabs, B, S):
    rows = proj.shape[0]
    tm = MLA_UP_TILE
    per_b = S // tm
    wq = w_uq.reshape(MLA_Q_RANK, N_HEADS, MLA_NOPE + MLA_ROPE)
    wq = jnp.concatenate([wq[:, :, :MLA_NOPE].reshape(MLA_Q_RANK, -1),
                          wq[:, :, MLA_NOPE:].reshape(MLA_Q_RANK, -1)], axis=1).T.astype(BF16)
    wkv = w_ukv.reshape(MLA_KV_RANK, N_HEADS, MLA_NOPE + MLA_V)
    wk = wkv[:, :, :MLA_NOPE].reshape(MLA_KV_RANK, -1).astype(BF16)
    wvt = wkv[:, :, MLA_NOPE:].reshape(MLA_KV_RANK, -1).T.astype(BF16)
    row_spec = lambda w, c: pl.BlockSpec((tm, w), lambda i: (i, c))
    full = lambda a: pl.BlockSpec(a.shape, lambda i: (0, 0))
    gq = g_q.reshape(1, -1)
    gkv = g_kv.reshape(1, -1)
    fm_shape = lambda r: jax.ShapeDtypeStruct((B, S // TB, r, TB), BF16)
    fm_spec = lambda r: pl.BlockSpec((None, tm // TB, r, TB),
                                     lambda i: (i // per_b, i % per_b, 0, 0))
    outs = [fm_shape(BLK), fm_shape(LANES), jax.ShapeDtypeStruct((rows, BLK), BF16), fm_shape(BLK)]
    return pl.pallas_call(
        _mla_up_kernel,
        grid=(rows // tm,),
        in_specs=[row_spec(BLK, PB_BCQ), row_spec(LANES, 2 * PB_BKV), full(gq), full(gkv),
                  full(wq), full(wk), full(wvt)]
                 + [pl.BlockSpec((tabs[0].shape[0], tm), lambda i: (0, i))] * 2,
        out_specs=[fm_spec(BLK), fm_spec(LANES), row_spec(BLK, 0), fm_spec(BLK)],
        out_shape=outs,
        compiler_params=pltpu.CompilerParams(dimension_semantics=("arbitrary",),
                                             vmem_limit_bytes=VMEM_LIMIT),
        name="mla_up",
    )(proj, proj, gq, gkv, wq, wk, wvt, *tabs)


def _compress_kernel(x_ref, perm_ref, pos_ref, w1_ref, w2_ref, kc_ref, vc_ref):
    n_tiles = x_ref.shape[0] // TILE
    per = TILE // NSA_CMP_STRIDE
    slabs = []
    for t in range(n_tiles):
        cm = _dot(perm_ref[...], x_ref[TILE * t:TILE * (t + 1), :])
        slabs.append(jnp.concatenate([cm[per * c:per * (c + 1), :]
                                      for c in range(NSA_CMP_STRIDE)], axis=1))
    x = jnp.concatenate(slabs, axis=0)
    half = x.shape[1]
    top = (x + pos_ref[0:1, :]).astype(BF16)
    bot = (x + pos_ref[1:2, :]).astype(BF16)
    y_top = _dot(top, w1_ref[0:half, :])
    y_bot = _dot(bot, w1_ref[half:2 * half, :])
    pre = y_top + pltpu.roll(y_bot, y_bot.shape[0] - 1, 0)
    hid = (pre * (1.0 / (1.0 + jnp.exp(-pre)))).astype(BF16)
    kc_ref[...] = _dot(hid[:, :NSA_CMP_HID], w2_ref[0]).astype(BF16)
    vc_ref[...] = _dot(hid[:, NSA_CMP_HID:], w2_ref[1]).astype(BF16)


def _compress(proj3, pos_k, pos_v, kw1, kw2, vw1, vw2, B, S):
    assert S % TILE == 0 and NSA_CMP_LEN == 2 * NSA_CMP_STRIDE
    n16 = S // NSA_CMP_STRIDE
    perm = jnp.asarray(_class_major_perm(NSA_CMP_STRIDE, TILE), BF16)
    pos = jnp.concatenate([pos_k, pos_v], axis=1).reshape(2, NSA_CMP_STRIDE * LANES)
    w1 = jnp.zeros((NSA_CMP_LEN, LANES, 2 * NSA_CMP_HID), F32)
    w1 = w1.at[:, :HEAD_DIM, :NSA_CMP_HID].set(kw1.reshape(NSA_CMP_LEN, HEAD_DIM, NSA_CMP_HID))
    w1 = w1.at[:, HEAD_DIM:, NSA_CMP_HID:].set(vw1.reshape(NSA_CMP_LEN, HEAD_DIM, NSA_CMP_HID))
    w1 = w1.reshape(NSA_CMP_LEN * LANES, 2 * NSA_CMP_HID).astype(BF16)
    w2 = jnp.stack([jnp.tile(kw2, (1, 2)), jnp.tile(vw2, (1, 2))]).astype(BF16)
    out = jax.ShapeDtypeStruct((B, n16, LANES), BF16)
    return pl.pallas_call(
        _compress_kernel,
        grid=(B,),
        in_specs=[pl.BlockSpec((None, S, LANES), lambda b: (b, 0, 2 * PB_C0)),
                  pl.BlockSpec(perm.shape, lambda b: (0, 0)),
                  pl.BlockSpec(pos.shape, lambda b: (0, 0)),
                  pl.BlockSpec(w1.shape, lambda b: (0, 0)),
                  pl.BlockSpec(w2.shape, lambda b: (0, 0, 0))],
        out_specs=[pl.BlockSpec((None, n16, LANES), lambda b: (b, 0, 0))] * 2,
        out_shape=[out, out],
        compiler_params=pltpu.CompilerParams(dimension_semantics=("arbitrary",),
                                             vmem_limit_bytes=VMEM_LIMIT),
        name="nsa_compress",
    )(proj3, perm, pos, w1, w2)


def _cmp_topk_kernel(q_ref, kc_ref, vc_ref, ovt_ref, o_ref, nsel_ref, *, T, n_cmp, n_slc):
    i = pl.program_id(1)
    t0 = i * T
    lane = lax.broadcasted_iota(jnp.int32, (T, LANES), 1)
    tok = t0 + lax.broadcasted_iota(jnp.int32, (T, LANES), 0)
    ok = jnp.logical_and(lane * NSA_CMP_STRIDE + (NSA_CMP_LEN - 1) <= tok, lane < n_cmp)
    lo = lane < HEAD_DIM
    qm = _masked_heads(q_ref, lo)
    kc = kc_ref[...]
    vc = vc_ref[...]
    psum = jnp.zeros((T, LANES), F32)
    outs = []
    for h in range(N_HEADS):
        s = jnp.where(ok, _dot_nt(qm[h], kc), NEG)
        mx = jnp.max(s, axis=-1, keepdims=True)
        e = jnp.where(ok, jnp.exp2(s - mx), 0.0)
        den = jnp.sum(e, axis=-1, keepdims=True)
        p = e * (1.0 / jnp.maximum(den, 1e-30))
        psum = psum + p
        outs.append(_dot(p.astype(BF16), vc))
    for pair in range(2):
        o_ref[:, pair * LANES:(pair + 1) * LANES] = jnp.where(
            lo, outs[2 * pair], outs[2 * pair + 1]).astype(BF16)

    p_hi = psum.astype(BF16)
    p_lo = (psum - p_hi.astype(F32)).astype(BF16)
    ovt = ovt_ref[...]
    imp = _dot_nt(ovt, p_hi) + _dot_nt(ovt, p_lo)
    jj = lax.broadcasted_iota(jnp.int32, (n_slc, T), 0)
    bt = (t0 + lax.broadcasted_iota(jnp.int32, (n_slc, T), 1)) >> 6
    forced = jnp.logical_or(jj == 0, jnp.logical_or(jj == bt, jj == bt - 1))
    val = jnp.where(forced, BIG, jnp.where(jj > bt, -BIG, imp))
    sel_rows = []
    for j in range(n_slc):
        vj = val[j:j + 1, :]
        beats = jnp.logical_or(val > vj, jnp.logical_and(val == vj, jj < j))
        rank = jnp.sum(beats.astype(F32), axis=0, keepdims=True)
        sel_rows.append((rank >= NSA_N_SEL).astype(F32))
    sel_rows.append(jnp.zeros((LANES - n_slc, T), F32))
    nsel = jnp.concatenate(sel_rows, axis=0).astype(BF16)
    for blk in range(T // TB):
        nsel_ref[blk] = nsel[:, blk * TB:(blk + 1) * TB]


def _cmp_topk(proj3, kc2, vc2, B, S):
    T = CMP_TILE
    n_cmp = (S - NSA_CMP_LEN) // NSA_CMP_STRIDE + 1
    n_slc = S // NSA_SEL_LEN
    assert NSA_SEL_LEN == 64 and n_slc <= LANES and n_cmp <= LANES
    ci = np.arange(LANES)[None, :] * NSA_CMP_STRIDE
    sj = np.arange(n_slc)[:, None] * NSA_SEL_LEN
    ovt = ((ci < sj + NSA_SEL_LEN) & (ci + NSA_CMP_LEN > sj) & (np.arange(LANES)[None, :] < n_cmp))
    ovt = jnp.asarray(ovt.astype(np.float32), BF16)
    return pl.pallas_call(
        functools.partial(_cmp_topk_kernel, T=T, n_cmp=n_cmp, n_slc=n_slc),
        grid=(B, S // T),
        in_specs=[pl.BlockSpec((None, T, BLK), lambda b, i: (b, i, PB_CQ)),
                  pl.BlockSpec((None, LANES, LANES), lambda b, i: (b, 0, 0)),
                  pl.BlockSpec((None, LANES, LANES), lambda b, i: (b, 0, 0)),
                  pl.BlockSpec(ovt.shape, lambda b, i: (0, 0))],
        out_specs=[pl.BlockSpec((None, T, BLK), lambda b, i: (b, i, 0)),
                   pl.BlockSpec((None, T // TB, LANES, TB), lambda b, i: (b, i, 0, 0))],
        out_shape=[jax.ShapeDtypeStruct((B, S, BLK), BF16),
                   jax.ShapeDtypeStruct((B, S // TB, LANES, TB), BF16)],
        compiler_params=pltpu.CompilerParams(dimension_semantics=("arbitrary", "arbitrary"),
                                             vmem_limit_bytes=VMEM_LIMIT),
        name="nsa_cmp_topk",
    )(proj3, kc2, vc2, ovt)


def _stick_kernel(q_all_ref, k_ref, vt_ref, u_ref, o_ref, za_ref, zb_ref, w_ref, carry_ref,
                  acc_ref):
    T = TB
    krow = lax.broadcasted_iota(jnp.int32, (T, T), 0)
    qcol = lax.broadcasted_iota(jnp.int32, (T, T), 1)
    strict = krow < qcol
    u = u_ref[...]
    z_refs = (za_ref, zb_ref)

    def query_block(i, carry):
        qm = _masked_head_rows(q_all_ref.at[i])

        def stage(j, slot):
            rows = pl.ds(pl.multiple_of(j * T, T), T)
            for h in range(N_HEADS):
                z_refs[slot][h] = _dot(k_ref[rows, (h // 2) * LANES:(h // 2 + 1) * LANES],
                                       qm[h])

        def apply_weights(j):
            vt_all = vt_ref[j]
            for h in range(N_HEADS):
                acc_ref[h] = acc_ref[h] + _dot(vt_all[h * HEAD_DIM:(h + 1) * HEAD_DIM, :], w_ref[h])

        def absorb(j, slot, diag, staged_next, applied_prev):
            z_ref = z_refs[slot]
            softplus = []
            for h in range(N_HEADS):
                z = z_ref[h]
                sp = jnp.maximum(z, jnp.log2(1.0 + jnp.exp2(jnp.minimum(z, 64.0))))
                if diag:
                    sp = jnp.where(strict, sp, 0.0)
                softplus.append(sp.astype(BF16))
            if staged_next is not None:
                stage(staged_next, 1 - slot)
            if applied_prev is not None:
                apply_weights(applied_prev)
            sums = [_dot(u, sp) for sp in softplus]
            for h in range(N_HEADS):
                a = jnp.exp2(z_ref[h] - (sums[h] + carry_ref[h]))
                if diag:
                    a = jnp.where(strict, a, 0.0)
                w_ref[h] = a.astype(BF16)
                carry_ref[h] = carry_ref[h] + sums[h][0:1, :]

        carry_ref[...] = jnp.zeros(carry_ref.shape, F32)
        acc_ref[...] = jnp.zeros(acc_ref.shape, F32)
        stage(i, 0)
        absorb(i, 0, True, jnp.maximum(i - 1, 0), None)

        def live():
            c = jnp.minimum(jnp.minimum(carry_ref[0], carry_ref[1]),
                            jnp.minimum(carry_ref[2], carry_ref[3]))
            return jnp.min(c) < STICK_DEAD

        def body(t, last):
            j = i - 1 - 2 * t
            go = live()

            @pl.when(go)
            def _():
                absorb(j, 1, False, j - 1, j + 1)
                absorb(j - 1, 0, False, jnp.maximum(j - 2, 0), j)

            return jnp.where(go, j - 1, last)

        last = lax.fori_loop(0, i // 2, body, i)
        tail = jnp.logical_and(i % 2 == 1, live())

        @pl.when(tail)
        def _():
            absorb(0, 1, False, None, 1)

        apply_weights(jnp.where(tail, 0, last))
        q_rows = pl.ds(pl.multiple_of(i * T, T), T)
        for pair in range(2):
            o_t = jnp.concatenate([acc_ref[2 * pair], acc_ref[2 * pair + 1]], axis=0)
            o_ref[q_rows, pair * LANES:(pair + 1) * LANES] = o_t.T.astype(BF16)
        return carry

    lax.fori_loop(0, q_all_ref.shape[0], query_block, 0)


def _stick_breaking(qdt, proj3, vdt, B, S):
    u = np.triu(np.ones((TB, TB), np.float32))
    return pl.pallas_call(
        _stick_kernel,
        grid=(B,),
        in_specs=[pl.BlockSpec((None,) + qdt.shape[1:], lambda b: (b, 0, 0, 0)),
                  pl.BlockSpec((None, S, BLK), lambda b: (b, 0, PB_DK)),
                  pl.BlockSpec((None,) + vdt.shape[1:], lambda b: (b, 0, 0, 0)),
                  pl.BlockSpec(u.shape, lambda b: (0, 0))],
        out_specs=pl.BlockSpec((None, S, BLK), lambda b: (b, 0, 0)),
        out_shape=jax.ShapeDtypeStruct((B, S, BLK), BF16),
        scratch_shapes=[pltpu.VMEM((N_HEADS, TB, TB), F32),
                        pltpu.VMEM((N_HEADS, TB, TB), F32),
                        pltpu.VMEM((N_HEADS, TB, TB), BF16),
                        pltpu.VMEM((N_HEADS, 1, TB), F32),
                        pltpu.VMEM((N_HEADS, HEAD_DIM, TB), F32)],
        compiler_params=pltpu.CompilerParams(dimension_semantics=("arbitrary",),
                                             vmem_limit_bytes=VMEM_LIMIT),
        name="stick_breaking",
    )(qdt, proj3, vdt, jnp.asarray(u, BF16))


def _outproj_kernel(x_ref, oa, ob, oc, os_, ow, od,
                    ga, gb, gl, gc, gd, expand_ref, w_ref, g_ref, out_ref):
    f = lambda r: r[...].astype(F32)
    gates = _dot(gl[...], expand_ref[...])
    o_c = (gates[:, 0:BLK] * f(oc) + gates[:, BLK:2 * BLK] * f(os_)
           + gates[:, 2 * BLK:3 * BLK] * f(ow))
    mixed = [f(oa) * f(ga), f(ob) * f(gb), o_c * f(gc), f(od) * f(gd)]
    y = _dot(mixed[0].astype(BF16), w_ref[0:BLK, :])
    for g in range(1, 4):
        y = y + _dot(mixed[g].astype(BF16), w_ref[g * BLK:(g + 1) * BLK, :])
    ms = jnp.mean(y * y, axis=-1, keepdims=True)
    out_ref[...] = x_ref[...] + (y * lax.rsqrt(ms + EPS)) * g_ref[...]


def _out_projection(x2, mixer_outs, proj, w_out, g_post):
    rows = x2.shape[0]
    tm = OUT_TILE
    blk = lambda c: pl.BlockSpec((tm, BLK), lambda i: (i, c))
    expand = np.zeros((LANES, N_BRANCH * BLK), np.float32)
    for h in range(N_HEADS):
        for r in range(N_BRANCH):
            expand[N_BRANCH * h + r, r * BLK + h * HEAD_DIM:r * BLK + (h + 1) * HEAD_DIM] = 1.0
    gate_specs = [blk(PB_AG), blk(PB_BG),
                  pl.BlockSpec((tm, LANES), lambda i: (i, 2 * PB_C1 + 1)), blk(PB_CG), blk(PB_DG)]
    return pl.pallas_call(
        _outproj_kernel,
        grid=(rows // tm,),
        in_specs=[pl.BlockSpec((tm, D_MODEL), lambda i: (i, 0))]
                 + [blk(0)] * len(mixer_outs) + gate_specs
                 + [pl.BlockSpec(expand.shape, lambda i: (0, 0)),
                    pl.BlockSpec((D_MODEL, D_MODEL), lambda i: (0, 0)),
                    pl.BlockSpec((1, D_MODEL), lambda i: (0, 0))],
        out_specs=pl.BlockSpec((tm, D_MODEL), lambda i: (i, 0)),
        out_shape=jax.ShapeDtypeStruct((rows, D_MODEL), F32),
        compiler_params=pltpu.CompilerParams(dimension_semantics=("arbitrary",),
                                             vmem_limit_bytes=VMEM_LIMIT),
        name="out_projection",
    )(x2, *mixer_outs, *([proj] * len(gate_specs)), jnp.asarray(expand, BF16),
      w_out.astype(BF16), g_post.reshape(1, D_MODEL))


def _layer(x2, tabs, B, S, layer, w_rows, w_t, w_out, g_pre, g_post, mla_g_q, mla_g_kv,
           mla_w_uq, mla_w_ukv, nsa_pos_k, nsa_pos_v, nsa_k_w1, nsa_k_w2, nsa_v_w1, nsa_v_w2):
    rows = B * S
    aqkv, aqkv4, aqkv16, proj, vct, vdt, qct, qdt = _in_projection(
        x2, g_pre, w_rows, w_t, layer, tabs, B, S)
    proj3 = proj.reshape(B, S, N_PBLK * BLK)

    o_a = _dilated(aqkv, aqkv4, aqkv16, B, S)

    qnt, qrt, kn, vbt = _mla_up(proj, mla_g_q, mla_g_kv, mla_w_uq, mla_w_ukv, tabs, B, S)
    o_b = _flash_t(qnt, kn.reshape(B, S, BLK), 0, BLK, vbt, B=B, S=S,
                   aux=("per_head", qrt, proj3, 2 * PB_BKV + 1), name="mla_attention")

    kc2, vc2 = _compress(proj3, nsa_pos_k, nsa_pos_v, nsa_k_w1, nsa_k_w2, nsa_v_w1, nsa_v_w2, B, S)
    o_cmp, nsel = _cmp_topk(proj3, kc2, vc2, B, S)
    penalty = np.zeros((S, LANES), np.float32)
    penalty[np.arange(S), np.arange(S) // NSA_SEL_LEN] = MASK_PENALTY
    o_slc = _flash_t(qct, proj3, 2 * PB_C0 + 1, LANES, vct, B=B, S=S, vrow0=0,
                     aux=("shared", nsel, jnp.asarray(penalty, BF16), 0), name="nsa_selected")
    o_win = _flash_t(qct, proj3, 2 * PB_C1, LANES, vct, B=B, S=S, vrow0=HEAD_DIM,
                     window=NSA_WINDOW - 1, name="nsa_window")

    o_d = _stick_breaking(qdt, proj3, vdt, B, S)

    flat = lambda t: t.reshape(rows, BLK)
    outs = [flat(o) for o in (o_a, o_b, o_cmp, o_slc, o_win, o_d)]
    return _out_projection(x2, outs, proj, w_out, g_post)


def kernel(x, positions, w_in, w_out, g_pre, g_post, mla_g_q, mla_g_kv, mla_w_uq, mla_w_ukv,
           nsa_pos_k, nsa_pos_v, nsa_k_w1, nsa_k_w2, nsa_v_w1, nsa_v_w2):
    B, S, D = x.shape
    assert D == D_MODEL and S % 1024 == 0
    x2 = x.reshape(B * S, D)
    tabs = _rope_tables(positions)
    w_rows, w_t = _pack_w_in(w_in)
    for l in range(w_in.shape[0]):
        x2 = _layer(x2, tabs, B, S, l, w_rows, w_t, w_out[l], g_pre[l], g_post[l],
                    mla_g_q[l], mla_g_kv[l], mla_w_uq[l], mla_w_ukv[l],
                    nsa_pos_k[l], nsa_pos_v[l], nsa_k_w1[l], nsa_k_w2[l],
                    nsa_v_w1[l], nsa_v_w2[l])
    return x2.reshape(B, S, D)
```

```python
import functools
import math

import numpy as np
import jax
import jax.numpy as jnp
from jax import lax
from jax.experimental import pallas as pl
from jax.experimental.pallas import tpu as pltpu

D_MODEL = 1024
HEAD_DIM = 64
N_HEADS = 4
ROPE_THETA = 10000.0
EPS = 1e-6
NEG = -1e30
LOG2E = math.log2(math.e)
MASK_PENALTY = -(2.0 ** 100)
BIG = 1e9
STICK_DEAD = 160.0

MLA_Q_RANK = 256
MLA_KV_RANK = 128
MLA_NOPE = 64
MLA_ROPE = 32
MLA_V = 64

NSA_CMP_LEN = 32
NSA_CMP_STRIDE = 16
NSA_CMP_HID = 256
NSA_SEL_LEN = 64
NSA_N_SEL = 16
NSA_WINDOW = 512

LANES = 128
BLK = 256
IN_TILE = 1024
OUT_TILE = 1024
TILE = 512
MLA_UP_TILE = 2048
ROPE_TILE = 2048
CMP_TILE = 2048
PACK_COLS = 256
CM_TILE = 256
DILATIONS = (1, 4, 16)
DIL_BLOCK = 128
LSE_PARTS = 2
TB = 256
ONES_ROWS = 16
VMEM_LIMIT = 48 * 1024 * 1024

F32 = jnp.float32
BF16 = jnp.bfloat16

(PB_AG, PB_BCQ, PB_BKV, PB_BG, PB_CQ, PB_C0, PB_C1, PB_CG, PB_DQ, PB_DK, PB_DG) = range(11)
N_PBLK = 11
N_BRANCH = 3
ROPE_NONE, ROPE_64, ROPE_32, ROPE_64_LO, ACT_SILU, ACT_SIGMOID = range(6)


def _dot_nt(a, b):
    return lax.dot_general(a, b, (((1,), (1,)), ((), ())), preferred_element_type=F32)


def _dot(a, b):
    return jnp.dot(a, b, preferred_element_type=F32)


def _rope_half(y, cos, sin_signed, group):
    half = group // 2
    lane = lax.broadcasted_iota(jnp.int32, y.shape, 1)
    first = (lane & (group - 1)) < half
    partner = jnp.where(first, pltpu.roll(y, LANES - half, 1), pltpu.roll(y, half, 1))
    return y * cos + partner * sin_signed


def _rope_table_kernel(pos_ref, inv_ref, cos_ref, sin_ref):
    ang = inv_ref[...] * pos_ref[...].astype(F32)
    cos_ref[...] = jnp.cos(ang)
    sin_ref[...] = jnp.sin(ang)


def _rope_tables(positions):
    rows = positions.size
    tm = ROPE_TILE
    inv =np.concatenate([ROPE_THETA ** (-np.arange(n, dtype=np.float64) / n)
                          for n in (HEAD_DIM // 2, MLA_ROPE // 2)]).astype(np.float32)[:, None]
    out = jax.ShapeDtypeStruct((inv.shape[0], rows), F32)
    tab_spec = pl.BlockSpec((inv.shape[0], tm), lambda i: (0, i))
    return pl.pallas_call(
        _rope_table_kernel,
        grid=(rows // tm,),
        in_specs=[pl.BlockSpec((1, tm), lambda i: (0, i)),
                  pl.BlockSpec(inv.shape, lambda i: (0, 0))],
        out_specs=[tab_spec] * 2,
        out_shape=[out] * 2,
        name="rope_tables",
    )(positions.reshape(1, rows), jnp.asarray(inv))


def _token_major_rope(cos_ref, sin_ref, group):
    n = group // 2
    lo = 0 if group == HEAD_DIM else HEAD_DIM // 2
    c = cos_ref[lo:lo + n, :]
    s = sin_ref[lo:lo + n, :]
    reps = LANES // n
    cos = jnp.concatenate([c] * reps, axis=0).T
    sin = jnp.concatenate([-s if k % 2 == 0 else s for k in range(reps)], axis=0).T
    return cos, sin


def _inproj_kernel(x_ref, g_ref, w_ref, wt_ref, perm_ref, cos_ref, sin_ref,
                   aqkv_ref, aqkv4_ref, aqkv16_ref, proj_ref, vct_ref, vdt_ref, qct_ref, qdt_ref,
                   *, plan_a, plan_p):
    x = x_ref[...]
    tm = x.shape[0]
    ms = jnp.mean(x * x, axis=-1, keepdims=True)
    h = ((x * lax.rsqrt(ms + EPS)) * g_ref[...]).astype(BF16)
    lane = lax.broadcasted_iota(jnp.int32, (tm, LANES), 1)
    c64, s64 = _token_major_rope(cos_ref, sin_ref, HEAD_DIM)
    c32, s32 = _token_major_rope(cos_ref, sin_ref, MLA_ROPE)

    def epilogue(y, ops, scale):
        halves = []
        for hf, op in enumerate(ops):
            yh = y[:, hf * LANES:(hf + 1) * LANES]
            if op == ROPE_64:
                yh = _rope_half(yh, c64, s64, 64)
            elif op == ROPE_32:
                yh = _rope_half(yh, c32, s32, 32)
            elif op == ROPE_64_LO:
                yh = jnp.where(lane < 64, _rope_half(yh, c64, s64, 64), yh)
            elif op == ACT_SILU:
                yh = yh * (1.0 / (1.0 + jnp.exp(-yh)))
            elif op == ACT_SIGMOID:
                yh = 1.0 / (1.0 + jnp.exp(-yh))
            if scale != 1.0:
                yh = yh * scale
            halves.append(yh)
        return jnp.concatenate(halves, axis=1)

    feature_major = {PB_CQ: qct_ref, PB_DQ: qdt_ref}
    col = 0
    for out_ref, plan in ((aqkv_ref, plan_a), (proj_ref, plan_p)):
        for b, (ops, scale) in enumerate(plan):
            y = epilogue(_dot(h, w_ref[:, col:col + BLK]), ops, scale)
            out_ref[:, b * BLK:(b + 1) * BLK] = y.astype(BF16)
            if out_ref is proj_ref and b in feature_major:
                y_t = y.T
                for blk in range(tm // TB):
                    feature_major[b][blk] = y_t[:, blk * TB:(blk + 1) * TB].astype(BF16)
            col += BLK

    for t in range(tm // CM_TILE):
        tile = slice(CM_TILE * t, CM_TILE * (t + 1))
        a_rows = aqkv_ref[tile, :]
        aqkv4_ref[tile, :] = _dot(perm_ref[0], a_rows).astype(BF16)
        aqkv16_ref[tile, :] = _dot(perm_ref[1], a_rows).astype(BF16)

    vt = _dot_nt(wt_ref[...], h)
    for blk in range(tm // TB):
        cols = slice(blk * TB, (blk + 1) * TB)
        vct_ref[blk] = vt[0:LANES, cols].astype(BF16)
        vdt_ref[blk] = vt[LANES:LANES + BLK, cols].astype(BF16)


_PLAN_A = (((ROPE_64, ROPE_64), LOG2E * HEAD_DIM ** -0.5),
           ((ROPE_64, ROPE_64), 1.0),
           ((ROPE_NONE, ROPE_NONE), 1.0))
_PLAN_P = (
    ((ACT_SILU, ACT_SILU), 1.0),
    ((ROPE_NONE, ROPE_NONE), 1.0),
    ((ROPE_NONE, ROPE_32), 1.0),
    ((ACT_SILU, ACT_SILU), 1.0),
    ((ROPE_64, ROPE_64), LOG2E * HEAD_DIM ** -0.5),
    ((ROPE_64_LO, ROPE_64), 1.0),
    ((ROPE_64, ACT_SIGMOID), 1.0),
    ((ACT_SILU, ACT_SILU), 1.0),
    ((ROPE_NONE, ROPE_NONE), LOG2E * HEAD_DIM ** -0.5),
    ((ROPE_NONE, ROPE_NONE), 1.0),
    ((ACT_SILU, ACT_SILU), 1.0),
)
assert len(_PLAN_P) == N_PBLK


_W_IN_SEGMENTS = (("aq", 256), ("ak", 256), ("av", 256), ("ag", 256),
                  ("bcq", 256), ("bckv", 128), ("bkr", 32), ("bg", 256),
                  ("cq", 256), ("ckc", 64), ("cvc", 64), ("cks", 64), ("cvs", 64),
                  ("ckw", 64), ("cvw", 64), ("cgl", 12), ("cg", 256),
                  ("dq", 256), ("dk", 256), ("dv", 256), ("dg", 256))
W_ROWS_COLS = (3 + N_PBLK) * BLK
W_T_ROWS = LANES + BLK


def _pack_w_kernel(wt_ref, rows_ref, t_ref):
    wt = wt_ref[...].astype(F32)
    o = 0
    seg = {}
    for name, width in _W_IN_SEGMENTS:
        seg[name] = wt[o:o + width, :]
        o += width
    gl_pad = jnp.zeros((LANES - N_HEADS * N_BRANCH, wt.shape[1]), wt.dtype)
    rows = [seg["aq"], seg["ak"], seg["av"],
            seg["ag"], seg["bcq"],
            seg["bckv"], seg["bkr"], seg["bkr"], seg["bkr"], seg["bkr"],
            seg["bg"], seg["cq"],
            seg["ckc"], seg["cvc"], seg["cks"], seg["cks"],
            seg["ckw"], seg["ckw"], seg["cgl"], gl_pad,
            seg["cg"], seg["dq"], seg["dk"], seg["dg"]]
    rows_ref[...] = jnp.concatenate(rows, axis=0).T.astype(BF16)
    t_ref[...] = jnp.concatenate([seg["cvs"], seg["cvw"], seg["dv"]], axis=0).astype(BF16)


def _pack_w_in(w_in):
    depth, d_model, d_in = w_in.shape
    assert d_model == D_MODEL and d_in == sum(width for _, width in _W_IN_SEGMENTS)
    tc = PACK_COLS
    return pl.pallas_call(
        _pack_w_kernel,
        grid=(depth, D_MODEL // tc),
        in_specs=[pl.BlockSpec((None, d_in, tc), lambda l, i: (l, 0, i))],
        out_specs=[pl.BlockSpec((None, tc, W_ROWS_COLS), lambda l, i: (l, i, 0)),
                   pl.BlockSpec((None, W_T_ROWS, tc), lambda l, i: (l, 0, i))],
        out_shape=[jax.ShapeDtypeStruct((depth, D_MODEL, W_ROWS_COLS), BF16),
                   jax.ShapeDtypeStruct((depth, W_T_ROWS, D_MODEL), BF16)],
        compiler_params=pltpu.CompilerParams(dimension_semantics=("arbitrary", "arbitrary"),
                                             vmem_limit_bytes=VMEM_LIMIT),
        name="pack_w_in",
    )(jnp.transpose(w_in, (0, 2, 1)).astype(BF16))


def _class_major_perm(dil, tile):
    n = tile // dil
    p = np.zeros((tile, tile), np.float32)
    pos = np.arange(tile)
    p[(pos % dil) * n + pos // dil, pos] = 1.0
    return p


def _in_projection(x2, g_pre, w_rows, w_t, layer, tabs, B, S):
    rows = x2.shape[0]
    tm = IN_TILE
    per_b = S // tm
    nkb = tm // TB
    perm = jnp.asarray(np.stack([_class_major_perm(d, CM_TILE) for d in DILATIONS[1:]]), BF16)
    tab_spec = pl.BlockSpec((tabs[0].shape[0], tm), lambda i: (0, i))
    vt_spec = lambda r: pl.BlockSpec((None, nkb, r, TB), lambda i: (i // per_b, i % per_b, 0, 0))
    a_spec = pl.BlockSpec((tm, 3 * BLK), lambda i: (i, 0))
    a_shape = jax.ShapeDtypeStruct((rows, 3 * BLK), BF16)
    return pl.pallas_call(
        functools.partial(_inproj_kernel, plan_a=_PLAN_A, plan_p=_PLAN_P),
        grid=(rows // tm,),
        in_specs=[pl.BlockSpec((tm, D_MODEL), lambda i: (i, 0)),
                  pl.BlockSpec((1, D_MODEL), lambda i: (0, 0)),
                  pl.BlockSpec((None,) + w_rows.shape[1:], lambda i: (layer, 0, 0)),
                  pl.BlockSpec((None,) + w_t.shape[1:], lambda i: (layer, 0, 0)),
                  pl.BlockSpec(perm.shape, lambda i: (0, 0, 0))] + [tab_spec] * 2,
        out_specs=[a_spec, a_spec, a_spec,
                   pl.BlockSpec((tm, N_PBLK * BLK), lambda i: (i, 0)),
                   vt_spec(LANES), vt_spec(BLK), vt_spec(BLK), vt_spec(BLK)],
        out_shape=[a_shape, a_shape, a_shape,
                   jax.ShapeDtypeStruct((rows, N_PBLK * BLK), BF16),
                   jax.ShapeDtypeStruct((B, S // TB, LANES, TB), BF16)]
                  + [jax.ShapeDtypeStruct((B, S // TB, BLK, TB), BF16)] * 3,
        compiler_params=pltpu.CompilerParams(dimension_semantics=("arbitrary",),
                                             vmem_limit_bytes=VMEM_LIMIT),
        name="in_projection",
    )(x2, g_pre.reshape(1, D_MODEL), w_rows, w_t, perm, *tabs)


def _head_masks(shape):
    lane = lax.broadcasted_iota(jnp.int32, shape, 1)
    return lane < HEAD_DIM


def _masked_heads(q_ref, lo_mask):
    out = []
    for h in range(N_HEADS):
        pair, e = divmod(h, 2)
        qp = q_ref[:, pair * LANES:(pair + 1) * LANES]
        keep = lo_mask if e == 0 else jnp.logical_not(lo_mask)
        out.append(jnp.where(keep, qp, jnp.zeros_like(qp)))
    return out


def _dilated_kernel(an_ref, a4_ref, a16_ref, unperm_ref, expand_ref, o_ref, res_ref, s_ref):
    T = DIL_BLOCK
    S = an_ref.shape[0]
    n_tiles = S // CM_TILE
    lo = _head_masks((T, LANES))
    lane = lax.broadcasted_iota(jnp.int32, (T, LANES), 1)
    dist2 = (lax.broadcasted_iota(jnp.int32, (T, 2 * T), 0)
             - lax.broadcasted_iota(jnp.int32, (T, 2 * T), 1))
    causal1 = (lax.broadcasted_iota(jnp.int32, (T, T), 1)
               <= lax.broadcasted_iota(jnp.int32, (T, T), 0))

    def scores(q, k):
        qm = _masked_heads(q, lo)
        out = []
        for pair in range(N_HEADS // 2):
            both = _dot_nt(jnp.concatenate([qm[2 * pair], qm[2 * pair + 1]], axis=0),
                           k[:, pair * LANES:(pair + 1) * LANES])
            out += [both[:T], both[T:]]
        return tuple(out)

    def finish(n_keys, v, mask, stage_next):
        probs = []
        for h in range(N_HEADS):
            s = jnp.where(mask, s_ref[h, :, 0:n_keys], NEG)
            m = jnp.max(s, axis=-1, keepdims=True)
            p = jnp.exp2(s - m)
            probs.append((m, jnp.sum(p, axis=-1, keepdims=True), p.astype(BF16)))
        stage_next()
        pv = []
        for pair in range(N_HEADS // 2):
            both = _dot(jnp.concatenate([probs[2 * pair][2], probs[2 * pair + 1][2]], axis=0),
                        v[:, pair * LANES:(pair + 1) * LANES])
            pv += [both[:T], both[T:]]
        outs = []
        top = jnp.zeros((T, LANES), F32)
        den = jnp.ones((T, LANES), F32)
        for h in range(N_HEADS):
            m, l, _ = probs[h]
            outs.append(pv[h] * (1.0 / l))
            mine = (lane & (N_HEADS - 1)) == h
            top = jnp.where(mine, m, top)
            den = jnp.where(mine, l, den)
        lse = top + jnp.log2(den)
        hi = lse.astype(BF16).astype(F32)
        parts = jnp.where(lane < N_HEADS, hi,
                          jnp.where(lane < LSE_PARTS * N_HEADS, lse - hi, 0.0))
        return jnp.concatenate(
            [jnp.where(lo, outs[0], outs[1]), jnp.where(lo, outs[2], outs[3]), parts],
            axis=1).astype(BF16)

    def band_mask(delta):
        d = dist2 + delta
        return jnp.logical_and(d >= 0, d <= DIL_BLOCK)

    def run(n_units, n_keys, load_qk, load_v, store, mask_of):
        def stage(u):
            for h, s in enumerate(scores(*load_qk(u))):
                s_ref[h, :, 0:n_keys] = s

        def body(u, carry):
            nxt = jnp.minimum(u + 1, n_units - 1)
            store(u, finish(n_keys, load_v(u), mask_of(u), lambda: stage(nxt)))
            return carry

        stage(0)
        lax.fori_loop(0, n_units, body, 0)

    def rows(start, size):
        return pl.ds(pl.multiple_of(start, 16), size)

    def k0_1(u):
        return T * jnp.maximum(u - 1, 0)

    def store1(u, r):
        res_ref[0, rows(T * u, T), :] = r

    run(S // T, 2 * T,
        lambda u: (an_ref[rows(T * u, T), 0:BLK], an_ref[rows(k0_1(u), 2 * T), BLK:2 * BLK]),
        lambda u: an_ref[rows(k0_1(u), 2 * T), 2 * BLK:3 * BLK],
        store1,
        lambda u: band_mask(T * u - k0_1(u)))

    per4 = CM_TILE // 4
    nq4 = S // 4 // T

    def pieces4(u, blk):
        first_tile = blk * (T // per4)
        return [rows(CM_TILE * (first_tile + k) + per4 * (u // nq4), per4)
                for k in range(T // per4)]

    def kb0_4(u):
        return jnp.maximum(u % nq4 - 1, 0)

    def load4(u, blks, cols):
        return jnp.concatenate([a4_ref[p, cols] for blk in blks for p in pieces4(u, blk)], axis=0)

    def store4(u, r):
        for k, p in enumerate(pieces4(u, u % nq4)):
            res_ref[1, p, :] = r[per4 * k:per4 * (k + 1), :]

    run(4 * nq4, 2 * T,
        lambda u: (load4(u, [u % nq4], slice(0, BLK)),
                   load4(u, [kb0_4(u), kb0_4(u) + 1], slice(BLK, 2 * BLK))),
        lambda u: load4(u, [kb0_4(u), kb0_4(u) + 1], slice(2 * BLK, 3 * BLK)),
        store4,
        lambda u: band_mask(T * (u % nq4 - kb0_4(u))))

    per16 = CM_TILE // 16

    def load16(c, cols):
        return jnp.concatenate([a16_ref[rows(CM_TILE * t + per16 * c, per16), cols]
                                for t in range(n_tiles)], axis=0)

    def store16(c, r):
        for t in range(n_tiles):
            res_ref[2, rows(CM_TILE * t + per16 * c, per16), :] = r[per16 * t:per16 * (t + 1), :]

    run(16, T,
        lambda c: (load16(c, slice(0, BLK)), load16(c, slice(BLK, 2 * BLK))),
        lambda c: load16(c, slice(2 * BLK, 3 * BLK)),
        store16,
        lambda c: causal1)

    expand = expand_ref[...]
    for t in range(n_tiles):
        tile = slice(CM_TILE * t, CM_TILE * (t + 1))
        nat = [res_ref[0, tile, :].astype(F32),
               _dot(unperm_ref[0], res_ref[1, tile, :]),
               _dot(unperm_ref[1], res_ref[2, tile, :])]
        lse = [_dot(r[:, BLK:].astype(BF16), expand) for r in nat]
        top = jnp.maximum(lse[0], jnp.maximum(lse[1], lse[2]))
        w = [jnp.exp2(l - top) for l in lse]
        mixed = (w[0] * nat[0][:, :BLK] + w[1] * nat[1][:, :BLK] + w[2] * nat[2][:, :BLK])
        o_ref[tile, :] = (mixed / (w[0] + w[1] + w[2])).astype(BF16)


def _dilated(aqkv, aqkv4, aqkv16, B, S):
    assert S == 16 * DIL_BLOCK and S % CM_TILE == 0 and DIL_BLOCK % (CM_TILE // 4) == 0
    r3 = lambda t: t.reshape(B, S, 3 * BLK)
    unperm = jnp.asarray(np.stack([_class_major_perm(d, CM_TILE).T for d in DILATIONS[1:]]), BF16)
    expand = np.zeros((LANES, BLK), np.float32)
    for h in range(N_HEADS):
        for part in range(LSE_PARTS):
            expand[N_HEADS * part + h, h * HEAD_DIM:(h + 1) * HEAD_DIM] = 1.0
    a_spec = pl.BlockSpec((None, S, 3 * BLK), lambda b: (b, 0, 0))
    return pl.pallas_call(
        _dilated_kernel,
        grid=(B,),
        in_specs=[a_spec, a_spec, a_spec,
                  pl.BlockSpec(unperm.shape, lambda b: (0, 0, 0)),
                  pl.BlockSpec(expand.shape, lambda b: (0, 0))],
        out_specs=pl.BlockSpec((None, S, BLK), lambda b: (b, 0, 0)),
        out_shape=jax.ShapeDtypeStruct((B, S, BLK), BF16),
        scratch_shapes=[pltpu.VMEM((3, S, BLK + LANES), BF16),
                        pltpu.VMEM((N_HEADS, DIL_BLOCK, 2 * DIL_BLOCK), F32)],
        compiler_params=pltpu.CompilerParams(dimension_semantics=("arbitrary",),
                                             vmem_limit_bytes=VMEM_LIMIT),
        name="dilated_mixture",
    )(r3(aqkv), r3(aqkv4), r3(aqkv16), unperm, jnp.asarray(expand, BF16))


def _masked_head_rows(qt_ref):
    out = []
    zeros = jnp.zeros((HEAD_DIM, qt_ref.shape[-1]), BF16)
    for h in range(N_HEADS):
        own = qt_ref[h * HEAD_DIM:(h + 1) * HEAD_DIM, :]
        out.append(jnp.concatenate([own, zeros] if h % 2 == 0 else [zeros, own], axis=0))
    return out


def _flash_t_kernel(*refs, window, kshared, vrow0, aux):
    it = iter(refs)
    q_all_ref = next(it)
    qx_all_ref = next(it) if aux else None
    k_ref = next(it)
    kx_ref = next(it) if aux else None
    vt_ref = next(it)
    o_ref = next(it)
    s_ref, m_ref, acc_ref = next(it), next(it), next(it)
    T = TB

    krow = lax.broadcasted_iota(jnp.int32, (T, T), 0)
    qcol = lax.broadcasted_iota(jnp.int32, (T, T), 1)
    ones = jnp.ones((ONES_ROWS, T), BF16)

    n_q = q_all_ref.shape[0]
    cross_stage = window is None
    if window is not None:
        o_full = (window - T + 1) // T
        o_none = (window + T - 1) // T + 1
        assert o_none - o_full == 2

    def query_weights(i):
        qcat = _masked_head_rows(q_all_ref.at[i])
        if aux == "per_head":
            group = LANES // N_HEADS
            qcat = [jnp.concatenate(
                [qcat[h]] + [qx_all_ref[i, g * group:(g + 1) * group, :] if g == h
                             else jnp.zeros((group, T), BF16) for g in range(N_HEADS)], axis=0)
                for h in range(N_HEADS)]
        elif aux == "shared":
            qx = qx_all_ref[i]
            qcat = [jnp.concatenate([qcat[h], qx], axis=0) for h in range(N_HEADS)]
        return qcat

    def first_full_block(i):
        return 0 if window is None else jnp.maximum(i - o_full, 0)

    def scores(j, qcat):
        rows = pl.ds(pl.multiple_of(j * T, T), T)
        out = []
        for h in range(N_HEADS):
            pair = h // 2
            cols = slice(0, LANES) if kshared else slice(pair * LANES, (pair + 1) * LANES)
            kb = k_ref[rows, cols]
            if aux:
                kb = jnp.concatenate([kb, kx_ref[rows, :]], axis=1)
            out.append(_dot(kb, qcat[h]))
        return out

    def stage(j, qcat):
        for h, s in enumerate(scores(j, qcat)):
            s_ref[h] = s

    def band_block(i):
        o = o_full + 1
        j = i - o
        limit = jnp.where(j >= 0, window - o * T, -2 * T)
        return jnp.maximum(j, 0), limit

    def stage_query_block(i):
        stage(first_full_block(i), query_weights(i))

    def query_block(i, carry):
        qcat = query_weights(i)

        def absorb(j, mask, staged_next=None, s_all=None):
            vt_all = vt_ref[j]
            probs = []
            for h in range(N_HEADS):
                s = s_ref[h] if s_all is None else s_all[h]
                if mask is not None:
                    s = jnp.where(mask, s, NEG)
                m = m_ref[h]
                m_new = jnp.maximum(m, jnp.max(s, axis=0, keepdims=True))
                alpha = jnp.exp2(m - m_new)
                p = jnp.exp2(s - m_new)
                probs.append((m_new, alpha, p.astype(BF16)))
            if staged_next is not None:
                staged_next()
            for h in range(N_HEADS):
                m_new, alpha, p = probs[h]
                r0 = vrow0 if kshared else h * HEAD_DIM
                lhs = jnp.concatenate([vt_all[r0:r0 + HEAD_DIM, :], ones], axis=0)
                acc_ref[h] = alpha * acc_ref[h] + _dot(lhs, p)
                m_ref[h] = m_new

        m_ref[...] = jnp.full(m_ref.shape, NEG, F32)
        acc_ref[...] = jnp.zeros(acc_ref.shape, F32)
        first = first_full_block(i)
        if window is not None:
            jc, limit = band_block(i)
            s_band = scores(jc, qcat)
            stage(first, qcat)
            absorb(jc, (qcol - krow) <= limit, s_all=s_band)

        def body(j, c):
            absorb(j, None, staged_next=lambda: stage(j + 1, qcat))
            return c

        lax.fori_loop(first, i, body, 0)
        if cross_stage:
            nxt = jnp.minimum(i + 1, n_q - 1)
            absorb(i, krow <= qcol, staged_next=lambda: stage_query_block(nxt))
        else:
            absorb(i, krow <= qcol)

        q_rows = pl.ds(pl.multiple_of(i * T, T), T)
        for pair in range(2):
            outs = []
            for e in range(2):
                acc = acc_ref[2 * pair + e]
                outs.append(acc[0:HEAD_DIM, :] * (1.0 / acc[HEAD_DIM:HEAD_DIM + 1, :]))
            o_ref[q_rows, pair * LANES:(pair + 1) * LANES] = (
                jnp.concatenate(outs, axis=0).T.astype(BF16))
        return carry

    if cross_stage:
        stage_query_block(0)
    lax.fori_loop(0, n_q, query_block, 0)


def _flash_t(qt, k, kcol, kw, vt, *, B, S, window=None, vrow0=0, aux=None, name):
    kshared = kw == LANES
    whole = lambda a: pl.BlockSpec((None,) + a.shape[1:], lambda b: (b,) + (0,) * (a.ndim - 1))
    in_specs = [whole(qt)]
    args = [qt]
    if aux is not None:
        kind, qx, kx, kxcol = aux
        in_specs.append(whole(qx))
        args.append(qx)
    in_specs.append(pl.BlockSpec((None, S, kw), lambda b: (b, 0, kcol)))
    args.append(k)
    if aux is not None:
        if kx.ndim == 3:
            in_specs.append(pl.BlockSpec((None, S, LANES), lambda b: (b, 0, kxcol)))
        else:
            in_specs.append(pl.BlockSpec((S, LANES), lambda b: (0, kxcol)))
        args.append(kx)
    in_specs.append(whole(vt))
    args.append(vt)
    return pl.pallas_call(
        functools.partial(_flash_t_kernel, window=window, kshared=kshared, vrow0=vrow0,
                          aux=None if aux is None else aux[0]),
        grid=(B,),
        in_specs=in_specs,
        out_specs=pl.BlockSpec((None, S, BLK), lambda b: (b, 0, 0)),
        out_shape=jax.ShapeDtypeStruct((B, S, BLK), BF16),
        scratch_shapes=[pltpu.VMEM((N_HEADS, TB, TB), F32),
                        pltpu.VMEM((N_HEADS, 1, TB), F32),
                        pltpu.VMEM((N_HEADS, HEAD_DIM + ONES_ROWS, TB), F32)],
        compiler_params=pltpu.CompilerParams(dimension_semantics=("arbitrary",),
                                             vmem_limit_bytes=VMEM_LIMIT),
        name=name,
    )(*args)


def _mla_up_kernel(cq_ref, ckv_ref, gq_ref, gkv_ref, wq_ref, wk_ref, wvt_ref, cos_ref, sin_ref,
                   qn_ref, qr_ref, kn_ref, vt_ref):
    def norm(t, g):
        t = t.astype(F32)
        ms = jnp.mean(t * t, axis=-1, keepdims=True)
        return ((t * lax.rsqrt(ms + EPS)) * g).astype(BF16)

    scale = LOG2E * (MLA_NOPE + MLA_ROPE) ** -0.5
    q_t = _dot_nt(wq_ref[...], norm(cq_ref[...], gq_ref[...])) * scale
    qn_t = q_t[:BLK, :]
    half = MLA_ROPE // 2
    cos = cos_ref[HEAD_DIM // 2:HEAD_DIM // 2 + half, :]
    sin = sin_ref[HEAD_DIM // 2:HEAD_DIM // 2 + half, :]
    rot = []
    for h in range(N_HEADS):
        x1 = q_t[BLK + h * MLA_ROPE:BLK + h * MLA_ROPE + half, :]
        x2 = q_t[BLK + h * MLA_ROPE + half:BLK + (h + 1) * MLA_ROPE, :]
        rot += [x1 * cos - x2 * sin, x1 * sin + x2 * cos]
    qr_t = jnp.concatenate(rot, axis=0)
    ckv = norm(ckv_ref[...], gkv_ref[...])
    kn_ref[...] = _dot(ckv, wk_ref[...]).astype(BF16)
    vt = _dot_nt(wvt_ref[...], ckv)
    for blk in range(vt.shape[1] // TB):
        cols = slice(blk * TB, (blk + 1) * TB)
        qn_ref[blk] = qn_t[:, cols].astype(BF16)
        qr_ref[blk] = qr_t[:, cols].astype(BF16)
        vt_ref[blk] = vt[:, cols].astype(BF16)


def _mla_up(proj, g_q, g_kv, w_uq, w_ukv, tabs, B, S):
    rows = proj.shape[0]
    tm = MLA_UP_TILE
    per_b = S // tm
    wq = w_uq.reshape(MLA_Q_RANK, N_HEADS, MLA_NOPE + MLA_ROPE)
    wq = jnp.concatenate([wq[:, :, :MLA_NOPE].reshape(MLA_Q_RANK, -1),
                          wq[:, :, MLA_NOPE:].reshape(MLA_Q_RANK, -1)], axis=1).T.astype(BF16)
    wkv = w_ukv.reshape(MLA_KV_RANK, N_HEADS, MLA_NOPE + MLA_V)
    wk = wkv[:, :, :MLA_NOPE].reshape(MLA_KV_RANK, -1).astype(BF16)
    wvt = wkv[:, :, MLA_NOPE:].reshape(MLA_KV_RANK, -1).T.astype(BF16)
    row_spec = lambda w, c: pl.BlockSpec((tm, w), lambda i: (i, c))
    full = lambda a: pl.BlockSpec(a.shape, lambda i: (0, 0))
    gq = g_q.reshape(1, -1)
    gkv = g_kv.reshape(1, -1)
    fm_shape = lambda r: jax.ShapeDtypeStruct((B, S // TB, r, TB), BF16)
    fm_spec = lambda r: pl.BlockSpec((None, tm // TB, r, TB),
                                     lambda i: (i // per_b, i % per_b, 0, 0))
    outs = [fm_shape(BLK), fm_shape(LANES), jax.ShapeDtypeStruct((rows, BLK), BF16), fm_shape(BLK)]
    return pl.pallas_call(
        _mla_up_kernel,
        grid=(rows // tm,),
        in_specs=[row_spec(BLK, PB_BCQ), row_spec(LANES, 2 * PB_BKV), full(gq), full(gkv),
                  full(wq), full(wk), full(wvt)]
                 + [pl.BlockSpec((tabs[0].shape[0], tm), lambda i: (0, i))] * 2,
        out_specs=[fm_spec(BLK), fm_spec(LANES), row_spec(BLK, 0), fm_spec(BLK)],
        out_shape=outs,
        compiler_params=pltpu.CompilerParams(dimension_semantics=("arbitrary",),
                                             vmem_limit_bytes=VMEM_LIMIT),
        name="mla_up",
    )(proj, proj, gq, gkv, wq, wk, wvt, *tabs)


def _compress_kernel(x_ref, perm_ref, pos_ref, w1_ref, w2_ref, kc_ref, vc_ref):
    n_tiles = x_ref.shape[0] // TILE
    per = TILE // NSA_CMP_STRIDE
    slabs = []
    for t in range(n_tiles):
        cm = _dot(perm_ref[...], x_ref[TILE * t:TILE * (t + 1), :])
        slabs.append(jnp.concatenate([cm[per * c:per * (c + 1), :]
                                      for c in range(NSA_CMP_STRIDE)], axis=1))
    x = jnp.concatenate(slabs, axis=0)
    half = x.shape[1]
    top = (x + pos_ref[0:1, :]).astype(BF16)
    bot = (x + pos_ref[1:2, :]).astype(BF16)
    y_top = _dot(top, w1_ref[0:half, :])
    y_bot = _dot(bot, w1_ref[half:2 * half, :])
    pre = y_top + pltpu.roll(y_bot, y_bot.shape[0] - 1, 0)
    hid = (pre * (1.0 / (1.0 + jnp.exp(-pre)))).astype(BF16)
    kc_ref[...] = _dot(hid[:, :NSA_CMP_HID], w2_ref[0]).astype(BF16)
    vc_ref[...] = _dot(hid[:, NSA_CMP_HID:], w2_ref[1]).astype(BF16)


def _compress(proj3, pos_k, pos_v, kw1, kw2, vw1, vw2, B, S):
    assert S % TILE == 0 and NSA_CMP_LEN == 2 * NSA_CMP_STRIDE
    n16 = S // NSA_CMP_STRIDE
    perm = jnp.asarray(_class_major_perm(NSA_CMP_STRIDE, TILE), BF16)
    pos = jnp.concatenate([pos_k, pos_v], axis=1).reshape(2, NSA_CMP_STRIDE * LANES)
    w1 = jnp.zeros((NSA_CMP_LEN, LANES, 2 * NSA_CMP_HID), F32)
    w1 = w1.at[:, :HEAD_DIM, :NSA_CMP_HID].set(kw1.reshape(NSA_CMP_LEN, HEAD_DIM, NSA_CMP_HID))
    w1 = w1.at[:, HEAD_DIM:, NSA_CMP_HID:].set(vw1.reshape(NSA_CMP_LEN, HEAD_DIM, NSA_CMP_HID))
    w1 = w1.reshape(NSA_CMP_LEN * LANES, 2 * NSA_CMP_HID).astype(BF16)
    w2 = jnp.stack([jnp.tile(kw2, (1, 2)), jnp.tile(vw2, (1, 2))]).astype(BF16)
    out = jax.ShapeDtypeStruct((B, n16, LANES), BF16)
    return pl.pallas_call(
        _compress_kernel,
        grid=(B,),
        in_specs=[pl.BlockSpec((None, S, LANES), lambda b: (b, 0, 2 * PB_C0)),
                  pl.BlockSpec(perm.shape, lambda b: (0, 0)),
                  pl.BlockSpec(pos.shape, lambda b: (0, 0)),
                  pl.BlockSpec(w1.shape, lambda b: (0, 0)),
                  pl.BlockSpec(w2.shape, lambda b: (0, 0, 0))],
        out_specs=[pl.BlockSpec((None, n16, LANES), lambda b: (b, 0, 0))] * 2,
        out_shape=[out, out],
        compiler_params=pltpu.CompilerParams(dimension_semantics=("arbitrary",),
                                             vmem_limit_bytes=VMEM_LIMIT),
        name="nsa_compress",
    )(proj3, perm, pos, w1, w2)


def _cmp_topk_kernel(q_ref, kc_ref, vc_ref, ovt_ref, o_ref, nsel_ref, *, T, n_cmp, n_slc):
    i = pl.program_id(1)
    t0 = i * T
    lane = lax.broadcasted_iota(jnp.int32, (T, LANES), 1)
    tok = t0 + lax.broadcasted_iota(jnp.int32, (T, LANES), 0)
    ok = jnp.logical_and(lane * NSA_CMP_STRIDE + (NSA_CMP_LEN - 1) <= tok, lane < n_cmp)
    lo = lane < HEAD_DIM
    qm = _masked_heads(q_ref, lo)
    kc = kc_ref[...]
    vc = vc_ref[...]
    psum = jnp.zeros((T, LANES), F32)
    outs = []
    for h in range(N_HEADS):
        s = jnp.where(ok, _dot_nt(qm[h], kc), NEG)
        mx = jnp.max(s, axis=-1, keepdims=True)
        e = jnp.where(ok, jnp.exp2(s - mx), 0.0)
        den = jnp.sum(e, axis=-1, keepdims=True)
        p = e * (1.0 / jnp.maximum(den, 1e-30))
        psum = psum + p
        outs.append(_dot(p.astype(BF16), vc))
    for pair in range(2):
        o_ref[:, pair * LANES:(pair + 1) * LANES] = jnp.where(
            lo, outs[2 * pair], outs[2 * pair + 1]).astype(BF16)

    p_hi = psum.astype(BF16)
    p_lo = (psum - p_hi.astype(F32)).astype(BF16)
    ovt = ovt_ref[...]
    imp = _dot_nt(ovt, p_hi) + _dot_nt(ovt, p_lo)
    jj = lax.broadcasted_iota(jnp.int32, (n_slc, T), 0)
    bt = (t0 + lax.broadcasted_iota(jnp.int32, (n_slc, T), 1)) >> 6
    forced = jnp.logical_or(jj == 0, jnp.logical_or(jj == bt, jj == bt - 1))
    val = jnp.where(forced, BIG, jnp.where(jj > bt, -BIG, imp))
    sel_rows = []
    for j in range(n_slc):
        vj = val[j:j + 1, :]
        beats = jnp.logical_or(val > vj, jnp.logical_and(val == vj, jj < j))
        rank = jnp.sum(beats.astype(F32), axis=0, keepdims=True)
        sel_rows.append((rank >= NSA_N_SEL).astype(F32))
    sel_rows.append(jnp.zeros((LANES - n_slc, T), F32))
    nsel = jnp.concatenate(sel_rows, axis=0).astype(BF16)
    for blk in range(T // TB):
        nsel_ref[blk] = nsel[:, blk * TB:(blk + 1) * TB]


def _cmp_topk(proj3, kc2, vc2, B, S):
    T = CMP_TILE
    n_cmp = (S - NSA_CMP_LEN) // NSA_CMP_STRIDE + 1
    n_slc = S // NSA_SEL_LEN
    assert NSA_SEL_LEN == 64 and n_slc <= LANES and n_cmp <= LANES
    ci = np.arange(LANES)[None, :] * NSA_CMP_STRIDE
    sj = np.arange(n_slc)[:, None] * NSA_SEL_LEN
    ovt = ((ci < sj + NSA_SEL_LEN) & (ci + NSA_CMP_LEN > sj) & (np.arange(LANES)[None, :] < n_cmp))
    ovt = jnp.asarray(ovt.astype(np.float32), BF16)
    return pl.pallas_call(
        functools.partial(_cmp_topk_kernel, T=T, n_cmp=n_cmp, n_slc=n_slc),
        grid=(B, S // T),
        in_specs=[pl.BlockSpec((None, T, BLK), lambda b, i: (b, i, PB_CQ)),
                  pl.BlockSpec((None, LANES, LANES), lambda b, i: (b, 0, 0)),
                  pl.BlockSpec((None, LANES, LANES), lambda b, i: (b, 0, 0)),
                  pl.BlockSpec(ovt.shape, lambda b, i: (0, 0))],
        out_specs=[pl.BlockSpec((None, T, BLK), lambda b, i: (b, i, 0)),
                   pl.BlockSpec((None, T // TB, LANES, TB), lambda b, i: (b, i, 0, 0))],
        out_shape=[jax.ShapeDtypeStruct((B, S, BLK), BF16),
                   jax.ShapeDtypeStruct((B, S // TB, LANES, TB), BF16)],
        compiler_params=pltpu.CompilerParams(dimension_semantics=("arbitrary", "arbitrary"),
                                             vmem_limit_bytes=VMEM_LIMIT),
        name="nsa_cmp_topk",
    )(proj3, kc2, vc2, ovt)


def _stick_kernel(q_all_ref, k_ref, vt_ref, u_ref, o_ref, za_ref, zb_ref, w_ref, carry_ref,
                  acc_ref):
    T = TB
    krow = lax.broadcasted_iota(jnp.int32, (T, T), 0)
    qcol = lax.broadcasted_iota(jnp.int32, (T, T), 1)
    strict = krow < qcol
    u = u_ref[...]
    z_refs = (za_ref, zb_ref)

    def query_block(i, carry):
        qm = _masked_head_rows(q_all_ref.at[i])

        def stage(j, slot):
            rows = pl.ds(pl.multiple_of(j * T, T), T)
            for h in range(N_HEADS):
                z_refs[slot][h] = _dot(k_ref[rows, (h // 2) * LANES:(h // 2 + 1) * LANES],
                                       qm[h])

        def apply_weights(j):
            vt_all = vt_ref[j]
            for h in range(N_HEADS):
                acc_ref[h] = acc_ref[h] + _dot(vt_all[h * HEAD_DIM:(h + 1) * HEAD_DIM, :], w_ref[h])

        def absorb(j, slot, diag, staged_next, applied_prev):
            z_ref = z_refs[slot]
            softplus = []
            for h in range(N_HEADS):
                z = z_ref[h]
                sp = jnp.maximum(z, jnp.log2(1.0 + jnp.exp2(jnp.minimum(z, 64.0))))
                if diag:
                    sp = jnp.where(strict, sp, 0.0)
                softplus.append(sp.astype(BF16))
            if staged_next is not None:
                stage(staged_next, 1 - slot)
            if applied_prev is not None:
                apply_weights(applied_prev)
            sums = [_dot(u, sp) for sp in softplus]
            for h in range(N_HEADS):
                a = jnp.exp2(z_ref[h] - (sums[h] + carry_ref[h]))
                if diag:
                    a = jnp.where(strict, a, 0.0)
                w_ref[h] = a.astype(BF16)
                carry_ref[h] = carry_ref[h] + sums[h][0:1, :]

        carry_ref[...] = jnp.zeros(carry_ref.shape, F32)
        acc_ref[...] = jnp.zeros(acc_ref.shape, F32)
        stage(i, 0)
        absorb(i, 0, True, jnp.maximum(i - 1, 0), None)

        def live():
            c = jnp.minimum(jnp.minimum(carry_ref[0], carry_ref[1]),
                            jnp.minimum(carry_ref[2], carry_ref[3]))
            return jnp.min(c) < STICK_DEAD

        def body(t, last):
            j = i - 1 - 2 * t
            go = live()

            @pl.when(go)
            def _():
                absorb(j, 1, False, j - 1, j + 1)

            go2 = jnp.logical_and(go, live())

            @pl.when(go2)
            def _():
                absorb(j - 1, 0, False, jnp.maximum(j - 2, 0), j)

            return jnp.where(go2, j - 1, jnp.where(go, j, last))

        last = lax.fori_loop(0, i // 2, body, i)
        tail = jnp.logical_and(i % 2 == 1, live())

        @pl.when(tail)
        def _():
            absorb(0, 1, False, None, 1)

        apply_weights(jnp.where(tail, 0, last))
        q_rows = pl.ds(pl.multiple_of(i * T, T), T)
        for pair in range(2):
            o_t = jnp.concatenate([acc_ref[2 * pair], acc_ref[2 * pair + 1]], axis=0)
            o_ref[q_rows, pair * LANES:(pair + 1) * LANES] = o_t.T.astype(BF16)
        return carry

    lax.fori_loop(0, q_all_ref.shape[0], query_block, 0)


def _stick_breaking(qdt, proj3, vdt, B, S):
    u = np.triu(np.ones((TB, TB), np.float32))
    return pl.pallas_call(
        _stick_kernel,
        grid=(B,),
        in_specs=[pl.BlockSpec((None,) + qdt.shape[1:], lambda b: (b, 0, 0, 0)),
                  pl.BlockSpec((None, S, BLK), lambda b: (b, 0, PB_DK)),
                  pl.BlockSpec((None,) + vdt.shape[1:], lambda b: (b, 0, 0, 0)),
                  pl.BlockSpec(u.shape, lambda b: (0, 0))],
        out_specs=pl.BlockSpec((None, S, BLK), lambda b: (b, 0, 0)),
        out_shape=jax.ShapeDtypeStruct((B, S, BLK), BF16),
        scratch_shapes=[pltpu.VMEM((N_HEADS, TB, TB), F32),
                        pltpu.VMEM((N_HEADS, TB, TB), F32),
                        pltpu.VMEM((N_HEADS, TB, TB), BF16),
                        pltpu.VMEM((N_HEADS, 1, TB), F32),
                        pltpu.VMEM((N_HEADS, HEAD_DIM, TB), F32)],
        compiler_params=pltpu.CompilerParams(dimension_semantics=("arbitrary",),
                                             vmem_limit_bytes=VMEM_LIMIT),
        name="stick_breaking",
    )(qdt, proj3, vdt, jnp.asarray(u, BF16))


def _outproj_kernel(x_ref, oa, ob, oc, os_, ow, od,
                    ga, gb, gl, gc, gd, expand_ref, w_ref, g_ref, out_ref):
    f = lambda r: r[...].astype(F32)
    gates = _dot(gl[...], expand_ref[...])
    o_c = (gates[:, 0:BLK] * f(oc) + gates[:, BLK:2 * BLK] * f(os_)
           + gates[:, 2 * BLK:3 * BLK] * f(ow))
    mixed = [f(oa) * f(ga), f(ob) * f(gb), o_c * f(gc), f(od) * f(gd)]
    y = _dot(mixed[0].astype(BF16), w_ref[0:BLK, :])
    for g in range(1, 4):
        y = y + _dot(mixed[g].astype(BF16), w_ref[g * BLK:(g + 1) * BLK, :])
    ms = jnp.mean(y * y, axis=-1, keepdims=True)
    out_ref[...] = x_ref[...] + (y * lax.rsqrt(ms + EPS)) * g_ref[...]


def _out_projection(x2, mixer_outs, proj, w_out, g_post):
    rows = x2.shape[0]
    tm = OUT_TILE
    blk = lambda c: pl.BlockSpec((tm, BLK), lambda i: (i, c))
    expand = np.zeros((LANES, N_BRANCH * BLK), np.float32)
    for h in range(N_HEADS):
        for r in range(N_BRANCH):
            expand[N_BRANCH * h + r, r * BLK + h * HEAD_DIM:r * BLK + (h + 1) * HEAD_DIM] = 1.0
    gate_specs = [blk(PB_AG), blk(PB_BG),
                  pl.BlockSpec((tm, LANES), lambda i: (i, 2 * PB_C1 + 1)), blk(PB_CG), blk(PB_DG)]
    return pl.pallas_call(
        _outproj_kernel,
        grid=(rows // tm,),
        in_specs=[pl.BlockSpec((tm, D_MODEL), lambda i: (i, 0))]
                 + [blk(0)] * len(mixer_outs) + gate_specs
                 + [pl.BlockSpec(expand.shape, lambda i: (0, 0)),
                    pl.BlockSpec((D_MODEL, D_MODEL), lambda i: (0, 0)),
                    pl.BlockSpec((1, D_MODEL), lambda i: (0, 0))],
        out_specs=pl.BlockSpec((tm, D_MODEL), lambda i: (i, 0)),
        out_shape=jax.ShapeDtypeStruct((rows, D_MODEL), F32),
        compiler_params=pltpu.CompilerParams(dimension_semantics=("arbitrary",),
                                             vmem_limit_bytes=VMEM_LIMIT),
        name="out_projection",
    )(x2, *mixer_outs, *([proj] * len(gate_specs)), jnp.asarray(expand, BF16),
      w_out.astype(BF16), g_post.reshape(1, D_MODEL))


def _layer(x2, tabs, B, S, layer, w_rows, w_t, w_out, g_pre, g_post, mla_g_q, mla_g_kv,
           mla_w_uq, mla_w_ukv, nsa_pos_k, nsa_pos_v, nsa_k_w1, nsa_k_w2, nsa_v_w1, nsa_v_w2):
    rows = B * S
    aqkv, aqkv4, aqkv16, proj, vct, vdt, qct, qdt = _in_projection(
        x2, g_pre, w_rows, w_t, layer, tabs, B, S)
    proj3 = proj.reshape(B, S, N_PBLK * BLK)

    o_a = _dilated(aqkv, aqkv4, aqkv16, B, S)

    qnt, qrt, kn, vbt = _mla_up(proj, mla_g_q, mla_g_kv, mla_w_uq, mla_w_ukv, tabs, B, S)
    o_b = _flash_t(qnt, kn.reshape(B, S, BLK), 0, BLK, vbt, B=B, S=S,
                   aux=("per_head", qrt, proj3, 2 * PB_BKV + 1), name="mla_attention")

    kc2, vc2 = _compress(proj3, nsa_pos_k, nsa_pos_v, nsa_k_w1, nsa_k_w2, nsa_v_w1, nsa_v_w2, B, S)
    o_cmp, nsel = _cmp_topk(proj3, kc2, vc2, B, S)
    penalty = np.zeros((S, LANES), np.float32)
    penalty[np.arange(S), np.arange(S) // NSA_SEL_LEN] = MASK_PENALTY
    o_slc = _flash_t(qct, proj3, 2 * PB_C0 + 1, LANES, vct, B=B, S=S, vrow0=0,
                     aux=("shared", nsel, jnp.asarray(penalty, BF16), 0), name="nsa_selected")
    o_win = _flash_t(qct, proj3, 2 * PB_C1, LANES, vct, B=B, S=S, vrow0=HEAD_DIM,
                     window=NSA_WINDOW - 1, name="nsa_window")

    o_d = _stick_breaking(qdt, proj3, vdt, B, S)

    flat = lambda t: t.reshape(rows, BLK)
    outs = [flat(o) for o in (o_a, o_b, o_cmp, o_slc, o_win, o_d)]
    return _out_projection(x2, outs, proj, w_out, g_post)


def kernel(x, positions, w_in, w_out, g_pre, g_post, mla_g_q, mla_g_kv, mla_w_uq, mla_w_ukv,
           nsa_pos_k, nsa_pos_v, nsa_k_w1, nsa_k_w2, nsa_v_w1, nsa_v_w2):
    B, S, D = x.shape
    assert D == D_MODEL and S % 1024 == 0
    x2 = x.reshape(B * S, D)
    tabs = _rope_tables(positions)
    w_rows, w_t = _pack_w_in(w_in)
    for l in range(w_in.shape[0]):
        x2 = _layer(x2, tabs, B, S, l, w_rows, w_t, w_out[l], g_pre[l], g_post[l],
                    mla_g_q[l], mla_g_kv[l], mla_w_uq[l], mla_w_ukv[l],
                    nsa_pos_k[l], nsa_pos_v[l], nsa_k_w1[l], nsa_k_w2[l],
                    nsa_v_w1[l], nsa_v_w2[l])
    return x2.reshape(B, S, D)
```
